```python
import jax, jax.numpy as jnp
from jax import lax
import numpy as np

D_MODEL = 2048
BATCH = 8
SEQ = 2048
DEPTH = 2

N_MIXERS = 2
MEM_LEN = 256
EPS = 1e-6

MIX_WIDTH = D_MODEL
TOK_WIDTH = 3 * D_MODEL // 4
XA_HEADS = 4
XA_HEAD_DIM = D_MODEL // 16
XA_WIDTH = XA_HEADS * XA_HEAD_DIM

POOL_WINDOWS = (2, 4, 8, 16)
N_POOL_GROUPS = 4
POOL_GROUP_WIDTH = TOK_WIDTH // N_POOL_GROUPS

HEAD_DIM = 64
N_Q_HEADS = TOK_WIDTH // HEAD_DIM
GQA_RATIO = 8
N_KV_HEADS = N_Q_HEADS // GQA_RATIO
KV_WIDTH = N_KV_HEADS * HEAD_DIM
ATTN_IN_WIDTH = TOK_WIDTH + 2 * KV_WIDTH + XA_WIDTH
WINDOW = 128
BLOCK = 128
ROPE_THETA = 500000.0
ROT_DIM = HEAD_DIM // 4
NEG_INF = -1e30

N_EXPERTS = 16
CAPACITY_FACTOR = 2
D_EXPERT = D_MODEL // 2

kernel_name = "interleaved_pool_swa_memxattn_ecmoe_encoder"


def rmsnorm(x, g):
    x32 = x.astype(jnp.float32)
    y = x32 * lax.rsqrt(jnp.mean(x32 * x32, axis=-1, keepdims=True) + EPS)
    return (y * g.astype(jnp.float32)).astype(x.dtype)


def partial_rotary(t, positions):
    rot, keep = t[..., :ROT_DIM], t[..., ROT_DIM:]
    inv_freq = ROPE_THETA ** (-jnp.arange(0, ROT_DIM, 2, dtype=jnp.float32) / ROT_DIM)
    ang = positions.astype(jnp.float32)[..., None] * inv_freq
    cos = jnp.cos(ang)[:, :, None, :]
    sin = jnp.sin(ang)[:, :, None, :]
    r = rot.astype(jnp.float32)
    r1, r2 = r[..., : ROT_DIM // 2], r[..., ROT_DIM // 2:]
    out = jnp.concatenate([r1 * cos - r2 * sin, r2 * cos + r1 * sin], axis=-1)
    return jnp.concatenate([out.astype(t.dtype), keep], axis=-1)


def multiscale_pool(u, w_group, scale):
    B, S, _ = u.shape
    u32 = u.astype(jnp.float32)
    cs = jnp.concatenate([jnp.zeros((B, 1, TOK_WIDTH), jnp.float32), jnp.cumsum(u32, axis=1)], axis=1)
    t = np.arange(S)
    outs = []
    for g, w in enumerate(POOL_WINDOWS):
        lo = np.maximum(t - w // 2, 0)
        hi = np.minimum(t + w // 2 - 1, S - 1)
        cnt = jnp.asarray((hi - lo + 1).astype(np.float32))[None, :, None]
        sl = slice(g * POOL_GROUP_WIDTH, (g + 1) * POOL_GROUP_WIDTH)
        csg = cs[:, :, sl]
        mean = (csg[:, hi + 1] - csg[:, lo]) / cnt
        outs.append(mean - u32[:, :, sl])
    pooled = jnp.stack(outs, axis=2).astype(u.dtype)
    mixed = jnp.einsum('bsgc,gcd->bsgd', pooled, w_group)
    return mixed.reshape(B, S, TOK_WIDTH) * scale


def window_gqa_sink(q, k, v, sink):
    B, S = q.shape[0], q.shape[1]
    nb = S // BLOCK
    qb = q.reshape(B, nb, BLOCK, N_KV_HEADS, GQA_RATIO, HEAD_DIM)
    pad = ((0, 0), (BLOCK, BLOCK), (0, 0), (0, 0))
    kp = jnp.pad(k, pad).reshape(B, nb + 2, BLOCK, N_KV_HEADS, HEAD_DIM)
    vp = jnp.pad(v, pad).reshape(B, nb + 2, BLOCK, N_KV_HEADS, HEAD_DIM)
    kb = jnp.concatenate([kp[:, :-2], kp[:, 1:-1], kp[:, 2:]], axis=2)
    vb = jnp.concatenate([vp[:, :-2], vp[:, 1:-1], vp[:, 2:]], axis=2)
    blk = np.arange(nb)[:, None, None] * BLOCK
    qpos = blk + np.arange(BLOCK)[None, :, None]
    kpos = blk - BLOCK + np.arange(3 * BLOCK)[None, None, :]
    valid = jnp.asarray((np.abs(kpos - qpos) <= WINDOW) & (kpos >= 0) & (kpos < S))
    scores = jnp.einsum('bnqhgd,bnkhd->bnhgqk', qb, kb).astype(jnp.float32) * (HEAD_DIM ** -0.5)
    scores = jnp.where(valid[None, :, None, None], scores, NEG_INF)
    s = sink.astype(jnp.float32).reshape(N_KV_HEADS, GQA_RATIO)[None, None, :, :, None, None]
    m = jnp.maximum(jnp.max(scores, axis=-1, keepdims=True), s)
    p = jnp.exp(scores - m)
    probs = p / (jnp.sum(p, axis=-1, keepdims=True) + jnp.exp(s - m))
    out = jnp.einsum('bnhgqk,bnkhd->bnqhgd', probs.astype(v.dtype), vb)
    return out.reshape(B, S, N_Q_HEADS * HEAD_DIM)


def memory_cross_attention(qx, mem_kv):
    B, S, _ = qx.shape
    q = qx.reshape(B, S, XA_HEADS, XA_HEAD_DIM)
    k = mem_kv[..., :XA_WIDTH].reshape(B, -1, XA_HEADS, XA_HEAD_DIM)
    v = mem_kv[..., XA_WIDTH:].reshape(B, -1, XA_HEADS, XA_HEAD_DIM)
    scores = jnp.einsum('bshd,bmhd->bhsm', q, k).astype(jnp.float32) * (XA_HEAD_DIM ** -0.5)
    probs = jax.nn.softmax(scores, axis=-1).astype(v.dtype)
    return jnp.einsum('bhsm,bmhd->bshd', probs, v).reshape(B, S, XA_WIDTH)


def expert_choice_ffn(h, w_router, w_gate, w_up, w_down):
    B, S, _ = h.shape
    cap = CAPACITY_FACTOR * S // N_EXPERTS
    aff = jax.nn.softmax(jnp.einsum('bsd,de->bse', h, w_router).astype(jnp.float32), axis=-1)
    gates, idx = lax.top_k(jnp.swapaxes(aff, 1, 2), cap)
    bidx = jnp.arange(B)[:, None, None]
    xg = h[bidx, idx]
    a = jnp.einsum('becd,edf->becf', xg, w_gate)
    u = jnp.einsum('becd,edf->becf', xg, w_up)
    y = jnp.einsum('becf,efd->becd', jax.nn.silu(a) * u, w_down) * gates[..., None].astype(h.dtype)
    return jnp.zeros_like(h).at[bidx, idx].add(y)


def setup_inputs(seed: int = 0) -> dict:
    key = jax.random.key(seed)
    ks = jax.random.split(key, 24)
    n_pool = (DEPTH + 1) // 2
    n_attn = DEPTH // 2
    f32 = jnp.float32

    def w(k, shape, fan_in):
        return jax.random.normal(k, shape, f32) * (fan_in ** -0.5)

    def gain(k, shape):
        return 1.0 + 0.05 * jax.random.normal(k, shape, f32)

    x = jax.random.normal(ks[0], (BATCH, SEQ, D_MODEL), f32)
    mem = jax.random.normal(ks[1], (BATCH, MEM_LEN, D_MODEL), f32)
    offset = jax.random.randint(ks[2], (BATCH, 1), 0, 4096, dtype=jnp.int32)
    positions = offset + jnp.arange(SEQ, dtype=jnp.int32)[None, :]
    return {
        "x": x,
        "mem": mem,
        "positions": positions,
        "norm_mix_g": gain(ks[3], (DEPTH, D_MODEL)),
        "norm_ffn_g": gain(ks[4], (DEPTH, D_MODEL)),
        "mem_norm_g": gain(ks[5], (D_MODEL,)),
        "final_g": gain(ks[6], (D_MODEL,)),
        "mem_w_kv": w(ks[7], (DEPTH, D_MODEL, 2 * XA_WIDTH), D_MODEL),
        "pool_w_in": w(ks[8], (n_pool, D_MODEL, MIX_WIDTH), D_MODEL),
        "pool_group_w": w(ks[9], (n_pool, N_POOL_GROUPS, POOL_GROUP_WIDTH, POOL_GROUP_WIDTH), POOL_GROUP_WIDTH),
        "pool_scale": gain(ks[10], (n_pool, TOK_WIDTH)),
        "pool_w_out": w(ks[11], (n_pool, MIX_WIDTH, D_MODEL), MIX_WIDTH),
        "attn_w_in": w(ks[12], (n_attn, D_MODEL, ATTN_IN_WIDTH), D_MODEL),
        "attn_sink": 0.5 * jax.random.normal(ks[13], (n_attn, N_Q_HEADS), f32),
        "attn_w_out": w(ks[14], (n_attn, MIX_WIDTH, D_MODEL), MIX_WIDTH),
        "router_w": w(ks[15], (DEPTH, D_MODEL, N_EXPERTS), D_MODEL),
        "exp_w_gate": w(ks[16], (DEPTH, N_EXPERTS, D_MODEL, D_EXPERT), D_MODEL),
        "exp_w_up": w(ks[17], (DEPTH, N_EXPERTS, D_MODEL, D_EXPERT), D_MODEL),
        "exp_w_down": w(ks[18], (DEPTH, N_EXPERTS, D_EXPERT, D_MODEL), D_EXPERT),
    }


def reference(x, mem, positions, norm_mix_g, norm_ffn_g, mem_norm_g, final_g, mem_w_kv,
              pool_w_in, pool_group_w, pool_scale, pool_w_out,
              attn_w_in, attn_sink, attn_w_out,
              router_w, exp_w_gate, exp_w_up, exp_w_down):
    B, S, _ = x.shape
    mem_n = rmsnorm(mem, mem_norm_g)
    for layer in range(DEPTH):
        h = rmsnorm(x, norm_mix_g[layer])
        mem_kv = jnp.einsum('bmd,de->bme', mem_n, mem_w_kv[layer])
        j = layer // N_MIXERS
        if layer % N_MIXERS == 0:
            proj = jnp.einsum('bsd,de->bse', h, pool_w_in[j])
            tok = multiscale_pool(proj[..., :TOK_WIDTH], pool_group_w[j], pool_scale[j])
            mem_out = memory_cross_attention(proj[..., TOK_WIDTH:], mem_kv)
            w_out = pool_w_out[j]
        else:
            proj = jnp.einsum('bsd,de->bse', h, attn_w_in[j])
            o1 = TOK_WIDTH
            o2 = o1 + KV_WIDTH
            o3 = o2 + KV_WIDTH
            q = partial_rotary(proj[..., :o1].reshape(B, S, N_Q_HEADS, HEAD_DIM), positions)
            k = partial_rotary(proj[..., o1:o2].reshape(B, S, N_KV_HEADS, HEAD_DIM), positions)
            v = proj[..., o2:o3].reshape(B, S, N_KV_HEADS, HEAD_DIM)
            tok = window_gqa_sink(q, k, v, attn_sink[j])
            mem_out = memory_cross_attention(proj[..., o3:], mem_kv)
            w_out = attn_w_out[j]
        x = x + jnp.einsum('bse,ed->bsd', jnp.concatenate([tok, mem_out], axis=-1), w_out)
        h = rmsnorm(x, norm_ffn_g[layer])
        x = x + expert_choice_ffn(h, router_w[layer], exp_w_gate[layer], exp_w_up[layer], exp_w_down[layer])
    return rmsnorm(x, final_g)
```

```python
import functools

import jax
import jax.numpy as jnp
import numpy as np
from jax import lax
from jax.experimental import pallas as pl
from jax.experimental.pallas import tpu as pltpu

F32 = jnp.float32
BF16 = jnp.bfloat16
I32 = jnp.int32

EPS = 1e-6
MEM_LEN = 256
XA_HEADS = 4
XA_HEAD_DIM = 128
XA_WIDTH = XA_HEADS * XA_HEAD_DIM
POOL_WINDOWS = (2, 4, 8, 16)
HEAD_DIM = 64
GQA_RATIO = 8
WINDOW = 128
BLOCK = 128
ROPE_THETA = 500000.0
ROT_DIM = 16
NEG_INF = -1e30
CAPACITY_FACTOR = 2

LANES = 128
MIB = 1024 * 1024
VMEM_LIMIT_BYTES = 56 * MIB

ROW_TILE = 512
POOL_PAD = 16
PREFIX_CHUNK = 256


def _params(n_grid_dims):
    return pltpu.CompilerParams(
        dimension_semantics=("arbitrary",) * n_grid_dims,
        vmem_limit_bytes=VMEM_LIMIT_BYTES,
    )


def _resident(block_shape, index_map):
    return pl.BlockSpec(block_shape, index_map, pipeline_mode=pl.Buffered(1))


def _rmsnorm_rows(x, g):
    return x * lax.rsqrt(jnp.mean(x * x, axis=-1, keepdims=True) + EPS) * g


def _dot(a, b):
    return jnp.dot(a, b, preferred_element_type=F32)


def _dot_nt(a, b):
    return lax.dot_general(a, b, (((1,), (1,)), ((), ())), preferred_element_type=F32)


def _memkv_kernel(mem_ref, g_ref, w_ref, o_ref):
    hn = _rmsnorm_rows(mem_ref[...], g_ref[...]).astype(BF16)
    o_ref[0] = _dot(hn, w_ref[0]).astype(BF16)


def _memkv(mem2d, g, w_bf16):
    depth, d_model, n = w_bf16.shape
    rows = mem2d.shape[0]
    return pl.pallas_call(
        _memkv_kernel,
        out_shape=jax.ShapeDtypeStruct((depth, rows, n), BF16),
        grid=(depth, rows // ROW_TILE),
        in_specs=[
            pl.BlockSpec((ROW_TILE, d_model), lambda l, i: (i, 0)),
            pl.BlockSpec((1, d_model), lambda l, i: (0, 0)),
            pl.BlockSpec((1, d_model, n), lambda l, i: (l, 0, 0)),
        ],
        out_specs=pl.BlockSpec((1, ROW_TILE, n), lambda l, i: (l, i, 0)),
        compiler_params=_params(2),
        name="memkv",
    )(mem2d, g, w_bf16)


def _inproj_pool_kernel(x_ref, g_ref, w_ref, u_ref, qm_ref, *, tok_width):
    hn = _rmsnorm_rows(x_ref[...], g_ref[...]).astype(BF16)
    chunk = 512
    for c in range(0, tok_width, chunk):
        u_ref[:, c:c + chunk] = _dot(hn, w_ref[:, c:c + chunk])
    qm_ref[...] = _dot(hn, w_ref[:, tok_width:]).astype(BF16)


def _inproj_pool(x2d, g, w_bf16, tok_width):
    t, d_model = x2d.shape
    n = w_bf16.shape[1]
    return pl.pallas_call(
        functools.partial(_inproj_pool_kernel, tok_width=tok_width),
        out_shape=(jax.ShapeDtypeStruct((t, tok_width), F32),
                   jax.ShapeDtypeStruct((t, n - tok_width), BF16)),
        grid=(t // ROW_TILE,),
        in_specs=[
            pl.BlockSpec((ROW_TILE, d_model), lambda i: (i, 0)),
            pl.BlockSpec((1, d_model), lambda i: (0, 0)),
            _resident((d_model, n), lambda i: (0, 0)),
        ],
        out_specs=(pl.BlockSpec((ROW_TILE, tok_width), lambda i: (i, 0)),
                   pl.BlockSpec((ROW_TILE, n - tok_width), lambda i: (i, 0))),
        compiler_params=_params(1),
        name="inproj_pool",
    )(x2d, g, w_bf16)


def _pool_group(u_ref, gw_ref, sc_ref, o_ref, a_ref, b_ref, *, window, seq):
    gwid = u_ref.shape[2]
    rows = seq + 2 * POOL_PAD
    zeros_pad = jnp.zeros((POOL_PAD, gwid), F32)
    a_ref[0:POOL_PAD, :] = zeros_pad
    b_ref[0:POOL_PAD, :] = zeros_pad
    a_ref[POOL_PAD:POOL_PAD + seq, :] = u_ref[0]
    a_ref[POOL_PAD + seq:rows, :] = zeros_pad
    src, dst = a_ref, b_ref
    shift = 1
    while shift < window:
        dst[POOL_PAD:rows, :] = src[POOL_PAD - shift:rows - shift, :] + src[POOL_PAD:rows, :]
        src, dst = dst, src
        shift *= 2
    off = POOL_PAD + window // 2 - 1
    win = src[off:off + seq, :]
    t = lax.broadcasted_iota(I32, (seq, 1), 0)
    lo = jnp.maximum(t - window // 2, 0)
    hi = jnp.minimum(t + window // 2 - 1, seq - 1)
    cnt = (hi - lo + 1).astype(F32)
    pooled = (win / cnt - u_ref[0]).astype(BF16)
    o_ref[0] = (_dot(pooled, gw_ref[0]) * sc_ref[0]).astype(BF16)


def _pool_kernel(u_ref, gw_ref, sc_ref, o_ref, a_ref, b_ref, *, seq):
    g = pl.program_id(1)
    for k, window in enumerate(POOL_WINDOWS):
        @pl.when(g == k)
        def _():
            _pool_group(u_ref, gw_ref, sc_ref, o_ref, a_ref, b_ref, window=window, seq=seq)


def _pool_mixer(u3d, gw_bf16, scale3d):
    b, seq, tok_width = u3d.shape
    n_groups, gwid, _ = gw_bf16.shape
    assert n_groups == len(POOL_WINDOWS) and n_groups * gwid == tok_width
    return pl.pallas_call(
        functools.partial(_pool_kernel, seq=seq),
        out_shape=jax.ShapeDtypeStruct((b, seq, tok_width), BF16),
        grid=(b, n_groups),
        in_specs=[
            pl.BlockSpec((1, seq, gwid), lambda i, g: (i, 0, g)),
            pl.BlockSpec((1, gwid, gwid), lambda i, g: (g, 0, 0)),
            pl.BlockSpec((1, 1, gwid), lambda i, g: (g, 0, 0)),
        ],
        out_specs=pl.BlockSpec((1, seq, gwid), lambda i, g: (i, 0, g)),
        scratch_shapes=[pltpu.VMEM((seq + 2 * POOL_PAD, gwid), F32),
                        pltpu.VMEM((seq + 2 * POOL_PAD, gwid), F32)],
        compiler_params=_params(2),
        name="pool_mixer",
    )(u3d, gw_bf16, scale3d)


def _xattn_kernel(q_ref, kv_ref, o_ref):
    scale = XA_HEAD_DIM ** -0.5
    for h in range(XA_HEADS):
        lo = h * XA_HEAD_DIM
        q = q_ref[:, lo:lo + XA_HEAD_DIM]
        k = kv_ref[0, :, lo:lo + XA_HEAD_DIM]
        v = kv_ref[0, :, XA_WIDTH + lo:XA_WIDTH + lo + XA_HEAD_DIM]
        s = _dot_nt(q, k) * scale
        m = jnp.max(s, axis=-1, keepdims=True)
        p = jnp.exp(s - m)
        den = jnp.sum(p, axis=-1, keepdims=True)
        o_ref[:, lo:lo + XA_HEAD_DIM] = (_dot(p.astype(BF16), v) / den).astype(BF16)


def _mem_xattn(qm2d, memkv_layer, seq):
    t = qm2d.shape[0]
    tiles_per_seq = seq // ROW_TILE
    return pl.pallas_call(
        _xattn_kernel,
        out_shape=jax.ShapeDtypeStruct((t, XA_WIDTH), BF16),
        grid=(t // ROW_TILE,),
        in_specs=[
            pl.BlockSpec((ROW_TILE, XA_WIDTH), lambda i: (i, 0)),
            pl.BlockSpec((1, MEM_LEN, 2 * XA_WIDTH), lambda i: (i // tiles_per_seq, 0, 0)),
        ],
        out_specs=pl.BlockSpec((ROW_TILE, XA_WIDTH), lambda i: (i, 0)),
        compiler_params=_params(1),
        name="mem_xattn",
    )(qm2d, memkv_layer)


def _outproj_kernel(tok_ref, mo_ref, x_ref, w_ref, g_ref, rw_ref,
                    x1_ref, h_ref, afft_ref, aff_ref, *, tok_width, n_experts):
    y = _dot(tok_ref[...], w_ref[0:tok_width, :]) + _dot(mo_ref[...], w_ref[tok_width:, :])
    x1 = x_ref[...] + y
    x1_ref[...] = x1
    hn = _rmsnorm_rows(x1, g_ref[...])
    h_hi = hn.astype(BF16)
    h_ref[...] = h_hi
    h_lo = (hn - h_hi.astype(F32)).astype(BF16)
    rw = rw_ref[...]
    w_hi = rw.astype(BF16)
    w_lo = (rw - w_hi.astype(F32)).astype(BF16)
    logits = _dot(h_hi, w_hi) + _dot(h_lo, w_hi) + _dot(h_hi, w_lo)
    lt = logits.T[0:n_experts, :]
    m = jnp.max(lt, axis=0, keepdims=True)
    ex = jnp.exp(lt - m)
    afft = ex / jnp.sum(ex, axis=0, keepdims=True)
    afft_ref[0] = afft
    padded = jnp.concatenate(
        [afft, jnp.zeros((LANES - n_experts, afft.shape[1]), F32)], axis=0)
    aff_ref[...] = padded.T


def _outproj(tok2d, mo2d, x2d, w_bf16, g, rw_pad, n_experts, seq):
    t, d_model = x2d.shape
    tok_width = tok2d.shape[1]
    tiles_per_seq = seq // ROW_TILE
    return pl.pallas_call(
        functools.partial(_outproj_kernel, tok_width=tok_width, n_experts=n_experts),
        out_shape=(jax.ShapeDtypeStruct((t, d_model), F32),
                   jax.ShapeDtypeStruct((t, d_model), BF16),
                   jax.ShapeDtypeStruct((t // seq, n_experts, seq), F32),
                   jax.ShapeDtypeStruct((t, LANES), F32)),
        grid=(t // ROW_TILE,),
        in_specs=[
            pl.BlockSpec((ROW_TILE, tok_width), lambda i: (i, 0)),
            pl.BlockSpec((ROW_TILE, mo2d.shape[1]), lambda i: (i, 0)),
            pl.BlockSpec((ROW_TILE, d_model), lambda i: (i, 0)),
            _resident((d_model, d_model), lambda i: (0, 0)),
            pl.BlockSpec((1, d_model), lambda i: (0, 0)),
            _resident((d_model, LANES), lambda i: (0, 0)),
        ],
        out_specs=(pl.BlockSpec((ROW_TILE, d_model), lambda i: (i, 0)),
                   pl.BlockSpec((ROW_TILE, d_model), lambda i: (i, 0)),
                   pl.BlockSpec((1, n_experts, ROW_TILE),
                                lambda i: (i // tiles_per_seq, 0, i % tiles_per_seq)),
                   pl.BlockSpec((ROW_TILE, LANES), lambda i: (i, 0))),
        compiler_params=_params(1),
        name="outproj_router",
    )(tok2d, mo2d, x2d, w_bf16, g, rw_pad)


def _strict_triangle(n, lower):
    r = lax.broadcasted_iota(I32, (n, n), 0)
    c = lax.broadcasted_iota(I32, (n, n), 1)
    return jnp.where((c < r) if lower else (r < c), 1.0, 0.0).astype(BF16)


def _prefix_rows(mask_f32):
    s, l = mask_f32.shape
    tri = _strict_triangle(PREFIX_CHUNK, lower=True)
    carry = jnp.zeros((1, l), F32)
    out = []
    for c in range(0, s, PREFIX_CHUNK):
        m = mask_f32[c:c + PREFIX_CHUNK, :]
        out.append(_dot(tri, m.astype(BF16)) + carry)
        carry = carry + jnp.sum(m, axis=0, keepdims=True)
    return jnp.concatenate(out, axis=0)


def _prefix_lanes(mask_f32):
    e, s = mask_f32.shape
    tri = _strict_triangle(PREFIX_CHUNK, lower=False)
    carry = jnp.zeros((e, 1), F32)
    out = []
    for c in range(0, s, PREFIX_CHUNK):
        m = mask_f32[:, c:c + PREFIX_CHUNK]
        out.append(_dot(m.astype(BF16), tri) + carry)
        carry = carry + jnp.sum(m, axis=1, keepdims=True)
    return jnp.concatenate(out, axis=1)


def _select_slots(key, thr, need, prefix_fn):
    gt = jnp.where(key > thr, 1.0, 0.0)
    eq = jnp.where(key == thr, 1.0, 0.0)
    eq_rank = prefix_fn(eq)
    sel = gt + eq * jnp.where(eq_rank < need, 1.0, 0.0)
    pos = prefix_fn(sel)
    return jnp.where(sel > 0.5, pos, -1.0)


def _gather_kernel(afft_ref, aff_ref, h_ref, xg_ref, gs_ref, thr_ref, need_ref,
                   posm_ref, a3_ref, *, cap):
    e = pl.program_id(1)
    n_experts = afft_ref.shape[1]

    @pl.when(e == 0)
    def _():
        key = pltpu.bitcast(afft_ref[0], I32)
        thr = jnp.zeros((n_experts, 1), I32)
        for bit in range(30, -1, -1):
            cand = thr | (1 << bit)
            cnt = jnp.sum(jnp.where(key >= cand, 1.0, 0.0), axis=1, keepdims=True)
            thr = jnp.where(cnt >= cap, cand, thr)
        n_gt = jnp.sum(jnp.where(key > thr, 1.0, 0.0), axis=1, keepdims=True)
        need = cap - n_gt
        thr_ref[0] = thr
        need_ref[0] = need
        posm_ref[...] = _select_slots(key, thr, need, _prefix_lanes)
        a = aff_ref[...]
        a_hi = a.astype(BF16)
        r1 = a - a_hi.astype(F32)
        a_mid = r1.astype(BF16)
        a_lo = (r1 - a_mid.astype(F32)).astype(BF16)
        a3_ref[0] = a_hi
        a3_ref[1] = a_mid
        a3_ref[2] = a_lo

    prow = posm_ref[pl.ds(e, 1), :]
    slot = lax.broadcasted_iota(I32, (cap, 1), 0).astype(F32)
    onehot = jnp.where(prow == slot, 1.0, 0.0).astype(BF16)
    xg_ref[0, 0] = _dot(onehot, h_ref[...]).astype(BF16)
    gates = _dot(onehot, a3_ref[0]) + _dot(onehot, a3_ref[1]) + _dot(onehot, a3_ref[2])
    lane = lax.broadcasted_iota(I32, (1, LANES), 1)
    gcol = jnp.sum(jnp.where(lane == e, gates, 0.0), axis=1, keepdims=True)
    gs_ref[0, 0] = jnp.broadcast_to(gcol, (cap, LANES))


def _expert_gather(afft, aff2d, h2d, cap):
    b, n_experts, seq = afft.shape
    d_model = h2d.shape[1]
    return pl.pallas_call(
        functools.partial(_gather_kernel, cap=cap),
        out_shape=(jax.ShapeDtypeStruct((n_experts, b, cap, d_model), BF16),
                   jax.ShapeDtypeStruct((n_experts, b, cap, LANES), F32),
                   jax.ShapeDtypeStruct((b, n_experts, 1), I32),
                   jax.ShapeDtypeStruct((b, n_experts, 1), F32)),
        grid=(b, n_experts),
        in_specs=[
            pl.BlockSpec((1, n_experts, seq), lambda i, e: (i, 0, 0)),
            pl.BlockSpec((seq, LANES), lambda i, e: (i, 0)),
            pl.BlockSpec((seq, d_model), lambda i, e: (i, 0)),
        ],
        out_specs=(pl.BlockSpec((1, 1, cap, d_model), lambda i, e: (e, i, 0, 0)),
                   pl.BlockSpec((1, 1, cap, LANES), lambda i, e: (e, i, 0, 0)),
                   pl.BlockSpec((1, n_experts, 1), lambda i, e: (i, 0, 0)),
                   pl.BlockSpec((1, n_experts, 1), lambda i, e: (i, 0, 0))),
        scratch_shapes=[pltpu.VMEM((n_experts, seq), F32),
                        pltpu.VMEM((3, seq, LANES), BF16)],
        compiler_params=_params(2),
        name="expert_gather",
    )(afft, aff2d, h2d)


EXPERT_ROWS = 1024
EXPERT_FTILE = 256


def _expert_kernel(xg_ref, gs_ref, wg_ref, wu_ref, wd_ref, o_ref, acc_ref):
    f = pl.program_id(2)
    nb, cap, d_model = xg_ref.shape[1], xg_ref.shape[2], xg_ref.shape[3]
    rows = nb * cap

    @pl.when(f == 0)
    def _():
        acc_ref[...] = jnp.zeros_like(acc_ref)

    x = xg_ref[0].reshape(rows, d_model)
    a = _dot(x, wg_ref[0].astype(BF16))
    u = _dot(x, wu_ref[0].astype(BF16))
    hact = (a * jax.nn.sigmoid(a) * u).astype(BF16)
    wd = wd_ref[0].astype(BF16)
    chunk = 512
    for c in range(0, d_model, chunk):
        acc_ref[:, c:c + chunk] += _dot(hact, wd[:, c:c + chunk])

    @pl.when(f == pl.num_programs(2) - 1)
    def _():
        g = gs_ref[0].reshape(rows, LANES)
        for c in range(0, d_model, LANES):
            y = acc_ref[:, c:c + LANES] * g
            o_ref[:, 0, :, c:c + LANES] = y.astype(BF16).reshape(nb, cap, LANES)


def _experts(xg, gs, w_gate, w_up, w_down):
    n_experts, b, cap, d_model = xg.shape
    d_expert = w_gate.shape[2]
    nb = EXPERT_ROWS // cap
    return pl.pallas_call(
        _expert_kernel,
        out_shape=jax.ShapeDtypeStruct((b, n_experts, cap, d_model), BF16),
        grid=(n_experts, b // nb, d_expert // EXPERT_FTILE),
        in_specs=[
            pl.BlockSpec((1, nb, cap, d_model), lambda e, m, f: (e, m, 0, 0)),
            pl.BlockSpec((1, nb, cap, LANES), lambda e, m, f: (e, m, 0, 0)),
            pl.BlockSpec((1, d_model, EXPERT_FTILE), lambda e, m, f: (e, 0, f)),
            pl.BlockSpec((1, d_model, EXPERT_FTILE), lambda e, m, f: (e, 0, f)),
            pl.BlockSpec((1, EXPERT_FTILE, d_model), lambda e, m, f: (e, f, 0)),
        ],
        out_specs=pl.BlockSpec((nb, 1, cap, d_model), lambda e, m, f: (m, e, 0, 0)),
        scratch_shapes=[pltpu.VMEM((EXPERT_ROWS, d_model), F32)],
        compiler_params=_params(3),
        name="experts",
    )(xg, gs, w_gate, w_up, w_down)


COMBINE_NTILE = 512
COMBINE_MCHUNK = 512


def _combine_kernel(aff_ref, thr_ref, need_ref, x1_ref, y_ref, o_ref, p_ref, *, cap, n_experts):
    n = pl.program_id(1)
    seq = aff_ref.shape[0]

    @pl.when(n == 0)
    def _():
        key = pltpu.bitcast(aff_ref[...], I32)
        posm = _select_slots(key, thr_ref[0], need_ref[0], _prefix_rows)
        slot = lax.broadcasted_iota(I32, (1, cap), 1).astype(F32)
        for e in range(n_experts):
            col = posm[:, e:e + 1]
            p_ref[:, e * cap:(e + 1) * cap] = jnp.where(col == slot, 1.0, 0.0).astype(BF16)

    for r in range(0, seq, COMBINE_MCHUNK):
        o_ref[0, r:r + COMBINE_MCHUNK, :] = (
            x1_ref[0, r:r + COMBINE_MCHUNK, :]
            + _dot(p_ref[r:r + COMBINE_MCHUNK, :], y_ref[0]))


def _combine(aff2d, thr_row, need_row, x1_3d, yg3d, cap, n_experts):
    b, seq, d_model = x1_3d.shape
    return pl.pallas_call(
        functools.partial(_combine_kernel, cap=cap, n_experts=n_experts),
        out_shape=jax.ShapeDtypeStruct((b, seq, d_model), F32),
        grid=(b, d_model // COMBINE_NTILE),
        in_specs=[
            pl.BlockSpec((seq, LANES), lambda i, n: (i, 0)),
            pl.BlockSpec((1, 1, LANES), lambda i, n: (i, 0, 0)),
            pl.BlockSpec((1, 1, LANES), lambda i, n: (i, 0, 0)),
            pl.BlockSpec((1, seq, COMBINE_NTILE), lambda i, n: (i, 0, n)),
            pl.BlockSpec((1, n_experts * cap, COMBINE_NTILE), lambda i, n: (i, 0, n)),
        ],
        out_specs=pl.BlockSpec((1, seq, COMBINE_NTILE), lambda i, n: (i, 0, n)),
        scratch_shapes=[pltpu.VMEM((seq, n_experts * cap), BF16)],
        compiler_params=_params(2),
        name="combine",
    )(aff2d, thr_row, need_row, x1_3d, yg3d)


def _moe(afft, aff2d, h2d, x1_2d, w_gate, w_up, w_down, b, seq):
    n_experts = afft.shape[1]
    d_model = h2d.shape[1]
    cap = CAPACITY_FACTOR * seq // n_experts
    xg, gs, thr, need = _expert_gather(afft, aff2d, h2d, cap)
    yg = _experts(xg, gs, w_gate, w_up, w_down)
    pad = LANES - n_experts
    thr_row = jnp.pad(thr.reshape(b, 1, n_experts), ((0, 0), (0, 0), (0, pad)),
                      constant_values=np.iinfo(np.int32).max)
    need_row = jnp.pad(need.reshape(b, 1, n_experts), ((0, 0), (0, 0), (0, pad)))
    return _combine(aff2d, thr_row, need_row, x1_2d.reshape(b, seq, d_model),
                    yg.reshape(b, n_experts * cap, d_model), cap, n_experts)


def _rotary_tile(t, cos, sin_lo, sin_hi):
    half = ROT_DIM // 2
    return t * cos + pltpu.roll(t, LANES - half, 1) * sin_lo + pltpu.roll(t, half, 1) * sin_hi


def _inproj_attn_kernel(x_ref, g_ref, w_ref, pos_ref, rot_ref,
                        q_ref, k_ref, v_ref, qm_ref, *, tok_width, kv_width):
    hn = _rmsnorm_rows(x_ref[...], g_ref[...]).astype(BF16)
    ang = pos_ref[...].astype(F32) * rot_ref[0:1, :]
    cos = jnp.cos(ang)
    sin = jnp.sin(ang)
    sin_lo = sin * rot_ref[1:2, :]
    sin_hi = sin * rot_ref[2:3, :]
    qscale = HEAD_DIM ** -0.5
    chunk = 512
    for c in range(0, tok_width, chunk):
        pc = _dot(hn, w_ref[:, c:c + chunk])
        for j in range(0, chunk, LANES):
            rot = _rotary_tile(pc[:, j:j + LANES], cos, sin_lo, sin_hi)
            q_ref[:, c + j:c + j + LANES] = (rot * qscale).astype(BF16)
    kv = _dot(hn, w_ref[:, tok_width:tok_width + 2 * kv_width])
    k01 = _rotary_tile(kv[:, 0:LANES], cos, sin_lo, sin_hi)
    k2x = _rotary_tile(kv[:, LANES:2 * LANES], cos, sin_lo, sin_hi)
    k_ref[0, 0] = k01[:, 0:HEAD_DIM].astype(BF16)
    k_ref[0, 1] = k01[:, HEAD_DIM:LANES].astype(BF16)
    k_ref[0, 2] = k2x[:, 0:HEAD_DIM].astype(BF16)
    for hh in range(kv_width // HEAD_DIM):
        lo = kv_width + hh * HEAD_DIM
        v_ref[0, hh] = kv[:, lo:lo + HEAD_DIM].astype(BF16)
    qm_ref[...] = _dot(hn, w_ref[:, tok_width + 2 * kv_width:]).astype(BF16)


def _inproj_attn(x2d, g, w_bf16, pos2d, rot_rows, tok_width, kv_width, seq):
    t, d_model = x2d.shape
    n = w_bf16.shape[1]
    n_kv = kv_width // HEAD_DIM
    assert n_kv == 3 and kv_width + HEAD_DIM == 2 * LANES
    tiles_per_seq = seq // ROW_TILE
    kv_spec = pl.BlockSpec((1, n_kv, ROW_TILE, HEAD_DIM),
                           lambda i: (i // tiles_per_seq, 0, i % tiles_per_seq, 0))
    return pl.pallas_call(
        functools.partial(_inproj_attn_kernel, tok_width=tok_width, kv_width=kv_width),
        out_shape=(jax.ShapeDtypeStruct((t, tok_width), BF16),
                   jax.ShapeDtypeStruct((t // seq, n_kv, seq, HEAD_DIM), BF16),
                   jax.ShapeDtypeStruct((t // seq, n_kv, seq, HEAD_DIM), BF16),
                   jax.ShapeDtypeStruct((t, n - tok_width - 2 * kv_width), BF16)),
        grid=(t // ROW_TILE,),
        in_specs=[
            pl.BlockSpec((ROW_TILE, d_model), lambda i: (i, 0)),
            pl.BlockSpec((1, d_model), lambda i: (0, 0)),
            _resident((d_model, n), lambda i: (0, 0)),
            pl.BlockSpec((ROW_TILE, 1), lambda i: (i, 0)),
            pl.BlockSpec((8, LANES), lambda i: (0, 0)),
        ],
        out_specs=(pl.BlockSpec((ROW_TILE, tok_width), lambda i: (i, 0)),
                   kv_spec, kv_spec,
                   pl.BlockSpec((ROW_TILE, n - tok_width - 2 * kv_width), lambda i: (i, 0))),
        compiler_params=_params(1),
        name="inproj_attn",
    )(x2d, g, w_bf16, pos2d, rot_rows)


def _wattn_kernel(sink_ref, q_ref, kp_ref, kc_ref, kn_ref, vp_ref, vc_ref, vn_ref, o_ref, *, seq):
    n = pl.program_id(1)
    n_kv = kc_ref.shape[1]
    rows = GQA_RATIO * BLOCK
    qi = lax.broadcasted_iota(I32, (rows, 3 * BLOCK), 0) & (BLOCK - 1)
    kj = lax.broadcasted_iota(I32, (rows, 3 * BLOCK), 1)
    first = jnp.maximum(qi, BLOCK - n * BLOCK)
    last = jnp.minimum(qi + 2 * WINDOW, seq + BLOCK - 1 - n * BLOCK)
    valid = ((kj - first) | (last - kj)) >= 0
    for hk in range(n_kv):
        heads = [hk * GQA_RATIO + g for g in range(GQA_RATIO)]
        qs = jnp.concatenate([q_ref[0, :, h * HEAD_DIM:(h + 1) * HEAD_DIM] for h in heads], axis=0)
        kw = jnp.concatenate([kp_ref[0, hk], kc_ref[0, hk], kn_ref[0, hk]], axis=0)
        vw = jnp.concatenate([vp_ref[0, hk], vc_ref[0, hk], vn_ref[0, hk]], axis=0)
        s = jnp.where(valid, _dot_nt(qs, kw), NEG_INF)
        sk = jnp.concatenate([jnp.full((BLOCK, 1), sink_ref[h], F32) for h in heads], axis=0)
        m = jnp.maximum(jnp.max(s, axis=-1, keepdims=True), sk)
        p = jnp.exp(s - m)
        den = jnp.sum(p, axis=-1, keepdims=True) + jnp.exp(sk - m)
        o = _dot(p.astype(BF16), vw) / den
        for g, h in enumerate(heads):
            o_ref[0, :, h * HEAD_DIM:(h + 1) * HEAD_DIM] = o[g * BLOCK:(g + 1) * BLOCK].astype(BF16)


def _window_attention(sink, q3d, k4d, v4d):
    b, seq, tok_width = q3d.shape
    n_kv = k4d.shape[1]
    nb = seq // BLOCK
    kv_block = (1, n_kv, BLOCK, HEAD_DIM)
    prev_spec = pl.BlockSpec(kv_block, lambda i, n: (i, 0, jnp.maximum(n - 1, 0), 0))
    cur_spec = pl.BlockSpec(kv_block, lambda i, n: (i, 0, n, 0))
    next_spec = pl.BlockSpec(kv_block, lambda i, n: (i, 0, jnp.minimum(n + 1, nb - 1), 0))
    return pl.pallas_call(
        functools.partial(_wattn_kernel, seq=seq),
        out_shape=jax.ShapeDtypeStruct((b, seq, tok_width), BF16),
        grid=(b, nb),
        in_specs=[
            pl.BlockSpec(memory_space=pltpu.SMEM),
            pl.BlockSpec((1, BLOCK, tok_width), lambda i, n: (i, n, 0)),
            prev_spec, cur_spec, next_spec, prev_spec, cur_spec, next_spec,
        ],
        out_specs=pl.BlockSpec((1, BLOCK, tok_width), lambda i, n: (i, n, 0)),
        compiler_params=_params(2),
        name="window_attention",
    )(sink, q3d, k4d, k4d, k4d, v4d, v4d, v4d)


def _final_norm_kernel(x_ref, g_ref, o_ref):
    o_ref[...] = _rmsnorm_rows(x_ref[...], g_ref[...])


def _final_norm(x2d, g):
    t, d_model = x2d.shape
    return pl.pallas_call(
        _final_norm_kernel,
        out_shape=jax.ShapeDtypeStruct((t, d_model), F32),
        grid=(t // ROW_TILE,),
        in_specs=[pl.BlockSpec((ROW_TILE, d_model), lambda i: (i, 0)),
                  pl.BlockSpec((1, d_model), lambda i: (0, 0))],
        out_specs=pl.BlockSpec((ROW_TILE, d_model), lambda i: (i, 0)),
        compiler_params=_params(1),
        name="final_norm",
    )(x2d, g)


def _rotary_rows(dtype=F32):
    half = ROT_DIM // 2
    inv_freq = ROPE_THETA ** (-jnp.arange(0, ROT_DIM, 2, dtype=jnp.float32) / ROT_DIM)
    lane = np.arange(LANES) % HEAD_DIM
    rotated = lane < ROT_DIM
    freq = jnp.where(jnp.asarray(rotated), inv_freq[jnp.asarray(lane % half)], 0.0)
    rows = jnp.zeros((8, LANES), dtype)
    rows = rows.at[0].set(freq)
    rows = rows.at[1].set(jnp.asarray(np.where(lane < half, -1.0, 0.0), dtype))
    rows = rows.at[2].set(jnp.asarray(np.where(rotated & (lane >= half), 1.0, 0.0), dtype))
    return rows


def kernel(x, mem, positions, norm_mix_g, norm_ffn_g, mem_norm_g, final_g, mem_w_kv,
           pool_w_in, pool_group_w, pool_scale, pool_w_out,
           attn_w_in, attn_sink, attn_w_out,
           router_w, exp_w_gate, exp_w_up, exp_w_down):
    b, seq, d_model = x.shape
    depth = norm_mix_g.shape[0]
    t = b * seq
    n_experts = router_w.shape[2]
    tok_width = pool_scale.shape[1]
    n_groups = pool_group_w.shape[1]
    kv_width = (attn_w_in.shape[2] - tok_width - XA_WIDTH) // 2
    assert seq % ROW_TILE == 0 and mem.shape[1] == MEM_LEN

    memkv = _memkv(mem.reshape(b * MEM_LEN, d_model), mem_norm_g.reshape(1, d_model),
                   mem_w_kv.astype(BF16))
    memkv = memkv.reshape(depth, b, MEM_LEN, 2 * XA_WIDTH)
    rw_pad = jnp.pad(router_w, ((0, 0), (0, 0), (0, LANES - n_experts)))
    pos2d = positions.reshape(t, 1)
    rot_rows = _rotary_rows()

    x2d = x.reshape(t, d_model)
    for layer in range(depth):
        j = layer // 2
        g_mix = norm_mix_g[layer].reshape(1, d_model)
        if layer % 2 == 0:
            u, qm = _inproj_pool(x2d, g_mix, pool_w_in[j].astype(BF16), tok_width)
            tok = _pool_mixer(u.reshape(b, seq, tok_width), pool_group_w[j].astype(BF16),
                              pool_scale[j].reshape(n_groups, 1, tok_width // n_groups))
            tok = tok.reshape(t, tok_width)
            w_out = pool_w_out[j]
        else:
            q, k, v, qm = _inproj_attn(x2d, g_mix, attn_w_in[j].astype(BF16), pos2d, rot_rows,
                                       tok_width, kv_width, seq)
            tok = _window_attention(attn_sink[j], q.reshape(b, seq, tok_width), k, v)
            tok = tok.reshape(t, tok_width)
            w_out = attn_w_out[j]
        mo = _mem_xattn(qm, memkv[layer], seq)
        x1, h, afft, aff = _outproj(tok, mo, x2d, w_out.astype(BF16),
                                    norm_ffn_g[layer].reshape(1, d_model), rw_pad[layer],
                                    n_experts, seq)
        x2 = _moe(afft, aff, h, x1, exp_w_gate[layer], exp_w_up[layer], exp_w_down[layer], b, seq)
        x2d = x2.reshape(t, d_model)
    return _final_norm(x2d, final_g.reshape(1, d_model)).reshape(b, seq, d_model)
```

```python
import functools

import jax
import jax.numpy as jnp
import numpy as np
from jax import lax
from jax.experimental import pallas as pl
from jax.experimental.pallas import tpu as pltpu

F32 = jnp.float32
BF16 = jnp.bfloat16
I32 = jnp.int32

EPS = 1e-6
MEM_LEN = 256
XA_HEADS = 4
XA_HEAD_DIM = 128
XA_WIDTH = XA_HEADS * XA_HEAD_DIM
POOL_WINDOWS = (2, 4, 8, 16)
HEAD_DIM = 64
GQA_RATIO = 8
WINDOW = 128
BLOCK = 128
ROPE_THETA = 500000.0
ROT_DIM = 16
NEG_INF = -1e30
CAPACITY_FACTOR = 2

LANES = 128
MIB = 1024 * 1024
VMEM_LIMIT_BYTES = 56 * MIB

ROW_TILE = 512
POOL_PAD = 16
PREFIX_CHUNK = 256


def _params(n_grid_dims, flags=None):
    return pltpu.CompilerParams(
        dimension_semantics=("arbitrary",) * n_grid_dims,
        vmem_limit_bytes=VMEM_LIMIT_BYTES,
        flags=flags,
    )


def _resident(block_shape, index_map):
    return pl.BlockSpec(block_shape, index_map, pipeline_mode=pl.Buffered(1))


def _rmsnorm_rows(x, g):
    return x * lax.rsqrt(jnp.mean(x * x, axis=-1, keepdims=True) + EPS) * g


def _dot(a, b):
    return jnp.dot(a, b, preferred_element_type=F32)


def _dot_nt(a, b):
    return lax.dot_general(a, b, (((1,), (1,)), ((), ())), preferred_element_type=F32)


def _memkv_kernel(mem_ref, g_ref, w_ref, o_ref):
    hn = _rmsnorm_rows(mem_ref[...], g_ref[...]).astype(BF16)
    o_ref[0] = _dot(hn, w_ref[0]).astype(BF16)


def _memkv(mem2d, g, w_bf16):
    depth, d_model, n = w_bf16.shape
    rows = mem2d.shape[0]
    return pl.pallas_call(
        _memkv_kernel,
        out_shape=jax.ShapeDtypeStruct((depth, rows, n), BF16),
        grid=(depth, rows // ROW_TILE),
        in_specs=[
            pl.BlockSpec((ROW_TILE, d_model), lambda l, i: (i, 0)),
            pl.BlockSpec((1, d_model), lambda l, i: (0, 0)),
            pl.BlockSpec((1, d_model, n), lambda l, i: (l, 0, 0)),
        ],
        out_specs=pl.BlockSpec((1, ROW_TILE, n), lambda l, i: (l, i, 0)),
        compiler_params=_params(2),
        name="memkv",
    )(mem2d, g, w_bf16)


def _inproj_pool_kernel(x_ref, g_ref, w_ref, u_ref, qm_ref, *, tok_width):
    hn = _rmsnorm_rows(x_ref[...], g_ref[...]).astype(BF16)
    chunk = 512
    for c in range(0, tok_width, chunk):
        u_ref[:, c:c + chunk] = _dot(hn, w_ref[:, c:c + chunk])
    qm_ref[...] = _dot(hn, w_ref[:, tok_width:]).astype(BF16)


def _inproj_pool(x2d, g, w_bf16, tok_width):
    t, d_model = x2d.shape
    n = w_bf16.shape[1]
    return pl.pallas_call(
        functools.partial(_inproj_pool_kernel, tok_width=tok_width),
        out_shape=(jax.ShapeDtypeStruct((t, tok_width), F32),
                   jax.ShapeDtypeStruct((t, n - tok_width), BF16)),
        grid=(t // ROW_TILE,),
        in_specs=[
            pl.BlockSpec((ROW_TILE, d_model), lambda i: (i, 0)),
            pl.BlockSpec((1, d_model), lambda i: (0, 0)),
            _resident((d_model, n), lambda i: (0, 0)),
        ],
        out_specs=(pl.BlockSpec((ROW_TILE, tok_width), lambda i: (i, 0)),
                   pl.BlockSpec((ROW_TILE, n - tok_width), lambda i: (i, 0))),
        compiler_params=_params(1),
        name="inproj_pool",
    )(x2d, g, w_bf16)


def _pool_group(u_ref, gw_ref, sc_ref, o_ref, a_ref, b_ref, *, window, seq):
    gwid = u_ref.shape[2]
    rows = seq + 2 * POOL_PAD
    zeros_pad = jnp.zeros((POOL_PAD, gwid), F32)
    a_ref[0:POOL_PAD, :] = zeros_pad
    b_ref[0:POOL_PAD, :] = zeros_pad
    a_ref[POOL_PAD:POOL_PAD + seq, :] = u_ref[0]
    a_ref[POOL_PAD + seq:rows, :] = zeros_pad
    src, dst = a_ref, b_ref
    shift = 1
    while shift < window:
        dst[POOL_PAD:rows, :] = src[POOL_PAD - shift:rows - shift, :] + src[POOL_PAD:rows, :]
        src, dst = dst, src
        shift *= 2
    off = POOL_PAD + window // 2 - 1
    win = src[off:off + seq, :]
    t = lax.broadcasted_iota(I32, (seq, 1), 0)
    lo = jnp.maximum(t - window // 2, 0)
    hi = jnp.minimum(t + window // 2 - 1, seq - 1)
    cnt = (hi - lo + 1).astype(F32)
    pooled = (win / cnt - u_ref[0]).astype(BF16)
    o_ref[0] = (_dot(pooled, gw_ref[0]) * sc_ref[0]).astype(BF16)


def _pool_kernel(u_ref, gw_ref, sc_ref, o_ref, a_ref, b_ref, *, seq):
    g = pl.program_id(1)
    for k, window in enumerate(POOL_WINDOWS):
        @pl.when(g == k)
        def _():
            _pool_group(u_ref, gw_ref, sc_ref, o_ref, a_ref, b_ref, window=window, seq=seq)


def _pool_mixer(u3d, gw_bf16, scale3d):
    b, seq, tok_width = u3d.shape
    n_groups, gwid, _ = gw_bf16.shape
    assert n_groups == len(POOL_WINDOWS) and n_groups * gwid == tok_width
    return pl.pallas_call(
        functools.partial(_pool_kernel, seq=seq),
        out_shape=jax.ShapeDtypeStruct((b, seq, tok_width), BF16),
        grid=(b, n_groups),
        in_specs=[
            pl.BlockSpec((1, seq, gwid), lambda i, g: (i, 0, g)),
            pl.BlockSpec((1, gwid, gwid), lambda i, g: (g, 0, 0)),
            pl.BlockSpec((1, 1, gwid), lambda i, g: (g, 0, 0)),
        ],
        out_specs=pl.BlockSpec((1, seq, gwid), lambda i, g: (i, 0, g)),
        scratch_shapes=[pltpu.VMEM((seq + 2 * POOL_PAD, gwid), F32),
                        pltpu.VMEM((seq + 2 * POOL_PAD, gwid), F32)],
        compiler_params=_params(2),
        name="pool_mixer",
    )(u3d, gw_bf16, scale3d)


def _xattn_kernel(q_ref, kv_ref, o_ref):
    scale = XA_HEAD_DIM ** -0.5
    for h in range(XA_HEADS):
        lo = h * XA_HEAD_DIM
        q = q_ref[:, lo:lo + XA_HEAD_DIM]
        k = kv_ref[0, :, lo:lo + XA_HEAD_DIM]
        v = kv_ref[0, :, XA_WIDTH + lo:XA_WIDTH + lo + XA_HEAD_DIM]
        s = _dot_nt(q, k) * scale
        m = jnp.max(s, axis=-1, keepdims=True)
        p = jnp.exp(s - m)
        den = jnp.sum(p, axis=-1, keepdims=True)
        o_ref[:, lo:lo + XA_HEAD_DIM] = (_dot(p.astype(BF16), v) / den).astype(BF16)


def _mem_xattn(qm2d, memkv_layer, seq):
    t = qm2d.shape[0]
    tiles_per_seq = seq // ROW_TILE
    return pl.pallas_call(
        _xattn_kernel,
        out_shape=jax.ShapeDtypeStruct((t, XA_WIDTH), BF16),
        grid=(t // ROW_TILE,),
        in_specs=[
            pl.BlockSpec((ROW_TILE, XA_WIDTH), lambda i: (i, 0)),
            pl.BlockSpec((1, MEM_LEN, 2 * XA_WIDTH), lambda i: (i // tiles_per_seq, 0, 0)),
        ],
        out_specs=pl.BlockSpec((ROW_TILE, XA_WIDTH), lambda i: (i, 0)),
        compiler_params=_params(1),
        name="mem_xattn",
    )(qm2d, memkv_layer)


def _outproj_kernel(tok_ref, mo_ref, x_ref, w_ref, g_ref, rw_ref,
                    x1_ref, h_ref, afft_ref, aff_ref, wcat_ref, *, tok_width, n_experts):
    @pl.when(pl.program_id(0) == 0)
    def _():
        rw = rw_ref[...]
        w_hi = rw.astype(BF16)
        wcat_ref[:, 0:LANES] = w_hi
        wcat_ref[:, LANES:2 * LANES] = (rw - w_hi.astype(F32)).astype(BF16)

    y = _dot(tok_ref[...], w_ref[0:tok_width, :]) + _dot(mo_ref[...], w_ref[tok_width:, :])
    x1 = x_ref[...] + y
    x1_ref[...] = x1
    d_model = x1.shape[1]
    hn = _rmsnorm_rows(x1, g_ref[...])
    h_hi = hn.astype(BF16)
    h_ref[:, 0:d_model] = h_hi
    h_lo = (hn - h_hi.astype(F32)).astype(BF16)
    r = _dot(h_hi, wcat_ref[...]) + _dot(h_lo, wcat_ref[...])
    logits = r[:, 0:LANES] + r[:, LANES:2 * LANES]
    lt = logits.T[0:n_experts, :]
    m = jnp.max(lt, axis=0, keepdims=True)
    ex = jnp.exp(lt - m)
    afft = ex / jnp.sum(ex, axis=0, keepdims=True)
    afft_ref[0] = afft
    padded = jnp.concatenate(
        [afft, jnp.zeros((LANES - n_experts, afft.shape[1]), F32)], axis=0)
    aff = padded.T
    aff_ref[...] = aff
    h_ref[:, d_model:d_model + LANES] = _pack_gate_lanes(aff)


def _outproj(tok2d, mo2d, x2d, w_bf16, g, rw_pad, n_experts, seq):
    t, d_model = x2d.shape
    tok_width = tok2d.shape[1]
    tiles_per_seq = seq // ROW_TILE
    return pl.pallas_call(
        functools.partial(_outproj_kernel, tok_width=tok_width, n_experts=n_experts),
        out_shape=(jax.ShapeDtypeStruct((t, d_model), F32),
                   jax.ShapeDtypeStruct((t, d_model + LANES), BF16),
                   jax.ShapeDtypeStruct((t // seq, n_experts, seq), F32),
                   jax.ShapeDtypeStruct((t, LANES), F32)),
        grid=(t // ROW_TILE,),
        in_specs=[
            pl.BlockSpec((ROW_TILE, tok_width), lambda i: (i, 0)),
            pl.BlockSpec((ROW_TILE, mo2d.shape[1]), lambda i: (i, 0)),
            pl.BlockSpec((ROW_TILE, d_model), lambda i: (i, 0)),
            _resident((d_model, d_model), lambda i: (0, 0)),
            pl.BlockSpec((1, d_model), lambda i: (0, 0)),
            _resident((d_model, LANES), lambda i: (0, 0)),
        ],
        out_specs=(pl.BlockSpec((ROW_TILE, d_model), lambda i: (i, 0)),
                   pl.BlockSpec((ROW_TILE, d_model + LANES), lambda i: (i, 0)),
                   pl.BlockSpec((1, n_experts, ROW_TILE),
                                lambda i: (i // tiles_per_seq, 0, i % tiles_per_seq)),
                   pl.BlockSpec((ROW_TILE, LANES), lambda i: (i, 0))),
        scratch_shapes=[pltpu.VMEM((d_model, 2 * LANES), BF16)],
        compiler_params=_params(1),
        name="outproj_router",
    )(tok2d, mo2d, x2d, w_bf16, g, rw_pad)


def _strict_triangle(n, lower):
    r = lax.broadcasted_iota(I32, (n, n), 0)
    c = lax.broadcasted_iota(I32, (n, n), 1)
    return jnp.where((c < r) if lower else (r < c), 1.0, 0.0).astype(BF16)


def _prefix_rows(mask_f32):
    s, l = mask_f32.shape
    tri = _strict_triangle(PREFIX_CHUNK, lower=True)
    carry = jnp.zeros((1, l), F32)
    out = []
    for c in range(0, s, PREFIX_CHUNK):
        m = mask_f32[c:c + PREFIX_CHUNK, :]
        out.append(_dot(tri, m.astype(BF16)) + carry)
        carry = carry + jnp.sum(m, axis=0, keepdims=True)
    return jnp.concatenate(out, axis=0)


def _prefix_lanes(mask_f32):
    e, s = mask_f32.shape
    tri = _strict_triangle(PREFIX_CHUNK, lower=False)
    carry = jnp.zeros((e, 1), F32)
    out = []
    for c in range(0, s, PREFIX_CHUNK):
        m = mask_f32[:, c:c + PREFIX_CHUNK]
        out.append(_dot(m.astype(BF16), tri) + carry)
        carry = carry + jnp.sum(m, axis=1, keepdims=True)
    return jnp.concatenate(out, axis=1)


def _select_slots(key, thr, need, prefix_fn):
    gt = jnp.where(key > thr, 1.0, 0.0)
    eq = jnp.where(key == thr, 1.0, 0.0)
    eq_rank = prefix_fn(eq)
    sel = gt + eq * jnp.where(eq_rank < need, 1.0, 0.0)
    pos = prefix_fn(sel)
    return jnp.where(sel > 0.5, pos, -1.0)


GATE_GROUP = 16
GATE_PIECES = 3


def _pack_gate_lanes(aff):
    hi = aff.astype(BF16).astype(F32)
    r1 = aff - hi
    mid = r1.astype(BF16).astype(F32)
    lo = (r1 - mid).astype(BF16).astype(F32)
    packed = hi + pltpu.roll(mid, GATE_GROUP, 1) + pltpu.roll(lo, 2 * GATE_GROUP, 1)
    return packed.astype(BF16)


def _unpack_gate(tail, e):
    lane = lax.broadcasted_iota(I32, (1, LANES), 1)
    mine = ((lane & (GATE_GROUP - 1)) == e) & (lane < GATE_PIECES * GATE_GROUP)
    return jnp.sum(jnp.where(mine, tail.astype(F32), 0.0), axis=1, keepdims=True)


def _gather_kernel(afft_ref, h_ref, xg_ref, thr_ref, need_ref, posm_ref, *, cap):
    e = pl.program_id(1)
    n_experts = afft_ref.shape[1]

    @pl.when(e == 0)
    def _():
        key = pltpu.bitcast(afft_ref[0], I32)
        thr = jnp.zeros((n_experts, 1), I32)
        for bit in range(30, -1, -1):
            cand = thr | (1 << bit)
            cnt = jnp.sum(jnp.where(key >= cand, 1.0, 0.0), axis=1, keepdims=True)
            thr = jnp.where(cnt >= cap, cand, thr)
        n_gt = jnp.sum(jnp.where(key > thr, 1.0, 0.0), axis=1, keepdims=True)
        need = cap - n_gt
        thr_ref[0] = thr
        need_ref[0] = need
        posm_ref[...] = _select_slots(key, thr, need, _prefix_lanes)

    prow = posm_ref[pl.ds(e, 1), :]
    slot = lax.broadcasted_iota(I32, (cap, 1), 0).astype(F32)
    onehot = jnp.where(prow == slot, 1.0, 0.0).astype(BF16)
    xg_ref[0, 0] = _dot(onehot, h_ref[...]).astype(BF16)


def _expert_gather(afft, h_ext, cap):
    b, n_experts, seq = afft.shape
    width = h_ext.shape[1]
    return pl.pallas_call(
        functools.partial(_gather_kernel, cap=cap),
        out_shape=(jax.ShapeDtypeStruct((n_experts, b, cap, width), BF16),
                   jax.ShapeDtypeStruct((b, n_experts, 1), I32),
                   jax.ShapeDtypeStruct((b, n_experts, 1), F32)),
        grid=(b, n_experts),
        in_specs=[
            pl.BlockSpec((1, n_experts, seq), lambda i, e: (i, 0, 0)),
            pl.BlockSpec((seq, width), lambda i, e: (i, 0)),
        ],
        out_specs=(pl.BlockSpec((1, 1, cap, width), lambda i, e: (e, i, 0, 0)),
                   pl.BlockSpec((1, n_experts, 1), lambda i, e: (i, 0, 0)),
                   pl.BlockSpec((1, n_experts, 1), lambda i, e: (i, 0, 0))),
        scratch_shapes=[pltpu.VMEM((n_experts, seq), F32)],
        compiler_params=_params(2),
        name="expert_gather",
    )(afft, h_ext)


EXPERT_ROWS = 1024
EXPERT_FTILE = 256


def _expert_kernel(xg_ref, wg_ref, wu_ref, wd_ref, o_ref, acc_ref):
    e = pl.program_id(0)
    f = pl.program_id(2)
    nb, cap, width = xg_ref.shape[1], xg_ref.shape[2], xg_ref.shape[3]
    d_model = width - LANES
    rows = nb * cap

    @pl.when(f == 0)
    def _():
        acc_ref[...] = jnp.zeros_like(acc_ref)

    x = xg_ref[0, :, :, 0:d_model].reshape(rows, d_model)
    a = _dot(x, wg_ref[0, 0].astype(BF16))
    u = _dot(x, wu_ref[0, 0].astype(BF16))
    hact = (a * jax.nn.sigmoid(a) * u).astype(BF16)
    wd = wd_ref[0, 0].astype(BF16)
    chunk = 512
    for c in range(0, d_model, chunk):
        acc_ref[:, c:c + chunk] += _dot(hact, wd[:, c:c + chunk])

    @pl.when(f == pl.num_programs(2) - 1)
    def _():
        g = _unpack_gate(xg_ref[0, :, :, d_model:width].reshape(rows, LANES), e)
        gfull = jnp.broadcast_to(g, (rows, LANES))
        for c in range(0, d_model, LANES):
            y = acc_ref[:, c:c + LANES] * gfull
            o_ref[:, 0, :, c:c + LANES] = y.astype(BF16).reshape(nb, cap, LANES)


def _experts(xg, w_gate, w_up, w_down, layer):
    n_experts, b, cap, width = xg.shape
    d_model = width - LANES
    d_expert = w_gate.shape[3]
    nb = EXPERT_ROWS // cap
    return pl.pallas_call(
        _expert_kernel,
        out_shape=jax.ShapeDtypeStruct((b, n_experts, cap, d_model), BF16),
        grid=(n_experts, b // nb, d_expert // EXPERT_FTILE),
        in_specs=[
            pl.BlockSpec((1, nb, cap, width), lambda e, m, f: (e, m, 0, 0)),
            pl.BlockSpec((1, 1, d_model, EXPERT_FTILE), lambda e, m, f: (layer, e, 0, f)),
            pl.BlockSpec((1, 1, d_model, EXPERT_FTILE), lambda e, m, f: (layer, e, 0, f)),
            pl.BlockSpec((1, 1, EXPERT_FTILE, d_model), lambda e, m, f: (layer, e, f, 0)),
        ],
        out_specs=pl.BlockSpec((nb, 1, cap, d_model), lambda e, m, f: (m, e, 0, 0)),
        scratch_shapes=[pltpu.VMEM((EXPERT_ROWS, d_model), F32)],
        compiler_params=_params(3),
        name="experts",
    )(xg, w_gate, w_up, w_down)


COMBINE_NTILE = 512
COMBINE_MCHUNK = 512


def _combine_kernel(aff_ref, thr_ref, need_ref, x1_ref, y_ref, o_ref, p_ref, *, cap, n_experts):
    n = pl.program_id(1)
    seq = aff_ref.shape[0]

    @pl.when(n == 0)
    def _():
        key = pltpu.bitcast(aff_ref[...], I32)
        posm = _select_slots(key, thr_ref[0], need_ref[0], _prefix_rows)
        slot = lax.broadcasted_iota(I32, (1, cap), 1).astype(F32)
        for e in range(n_experts):
            col = posm[:, e:e + 1]
            p_ref[:, e * cap:(e + 1) * cap] = jnp.where(col == slot, 1.0, 0.0).astype(BF16)

    for r in range(0, seq, COMBINE_MCHUNK):
        o_ref[0, r:r + COMBINE_MCHUNK, :] = (
            x1_ref[0, r:r + COMBINE_MCHUNK, :]
            + _dot(p_ref[r:r + COMBINE_MCHUNK, :], y_ref[0]))


def _combine(aff2d, thr_row, need_row, x1_3d, yg3d, cap, n_experts):
    b, seq, d_model = x1_3d.shape
    return pl.pallas_call(
        functools.partial(_combine_kernel, cap=cap, n_experts=n_experts),
        out_shape=jax.ShapeDtypeStruct((b, seq, d_model), F32),
        grid=(b, d_model // COMBINE_NTILE),
        in_specs=[
            pl.BlockSpec((seq, LANES), lambda i, n: (i, 0)),
            pl.BlockSpec((1, 1, LANES), lambda i, n: (i, 0, 0)),
            pl.BlockSpec((1, 1, LANES), lambda i, n: (i, 0, 0)),
            pl.BlockSpec((1, seq, COMBINE_NTILE), lambda i, n: (i, 0, n)),
            pl.BlockSpec((1, n_experts * cap, COMBINE_NTILE), lambda i, n: (i, 0, n)),
        ],
        out_specs=pl.BlockSpec((1, seq, COMBINE_NTILE), lambda i, n: (i, 0, n)),
        scratch_shapes=[pltpu.VMEM((seq, n_experts * cap), BF16)],
        compiler_params=_params(2),
        name="combine",
    )(aff2d, thr_row, need_row, x1_3d, yg3d)


def _moe(afft, aff2d, h_ext, x1_2d, w_gate, w_up, w_down, layer, b, seq):
    n_experts = afft.shape[1]
    assert n_experts <= GATE_GROUP
    d_model = x1_2d.shape[1]
    cap = CAPACITY_FACTOR * seq // n_experts
    xg, thr, need = _expert_gather(afft, h_ext, cap)
    yg = _experts(xg, w_gate, w_up, w_down, layer)
    pad = LANES - n_experts
    thr_row = jnp.pad(thr.reshape(b, 1, n_experts), ((0, 0), (0, 0), (0, pad)),
                      constant_values=np.iinfo(np.int32).max)
    need_row = jnp.pad(need.reshape(b, 1, n_experts), ((0, 0), (0, 0), (0, pad)))
    return _combine(aff2d, thr_row, need_row, x1_2d.reshape(b, seq, d_model),
                    yg.reshape(b, n_experts * cap, d_model), cap, n_experts)


def _rotary_tile(t, cos, sin_lo, sin_hi):
    half = ROT_DIM // 2
    return t * cos + pltpu.roll(t, LANES - half, 1) * sin_lo + pltpu.roll(t, half, 1) * sin_hi


def _inproj_attn_kernel(x_ref, g_ref, w_ref, pos_ref, rot_ref,
                        q_ref, k_ref, v_ref, qm_ref, *, tok_width, kv_width):
    hn = _rmsnorm_rows(x_ref[...], g_ref[...]).astype(BF16)
    ang = pos_ref[...].astype(F32) * rot_ref[0:1, :]
    cos = jnp.cos(ang)
    sin = jnp.sin(ang)
    sin_lo = sin * rot_ref[1:2, :]
    sin_hi = sin * rot_ref[2:3, :]
    qscale = HEAD_DIM ** -0.5
    chunk = 512
    for c in range(0, tok_width, chunk):
        pc = _dot(hn, w_ref[:, c:c + chunk])
        for j in range(0, chunk, LANES):
            rot = _rotary_tile(pc[:, j:j + LANES], cos, sin_lo, sin_hi)
            q_ref[:, c + j:c + j + LANES] = (rot * qscale).astype(BF16)
    kv = _dot(hn, w_ref[:, tok_width:tok_width + 2 * kv_width])
    k01 = _rotary_tile(kv[:, 0:LANES], cos, sin_lo, sin_hi)
    k2x = _rotary_tile(kv[:, LANES:2 * LANES], cos, sin_lo, sin_hi)
    k_ref[0, 0] = k01[:, 0:HEAD_DIM].astype(BF16)
    k_ref[0, 1] = k01[:, HEAD_DIM:LANES].astype(BF16)
    k_ref[0, 2] = k2x[:, 0:HEAD_DIM].astype(BF16)
    for hh in range(kv_width // HEAD_DIM):
        lo = kv_width + hh * HEAD_DIM
        v_ref[0, hh] = kv[:, lo:lo + HEAD_DIM].astype(BF16)
    qm_ref[...] = _dot(hn, w_ref[:, tok_width + 2 * kv_width:]).astype(BF16)


def _inproj_attn(x2d, g, w_bf16, pos2d, rot_rows, tok_width, kv_width, seq):
    t, d_model = x2d.shape
    n = w_bf16.shape[1]
    n_kv = kv_width // HEAD_DIM
    assert n_kv == 3 and kv_width + HEAD_DIM == 2 * LANES
    tiles_per_seq = seq // ROW_TILE
    kv_spec = pl.BlockSpec((1, n_kv, ROW_TILE, HEAD_DIM),
                           lambda i: (i // tiles_per_seq, 0, i % tiles_per_seq, 0))
    return pl.pallas_call(
        functools.partial(_inproj_attn_kernel, tok_width=tok_width, kv_width=kv_width),
        out_shape=(jax.ShapeDtypeStruct((t, tok_width), BF16),
                   jax.ShapeDtypeStruct((t // seq, n_kv, seq, HEAD_DIM), BF16),
                   jax.ShapeDtypeStruct((t // seq, n_kv, seq, HEAD_DIM), BF16),
                   jax.ShapeDtypeStruct((t, n - tok_width - 2 * kv_width), BF16)),
        grid=(t // ROW_TILE,),
        in_specs=[
            pl.BlockSpec((ROW_TILE, d_model), lambda i: (i, 0)),
            pl.BlockSpec((1, d_model), lambda i: (0, 0)),
            _resident((d_model, n), lambda i: (0, 0)),
            pl.BlockSpec((ROW_TILE, 1), lambda i: (i, 0)),
            pl.BlockSpec((8, LANES), lambda i: (0, 0)),
        ],
        out_specs=(pl.BlockSpec((ROW_TILE, tok_width), lambda i: (i, 0)),
                   kv_spec, kv_spec,
                   pl.BlockSpec((ROW_TILE, n - tok_width - 2 * kv_width), lambda i: (i, 0))),
        compiler_params=_params(1),
        name="inproj_attn",
    )(x2d, g, w_bf16, pos2d, rot_rows)


def _wattn_kernel(sink_ref, q_ref, kp_ref, kc_ref, kn_ref, vp_ref, vc_ref, vn_ref, o_ref,
                  valid_ref, kpad_ref, vpad_ref, s_ref, p_ref, inv_ref, *, seq):
    n = pl.program_id(1)
    n_kv = kc_ref.shape[1]
    pairs = GQA_RATIO // 2
    half_rows = pairs * BLOCK
    qi = lax.broadcasted_iota(I32, (BLOCK, 3 * BLOCK), 0)
    kj = lax.broadcasted_iota(I32, (BLOCK, 3 * BLOCK), 1)
    first = jnp.maximum(qi, BLOCK - n * BLOCK)
    last = jnp.minimum(qi + 2 * WINDOW, seq + BLOCK - 1 - n * BLOCK)
    valid_ref[...] = jnp.where(((kj - first) | (last - kj)) >= 0, 1.0, 0.0)
    zeros = jnp.zeros((3 * BLOCK, HEAD_DIM), BF16)
    for hk in range(n_kv):
        kw = jnp.concatenate([kp_ref[0, hk], kc_ref[0, hk], kn_ref[0, hk]], axis=0)
        vw = jnp.concatenate([vp_ref[0, hk], vc_ref[0, hk], vn_ref[0, hk]], axis=0)
        kpad_ref[2 * hk] = jnp.concatenate([kw, zeros], axis=1)
        kpad_ref[2 * hk + 1] = jnp.concatenate([zeros, kw], axis=1)
        vpad_ref[2 * hk] = jnp.concatenate([vw, zeros], axis=1)
        vpad_ref[2 * hk + 1] = jnp.concatenate([zeros, vw], axis=1)
        tile0 = hk * pairs
        qs = jnp.concatenate(
            [q_ref[0, :, (tile0 + j) * LANES:(tile0 + j + 1) * LANES] for j in range(pairs)], axis=0)
        s_ref[hk, 0:half_rows, :] = _dot_nt(qs, kpad_ref[2 * hk])
        s_ref[hk, half_rows:2 * half_rows, :] = _dot_nt(qs, kpad_ref[2 * hk + 1])
    for hk in range(n_kv):
        for c in range(GQA_RATIO):
            j, odd = c % pairs, c // pairs
            r = c * BLOCK
            s = jnp.where(valid_ref[...] > 0.5, s_ref[hk, r:r + BLOCK, :], NEG_INF)
            sk = sink_ref[hk * GQA_RATIO + 2 * j + odd]
            m = jnp.maximum(jnp.max(s, axis=-1, keepdims=True), sk)
            p = jnp.exp(s - m)
            inv = 1.0 / (jnp.sum(p, axis=-1, keepdims=True) + jnp.exp(sk - m))
            p_ref[hk, r:r + BLOCK, :] = p.astype(BF16)
            inv_ref[hk, j * BLOCK:(j + 1) * BLOCK, odd * HEAD_DIM:(odd + 1) * HEAD_DIM] = (
                jnp.broadcast_to(inv, (BLOCK, HEAD_DIM)))
    for hk in range(n_kv):
        o = (_dot(p_ref[hk, 0:half_rows, :], vpad_ref[2 * hk])
             + _dot(p_ref[hk, half_rows:2 * half_rows, :], vpad_ref[2 * hk + 1])) * inv_ref[hk]
        for j in range(pairs):
            lo = (hk * pairs + j) * LANES
            o_ref[0, :, lo:lo + LANES] = o[j * BLOCK:(j + 1) * BLOCK].astype(BF16)


def _window_attention(sink, q3d, k4d, v4d):
    b, seq, tok_width = q3d.shape
    n_kv = k4d.shape[1]
    nb = seq // BLOCK
    kv_block = (1, n_kv, BLOCK, HEAD_DIM)
    prev_spec = pl.BlockSpec(kv_block, lambda i, n: (i, 0, jnp.maximum(n - 1, 0), 0))
    cur_spec = pl.BlockSpec(kv_block, lambda i, n: (i, 0, n, 0))
    next_spec = pl.BlockSpec(kv_block, lambda i, n: (i, 0, jnp.minimum(n + 1, nb - 1), 0))
    return pl.pallas_call(
        functools.partial(_wattn_kernel, seq=seq),
        out_shape=jax.ShapeDtypeStruct((b, seq, tok_width), BF16),
        grid=(b, nb),
        in_specs=[
            pl.BlockSpec(memory_space=pltpu.SMEM),
            pl.BlockSpec((1, BLOCK, tok_width), lambda i, n: (i, n, 0)),
            prev_spec, cur_spec, next_spec, prev_spec, cur_spec, next_spec,
        ],
        out_specs=pl.BlockSpec((1, BLOCK, tok_width), lambda i, n: (i, n, 0)),
        scratch_shapes=[pltpu.VMEM((BLOCK, 3 * BLOCK), F32),
                        pltpu.VMEM((2 * n_kv, 3 * BLOCK, LANES), BF16),
                        pltpu.VMEM((2 * n_kv, 3 * BLOCK, LANES), BF16),
                        pltpu.VMEM((n_kv, GQA_RATIO * BLOCK, 3 * BLOCK), F32),
                        pltpu.VMEM((n_kv, GQA_RATIO * BLOCK, 3 * BLOCK), BF16),
                        pltpu.VMEM((n_kv, GQA_RATIO // 2 * BLOCK, LANES), F32)],
        compiler_params=_params(2),
        name="window_attention",
    )(sink, q3d, k4d, k4d, k4d, v4d, v4d, v4d)


def _final_norm_kernel(x_ref, g_ref, o_ref):
    o_ref[...] = _rmsnorm_rows(x_ref[...], g_ref[...])


def _final_norm(x2d, g):
    t, d_model = x2d.shape
    return pl.pallas_call(
        _final_norm_kernel,
        out_shape=jax.ShapeDtypeStruct((t, d_model), F32),
        grid=(t // ROW_TILE,),
        in_specs=[pl.BlockSpec((ROW_TILE, d_model), lambda i: (i, 0)),
                  pl.BlockSpec((1, d_model), lambda i: (0, 0))],
        out_specs=pl.BlockSpec((ROW_TILE, d_model), lambda i: (i, 0)),
        compiler_params=_params(1),
        name="final_norm",
    )(x2d, g)


def _rotary_rows(dtype=F32):
    half = ROT_DIM // 2
    inv_freq = ROPE_THETA ** (-jnp.arange(0, ROT_DIM, 2, dtype=jnp.float32) / ROT_DIM)
    lane = np.arange(LANES) % HEAD_DIM
    rotated = lane < ROT_DIM
    freq = jnp.where(jnp.asarray(rotated), inv_freq[jnp.asarray(lane % half)], 0.0)
    rows = jnp.zeros((8, LANES), dtype)
    rows = rows.at[0].set(freq)
    rows = rows.at[1].set(jnp.asarray(np.where(lane < half, -1.0, 0.0), dtype))
    rows = rows.at[2].set(jnp.asarray(np.where(rotated & (lane >= half), 1.0, 0.0), dtype))
    return rows


def kernel(x, mem, positions, norm_mix_g, norm_ffn_g, mem_norm_g, final_g, mem_w_kv,
           pool_w_in, pool_group_w, pool_scale, pool_w_out,
           attn_w_in, attn_sink, attn_w_out,
           router_w, exp_w_gate, exp_w_up, exp_w_down):
    b, seq, d_model = x.shape
    depth = norm_mix_g.shape[0]
    t = b * seq
    n_experts = router_w.shape[2]
    tok_width = pool_scale.shape[1]
    n_groups = pool_group_w.shape[1]
    kv_width = (attn_w_in.shape[2] - tok_width - XA_WIDTH) // 2
    assert seq % ROW_TILE == 0 and mem.shape[1] == MEM_LEN

    memkv = _memkv(mem.reshape(b * MEM_LEN, d_model), mem_norm_g.reshape(1, d_model),
                   mem_w_kv.astype(BF16))
    memkv = memkv.reshape(depth, b, MEM_LEN, 2 * XA_WIDTH)
    rw_pad = jnp.pad(router_w, ((0, 0), (0, 0), (0, LANES - n_experts)))
    pos2d = positions.reshape(t, 1)
    rot_rows = _rotary_rows()

    x2d = x.reshape(t, d_model)
    for layer in range(depth):
        j = layer // 2
        g_mix = norm_mix_g[layer].reshape(1, d_model)
        if layer % 2 == 0:
            u, qm = _inproj_pool(x2d, g_mix, pool_w_in[j].astype(BF16), tok_width)
            tok = _pool_mixer(u.reshape(b, seq, tok_width), pool_group_w[j].astype(BF16),
                              pool_scale[j].reshape(n_groups, 1, tok_width // n_groups))
            tok = tok.reshape(t, tok_width)
            w_out = pool_w_out[j]
        else:
            q, k, v, qm = _inproj_attn(x2d, g_mix, attn_w_in[j].astype(BF16), pos2d, rot_rows,
                                       tok_width, kv_width, seq)
            tok = _window_attention(attn_sink[j], q.reshape(b, seq, tok_width), k, v)
            tok = tok.reshape(t, tok_width)
            w_out = attn_w_out[j]
        mo = _mem_xattn(qm, memkv[layer], seq)
        x1, h, afft, aff = _outproj(tok, mo, x2d, w_out.astype(BF16),
                                    norm_ffn_g[layer].reshape(1, d_model), rw_pad[layer],
                                    n_experts, seq)
        x2 = _moe(afft, aff, h, x1, exp_w_gate, exp_w_up, exp_w_down, layer, b, seq)
        x2d = x2.reshape(t, d_model)
    return _final_norm(x2d, final_g.reshape(1, d_model)).reshape(b, seq, d_model)
```

```python
import functools

import jax
import jax.numpy as jnp
import numpy as np
from jax import lax
from jax.experimental import pallas as pl
from jax.experimental.pallas import tpu as pltpu
from jax.experimental.pallas import tpu_sc as plsc

F32 = jnp.float32
BF16 = jnp.bfloat16
I32 = jnp.int32
U32 = jnp.uint32

EPS = 1e-6
MEM_LEN = 256
XA_HEADS = 4
XA_HEAD_DIM = 128
XA_WIDTH = XA_HEADS * XA_HEAD_DIM
POOL_WINDOWS = (2, 4, 8, 16)
HEAD_DIM = 64
GQA_RATIO = 8
WINDOW = 128
BLOCK = 128
ROPE_THETA = 500000.0
ROT_DIM = 16
NEG_INF = -1e30
CAPACITY_FACTOR = 2

LANES = 128
MIB = 1024 * 1024
VMEM_LIMIT_BYTES = 56 * MIB

ROW_TILE = 512
POOL_PAD = 16
PREFIX_CHUNK = 256


def _params(n_grid_dims, flags=None):
    return pltpu.CompilerParams(
        dimension_semantics=("arbitrary",) * n_grid_dims,
        vmem_limit_bytes=VMEM_LIMIT_BYTES,
        flags=flags,
    )


def _resident(block_shape, index_map):
    return pl.BlockSpec(block_shape, index_map, pipeline_mode=pl.Buffered(1))


def _rmsnorm_rows(x, g):
    return x * lax.rsqrt(jnp.mean(x * x, axis=-1, keepdims=True) + EPS) * g


def _dot(a, b):
    return jnp.dot(a, b, preferred_element_type=F32)


def _dot_nt(a, b):
    return lax.dot_general(a, b, (((1,), (1,)), ((), ())), preferred_element_type=F32)


def _memkv_kernel(mem_ref, g_ref, w_ref, o_ref):
    hn = _rmsnorm_rows(mem_ref[...], g_ref[...]).astype(BF16)
    o_ref[0] = _dot(hn, w_ref[0]).astype(BF16)


def _memkv(mem2d, g, w_bf16):
    depth, d_model, n = w_bf16.shape
    rows = mem2d.shape[0]
    return pl.pallas_call(
        _memkv_kernel,
        out_shape=jax.ShapeDtypeStruct((depth, rows, n), BF16),
        grid=(depth, rows // ROW_TILE),
        in_specs=[
            pl.BlockSpec((ROW_TILE, d_model), lambda l, i: (i, 0)),
            pl.BlockSpec((1, d_model), lambda l, i: (0, 0)),
            pl.BlockSpec((1, d_model, n), lambda l, i: (l, 0, 0)),
        ],
        out_specs=pl.BlockSpec((1, ROW_TILE, n), lambda l, i: (l, i, 0)),
        compiler_params=_params(2),
        name="memkv",
    )(mem2d, g, w_bf16)


def _inproj_pool_kernel(x_ref, g_ref, w_ref, u_ref, qm_ref, *, tok_width):
    hn = _rmsnorm_rows(x_ref[...], g_ref[...]).astype(BF16)
    chunk = 512
    for c in range(0, tok_width, chunk):
        u_ref[:, c:c + chunk] = _dot(hn, w_ref[:, c:c + chunk])
    qm_ref[...] = _dot(hn, w_ref[:, tok_width:]).astype(BF16)


def _inproj_pool(x2d, g, w_bf16, tok_width):
    t, d_model = x2d.shape
    n = w_bf16.shape[1]
    return pl.pallas_call(
        functools.partial(_inproj_pool_kernel, tok_width=tok_width),
        out_shape=(jax.ShapeDtypeStruct((t, tok_width), F32),
                   jax.ShapeDtypeStruct((t, n - tok_width), BF16)),
        grid=(t // ROW_TILE,),
        in_specs=[
            pl.BlockSpec((ROW_TILE, d_model), lambda i: (i, 0)),
            pl.BlockSpec((1, d_model), lambda i: (0, 0)),
            _resident((d_model, n), lambda i: (0, 0)),
        ],
        out_specs=(pl.BlockSpec((ROW_TILE, tok_width), lambda i: (i, 0)),
                   pl.BlockSpec((ROW_TILE, n - tok_width), lambda i: (i, 0))),
        compiler_params=_params(1),
        name="inproj_pool",
    )(x2d, g, w_bf16)


def _pool_group(u_ref, gw_ref, sc_ref, o_ref, a_ref, b_ref, *, window, seq):
    gwid = u_ref.shape[2]
    rows = seq + 2 * POOL_PAD
    zeros_pad = jnp.zeros((POOL_PAD, gwid), F32)
    a_ref[0:POOL_PAD, :] = zeros_pad
    b_ref[0:POOL_PAD, :] = zeros_pad
    a_ref[POOL_PAD:POOL_PAD + seq, :] = u_ref[0]
    a_ref[POOL_PAD + seq:rows, :] = zeros_pad
    src, dst = a_ref, b_ref
    shift = 1
    while shift < window:
        dst[POOL_PAD:rows, :] = src[POOL_PAD - shift:rows - shift, :] + src[POOL_PAD:rows, :]
        src, dst = dst, src
        shift *= 2
    off = POOL_PAD + window // 2 - 1
    win = src[off:off + seq, :]
    t = lax.broadcasted_iota(I32, (seq, 1), 0)
    lo = jnp.maximum(t - window // 2, 0)
    hi = jnp.minimum(t + window // 2 - 1, seq - 1)
    cnt = (hi - lo + 1).astype(F32)
    pooled = (win / cnt - u_ref[0]).astype(BF16)
    o_ref[0] = (_dot(pooled, gw_ref[0]) * sc_ref[0]).astype(BF16)


def _pool_kernel(u_ref, gw_ref, sc_ref, o_ref, a_ref, b_ref, *, seq):
    g = pl.program_id(1)
    for k, window in enumerate(POOL_WINDOWS):
        @pl.when(g == k)
        def _():
            _pool_group(u_ref, gw_ref, sc_ref, o_ref, a_ref, b_ref, window=window, seq=seq)


def _pool_mixer(u3d, gw_bf16, scale3d):
    b, seq, tok_width = u3d.shape
    n_groups, gwid, _ = gw_bf16.shape
    assert n_groups == len(POOL_WINDOWS) and n_groups * gwid == tok_width
    return pl.pallas_call(
        functools.partial(_pool_kernel, seq=seq),
        out_shape=jax.ShapeDtypeStruct((b, seq, tok_width), BF16),
        grid=(b, n_groups),
        in_specs=[
            pl.BlockSpec((1, seq, gwid), lambda i, g: (i, 0, g)),
            pl.BlockSpec((1, gwid, gwid), lambda i, g: (g, 0, 0)),
            pl.BlockSpec((1, 1, gwid), lambda i, g: (g, 0, 0)),
        ],
        out_specs=pl.BlockSpec((1, seq, gwid), lambda i, g: (i, 0, g)),
        scratch_shapes=[pltpu.VMEM((seq + 2 * POOL_PAD, gwid), F32),
                        pltpu.VMEM((seq + 2 * POOL_PAD, gwid), F32)],
        compiler_params=_params(2),
        name="pool_mixer",
    )(u3d, gw_bf16, scale3d)


def _xattn_kernel(q_ref, kv_ref, o_ref):
    scale = XA_HEAD_DIM ** -0.5
    for h in range(XA_HEADS):
        lo = h * XA_HEAD_DIM
        q = q_ref[:, lo:lo + XA_HEAD_DIM]
        k = kv_ref[0, :, lo:lo + XA_HEAD_DIM]
        v = kv_ref[0, :, XA_WIDTH + lo:XA_WIDTH + lo + XA_HEAD_DIM]
        s = _dot_nt(q, k) * scale
        m = jnp.max(s, axis=-1, keepdims=True)
        p = jnp.exp(s - m)
        den = jnp.sum(p, axis=-1, keepdims=True)
        o_ref[:, lo:lo + XA_HEAD_DIM] = (_dot(p.astype(BF16), v) / den).astype(BF16)


def _mem_xattn(qm2d, memkv_layer, seq):
    t = qm2d.shape[0]
    tiles_per_seq = seq // ROW_TILE
    return pl.pallas_call(
        _xattn_kernel,
        out_shape=jax.ShapeDtypeStruct((t, XA_WIDTH), BF16),
        grid=(t // ROW_TILE,),
        in_specs=[
            pl.BlockSpec((ROW_TILE, XA_WIDTH), lambda i: (i, 0)),
            pl.BlockSpec((1, MEM_LEN, 2 * XA_WIDTH), lambda i: (i // tiles_per_seq, 0, 0)),
        ],
        out_specs=pl.BlockSpec((ROW_TILE, XA_WIDTH), lambda i: (i, 0)),
        compiler_params=_params(1),
        name="mem_xattn",
    )(qm2d, memkv_layer)


def _outproj_kernel(tok_ref, mo_ref, x_ref, w_ref, g_ref, rw_ref,
                    x1_ref, h_ref, afft_ref, aff_ref, wcat_ref, hi_ref, lo_ref,
                    *, tok_width, n_experts):
    d_model = x_ref.shape[1]

    @pl.when(pl.program_id(0) == 0)
    def _():
        rw = rw_ref[...]
        w_hi = rw.astype(BF16)
        wcat_ref[:, 0:LANES] = w_hi
        wcat_ref[:, LANES:2 * LANES] = (rw - w_hi.astype(F32)).astype(BF16)
        hi_ref[...] = jnp.zeros_like(hi_ref)
        lo_ref[...] = jnp.zeros_like(lo_ref)

    h_prev = hi_ref[...]
    r = _dot(h_prev, wcat_ref[...]) + _dot(lo_ref[...], wcat_ref[...])
    logits = r[:, 0:LANES] + r[:, LANES:2 * LANES]
    lt = logits.T[0:n_experts, :]
    m = jnp.max(lt, axis=0, keepdims=True)
    ex = jnp.exp(lt - m)
    afft = ex / jnp.sum(ex, axis=0, keepdims=True)
    afft_ref[0] = afft
    padded = jnp.concatenate(
        [afft, jnp.zeros((LANES - n_experts, afft.shape[1]), F32)], axis=0)
    aff = padded.T
    aff_ref[...] = aff
    h_ref[...] = _pack_row_words(h_prev, _pack_gate_lanes(aff))

    y = _dot(tok_ref[...], w_ref[0:tok_width, :]) + _dot(mo_ref[...], w_ref[tok_width:, :])
    x1 = x_ref[...] + y
    x1_ref[...] = x1
    hn = _rmsnorm_rows(x1, g_ref[...])
    h_hi = hn.astype(BF16)
    hi_ref[...] = h_hi
    lo_ref[...] = (hn - h_hi.astype(F32)).astype(BF16)


def _outproj(tok2d, mo2d, x2d, w_bf16, g, rw_pad, n_experts, seq):
    t, d_model = x2d.shape
    tok_width = tok2d.shape[1]
    tiles_per_seq = seq // ROW_TILE
    n_tiles = t // ROW_TILE

    def cur(i):
        return jnp.minimum(i, n_tiles - 1)

    def prev(i):
        return jnp.maximum(i - 1, 0)

    return pl.pallas_call(
        functools.partial(_outproj_kernel, tok_width=tok_width, n_experts=n_experts),
        out_shape=(jax.ShapeDtypeStruct((t, d_model), F32),
                   jax.ShapeDtypeStruct((t, d_model // 2 + LANES), U32),
                   jax.ShapeDtypeStruct((t // seq, n_experts, seq), F32),
                   jax.ShapeDtypeStruct((t, LANES), F32)),
        grid=(n_tiles + 1,),
        in_specs=[
            pl.BlockSpec((ROW_TILE, tok_width), lambda i: (cur(i), 0)),
            pl.BlockSpec((ROW_TILE, mo2d.shape[1]), lambda i: (cur(i), 0)),
            pl.BlockSpec((ROW_TILE, d_model), lambda i: (cur(i), 0)),
            _resident((d_model, d_model), lambda i: (0, 0)),
            pl.BlockSpec((1, d_model), lambda i: (0, 0)),
            _resident((d_model, LANES), lambda i: (0, 0)),
        ],
        out_specs=(pl.BlockSpec((ROW_TILE, d_model), lambda i: (cur(i), 0)),
                   pl.BlockSpec((ROW_TILE, d_model // 2 + LANES), lambda i: (prev(i), 0)),
                   pl.BlockSpec((1, n_experts, ROW_TILE),
                                lambda i: (prev(i) // tiles_per_seq, 0, prev(i) % tiles_per_seq)),
                   pl.BlockSpec((ROW_TILE, LANES), lambda i: (prev(i), 0))),
        scratch_shapes=[pltpu.VMEM((d_model, 2 * LANES), BF16),
                        pltpu.VMEM((ROW_TILE, d_model), BF16),
                        pltpu.VMEM((ROW_TILE, d_model), BF16)],
        compiler_params=_params(1),
        name="outproj_router",
    )(tok2d, mo2d, x2d, w_bf16, g, rw_pad)


def _strict_triangle(n, lower):
    r = lax.broadcasted_iota(I32, (n, n), 0)
    c = lax.broadcasted_iota(I32, (n, n), 1)
    return jnp.where((c < r) if lower else (r < c), 1.0, 0.0).astype(BF16)


def _prefix_rows(mask_f32):
    s, l = mask_f32.shape
    tri = _strict_triangle(PREFIX_CHUNK, lower=True)
    carry = jnp.zeros((1, l), F32)
    out = []
    for c in range(0, s, PREFIX_CHUNK):
        m = mask_f32[c:c + PREFIX_CHUNK, :]
        out.append(_dot(tri, m.astype(BF16)) + carry)
        carry = carry + jnp.sum(m, axis=0, keepdims=True)
    return jnp.concatenate(out, axis=0)


def _prefix_lanes(mask_f32):
    e, s = mask_f32.shape
    tri = _strict_triangle(PREFIX_CHUNK, lower=False)
    carry = jnp.zeros((e, 1), F32)
    out = []
    for c in range(0, s, PREFIX_CHUNK):
        m = mask_f32[:, c:c + PREFIX_CHUNK]
        out.append(_dot(m.astype(BF16), tri) + carry)
        carry = carry + jnp.sum(m, axis=1, keepdims=True)
    return jnp.concatenate(out, axis=1)


def _select_slots(key, thr, need, prefix_fn):
    gt = jnp.where(key > thr, 1.0, 0.0)
    eq = jnp.where(key == thr, 1.0, 0.0)
    eq_rank = prefix_fn(eq)
    sel = gt + eq * jnp.where(eq_rank < need, 1.0, 0.0)
    pos = prefix_fn(sel)
    return jnp.where(sel > 0.5, pos, -1.0)


GATE_GROUP = 16
GATE_PIECES = 3


def _pack_gate_lanes(aff):
    hi = aff.astype(BF16).astype(F32)
    r1 = aff - hi
    mid = r1.astype(BF16).astype(F32)
    lo = (r1 - mid).astype(BF16).astype(F32)
    packed = hi + pltpu.roll(mid, GATE_GROUP, 1) + pltpu.roll(lo, 2 * GATE_GROUP, 1)
    return packed.astype(BF16)


def _unpack_gate(tail, e):
    lane = lax.broadcasted_iota(I32, (1, LANES), 1)
    mine = ((lane & (GATE_GROUP - 1)) == e) & (lane < GATE_PIECES * GATE_GROUP)
    return jnp.sum(jnp.where(mine, tail.astype(F32), 0.0), axis=1, keepdims=True)


def _pack_row_words(h, gate_tile):
    rows, d_model = h.shape
    half = d_model // 2
    hi = jnp.concatenate([h[:, 0:half], gate_tile], axis=1).astype(F32)
    lo = jnp.concatenate([h[:, half:], jnp.zeros((rows, LANES), BF16)], axis=1).astype(F32)
    return pltpu.bitcast(hi, U32) | (pltpu.bitcast(lo, U32) >> 16)


def _unpack_row_words(words):
    hi = pltpu.bitcast(words & jnp.uint32(0xFFFF0000), F32)
    lo = pltpu.bitcast(words << 16, F32)
    return hi, lo


def _route_kernel(afft_ref, posm_ref, thr_ref, need_ref, *, cap):
    n_experts = afft_ref.shape[1]
    key = pltpu.bitcast(afft_ref[0], I32)
    thr = jnp.zeros((n_experts, 1), I32)
    for bit in range(30, -1, -1):
        cand = thr | (1 << bit)
        cnt = jnp.sum(jnp.where(key >= cand, 1.0, 0.0), axis=1, keepdims=True)
        thr = jnp.where(cnt >= cap, cand, thr)
    n_gt = jnp.sum(jnp.where(key > thr, 1.0, 0.0), axis=1, keepdims=True)
    need = cap - n_gt
    thr_ref[0] = thr
    need_ref[0] = need
    posm_ref[0] = _select_slots(key, thr, need, _prefix_lanes).astype(I32)


def _route(afft, cap):
    b, n_experts, seq = afft.shape
    return pl.pallas_call(
        functools.partial(_route_kernel, cap=cap),
        out_shape=(jax.ShapeDtypeStruct((b, n_experts, seq), I32),
                   jax.ShapeDtypeStruct((b, n_experts, 1), I32),
                   jax.ShapeDtypeStruct((b, n_experts, 1), F32)),
        grid=(b,),
        in_specs=[pl.BlockSpec((1, n_experts, seq), lambda i: (i, 0, 0))],
        out_specs=(pl.BlockSpec((1, n_experts, seq), lambda i: (i, 0, 0)),
                   pl.BlockSpec((1, n_experts, 1), lambda i: (i, 0, 0)),
                   pl.BlockSpec((1, n_experts, 1), lambda i: (i, 0, 0))),
        compiler_params=_params(1),
        name="expert_route",
    )(afft)


SC_LANES = 16
SC_GATHER_ROWS = 64


def _sc_expert_gather(posm2d, h_words, n_experts, b, seq, cap):
    width = h_words.shape[1]
    info = plsc.get_sparse_core_info()
    n_cores, n_subcores = info.num_cores, info.num_subcores
    n_workers = n_cores * n_subcores
    assert info.num_lanes == SC_LANES and (b * n_experts) % n_workers == 0
    pairs_per_worker = (b * n_experts) // n_workers
    mesh = plsc.VectorSubcoreMesh(core_axis_name="c", subcore_axis_name="s")

    @functools.partial(
        pl.kernel, mesh=mesh,
        out_type=jax.ShapeDtypeStruct((n_experts * b * cap, width), U32),
        compiler_params=pltpu.CompilerParams(needs_layout_passes=False),
        scratch_types=[
            pltpu.VMEM((seq,), I32),
            pltpu.VMEM((cap,), I32),
            pltpu.VMEM((SC_GATHER_ROWS, width), U32),
            pltpu.SemaphoreType.DMA,
        ],
        name="sc_expert_gather",
    )
    def gather(posm_hbm, h_hbm, out_hbm, pos_v, idx_v, rows_v, sem):
        wid = lax.axis_index("s") * n_cores + lax.axis_index("c")
        for p in range(pairs_per_worker):
            pair = wid * pairs_per_worker + p
            bi = pair // n_experts
            e = pair - bi * n_experts
            pltpu.sync_copy(posm_hbm.at[pair], pos_v)

            @pl.loop(0, seq, step=SC_LANES)
            def _(t0):
                slots = pos_v[pl.ds(t0, SC_LANES)]
                rows = lax.iota(I32, SC_LANES) + (t0 + bi * seq)
                plsc.store_scatter(idx_v, [slots], rows, mask=slots >= 0)

            out_base = (e * b + bi) * cap
            for c in range(cap // SC_GATHER_ROWS):
                chunk = idx_v.at[pl.ds(c * SC_GATHER_ROWS, SC_GATHER_ROWS)]
                pltpu.async_copy(h_hbm.at[chunk], rows_v, sem).wait()
                pltpu.sync_copy(rows_v, out_hbm.at[pl.ds(out_base + c * SC_GATHER_ROWS, SC_GATHER_ROWS)])

    return gather(posm2d, h_words).reshape(n_experts, b, cap, width)


EXPERT_ROWS = 1024
EXPERT_FTILE = 256


def _expert_kernel(xg_ref, wg_ref, wu_ref, wd_ref, o_ref, acc_ref, x_ref, g_ref):
    e = pl.program_id(0)
    f = pl.program_id(2)
    nb, cap, width = xg_ref.shape[1], xg_ref.shape[2], xg_ref.shape[3]
    half = width - LANES
    d_model = 2 * half
    rows = nb * cap

    @pl.when(f == 0)
    def _():
        acc_ref[...] = jnp.zeros_like(acc_ref)
        hi, lo = _unpack_row_words(xg_ref[0].reshape(rows, width))
        x_ref[:, 0:half] = hi[:, 0:half].astype(BF16)
        x_ref[:, half:d_model] = lo[:, 0:half].astype(BF16)
        g_ref[...] = jnp.broadcast_to(_unpack_gate(hi[:, half:width], e), (rows, LANES))

    x = x_ref[...]
    a = _dot(x, wg_ref[0, 0].astype(BF16))
    u = _dot(x, wu_ref[0, 0].astype(BF16))
    hact = (a * jax.nn.sigmoid(a) * u).astype(BF16)
    wd = wd_ref[0, 0].astype(BF16)
    chunk = 512
    for c in range(0, d_model, chunk):
        acc_ref[:, c:c + chunk] += _dot(hact, wd[:, c:c + chunk])

    @pl.when(f == pl.num_programs(2) - 1)
    def _():
        gfull = g_ref[...]
        for c in range(0, d_model, LANES):
            y = acc_ref[:, c:c + LANES] * gfull
            o_ref[:, 0, :, c:c + LANES] = y.astype(BF16).reshape(nb, cap, LANES)


def _experts(xg, w_gate, w_up, w_down, layer):
    n_experts, b, cap, width = xg.shape
    d_model = 2 * (width - LANES)
    d_expert = w_gate.shape[3]
    nb = EXPERT_ROWS // cap
    return pl.pallas_call(
        _expert_kernel,
        out_shape=jax.ShapeDtypeStruct((b, n_experts, cap, d_model), BF16),
        grid=(n_experts, b // nb, d_expert // EXPERT_FTILE),
        in_specs=[
            pl.BlockSpec((1, nb, cap, width), lambda e, m, f: (e, m, 0, 0)),
            pl.BlockSpec((1, 1, d_model, EXPERT_FTILE), lambda e, m, f: (layer, e, 0, f)),
            pl.BlockSpec((1, 1, d_model, EXPERT_FTILE), lambda e, m, f: (layer, e, 0, f)),
            pl.BlockSpec((1, 1, EXPERT_FTILE, d_model), lambda e, m, f: (layer, e, f, 0)),
        ],
        out_specs=pl.BlockSpec((nb, 1, cap, d_model), lambda e, m, f: (m, e, 0, 0)),
        scratch_shapes=[pltpu.VMEM((EXPERT_ROWS, d_model), F32),
                        pltpu.VMEM((EXPERT_ROWS, d_model), BF16),
                        pltpu.VMEM((EXPERT_ROWS, LANES), F32)],
        compiler_params=_params(3),
        name="experts",
    )(xg, w_gate, w_up, w_down)


COMBINE_NTILE = 512
COMBINE_MCHUNK = 512


def _combine_kernel(aff_ref, thr_ref, need_ref, x1_ref, y_ref, o_ref, p_ref, *, cap, n_experts):
    n = pl.program_id(1)
    seq = aff_ref.shape[0]

    @pl.when(n == 0)
    def _():
        key = pltpu.bitcast(aff_ref[...], I32)
        posm = _select_slots(key, thr_ref[0], need_ref[0], _prefix_rows)
        slot = lax.broadcasted_iota(I32, (1, cap), 1).astype(F32)
        for e in range(n_experts):
            col = posm[:, e:e + 1]
            p_ref[:, e * cap:(e + 1) * cap] = jnp.where(col == slot, 1.0, 0.0).astype(BF16)

    for r in range(0, seq, COMBINE_MCHUNK):
        o_ref[0, r:r + COMBINE_MCHUNK, :] = (
            x1_ref[0, r:r + COMBINE_MCHUNK, :]
            + _dot(p_ref[r:r + COMBINE_MCHUNK, :], y_ref[0]))


def _combine(aff2d, thr_row, need_row, x1_3d, yg3d, cap, n_experts):
    b, seq, d_model = x1_3d.shape
    return pl.pallas_call(
        functools.partial(_combine_kernel, cap=cap, n_experts=n_experts),
        out_shape=jax.ShapeDtypeStruct((b, seq, d_model), F32),
        grid=(b, d_model // COMBINE_NTILE),
        in_specs=[
            pl.BlockSpec((seq, LANES), lambda i, n: (i, 0)),
            pl.BlockSpec((1, 1, LANES), lambda i, n: (i, 0, 0)),
            pl.BlockSpec((1, 1, LANES), lambda i, n: (i, 0, 0)),
            pl.BlockSpec((1, seq, COMBINE_NTILE), lambda i, n: (i, 0, n)),
            pl.BlockSpec((1, n_experts * cap, COMBINE_NTILE), lambda i, n: (i, 0, n)),
        ],
        out_specs=pl.BlockSpec((1, seq, COMBINE_NTILE), lambda i, n: (i, 0, n)),
        scratch_shapes=[pltpu.VMEM((seq, n_experts * cap), BF16)],
        compiler_params=_params(2),
        name="combine",
    )(aff2d, thr_row, need_row, x1_3d, yg3d)


def _moe(afft, aff2d, h_ext, x1_2d, w_gate, w_up, w_down, layer, b, seq):
    n_experts = afft.shape[1]
    assert n_experts <= GATE_GROUP
    d_model = x1_2d.shape[1]
    cap = CAPACITY_FACTOR * seq // n_experts
    posm, thr, need = _route(afft, cap)
    xg = _sc_expert_gather(posm.reshape(b * n_experts, seq), h_ext, n_experts, b, seq, cap)
    yg = _experts(xg, w_gate, w_up, w_down, layer)
    pad = LANES - n_experts
    thr_row = jnp.pad(thr.reshape(b, 1, n_experts), ((0, 0), (0, 0), (0, pad)),
                      constant_values=np.iinfo(np.int32).max)
    need_row = jnp.pad(need.reshape(b, 1, n_experts), ((0, 0), (0, 0), (0, pad)))
    return _combine(aff2d, thr_row, need_row, x1_2d.reshape(b, seq, d_model),
                    yg.reshape(b, n_experts * cap, d_model), cap, n_experts)


def _rotary_tile(t, cos, sin_lo, sin_hi):
    half = ROT_DIM // 2
    return t * cos + pltpu.roll(t, LANES - half, 1) * sin_lo + pltpu.roll(t, half, 1) * sin_hi


def _inproj_attn_kernel(x_ref, g_ref, w_ref, pos_ref, rot_ref,
                        q_ref, k_ref, v_ref, qm_ref, *, tok_width, kv_width):
    hn = _rmsnorm_rows(x_ref[...], g_ref[...]).astype(BF16)
    ang = pos_ref[...].astype(F32) * rot_ref[0:1, :]
    cos = jnp.cos(ang)
    sin = jnp.sin(ang)
    sin_lo = sin * rot_ref[1:2, :]
    sin_hi = sin * rot_ref[2:3, :]
    qscale = HEAD_DIM ** -0.5
    chunk = 512
    for c in range(0, tok_width, chunk):
        pc = _dot(hn, w_ref[:, c:c + chunk])
        for j in range(0, chunk, LANES):
            rot = _rotary_tile(pc[:, j:j + LANES], cos, sin_lo, sin_hi)
            q_ref[:, c + j:c + j + LANES] = (rot * qscale).astype(BF16)
    kv = _dot(hn, w_ref[:, tok_width:tok_width + 2 * kv_width])
    k01 = _rotary_tile(kv[:, 0:LANES], cos, sin_lo, sin_hi)
    k2x = _rotary_tile(kv[:, LANES:2 * LANES], cos, sin_lo, sin_hi)
    k_ref[0, 0] = k01[:, 0:HEAD_DIM].astype(BF16)
    k_ref[0, 1] = k01[:, HEAD_DIM:LANES].astype(BF16)
    k_ref[0, 2] = k2x[:, 0:HEAD_DIM].astype(BF16)
    for hh in range(kv_width // HEAD_DIM):
        lo = kv_width + hh * HEAD_DIM
        v_ref[0, hh] = kv[:, lo:lo + HEAD_DIM].astype(BF16)
    qm_ref[...] = _dot(hn, w_ref[:, tok_width + 2 * kv_width:]).astype(BF16)


def _inproj_attn(x2d, g, w_bf16, pos2d, rot_rows, tok_width, kv_width, seq):
    t, d_model = x2d.shape
    n = w_bf16.shape[1]
    n_kv = kv_width // HEAD_DIM
    assert n_kv == 3 and kv_width + HEAD_DIM == 2 * LANES
    tiles_per_seq = seq // ROW_TILE
    kv_spec = pl.BlockSpec((1, n_kv, ROW_TILE, HEAD_DIM),
                           lambda i: (i // tiles_per_seq, 0, i % tiles_per_seq, 0))
    return pl.pallas_call(
        functools.partial(_inproj_attn_kernel, tok_width=tok_width, kv_width=kv_width),
        out_shape=(jax.ShapeDtypeStruct((t, tok_width), BF16),
                   jax.ShapeDtypeStruct((t // seq, n_kv, seq, HEAD_DIM), BF16),
                   jax.ShapeDtypeStruct((t // seq, n_kv, seq, HEAD_DIM), BF16),
                   jax.ShapeDtypeStruct((t, n - tok_width - 2 * kv_width), BF16)),
        grid=(t // ROW_TILE,),
        in_specs=[
            pl.BlockSpec((ROW_TILE, d_model), lambda i: (i, 0)),
            pl.BlockSpec((1, d_model), lambda i: (0, 0)),
            _resident((d_model, n), lambda i: (0, 0)),
            pl.BlockSpec((ROW_TILE, 1), lambda i: (i, 0)),
            pl.BlockSpec((8, LANES), lambda i: (0, 0)),
        ],
        out_specs=(pl.BlockSpec((ROW_TILE, tok_width), lambda i: (i, 0)),
                   kv_spec, kv_spec,
                   pl.BlockSpec((ROW_TILE, n - tok_width - 2 * kv_width), lambda i: (i, 0))),
        compiler_params=_params(1),
        name="inproj_attn",
    )(x2d, g, w_bf16, pos2d, rot_rows)


def _wattn_kernel(sink_ref, q_ref, kp_ref, kc_ref, kn_ref, vp_ref, vc_ref, vn_ref, o_ref,
                  valid_ref, kpad_ref, vpad_ref, s_ref, p_ref, inv_ref, *, seq):
    n = pl.program_id(1)
    n_kv = kc_ref.shape[1]
    pairs = GQA_RATIO // 2
    half_rows = pairs * BLOCK
    qi = lax.broadcasted_iota(I32, (BLOCK, 3 * BLOCK), 0)
    kj = lax.broadcasted_iota(I32, (BLOCK, 3 * BLOCK), 1)
    first = jnp.maximum(qi, BLOCK - n * BLOCK)
    last = jnp.minimum(qi + 2 * WINDOW, seq + BLOCK - 1 - n * BLOCK)
    valid_ref[...] = jnp.where(((kj - first) | (last - kj)) >= 0, 1.0, 0.0)
    zeros = jnp.zeros((3 * BLOCK, HEAD_DIM), BF16)
    for hk in range(n_kv):
        kw = jnp.concatenate([kp_ref[0, hk], kc_ref[0, hk], kn_ref[0, hk]], axis=0)
        vw = jnp.concatenate([vp_ref[0, hk], vc_ref[0, hk], vn_ref[0, hk]], axis=0)
        kpad_ref[2 * hk] = jnp.concatenate([kw, zeros], axis=1)
        kpad_ref[2 * hk + 1] = jnp.concatenate([zeros, kw], axis=1)
        vpad_ref[2 * hk] = jnp.concatenate([vw, zeros], axis=1)
        vpad_ref[2 * hk + 1] = jnp.concatenate([zeros, vw], axis=1)
        tile0 = hk * pairs
        qs = jnp.concatenate(
            [q_ref[0, :, (tile0 + j) * LANES:(tile0 + j + 1) * LANES] for j in range(pairs)], axis=0)
        s_ref[hk, 0:half_rows, :] = _dot_nt(qs, kpad_ref[2 * hk])
        s_ref[hk, half_rows:2 * half_rows, :] = _dot_nt(qs, kpad_ref[2 * hk + 1])
    for hk in range(n_kv):
        for c in range(GQA_RATIO):
            j, odd = c % pairs, c // pairs
            r = c * BLOCK
            s = jnp.where(valid_ref[...] > 0.5, s_ref[hk, r:r + BLOCK, :], NEG_INF)
            sk = sink_ref[hk * GQA_RATIO + 2 * j + odd]
            m = jnp.maximum(jnp.max(s, axis=-1, keepdims=True), sk)
            p = jnp.exp(s - m)
            inv = 1.0 / (jnp.sum(p, axis=-1, keepdims=True) + jnp.exp(sk - m))
            p_ref[hk, r:r + BLOCK, :] = p.astype(BF16)
            inv_ref[hk, j * BLOCK:(j + 1) * BLOCK, odd * HEAD_DIM:(odd + 1) * HEAD_DIM] = (
                jnp.broadcast_to(inv, (BLOCK, HEAD_DIM)))
    for hk in range(n_kv):
        o = (_dot(p_ref[hk, 0:half_rows, :], vpad_ref[2 * hk])
             + _dot(p_ref[hk, half_rows:2 * half_rows, :], vpad_ref[2 * hk + 1])) * inv_ref[hk]
        for j in range(pairs):
            lo = (hk * pairs + j) * LANES
            o_ref[0, :, lo:lo + LANES] = o[j * BLOCK:(j + 1) * BLOCK].astype(BF16)


def _window_attention(sink, q3d, k4d, v4d):
    b, seq, tok_width = q3d.shape
    n_kv = k4d.shape[1]
    nb = seq // BLOCK
    kv_block = (1, n_kv, BLOCK, HEAD_DIM)
    prev_spec = pl.BlockSpec(kv_block, lambda i, n: (i, 0, jnp.maximum(n - 1, 0), 0))
    cur_spec = pl.BlockSpec(kv_block, lambda i, n: (i, 0, n, 0))
    next_spec = pl.BlockSpec(kv_block, lambda i, n: (i, 0, jnp.minimum(n + 1, nb - 1), 0))
    return pl.pallas_call(
        functools.partial(_wattn_kernel, seq=seq),
        out_shape=jax.ShapeDtypeStruct((b, seq, tok_width), BF16),
        grid=(b, nb),
        in_specs=[
            pl.BlockSpec(memory_space=pltpu.SMEM),
            pl.BlockSpec((1, BLOCK, tok_width), lambda i, n: (i, n, 0)),
            prev_spec, cur_spec, next_spec, prev_spec, cur_spec, next_spec,
        ],
        out_specs=pl.BlockSpec((1, BLOCK, tok_width), lambda i, n: (i, n, 0)),
        scratch_shapes=[pltpu.VMEM((BLOCK, 3 * BLOCK), F32),
                        pltpu.VMEM((2 * n_kv, 3 * BLOCK, LANES), BF16),
                        pltpu.VMEM((2 * n_kv, 3 * BLOCK, LANES), BF16),
                        pltpu.VMEM((n_kv, GQA_RATIO * BLOCK, 3 * BLOCK), F32),
                        pltpu.VMEM((n_kv, GQA_RATIO * BLOCK, 3 * BLOCK), BF16),
                        pltpu.VMEM((n_kv, GQA_RATIO // 2 * BLOCK, LANES), F32)],
        compiler_params=_params(2),
        name="window_attention",
    )(sink, q3d, k4d, k4d, k4d, v4d, v4d, v4d)


def _final_norm_kernel(x_ref, g_ref, o_ref):
    o_ref[...] = _rmsnorm_rows(x_ref[...], g_ref[...])


def _final_norm(x2d, g):
    t, d_model = x2d.shape
    return pl.pallas_call(
        _final_norm_kernel,
        out_shape=jax.ShapeDtypeStruct((t, d_model), F32),
        grid=(t // ROW_TILE,),
        in_specs=[pl.BlockSpec((ROW_TILE, d_model), lambda i: (i, 0)),
                  pl.BlockSpec((1, d_model), lambda i: (0, 0))],
        out_specs=pl.BlockSpec((ROW_TILE, d_model), lambda i: (i, 0)),
        compiler_params=_params(1),
        name="final_norm",
    )(x2d, g)


def _rotary_rows(dtype=F32):
    half = ROT_DIM // 2
    inv_freq = ROPE_THETA ** (-jnp.arange(0, ROT_DIM, 2, dtype=jnp.float32) / ROT_DIM)
    lane = np.arange(LANES) % HEAD_DIM
    rotated = lane < ROT_DIM
    freq = jnp.where(jnp.asarray(rotated), inv_freq[jnp.asarray(lane % half)], 0.0)
    rows = jnp.zeros((8, LANES), dtype)
    rows = rows.at[0].set(freq)
    rows = rows.at[1].set(jnp.asarray(np.where(lane < half, -1.0, 0.0), dtype))
    rows = rows.at[2].set(jnp.asarray(np.where(rotated & (lane >= half), 1.0, 0.0), dtype))
    return rows


def kernel(x, mem, positions, norm_mix_g, norm_ffn_g, mem_norm_g, final_g, mem_w_kv,
           pool_w_in, pool_group_w, pool_scale, pool_w_out,
           attn_w_in, attn_sink, attn_w_out,
           router_w, exp_w_gate, exp_w_up, exp_w_down):
    b, seq, d_model = x.shape
    depth = norm_mix_g.shape[0]
    t = b * seq
    n_experts = router_w.shape[2]
    tok_width = pool_scale.shape[1]
    n_groups = pool_group_w.shape[1]
    kv_width = (attn_w_in.shape[2] - tok_width - XA_WIDTH) // 2
    assert seq % ROW_TILE == 0 and mem.shape[1] == MEM_LEN

    memkv = _memkv(mem.reshape(b * MEM_LEN, d_model), mem_norm_g.reshape(1, d_model),
                   mem_w_kv.astype(BF16))
    memkv = memkv.reshape(depth, b, MEM_LEN, 2 * XA_WIDTH)
    rw_pad = jnp.pad(router_w, ((0, 0), (0, 0), (0, LANES - n_experts)))
    pos2d = positions.reshape(t, 1)
    rot_rows = _rotary_rows()

    x2d = x.reshape(t, d_model)
    for layer in range(depth):
        j = layer // 2
        g_mix = norm_mix_g[layer].reshape(1, d_model)
        if layer % 2 == 0:
            u, qm = _inproj_pool(x2d, g_mix, pool_w_in[j].astype(BF16), tok_width)
            tok = _pool_mixer(u.reshape(b, seq, tok_width), pool_group_w[j].astype(BF16),
                              pool_scale[j].reshape(n_groups, 1, tok_width // n_groups))
            tok = tok.reshape(t, tok_width)
            w_out = pool_w_out[j]
        else:
            q, k, v, qm = _inproj_attn(x2d, g_mix, attn_w_in[j].astype(BF16), pos2d, rot_rows,
                                       tok_width, kv_width, seq)
            tok = _window_attention(attn_sink[j], q.reshape(b, seq, tok_width), k, v)
            tok = tok.reshape(t, tok_width)
            w_out = attn_w_out[j]
        mo = _mem_xattn(qm, memkv[layer], seq)
        x1, h, afft, aff = _outproj(tok, mo, x2d, w_out.astype(BF16),
                                    norm_ffn_g[layer].reshape(1, d_model), rw_pad[layer],
                                    n_experts, seq)
        x2 = _moe(afft, aff, h, x1, exp_w_gate, exp_w_up, exp_w_down, layer, b, seq)
        x2d = x2.reshape(t, d_model)
    return _final_norm(x2d, final_g.reshape(1, d_model)).reshape(b, seq, d_model)
```

```python
import functools

import jax
import jax.numpy as jnp
import numpy as np
from jax import lax
from jax.experimental import pallas as pl
from jax.experimental.pallas import tpu as pltpu
from jax.experimental.pallas import tpu_sc as plsc

F32 = jnp.float32
BF16 = jnp.bfloat16
I32 = jnp.int32
U32 = jnp.uint32

EPS = 1e-6
MEM_LEN = 256
XA_HEADS = 4
XA_HEAD_DIM = 128
XA_WIDTH = XA_HEADS * XA_HEAD_DIM
POOL_WINDOWS = (2, 4, 8, 16)
HEAD_DIM = 64
GQA_RATIO = 8
WINDOW = 128
BLOCK = 128
ROPE_THETA = 500000.0
ROT_DIM = 16
NEG_INF = -1e30
CAPACITY_FACTOR = 2

LANES = 128
MIB = 1024 * 1024
VMEM_LIMIT_BYTES = 56 * MIB

ROW_TILE = 512
POOL_PAD = 16
PREFIX_CHUNK = 256


def _params(n_grid_dims, flags=None):
    return pltpu.CompilerParams(
        dimension_semantics=("arbitrary",) * n_grid_dims,
        vmem_limit_bytes=VMEM_LIMIT_BYTES,
        flags=flags,
    )


def _resident(block_shape, index_map):
    return pl.BlockSpec(block_shape, index_map, pipeline_mode=pl.Buffered(1))


def _rmsnorm_rows(x, g):
    return x * lax.rsqrt(jnp.mean(x * x, axis=-1, keepdims=True) + EPS) * g


def _dot(a, b):
    return jnp.dot(a, b, preferred_element_type=F32)


def _dot_nt(a, b):
    return lax.dot_general(a, b, (((1,), (1,)), ((), ())), preferred_element_type=F32)


def _memkv_kernel(mem_ref, g_ref, w_ref, o_ref):
    hn = _rmsnorm_rows(mem_ref[...], g_ref[...]).astype(BF16)
    o_ref[0] = _dot(hn, w_ref[0]).astype(BF16)


def _memkv(mem2d, g, w_bf16):
    depth, d_model, n = w_bf16.shape
    rows = mem2d.shape[0]
    return pl.pallas_call(
        _memkv_kernel,
        out_shape=jax.ShapeDtypeStruct((depth, rows, n), BF16),
        grid=(depth, rows // ROW_TILE),
        in_specs=[
            pl.BlockSpec((ROW_TILE, d_model), lambda l, i: (i, 0)),
            pl.BlockSpec((1, d_model), lambda l, i: (0, 0)),
            pl.BlockSpec((1, d_model, n), lambda l, i: (l, 0, 0)),
        ],
        out_specs=pl.BlockSpec((1, ROW_TILE, n), lambda l, i: (l, i, 0)),
        compiler_params=_params(2),
        name="memkv",
    )(mem2d, g, w_bf16)


def _inproj_pool_kernel(x_ref, g_ref, w_ref, u_ref, qm_ref, *, tok_width):
    hn = _rmsnorm_rows(x_ref[...], g_ref[...]).astype(BF16)
    chunk = 512
    for c in range(0, tok_width, chunk):
        u_ref[:, c:c + chunk] = _dot(hn, w_ref[:, c:c + chunk])
    qm_ref[...] = _dot(hn, w_ref[:, tok_width:]).astype(BF16)


def _inproj_pool(x2d, g, w_bf16, tok_width):
    t, d_model = x2d.shape
    n = w_bf16.shape[1]
    return pl.pallas_call(
        functools.partial(_inproj_pool_kernel, tok_width=tok_width),
        out_shape=(jax.ShapeDtypeStruct((t, tok_width), F32),
                   jax.ShapeDtypeStruct((t, n - tok_width), BF16)),
        grid=(t // ROW_TILE,),
        in_specs=[
            pl.BlockSpec((ROW_TILE, d_model), lambda i: (i, 0)),
            pl.BlockSpec((1, d_model), lambda i: (0, 0)),
            _resident((d_model, n), lambda i: (0, 0)),
        ],
        out_specs=(pl.BlockSpec((ROW_TILE, tok_width), lambda i: (i, 0)),
                   pl.BlockSpec((ROW_TILE, n - tok_width), lambda i: (i, 0))),
        compiler_params=_params(1),
        name="inproj_pool",
    )(x2d, g, w_bf16)


def _pool_group(u_ref, gw_ref, sc_ref, o_ref, a_ref, b_ref, *, window, seq):
    gwid = u_ref.shape[2]
    rows = seq + 2 * POOL_PAD
    zeros_pad = jnp.zeros((POOL_PAD, gwid), F32)
    a_ref[0:POOL_PAD, :] = zeros_pad
    b_ref[0:POOL_PAD, :] = zeros_pad
    a_ref[POOL_PAD:POOL_PAD + seq, :] = u_ref[0]
    a_ref[POOL_PAD + seq:rows, :] = zeros_pad
    src, dst = a_ref, b_ref
    shift = 1
    while shift < window:
        dst[POOL_PAD:rows, :] = src[POOL_PAD - shift:rows - shift, :] + src[POOL_PAD:rows, :]
        src, dst = dst, src
        shift *= 2
    off = POOL_PAD + window // 2 - 1
    win = src[off:off + seq, :]
    t = lax.broadcasted_iota(I32, (seq, 1), 0)
    lo = jnp.maximum(t - window // 2, 0)
    hi = jnp.minimum(t + window // 2 - 1, seq - 1)
    cnt = (hi - lo + 1).astype(F32)
    pooled = (win / cnt - u_ref[0]).astype(BF16)
    o_ref[0] = (_dot(pooled, gw_ref[0]) * sc_ref[0]).astype(BF16)


def _pool_kernel(u_ref, gw_ref, sc_ref, o_ref, a_ref, b_ref, *, seq):
    g = pl.program_id(1)
    for k, window in enumerate(POOL_WINDOWS):
        @pl.when(g == k)
        def _():
            _pool_group(u_ref, gw_ref, sc_ref, o_ref, a_ref, b_ref, window=window, seq=seq)


def _pool_mixer(u3d, gw_bf16, scale3d):
    b, seq, tok_width = u3d.shape
    n_groups, gwid, _ = gw_bf16.shape
    assert n_groups == len(POOL_WINDOWS) and n_groups * gwid == tok_width
    return pl.pallas_call(
        functools.partial(_pool_kernel, seq=seq),
        out_shape=jax.ShapeDtypeStruct((b, seq, tok_width), BF16),
        grid=(b, n_groups),
        in_specs=[
            pl.BlockSpec((1, seq, gwid), lambda i, g: (i, 0, g)),
            pl.BlockSpec((1, gwid, gwid), lambda i, g: (g, 0, 0)),
            pl.BlockSpec((1, 1, gwid), lambda i, g: (g, 0, 0)),
        ],
        out_specs=pl.BlockSpec((1, seq, gwid), lambda i, g: (i, 0, g)),
        scratch_shapes=[pltpu.VMEM((seq + 2 * POOL_PAD, gwid), F32),
                        pltpu.VMEM((seq + 2 * POOL_PAD, gwid), F32)],
        compiler_params=_params(2),
        name="pool_mixer",
    )(u3d, gw_bf16, scale3d)


def _xattn_kernel(q_ref, kv_ref, o_ref):
    scale = XA_HEAD_DIM ** -0.5
    for h in range(XA_HEADS):
        lo = h * XA_HEAD_DIM
        q = q_ref[:, lo:lo + XA_HEAD_DIM]
        k = kv_ref[0, :, lo:lo + XA_HEAD_DIM]
        v = kv_ref[0, :, XA_WIDTH + lo:XA_WIDTH + lo + XA_HEAD_DIM]
        s = _dot_nt(q, k) * scale
        m = jnp.max(s, axis=-1, keepdims=True)
        p = jnp.exp(s - m)
        den = jnp.sum(p, axis=-1, keepdims=True)
        o_ref[:, lo:lo + XA_HEAD_DIM] = (_dot(p.astype(BF16), v) / den).astype(BF16)


def _mem_xattn(qm2d, memkv_layer, seq):
    t = qm2d.shape[0]
    tiles_per_seq = seq // ROW_TILE
    return pl.pallas_call(
        _xattn_kernel,
        out_shape=jax.ShapeDtypeStruct((t, XA_WIDTH), BF16),
        grid=(t // ROW_TILE,),
        in_specs=[
            pl.BlockSpec((ROW_TILE, XA_WIDTH), lambda i: (i, 0)),
            pl.BlockSpec((1, MEM_LEN, 2 * XA_WIDTH), lambda i: (i // tiles_per_seq, 0, 0)),
        ],
        out_specs=pl.BlockSpec((ROW_TILE, XA_WIDTH), lambda i: (i, 0)),
        compiler_params=_params(1),
        name="mem_xattn",
    )(qm2d, memkv_layer)


def _outproj_kernel(tok_ref, mo_ref, x_ref, w_ref, g_ref, rw_ref,
                    x1_ref, h_ref, afft_ref, aff_ref, wcat_ref, hi_ref, lo_ref,
                    *, tok_width, n_experts):
    d_model = x_ref.shape[1]

    @pl.when(pl.program_id(0) == 0)
    def _():
        rw = rw_ref[...]
        w_hi = rw.astype(BF16)
        wcat_ref[:, 0:LANES] = w_hi
        wcat_ref[:, LANES:2 * LANES] = (rw - w_hi.astype(F32)).astype(BF16)
        hi_ref[...] = jnp.zeros_like(hi_ref)
        lo_ref[...] = jnp.zeros_like(lo_ref)

    h_prev = hi_ref[...]
    r = _dot(h_prev, wcat_ref[...]) + _dot(lo_ref[...], wcat_ref[...])
    logits = r[:, 0:LANES] + r[:, LANES:2 * LANES]
    lt = logits.T[0:n_experts, :]
    m = jnp.max(lt, axis=0, keepdims=True)
    ex = jnp.exp(lt - m)
    afft = ex / jnp.sum(ex, axis=0, keepdims=True)
    afft_ref[0] = afft
    padded = jnp.concatenate(
        [afft, jnp.zeros((LANES - n_experts, afft.shape[1]), F32)], axis=0)
    aff = padded.T
    aff_ref[...] = aff
    h_ref[...] = _pack_row_words(h_prev, _pack_gate_lanes(aff))

    y = _dot(tok_ref[...], w_ref[0:tok_width, :]) + _dot(mo_ref[...], w_ref[tok_width:, :])
    x1 = x_ref[...] + y
    x1_ref[...] = x1
    hn = _rmsnorm_rows(x1, g_ref[...])
    h_hi = hn.astype(BF16)
    hi_ref[...] = h_hi
    lo_ref[...] = (hn - h_hi.astype(F32)).astype(BF16)


def _outproj(tok2d, mo2d, x2d, w_bf16, g, rw_pad, n_experts, seq):
    t, d_model = x2d.shape
    tok_width = tok2d.shape[1]
    tiles_per_seq = seq // ROW_TILE
    n_tiles = t // ROW_TILE

    def cur(i):
        return jnp.minimum(i, n_tiles - 1)

    def prev(i):
        return jnp.maximum(i - 1, 0)

    return pl.pallas_call(
        functools.partial(_outproj_kernel, tok_width=tok_width, n_experts=n_experts),
        out_shape=(jax.ShapeDtypeStruct((t, d_model), F32),
                   jax.ShapeDtypeStruct((t, d_model // 2 + LANES), U32),
                   jax.ShapeDtypeStruct((t // seq, n_experts, seq), F32),
                   jax.ShapeDtypeStruct((t, LANES), F32)),
        grid=(n_tiles + 1,),
        in_specs=[
            pl.BlockSpec((ROW_TILE, tok_width), lambda i: (cur(i), 0)),
            pl.BlockSpec((ROW_TILE, mo2d.shape[1]), lambda i: (cur(i), 0)),
            pl.BlockSpec((ROW_TILE, d_model), lambda i: (cur(i), 0)),
            _resident((d_model, d_model), lambda i: (0, 0)),
            pl.BlockSpec((1, d_model), lambda i: (0, 0)),
            _resident((d_model, LANES), lambda i: (0, 0)),
        ],
        out_specs=(pl.BlockSpec((ROW_TILE, d_model), lambda i: (cur(i), 0)),
                   pl.BlockSpec((ROW_TILE, d_model // 2 + LANES), lambda i: (prev(i), 0)),
                   pl.BlockSpec((1, n_experts, ROW_TILE),
                                lambda i: (prev(i) // tiles_per_seq, 0, prev(i) % tiles_per_seq)),
                   pl.BlockSpec((ROW_TILE, LANES), lambda i: (prev(i), 0))),
        scratch_shapes=[pltpu.VMEM((d_model, 2 * LANES), BF16),
                        pltpu.VMEM((ROW_TILE, d_model), BF16),
                        pltpu.VMEM((ROW_TILE, d_model), BF16)],
        compiler_params=_params(1),
        name="outproj_router",
    )(tok2d, mo2d, x2d, w_bf16, g, rw_pad)


def _strict_triangle(n, lower):
    r = lax.broadcasted_iota(I32, (n, n), 0)
    c = lax.broadcasted_iota(I32, (n, n), 1)
    return jnp.where((c < r) if lower else (r < c), 1.0, 0.0).astype(BF16)


def _prefix_rows(mask_f32):
    s, l = mask_f32.shape
    tri = _strict_triangle(PREFIX_CHUNK, lower=True)
    carry = jnp.zeros((1, l), F32)
    out = []
    for c in range(0, s, PREFIX_CHUNK):
        m = mask_f32[c:c + PREFIX_CHUNK, :]
        out.append(_dot(tri, m.astype(BF16)) + carry)
        carry = carry + jnp.sum(m, axis=0, keepdims=True)
    return jnp.concatenate(out, axis=0)


def _prefix_lanes(mask_f32):
    e, s = mask_f32.shape
    tri = _strict_triangle(PREFIX_CHUNK, lower=False)
    carry = jnp.zeros((e, 1), F32)
    out = []
    for c in range(0, s, PREFIX_CHUNK):
        m = mask_f32[:, c:c + PREFIX_CHUNK]
        out.append(_dot(m.astype(BF16), tri) + carry)
        carry = carry + jnp.sum(m, axis=1, keepdims=True)
    return jnp.concatenate(out, axis=1)


def _select_slots(key, thr, need, prefix_fn):
    gt = jnp.where(key > thr, 1.0, 0.0)
    eq = jnp.where(key == thr, 1.0, 0.0)
    eq_rank = prefix_fn(eq)
    sel = gt + eq * jnp.where(eq_rank < need, 1.0, 0.0)
    pos = prefix_fn(sel)
    return jnp.where(sel > 0.5, pos, -1.0)


GATE_GROUP = 16
GATE_PIECES = 3


def _pack_gate_lanes(aff):
    hi = aff.astype(BF16).astype(F32)
    r1 = aff - hi
    mid = r1.astype(BF16).astype(F32)
    lo = (r1 - mid).astype(BF16).astype(F32)
    packed = hi + pltpu.roll(mid, GATE_GROUP, 1) + pltpu.roll(lo, 2 * GATE_GROUP, 1)
    return packed.astype(BF16)


def _unpack_gate(tail, e):
    lane = lax.broadcasted_iota(I32, (1, LANES), 1)
    mine = ((lane & (GATE_GROUP - 1)) == e) & (lane < GATE_PIECES * GATE_GROUP)
    return jnp.sum(jnp.where(mine, tail.astype(F32), 0.0), axis=1, keepdims=True)


def _pack_row_words(h, gate_tile):
    rows, d_model = h.shape
    half = d_model // 2
    hi = jnp.concatenate([h[:, 0:half], gate_tile], axis=1).astype(F32)
    lo = jnp.concatenate([h[:, half:], jnp.zeros((rows, LANES), BF16)], axis=1).astype(F32)
    return pltpu.bitcast(hi, U32) | (pltpu.bitcast(lo, U32) >> 16)


def _unpack_row_words(words):
    hi = pltpu.bitcast(words & jnp.uint32(0xFFFF0000), F32)
    lo = pltpu.bitcast(words << 16, F32)
    return hi, lo


def _route_kernel(afft_ref, posm_ref, thr_ref, need_ref, *, cap):
    n_experts = afft_ref.shape[1]
    key = pltpu.bitcast(afft_ref[0], I32)
    thr = jnp.zeros((n_experts, 1), I32)
    for bit in range(30, -1, -1):
        cand = thr | (1 << bit)
        cnt = jnp.sum(jnp.where(key >= cand, 1.0, 0.0), axis=1, keepdims=True)
        thr = jnp.where(cnt >= cap, cand, thr)
    n_gt = jnp.sum(jnp.where(key > thr, 1.0, 0.0), axis=1, keepdims=True)
    need = cap - n_gt
    thr_ref[0] = thr
    need_ref[0] = need
    posm_ref[0] = _select_slots(key, thr, need, _prefix_lanes).astype(I32)


def _route(afft, cap):
    b, n_experts, seq = afft.shape
    return pl.pallas_call(
        functools.partial(_route_kernel, cap=cap),
        out_shape=(jax.ShapeDtypeStruct((b, n_experts, seq), I32),
                   jax.ShapeDtypeStruct((b, n_experts, 1), I32),
                   jax.ShapeDtypeStruct((b, n_experts, 1), F32)),
        grid=(b,),
        in_specs=[pl.BlockSpec((1, n_experts, seq), lambda i: (i, 0, 0))],
        out_specs=(pl.BlockSpec((1, n_experts, seq), lambda i: (i, 0, 0)),
                   pl.BlockSpec((1, n_experts, 1), lambda i: (i, 0, 0)),
                   pl.BlockSpec((1, n_experts, 1), lambda i: (i, 0, 0))),
        compiler_params=_params(1),
        name="expert_route",
    )(afft)


SC_LANES = 16
SC_GATHER_ROWS = 64


def _sc_expert_gather(posm2d, h_words, n_experts, e_offset, n_e, b, seq, cap):
    width = h_words.shape[1]
    info = plsc.get_sparse_core_info()
    n_cores, n_subcores = info.num_cores, info.num_subcores
    n_workers = n_cores * n_subcores
    assert info.num_lanes == SC_LANES and (b * n_e) % n_workers == 0
    pairs_per_worker = (b * n_e) // n_workers
    mesh = plsc.VectorSubcoreMesh(core_axis_name="c", subcore_axis_name="s")

    @functools.partial(
        pl.kernel, mesh=mesh,
        out_type=jax.ShapeDtypeStruct((n_e * b * cap, width), U32),
        compiler_params=pltpu.CompilerParams(needs_layout_passes=False),
        scratch_types=[
            pltpu.VMEM((seq,), I32),
            pltpu.VMEM((cap,), I32),
            pltpu.VMEM((SC_GATHER_ROWS, width), U32),
            pltpu.SemaphoreType.DMA,
        ],
        name="sc_expert_gather",
    )
    def gather(posm_hbm, h_hbm, out_hbm, pos_v, idx_v, rows_v, sem):
        wid = lax.axis_index("s") * n_cores + lax.axis_index("c")
        for p in range(pairs_per_worker):
            pair = wid * pairs_per_worker + p
            bi = pair // n_e
            e = pair - bi * n_e
            pltpu.sync_copy(posm_hbm.at[bi * n_experts + e_offset + e], pos_v)

            @pl.loop(0, seq, step=SC_LANES)
            def _(t0):
                slots = pos_v[pl.ds(t0, SC_LANES)]
                rows = lax.iota(I32, SC_LANES) + (t0 + bi * seq)
                plsc.store_scatter(idx_v, [slots], rows, mask=slots >= 0)

            out_base = (e * b + bi) * cap
            for c in range(cap // SC_GATHER_ROWS):
                chunk = idx_v.at[pl.ds(c * SC_GATHER_ROWS, SC_GATHER_ROWS)]
                pltpu.async_copy(h_hbm.at[chunk], rows_v, sem).wait()
                pltpu.sync_copy(rows_v, out_hbm.at[pl.ds(out_base + c * SC_GATHER_ROWS, SC_GATHER_ROWS)])

    return gather(posm2d, h_words).reshape(n_e, b, cap, width)


EXPERT_ROWS = 1024
EXPERT_FTILE = 256


EXPERT_WBUFS = 2


def _expert_kernel(xg_ref, wg_hbm, wu_hbm, wd_hbm, o_ref,
                   acc_ref, x_ref, g_ref, wg_buf, wu_buf, wd_buf, sem,
                   *, layer, e_offset, n_ftiles):
    e = pl.program_id(0)
    m = pl.program_id(1)
    n_m = pl.num_programs(1)
    step = e * n_m + m
    last_step = pl.num_programs(0) * n_m - 1
    nb, cap, width = xg_ref.shape[1], xg_ref.shape[2], xg_ref.shape[3]
    half = width - LANES
    d_model = 2 * half
    rows = nb * cap
    tf = EXPERT_FTILE
    assert n_ftiles % EXPERT_WBUFS == 0

    def tile_copies(expert, f, slot):
        ge = e_offset + expert
        return (
            pltpu.make_async_copy(wg_hbm.at[layer, ge, :, pl.ds(f * tf, tf)], wg_buf.at[slot], sem.at[0, slot]),
            pltpu.make_async_copy(wu_hbm.at[layer, ge, :, pl.ds(f * tf, tf)], wu_buf.at[slot], sem.at[1, slot]),
            pltpu.make_async_copy(wd_hbm.at[layer, ge, pl.ds(f * tf, tf), :], wd_buf.at[slot], sem.at[2, slot]),
        )

    @pl.when(step == 0)
    def _():
        for cp in tile_copies(e, 0, 0):
            cp.start()

    hi, lo = _unpack_row_words(xg_ref[0].reshape(rows, width))
    x_ref[:, 0:half] = hi[:, 0:half].astype(BF16)
    x_ref[:, half:d_model] = lo[:, 0:half].astype(BF16)
    g_ref[...] = jnp.broadcast_to(_unpack_gate(hi[:, half:width], e_offset + e), (rows, LANES))

    chunk = 512
    for f in range(n_ftiles):
        slot = f % EXPERT_WBUFS
        for cp in tile_copies(e, f, slot):
            cp.wait()
        if f + 1 < n_ftiles:
            for cp in tile_copies(e, f + 1, (f + 1) % EXPERT_WBUFS):
                cp.start()
        else:
            @pl.when(step < last_step)
            def _():
                nxt = jnp.where(m + 1 < n_m, e, e + 1)
                for cp in tile_copies(nxt, 0, 0):
                    cp.start()

        x = x_ref[...]
        a = _dot(x, wg_buf[slot].astype(BF16))
        u = _dot(x, wu_buf[slot].astype(BF16))
        hact = (a * jax.nn.sigmoid(a) * u).astype(BF16)
        wd = wd_buf[slot].astype(BF16)
        for c in range(0, d_model, chunk):
            part = _dot(hact, wd[:, c:c + chunk])
            if f == 0:
                acc_ref[:, c:c + chunk] = part
            elif f + 1 < n_ftiles:
                acc_ref[:, c:c + chunk] += part
            else:
                y = acc_ref[:, c:c + chunk] + part
                for j in range(0, chunk, LANES):
                    o_ref[:, 0, :, c + j:c + j + LANES] = (
                        (y[:, j:j + LANES] * g_ref[...]).astype(BF16).reshape(nb, cap, LANES))


def _experts(xg, w_gate, w_up, w_down, layer, e_offset):
    n_e, b, cap, width = xg.shape
    d_model = 2 * (width - LANES)
    d_expert = w_gate.shape[3]
    nb = EXPERT_ROWS // cap
    n_ftiles = d_expert // EXPERT_FTILE
    return pl.pallas_call(
        functools.partial(_expert_kernel, layer=layer, e_offset=e_offset, n_ftiles=n_ftiles),
        out_shape=jax.ShapeDtypeStruct((b, n_e, cap, d_model), BF16),
        grid=(n_e, b // nb),
        in_specs=[
            pl.BlockSpec((1, nb, cap, width), lambda e, m: (e, m, 0, 0)),
            pl.BlockSpec(memory_space=pl.ANY),
            pl.BlockSpec(memory_space=pl.ANY),
            pl.BlockSpec(memory_space=pl.ANY),
        ],
        out_specs=pl.BlockSpec((nb, 1, cap, d_model), lambda e, m: (m, e, 0, 0)),
        scratch_shapes=[pltpu.VMEM((EXPERT_ROWS, d_model), F32),
                        pltpu.VMEM((EXPERT_ROWS, d_model), BF16),
                        pltpu.VMEM((EXPERT_ROWS, LANES), F32),
                        pltpu.VMEM((EXPERT_WBUFS, d_model, EXPERT_FTILE), F32),
                        pltpu.VMEM((EXPERT_WBUFS, d_model, EXPERT_FTILE), F32),
                        pltpu.VMEM((EXPERT_WBUFS, EXPERT_FTILE, d_model), F32),
                        pltpu.SemaphoreType.DMA((3, EXPERT_WBUFS))],
        compiler_params=_params(2),
        name="experts",
    )(xg, w_gate, w_up, w_down)


COMBINE_NTILE = 512
COMBINE_MCHUNK = 512


def _combine_kernel(aff_ref, thr_ref, need_ref, x1_ref, *rest, cap, n_experts):
    y_refs, o_ref, p_ref = rest[:-2], rest[-2], rest[-1]
    n = pl.program_id(1)
    seq = aff_ref.shape[0]

    @pl.when(n == 0)
    def _():
        key = pltpu.bitcast(aff_ref[...], I32)
        posm = _select_slots(key, thr_ref[0], need_ref[0], _prefix_rows)
        slot = lax.broadcasted_iota(I32, (1, cap), 1).astype(F32)
        for e in range(n_experts):
            col = posm[:, e:e + 1]
            p_ref[:, e * cap:(e + 1) * cap] = jnp.where(col == slot, 1.0, 0.0).astype(BF16)

    for r in range(0, seq, COMBINE_MCHUNK):
        acc = x1_ref[0, r:r + COMBINE_MCHUNK, :]
        k0 = 0
        for y_ref in y_refs:
            k1 = k0 + y_ref.shape[1]
            acc = acc + _dot(p_ref[r:r + COMBINE_MCHUNK, k0:k1], y_ref[0])
            k0 = k1
        o_ref[0, r:r + COMBINE_MCHUNK, :] = acc


def _combine(aff2d, thr_row, need_row, x1_3d, y_groups, cap, n_experts):
    b, seq, d_model = x1_3d.shape
    assert sum(y.shape[1] for y in y_groups) == n_experts * cap
    return pl.pallas_call(
        functools.partial(_combine_kernel, cap=cap, n_experts=n_experts),
        out_shape=jax.ShapeDtypeStruct((b, seq, d_model), F32),
        grid=(b, d_model // COMBINE_NTILE),
        in_specs=[
            pl.BlockSpec((seq, LANES), lambda i, n: (i, 0)),
            pl.BlockSpec((1, 1, LANES), lambda i, n: (i, 0, 0)),
            pl.BlockSpec((1, 1, LANES), lambda i, n: (i, 0, 0)),
            pl.BlockSpec((1, seq, COMBINE_NTILE), lambda i, n: (i, 0, n)),
        ] + [pl.BlockSpec((1, y.shape[1], COMBINE_NTILE), lambda i, n: (i, 0, n)) for y in y_groups],
        out_specs=pl.BlockSpec((1, seq, COMBINE_NTILE), lambda i, n: (i, 0, n)),
        scratch_shapes=[pltpu.VMEM((seq, n_experts * cap), BF16)],
        compiler_params=_params(2),
        name="combine",
    )(aff2d, thr_row, need_row, x1_3d, *y_groups)


EXPERT_GROUPS = 2


def _moe(afft, aff2d, h_words, x1_2d, w_gate, w_up, w_down, layer, b, seq):
    n_experts = afft.shape[1]
    assert n_experts <= GATE_GROUP and n_experts % EXPERT_GROUPS == 0
    d_model = x1_2d.shape[1]
    cap = CAPACITY_FACTOR * seq // n_experts
    n_e = n_experts // EXPERT_GROUPS
    posm, thr, need = _route(afft, cap)
    posm2d = posm.reshape(b * n_experts, seq)
    xgs = [_sc_expert_gather(posm2d, h_words, n_experts, g * n_e, n_e, b, seq, cap)
           for g in range(EXPERT_GROUPS)]
    ys = [_experts(xg, w_gate, w_up, w_down, layer, g * n_e).reshape(b, n_e * cap, d_model)
          for g, xg in enumerate(xgs)]
    pad = LANES - n_experts
    thr_row = jnp.pad(thr.reshape(b, 1, n_experts), ((0, 0), (0, 0), (0, pad)),
                      constant_values=np.iinfo(np.int32).max)
    need_row = jnp.pad(need.reshape(b, 1, n_experts), ((0, 0), (0, 0), (0, pad)))
    return _combine(aff2d, thr_row, need_row, x1_2d.reshape(b, seq, d_model), ys, cap, n_experts)


def _rotary_tile(t, cos, sin_lo, sin_hi):
    half = ROT_DIM // 2
    return t * cos + pltpu.roll(t, LANES - half, 1) * sin_lo + pltpu.roll(t, half, 1) * sin_hi


def _inproj_attn_kernel(x_ref, g_ref, w_ref, pos_ref, rot_ref,
                        q_ref, k_ref, v_ref, qm_ref, *, tok_width, kv_width):
    hn = _rmsnorm_rows(x_ref[...], g_ref[...]).astype(BF16)
    ang = pos_ref[...].astype(F32) * rot_ref[0:1, :]
    cos = jnp.cos(ang)
    sin = jnp.sin(ang)
    sin_lo = sin * rot_ref[1:2, :]
    sin_hi = sin * rot_ref[2:3, :]
    qscale = HEAD_DIM ** -0.5
    chunk = 512
    for c in range(0, tok_width, chunk):
        pc = _dot(hn, w_ref[:, c:c + chunk])
        for j in range(0, chunk, LANES):
            rot = _rotary_tile(pc[:, j:j + LANES], cos, sin_lo, sin_hi)
            q_ref[:, c + j:c + j + LANES] = (rot * qscale).astype(BF16)
    kv = _dot(hn, w_ref[:, tok_width:tok_width + 2 * kv_width])
    k01 = _rotary_tile(kv[:, 0:LANES], cos, sin_lo, sin_hi)
    k2x = _rotary_tile(kv[:, LANES:2 * LANES], cos, sin_lo, sin_hi)
    k_ref[0, 0] = k01[:, 0:HEAD_DIM].astype(BF16)
    k_ref[0, 1] = k01[:, HEAD_DIM:LANES].astype(BF16)
    k_ref[0, 2] = k2x[:, 0:HEAD_DIM].astype(BF16)
    for hh in range(kv_width // HEAD_DIM):
        lo = kv_width + hh * HEAD_DIM
        v_ref[0, hh] = kv[:, lo:lo + HEAD_DIM].astype(BF16)
    qm_ref[...] = _dot(hn, w_ref[:, tok_width + 2 * kv_width:]).astype(BF16)


def _inproj_attn(x2d, g, w_bf16, pos2d, rot_rows, tok_width, kv_width, seq):
    t, d_model = x2d.shape
    n = w_bf16.shape[1]
    n_kv = kv_width // HEAD_DIM
    assert n_kv == 3 and kv_width + HEAD_DIM == 2 * LANES
    tiles_per_seq = seq // ROW_TILE
    kv_spec = pl.BlockSpec((1, n_kv, ROW_TILE, HEAD_DIM),
                           lambda i: (i // tiles_per_seq, 0, i % tiles_per_seq, 0))
    return pl.pallas_call(
        functools.partial(_inproj_attn_kernel, tok_width=tok_width, kv_width=kv_width),
        out_shape=(jax.ShapeDtypeStruct((t, tok_width), BF16),
                   jax.ShapeDtypeStruct((t // seq, n_kv, seq, HEAD_DIM), BF16),
                   jax.ShapeDtypeStruct((t // seq, n_kv, seq, HEAD_DIM), BF16),
                   jax.ShapeDtypeStruct((t, n - tok_width - 2 * kv_width), BF16)),
        grid=(t // ROW_TILE,),
        in_specs=[
            pl.BlockSpec((ROW_TILE, d_model), lambda i: (i, 0)),
            pl.BlockSpec((1, d_model), lambda i: (0, 0)),
            _resident((d_model, n), lambda i: (0, 0)),
            pl.BlockSpec((ROW_TILE, 1), lambda i: (i, 0)),
            pl.BlockSpec((8, LANES), lambda i: (0, 0)),
        ],
        out_specs=(pl.BlockSpec((ROW_TILE, tok_width), lambda i: (i, 0)),
                   kv_spec, kv_spec,
                   pl.BlockSpec((ROW_TILE, n - tok_width - 2 * kv_width), lambda i: (i, 0))),
        compiler_params=_params(1),
        name="inproj_attn",
    )(x2d, g, w_bf16, pos2d, rot_rows)


def _wattn_kernel(sink_ref, q_ref, kp_ref, kc_ref, kn_ref, vp_ref, vc_ref, vn_ref, o_ref,
                  valid_ref, kpad_ref, vpad_ref, s_ref, p_ref, inv_ref, *, seq):
    n = pl.program_id(1)
    n_kv = kc_ref.shape[1]
    pairs = GQA_RATIO // 2
    half_rows = pairs * BLOCK
    qi = lax.broadcasted_iota(I32, (BLOCK, 3 * BLOCK), 0)
    kj = lax.broadcasted_iota(I32, (BLOCK, 3 * BLOCK), 1)
    first = jnp.maximum(qi, BLOCK - n * BLOCK)
    last = jnp.minimum(qi + 2 * WINDOW, seq + BLOCK - 1 - n * BLOCK)
    valid_ref[...] = jnp.where(((kj - first) | (last - kj)) >= 0, 1.0, 0.0)
    zeros = jnp.zeros((3 * BLOCK, HEAD_DIM), BF16)
    for hk in range(n_kv):
        kw = jnp.concatenate([kp_ref[0, hk], kc_ref[0, hk], kn_ref[0, hk]], axis=0)
        vw = jnp.concatenate([vp_ref[0, hk], vc_ref[0, hk], vn_ref[0, hk]], axis=0)
        kpad_ref[2 * hk] = jnp.concatenate([kw, zeros], axis=1)
        kpad_ref[2 * hk + 1] = jnp.concatenate([zeros, kw], axis=1)
        vpad_ref[2 * hk] = jnp.concatenate([vw, zeros], axis=1)
        vpad_ref[2 * hk + 1] = jnp.concatenate([zeros, vw], axis=1)
        tile0 = hk * pairs
        qs = jnp.concatenate(
            [q_ref[0, :, (tile0 + j) * LANES:(tile0 + j + 1) * LANES] for j in range(pairs)], axis=0)
        s_ref[hk, 0:half_rows, :] = _dot_nt(qs, kpad_ref[2 * hk])
        s_ref[hk, half_rows:2 * half_rows, :] = _dot_nt(qs, kpad_ref[2 * hk + 1])
    for hk in range(n_kv):
        for c in range(GQA_RATIO):
            j, odd = c % pairs, c // pairs
            r = c * BLOCK
            s = jnp.where(valid_ref[...] > 0.5, s_ref[hk, r:r + BLOCK, :], NEG_INF)
            sk = sink_ref[hk * GQA_RATIO + 2 * j + odd]
            m = jnp.maximum(jnp.max(s, axis=-1, keepdims=True), sk)
            p = jnp.exp(s - m)
            inv = 1.0 / (jnp.sum(p, axis=-1, keepdims=True) + jnp.exp(sk - m))
            p_ref[hk, r:r + BLOCK, :] = p.astype(BF16)
            inv_ref[hk, j * BLOCK:(j + 1) * BLOCK, odd * HEAD_DIM:(odd + 1) * HEAD_DIM] = (
                jnp.broadcast_to(inv, (BLOCK, HEAD_DIM)))
    for hk in range(n_kv):
        o = (_dot(p_ref[hk, 0:half_rows, :], vpad_ref[2 * hk])
             + _dot(p_ref[hk, half_rows:2 * half_rows, :], vpad_ref[2 * hk + 1])) * inv_ref[hk]
        for j in range(pairs):
            lo = (hk * pairs + j) * LANES
            o_ref[0, :, lo:lo + LANES] = o[j * BLOCK:(j + 1) * BLOCK].astype(BF16)


def _window_attention(sink, q3d, k4d, v4d):
    b, seq, tok_width = q3d.shape
    n_kv = k4d.shape[1]
    nb = seq // BLOCK
    kv_block = (1, n_kv, BLOCK, HEAD_DIM)
    prev_spec = pl.BlockSpec(kv_block, lambda i, n: (i, 0, jnp.maximum(n - 1, 0), 0))
    cur_spec = pl.BlockSpec(kv_block, lambda i, n: (i, 0, n, 0))
    next_spec = pl.BlockSpec(kv_block, lambda i, n: (i, 0, jnp.minimum(n + 1, nb - 1), 0))
    return pl.pallas_call(
        functools.partial(_wattn_kernel, seq=seq),
        out_shape=jax.ShapeDtypeStruct((b, seq, tok_width), BF16),
        grid=(b, nb),
        in_specs=[
            pl.BlockSpec(memory_space=pltpu.SMEM),
            pl.BlockSpec((1, BLOCK, tok_width), lambda i, n: (i, n, 0)),
            prev_spec, cur_spec, next_spec, prev_spec, cur_spec, next_spec,
        ],
        out_specs=pl.BlockSpec((1, BLOCK, tok_width), lambda i, n: (i, n, 0)),
        scratch_shapes=[pltpu.VMEM((BLOCK, 3 * BLOCK), F32),
                        pltpu.VMEM((2 * n_kv, 3 * BLOCK, LANES), BF16),
                        pltpu.VMEM((2 * n_kv, 3 * BLOCK, LANES), BF16),
                        pltpu.VMEM((n_kv, GQA_RATIO * BLOCK, 3 * BLOCK), F32),
                        pltpu.VMEM((n_kv, GQA_RATIO * BLOCK, 3 * BLOCK), BF16),
                        pltpu.VMEM((n_kv, GQA_RATIO // 2 * BLOCK, LANES), F32)],
        compiler_params=_params(2),
        name="window_attention",
    )(sink, q3d, k4d, k4d, k4d, v4d, v4d, v4d)


def _final_norm_kernel(x_ref, g_ref, o_ref):
    o_ref[...] = _rmsnorm_rows(x_ref[...], g_ref[...])


def _final_norm(x2d, g):
    t, d_model = x2d.shape
    return pl.pallas_call(
        _final_norm_kernel,
        out_shape=jax.ShapeDtypeStruct((t, d_model), F32),
        grid=(t // ROW_TILE,),
        in_specs=[pl.BlockSpec((ROW_TILE, d_model), lambda i: (i, 0)),
                  pl.BlockSpec((1, d_model), lambda i: (0, 0))],
        out_specs=pl.BlockSpec((ROW_TILE, d_model), lambda i: (i, 0)),
        compiler_params=_params(1),
        name="final_norm",
    )(x2d, g)


def _rotary_rows(dtype=F32):
    half = ROT_DIM // 2
    inv_freq = ROPE_THETA ** (-jnp.arange(0, ROT_DIM, 2, dtype=jnp.float32) / ROT_DIM)
    lane = np.arange(LANES) % HEAD_DIM
    rotated = lane < ROT_DIM
    freq = jnp.where(jnp.asarray(rotated), inv_freq[jnp.asarray(lane % half)], 0.0)
    rows = jnp.zeros((8, LANES), dtype)
    rows = rows.at[0].set(freq)
    rows = rows.at[1].set(jnp.asarray(np.where(lane < half, -1.0, 0.0), dtype))
    rows = rows.at[2].set(jnp.asarray(np.where(rotated & (lane >= half), 1.0, 0.0), dtype))
    return rows


def kernel(x, mem, positions, norm_mix_g, norm_ffn_g, mem_norm_g, final_g, mem_w_kv,
           pool_w_in, pool_group_w, pool_scale, pool_w_out,
           attn_w_in, attn_sink, attn_w_out,
           router_w, exp_w_gate, exp_w_up, exp_w_down):
    b, seq, d_model = x.shape
    depth = norm_mix_g.shape[0]
    t = b * seq
    n_experts = router_w.shape[2]
    tok_width = pool_scale.shape[1]
    n_groups = pool_group_w.shape[1]
    kv_width = (attn_w_in.shape[2] - tok_width - XA_WIDTH) // 2
    assert seq % ROW_TILE == 0 and mem.shape[1] == MEM_LEN

    memkv = _memkv(mem.reshape(b * MEM_LEN, d_model), mem_norm_g.reshape(1, d_model),
                   mem_w_kv.astype(BF16))
    memkv = memkv.reshape(depth, b, MEM_LEN, 2 * XA_WIDTH)
    rw_pad = jnp.pad(router_w, ((0, 0), (0, 0), (0, LANES - n_experts)))
    pos2d = positions.reshape(t, 1)
    rot_rows = _rotary_rows()

    x2d = x.reshape(t, d_model)
    for layer in range(depth):
        j = layer // 2
        g_mix = norm_mix_g[layer].reshape(1, d_model)
        if layer % 2 == 0:
            u, qm = _inproj_pool(x2d, g_mix, pool_w_in[j].astype(BF16), tok_width)
            tok = _pool_mixer(u.reshape(b, seq, tok_width), pool_group_w[j].astype(BF16),
                              pool_scale[j].reshape(n_groups, 1, tok_width // n_groups))
            tok = tok.reshape(t, tok_width)
            w_out = pool_w_out[j]
        else:
            q, k, v, qm = _inproj_attn(x2d, g_mix, attn_w_in[j].astype(BF16), pos2d, rot_rows,
                                       tok_width, kv_width, seq)
            tok = _window_attention(attn_sink[j], q.reshape(b, seq, tok_width), k, v)
            tok = tok.reshape(t, tok_width)
            w_out = attn_w_out[j]
        mo = _mem_xattn(qm, memkv[layer], seq)
        x1, h, afft, aff = _outproj(tok, mo, x2d, w_out.astype(BF16),
                                    norm_ffn_g[layer].reshape(1, d_model), rw_pad[layer],
                                    n_experts, seq)
        x2 = _moe(afft, aff, h, x1, exp_w_gate, exp_w_up, exp_w_down, layer, b, seq)
        x2d = x2.reshape(t, d_model)
    return _final_norm(x2d, final_g.reshape(1, d_model)).reshape(b, seq, d_model)
```

```python
import functools

import jax
import jax.numpy as jnp
import numpy as np
from jax import lax
from jax.experimental import pallas as pl
from jax.experimental.pallas import tpu as pltpu
from jax.experimental.pallas import tpu_sc as plsc

F32 = jnp.float32
BF16 = jnp.bfloat16
I32 = jnp.int32
U32 = jnp.uint32

EPS = 1e-6
MEM_LEN = 256
XA_HEADS = 4
XA_HEAD_DIM = 128
XA_WIDTH = XA_HEADS * XA_HEAD_DIM
POOL_WINDOWS = (2, 4, 8, 16)
HEAD_DIM = 64
GQA_RATIO = 8
WINDOW = 128
BLOCK = 128
ROPE_THETA = 500000.0
ROT_DIM = 16
NEG_INF = -1e30
CAPACITY_FACTOR = 2

LANES = 128
MIB = 1024 * 1024
VMEM_LIMIT_BYTES = 56 * MIB

ROW_TILE = 512
POOL_PAD = 16
PREFIX_CHUNK = 256


def _params(n_grid_dims, flags=None):
    return pltpu.CompilerParams(
        dimension_semantics=("arbitrary",) * n_grid_dims,
        vmem_limit_bytes=VMEM_LIMIT_BYTES,
        flags=flags,
    )


def _resident(block_shape, index_map):
    return pl.BlockSpec(block_shape, index_map, pipeline_mode=pl.Buffered(1))


def _rmsnorm_rows(x, g):
    return x * lax.rsqrt(jnp.mean(x * x, axis=-1, keepdims=True) + EPS) * g


def _dot(a, b):
    return jnp.dot(a, b, preferred_element_type=F32)


def _dot_nt(a, b):
    return lax.dot_general(a, b, (((1,), (1,)), ((), ())), preferred_element_type=F32)


def _memkv_kernel(mem_ref, g_ref, w_ref, o_ref):
    hn = _rmsnorm_rows(mem_ref[...], g_ref[...]).astype(BF16)
    o_ref[0] = _dot(hn, w_ref[0]).astype(BF16)


def _memkv(mem2d, g, w_bf16):
    depth, d_model, n = w_bf16.shape
    rows = mem2d.shape[0]
    return pl.pallas_call(
        _memkv_kernel,
        out_shape=jax.ShapeDtypeStruct((depth, rows, n), BF16),
        grid=(depth, rows // ROW_TILE),
        in_specs=[
            pl.BlockSpec((ROW_TILE, d_model), lambda l, i: (i, 0)),
            pl.BlockSpec((1, d_model), lambda l, i: (0, 0)),
            pl.BlockSpec((1, d_model, n), lambda l, i: (l, 0, 0)),
        ],
        out_specs=pl.BlockSpec((1, ROW_TILE, n), lambda l, i: (l, i, 0)),
        compiler_params=_params(2),
        name="memkv",
    )(mem2d, g, w_bf16)


def _inproj_pool_kernel(x_ref, g_ref, w_ref, u_ref, qm_ref, *, tok_width):
    hn = _rmsnorm_rows(x_ref[...], g_ref[...]).astype(BF16)
    chunk = 512
    for c in range(0, tok_width, chunk):
        u_ref[:, c:c + chunk] = _dot(hn, w_ref[:, c:c + chunk])
    qm_ref[...] = _dot(hn, w_ref[:, tok_width:]).astype(BF16)


def _inproj_pool(x2d, g, w_bf16, tok_width):
    t, d_model = x2d.shape
    n = w_bf16.shape[1]
    return pl.pallas_call(
        functools.partial(_inproj_pool_kernel, tok_width=tok_width),
        out_shape=(jax.ShapeDtypeStruct((t, tok_width), F32),
                   jax.ShapeDtypeStruct((t, n - tok_width), BF16)),
        grid=(t // ROW_TILE,),
        in_specs=[
            pl.BlockSpec((ROW_TILE, d_model), lambda i: (i, 0)),
            pl.BlockSpec((1, d_model), lambda i: (0, 0)),
            _resident((d_model, n), lambda i: (0, 0)),
        ],
        out_specs=(pl.BlockSpec((ROW_TILE, tok_width), lambda i: (i, 0)),
                   pl.BlockSpec((ROW_TILE, n - tok_width), lambda i: (i, 0))),
        compiler_params=_params(1),
        name="inproj_pool",
    )(x2d, g, w_bf16)


def _pool_group(u_ref, gw_ref, sc_ref, o_ref, a_ref, b_ref, *, window, seq):
    gwid = u_ref.shape[2]
    rows = seq + 2 * POOL_PAD
    zeros_pad = jnp.zeros((POOL_PAD, gwid), F32)
    a_ref[0:POOL_PAD, :] = zeros_pad
    b_ref[0:POOL_PAD, :] = zeros_pad
    a_ref[POOL_PAD:POOL_PAD + seq, :] = u_ref[0]
    a_ref[POOL_PAD + seq:rows, :] = zeros_pad
    src, dst = a_ref, b_ref
    shift = 1
    while shift < window:
        dst[POOL_PAD:rows, :] = src[POOL_PAD - shift:rows - shift, :] + src[POOL_PAD:rows, :]
        src, dst = dst, src
        shift *= 2
    off = POOL_PAD + window // 2 - 1
    win = src[off:off + seq, :]
    t = lax.broadcasted_iota(I32, (seq, 1), 0)
    lo = jnp.maximum(t - window // 2, 0)
    hi = jnp.minimum(t + window // 2 - 1, seq - 1)
    cnt = (hi - lo + 1).astype(F32)
    pooled = (win / cnt - u_ref[0]).astype(BF16)
    o_ref[0] = (_dot(pooled, gw_ref[0]) * sc_ref[0]).astype(BF16)


def _pool_kernel(u_ref, gw_ref, sc_ref, o_ref, a_ref, b_ref, *, seq):
    g = pl.program_id(1)
    for k, window in enumerate(POOL_WINDOWS):
        @pl.when(g == k)
        def _():
            _pool_group(u_ref, gw_ref, sc_ref, o_ref, a_ref, b_ref, window=window, seq=seq)


def _pool_mixer(u3d, gw_bf16, scale3d):
    b, seq, tok_width = u3d.shape
    n_groups, gwid, _ = gw_bf16.shape
    assert n_groups == len(POOL_WINDOWS) and n_groups * gwid == tok_width
    return pl.pallas_call(
        functools.partial(_pool_kernel, seq=seq),
        out_shape=jax.ShapeDtypeStruct((b, seq, tok_width), BF16),
        grid=(b, n_groups),
        in_specs=[
            pl.BlockSpec((1, seq, gwid), lambda i, g: (i, 0, g)),
            pl.BlockSpec((1, gwid, gwid), lambda i, g: (g, 0, 0)),
            pl.BlockSpec((1, 1, gwid), lambda i, g: (g, 0, 0)),
        ],
        out_specs=pl.BlockSpec((1, seq, gwid), lambda i, g: (i, 0, g)),
        scratch_shapes=[pltpu.VMEM((seq + 2 * POOL_PAD, gwid), F32),
                        pltpu.VMEM((seq + 2 * POOL_PAD, gwid), F32)],
        compiler_params=_params(2),
        name="pool_mixer",
    )(u3d, gw_bf16, scale3d)


def _xattn_kernel(q_ref, kv_ref, o_ref):
    scale = XA_HEAD_DIM ** -0.5
    for h in range(XA_HEADS):
        lo = h * XA_HEAD_DIM
        q = q_ref[:, lo:lo + XA_HEAD_DIM]
        k = kv_ref[0, :, lo:lo + XA_HEAD_DIM]
        v = kv_ref[0, :, XA_WIDTH + lo:XA_WIDTH + lo + XA_HEAD_DIM]
        s = _dot_nt(q, k) * scale
        m = jnp.max(s, axis=-1, keepdims=True)
        p = jnp.exp(s - m)
        den = jnp.sum(p, axis=-1, keepdims=True)
        o_ref[:, lo:lo + XA_HEAD_DIM] = (_dot(p.astype(BF16), v) / den).astype(BF16)


def _mem_xattn(qm2d, memkv_layer, seq):
    t = qm2d.shape[0]
    tiles_per_seq = seq // ROW_TILE
    return pl.pallas_call(
        _xattn_kernel,
        out_shape=jax.ShapeDtypeStruct((t, XA_WIDTH), BF16),
        grid=(t // ROW_TILE,),
        in_specs=[
            pl.BlockSpec((ROW_TILE, XA_WIDTH), lambda i: (i, 0)),
            pl.BlockSpec((1, MEM_LEN, 2 * XA_WIDTH), lambda i: (i // tiles_per_seq, 0, 0)),
        ],
        out_specs=pl.BlockSpec((ROW_TILE, XA_WIDTH), lambda i: (i, 0)),
        compiler_params=_params(1),
        name="mem_xattn",
    )(qm2d, memkv_layer)


def _outproj_kernel(tok_ref, mo_ref, x_ref, w_ref, g_ref, rw_ref,
                    x1_ref, h_ref, afft_ref, aff_ref, wcat_ref, hi_ref, lo_ref,
                    *, tok_width, n_experts):
    d_model = x_ref.shape[1]

    @pl.when(pl.program_id(0) == 0)
    def _():
        rw = rw_ref[...]
        w_hi = rw.astype(BF16)
        wcat_ref[:, 0:LANES] = w_hi
        wcat_ref[:, LANES:2 * LANES] = (rw - w_hi.astype(F32)).astype(BF16)
        hi_ref[...] = jnp.zeros_like(hi_ref)
        lo_ref[...] = jnp.zeros_like(lo_ref)

    h_prev = hi_ref[...]
    r = _dot(h_prev, wcat_ref[...]) + _dot(lo_ref[...], wcat_ref[...])
    logits = r[:, 0:LANES] + r[:, LANES:2 * LANES]
    lt = logits.T[0:n_experts, :]
    m = jnp.max(lt, axis=0, keepdims=True)
    ex = jnp.exp(lt - m)
    afft = ex / jnp.sum(ex, axis=0, keepdims=True)
    afft_ref[0] = afft
    padded = jnp.concatenate(
        [afft, jnp.zeros((LANES - n_experts, afft.shape[1]), F32)], axis=0)
    aff = padded.T
    aff_ref[...] = aff
    h_ref[...] = _pack_row_words(h_prev, _pack_gate_lanes(aff))

    y = _dot(tok_ref[...], w_ref[0:tok_width, :]) + _dot(mo_ref[...], w_ref[tok_width:, :])
    x1 = x_ref[...] + y
    x1_ref[...] = x1
    hn = _rmsnorm_rows(x1, g_ref[...])
    h_hi = hn.astype(BF16)
    hi_ref[...] = h_hi
    lo_ref[...] = (hn - h_hi.astype(F32)).astype(BF16)


def _outproj(tok2d, mo2d, x2d, w_bf16, g, rw_pad, n_experts, seq):
    t, d_model = x2d.shape
    tok_width = tok2d.shape[1]
    tiles_per_seq = seq // ROW_TILE
    n_tiles = t // ROW_TILE

    def cur(i):
        return jnp.minimum(i, n_tiles - 1)

    def prev(i):
        return jnp.maximum(i - 1, 0)

    return pl.pallas_call(
        functools.partial(_outproj_kernel, tok_width=tok_width, n_experts=n_experts),
        out_shape=(jax.ShapeDtypeStruct((t, d_model), F32),
                   jax.ShapeDtypeStruct((t, d_model // 2 + LANES), U32),
                   jax.ShapeDtypeStruct((t // seq, n_experts, seq), F32),
                   jax.ShapeDtypeStruct((t, LANES), F32)),
        grid=(n_tiles + 1,),
        in_specs=[
            pl.BlockSpec((ROW_TILE, tok_width), lambda i: (cur(i), 0)),
            pl.BlockSpec((ROW_TILE, mo2d.shape[1]), lambda i: (cur(i), 0)),
            pl.BlockSpec((ROW_TILE, d_model), lambda i: (cur(i), 0)),
            _resident((d_model, d_model), lambda i: (0, 0)),
            pl.BlockSpec((1, d_model), lambda i: (0, 0)),
            _resident((d_model, LANES), lambda i: (0, 0)),
        ],
        out_specs=(pl.BlockSpec((ROW_TILE, d_model), lambda i: (cur(i), 0)),
                   pl.BlockSpec((ROW_TILE, d_model // 2 + LANES), lambda i: (prev(i), 0)),
                   pl.BlockSpec((1, n_experts, ROW_TILE),
                                lambda i: (prev(i) // tiles_per_seq, 0, prev(i) % tiles_per_seq)),
                   pl.BlockSpec((ROW_TILE, LANES), lambda i: (prev(i), 0))),
        scratch_shapes=[pltpu.VMEM((d_model, 2 * LANES), BF16),
                        pltpu.VMEM((ROW_TILE, d_model), BF16),
                        pltpu.VMEM((ROW_TILE, d_model), BF16)],
        compiler_params=_params(1),
        name="outproj_router",
    )(tok2d, mo2d, x2d, w_bf16, g, rw_pad)


def _strict_triangle(n, lower):
    r = lax.broadcasted_iota(I32, (n, n), 0)
    c = lax.broadcasted_iota(I32, (n, n), 1)
    return jnp.where((c < r) if lower else (r < c), 1.0, 0.0).astype(BF16)


def _prefix_rows(mask_f32):
    s, l = mask_f32.shape
    tri = _strict_triangle(PREFIX_CHUNK, lower=True)
    carry = jnp.zeros((1, l), F32)
    out = []
    for c in range(0, s, PREFIX_CHUNK):
        m = mask_f32[c:c + PREFIX_CHUNK, :]
        out.append(_dot(tri, m.astype(BF16)) + carry)
        carry = carry + jnp.sum(m, axis=0, keepdims=True)
    return jnp.concatenate(out, axis=0)


def _prefix_lanes(mask_f32):
    e, s = mask_f32.shape
    tri = _strict_triangle(PREFIX_CHUNK, lower=False)
    carry = jnp.zeros((e, 1), F32)
    out = []
    for c in range(0, s, PREFIX_CHUNK):
        m = mask_f32[:, c:c + PREFIX_CHUNK]
        out.append(_dot(m.astype(BF16), tri) + carry)
        carry = carry + jnp.sum(m, axis=1, keepdims=True)
    return jnp.concatenate(out, axis=1)


def _select_slots(key, thr, need, prefix_fn):
    gt = jnp.where(key > thr, 1.0, 0.0)
    eq = jnp.where(key == thr, 1.0, 0.0)
    eq_rank = prefix_fn(eq)
    sel = gt + eq * jnp.where(eq_rank < need, 1.0, 0.0)
    pos = prefix_fn(sel)
    return jnp.where(sel > 0.5, pos, -1.0)


GATE_GROUP = 16
GATE_PIECES = 3


def _pack_gate_lanes(aff):
    hi = aff.astype(BF16).astype(F32)
    r1 = aff - hi
    mid = r1.astype(BF16).astype(F32)
    lo = (r1 - mid).astype(BF16).astype(F32)
    packed = hi + pltpu.roll(mid, GATE_GROUP, 1) + pltpu.roll(lo, 2 * GATE_GROUP, 1)
    return packed.astype(BF16)


def _unpack_gate(tail, e):
    lane = lax.broadcasted_iota(I32, (1, LANES), 1)
    mine = ((lane & (GATE_GROUP - 1)) == e) & (lane < GATE_PIECES * GATE_GROUP)
    return jnp.sum(jnp.where(mine, tail.astype(F32), 0.0), axis=1, keepdims=True)


def _pack_row_words(h, gate_tile):
    rows, d_model = h.shape
    half = d_model // 2
    hi = jnp.concatenate([h[:, 0:half], gate_tile], axis=1).astype(F32)
    lo = jnp.concatenate([h[:, half:], jnp.zeros((rows, LANES), BF16)], axis=1).astype(F32)
    return pltpu.bitcast(hi, U32) | (pltpu.bitcast(lo, U32) >> 16)


def _unpack_row_words(words):
    hi = pltpu.bitcast(words & jnp.uint32(0xFFFF0000), F32)
    lo = pltpu.bitcast(words << 16, F32)
    return hi, lo


def _route_kernel(afft_ref, posm_ref, thr_ref, need_ref, *, cap):
    n_experts = afft_ref.shape[1]
    key = pltpu.bitcast(afft_ref[0], I32)
    thr = jnp.zeros((n_experts, 1), I32)
    for bit in range(30, -1, -1):
        cand = thr | (1 << bit)
        cnt = jnp.sum(jnp.where(key >= cand, 1.0, 0.0), axis=1, keepdims=True)
        thr = jnp.where(cnt >= cap, cand, thr)
    n_gt = jnp.sum(jnp.where(key > thr, 1.0, 0.0), axis=1, keepdims=True)
    need = cap - n_gt
    thr_ref[0] = thr
    need_ref[0] = need
    posm_ref[0] = _select_slots(key, thr, need, _prefix_lanes).astype(I32)


def _route(afft, cap):
    b, n_experts, seq = afft.shape
    return pl.pallas_call(
        functools.partial(_route_kernel, cap=cap),
        out_shape=(jax.ShapeDtypeStruct((b, n_experts, seq), I32),
                   jax.ShapeDtypeStruct((b, n_experts, 1), I32),
                   jax.ShapeDtypeStruct((b, n_experts, 1), F32)),
        grid=(b,),
        in_specs=[pl.BlockSpec((1, n_experts, seq), lambda i: (i, 0, 0))],
        out_specs=(pl.BlockSpec((1, n_experts, seq), lambda i: (i, 0, 0)),
                   pl.BlockSpec((1, n_experts, 1), lambda i: (i, 0, 0)),
                   pl.BlockSpec((1, n_experts, 1), lambda i: (i, 0, 0))),
        compiler_params=_params(1),
        name="expert_route",
    )(afft)


SC_LANES = 16
SC_GATHER_ROWS = 64


def _sc_expert_gather(posm2d, h_words, n_experts, e_offset, n_e, b, seq, cap):
    width = h_words.shape[1]
    info = plsc.get_sparse_core_info()
    n_cores, n_subcores = info.num_cores, info.num_subcores
    n_workers = n_cores * n_subcores
    assert info.num_lanes == SC_LANES and (b * n_e) % n_workers == 0
    pairs_per_worker = (b * n_e) // n_workers
    mesh = plsc.VectorSubcoreMesh(core_axis_name="c", subcore_axis_name="s")

    @functools.partial(
        pl.kernel, mesh=mesh,
        out_type=jax.ShapeDtypeStruct((n_e * b * cap, width), U32),
        compiler_params=pltpu.CompilerParams(needs_layout_passes=False),
        scratch_types=[
            pltpu.VMEM((seq,), I32),
            pltpu.VMEM((cap,), I32),
            pltpu.VMEM((SC_GATHER_ROWS, width), U32),
            pltpu.SemaphoreType.DMA,
        ],
        name="sc_expert_gather",
    )
    def gather(posm_hbm, h_hbm, out_hbm, pos_v, idx_v, rows_v, sem):
        wid = lax.axis_index("s") * n_cores + lax.axis_index("c")
        for p in range(pairs_per_worker):
            pair = wid * pairs_per_worker + p
            bi = pair // n_e
            e = pair - bi * n_e
            pltpu.sync_copy(posm_hbm.at[bi * n_experts + e_offset + e], pos_v)

            @pl.loop(0, seq, step=SC_LANES)
            def _(t0):
                slots = pos_v[pl.ds(t0, SC_LANES)]
                rows = lax.iota(I32, SC_LANES) + (t0 + bi * seq)
                plsc.store_scatter(idx_v, [slots], rows, mask=slots >= 0)

            out_base = (e * b + bi) * cap
            for c in range(cap // SC_GATHER_ROWS):
                chunk = idx_v.at[pl.ds(c * SC_GATHER_ROWS, SC_GATHER_ROWS)]
                pltpu.async_copy(h_hbm.at[chunk], rows_v, sem).wait()
                pltpu.sync_copy(rows_v, out_hbm.at[pl.ds(out_base + c * SC_GATHER_ROWS, SC_GATHER_ROWS)])

    return gather(posm2d, h_words).reshape(n_e, b, cap, width)


EXPERT_ROWS = 1024
EXPERT_FTILE = 256


EXPERT_WBUFS = 2


def _expert_kernel(xg_ref, wg_hbm, wu_hbm, wd_hbm, o_ref,
                   x_ref, g_ref, hact_ref, wg_full, wu_full, wd_full, wg_buf, wu_buf, wd_buf, sem,
                   *, layer, e_offset, n_ftiles):
    e = pl.program_id(0)
    m = pl.program_id(1)
    n_e = pl.num_programs(0)
    nb, cap, width = xg_ref.shape[1], xg_ref.shape[2], xg_ref.shape[3]
    half = width - LANES
    d_model = 2 * half
    rows = nb * cap
    tf = EXPERT_FTILE
    assert n_ftiles % EXPERT_WBUFS == 0

    def tile_copies(expert, f):
        ge = e_offset + expert
        slot = f % EXPERT_WBUFS
        return (
            pltpu.make_async_copy(wg_hbm.at[layer, ge, :, pl.ds(f * tf, tf)], wg_buf.at[slot], sem.at[0, slot]),
            pltpu.make_async_copy(wu_hbm.at[layer, ge, :, pl.ds(f * tf, tf)], wu_buf.at[slot], sem.at[1, slot]),
            pltpu.make_async_copy(wd_hbm.at[layer, ge, pl.ds(f * tf, tf), :], wd_buf.at[slot], sem.at[2, slot]),
        )

    def start(expert, f):
        for cp in tile_copies(expert, f):
            cp.start()

    @pl.when((e == 0) & (m == 0))
    def _():
        for f in range(EXPERT_WBUFS):
            start(e, f)

    for i in range(nb):
        r0 = i * cap
        hi, lo = _unpack_row_words(xg_ref[0, i])
        x_ref[r0:r0 + cap, 0:half] = hi[:, 0:half].astype(BF16)
        x_ref[r0:r0 + cap, half:d_model] = lo[:, 0:half].astype(BF16)
        g_ref[r0:r0 + cap, :] = jnp.broadcast_to(
            _unpack_gate(hi[:, half:width], e_offset + e), (cap, LANES))

    def receive(f):
        slot = f % EXPERT_WBUFS
        for cp in tile_copies(e, f):
            cp.wait()
        wg_full[:, f * tf:(f + 1) * tf] = wg_buf[slot].astype(BF16)
        wu_full[:, f * tf:(f + 1) * tf] = wu_buf[slot].astype(BF16)
        wd_full[f * tf:(f + 1) * tf, :] = wd_buf[slot].astype(BF16)
        ahead = f + EXPERT_WBUFS
        if ahead < n_ftiles:
            start(e, ahead)
        else:
            @pl.when(e + 1 < n_e)
            def _():
                start(e + 1, ahead - n_ftiles)

    def body(first_group):
        x = x_ref[...]
        for f in range(n_ftiles):
            if first_group:
                receive(f)
            a = _dot(x, wg_full[:, f * tf:(f + 1) * tf])
            u = _dot(x, wu_full[:, f * tf:(f + 1) * tf])
            hact_ref[:, f * tf:(f + 1) * tf] = (a * jax.nn.sigmoid(a) * u).astype(BF16)
        chunk = 512
        for c in range(0, d_model, chunk):
            y = _dot(hact_ref[...], wd_full[:, c:c + chunk])
            for j in range(0, chunk, LANES):
                o_ref[:, 0, :, c + j:c + j + LANES] = (
                    (y[:, j:j + LANES] * g_ref[...]).astype(BF16).reshape(nb, cap, LANES))

    @pl.when(m == 0)
    def _():
        body(True)

    @pl.when(m != 0)
    def _():
        body(False)


def _experts(xg, w_gate, w_up, w_down, layer, e_offset):
    n_e, b, cap, width = xg.shape
    d_model = 2 * (width - LANES)
    d_expert = w_gate.shape[3]
    nb = EXPERT_ROWS // cap
    n_ftiles = d_expert // EXPERT_FTILE
    return pl.pallas_call(
        functools.partial(_expert_kernel, layer=layer, e_offset=e_offset, n_ftiles=n_ftiles),
        out_shape=jax.ShapeDtypeStruct((b, n_e, cap, d_model), BF16),
        grid=(n_e, b // nb),
        in_specs=[
            pl.BlockSpec((1, nb, cap, width), lambda e, m: (e, m, 0, 0)),
            pl.BlockSpec(memory_space=pl.ANY),
            pl.BlockSpec(memory_space=pl.ANY),
            pl.BlockSpec(memory_space=pl.ANY),
        ],
        out_specs=pl.BlockSpec((nb, 1, cap, d_model), lambda e, m: (m, e, 0, 0)),
        scratch_shapes=[pltpu.VMEM((EXPERT_ROWS, d_model), BF16),
                        pltpu.VMEM((EXPERT_ROWS, LANES), F32),
                        pltpu.VMEM((EXPERT_ROWS, d_expert), BF16),
                        pltpu.VMEM((d_model, d_expert), BF16),
                        pltpu.VMEM((d_model, d_expert), BF16),
                        pltpu.VMEM((d_expert, d_model), BF16),
                        pltpu.VMEM((EXPERT_WBUFS, d_model, EXPERT_FTILE), F32),
                        pltpu.VMEM((EXPERT_WBUFS, d_model, EXPERT_FTILE), F32),
                        pltpu.VMEM((EXPERT_WBUFS, EXPERT_FTILE, d_model), F32),
                        pltpu.SemaphoreType.DMA((3, EXPERT_WBUFS))],
        compiler_params=_params(2),
        name="experts",
    )(xg, w_gate, w_up, w_down)


COMBINE_NTILE = 512
COMBINE_MCHUNK = 512


def _combine_kernel(aff_ref, thr_ref, need_ref, x1_ref, *rest, cap, n_experts):
    y_refs, o_ref, p_ref = rest[:-2], rest[-2], rest[-1]
    n = pl.program_id(1)
    seq = aff_ref.shape[0]

    @pl.when(n == 0)
    def _():
        key = pltpu.bitcast(aff_ref[...], I32)
        posm = _select_slots(key, thr_ref[0], need_ref[0], _prefix_rows)
        slot = lax.broadcasted_iota(I32, (1, cap), 1).astype(F32)
        for e in range(n_experts):
            col = posm[:, e:e + 1]
            p_ref[:, e * cap:(e + 1) * cap] = jnp.where(col == slot, 1.0, 0.0).astype(BF16)

    for r in range(0, seq, COMBINE_MCHUNK):
        acc = x1_ref[0, r:r + COMBINE_MCHUNK, :]
        k0 = 0
        for y_ref in y_refs:
            k1 = k0 + y_ref.shape[1]
            acc = acc + _dot(p_ref[r:r + COMBINE_MCHUNK, k0:k1], y_ref[0])
            k0 = k1
        o_ref[0, r:r + COMBINE_MCHUNK, :] = acc


def _combine(aff2d, thr_row, need_row, x1_3d, y_groups, cap, n_experts):
    b, seq, d_model = x1_3d.shape
    assert sum(y.shape[1] for y in y_groups) == n_experts * cap
    return pl.pallas_call(
        functools.partial(_combine_kernel, cap=cap, n_experts=n_experts),
        out_shape=jax.ShapeDtypeStruct((b, seq, d_model), F32),
        grid=(b, d_model // COMBINE_NTILE),
        in_specs=[
            pl.BlockSpec((seq, LANES), lambda i, n: (i, 0)),
            pl.BlockSpec((1, 1, LANES), lambda i, n: (i, 0, 0)),
            pl.BlockSpec((1, 1, LANES), lambda i, n: (i, 0, 0)),
            pl.BlockSpec((1, seq, COMBINE_NTILE), lambda i, n: (i, 0, n)),
        ] + [pl.BlockSpec((1, y.shape[1], COMBINE_NTILE), lambda i, n: (i, 0, n)) for y in y_groups],
        out_specs=pl.BlockSpec((1, seq, COMBINE_NTILE), lambda i, n: (i, 0, n)),
        scratch_shapes=[pltpu.VMEM((seq, n_experts * cap), BF16)],
        compiler_params=_params(2),
        name="combine",
    )(aff2d, thr_row, need_row, x1_3d, *y_groups)


EXPERT_GROUPS = 2


def _moe(afft, aff2d, h_words, x1_2d, w_gate, w_up, w_down, layer, b, seq):
    n_experts = afft.shape[1]
    assert n_experts <= GATE_GROUP and n_experts % EXPERT_GROUPS == 0
    d_model = x1_2d.shape[1]
    cap = CAPACITY_FACTOR * seq // n_experts
    n_e = n_experts // EXPERT_GROUPS
    posm, thr, need = _route(afft, cap)
    posm2d = posm.reshape(b * n_experts, seq)
    xgs = [_sc_expert_gather(posm2d, h_words, n_experts, g * n_e, n_e, b, seq, cap)
           for g in range(EXPERT_GROUPS)]
    ys = [_experts(xg, w_gate, w_up, w_down, layer, g * n_e).reshape(b, n_e * cap, d_model)
          for g, xg in enumerate(xgs)]
    pad = LANES - n_experts
    thr_row = jnp.pad(thr.reshape(b, 1, n_experts), ((0, 0), (0, 0), (0, pad)),
                      constant_values=np.iinfo(np.int32).max)
    need_row = jnp.pad(need.reshape(b, 1, n_experts), ((0, 0), (0, 0), (0, pad)))
    return _combine(aff2d, thr_row, need_row, x1_2d.reshape(b, seq, d_model), ys, cap, n_experts)


def _rotary_tile(t, cos, sin_lo, sin_hi):
    half = ROT_DIM // 2
    return t * cos + pltpu.roll(t, LANES - half, 1) * sin_lo + pltpu.roll(t, half, 1) * sin_hi


def _inproj_attn_kernel(x_ref, g_ref, w_ref, pos_ref, rot_ref,
                        q_ref, k_ref, v_ref, qm_ref, *, tok_width, kv_width):
    hn = _rmsnorm_rows(x_ref[...], g_ref[...]).astype(BF16)
    ang = pos_ref[...].astype(F32) * rot_ref[0:1, :]
    cos = jnp.cos(ang)
    sin = jnp.sin(ang)
    sin_lo = sin * rot_ref[1:2, :]
    sin_hi = sin * rot_ref[2:3, :]
    qscale = HEAD_DIM ** -0.5
    chunk = 512
    for c in range(0, tok_width, chunk):
        pc = _dot(hn, w_ref[:, c:c + chunk])
        for j in range(0, chunk, LANES):
            rot = _rotary_tile(pc[:, j:j + LANES], cos, sin_lo, sin_hi)
            q_ref[:, c + j:c + j + LANES] = (rot * qscale).astype(BF16)
    kv = _dot(hn, w_ref[:, tok_width:tok_width + 2 * kv_width])
    k01 = _rotary_tile(kv[:, 0:LANES], cos, sin_lo, sin_hi)
    k2x = _rotary_tile(kv[:, LANES:2 * LANES], cos, sin_lo, sin_hi)
    k_ref[0, 0] = k01[:, 0:HEAD_DIM].astype(BF16)
    k_ref[0, 1] = k01[:, HEAD_DIM:LANES].astype(BF16)
    k_ref[0, 2] = k2x[:, 0:HEAD_DIM].astype(BF16)
    for hh in range(kv_width // HEAD_DIM):
        lo = kv_width + hh * HEAD_DIM
        v_ref[0, hh] = kv[:, lo:lo + HEAD_DIM].astype(BF16)
    qm_ref[...] = _dot(hn, w_ref[:, tok_width + 2 * kv_width:]).astype(BF16)


def _inproj_attn(x2d, g, w_bf16, pos2d, rot_rows, tok_width, kv_width, seq):
    t, d_model = x2d.shape
    n = w_bf16.shape[1]
    n_kv = kv_width // HEAD_DIM
    assert n_kv == 3 and kv_width + HEAD_DIM == 2 * LANES
    tiles_per_seq = seq // ROW_TILE
    kv_spec = pl.BlockSpec((1, n_kv, ROW_TILE, HEAD_DIM),
                           lambda i: (i // tiles_per_seq, 0, i % tiles_per_seq, 0))
    return pl.pallas_call(
        functools.partial(_inproj_attn_kernel, tok_width=tok_width, kv_width=kv_width),
        out_shape=(jax.ShapeDtypeStruct((t, tok_width), BF16),
                   jax.ShapeDtypeStruct((t // seq, n_kv, seq, HEAD_DIM), BF16),
                   jax.ShapeDtypeStruct((t // seq, n_kv, seq, HEAD_DIM), BF16),
                   jax.ShapeDtypeStruct((t, n - tok_width - 2 * kv_width), BF16)),
        grid=(t // ROW_TILE,),
        in_specs=[
            pl.BlockSpec((ROW_TILE, d_model), lambda i: (i, 0)),
            pl.BlockSpec((1, d_model), lambda i: (0, 0)),
            _resident((d_model, n), lambda i: (0, 0)),
            pl.BlockSpec((ROW_TILE, 1), lambda i: (i, 0)),
            pl.BlockSpec((8, LANES), lambda i: (0, 0)),
        ],
        out_specs=(pl.BlockSpec((ROW_TILE, tok_width), lambda i: (i, 0)),
                   kv_spec, kv_spec,
                   pl.BlockSpec((ROW_TILE, n - tok_width - 2 * kv_width), lambda i: (i, 0))),
        compiler_params=_params(1),
        name="inproj_attn",
    )(x2d, g, w_bf16, pos2d, rot_rows)


def _wattn_kernel(sink_ref, q_ref, kp_ref, kc_ref, kn_ref, vp_ref, vc_ref, vn_ref, o_ref,
                  valid_ref, kpad_ref, vpad_ref, s_ref, p_ref, inv_ref, *, seq):
    n = pl.program_id(1)
    n_kv = kc_ref.shape[1]
    pairs = GQA_RATIO // 2
    half_rows = pairs * BLOCK
    qi = lax.broadcasted_iota(I32, (BLOCK, 3 * BLOCK), 0)
    kj = lax.broadcasted_iota(I32, (BLOCK, 3 * BLOCK), 1)
    first = jnp.maximum(qi, BLOCK - n * BLOCK)
    last = jnp.minimum(qi + 2 * WINDOW, seq + BLOCK - 1 - n * BLOCK)
    valid_ref[...] = jnp.where(((kj - first) | (last - kj)) >= 0, 1.0, 0.0)
    zeros = jnp.zeros((3 * BLOCK, HEAD_DIM), BF16)
    for hk in range(n_kv):
        kw = jnp.concatenate([kp_ref[0, hk], kc_ref[0, hk], kn_ref[0, hk]], axis=0)
        vw = jnp.concatenate([vp_ref[0, hk], vc_ref[0, hk], vn_ref[0, hk]], axis=0)
        kpad_ref[2 * hk] = jnp.concatenate([kw, zeros], axis=1)
        kpad_ref[2 * hk + 1] = jnp.concatenate([zeros, kw], axis=1)
        vpad_ref[2 * hk] = jnp.concatenate([vw, zeros], axis=1)
        vpad_ref[2 * hk + 1] = jnp.concatenate([zeros, vw], axis=1)
        tile0 = hk * pairs
        qs = jnp.concatenate(
            [q_ref[0, :, (tile0 + j) * LANES:(tile0 + j + 1) * LANES] for j in range(pairs)], axis=0)
        s_ref[hk, 0:half_rows, :] = _dot_nt(qs, kpad_ref[2 * hk])
        s_ref[hk, half_rows:2 * half_rows, :] = _dot_nt(qs, kpad_ref[2 * hk + 1])
    for hk in range(n_kv):
        for c in range(GQA_RATIO):
            j, odd = c % pairs, c // pairs
            r = c * BLOCK
            s = jnp.where(valid_ref[...] > 0.5, s_ref[hk, r:r + BLOCK, :], NEG_INF)
            sk = sink_ref[hk * GQA_RATIO + 2 * j + odd]
            m = jnp.maximum(jnp.max(s, axis=-1, keepdims=True), sk)
            p = jnp.exp(s - m)
            inv = 1.0 / (jnp.sum(p, axis=-1, keepdims=True) + jnp.exp(sk - m))
            p_ref[hk, r:r + BLOCK, :] = p.astype(BF16)
            inv_ref[hk, j * BLOCK:(j + 1) * BLOCK, odd * HEAD_DIM:(odd + 1) * HEAD_DIM] = (
                jnp.broadcast_to(inv, (BLOCK, HEAD_DIM)))
    for hk in range(n_kv):
        o = (_dot(p_ref[hk, 0:half_rows, :], vpad_ref[2 * hk])
             + _dot(p_ref[hk, half_rows:2 * half_rows, :], vpad_ref[2 * hk + 1])) * inv_ref[hk]
        for j in range(pairs):
            lo = (hk * pairs + j) * LANES
            o_ref[0, :, lo:lo + LANES] = o[j * BLOCK:(j + 1) * BLOCK].astype(BF16)


def _window_attention(sink, q3d, k4d, v4d):
    b, seq, tok_width = q3d.shape
    n_kv = k4d.shape[1]
    nb = seq // BLOCK
    kv_block = (1, n_kv, BLOCK, HEAD_DIM)
    prev_spec = pl.BlockSpec(kv_block, lambda i, n: (i, 0, jnp.maximum(n - 1, 0), 0))
    cur_spec = pl.BlockSpec(kv_block, lambda i, n: (i, 0, n, 0))
    next_spec = pl.BlockSpec(kv_block, lambda i, n: (i, 0, jnp.minimum(n + 1, nb - 1), 0))
    return pl.pallas_call(
        functools.partial(_wattn_kernel, seq=seq),
        out_shape=jax.ShapeDtypeStruct((b, seq, tok_width), BF16),
        grid=(b, nb),
        in_specs=[
            pl.BlockSpec(memory_space=pltpu.SMEM),
            pl.BlockSpec((1, BLOCK, tok_width), lambda i, n: (i, n, 0)),
            prev_spec, cur_spec, next_spec, prev_spec, cur_spec, next_spec,
        ],
        out_specs=pl.BlockSpec((1, BLOCK, tok_width), lambda i, n: (i, n, 0)),
        scratch_shapes=[pltpu.VMEM((BLOCK, 3 * BLOCK), F32),
                        pltpu.VMEM((2 * n_kv, 3 * BLOCK, LANES), BF16),
                        pltpu.VMEM((2 * n_kv, 3 * BLOCK, LANES), BF16),
                        pltpu.VMEM((n_kv, GQA_RATIO * BLOCK, 3 * BLOCK), F32),
                        pltpu.VMEM((n_kv, GQA_RATIO * BLOCK, 3 * BLOCK), BF16),
                        pltpu.VMEM((n_kv, GQA_RATIO // 2 * BLOCK, LANES), F32)],
        compiler_params=_params(2),
        name="window_attention",
    )(sink, q3d, k4d, k4d, k4d, v4d, v4d, v4d)


def _final_norm_kernel(x_ref, g_ref, o_ref):
    o_ref[...] = _rmsnorm_rows(x_ref[...], g_ref[...])


def _final_norm(x2d, g):
    t, d_model = x2d.shape
    return pl.pallas_call(
        _final_norm_kernel,
        out_shape=jax.ShapeDtypeStruct((t, d_model), F32),
        grid=(t // ROW_TILE,),
        in_specs=[pl.BlockSpec((ROW_TILE, d_model), lambda i: (i, 0)),
                  pl.BlockSpec((1, d_model), lambda i: (0, 0))],
        out_specs=pl.BlockSpec((ROW_TILE, d_model), lambda i: (i, 0)),
        compiler_params=_params(1),
        name="final_norm",
    )(x2d, g)


def _rotary_rows(dtype=F32):
    half = ROT_DIM // 2
    inv_freq = ROPE_THETA ** (-jnp.arange(0, ROT_DIM, 2, dtype=jnp.float32) / ROT_DIM)
    lane = np.arange(LANES) % HEAD_DIM
    rotated = lane < ROT_DIM
    freq = jnp.where(jnp.asarray(rotated), inv_freq[jnp.asarray(lane % half)], 0.0)
    rows = jnp.zeros((8, LANES), dtype)
    rows = rows.at[0].set(freq)
    rows = rows.at[1].set(jnp.asarray(np.where(lane < half, -1.0, 0.0), dtype))
    rows = rows.at[2].set(jnp.asarray(np.where(rotated & (lane >= half), 1.0, 0.0), dtype))
    return rows


def kernel(x, mem, positions, norm_mix_g, norm_ffn_g, mem_norm_g, final_g, mem_w_kv,
           pool_w_in, pool_group_w, pool_scale, pool_w_out,
           attn_w_in, attn_sink, attn_w_out,
           router_w, exp_w_gate, exp_w_up, exp_w_down):
    b, seq, d_model = x.shape
    depth = norm_mix_g.shape[0]
    t = b * seq
    n_experts = router_w.shape[2]
    tok_width = pool_scale.shape[1]
    n_groups = pool_group_w.shape[1]
    kv_width = (attn_w_in.shape[2] - tok_width - XA_WIDTH) // 2
    assert seq % ROW_TILE == 0 and mem.shape[1] == MEM_LEN

    memkv = _memkv(mem.reshape(b * MEM_LEN, d_model), mem_norm_g.reshape(1, d_model),
                   mem_w_kv.astype(BF16))
    memkv = memkv.reshape(depth, b, MEM_LEN, 2 * XA_WIDTH)
    rw_pad = jnp.pad(router_w, ((0, 0), (0, 0), (0, LANES - n_experts)))
    pos2d = positions.reshape(t, 1)
    rot_rows = _rotary_rows()

    x2d = x.reshape(t, d_model)
    for layer in range(depth):
        j = layer // 2
        g_mix = norm_mix_g[layer].reshape(1, d_model)
        if layer % 2 == 0:
            u, qm = _inproj_pool(x2d, g_mix, pool_w_in[j].astype(BF16), tok_width)
            tok = _pool_mixer(u.reshape(b, seq, tok_width), pool_group_w[j].astype(BF16),
                              pool_scale[j].reshape(n_groups, 1, tok_width // n_groups))
            tok = tok.reshape(t, tok_width)
            w_out = pool_w_out[j]
        else:
            q, k, v, qm = _inproj_attn(x2d, g_mix, attn_w_in[j].astype(BF16), pos2d, rot_rows,
                                       tok_width, kv_width, seq)
            tok = _window_attention(attn_sink[j], q.reshape(b, seq, tok_width), k, v)
            tok = tok.reshape(t, tok_width)
            w_out = attn_w_out[j]
        mo = _mem_xattn(qm, memkv[layer], seq)
        x1, h, afft, aff = _outproj(tok, mo, x2d, w_out.astype(BF16),
                                    norm_ffn_g[layer].reshape(1, d_model), rw_pad[layer],
                                    n_experts, seq)
        x2 = _moe(afft, aff, h, x1, exp_w_gate, exp_w_up, exp_w_down, layer, b, seq)
        x2d = x2.reshape(t, d_model)
    return _final_norm(x2d, final_g.reshape(1, d_model)).reshape(b, seq, d_model)
```

```python
import functools

import jax
import jax.numpy as jnp
import numpy as np
from jax import lax
from jax.experimental import pallas as pl
from jax.experimental.pallas import tpu as pltpu
from jax.experimental.pallas import tpu_sc as plsc

F32 = jnp.float32
BF16 = jnp.bfloat16
I32 = jnp.int32
U32 = jnp.uint32

EPS = 1e-6
MEM_LEN = 256
XA_HEADS = 4
XA_HEAD_DIM = 128
XA_WIDTH = XA_HEADS * XA_HEAD_DIM
POOL_WINDOWS = (2, 4, 8, 16)
HEAD_DIM = 64
GQA_RATIO = 8
WINDOW = 128
BLOCK = 128
ROPE_THETA = 500000.0
ROT_DIM = 16
NEG_INF = -1e30
CAPACITY_FACTOR = 2

LANES = 128
MIB = 1024 * 1024
VMEM_LIMIT_BYTES = 56 * MIB

ROW_TILE = 512
POOL_PAD = 16
PREFIX_CHUNK = 256


def _params(n_grid_dims, flags=None):
    return pltpu.CompilerParams(
        dimension_semantics=("arbitrary",) * n_grid_dims,
        vmem_limit_bytes=VMEM_LIMIT_BYTES,
        flags=flags,
    )


def _resident(block_shape, index_map):
    return pl.BlockSpec(block_shape, index_map, pipeline_mode=pl.Buffered(1))


def _rmsnorm_rows(x, g):
    return x * lax.rsqrt(jnp.mean(x * x, axis=-1, keepdims=True) + EPS) * g


def _dot(a, b):
    return jnp.dot(a, b, preferred_element_type=F32)


def _dot_nt(a, b):
    return lax.dot_general(a, b, (((1,), (1,)), ((), ())), preferred_element_type=F32)


def _memkv_kernel(mem_ref, g_ref, w_ref, o_ref):
    hn = _rmsnorm_rows(mem_ref[...], g_ref[...]).astype(BF16)
    o_ref[0] = _dot(hn, w_ref[0]).astype(BF16)


def _memkv(mem2d, g, w_bf16):
    depth, d_model, n = w_bf16.shape
    rows = mem2d.shape[0]
    return pl.pallas_call(
        _memkv_kernel,
        out_shape=jax.ShapeDtypeStruct((depth, rows, n), BF16),
        grid=(depth, rows // ROW_TILE),
        in_specs=[
            pl.BlockSpec((ROW_TILE, d_model), lambda l, i: (i, 0)),
            pl.BlockSpec((1, d_model), lambda l, i: (0, 0)),
            pl.BlockSpec((1, d_model, n), lambda l, i: (l, 0, 0)),
        ],
        out_specs=pl.BlockSpec((1, ROW_TILE, n), lambda l, i: (l, i, 0)),
        compiler_params=_params(2),
        name="memkv",
    )(mem2d, g, w_bf16)


def _inproj_pool_kernel(x_ref, g_ref, w_ref, u_ref, qm_ref, *, tok_width):
    hn = _rmsnorm_rows(x_ref[...], g_ref[...]).astype(BF16)
    chunk = 512
    for c in range(0, tok_width, chunk):
        u_ref[:, c:c + chunk] = _dot(hn, w_ref[:, c:c + chunk])
    qm_ref[...] = _dot(hn, w_ref[:, tok_width:]).astype(BF16)


def _inproj_pool(x2d, g, w_bf16, tok_width):
    t, d_model = x2d.shape
    n = w_bf16.shape[1]
    return pl.pallas_call(
        functools.partial(_inproj_pool_kernel, tok_width=tok_width),
        out_shape=(jax.ShapeDtypeStruct((t, tok_width), F32),
                   jax.ShapeDtypeStruct((t, n - tok_width), BF16)),
        grid=(t // ROW_TILE,),
        in_specs=[
            pl.BlockSpec((ROW_TILE, d_model), lambda i: (i, 0)),
            pl.BlockSpec((1, d_model), lambda i: (0, 0)),
            _resident((d_model, n), lambda i: (0, 0)),
        ],
        out_specs=(pl.BlockSpec((ROW_TILE, tok_width), lambda i: (i, 0)),
                   pl.BlockSpec((ROW_TILE, n - tok_width), lambda i: (i, 0))),
        compiler_params=_params(1),
        name="inproj_pool",
    )(x2d, g, w_bf16)


def _pool_group(u_ref, gw_ref, sc_ref, o_ref, a_ref, b_ref, *, window, seq):
    gwid = u_ref.shape[2]
    rows = seq + 2 * POOL_PAD
    zeros_pad = jnp.zeros((POOL_PAD, gwid), F32)
    a_ref[0:POOL_PAD, :] = zeros_pad
    b_ref[0:POOL_PAD, :] = zeros_pad
    a_ref[POOL_PAD:POOL_PAD + seq, :] = u_ref[0]
    a_ref[POOL_PAD + seq:rows, :] = zeros_pad
    src, dst = a_ref, b_ref
    shift = 1
    while shift < window:
        dst[POOL_PAD:rows, :] = src[POOL_PAD - shift:rows - shift, :] + src[POOL_PAD:rows, :]
        src, dst = dst, src
        shift *= 2
    off = POOL_PAD + window // 2 - 1
    win = src[off:off + seq, :]
    t = lax.broadcasted_iota(I32, (seq, 1), 0)
    lo = jnp.maximum(t - window // 2, 0)
    hi = jnp.minimum(t + window // 2 - 1, seq - 1)
    cnt = (hi - lo + 1).astype(F32)
    pooled = (win / cnt - u_ref[0]).astype(BF16)
    o_ref[0] = (_dot(pooled, gw_ref[0]) * sc_ref[0]).astype(BF16)


def _pool_kernel(u_ref, gw_ref, sc_ref, o_ref, a_ref, b_ref, *, seq):
    g = pl.program_id(1)
    for k, window in enumerate(POOL_WINDOWS):
        @pl.when(g == k)
        def _():
            _pool_group(u_ref, gw_ref, sc_ref, o_ref, a_ref, b_ref, window=window, seq=seq)


def _pool_mixer(u3d, gw_bf16, scale3d):
    b, seq, tok_width = u3d.shape
    n_groups, gwid, _ = gw_bf16.shape
    assert n_groups == len(POOL_WINDOWS) and n_groups * gwid == tok_width
    return pl.pallas_call(
        functools.partial(_pool_kernel, seq=seq),
        out_shape=jax.ShapeDtypeStruct((b, seq, tok_width), BF16),
        grid=(b, n_groups),
        in_specs=[
            pl.BlockSpec((1, seq, gwid), lambda i, g: (i, 0, g)),
            pl.BlockSpec((1, gwid, gwid), lambda i, g: (g, 0, 0)),
            pl.BlockSpec((1, 1, gwid), lambda i, g: (g, 0, 0)),
        ],
        out_specs=pl.BlockSpec((1, seq, gwid), lambda i, g: (i, 0, g)),
        scratch_shapes=[pltpu.VMEM((seq + 2 * POOL_PAD, gwid), F32),
                        pltpu.VMEM((seq + 2 * POOL_PAD, gwid), F32)],
        compiler_params=_params(2),
        name="pool_mixer",
    )(u3d, gw_bf16, scale3d)


def _xattn_kernel(q_ref, kv_ref, o_ref):
    scale = XA_HEAD_DIM ** -0.5
    for h in range(XA_HEADS):
        lo = h * XA_HEAD_DIM
        q = q_ref[:, lo:lo + XA_HEAD_DIM]
        k = kv_ref[0, :, lo:lo + XA_HEAD_DIM]
        v = kv_ref[0, :, XA_WIDTH + lo:XA_WIDTH + lo + XA_HEAD_DIM]
        s = _dot_nt(q, k) * scale
        m = jnp.max(s, axis=-1, keepdims=True)
        p = jnp.exp(s - m)
        den = jnp.sum(p, axis=-1, keepdims=True)
        o_ref[:, lo:lo + XA_HEAD_DIM] = (_dot(p.astype(BF16), v) / den).astype(BF16)


def _mem_xattn(qm2d, memkv_layer, seq):
    t = qm2d.shape[0]
    tiles_per_seq = seq // ROW_TILE
    return pl.pallas_call(
        _xattn_kernel,
        out_shape=jax.ShapeDtypeStruct((t, XA_WIDTH), BF16),
        grid=(t // ROW_TILE,),
        in_specs=[
            pl.BlockSpec((ROW_TILE, XA_WIDTH), lambda i: (i, 0)),
            pl.BlockSpec((1, MEM_LEN, 2 * XA_WIDTH), lambda i: (i // tiles_per_seq, 0, 0)),
        ],
        out_specs=pl.BlockSpec((ROW_TILE, XA_WIDTH), lambda i: (i, 0)),
        compiler_params=_params(1),
        name="mem_xattn",
    )(qm2d, memkv_layer)


def _outproj_kernel(tok_ref, mo_ref, x_ref, w_ref, g_ref, rw_ref,
                    x1_ref, h_ref, afft_ref, aff_ref, wcat_ref, x1prev_ref,
                    *, tok_width, n_experts):
    @pl.when(pl.program_id(0) == 0)
    def _():
        rw = rw_ref[...]
        w_hi = rw.astype(BF16)
        wcat_ref[:, 0:LANES] = w_hi
        wcat_ref[:, LANES:2 * LANES] = (rw - w_hi.astype(F32)).astype(BF16)
        x1prev_ref[...] = jnp.zeros_like(x1prev_ref)

    hn = _rmsnorm_rows(x1prev_ref[...], g_ref[...])
    h_prev = hn.astype(BF16)
    h_lo = (hn - h_prev.astype(F32)).astype(BF16)
    r = _dot(h_prev, wcat_ref[...]) + _dot(h_lo, wcat_ref[...])
    logits = r[:, 0:LANES] + r[:, LANES:2 * LANES]
    lt = logits.T[0:n_experts, :]
    m = jnp.max(lt, axis=0, keepdims=True)
    ex = jnp.exp(lt - m)
    afft = ex / jnp.sum(ex, axis=0, keepdims=True)
    afft_ref[0] = afft
    padded = jnp.concatenate(
        [afft, jnp.zeros((LANES - n_experts, afft.shape[1]), F32)], axis=0)
    aff = padded.T
    aff_ref[...] = aff
    h_ref[...] = _pack_row_words(h_prev, _pack_gate_lanes(aff))

    y = _dot(tok_ref[...], w_ref[0:tok_width, :]) + _dot(mo_ref[...], w_ref[tok_width:, :])
    x1 = x_ref[...] + y
    x1_ref[...] = x1
    x1prev_ref[...] = x1


def _outproj(tok2d, mo2d, x2d, w_bf16, g, rw_pad, n_experts, seq):
    t, d_model = x2d.shape
    tok_width = tok2d.shape[1]
    tiles_per_seq = seq // ROW_TILE
    n_tiles = t // ROW_TILE

    def cur(i):
        return jnp.minimum(i, n_tiles - 1)

    def prev(i):
        return jnp.maximum(i - 1, 0)

    return pl.pallas_call(
        functools.partial(_outproj_kernel, tok_width=tok_width, n_experts=n_experts),
        out_shape=(jax.ShapeDtypeStruct((t, d_model), F32),
                   jax.ShapeDtypeStruct((t, d_model // 2 + LANES), U32),
                   jax.ShapeDtypeStruct((t // seq, n_experts, seq), F32),
                   jax.ShapeDtypeStruct((t, LANES), F32)),
        grid=(n_tiles + 1,),
        in_specs=[
            pl.BlockSpec((ROW_TILE, tok_width), lambda i: (cur(i), 0)),
            pl.BlockSpec((ROW_TILE, mo2d.shape[1]), lambda i: (cur(i), 0)),
            pl.BlockSpec((ROW_TILE, d_model), lambda i: (cur(i), 0)),
            _resident((d_model, d_model), lambda i: (0, 0)),
            pl.BlockSpec((1, d_model), lambda i: (0, 0)),
            _resident((d_model, LANES), lambda i: (0, 0)),
        ],
        out_specs=(pl.BlockSpec((ROW_TILE, d_model), lambda i: (cur(i), 0)),
                   pl.BlockSpec((ROW_TILE, d_model // 2 + LANES), lambda i: (prev(i), 0)),
                   pl.BlockSpec((1, n_experts, ROW_TILE),
                                lambda i: (prev(i) // tiles_per_seq, 0, prev(i) % tiles_per_seq)),
                   pl.BlockSpec((ROW_TILE, LANES), lambda i: (prev(i), 0))),
        scratch_shapes=[pltpu.VMEM((d_model, 2 * LANES), BF16),
                        pltpu.VMEM((ROW_TILE, d_model), F32)],
        compiler_params=_params(1),
        name="outproj_router",
    )(tok2d, mo2d, x2d, w_bf16, g, rw_pad)


def _strict_triangle(n, lower):
    r = lax.broadcasted_iota(I32, (n, n), 0)
    c = lax.broadcasted_iota(I32, (n, n), 1)
    return jnp.where((c < r) if lower else (r < c), 1.0, 0.0).astype(BF16)


def _prefix_rows(mask_f32):
    s, l = mask_f32.shape
    tri = _strict_triangle(PREFIX_CHUNK, lower=True)
    carry = jnp.zeros((1, l), F32)
    out = []
    for c in range(0, s, PREFIX_CHUNK):
        m = mask_f32[c:c + PREFIX_CHUNK, :]
        out.append(_dot(tri, m.astype(BF16)) + carry)
        carry = carry + jnp.sum(m, axis=0, keepdims=True)
    return jnp.concatenate(out, axis=0)


def _prefix_lanes(mask_f32):
    e, s = mask_f32.shape
    tri = _strict_triangle(PREFIX_CHUNK, lower=False)
    carry = jnp.zeros((e, 1), F32)
    out = []
    for c in range(0, s, PREFIX_CHUNK):
        m = mask_f32[:, c:c + PREFIX_CHUNK]
        out.append(_dot(m.astype(BF16), tri) + carry)
        carry = carry + jnp.sum(m, axis=1, keepdims=True)
    return jnp.concatenate(out, axis=1)


def _select_slots(key, thr, need, prefix_fn):
    gt = jnp.where(key > thr, 1.0, 0.0)
    eq = jnp.where(key == thr, 1.0, 0.0)
    eq_rank = prefix_fn(eq)
    sel = gt + eq * jnp.where(eq_rank < need, 1.0, 0.0)
    pos = prefix_fn(sel)
    return jnp.where(sel > 0.5, pos, -1.0)


GATE_GROUP = 16
GATE_PIECES = 3


def _pack_gate_lanes(aff):
    hi = aff.astype(BF16).astype(F32)
    r1 = aff - hi
    mid = r1.astype(BF16).astype(F32)
    lo = (r1 - mid).astype(BF16).astype(F32)
    packed = hi + pltpu.roll(mid, GATE_GROUP, 1) + pltpu.roll(lo, 2 * GATE_GROUP, 1)
    return packed.astype(BF16)


def _unpack_gate(tail, e):
    lane = lax.broadcasted_iota(I32, (1, LANES), 1)
    mine = ((lane & (GATE_GROUP - 1)) == e) & (lane < GATE_PIECES * GATE_GROUP)
    return jnp.sum(jnp.where(mine, tail.astype(F32), 0.0), axis=1, keepdims=True)


def _pack_row_words(h, gate_tile):
    rows, d_model = h.shape
    half = d_model // 2
    hi = jnp.concatenate([h[:, 0:half], gate_tile], axis=1).astype(F32)
    lo = jnp.concatenate([h[:, half:], jnp.zeros((rows, LANES), BF16)], axis=1).astype(F32)
    return pltpu.bitcast(hi, U32) | (pltpu.bitcast(lo, U32) >> 16)


def _unpack_row_words(words):
    hi = pltpu.bitcast(words & jnp.uint32(0xFFFF0000), F32)
    lo = pltpu.bitcast(words << 16, F32)
    return hi, lo


def _route_kernel(afft_ref, posm_ref, thr_ref, need_ref, *, cap):
    n_experts = afft_ref.shape[1]
    key = pltpu.bitcast(afft_ref[0], I32)
    thr = jnp.zeros((n_experts, 1), I32)
    for bit in range(30, -1, -1):
        cand = thr | (1 << bit)
        cnt = jnp.sum(jnp.where(key >= cand, 1.0, 0.0), axis=1, keepdims=True)
        thr = jnp.where(cnt >= cap, cand, thr)
    n_gt = jnp.sum(jnp.where(key > thr, 1.0, 0.0), axis=1, keepdims=True)
    need = cap - n_gt
    thr_ref[0] = thr
    need_ref[0] = need
    posm_ref[0] = _select_slots(key, thr, need, _prefix_lanes).astype(I32)


def _route(afft, cap):
    b, n_experts, seq = afft.shape
    return pl.pallas_call(
        functools.partial(_route_kernel, cap=cap),
        out_shape=(jax.ShapeDtypeStruct((b, n_experts, seq), I32),
                   jax.ShapeDtypeStruct((b, n_experts, 1), I32),
                   jax.ShapeDtypeStruct((b, n_experts, 1), F32)),
        grid=(b,),
        in_specs=[pl.BlockSpec((1, n_experts, seq), lambda i: (i, 0, 0))],
        out_specs=(pl.BlockSpec((1, n_experts, seq), lambda i: (i, 0, 0)),
                   pl.BlockSpec((1, n_experts, 1), lambda i: (i, 0, 0)),
                   pl.BlockSpec((1, n_experts, 1), lambda i: (i, 0, 0))),
        compiler_params=_params(1),
        name="expert_route",
    )(afft)


SC_LANES = 16
SC_GATHER_ROWS = 64


def _sc_expert_gather(posm2d, h_words, n_experts, e_offset, n_e, b, seq, cap):
    width = h_words.shape[1]
    info = plsc.get_sparse_core_info()
    n_cores, n_subcores = info.num_cores, info.num_subcores
    n_workers = n_cores * n_subcores
    assert info.num_lanes == SC_LANES and (b * n_e) % n_workers == 0
    pairs_per_worker = (b * n_e) // n_workers
    mesh = plsc.VectorSubcoreMesh(core_axis_name="c", subcore_axis_name="s")

    @functools.partial(
        pl.kernel, mesh=mesh,
        out_type=jax.ShapeDtypeStruct((n_e * b * cap, width), U32),
        compiler_params=pltpu.CompilerParams(needs_layout_passes=False),
        scratch_types=[
            pltpu.VMEM((seq,), I32),
            pltpu.VMEM((cap,), I32),
            pltpu.VMEM((SC_GATHER_ROWS, width), U32),
            pltpu.SemaphoreType.DMA,
        ],
        name="sc_expert_gather",
    )
    def gather(posm_hbm, h_hbm, out_hbm, pos_v, idx_v, rows_v, sem):
        wid = lax.axis_index("s") * n_cores + lax.axis_index("c")
        for p in range(pairs_per_worker):
            pair = wid * pairs_per_worker + p
            bi = pair // n_e
            e = pair - bi * n_e
            pltpu.sync_copy(posm_hbm.at[bi * n_experts + e_offset + e], pos_v)

            @pl.loop(0, seq, step=SC_LANES)
            def _(t0):
                slots = pos_v[pl.ds(t0, SC_LANES)]
                rows = lax.iota(I32, SC_LANES) + (t0 + bi * seq)
                plsc.store_scatter(idx_v, [slots], rows, mask=slots >= 0)

            out_base = (e * b + bi) * cap
            for c in range(cap // SC_GATHER_ROWS):
                chunk = idx_v.at[pl.ds(c * SC_GATHER_ROWS, SC_GATHER_ROWS)]
                pltpu.async_copy(h_hbm.at[chunk], rows_v, sem).wait()
                pltpu.sync_copy(rows_v, out_hbm.at[pl.ds(out_base + c * SC_GATHER_ROWS, SC_GATHER_ROWS)])

    return gather(posm2d, h_words).reshape(n_e, b, cap, width)


EXPERT_ROWS = 1024
EXPERT_FTILE = 256


EXPERT_WBUFS = 2


def _expert_kernel(xg_ref, wg_hbm, wu_hbm, wd_hbm, o_ref,
                   x_ref, g_ref, hact_ref, wg_full, wu_full, wd_full, wg_buf, wu_buf, wd_buf, sem,
                   *, layer, e_offset, n_ftiles):
    e = pl.program_id(0)
    m = pl.program_id(1)
    n_e = pl.num_programs(0)
    nb, cap, width = xg_ref.shape[1], xg_ref.shape[2], xg_ref.shape[3]
    half = width - LANES
    d_model = 2 * half
    rows = nb * cap
    tf = EXPERT_FTILE
    assert n_ftiles % EXPERT_WBUFS == 0

    def tile_copies(expert, f):
        ge = e_offset + expert
        slot = f % EXPERT_WBUFS
        return (
            pltpu.make_async_copy(wg_hbm.at[layer, ge, :, pl.ds(f * tf, tf)], wg_buf.at[slot], sem.at[0, slot]),
            pltpu.make_async_copy(wu_hbm.at[layer, ge, :, pl.ds(f * tf, tf)], wu_buf.at[slot], sem.at[1, slot]),
            pltpu.make_async_copy(wd_hbm.at[layer, ge, pl.ds(f * tf, tf), :], wd_buf.at[slot], sem.at[2, slot]),
        )

    def start(expert, f):
        for cp in tile_copies(expert, f):
            cp.start()

    @pl.when((e == 0) & (m == 0))
    def _():
        for f in range(EXPERT_WBUFS):
            start(e, f)

    for i in range(nb):
        r0 = i * cap
        hi, lo = _unpack_row_words(xg_ref[0, i])
        x_ref[r0:r0 + cap, 0:half] = hi[:, 0:half].astype(BF16)
        x_ref[r0:r0 + cap, half:d_model] = lo[:, 0:half].astype(BF16)
        g_ref[r0:r0 + cap, :] = jnp.broadcast_to(
            _unpack_gate(hi[:, half:width], e_offset + e), (cap, LANES))

    def receive(f):
        slot = f % EXPERT_WBUFS
        for cp in tile_copies(e, f):
            cp.wait()
        wg_full[:, f * tf:(f + 1) * tf] = wg_buf[slot].astype(BF16)
        wu_full[:, f * tf:(f + 1) * tf] = wu_buf[slot].astype(BF16)
        wd_full[f * tf:(f + 1) * tf, :] = wd_buf[slot].astype(BF16)
        ahead = f + EXPERT_WBUFS
        if ahead < n_ftiles:
            start(e, ahead)
        else:
            @pl.when(e + 1 < n_e)
            def _():
                start(e + 1, ahead - n_ftiles)

    def body(first_group):
        x = x_ref[...]
        for f in range(n_ftiles):
            if first_group:
                receive(f)
            a = _dot(x, wg_full[:, f * tf:(f + 1) * tf])
            u = _dot(x, wu_full[:, f * tf:(f + 1) * tf])
            hact_ref[:, f * tf:(f + 1) * tf] = (a * jax.nn.sigmoid(a) * u).astype(BF16)
        chunk = 512
        for c in range(0, d_model, chunk):
            y = _dot(hact_ref[...], wd_full[:, c:c + chunk])
            for j in range(0, chunk, LANES):
                o_ref[:, 0, :, c + j:c + j + LANES] = (
                    (y[:, j:j + LANES] * g_ref[...]).astype(BF16).reshape(nb, cap, LANES))

    @pl.when(m == 0)
    def _():
        body(True)

    @pl.when(m != 0)
    def _():
        body(False)


def _experts(xg, w_gate, w_up, w_down, layer, e_offset):
    n_e, b, cap, width = xg.shape
    d_model = 2 * (width - LANES)
    d_expert = w_gate.shape[3]
    nb = EXPERT_ROWS // cap
    n_ftiles = d_expert // EXPERT_FTILE
    return pl.pallas_call(
        functools.partial(_expert_kernel, layer=layer, e_offset=e_offset, n_ftiles=n_ftiles),
        out_shape=jax.ShapeDtypeStruct((b, n_e, cap, d_model), BF16),
        grid=(n_e, b // nb),
        in_specs=[
            pl.BlockSpec((1, nb, cap, width), lambda e, m: (e, m, 0, 0)),
            pl.BlockSpec(memory_space=pl.ANY),
            pl.BlockSpec(memory_space=pl.ANY),
            pl.BlockSpec(memory_space=pl.ANY),
        ],
        out_specs=pl.BlockSpec((nb, 1, cap, d_model), lambda e, m: (m, e, 0, 0)),
        scratch_shapes=[pltpu.VMEM((EXPERT_ROWS, d_model), BF16),
                        pltpu.VMEM((EXPERT_ROWS, LANES), F32),
                        pltpu.VMEM((EXPERT_ROWS, d_expert), BF16),
                        pltpu.VMEM((d_model, d_expert), BF16),
                        pltpu.VMEM((d_model, d_expert), BF16),
                        pltpu.VMEM((d_expert, d_model), BF16),
                        pltpu.VMEM((EXPERT_WBUFS, d_model, EXPERT_FTILE), F32),
                        pltpu.VMEM((EXPERT_WBUFS, d_model, EXPERT_FTILE), F32),
                        pltpu.VMEM((EXPERT_WBUFS, EXPERT_FTILE, d_model), F32),
                        pltpu.SemaphoreType.DMA((3, EXPERT_WBUFS))],
        compiler_params=_params(2),
        name="experts",
    )(xg, w_gate, w_up, w_down)


COMBINE_NTILE = 512
COMBINE_MCHUNK = 512


def _combine_kernel(aff_ref, thr_ref, need_ref, x1_ref, *rest, cap, n_experts):
    y_refs, o_ref, p_ref = rest[:-2], rest[-2], rest[-1]
    n = pl.program_id(1)
    seq = aff_ref.shape[0]

    @pl.when(n == 0)
    def _():
        key = pltpu.bitcast(aff_ref[...], I32)
        posm = _select_slots(key, thr_ref[0], need_ref[0], _prefix_rows)
        slot = lax.broadcasted_iota(I32, (1, cap), 1).astype(F32)
        for e in range(n_experts):
            col = posm[:, e:e + 1]
            p_ref[:, e * cap:(e + 1) * cap] = jnp.where(col == slot, 1.0, 0.0).astype(BF16)

    for r in range(0, seq, COMBINE_MCHUNK):
        acc = x1_ref[0, r:r + COMBINE_MCHUNK, :]
        k0 = 0
        for y_ref in y_refs:
            k1 = k0 + y_ref.shape[1]
            acc = acc + _dot(p_ref[r:r + COMBINE_MCHUNK, k0:k1], y_ref[0])
            k0 = k1
        o_ref[0, r:r + COMBINE_MCHUNK, :] = acc


def _combine(aff2d, thr_row, need_row, x1_3d, y_groups, cap, n_experts):
    b, seq, d_model = x1_3d.shape
    assert sum(y.shape[1] for y in y_groups) == n_experts * cap
    return pl.pallas_call(
        functools.partial(_combine_kernel, cap=cap, n_experts=n_experts),
        out_shape=jax.ShapeDtypeStruct((b, seq, d_model), F32),
        grid=(b, d_model // COMBINE_NTILE),
        in_specs=[
            pl.BlockSpec((seq, LANES), lambda i, n: (i, 0)),
            pl.BlockSpec((1, 1, LANES), lambda i, n: (i, 0, 0)),
            pl.BlockSpec((1, 1, LANES), lambda i, n: (i, 0, 0)),
            pl.BlockSpec((1, seq, COMBINE_NTILE), lambda i, n: (i, 0, n)),
        ] + [pl.BlockSpec((1, y.shape[1], COMBINE_NTILE), lambda i, n: (i, 0, n)) for y in y_groups],
        out_specs=pl.BlockSpec((1, seq, COMBINE_NTILE), lambda i, n: (i, 0, n)),
        scratch_shapes=[pltpu.VMEM((seq, n_experts * cap), BF16)],
        compiler_params=_params(2),
        name="combine",
    )(aff2d, thr_row, need_row, x1_3d, *y_groups)


EXPERT_GROUPS = 2


def _moe(afft, aff2d, h_words, x1_2d, w_gate, w_up, w_down, layer, b, seq):
    n_experts = afft.shape[1]
    assert n_experts <= GATE_GROUP and n_experts % EXPERT_GROUPS == 0
    d_model = x1_2d.shape[1]
    cap = CAPACITY_FACTOR * seq // n_experts
    n_e = n_experts // EXPERT_GROUPS
    posm, thr, need = _route(afft, cap)
    posm2d = posm.reshape(b * n_experts, seq)
    xgs = [_sc_expert_gather(posm2d, h_words, n_experts, g * n_e, n_e, b, seq, cap)
           for g in range(EXPERT_GROUPS)]
    ys = [_experts(xg, w_gate, w_up, w_down, layer, g * n_e).reshape(b, n_e * cap, d_model)
          for g, xg in enumerate(xgs)]
    pad = LANES - n_experts
    thr_row = jnp.pad(thr.reshape(b, 1, n_experts), ((0, 0), (0, 0), (0, pad)),
                      constant_values=np.iinfo(np.int32).max)
    need_row = jnp.pad(need.reshape(b, 1, n_experts), ((0, 0), (0, 0), (0, pad)))
    return _combine(aff2d, thr_row, need_row, x1_2d.reshape(b, seq, d_model), ys, cap, n_experts)


def _rotary_tile(t, cos, sin_lo, sin_hi):
    half = ROT_DIM // 2
    return t * cos + pltpu.roll(t, LANES - half, 1) * sin_lo + pltpu.roll(t, half, 1) * sin_hi


def _inproj_attn_kernel(x_ref, g_ref, w_ref, pos_ref, rot_ref,
                        q_ref, k_ref, v_ref, qm_ref, *, tok_width, kv_width):
    hn = _rmsnorm_rows(x_ref[...], g_ref[...]).astype(BF16)
    ang = pos_ref[...].astype(F32) * rot_ref[0:1, :]
    cos = jnp.cos(ang)
    sin = jnp.sin(ang)
    sin_lo = sin * rot_ref[1:2, :]
    sin_hi = sin * rot_ref[2:3, :]
    qscale = HEAD_DIM ** -0.5
    chunk = 512
    for c in range(0, tok_width, chunk):
        pc = _dot(hn, w_ref[:, c:c + chunk])
        for j in range(0, chunk, LANES):
            rot = _rotary_tile(pc[:, j:j + LANES], cos, sin_lo, sin_hi)
            q_ref[:, c + j:c + j + LANES] = (rot * qscale).astype(BF16)
    kv = _dot(hn, w_ref[:, tok_width:tok_width + 2 * kv_width])
    k01 = _rotary_tile(kv[:, 0:LANES], cos, sin_lo, sin_hi)
    k2x = _rotary_tile(kv[:, LANES:2 * LANES], cos, sin_lo, sin_hi)
    k_ref[0, 0] = k01[:, 0:HEAD_DIM].astype(BF16)
    k_ref[0, 1] = k01[:, HEAD_DIM:LANES].astype(BF16)
    k_ref[0, 2] = k2x[:, 0:HEAD_DIM].astype(BF16)
    for hh in range(kv_width // HEAD_DIM):
        lo = kv_width + hh * HEAD_DIM
        v_ref[0, hh] = kv[:, lo:lo + HEAD_DIM].astype(BF16)
    qm_ref[...] = _dot(hn, w_ref[:, tok_width + 2 * kv_width:]).astype(BF16)


def _inproj_attn(x2d, g, w_bf16, pos2d, rot_rows, tok_width, kv_width, seq):
    t, d_model = x2d.shape
    n = w_bf16.shape[1]
    n_kv = kv_width // HEAD_DIM
    assert n_kv == 3 and kv_width + HEAD_DIM == 2 * LANES
    tiles_per_seq = seq // ROW_TILE
    kv_spec = pl.BlockSpec((1, n_kv, ROW_TILE, HEAD_DIM),
                           lambda i: (i // tiles_per_seq, 0, i % tiles_per_seq, 0))
    return pl.pallas_call(
        functools.partial(_inproj_attn_kernel, tok_width=tok_width, kv_width=kv_width),
        out_shape=(jax.ShapeDtypeStruct((t, tok_width), BF16),
                   jax.ShapeDtypeStruct((t // seq, n_kv, seq, HEAD_DIM), BF16),
                   jax.ShapeDtypeStruct((t // seq, n_kv, seq, HEAD_DIM), BF16),
                   jax.ShapeDtypeStruct((t, n - tok_width - 2 * kv_width), BF16)),
        grid=(t // ROW_TILE,),
        in_specs=[
            pl.BlockSpec((ROW_TILE, d_model), lambda i: (i, 0)),
            pl.BlockSpec((1, d_model), lambda i: (0, 0)),
            _resident((d_model, n), lambda i: (0, 0)),
            pl.BlockSpec((ROW_TILE, 1), lambda i: (i, 0)),
            pl.BlockSpec((8, LANES), lambda i: (0, 0)),
        ],
        out_specs=(pl.BlockSpec((ROW_TILE, tok_width), lambda i: (i, 0)),
                   kv_spec, kv_spec,
                   pl.BlockSpec((ROW_TILE, n - tok_width - 2 * kv_width), lambda i: (i, 0))),
        compiler_params=_params(1),
        name="inproj_attn",
    )(x2d, g, w_bf16, pos2d, rot_rows)


def _wattn_kernel(sink_ref, q_ref, kp_ref, kc_ref, kn_ref, vp_ref, vc_ref, vn_ref, o_ref,
                  valid_ref, kpad_ref, vpad_ref, s_ref, p_ref, inv_ref, *, seq):
    n = pl.program_id(1)
    n_kv = kc_ref.shape[1]
    pairs = GQA_RATIO // 2
    half_rows = pairs * BLOCK
    qi = lax.broadcasted_iota(I32, (BLOCK, 3 * BLOCK), 0)
    kj = lax.broadcasted_iota(I32, (BLOCK, 3 * BLOCK), 1)
    first = jnp.maximum(qi, BLOCK - n * BLOCK)
    last = jnp.minimum(qi + 2 * WINDOW, seq + BLOCK - 1 - n * BLOCK)
    valid_ref[...] = jnp.where(((kj - first) | (last - kj)) >= 0, 1.0, 0.0)
    zeros = jnp.zeros((3 * BLOCK, HEAD_DIM), BF16)
    ones_col = jnp.where(lax.broadcasted_iota(I32, (3 * BLOCK, HEAD_DIM), 1) == 0, 1.0, 0.0).astype(BF16)
    low_half = lax.broadcasted_iota(I32, (1, LANES), 1) < HEAD_DIM
    for hk in range(n_kv):
        kw = jnp.concatenate([kp_ref[0, hk], kc_ref[0, hk], kn_ref[0, hk]], axis=0)
        vw = jnp.concatenate([vp_ref[0, hk], vc_ref[0, hk], vn_ref[0, hk]], axis=0)
        kpad_ref[2 * hk] = jnp.concatenate([kw, zeros], axis=1)
        kpad_ref[2 * hk + 1] = jnp.concatenate([zeros, kw], axis=1)
        vpad_ref[2 * hk] = jnp.concatenate([vw, ones_col], axis=1)
        vpad_ref[2 * hk + 1] = jnp.concatenate([ones_col, vw], axis=1)
        tile0 = hk * pairs
        qs = jnp.concatenate(
            [q_ref[0, :, (tile0 + j) * LANES:(tile0 + j + 1) * LANES] for j in range(pairs)], axis=0)
        s_ref[hk, 0:half_rows, :] = _dot_nt(qs, kpad_ref[2 * hk])
        s_ref[hk, half_rows:2 * half_rows, :] = _dot_nt(qs, kpad_ref[2 * hk + 1])
    for hk in range(n_kv):
        for c in range(GQA_RATIO):
            j, odd = c % pairs, c // pairs
            r = c * BLOCK
            s = jnp.concatenate([
                jnp.where(valid_ref[:, 0:BLOCK] > 0.5, s_ref[hk, r:r + BLOCK, 0:BLOCK], NEG_INF),
                s_ref[hk, r:r + BLOCK, BLOCK:2 * BLOCK],
                jnp.where(valid_ref[:, 2 * BLOCK:] > 0.5, s_ref[hk, r:r + BLOCK, 2 * BLOCK:], NEG_INF),
            ], axis=1)
            sk = sink_ref[hk * GQA_RATIO + 2 * j + odd]
            m = jnp.maximum(jnp.max(s, axis=-1, keepdims=True), sk)
            p_ref[hk, r:r + BLOCK, :] = jnp.exp(s - m).astype(BF16)
            inv_ref[hk, j * BLOCK:(j + 1) * BLOCK, odd * HEAD_DIM:(odd + 1) * HEAD_DIM] = (
                jnp.broadcast_to(jnp.exp(sk - m), (BLOCK, HEAD_DIM)))
    for hk in range(n_kv):
        pv_even = _dot(p_ref[hk, 0:half_rows, :], vpad_ref[2 * hk])
        pv_odd = _dot(p_ref[hk, half_rows:2 * half_rows, :], vpad_ref[2 * hk + 1])
        den = jnp.where(low_half, pv_even[:, HEAD_DIM:HEAD_DIM + 1], pv_odd[:, 0:1]) + inv_ref[hk]
        o = jnp.where(low_half, pv_even, pv_odd) / den
        for j in range(pairs):
            lo = (hk * pairs + j) * LANES
            o_ref[0, :, lo:lo + LANES] = o[j * BLOCK:(j + 1) * BLOCK].astype(BF16)


def _window_attention(sink, q3d, k4d, v4d):
    b, seq, tok_width = q3d.shape
    n_kv = k4d.shape[1]
    nb = seq // BLOCK
    kv_block = (1, n_kv, BLOCK, HEAD_DIM)
    prev_spec = pl.BlockSpec(kv_block, lambda i, n: (i, 0, jnp.maximum(n - 1, 0), 0))
    cur_spec = pl.BlockSpec(kv_block, lambda i, n: (i, 0, n, 0))
    next_spec = pl.BlockSpec(kv_block, lambda i, n: (i, 0, jnp.minimum(n + 1, nb - 1), 0))
    return pl.pallas_call(
        functools.partial(_wattn_kernel, seq=seq),
        out_shape=jax.ShapeDtypeStruct((b, seq, tok_width), BF16),
        grid=(b, nb),
        in_specs=[
            pl.BlockSpec(memory_space=pltpu.SMEM),
            pl.BlockSpec((1, BLOCK, tok_width), lambda i, n: (i, n, 0)),
            prev_spec, cur_spec, next_spec, prev_spec, cur_spec, next_spec,
        ],
        out_specs=pl.BlockSpec((1, BLOCK, tok_width), lambda i, n: (i, n, 0)),
        scratch_shapes=[pltpu.VMEM((BLOCK, 3 * BLOCK), F32),
                        pltpu.VMEM((2 * n_kv, 3 * BLOCK, LANES), BF16),
                        pltpu.VMEM((2 * n_kv, 3 * BLOCK, LANES), BF16),
                        pltpu.VMEM((n_kv, GQA_RATIO * BLOCK, 3 * BLOCK), F32),
                        pltpu.VMEM((n_kv, GQA_RATIO * BLOCK, 3 * BLOCK), BF16),
                        pltpu.VMEM((n_kv, GQA_RATIO // 2 * BLOCK, LANES), F32)],
        compiler_params=_params(2),
        name="window_attention",
    )(sink, q3d, k4d, k4d, k4d, v4d, v4d, v4d)


def _final_norm_kernel(x_ref, g_ref, o_ref):
    o_ref[...] = _rmsnorm_rows(x_ref[...], g_ref[...])


def _final_norm(x2d, g):
    t, d_model = x2d.shape
    return pl.pallas_call(
        _final_norm_kernel,
        out_shape=jax.ShapeDtypeStruct((t, d_model), F32),
        grid=(t // ROW_TILE,),
        in_specs=[pl.BlockSpec((ROW_TILE, d_model), lambda i: (i, 0)),
                  pl.BlockSpec((1, d_model), lambda i: (0, 0))],
        out_specs=pl.BlockSpec((ROW_TILE, d_model), lambda i: (i, 0)),
        compiler_params=_params(1),
        name="final_norm",
    )(x2d, g)


def _rotary_rows(dtype=F32):
    half = ROT_DIM // 2
    inv_freq = ROPE_THETA ** (-jnp.arange(0, ROT_DIM, 2, dtype=jnp.float32) / ROT_DIM)
    lane = np.arange(LANES) % HEAD_DIM
    rotated = lane < ROT_DIM
    freq = jnp.where(jnp.asarray(rotated), inv_freq[jnp.asarray(lane % half)], 0.0)
    rows = jnp.zeros((8, LANES), dtype)
    rows = rows.at[0].set(freq)
    rows = rows.at[1].set(jnp.asarray(np.where(lane < half, -1.0, 0.0), dtype))
    rows = rows.at[2].set(jnp.asarray(np.where(rotated & (lane >= half), 1.0, 0.0), dtype))
    return rows


def kernel(x, mem, positions, norm_mix_g, norm_ffn_g, mem_norm_g, final_g, mem_w_kv,
           pool_w_in, pool_group_w, pool_scale, pool_w_out,
           attn_w_in, attn_sink, attn_w_out,
           router_w, exp_w_gate, exp_w_up, exp_w_down):
    b, seq, d_model = x.shape
    depth = norm_mix_g.shape[0]
    t = b * seq
    n_experts = router_w.shape[2]
    tok_width = pool_scale.shape[1]
    n_groups = pool_group_w.shape[1]
    kv_width = (attn_w_in.shape[2] - tok_width - XA_WIDTH) // 2
    assert seq % ROW_TILE == 0 and mem.shape[1] == MEM_LEN

    memkv = _memkv(mem.reshape(b * MEM_LEN, d_model), mem_norm_g.reshape(1, d_model),
                   mem_w_kv.astype(BF16))
    memkv = memkv.reshape(depth, b, MEM_LEN, 2 * XA_WIDTH)
    rw_pad = jnp.pad(router_w, ((0, 0), (0, 0), (0, LANES - n_experts)))
    pos2d = positions.reshape(t, 1)
    rot_rows = _rotary_rows()

    x2d = x.reshape(t, d_model)
    for layer in range(depth):
        j = layer // 2
        g_mix = norm_mix_g[layer].reshape(1, d_model)
        if layer % 2 == 0:
            u, qm = _inproj_pool(x2d, g_mix, pool_w_in[j].astype(BF16), tok_width)
            tok = _pool_mixer(u.reshape(b, seq, tok_width), pool_group_w[j].astype(BF16),
                              pool_scale[j].reshape(n_groups, 1, tok_width // n_groups))
            tok = tok.reshape(t, tok_width)
            w_out = pool_w_out[j]
        else:
            q, k, v, qm = _inproj_attn(x2d, g_mix, attn_w_in[j].astype(BF16), pos2d, rot_rows,
                                       tok_width, kv_width, seq)
            tok = _window_attention(attn_sink[j], q.reshape(b, seq, tok_width), k, v)
            tok = tok.reshape(t, tok_width)
            w_out = attn_w_out[j]
        mo = _mem_xattn(qm, memkv[layer], seq)
        x1, h, afft, aff = _outproj(tok, mo, x2d, w_out.astype(BF16),
                                    norm_ffn_g[layer].reshape(1, d_model), rw_pad[layer],
                                    n_experts, seq)
        x2 = _moe(afft, aff, h, x1, exp_w_gate, exp_w_up, exp_w_down, layer, b, seq)
        x2d = x2.reshape(t, d_model)
    return _final_norm(x2d, final_g.reshape(1, d_model)).reshape(b, seq, d_model)
```

```python
import functools

import jax
import jax.numpy as jnp
import numpy as np
from jax import lax
from jax.experimental import pallas as pl
from jax.experimental.pallas import tpu as pltpu
from jax.experimental.pallas import tpu_sc as plsc

F32 = jnp.float32
BF16 = jnp.bfloat16
I32 = jnp.int32
U32 = jnp.uint32

EPS = 1e-6
MEM_LEN = 256
XA_HEADS = 4
XA_HEAD_DIM = 128
XA_WIDTH = XA_HEADS * XA_HEAD_DIM
POOL_WINDOWS = (2, 4, 8, 16)
HEAD_DIM = 64
GQA_RATIO = 8
WINDOW = 128
BLOCK = 128
ROPE_THETA = 500000.0
ROT_DIM = 16
NEG_INF = -1e30
CAPACITY_FACTOR = 2

LANES = 128
MIB = 1024 * 1024
VMEM_LIMIT_BYTES = 56 * MIB

ROW_TILE = 512
POOL_PAD = 16
PREFIX_CHUNK = 256


def _params(n_grid_dims, flags=None):
    return pltpu.CompilerParams(
        dimension_semantics=("arbitrary",) * n_grid_dims,
        vmem_limit_bytes=VMEM_LIMIT_BYTES,
        flags=flags,
    )


def _resident(block_shape, index_map):
    return pl.BlockSpec(block_shape, index_map, pipeline_mode=pl.Buffered(1))


def _rmsnorm_rows(x, g):
    return x * lax.rsqrt(jnp.mean(x * x, axis=-1, keepdims=True) + EPS) * g


def _dot(a, b):
    return jnp.dot(a, b, preferred_element_type=F32)


def _dot_nt(a, b):
    return lax.dot_general(a, b, (((1,), (1,)), ((), ())), preferred_element_type=F32)


def _memkv_kernel(mem_ref, g_ref, w_ref, o_ref):
    hn = _rmsnorm_rows(mem_ref[...], g_ref[...]).astype(BF16)
    o_ref[0] = _dot(hn, w_ref[0]).astype(BF16)


def _memkv(mem2d, g, w_bf16):
    depth, d_model, n = w_bf16.shape
    rows = mem2d.shape[0]
    return pl.pallas_call(
        _memkv_kernel,
        out_shape=jax.ShapeDtypeStruct((depth, rows, n), BF16),
        grid=(depth, rows // ROW_TILE),
        in_specs=[
            pl.BlockSpec((ROW_TILE, d_model), lambda l, i: (i, 0)),
            pl.BlockSpec((1, d_model), lambda l, i: (0, 0)),
            pl.BlockSpec((1, d_model, n), lambda l, i: (l, 0, 0)),
        ],
        out_specs=pl.BlockSpec((1, ROW_TILE, n), lambda l, i: (l, i, 0)),
        compiler_params=_params(2),
        name="memkv",
    )(mem2d, g, w_bf16)


def _inproj_pool_kernel(x_ref, g_ref, w_ref, u_ref, qm_ref, *, tok_width):
    hn = _rmsnorm_rows(x_ref[...], g_ref[...]).astype(BF16)
    chunk = 512
    for c in range(0, tok_width, chunk):
        u_ref[:, c:c + chunk] = _dot(hn, w_ref[:, c:c + chunk])
    qm_ref[...] = _dot(hn, w_ref[:, tok_width:]).astype(BF16)


def _inproj_pool(x2d, g, w_bf16, tok_width):
    t, d_model = x2d.shape
    n = w_bf16.shape[1]
    return pl.pallas_call(
        functools.partial(_inproj_pool_kernel, tok_width=tok_width),
        out_shape=(jax.ShapeDtypeStruct((t, tok_width), F32),
                   jax.ShapeDtypeStruct((t, n - tok_width), BF16)),
        grid=(t // ROW_TILE,),
        in_specs=[
            pl.BlockSpec((ROW_TILE, d_model), lambda i: (i, 0)),
            pl.BlockSpec((1, d_model), lambda i: (0, 0)),
            _resident((d_model, n), lambda i: (0, 0)),
        ],
        out_specs=(pl.BlockSpec((ROW_TILE, tok_width), lambda i: (i, 0)),
                   pl.BlockSpec((ROW_TILE, n - tok_width), lambda i: (i, 0))),
        compiler_params=_params(1),
        name="inproj_pool",
    )(x2d, g, w_bf16)


def _pool_group(u_ref, gw_ref, sc_ref, o_ref, a_ref, b_ref, *, window, seq):
    gwid = u_ref.shape[2]
    rows = seq + 2 * POOL_PAD
    zeros_pad = jnp.zeros((POOL_PAD, gwid), F32)
    a_ref[0:POOL_PAD, :] = zeros_pad
    b_ref[0:POOL_PAD, :] = zeros_pad
    a_ref[POOL_PAD:POOL_PAD + seq, :] = u_ref[0]
    a_ref[POOL_PAD + seq:rows, :] = zeros_pad
    src, dst = a_ref, b_ref
    shift = 1
    while shift < window:
        dst[POOL_PAD:rows, :] = src[POOL_PAD - shift:rows - shift, :] + src[POOL_PAD:rows, :]
        src, dst = dst, src
        shift *= 2
    off = POOL_PAD + window // 2 - 1
    win = src[off:off + seq, :]
    t = lax.broadcasted_iota(I32, (seq, 1), 0)
    lo = jnp.maximum(t - window // 2, 0)
    hi = jnp.minimum(t + window // 2 - 1, seq - 1)
    cnt = (hi - lo + 1).astype(F32)
    pooled = (win / cnt - u_ref[0]).astype(BF16)
    o_ref[0] = (_dot(pooled, gw_ref[0]) * sc_ref[0]).astype(BF16)


def _pool_kernel(u_ref, gw_ref, sc_ref, o_ref, a_ref, b_ref, *, seq):
    g = pl.program_id(1)
    for k, window in enumerate(POOL_WINDOWS):
        @pl.when(g == k)
        def _():
            _pool_group(u_ref, gw_ref, sc_ref, o_ref, a_ref, b_ref, window=window, seq=seq)


def _pool_mixer(u3d, gw_bf16, scale3d):
    b, seq, tok_width = u3d.shape
    n_groups, gwid, _ = gw_bf16.shape
    assert n_groups == len(POOL_WINDOWS) and n_groups * gwid == tok_width
    return pl.pallas_call(
        functools.partial(_pool_kernel, seq=seq),
        out_shape=jax.ShapeDtypeStruct((b, seq, tok_width), BF16),
        grid=(b, n_groups),
        in_specs=[
            pl.BlockSpec((1, seq, gwid), lambda i, g: (i, 0, g)),
            pl.BlockSpec((1, gwid, gwid), lambda i, g: (g, 0, 0)),
            pl.BlockSpec((1, 1, gwid), lambda i, g: (g, 0, 0)),
        ],
        out_specs=pl.BlockSpec((1, seq, gwid), lambda i, g: (i, 0, g)),
        scratch_shapes=[pltpu.VMEM((seq + 2 * POOL_PAD, gwid), F32),
                        pltpu.VMEM((seq + 2 * POOL_PAD, gwid), F32)],
        compiler_params=_params(2),
        name="pool_mixer",
    )(u3d, gw_bf16, scale3d)


def _xattn_kernel(q_ref, kv_ref, o_ref):
    scale = XA_HEAD_DIM ** -0.5
    for h in range(XA_HEADS):
        lo = h * XA_HEAD_DIM
        q = q_ref[:, lo:lo + XA_HEAD_DIM]
        k = kv_ref[0, :, lo:lo + XA_HEAD_DIM]
        v = kv_ref[0, :, XA_WIDTH + lo:XA_WIDTH + lo + XA_HEAD_DIM]
        s = _dot_nt(q, k) * scale
        m = jnp.max(s, axis=-1, keepdims=True)
        p = jnp.exp(s - m)
        den = jnp.sum(p, axis=-1, keepdims=True)
        o_ref[:, lo:lo + XA_HEAD_DIM] = (_dot(p.astype(BF16), v) / den).astype(BF16)


def _mem_xattn(qm2d, memkv_layer, seq):
    t = qm2d.shape[0]
    tiles_per_seq = seq // ROW_TILE
    return pl.pallas_call(
        _xattn_kernel,
        out_shape=jax.ShapeDtypeStruct((t, XA_WIDTH), BF16),
        grid=(t // ROW_TILE,),
        in_specs=[
            pl.BlockSpec((ROW_TILE, XA_WIDTH), lambda i: (i, 0)),
            pl.BlockSpec((1, MEM_LEN, 2 * XA_WIDTH), lambda i: (i // tiles_per_seq, 0, 0)),
        ],
        out_specs=pl.BlockSpec((ROW_TILE, XA_WIDTH), lambda i: (i, 0)),
        compiler_params=_params(1),
        name="mem_xattn",
    )(qm2d, memkv_layer)


def _outproj_kernel(tok_ref, mo_ref, x_ref, w_ref, g_ref, rw_ref,
                    x1_ref, h_ref, afft_ref, aff_ref, wcat_ref, x1prev_ref,
                    *, tok_width, n_experts):
    @pl.when(pl.program_id(0) == 0)
    def _():
        rw = rw_ref[...]
        w_hi = rw.astype(BF16)
        wcat_ref[:, 0:LANES] = w_hi
        wcat_ref[:, LANES:2 * LANES] = (rw - w_hi.astype(F32)).astype(BF16)
        x1prev_ref[...] = jnp.zeros_like(x1prev_ref)

    hn = _rmsnorm_rows(x1prev_ref[...], g_ref[...])
    h_prev = hn.astype(BF16)
    h_lo = (hn - h_prev.astype(F32)).astype(BF16)
    r = _dot(h_prev, wcat_ref[...]) + _dot(h_lo, wcat_ref[...])
    logits = r[:, 0:LANES] + r[:, LANES:2 * LANES]
    lt = logits.T[0:n_experts, :]
    m = jnp.max(lt, axis=0, keepdims=True)
    ex = jnp.exp(lt - m)
    afft = ex / jnp.sum(ex, axis=0, keepdims=True)
    afft_ref[0] = afft
    padded = jnp.concatenate(
        [afft, jnp.zeros((LANES - n_experts, afft.shape[1]), F32)], axis=0)
    aff = padded.T
    aff_ref[...] = aff
    h_ref[...] = _pack_row_words(h_prev, _pack_gate_lanes(aff))

    y = _dot(tok_ref[...], w_ref[0:tok_width, :]) + _dot(mo_ref[...], w_ref[tok_width:, :])
    x1 = x_ref[...] + y
    x1_ref[...] = x1
    x1prev_ref[...] = x1


def _outproj(tok2d, mo2d, x2d, w_bf16, g, rw_pad, n_experts, seq):
    t, d_model = x2d.shape
    tok_width = tok2d.shape[1]
    tiles_per_seq = seq // ROW_TILE
    n_tiles = t // ROW_TILE

    def cur(i):
        return jnp.minimum(i, n_tiles - 1)

    def prev(i):
        return jnp.maximum(i - 1, 0)

    return pl.pallas_call(
        functools.partial(_outproj_kernel, tok_width=tok_width, n_experts=n_experts),
        out_shape=(jax.ShapeDtypeStruct((t, d_model), F32),
                   jax.ShapeDtypeStruct((t, d_model // 2 + LANES), U32),
                   jax.ShapeDtypeStruct((t // seq, n_experts, seq), F32),
                   jax.ShapeDtypeStruct((t, LANES), F32)),
        grid=(n_tiles + 1,),
        in_specs=[
            pl.BlockSpec((ROW_TILE, tok_width), lambda i: (cur(i), 0)),
            pl.BlockSpec((ROW_TILE, mo2d.shape[1]), lambda i: (cur(i), 0)),
            pl.BlockSpec((ROW_TILE, d_model), lambda i: (cur(i), 0)),
            _resident((d_model, d_model), lambda i: (0, 0)),
            pl.BlockSpec((1, d_model), lambda i: (0, 0)),
            _resident((d_model, LANES), lambda i: (0, 0)),
        ],
        out_specs=(pl.BlockSpec((ROW_TILE, d_model), lambda i: (cur(i), 0)),
                   pl.BlockSpec((ROW_TILE, d_model // 2 + LANES), lambda i: (prev(i), 0)),
                   pl.BlockSpec((1, n_experts, ROW_TILE),
                                lambda i: (prev(i) // tiles_per_seq, 0, prev(i) % tiles_per_seq)),
                   pl.BlockSpec((ROW_TILE, LANES), lambda i: (prev(i), 0))),
        scratch_shapes=[pltpu.VMEM((d_model, 2 * LANES), BF16),
                        pltpu.VMEM((ROW_TILE, d_model), F32)],
        compiler_params=_params(1),
        name="outproj_router",
    )(tok2d, mo2d, x2d, w_bf16, g, rw_pad)


def _strict_triangle(n, lower):
    r = lax.broadcasted_iota(I32, (n, n), 0)
    c = lax.broadcasted_iota(I32, (n, n), 1)
    return jnp.where((c < r) if lower else (r < c), 1.0, 0.0).astype(BF16)


def _prefix_rows(mask_f32):
    s, l = mask_f32.shape
    tri = _strict_triangle(PREFIX_CHUNK, lower=True)
    carry = jnp.zeros((1, l), F32)
    out = []
    for c in range(0, s, PREFIX_CHUNK):
        m = mask_f32[c:c + PREFIX_CHUNK, :]
        out.append(_dot(tri, m.astype(BF16)) + carry)
        carry = carry + jnp.sum(m, axis=0, keepdims=True)
    return jnp.concatenate(out, axis=0)


def _prefix_lanes(mask_f32):
    e, s = mask_f32.shape
    tri = _strict_triangle(PREFIX_CHUNK, lower=False)
    carry = jnp.zeros((e, 1), F32)
    out = []
    for c in range(0, s, PREFIX_CHUNK):
        m = mask_f32[:, c:c + PREFIX_CHUNK]
        out.append(_dot(m.astype(BF16), tri) + carry)
        carry = carry + jnp.sum(m, axis=1, keepdims=True)
    return jnp.concatenate(out, axis=1)


def _select_slots(key, thr, need, prefix_fn):
    return _select_slots_and_counts(key, thr, need, prefix_fn)[0]


def _select_slots_and_counts(key, thr, need, prefix_fn):
    gt = jnp.where(key > thr, 1.0, 0.0)
    eq = jnp.where(key == thr, 1.0, 0.0)
    eq_rank = prefix_fn(eq)
    sel = gt + eq * jnp.where(eq_rank < need, 1.0, 0.0)
    pos = prefix_fn(sel)
    return jnp.where(sel > 0.5, pos, -1.0), pos


GATE_GROUP = 16
GATE_PIECES = 3


def _pack_gate_lanes(aff):
    hi = aff.astype(BF16).astype(F32)
    r1 = aff - hi
    mid = r1.astype(BF16).astype(F32)
    lo = (r1 - mid).astype(BF16).astype(F32)
    packed = hi + pltpu.roll(mid, GATE_GROUP, 1) + pltpu.roll(lo, 2 * GATE_GROUP, 1)
    return packed.astype(BF16)


def _unpack_gate(tail, e):
    lane = lax.broadcasted_iota(I32, (1, LANES), 1)
    mine = ((lane & (GATE_GROUP - 1)) == e) & (lane < GATE_PIECES * GATE_GROUP)
    return jnp.sum(jnp.where(mine, tail.astype(F32), 0.0), axis=1, keepdims=True)


def _pack_row_words(h, gate_tile):
    rows, d_model = h.shape
    half = d_model // 2
    hi = jnp.concatenate([h[:, 0:half], gate_tile], axis=1).astype(F32)
    lo = jnp.concatenate([h[:, half:], jnp.zeros((rows, LANES), BF16)], axis=1).astype(F32)
    return pltpu.bitcast(hi, U32) | (pltpu.bitcast(lo, U32) >> 16)


def _unpack_row_words(words):
    hi = pltpu.bitcast(words & jnp.uint32(0xFFFF0000), F32)
    lo = pltpu.bitcast(words << 16, F32)
    return hi, lo


def _route_kernel(afft_ref, posm_ref, thr_ref, need_ref, starts_ref, *, cap):
    n_experts, seq = afft_ref.shape[1], afft_ref.shape[2]
    key = pltpu.bitcast(afft_ref[0], I32)
    thr = jnp.zeros((n_experts, 1), I32)
    for bit in range(30, -1, -1):
        cand = thr | (1 << bit)
        cnt = jnp.sum(jnp.where(key >= cand, 1.0, 0.0), axis=1, keepdims=True)
        thr = jnp.where(cnt >= cap, cand, thr)
    n_gt = jnp.sum(jnp.where(key > thr, 1.0, 0.0), axis=1, keepdims=True)
    need = cap - n_gt
    thr_ref[0] = thr
    need_ref[0] = need
    posm, before = _select_slots_and_counts(key, thr, need, _prefix_lanes)
    posm_ref[0] = posm.astype(I32)
    starts_ref[0] = jnp.concatenate(
        [before[:, r:r + 1] for r in range(0, seq, COMBINE_ROWS)], axis=1).astype(I32)


def _route(afft, cap):
    b, n_experts, seq = afft.shape
    n_tiles = seq // COMBINE_ROWS
    return pl.pallas_call(
        functools.partial(_route_kernel, cap=cap),
        out_shape=(jax.ShapeDtypeStruct((b, n_experts, seq), I32),
                   jax.ShapeDtypeStruct((b, n_experts, 1), I32),
                   jax.ShapeDtypeStruct((b, n_experts, 1), F32),
                   jax.ShapeDtypeStruct((b, n_experts, n_tiles), I32)),
        grid=(b,),
        in_specs=[pl.BlockSpec((1, n_experts, seq), lambda i: (i, 0, 0))],
        out_specs=(pl.BlockSpec((1, n_experts, seq), lambda i: (i, 0, 0)),
                   pl.BlockSpec((1, n_experts, 1), lambda i: (i, 0, 0)),
                   pl.BlockSpec((1, n_experts, 1), lambda i: (i, 0, 0)),
                   pl.BlockSpec((1, n_experts, n_tiles), lambda i: (i, 0, 0))),
        compiler_params=_params(1),
        name="expert_route",
    )(afft)


SC_LANES = 16
SC_GATHER_ROWS = 64


def _sc_expert_gather(posm2d, h_words, n_experts, e_offset, n_e, b, seq, cap):
    width = h_words.shape[1]
    info = plsc.get_sparse_core_info()
    n_cores, n_subcores = info.num_cores, info.num_subcores
    n_workers = n_cores * n_subcores
    assert info.num_lanes == SC_LANES and (b * n_e) % n_workers == 0
    pairs_per_worker = (b * n_e) // n_workers
    mesh = plsc.VectorSubcoreMesh(core_axis_name="c", subcore_axis_name="s")

    @functools.partial(
        pl.kernel, mesh=mesh,
        out_type=jax.ShapeDtypeStruct((n_e * b * cap, width), U32),
        compiler_params=pltpu.CompilerParams(needs_layout_passes=False),
        scratch_types=[
            pltpu.VMEM((seq,), I32),
            pltpu.VMEM((cap,), I32),
            pltpu.VMEM((SC_GATHER_ROWS, width), U32),
            pltpu.SemaphoreType.DMA,
        ],
        name="sc_expert_gather",
    )
    def gather(posm_hbm, h_hbm, out_hbm, pos_v, idx_v, rows_v, sem):
        wid = lax.axis_index("s") * n_cores + lax.axis_index("c")
        for p in range(pairs_per_worker):
            pair = wid * pairs_per_worker + p
            bi = pair // n_e
            e = pair - bi * n_e
            pltpu.sync_copy(posm_hbm.at[bi * n_experts + e_offset + e], pos_v)

            @pl.loop(0, seq, step=SC_LANES)
            def _(t0):
                slots = pos_v[pl.ds(t0, SC_LANES)]
                rows = lax.iota(I32, SC_LANES) + (t0 + bi * seq)
                plsc.store_scatter(idx_v, [slots], rows, mask=slots >= 0)

            out_base = (e * b + bi) * cap
            for c in range(cap // SC_GATHER_ROWS):
                chunk = idx_v.at[pl.ds(c * SC_GATHER_ROWS, SC_GATHER_ROWS)]
                pltpu.async_copy(h_hbm.at[chunk], rows_v, sem).wait()
                pltpu.sync_copy(rows_v, out_hbm.at[pl.ds(out_base + c * SC_GATHER_ROWS, SC_GATHER_ROWS)])

    return gather(posm2d, h_words).reshape(n_e, b, cap, width)


EXPERT_ROWS = 1024
EXPERT_FTILE = 256


EXPERT_WBUFS = 2


def _expert_kernel(xg_ref, wg_hbm, wu_hbm, wd_hbm, o_ref,
                   x_ref, g_ref, hact_ref, wg_full, wu_full, wd_full, wg_buf, wu_buf, wd_buf, sem,
                   *, layer, e_offset, n_ftiles):
    e = pl.program_id(0)
    m = pl.program_id(1)
    n_e = pl.num_programs(0)
    nb, cap, width = xg_ref.shape[1], xg_ref.shape[2], xg_ref.shape[3]
    half = width - LANES
    d_model = 2 * half
    rows = nb * cap
    tf = EXPERT_FTILE
    assert n_ftiles % EXPERT_WBUFS == 0

    def tile_copies(expert, f):
        ge = e_offset + expert
        slot = f % EXPERT_WBUFS
        return (
            pltpu.make_async_copy(wg_hbm.at[layer, ge, :, pl.ds(f * tf, tf)], wg_buf.at[slot], sem.at[0, slot]),
            pltpu.make_async_copy(wu_hbm.at[layer, ge, :, pl.ds(f * tf, tf)], wu_buf.at[slot], sem.at[1, slot]),
            pltpu.make_async_copy(wd_hbm.at[layer, ge, pl.ds(f * tf, tf), :], wd_buf.at[slot], sem.at[2, slot]),
        )

    def start(expert, f):
        for cp in tile_copies(expert, f):
            cp.start()

    @pl.when((e == 0) & (m == 0))
    def _():
        for f in range(EXPERT_WBUFS):
            start(e, f)

    for i in range(nb):
        r0 = i * cap
        hi, lo = _unpack_row_words(xg_ref[0, i])
        x_ref[r0:r0 + cap, 0:half] = hi[:, 0:half].astype(BF16)
        x_ref[r0:r0 + cap, half:d_model] = lo[:, 0:half].astype(BF16)
        g_ref[r0:r0 + cap, :] = jnp.broadcast_to(
            _unpack_gate(hi[:, half:width], e_offset + e), (cap, LANES))

    def receive(f):
        slot = f % EXPERT_WBUFS
        for cp in tile_copies(e, f):
            cp.wait()
        wg_full[:, f * tf:(f + 1) * tf] = wg_buf[slot].astype(BF16)
        wu_full[:, f * tf:(f + 1) * tf] = wu_buf[slot].astype(BF16)
        wd_full[f * tf:(f + 1) * tf, :] = wd_buf[slot].astype(BF16)
        ahead = f + EXPERT_WBUFS
        if ahead < n_ftiles:
            start(e, ahead)
        else:
            @pl.when(e + 1 < n_e)
            def _():
                start(e + 1, ahead - n_ftiles)

    def body(first_group):
        x = x_ref[...]
        for f in range(n_ftiles):
            if first_group:
                receive(f)
            a = _dot(x, wg_full[:, f * tf:(f + 1) * tf])
            u = _dot(x, wu_full[:, f * tf:(f + 1) * tf])
            hact_ref[:, f * tf:(f + 1) * tf] = (a * jax.nn.sigmoid(a) * u).astype(BF16)
        chunk = 512
        for c in range(0, d_model, chunk):
            y = _dot(hact_ref[...], wd_full[:, c:c + chunk])
            for j in range(0, chunk, LANES):
                o_ref[:, 0, :, c + j:c + j + LANES] = (
                    (y[:, j:j + LANES] * g_ref[...]).astype(BF16).reshape(nb, cap, LANES))

    @pl.when(m == 0)
    def _():
        body(True)

    @pl.when(m != 0)
    def _():
        body(False)


def _experts(xg, w_gate, w_up, w_down, layer, e_offset):
    n_e, b, cap, width = xg.shape
    d_model = 2 * (width - LANES)
    d_expert = w_gate.shape[3]
    nb = EXPERT_ROWS // cap
    n_ftiles = d_expert // EXPERT_FTILE
    return pl.pallas_call(
        functools.partial(_expert_kernel, layer=layer, e_offset=e_offset, n_ftiles=n_ftiles),
        out_shape=jax.ShapeDtypeStruct((b, n_e, cap, d_model), BF16),
        grid=(n_e, b // nb),
        in_specs=[
            pl.BlockSpec((1, nb, cap, width), lambda e, m: (e, m, 0, 0)),
            pl.BlockSpec(memory_space=pl.ANY),
            pl.BlockSpec(memory_space=pl.ANY),
            pl.BlockSpec(memory_space=pl.ANY),
        ],
        out_specs=pl.BlockSpec((nb, 1, cap, d_model), lambda e, m: (m, e, 0, 0)),
        scratch_shapes=[pltpu.VMEM((EXPERT_ROWS, d_model), BF16),
                        pltpu.VMEM((EXPERT_ROWS, LANES), F32),
                        pltpu.VMEM((EXPERT_ROWS, d_expert), BF16),
                        pltpu.VMEM((d_model, d_expert), BF16),
                        pltpu.VMEM((d_model, d_expert), BF16),
                        pltpu.VMEM((d_expert, d_model), BF16),
                        pltpu.VMEM((EXPERT_WBUFS, d_model, EXPERT_FTILE), F32),
                        pltpu.VMEM((EXPERT_WBUFS, d_model, EXPERT_FTILE), F32),
                        pltpu.VMEM((EXPERT_WBUFS, EXPERT_FTILE, d_model), F32),
                        pltpu.SemaphoreType.DMA((3, EXPERT_WBUFS))],
        compiler_params=_params(2),
        name="experts",
    )(xg, w_gate, w_up, w_down)


COMBINE_ROWS = 256
COMBINE_WINDOW = 128
BF16_ROWS = 16


def _combine_kernel(starts_ref, aff_ref, thr_ref, need_ref, x1_ref, *rest, cap, n_experts, final_norm):
    n_y = len(rest) - (4 if final_norm else 3)
    y_refs = rest[:n_y]
    g_ref = rest[n_y] if final_norm else None
    o_ref, post_ref, pfull_ref = rest[-3], rest[-2], rest[-1]
    b = pl.program_id(0)
    t = pl.program_id(1)
    n_t = pl.num_programs(1)
    rows, win = COMBINE_ROWS, COMBINE_WINDOW
    experts_per_group = n_experts // n_y

    @pl.when(t == 0)
    def _():
        key = pltpu.bitcast(aff_ref[...], I32)
        post_ref[...] = _select_slots(key, thr_ref[0], need_ref[0], _prefix_rows)

    posm = post_ref[pl.ds(pl.multiple_of(t * rows, rows), rows), :]
    base = (b * (n_t + 1) + t) * n_experts
    wstart, ok = [], None
    for e in range(n_experts):
        first = starts_ref[base + e]
        end = starts_ref[base + n_experts + e]
        w0 = jnp.minimum((first // BF16_ROWS) * BF16_ROWS, cap - win)
        fits = end - w0 <= win
        wstart.append(w0)
        ok = fits if ok is None else jnp.logical_and(ok, fits)

    def finish(acc):
        if final_norm:
            acc = _rmsnorm_rows(acc, g_ref[...])
        o_ref[0] = acc

    @pl.when(ok)
    def _():
        lane = lax.broadcasted_iota(I32, (1, win), 1).astype(F32)
        acc = x1_ref[0]
        for pair in range(n_experts // 2):
            onehots, windows = [], []
            for e in (2 * pair, 2 * pair + 1):
                g, el = divmod(e, experts_per_group)
                rel = posm[:, e:e + 1] - wstart[e].astype(F32)
                onehots.append(jnp.where(rel == lane, 1.0, 0.0).astype(BF16))
                r0 = pl.multiple_of(el * cap + wstart[e], BF16_ROWS)
                windows.append(y_refs[g][0, pl.ds(r0, win), :])
            acc = acc + _dot(jnp.concatenate(onehots, axis=1), jnp.concatenate(windows, axis=0))
        finish(acc)

    @pl.when(jnp.logical_not(ok))
    def _():
        slot = lax.broadcasted_iota(I32, (1, cap), 1).astype(F32)
        for e in range(n_experts):
            pfull_ref[:, e * cap:(e + 1) * cap] = jnp.where(posm[:, e:e + 1] == slot, 1.0, 0.0).astype(BF16)
        acc = x1_ref[0]
        k0 = 0
        for y_ref in y_refs:
            k1 = k0 + y_ref.shape[1]
            acc = acc + _dot(pfull_ref[:, k0:k1], y_ref[0])
            k0 = k1
        finish(acc)


def _combine(starts, aff2d, thr_row, need_row, x1_3d, y_groups, cap, n_experts, final_g=None):
    b, seq, d_model = x1_3d.shape
    assert sum(y.shape[1] for y in y_groups) == n_experts * cap and cap >= COMBINE_WINDOW
    final_norm = final_g is not None
    y_specs = [pl.BlockSpec((1, y.shape[1], d_model), lambda i, t, s: (i, 0, 0)) for y in y_groups]
    g_specs = [pl.BlockSpec((1, d_model), lambda i, t, s: (0, 0))] if final_norm else []
    g_args = [final_g] if final_norm else []
    return pl.pallas_call(
        functools.partial(_combine_kernel, cap=cap, n_experts=n_experts, final_norm=final_norm),
        out_shape=jax.ShapeDtypeStruct((b, seq, d_model), F32),
        grid_spec=pltpu.PrefetchScalarGridSpec(
            num_scalar_prefetch=1,
            grid=(b, seq // COMBINE_ROWS),
            in_specs=[
                pl.BlockSpec((seq, LANES), lambda i, t, s: (i, 0)),
                pl.BlockSpec((1, 1, LANES), lambda i, t, s: (i, 0, 0)),
                pl.BlockSpec((1, 1, LANES), lambda i, t, s: (i, 0, 0)),
                pl.BlockSpec((1, COMBINE_ROWS, d_model), lambda i, t, s: (i, t, 0)),
            ] + y_specs + g_specs,
            out_specs=pl.BlockSpec((1, COMBINE_ROWS, d_model), lambda i, t, s: (i, t, 0)),
            scratch_shapes=[pltpu.VMEM((seq, LANES), F32),
                            pltpu.VMEM((COMBINE_ROWS, n_experts * cap), BF16)],
        ),
        compiler_params=_params(2),
        name="combine",
    )(starts, aff2d, thr_row, need_row, x1_3d, *y_groups, *g_args)


EXPERT_GROUPS = 2


def _moe(afft, aff2d, h_words, x1_2d, w_gate, w_up, w_down, layer, b, seq, final_g=None):
    n_experts = afft.shape[1]
    assert n_experts <= GATE_GROUP and n_experts % EXPERT_GROUPS == 0
    d_model = x1_2d.shape[1]
    cap = CAPACITY_FACTOR * seq // n_experts
    n_e = n_experts // EXPERT_GROUPS
    posm, thr, need, tile_starts = _route(afft, cap)
    starts = jnp.concatenate([jnp.swapaxes(tile_starts, 1, 2),
                              jnp.full((b, 1, n_experts), cap, I32)], axis=1).reshape(-1)
    posm2d = posm.reshape(b * n_experts, seq)
    xgs = [_sc_expert_gather(posm2d, h_words, n_experts, g * n_e, n_e, b, seq, cap)
           for g in range(EXPERT_GROUPS)]
    ys = [_experts(xg, w_gate, w_up, w_down, layer, g * n_e).reshape(b, n_e * cap, d_model)
          for g, xg in enumerate(xgs)]
    pad = LANES - n_experts
    thr_row = jnp.pad(thr.reshape(b, 1, n_experts), ((0, 0), (0, 0), (0, pad)),
                      constant_values=np.iinfo(np.int32).max)
    need_row = jnp.pad(need.reshape(b, 1, n_experts), ((0, 0), (0, 0), (0, pad)))
    return _combine(starts, aff2d, thr_row, need_row, x1_2d.reshape(b, seq, d_model), ys, cap,
                    n_experts, final_g)


def _rotary_tile(t, cos, sin_lo, sin_hi):
    half = ROT_DIM // 2
    return t * cos + pltpu.roll(t, LANES - half, 1) * sin_lo + pltpu.roll(t, half, 1) * sin_hi


def _inproj_attn_kernel(x_ref, g_ref, w_ref, pos_ref, rot_ref,
                        q_ref, k_ref, v_ref, qm_ref, *, tok_width, kv_width):
    hn = _rmsnorm_rows(x_ref[...], g_ref[...]).astype(BF16)
    ang = pos_ref[...].astype(F32) * rot_ref[0:1, :]
    cos = jnp.cos(ang)
    sin = jnp.sin(ang)
    sin_lo = sin * rot_ref[1:2, :]
    sin_hi = sin * rot_ref[2:3, :]
    qscale = HEAD_DIM ** -0.5
    chunk = 512
    for c in range(0, tok_width, chunk):
        pc = _dot(hn, w_ref[:, c:c + chunk])
        for j in range(0, chunk, LANES):
            rot = _rotary_tile(pc[:, j:j + LANES], cos, sin_lo, sin_hi)
            q_ref[:, c + j:c + j + LANES] = (rot * qscale).astype(BF16)
    kv = _dot(hn, w_ref[:, tok_width:tok_width + 2 * kv_width])
    k01 = _rotary_tile(kv[:, 0:LANES], cos, sin_lo, sin_hi)
    k2x = _rotary_tile(kv[:, LANES:2 * LANES], cos, sin_lo, sin_hi)
    k_ref[0, 0] = k01[:, 0:HEAD_DIM].astype(BF16)
    k_ref[0, 1] = k01[:, HEAD_DIM:LANES].astype(BF16)
    k_ref[0, 2] = k2x[:, 0:HEAD_DIM].astype(BF16)
    for hh in range(kv_width // HEAD_DIM):
        lo = kv_width + hh * HEAD_DIM
        v_ref[0, hh] = kv[:, lo:lo + HEAD_DIM].astype(BF16)
    qm_ref[...] = _dot(hn, w_ref[:, tok_width + 2 * kv_width:]).astype(BF16)


def _inproj_attn(x2d, g, w_bf16, pos2d, rot_rows, tok_width, kv_width, seq):
    t, d_model = x2d.shape
    n = w_bf16.shape[1]
    n_kv = kv_width // HEAD_DIM
    assert n_kv == 3 and kv_width + HEAD_DIM == 2 * LANES
    tiles_per_seq = seq // ROW_TILE
    kv_spec = pl.BlockSpec((1, n_kv, ROW_TILE, HEAD_DIM),
                           lambda i: (i // tiles_per_seq, 0, i % tiles_per_seq, 0))
    return pl.pallas_call(
        functools.partial(_inproj_attn_kernel, tok_width=tok_width, kv_width=kv_width),
        out_shape=(jax.ShapeDtypeStruct((t, tok_width), BF16),
                   jax.ShapeDtypeStruct((t // seq, n_kv, seq, HEAD_DIM), BF16),
                   jax.ShapeDtypeStruct((t // seq, n_kv, seq, HEAD_DIM), BF16),
                   jax.ShapeDtypeStruct((t, n - tok_width - 2 * kv_width), BF16)),
        grid=(t // ROW_TILE,),
        in_specs=[
            pl.BlockSpec((ROW_TILE, d_model), lambda i: (i, 0)),
            pl.BlockSpec((1, d_model), lambda i: (0, 0)),
            _resident((d_model, n), lambda i: (0, 0)),
            pl.BlockSpec((ROW_TILE, 1), lambda i: (i, 0)),
            pl.BlockSpec((8, LANES), lambda i: (0, 0)),
        ],
        out_specs=(pl.BlockSpec((ROW_TILE, tok_width), lambda i: (i, 0)),
                   kv_spec, kv_spec,
                   pl.BlockSpec((ROW_TILE, n - tok_width - 2 * kv_width), lambda i: (i, 0))),
        compiler_params=_params(1),
        name="inproj_attn",
    )(x2d, g, w_bf16, pos2d, rot_rows)


def _wattn_kernel(sink_ref, q_ref, kp_ref, kc_ref, kn_ref, vp_ref, vc_ref, vn_ref, o_ref,
                  valid_ref, kpad_ref, vpad_ref, s_ref, p_ref, inv_ref, *, seq):
    n = pl.program_id(1)
    n_kv = kc_ref.shape[1]
    pairs = GQA_RATIO // 2
    half_rows = pairs * BLOCK
    qi = lax.broadcasted_iota(I32, (BLOCK, 3 * BLOCK), 0)
    kj = lax.broadcasted_iota(I32, (BLOCK, 3 * BLOCK), 1)
    first = jnp.maximum(qi, BLOCK - n * BLOCK)
    last = jnp.minimum(qi + 2 * WINDOW, seq + BLOCK - 1 - n * BLOCK)
    valid_ref[...] = jnp.where(((kj - first) | (last - kj)) >= 0, 1.0, 0.0)
    zeros = jnp.zeros((3 * BLOCK, HEAD_DIM), BF16)
    ones_col = jnp.where(lax.broadcasted_iota(I32, (3 * BLOCK, HEAD_DIM), 1) == 0, 1.0, 0.0).astype(BF16)
    low_half = lax.broadcasted_iota(I32, (1, LANES), 1) < HEAD_DIM
    for hk in range(n_kv):
        kw = jnp.concatenate([kp_ref[0, hk], kc_ref[0, hk], kn_ref[0, hk]], axis=0)
        vw = jnp.concatenate([vp_ref[0, hk], vc_ref[0, hk], vn_ref[0, hk]], axis=0)
        kpad_ref[2 * hk] = jnp.concatenate([kw, zeros], axis=1)
        kpad_ref[2 * hk + 1] = jnp.concatenate([zeros, kw], axis=1)
        vpad_ref[2 * hk] = jnp.concatenate([vw, ones_col], axis=1)
        vpad_ref[2 * hk + 1] = jnp.concatenate([ones_col, vw], axis=1)
        tile0 = hk * pairs
        qs = jnp.concatenate(
            [q_ref[0, :, (tile0 + j) * LANES:(tile0 + j + 1) * LANES] for j in range(pairs)], axis=0)
        s_ref[hk, 0:half_rows, :] = _dot_nt(qs, kpad_ref[2 * hk])
        s_ref[hk, half_rows:2 * half_rows, :] = _dot_nt(qs, kpad_ref[2 * hk + 1])
    for hk in range(n_kv):
        for c in range(GQA_RATIO):
            j, odd = c % pairs, c // pairs
            r = c * BLOCK
            s = jnp.concatenate([
                jnp.where(valid_ref[:, 0:BLOCK] > 0.5, s_ref[hk, r:r + BLOCK, 0:BLOCK], NEG_INF),
                s_ref[hk, r:r + BLOCK, BLOCK:2 * BLOCK],
                jnp.where(valid_ref[:, 2 * BLOCK:] > 0.5, s_ref[hk, r:r + BLOCK, 2 * BLOCK:], NEG_INF),
            ], axis=1)
            sk = sink_ref[hk * GQA_RATIO + 2 * j + odd]
            m = jnp.maximum(jnp.max(s, axis=-1, keepdims=True), sk)
            p_ref[hk, r:r + BLOCK, :] = jnp.exp(s - m).astype(BF16)
            inv_ref[hk, j * BLOCK:(j + 1) * BLOCK, odd * HEAD_DIM:(odd + 1) * HEAD_DIM] = (
                jnp.broadcast_to(jnp.exp(sk - m), (BLOCK, HEAD_DIM)))
    for hk in range(n_kv):
        pv_even = _dot(p_ref[hk, 0:half_rows, :], vpad_ref[2 * hk])
        pv_odd = _dot(p_ref[hk, half_rows:2 * half_rows, :], vpad_ref[2 * hk + 1])
        den = jnp.where(low_half, pv_even[:, HEAD_DIM:HEAD_DIM + 1], pv_odd[:, 0:1]) + inv_ref[hk]
        o = jnp.where(low_half, pv_even, pv_odd) / den
        for j in range(pairs):
            lo = (hk * pairs + j) * LANES
            o_ref[0, :, lo:lo + LANES] = o[j * BLOCK:(j + 1) * BLOCK].astype(BF16)


def _window_attention(sink, q3d, k4d, v4d):
    b, seq, tok_width = q3d.shape
    n_kv = k4d.shape[1]
    nb = seq // BLOCK
    kv_block = (1, n_kv, BLOCK, HEAD_DIM)
    prev_spec = pl.BlockSpec(kv_block, lambda i, n: (i, 0, jnp.maximum(n - 1, 0), 0))
    cur_spec = pl.BlockSpec(kv_block, lambda i, n: (i, 0, n, 0))
    next_spec = pl.BlockSpec(kv_block, lambda i, n: (i, 0, jnp.minimum(n + 1, nb - 1), 0))
    return pl.pallas_call(
        functools.partial(_wattn_kernel, seq=seq),
        out_shape=jax.ShapeDtypeStruct((b, seq, tok_width), BF16),
        grid=(b, nb),
        in_specs=[
            pl.BlockSpec(memory_space=pltpu.SMEM),
            pl.BlockSpec((1, BLOCK, tok_width), lambda i, n: (i, n, 0)),
            prev_spec, cur_spec, next_spec, prev_spec, cur_spec, next_spec,
        ],
        out_specs=pl.BlockSpec((1, BLOCK, tok_width), lambda i, n: (i, n, 0)),
        scratch_shapes=[pltpu.VMEM((BLOCK, 3 * BLOCK), F32),
                        pltpu.VMEM((2 * n_kv, 3 * BLOCK, LANES), BF16),
                        pltpu.VMEM((2 * n_kv, 3 * BLOCK, LANES), BF16),
                        pltpu.VMEM((n_kv, GQA_RATIO * BLOCK, 3 * BLOCK), F32),
                        pltpu.VMEM((n_kv, GQA_RATIO * BLOCK, 3 * BLOCK), BF16),
                        pltpu.VMEM((n_kv, GQA_RATIO // 2 * BLOCK, LANES), F32)],
        compiler_params=_params(2),
        name="window_attention",
    )(sink, q3d, k4d, k4d, k4d, v4d, v4d, v4d)


def _rotary_rows(dtype=F32):
    half = ROT_DIM // 2
    inv_freq = ROPE_THETA ** (-jnp.arange(0, ROT_DIM, 2, dtype=jnp.float32) / ROT_DIM)
    lane = np.arange(LANES) % HEAD_DIM
    rotated = lane < ROT_DIM
    freq = jnp.where(jnp.asarray(rotated), inv_freq[jnp.asarray(lane % half)], 0.0)
    rows = jnp.zeros((8, LANES), dtype)
    rows = rows.at[0].set(freq)
    rows = rows.at[1].set(jnp.asarray(np.where(lane < half, -1.0, 0.0), dtype))
    rows = rows.at[2].set(jnp.asarray(np.where(rotated & (lane >= half), 1.0, 0.0), dtype))
    return rows


def kernel(x, mem, positions, norm_mix_g, norm_ffn_g, mem_norm_g, final_g, mem_w_kv,
           pool_w_in, pool_group_w, pool_scale, pool_w_out,
           attn_w_in, attn_sink, attn_w_out,
           router_w, exp_w_gate, exp_w_up, exp_w_down):
    b, seq, d_model = x.shape
    depth = norm_mix_g.shape[0]
    t = b * seq
    n_experts = router_w.shape[2]
    tok_width = pool_scale.shape[1]
    n_groups = pool_group_w.shape[1]
    kv_width = (attn_w_in.shape[2] - tok_width - XA_WIDTH) // 2
    assert seq % ROW_TILE == 0 and mem.shape[1] == MEM_LEN

    memkv = _memkv(mem.reshape(b * MEM_LEN, d_model), mem_norm_g.reshape(1, d_model),
                   mem_w_kv.astype(BF16))
    memkv = memkv.reshape(depth, b, MEM_LEN, 2 * XA_WIDTH)
    rw_pad = jnp.pad(router_w, ((0, 0), (0, 0), (0, LANES - n_experts)))
    pos2d = positions.reshape(t, 1)
    rot_rows = _rotary_rows()

    x2d = x.reshape(t, d_model)
    for layer in range(depth):
        j = layer // 2
        g_mix = norm_mix_g[layer].reshape(1, d_model)
        if layer % 2 == 0:
            u, qm = _inproj_pool(x2d, g_mix, pool_w_in[j].astype(BF16), tok_width)
            tok = _pool_mixer(u.reshape(b, seq, tok_width), pool_group_w[j].astype(BF16),
                              pool_scale[j].reshape(n_groups, 1, tok_width // n_groups))
            tok = tok.reshape(t, tok_width)
            w_out = pool_w_out[j]
        else:
            q, k, v, qm = _inproj_attn(x2d, g_mix, attn_w_in[j].astype(BF16), pos2d, rot_rows,
                                       tok_width, kv_width, seq)
            tok = _window_attention(attn_sink[j], q.reshape(b, seq, tok_width), k, v)
            tok = tok.reshape(t, tok_width)
            w_out = attn_w_out[j]
        mo = _mem_xattn(qm, memkv[layer], seq)
        x1, h, afft, aff = _outproj(tok, mo, x2d, w_out.astype(BF16),
                                    norm_ffn_g[layer].reshape(1, d_model), rw_pad[layer],
                                    n_experts, seq)
        last = layer == depth - 1
        x2 = _moe(afft, aff, h, x1, exp_w_gate, exp_w_up, exp_w_down, layer, b, seq,
                  final_g.reshape(1, d_model) if last else None)
        x2d = x2.reshape(t, d_model)
    return x2d.reshape(b, seq, d_model)
```

```python
import functools

import jax
import jax.numpy as jnp
import numpy as np
from jax import lax
from jax.experimental import pallas as pl
from jax.experimental.pallas import tpu as pltpu
from jax.experimental.pallas import tpu_sc as plsc

F32 = jnp.float32
BF16 = jnp.bfloat16
I32 = jnp.int32
U32 = jnp.uint32

EPS = 1e-6
MEM_LEN = 256
XA_HEADS = 4
XA_HEAD_DIM = 128
XA_WIDTH = XA_HEADS * XA_HEAD_DIM
POOL_WINDOWS = (2, 4, 8, 16)
HEAD_DIM = 64
GQA_RATIO = 8
WINDOW = 128
BLOCK = 128
ROPE_THETA = 500000.0
ROT_DIM = 16
NEG_INF = -1e30
CAPACITY_FACTOR = 2

LANES = 128
MIB = 1024 * 1024
VMEM_LIMIT_BYTES = 56 * MIB

ROW_TILE = 512
POOL_PAD = 16
PREFIX_CHUNK = 256


def _params(n_grid_dims, flags=None):
    return pltpu.CompilerParams(
        dimension_semantics=("arbitrary",) * n_grid_dims,
        vmem_limit_bytes=VMEM_LIMIT_BYTES,
        flags=flags,
    )


def _resident(block_shape, index_map):
    return pl.BlockSpec(block_shape, index_map, pipeline_mode=pl.Buffered(1))


def _rmsnorm_rows(x, g):
    return x * lax.rsqrt(jnp.mean(x * x, axis=-1, keepdims=True) + EPS) * g


def _dot(a, b):
    return jnp.dot(a, b, preferred_element_type=F32)


def _dot_nt(a, b):
    return lax.dot_general(a, b, (((1,), (1,)), ((), ())), preferred_element_type=F32)


def _memkv_kernel(mem_ref, g_ref, w_ref, o_ref):
    hn = _rmsnorm_rows(mem_ref[...], g_ref[...]).astype(BF16)
    o_ref[0] = _dot(hn, w_ref[0]).astype(BF16)


def _memkv(mem2d, g, w_bf16):
    depth, d_model, n = w_bf16.shape
    rows = mem2d.shape[0]
    return pl.pallas_call(
        _memkv_kernel,
        out_shape=jax.ShapeDtypeStruct((depth, rows, n), BF16),
        grid=(depth, rows // ROW_TILE),
        in_specs=[
            pl.BlockSpec((ROW_TILE, d_model), lambda l, i: (i, 0)),
            pl.BlockSpec((1, d_model), lambda l, i: (0, 0)),
            pl.BlockSpec((1, d_model, n), lambda l, i: (l, 0, 0)),
        ],
        out_specs=pl.BlockSpec((1, ROW_TILE, n), lambda l, i: (l, i, 0)),
        compiler_params=_params(2),
        name="memkv",
    )(mem2d, g, w_bf16)


def _inproj_pool_kernel(x_ref, g_ref, w_ref, u_ref, qm_ref, *, tok_width):
    hn = _rmsnorm_rows(x_ref[...], g_ref[...]).astype(BF16)
    chunk = 512
    for c in range(0, tok_width, chunk):
        u_ref[:, c:c + chunk] = _dot(hn, w_ref[:, c:c + chunk])
    qm_ref[...] = _dot(hn, w_ref[:, tok_width:]).astype(BF16)


def _inproj_pool(x2d, g, w_bf16, tok_width):
    t, d_model = x2d.shape
    n = w_bf16.shape[1]
    return pl.pallas_call(
        functools.partial(_inproj_pool_kernel, tok_width=tok_width),
        out_shape=(jax.ShapeDtypeStruct((t, tok_width), F32),
                   jax.ShapeDtypeStruct((t, n - tok_width), BF16)),
        grid=(t // ROW_TILE,),
        in_specs=[
            pl.BlockSpec((ROW_TILE, d_model), lambda i: (i, 0)),
            pl.BlockSpec((1, d_model), lambda i: (0, 0)),
            _resident((d_model, n), lambda i: (0, 0)),
        ],
        out_specs=(pl.BlockSpec((ROW_TILE, tok_width), lambda i: (i, 0)),
                   pl.BlockSpec((ROW_TILE, n - tok_width), lambda i: (i, 0))),
        compiler_params=_params(1),
        name="inproj_pool",
    )(x2d, g, w_bf16)


def _pool_group(u_ref, gw_ref, sc_ref, o_ref, a_ref, b_ref, *, window, seq):
    gwid = u_ref.shape[2]
    rows = seq + 2 * POOL_PAD
    zeros_pad = jnp.zeros((POOL_PAD, gwid), F32)
    a_ref[0:POOL_PAD, :] = zeros_pad
    b_ref[0:POOL_PAD, :] = zeros_pad
    a_ref[POOL_PAD:POOL_PAD + seq, :] = u_ref[0]
    a_ref[POOL_PAD + seq:rows, :] = zeros_pad
    src, dst = a_ref, b_ref
    shift = 1
    while shift < window:
        dst[POOL_PAD:rows, :] = src[POOL_PAD - shift:rows - shift, :] + src[POOL_PAD:rows, :]
        src, dst = dst, src
        shift *= 2
    off = POOL_PAD + window // 2 - 1
    win = src[off:off + seq, :]
    t = lax.broadcasted_iota(I32, (seq, 1), 0)
    lo = jnp.maximum(t - window // 2, 0)
    hi = jnp.minimum(t + window // 2 - 1, seq - 1)
    cnt = (hi - lo + 1).astype(F32)
    pooled = (win / cnt - u_ref[0]).astype(BF16)
    o_ref[0] = (_dot(pooled, gw_ref[0]) * sc_ref[0]).astype(BF16)


def _pool_kernel(u_ref, gw_ref, sc_ref, o_ref, a_ref, b_ref, *, seq):
    g = pl.program_id(1)
    for k, window in enumerate(POOL_WINDOWS):
        @pl.when(g == k)
        def _():
            _pool_group(u_ref, gw_ref, sc_ref, o_ref, a_ref, b_ref, window=window, seq=seq)


def _pool_mixer(u3d, gw_bf16, scale3d):
    b, seq, tok_width = u3d.shape
    n_groups, gwid, _ = gw_bf16.shape
    assert n_groups == len(POOL_WINDOWS) and n_groups * gwid == tok_width
    return pl.pallas_call(
        functools.partial(_pool_kernel, seq=seq),
        out_shape=jax.ShapeDtypeStruct((b, seq, tok_width), BF16),
        grid=(b, n_groups),
        in_specs=[
            pl.BlockSpec((1, seq, gwid), lambda i, g: (i, 0, g)),
            pl.BlockSpec((1, gwid, gwid), lambda i, g: (g, 0, 0)),
            pl.BlockSpec((1, 1, gwid), lambda i, g: (g, 0, 0)),
        ],
        out_specs=pl.BlockSpec((1, seq, gwid), lambda i, g: (i, 0, g)),
        scratch_shapes=[pltpu.VMEM((seq + 2 * POOL_PAD, gwid), F32),
                        pltpu.VMEM((seq + 2 * POOL_PAD, gwid), F32)],
        compiler_params=_params(2),
        name="pool_mixer",
    )(u3d, gw_bf16, scale3d)


def _xattn_kernel(q_ref, kv_ref, o_ref):
    scale = XA_HEAD_DIM ** -0.5
    for h in range(XA_HEADS):
        lo = h * XA_HEAD_DIM
        q = q_ref[:, lo:lo + XA_HEAD_DIM]
        k = kv_ref[0, :, lo:lo + XA_HEAD_DIM]
        v = kv_ref[0, :, XA_WIDTH + lo:XA_WIDTH + lo + XA_HEAD_DIM]
        s = _dot_nt(q, k) * scale
        m = jnp.max(s, axis=-1, keepdims=True)
        p = jnp.exp(s - m)
        den = jnp.sum(p, axis=-1, keepdims=True)
        o_ref[:, lo:lo + XA_HEAD_DIM] = (_dot(p.astype(BF16), v) / den).astype(BF16)


def _mem_xattn(qm2d, memkv_layer, seq):
    t = qm2d.shape[0]
    tiles_per_seq = seq // ROW_TILE
    return pl.pallas_call(
        _xattn_kernel,
        out_shape=jax.ShapeDtypeStruct((t, XA_WIDTH), BF16),
        grid=(t // ROW_TILE,),
        in_specs=[
            pl.BlockSpec((ROW_TILE, XA_WIDTH), lambda i: (i, 0)),
            pl.BlockSpec((1, MEM_LEN, 2 * XA_WIDTH), lambda i: (i // tiles_per_seq, 0, 0)),
        ],
        out_specs=pl.BlockSpec((ROW_TILE, XA_WIDTH), lambda i: (i, 0)),
        compiler_params=_params(1),
        name="mem_xattn",
    )(qm2d, memkv_layer)


def _outproj_kernel(tok_ref, mo_ref, x_ref, w_ref, g_ref, rw_ref,
                    x1_ref, h_ref, afft_ref, aff_ref, wcat_ref, x1prev_ref,
                    *, tok_width, n_experts):
    @pl.when(pl.program_id(0) == 0)
    def _():
        rw = rw_ref[...]
        w_hi = rw.astype(BF16)
        wcat_ref[:, 0:LANES] = w_hi
        wcat_ref[:, LANES:2 * LANES] = (rw - w_hi.astype(F32)).astype(BF16)
        x1prev_ref[...] = jnp.zeros_like(x1prev_ref)

    hn = _rmsnorm_rows(x1prev_ref[...], g_ref[...])
    h_prev = hn.astype(BF16)
    h_lo = (hn - h_prev.astype(F32)).astype(BF16)
    r = _dot(h_prev, wcat_ref[...]) + _dot(h_lo, wcat_ref[...])
    logits = r[:, 0:LANES] + r[:, LANES:2 * LANES]
    lt = logits.T[0:n_experts, :]
    m = jnp.max(lt, axis=0, keepdims=True)
    ex = jnp.exp(lt - m)
    afft = ex / jnp.sum(ex, axis=0, keepdims=True)
    afft_ref[0] = afft
    padded = jnp.concatenate(
        [afft, jnp.zeros((LANES - n_experts, afft.shape[1]), F32)], axis=0)
    aff = padded.T
    aff_ref[...] = aff
    h_ref[...] = _pack_row_words(h_prev, _pack_gate_lanes(aff))

    y = _dot(tok_ref[...], w_ref[0:tok_width, :]) + _dot(mo_ref[...], w_ref[tok_width:, :])
    x1 = x_ref[...] + y
    x1_ref[...] = x1
    x1prev_ref[...] = x1


def _outproj(tok2d, mo2d, x2d, w_bf16, g, rw_pad, n_experts, seq):
    t, d_model = x2d.shape
    tok_width = tok2d.shape[1]
    tiles_per_seq = seq // ROW_TILE
    n_tiles = t // ROW_TILE

    def cur(i):
        return jnp.minimum(i, n_tiles - 1)

    def prev(i):
        return jnp.maximum(i - 1, 0)

    return pl.pallas_call(
        functools.partial(_outproj_kernel, tok_width=tok_width, n_experts=n_experts),
        out_shape=(jax.ShapeDtypeStruct((t, d_model), F32),
                   jax.ShapeDtypeStruct((t, d_model // 2 + LANES), U32),
                   jax.ShapeDtypeStruct((t // seq, n_experts, seq), F32),
                   jax.ShapeDtypeStruct((t, LANES), F32)),
        grid=(n_tiles + 1,),
        in_specs=[
            pl.BlockSpec((ROW_TILE, tok_width), lambda i: (cur(i), 0)),
            pl.BlockSpec((ROW_TILE, mo2d.shape[1]), lambda i: (cur(i), 0)),
            pl.BlockSpec((ROW_TILE, d_model), lambda i: (cur(i), 0)),
            _resident((d_model, d_model), lambda i: (0, 0)),
            pl.BlockSpec((1, d_model), lambda i: (0, 0)),
            _resident((d_model, LANES), lambda i: (0, 0)),
        ],
        out_specs=(pl.BlockSpec((ROW_TILE, d_model), lambda i: (cur(i), 0)),
                   pl.BlockSpec((ROW_TILE, d_model // 2 + LANES), lambda i: (prev(i), 0)),
                   pl.BlockSpec((1, n_experts, ROW_TILE),
                                lambda i: (prev(i) // tiles_per_seq, 0, prev(i) % tiles_per_seq)),
                   pl.BlockSpec((ROW_TILE, LANES), lambda i: (prev(i), 0))),
        scratch_shapes=[pltpu.VMEM((d_model, 2 * LANES), BF16),
                        pltpu.VMEM((ROW_TILE, d_model), F32)],
        compiler_params=_params(1),
        name="outproj_router",
    )(tok2d, mo2d, x2d, w_bf16, g, rw_pad)


def _strict_triangle(n, lower):
    r = lax.broadcasted_iota(I32, (n, n), 0)
    c = lax.broadcasted_iota(I32, (n, n), 1)
    return jnp.where((c < r) if lower else (r < c), 1.0, 0.0).astype(BF16)


def _prefix_rows(mask_f32):
    s, l = mask_f32.shape
    tri = _strict_triangle(PREFIX_CHUNK, lower=True)
    carry = jnp.zeros((1, l), F32)
    out = []
    for c in range(0, s, PREFIX_CHUNK):
        m = mask_f32[c:c + PREFIX_CHUNK, :]
        out.append(_dot(tri, m.astype(BF16)) + carry)
        carry = carry + jnp.sum(m, axis=0, keepdims=True)
    return jnp.concatenate(out, axis=0)


def _prefix_lanes(mask_f32):
    e, s = mask_f32.shape
    tri = _strict_triangle(PREFIX_CHUNK, lower=False)
    carry = jnp.zeros((e, 1), F32)
    out = []
    for c in range(0, s, PREFIX_CHUNK):
        m = mask_f32[:, c:c + PREFIX_CHUNK]
        out.append(_dot(m.astype(BF16), tri) + carry)
        carry = carry + jnp.sum(m, axis=1, keepdims=True)
    return jnp.concatenate(out, axis=1)


def _select_slots(key, thr, need, prefix_fn):
    return _select_slots_and_counts(key, thr, need, prefix_fn)[0]


def _select_slots_and_counts(key, thr, need, prefix_fn):
    gt = jnp.where(key > thr, 1.0, 0.0)
    eq = jnp.where(key == thr, 1.0, 0.0)
    eq_rank = prefix_fn(eq)
    sel = gt + eq * jnp.where(eq_rank < need, 1.0, 0.0)
    pos = prefix_fn(sel)
    return jnp.where(sel > 0.5, pos, -1.0), pos


GATE_GROUP = 16
GATE_PIECES = 3


def _pack_gate_lanes(aff):
    hi = aff.astype(BF16).astype(F32)
    r1 = aff - hi
    mid = r1.astype(BF16).astype(F32)
    lo = (r1 - mid).astype(BF16).astype(F32)
    packed = hi + pltpu.roll(mid, GATE_GROUP, 1) + pltpu.roll(lo, 2 * GATE_GROUP, 1)
    return packed.astype(BF16)


def _unpack_gate(tail, e):
    lane = lax.broadcasted_iota(I32, (1, LANES), 1)
    mine = ((lane & (GATE_GROUP - 1)) == e) & (lane < GATE_PIECES * GATE_GROUP)
    return jnp.sum(jnp.where(mine, tail.astype(F32), 0.0), axis=1, keepdims=True)


def _pack_row_words(h, gate_tile):
    rows, d_model = h.shape
    half = d_model // 2
    hi = jnp.concatenate([h[:, 0:half], gate_tile], axis=1).astype(F32)
    lo = jnp.concatenate([h[:, half:], jnp.zeros((rows, LANES), BF16)], axis=1).astype(F32)
    return pltpu.bitcast(hi, U32) | (pltpu.bitcast(lo, U32) >> 16)


def _unpack_row_words(words):
    hi = pltpu.bitcast(words & jnp.uint32(0xFFFF0000), F32)
    lo = pltpu.bitcast(words << 16, F32)
    return hi, lo


def _route_kernel(afft_ref, posm_ref, thr_ref, need_ref, starts_ref, *, cap):
    n_experts, seq = afft_ref.shape[1], afft_ref.shape[2]
    key = pltpu.bitcast(afft_ref[0], I32)
    thr = jnp.zeros((n_experts, 1), I32)
    for bit in range(30, -1, -1):
        cand = thr | (1 << bit)
        cnt = jnp.sum(jnp.where(key >= cand, 1.0, 0.0), axis=1, keepdims=True)
        thr = jnp.where(cnt >= cap, cand, thr)
    n_gt = jnp.sum(jnp.where(key > thr, 1.0, 0.0), axis=1, keepdims=True)
    need = cap - n_gt
    thr_ref[0] = thr
    need_ref[0] = need
    posm, before = _select_slots_and_counts(key, thr, need, _prefix_lanes)
    posm_ref[0] = posm.astype(I32)
    starts_ref[0] = jnp.concatenate(
        [before[:, r:r + 1] for r in range(0, seq, COMBINE_ROWS)], axis=1).astype(I32)


def _route(afft, cap):
    b, n_experts, seq = afft.shape
    n_tiles = seq // COMBINE_ROWS
    return pl.pallas_call(
        functools.partial(_route_kernel, cap=cap),
        out_shape=(jax.ShapeDtypeStruct((b, n_experts, seq), I32),
                   jax.ShapeDtypeStruct((b, n_experts, 1), I32),
                   jax.ShapeDtypeStruct((b, n_experts, 1), F32),
                   jax.ShapeDtypeStruct((b, n_experts, n_tiles), I32)),
        grid=(b,),
        in_specs=[pl.BlockSpec((1, n_experts, seq), lambda i: (i, 0, 0))],
        out_specs=(pl.BlockSpec((1, n_experts, seq), lambda i: (i, 0, 0)),
                   pl.BlockSpec((1, n_experts, 1), lambda i: (i, 0, 0)),
                   pl.BlockSpec((1, n_experts, 1), lambda i: (i, 0, 0)),
                   pl.BlockSpec((1, n_experts, n_tiles), lambda i: (i, 0, 0))),
        compiler_params=_params(1),
        name="expert_route",
    )(afft)


SC_LANES = 16
SC_GATHER_ROWS = 64


def _sc_expert_gather(posm2d, h_words, n_experts, e_offset, n_e, b, seq, cap):
    width = h_words.shape[1]
    info = plsc.get_sparse_core_info()
    n_cores, n_subcores = info.num_cores, info.num_subcores
    n_workers = n_cores * n_subcores
    assert info.num_lanes == SC_LANES and (b * n_e) % n_workers == 0
    pairs_per_worker = (b * n_e) // n_workers
    mesh = plsc.VectorSubcoreMesh(core_axis_name="c", subcore_axis_name="s")

    @functools.partial(
        pl.kernel, mesh=mesh,
        out_type=jax.ShapeDtypeStruct((n_e * b * cap, width), U32),
        compiler_params=pltpu.CompilerParams(needs_layout_passes=False),
        scratch_types=[
            pltpu.VMEM((seq,), I32),
            pltpu.VMEM((cap,), I32),
            pltpu.VMEM((SC_GATHER_ROWS, width), U32),
            pltpu.SemaphoreType.DMA,
        ],
        name="sc_expert_gather",
    )
    def gather(posm_hbm, h_hbm, out_hbm, pos_v, idx_v, rows_v, sem):
        wid = lax.axis_index("s") * n_cores + lax.axis_index("c")
        for p in range(pairs_per_worker):
            pair = wid * pairs_per_worker + p
            bi = pair // n_e
            e = pair - bi * n_e
            pltpu.sync_copy(posm_hbm.at[bi * n_experts + e_offset + e], pos_v)

            @pl.loop(0, seq, step=SC_LANES)
            def _(t0):
                slots = pos_v[pl.ds(t0, SC_LANES)]
                rows = lax.iota(I32, SC_LANES) + (t0 + bi * seq)
                plsc.store_scatter(idx_v, [slots], rows, mask=slots >= 0)

            out_base = (e * b + bi) * cap
            for c in range(cap // SC_GATHER_ROWS):
                chunk = idx_v.at[pl.ds(c * SC_GATHER_ROWS, SC_GATHER_ROWS)]
                pltpu.async_copy(h_hbm.at[chunk], rows_v, sem).wait()
                pltpu.sync_copy(rows_v, out_hbm.at[pl.ds(out_base + c * SC_GATHER_ROWS, SC_GATHER_ROWS)])

    return gather(posm2d, h_words).reshape(n_e, b, cap, width)


EXPERT_ROWS = 1024
EXPERT_FTILE = 256


EXPERT_WBUFS = 2


def _expert_kernel(xg_ref, wg_hbm, wu_hbm, wd_hbm, o_ref,
                   x_ref, g_ref, hact_ref, wg_full, wu_full, wd_full, wg_buf, wu_buf, wd_buf, sem,
                   *, layer, e_offset, n_ftiles):
    e = pl.program_id(0)
    m = pl.program_id(1)
    n_e = pl.num_programs(0)
    nb, cap, width = xg_ref.shape[1], xg_ref.shape[2], xg_ref.shape[3]
    half = width - LANES
    d_model = 2 * half
    rows = nb * cap
    tf = EXPERT_FTILE
    assert n_ftiles % EXPERT_WBUFS == 0

    def tile_copies(expert, f):
        ge = e_offset + expert
        slot = f % EXPERT_WBUFS
        return (
            pltpu.make_async_copy(wg_hbm.at[layer, ge, :, pl.ds(f * tf, tf)], wg_buf.at[slot], sem.at[0, slot]),
            pltpu.make_async_copy(wu_hbm.at[layer, ge, :, pl.ds(f * tf, tf)], wu_buf.at[slot], sem.at[1, slot]),
            pltpu.make_async_copy(wd_hbm.at[layer, ge, pl.ds(f * tf, tf), :], wd_buf.at[slot], sem.at[2, slot]),
        )

    def start(expert, f):
        for cp in tile_copies(expert, f):
            cp.start()

    @pl.when((e == 0) & (m == 0))
    def _():
        for f in range(EXPERT_WBUFS):
            start(e, f)

    for i in range(nb):
        r0 = i * cap
        hi, lo = _unpack_row_words(xg_ref[0, i])
        x_ref[r0:r0 + cap, 0:half] = hi[:, 0:half].astype(BF16)
        x_ref[r0:r0 + cap, half:d_model] = lo[:, 0:half].astype(BF16)
        g_ref[r0:r0 + cap, :] = jnp.broadcast_to(
            _unpack_gate(hi[:, half:width], e_offset + e), (cap, LANES))

    def receive(f):
        slot = f % EXPERT_WBUFS
        for cp in tile_copies(e, f):
            cp.wait()
        wg_full[:, f * tf:(f + 1) * tf] = wg_buf[slot].astype(BF16)
        wu_full[:, f * tf:(f + 1) * tf] = wu_buf[slot].astype(BF16)
        wd_full[f * tf:(f + 1) * tf, :] = wd_buf[slot].astype(BF16)
        ahead = f + EXPERT_WBUFS
        if ahead < n_ftiles:
            start(e, ahead)
        else:
            @pl.when(e + 1 < n_e)
            def _():
                start(e + 1, ahead - n_ftiles)

    def body(first_group):
        x = x_ref[...]
        for f in range(n_ftiles):
            if first_group:
                receive(f)
            a = _dot(x, wg_full[:, f * tf:(f + 1) * tf])
            u = _dot(x, wu_full[:, f * tf:(f + 1) * tf])
            hact_ref[:, f * tf:(f + 1) * tf] = (a * jax.nn.sigmoid(a) * u).astype(BF16)
        chunk = 512
        for c in range(0, d_model, chunk):
            y = _dot(hact_ref[...], wd_full[:, c:c + chunk])
            for j in range(0, chunk, LANES):
                o_ref[:, 0, :, c + j:c + j + LANES] = (
                    (y[:, j:j + LANES] * g_ref[...]).astype(BF16).reshape(nb, cap, LANES))

    @pl.when(m == 0)
    def _():
        body(True)

    @pl.when(m != 0)
    def _():
        body(False)


def _experts(xg, w_gate, w_up, w_down, layer, e_offset):
    n_e, b, cap, width = xg.shape
    d_model = 2 * (width - LANES)
    d_expert = w_gate.shape[3]
    nb = EXPERT_ROWS // cap
    n_ftiles = d_expert // EXPERT_FTILE
    return pl.pallas_call(
        functools.partial(_expert_kernel, layer=layer, e_offset=e_offset, n_ftiles=n_ftiles),
        out_shape=jax.ShapeDtypeStruct((b, n_e, cap, d_model), BF16),
        grid=(n_e, b // nb),
        in_specs=[
            pl.BlockSpec((1, nb, cap, width), lambda e, m: (e, m, 0, 0)),
            pl.BlockSpec(memory_space=pl.ANY),
            pl.BlockSpec(memory_space=pl.ANY),
            pl.BlockSpec(memory_space=pl.ANY),
        ],
        out_specs=pl.BlockSpec((nb, 1, cap, d_model), lambda e, m: (m, e, 0, 0)),
        scratch_shapes=[pltpu.VMEM((EXPERT_ROWS, d_model), BF16),
                        pltpu.VMEM((EXPERT_ROWS, LANES), F32),
                        pltpu.VMEM((EXPERT_ROWS, d_expert), BF16),
                        pltpu.VMEM((d_model, d_expert), BF16),
                        pltpu.VMEM((d_model, d_expert), BF16),
                        pltpu.VMEM((d_expert, d_model), BF16),
                        pltpu.VMEM((EXPERT_WBUFS, d_model, EXPERT_FTILE), F32),
                        pltpu.VMEM((EXPERT_WBUFS, d_model, EXPERT_FTILE), F32),
                        pltpu.VMEM((EXPERT_WBUFS, EXPERT_FTILE, d_model), F32),
                        pltpu.SemaphoreType.DMA((3, EXPERT_WBUFS))],
        compiler_params=_params(2),
        name="experts",
    )(xg, w_gate, w_up, w_down)


COMBINE_ROWS = 256
COMBINE_WINDOW = 128
BF16_ROWS = 16


def _combine_kernel(starts_ref, aff_ref, thr_ref, need_ref, x1_ref, *rest, cap, n_experts, final_norm):
    n_y = len(rest) - (6 if final_norm else 5)
    y_hbm = rest[:n_y]
    g_ref = rest[n_y] if final_norm else None
    o_ref, post_ref, pfull_ref, ybuf, sem = rest[-5:]
    b = pl.program_id(0)
    t = pl.program_id(1)
    n_b = pl.num_programs(0)
    n_t = pl.num_programs(1)
    rows, win = COMBINE_ROWS, COMBINE_WINDOW
    experts_per_group = n_experts // n_y
    group_rows = ybuf.shape[0] // (2 * n_y)

    def ybuf_row(buf_slot, g):
        return pl.multiple_of((buf_slot * n_y + g) * group_rows, BF16_ROWS)

    n_chunks = sem.shape[1]
    chunks_per_group = n_chunks // n_y
    chunk_rows = group_rows // chunks_per_group
    slot = b % 2

    def chunk_copy(seq_idx, c, dst_slot):
        g, r = c // chunks_per_group, (c % chunks_per_group) * chunk_rows
        return pltpu.make_async_copy(y_hbm[g].at[seq_idx, pl.ds(r, chunk_rows), :],
                                     ybuf.at[pl.ds(ybuf_row(dst_slot, g) + r, chunk_rows), :],
                                     sem.at[dst_slot, c])

    @pl.when((b == 0) & (t == 0))
    def _():
        for c in range(n_chunks):
            chunk_copy(0, c, 0).start()

    @pl.when(t == 0)
    def _():
        for c in range(n_chunks):
            chunk_copy(b, c, slot).wait()
        key = pltpu.bitcast(aff_ref[...], I32)
        post_ref[...] = _select_slots(key, thr_ref[0], need_ref[0], _prefix_rows)

    for c in range(n_chunks):
        @pl.when((t == c) & (b + 1 < n_b))
        def _():
            chunk_copy(b + 1, c, 1 - slot).start()

    posm = post_ref[pl.ds(pl.multiple_of(t * rows, rows), rows), :]
    base = (b * (n_t + 1) + t) * n_experts
    wstart, ok = [], None
    for e in range(n_experts):
        first = starts_ref[base + e]
        end = starts_ref[base + n_experts + e]
        w0 = jnp.minimum((first // BF16_ROWS) * BF16_ROWS, cap - win)
        fits = end - w0 <= win
        wstart.append(w0)
        ok = fits if ok is None else jnp.logical_and(ok, fits)

    def finish(acc):
        if final_norm:
            acc = _rmsnorm_rows(acc, g_ref[...])
        o_ref[0] = acc

    @pl.when(ok)
    def _():
        lane = lax.broadcasted_iota(I32, (1, win), 1).astype(F32)
        acc = x1_ref[0]
        for pair in range(n_experts // 2):
            onehots, windows = [], []
            for e in (2 * pair, 2 * pair + 1):
                g, el = divmod(e, experts_per_group)
                rel = posm[:, e:e + 1] - wstart[e].astype(F32)
                onehots.append(jnp.where(rel == lane, 1.0, 0.0).astype(BF16))
                r0 = pl.multiple_of(ybuf_row(slot, g) + el * cap + wstart[e], BF16_ROWS)
                windows.append(ybuf[pl.ds(r0, win), :])
            acc = acc + _dot(jnp.concatenate(onehots, axis=1), jnp.concatenate(windows, axis=0))
        finish(acc)

    @pl.when(jnp.logical_not(ok))
    def _():
        slot_ids = lax.broadcasted_iota(I32, (1, cap), 1).astype(F32)
        for e in range(n_experts):
            pfull_ref[:, e * cap:(e + 1) * cap] = jnp.where(
                posm[:, e:e + 1] == slot_ids, 1.0, 0.0).astype(BF16)
        acc = x1_ref[0]
        for g in range(n_y):
            acc = acc + _dot(pfull_ref[:, g * group_rows:(g + 1) * group_rows],
                             ybuf[pl.ds(ybuf_row(slot, g), group_rows), :])
        finish(acc)


def _combine(starts, aff2d, thr_row, need_row, x1_3d, y_groups, cap, n_experts, final_g=None):
    b, seq, d_model = x1_3d.shape
    assert sum(y.shape[1] for y in y_groups) == n_experts * cap and cap >= COMBINE_WINDOW
    final_norm = final_g is not None
    n_tiles = seq // COMBINE_ROWS
    group_rows = y_groups[0].shape[1]
    assert all(y.shape[1] == group_rows for y in y_groups)
    assert n_tiles % len(y_groups) == 0 and group_rows % (n_tiles // len(y_groups)) == 0
    y_specs = [pl.BlockSpec(memory_space=pl.ANY) for _ in y_groups]
    g_specs = [pl.BlockSpec((1, d_model), lambda i, t, s: (0, 0))] if final_norm else []
    g_args = [final_g] if final_norm else []
    return pl.pallas_call(
        functools.partial(_combine_kernel, cap=cap, n_experts=n_experts, final_norm=final_norm),
        out_shape=jax.ShapeDtypeStruct((b, seq, d_model), F32),
        grid_spec=pltpu.PrefetchScalarGridSpec(
            num_scalar_prefetch=1,
            grid=(b, n_tiles),
            in_specs=[
                pl.BlockSpec((seq, LANES), lambda i, t, s: (i, 0)),
                pl.BlockSpec((1, 1, LANES), lambda i, t, s: (i, 0, 0)),
                pl.BlockSpec((1, 1, LANES), lambda i, t, s: (i, 0, 0)),
                pl.BlockSpec((1, COMBINE_ROWS, d_model), lambda i, t, s: (i, t, 0)),
            ] + y_specs + g_specs,
            out_specs=pl.BlockSpec((1, COMBINE_ROWS, d_model), lambda i, t, s: (i, t, 0)),
            scratch_shapes=[pltpu.VMEM((seq, LANES), F32),
                            pltpu.VMEM((COMBINE_ROWS, n_experts * cap), BF16),
                            pltpu.VMEM((2 * len(y_groups) * group_rows, d_model), BF16),
                            pltpu.SemaphoreType.DMA((2, n_tiles))],
        ),
        compiler_params=_params(2),
        name="combine",
    )(starts, aff2d, thr_row, need_row, x1_3d, *y_groups, *g_args)


EXPERT_GROUPS = 2


def _moe(afft, aff2d, h_words, x1_2d, w_gate, w_up, w_down, layer, b, seq, final_g=None):
    n_experts = afft.shape[1]
    assert n_experts <= GATE_GROUP and n_experts % EXPERT_GROUPS == 0
    d_model = x1_2d.shape[1]
    cap = CAPACITY_FACTOR * seq // n_experts
    n_e = n_experts // EXPERT_GROUPS
    posm, thr, need, tile_starts = _route(afft, cap)
    starts = jnp.concatenate([jnp.swapaxes(tile_starts, 1, 2),
                              jnp.full((b, 1, n_experts), cap, I32)], axis=1).reshape(-1)
    posm2d = posm.reshape(b * n_experts, seq)
    xgs = [_sc_expert_gather(posm2d, h_words, n_experts, g * n_e, n_e, b, seq, cap)
           for g in range(EXPERT_GROUPS)]
    ys = [_experts(xg, w_gate, w_up, w_down, layer, g * n_e).reshape(b, n_e * cap, d_model)
          for g, xg in enumerate(xgs)]
    pad = LANES - n_experts
    thr_row = jnp.pad(thr.reshape(b, 1, n_experts), ((0, 0), (0, 0), (0, pad)),
                      constant_values=np.iinfo(np.int32).max)
    need_row = jnp.pad(need.reshape(b, 1, n_experts), ((0, 0), (0, 0), (0, pad)))
    return _combine(starts, aff2d, thr_row, need_row, x1_2d.reshape(b, seq, d_model), ys, cap,
                    n_experts, final_g)


def _rotary_tile(t, cos, sin_lo, sin_hi):
    half = ROT_DIM // 2
    return t * cos + pltpu.roll(t, LANES - half, 1) * sin_lo + pltpu.roll(t, half, 1) * sin_hi


def _inproj_attn_kernel(x_ref, g_ref, w_ref, pos_ref, rot_ref,
                        q_ref, k_ref, v_ref, qm_ref, *, tok_width, kv_width):
    hn = _rmsnorm_rows(x_ref[...], g_ref[...]).astype(BF16)
    ang = pos_ref[...].astype(F32) * rot_ref[0:1, :]
    cos = jnp.cos(ang)
    sin = jnp.sin(ang)
    sin_lo = sin * rot_ref[1:2, :]
    sin_hi = sin * rot_ref[2:3, :]
    qscale = HEAD_DIM ** -0.5
    chunk = 512
    for c in range(0, tok_width, chunk):
        pc = _dot(hn, w_ref[:, c:c + chunk])
        for j in range(0, chunk, LANES):
            rot = _rotary_tile(pc[:, j:j + LANES], cos, sin_lo, sin_hi)
            q_ref[:, c + j:c + j + LANES] = (rot * qscale).astype(BF16)
    kv = _dot(hn, w_ref[:, tok_width:tok_width + 2 * kv_width])
    k01 = _rotary_tile(kv[:, 0:LANES], cos, sin_lo, sin_hi)
    k2x = _rotary_tile(kv[:, LANES:2 * LANES], cos, sin_lo, sin_hi)
    k_ref[0, 0] = k01[:, 0:HEAD_DIM].astype(BF16)
    k_ref[0, 1] = k01[:, HEAD_DIM:LANES].astype(BF16)
    k_ref[0, 2] = k2x[:, 0:HEAD_DIM].astype(BF16)
    for hh in range(kv_width // HEAD_DIM):
        lo = kv_width + hh * HEAD_DIM
        v_ref[0, hh] = kv[:, lo:lo + HEAD_DIM].astype(BF16)
    qm_ref[...] = _dot(hn, w_ref[:, tok_width + 2 * kv_width:]).astype(BF16)


def _inproj_attn(x2d, g, w_bf16, pos2d, rot_rows, tok_width, kv_width, seq):
    t, d_model = x2d.shape
    n = w_bf16.shape[1]
    n_kv = kv_width // HEAD_DIM
    assert n_kv == 3 and kv_width + HEAD_DIM == 2 * LANES
    tiles_per_seq = seq // ROW_TILE
    kv_spec = pl.BlockSpec((1, n_kv, ROW_TILE, HEAD_DIM),
                           lambda i: (i // tiles_per_seq, 0, i % tiles_per_seq, 0))
    return pl.pallas_call(
        functools.partial(_inproj_attn_kernel, tok_width=tok_width, kv_width=kv_width),
        out_shape=(jax.ShapeDtypeStruct((t, tok_width), BF16),
                   jax.ShapeDtypeStruct((t // seq, n_kv, seq, HEAD_DIM), BF16),
                   jax.ShapeDtypeStruct((t // seq, n_kv, seq, HEAD_DIM), BF16),
                   jax.ShapeDtypeStruct((t, n - tok_width - 2 * kv_width), BF16)),
        grid=(t // ROW_TILE,),
        in_specs=[
            pl.BlockSpec((ROW_TILE, d_model), lambda i: (i, 0)),
            pl.BlockSpec((1, d_model), lambda i: (0, 0)),
            _resident((d_model, n), lambda i: (0, 0)),
            pl.BlockSpec((ROW_TILE, 1), lambda i: (i, 0)),
            pl.BlockSpec((8, LANES), lambda i: (0, 0)),
        ],
        out_specs=(pl.BlockSpec((ROW_TILE, tok_width), lambda i: (i, 0)),
                   kv_spec, kv_spec,
                   pl.BlockSpec((ROW_TILE, n - tok_width - 2 * kv_width), lambda i: (i, 0))),
        compiler_params=_params(1),
        name="inproj_attn",
    )(x2d, g, w_bf16, pos2d, rot_rows)


def _wattn_kernel(sink_ref, q_ref, kp_ref, kc_ref, kn_ref, vp_ref, vc_ref, vn_ref, o_ref,
                  valid_ref, kpad_ref, vpad_ref, s_ref, p_ref, inv_ref, *, seq):
    n = pl.program_id(1)
    n_kv = kc_ref.shape[1]
    pairs = GQA_RATIO // 2
    half_rows = pairs * BLOCK
    qi = lax.broadcasted_iota(I32, (BLOCK, 3 * BLOCK), 0)
    kj = lax.broadcasted_iota(I32, (BLOCK, 3 * BLOCK), 1)
    first = jnp.maximum(qi, BLOCK - n * BLOCK)
    last = jnp.minimum(qi + 2 * WINDOW, seq + BLOCK - 1 - n * BLOCK)
    valid_ref[...] = jnp.where(((kj - first) | (last - kj)) >= 0, 1.0, 0.0)
    zeros = jnp.zeros((3 * BLOCK, HEAD_DIM), BF16)
    ones_col = jnp.where(lax.broadcasted_iota(I32, (3 * BLOCK, HEAD_DIM), 1) == 0, 1.0, 0.0).astype(BF16)
    low_half = lax.broadcasted_iota(I32, (1, LANES), 1) < HEAD_DIM
    for hk in range(n_kv):
        kw = jnp.concatenate([kp_ref[0, hk], kc_ref[0, hk], kn_ref[0, hk]], axis=0)
        vw = jnp.concatenate([vp_ref[0, hk], vc_ref[0, hk], vn_ref[0, hk]], axis=0)
        kpad_ref[2 * hk] = jnp.concatenate([kw, zeros], axis=1)
        kpad_ref[2 * hk + 1] = jnp.concatenate([zeros, kw], axis=1)
        vpad_ref[2 * hk] = jnp.concatenate([vw, ones_col], axis=1)
        vpad_ref[2 * hk + 1] = jnp.concatenate([ones_col, vw], axis=1)
        tile0 = hk * pairs
        qs = jnp.concatenate(
            [q_ref[0, :, (tile0 + j) * LANES:(tile0 + j + 1) * LANES] for j in range(pairs)], axis=0)
        s_ref[hk, 0:half_rows, :] = _dot_nt(qs, kpad_ref[2 * hk])
        s_ref[hk, half_rows:2 * half_rows, :] = _dot_nt(qs, kpad_ref[2 * hk + 1])
    for hk in range(n_kv):
        for c in range(GQA_RATIO):
            j, odd = c % pairs, c // pairs
            r = c * BLOCK
            s = jnp.concatenate([
                jnp.where(valid_ref[:, 0:BLOCK] > 0.5, s_ref[hk, r:r + BLOCK, 0:BLOCK], NEG_INF),
                s_ref[hk, r:r + BLOCK, BLOCK:2 * BLOCK],
                jnp.where(valid_ref[:, 2 * BLOCK:] > 0.5, s_ref[hk, r:r + BLOCK, 2 * BLOCK:], NEG_INF),
            ], axis=1)
            sk = sink_ref[hk * GQA_RATIO + 2 * j + odd]
            m = jnp.maximum(jnp.max(s, axis=-1, keepdims=True), sk)
            p_ref[hk, r:r + BLOCK, :] = jnp.exp(s - m).astype(BF16)
            inv_ref[hk, j * BLOCK:(j + 1) * BLOCK, odd * HEAD_DIM:(odd + 1) * HEAD_DIM] = (
                jnp.broadcast_to(jnp.exp(sk - m), (BLOCK, HEAD_DIM)))
    for hk in range(n_kv):
        pv_even = _dot(p_ref[hk, 0:half_rows, :], vpad_ref[2 * hk])
        pv_odd = _dot(p_ref[hk, half_rows:2 * half_rows, :], vpad_ref[2 * hk + 1])
        den = jnp.where(low_half, pv_even[:, HEAD_DIM:HEAD_DIM + 1], pv_odd[:, 0:1]) + inv_ref[hk]
        o = jnp.where(low_half, pv_even, pv_odd) / den
        for j in range(pairs):
            lo = (hk * pairs + j) * LANES
            o_ref[0, :, lo:lo + LANES] = o[j * BLOCK:(j + 1) * BLOCK].astype(BF16)


def _window_attention(sink, q3d, k4d, v4d):
    b, seq, tok_width = q3d.shape
    n_kv = k4d.shape[1]
    nb = seq // BLOCK
    kv_block = (1, n_kv, BLOCK, HEAD_DIM)
    prev_spec = pl.BlockSpec(kv_block, lambda i, n: (i, 0, jnp.maximum(n - 1, 0), 0))
    cur_spec = pl.BlockSpec(kv_block, lambda i, n: (i, 0, n, 0))
    next_spec = pl.BlockSpec(kv_block, lambda i, n: (i, 0, jnp.minimum(n + 1, nb - 1), 0))
    return pl.pallas_call(
        functools.partial(_wattn_kernel, seq=seq),
        out_shape=jax.ShapeDtypeStruct((b, seq, tok_width), BF16),
        grid=(b, nb),
        in_specs=[
            pl.BlockSpec(memory_space=pltpu.SMEM),
            pl.BlockSpec((1, BLOCK, tok_width), lambda i, n: (i, n, 0)),
            prev_spec, cur_spec, next_spec, prev_spec, cur_spec, next_spec,
        ],
        out_specs=pl.BlockSpec((1, BLOCK, tok_width), lambda i, n: (i, n, 0)),
        scratch_shapes=[pltpu.VMEM((BLOCK, 3 * BLOCK), F32),
                        pltpu.VMEM((2 * n_kv, 3 * BLOCK, LANES), BF16),
                        pltpu.VMEM((2 * n_kv, 3 * BLOCK, LANES), BF16),
                        pltpu.VMEM((n_kv, GQA_RATIO * BLOCK, 3 * BLOCK), F32),
                        pltpu.VMEM((n_kv, GQA_RATIO * BLOCK, 3 * BLOCK), BF16),
                        pltpu.VMEM((n_kv, GQA_RATIO // 2 * BLOCK, LANES), F32)],
        compiler_params=_params(2),
        name="window_attention",
    )(sink, q3d, k4d, k4d, k4d, v4d, v4d, v4d)


def _rotary_rows(dtype=F32):
    half = ROT_DIM // 2
    inv_freq = ROPE_THETA ** (-jnp.arange(0, ROT_DIM, 2, dtype=jnp.float32) / ROT_DIM)
    lane = np.arange(LANES) % HEAD_DIM
    rotated = lane < ROT_DIM
    freq = jnp.where(jnp.asarray(rotated), inv_freq[jnp.asarray(lane % half)], 0.0)
    rows = jnp.zeros((8, LANES), dtype)
    rows = rows.at[0].set(freq)
    rows = rows.at[1].set(jnp.asarray(np.where(lane < half, -1.0, 0.0), dtype))
    rows = rows.at[2].set(jnp.asarray(np.where(rotated & (lane >= half), 1.0, 0.0), dtype))
    return rows


def kernel(x, mem, positions, norm_mix_g, norm_ffn_g, mem_norm_g, final_g, mem_w_kv,
           pool_w_in, pool_group_w, pool_scale, pool_w_out,
           attn_w_in, attn_sink, attn_w_out,
           router_w, exp_w_gate, exp_w_up, exp_w_down):
    b, seq, d_model = x.shape
    depth = norm_mix_g.shape[0]
    t = b * seq
    n_experts = router_w.shape[2]
    tok_width = pool_scale.shape[1]
    n_groups = pool_group_w.shape[1]
    kv_width = (attn_w_in.shape[2] - tok_width - XA_WIDTH) // 2
    assert seq % ROW_TILE == 0 and mem.shape[1] == MEM_LEN

    memkv = _memkv(mem.reshape(b * MEM_LEN, d_model), mem_norm_g.reshape(1, d_model),
                   mem_w_kv.astype(BF16))
    memkv = memkv.reshape(depth, b, MEM_LEN, 2 * XA_WIDTH)
    rw_pad = jnp.pad(router_w, ((0, 0), (0, 0), (0, LANES - n_experts)))
    pos2d = positions.reshape(t, 1)
    rot_rows = _rotary_rows()

    x2d = x.reshape(t, d_model)
    for layer in range(depth):
        j = layer // 2
        g_mix = norm_mix_g[layer].reshape(1, d_model)
        if layer % 2 == 0:
            u, qm = _inproj_pool(x2d, g_mix, pool_w_in[j].astype(BF16), tok_width)
            tok = _pool_mixer(u.reshape(b, seq, tok_width), pool_group_w[j].astype(BF16),
                              pool_scale[j].reshape(n_groups, 1, tok_width // n_groups))
            tok = tok.reshape(t, tok_width)
            w_out = pool_w_out[j]
        else:
            q, k, v, qm = _inproj_attn(x2d, g_mix, attn_w_in[j].astype(BF16), pos2d, rot_rows,
                                       tok_width, kv_width, seq)
            tok = _window_attention(attn_sink[j], q.reshape(b, seq, tok_width), k, v)
            tok = tok.reshape(t, tok_width)
            w_out = attn_w_out[j]
        mo = _mem_xattn(qm, memkv[layer], seq)
        x1, h, afft, aff = _outproj(tok, mo, x2d, w_out.astype(BF16),
                                    norm_ffn_g[layer].reshape(1, d_model), rw_pad[layer],
                                    n_experts, seq)
        last = layer == depth - 1
        x2 = _moe(afft, aff, h, x1, exp_w_gate, exp_w_up, exp_w_down, layer, b, seq,
                  final_g.reshape(1, d_model) if last else None)
        x2d = x2.reshape(t, d_model)
    return x2d.reshape(b, seq, d_model)
```

```python
import functools

import jax
import jax.numpy as jnp
import numpy as np
from jax import lax
from jax.experimental import pallas as pl
from jax.experimental.pallas import tpu as pltpu
from jax.experimental.pallas import tpu_sc as plsc

F32 = jnp.float32
BF16 = jnp.bfloat16
I32 = jnp.int32
U32 = jnp.uint32

EPS = 1e-6
MEM_LEN = 256
XA_HEADS = 4
XA_HEAD_DIM = 128
XA_WIDTH = XA_HEADS * XA_HEAD_DIM
POOL_WINDOWS = (2, 4, 8, 16)
HEAD_DIM = 64
GQA_RATIO = 8
WINDOW = 128
BLOCK = 128
ROPE_THETA = 500000.0
ROT_DIM = 16
NEG_INF = -1e30
CAPACITY_FACTOR = 2

LANES = 128
MIB = 1024 * 1024
VMEM_LIMIT_BYTES = 56 * MIB

ROW_TILE = 512
POOL_PAD = 16
PREFIX_CHUNK = 256
COL_CHUNK = 512
F32_KEY_BITS = 31


def _params(n_grid_dims):
    return pltpu.CompilerParams(
        dimension_semantics=("arbitrary",) * n_grid_dims,
        vmem_limit_bytes=VMEM_LIMIT_BYTES,
    )


def _resident(block_shape, index_map):
    return pl.BlockSpec(block_shape, index_map, pipeline_mode=pl.Buffered(1))


def _rmsnorm_rows(x, g):
    return x * lax.rsqrt(jnp.mean(x * x, axis=-1, keepdims=True) + EPS) * g


def _dot(a, b):
    return jnp.dot(a, b, preferred_element_type=F32)


def _dot_nt(a, b):
    return lax.dot_general(a, b, (((1,), (1,)), ((), ())), preferred_element_type=F32)


def _memkv_kernel(mem_ref, g_ref, w_ref, o_ref):
    hn = _rmsnorm_rows(mem_ref[...], g_ref[...]).astype(BF16)
    o_ref[0] = _dot(hn, w_ref[0]).astype(BF16)


def _memkv(mem2d, g, w_bf16):
    depth, d_model, n = w_bf16.shape
    rows = mem2d.shape[0]
    return pl.pallas_call(
        _memkv_kernel,
        out_shape=jax.ShapeDtypeStruct((depth, rows, n), BF16),
        grid=(depth, rows // ROW_TILE),
        in_specs=[
            pl.BlockSpec((ROW_TILE, d_model), lambda l, i: (i, 0)),
            pl.BlockSpec((1, d_model), lambda l, i: (0, 0)),
            pl.BlockSpec((1, d_model, n), lambda l, i: (l, 0, 0)),
        ],
        out_specs=pl.BlockSpec((1, ROW_TILE, n), lambda l, i: (l, i, 0)),
        compiler_params=_params(2),
        name="memkv",
    )(mem2d, g, w_bf16)


def _inproj_pool_kernel(x_ref, g_ref, w_ref, u_ref, qm_ref, *, tok_width):
    hn = _rmsnorm_rows(x_ref[...], g_ref[...]).astype(BF16)
    for c in range(0, tok_width, COL_CHUNK):
        u_ref[:, c:c + COL_CHUNK] = _dot(hn, w_ref[:, c:c + COL_CHUNK])
    qm_ref[...] = _dot(hn, w_ref[:, tok_width:]).astype(BF16)


def _inproj_pool(x2d, g, w_bf16, tok_width):
    t, d_model = x2d.shape
    n = w_bf16.shape[1]
    return pl.pallas_call(
        functools.partial(_inproj_pool_kernel, tok_width=tok_width),
        out_shape=(jax.ShapeDtypeStruct((t, tok_width), F32),
                   jax.ShapeDtypeStruct((t, n - tok_width), BF16)),
        grid=(t // ROW_TILE,),
        in_specs=[
            pl.BlockSpec((ROW_TILE, d_model), lambda i: (i, 0)),
            pl.BlockSpec((1, d_model), lambda i: (0, 0)),
            _resident((d_model, n), lambda i: (0, 0)),
        ],
        out_specs=(pl.BlockSpec((ROW_TILE, tok_width), lambda i: (i, 0)),
                   pl.BlockSpec((ROW_TILE, n - tok_width), lambda i: (i, 0))),
        compiler_params=_params(1),
        name="inproj_pool",
    )(x2d, g, w_bf16)


def _pool_group(u_ref, gw_ref, sc_ref, o_ref, a_ref, b_ref, *, window, seq):
    gwid = u_ref.shape[2]
    rows = seq + 2 * POOL_PAD
    zeros_pad = jnp.zeros((POOL_PAD, gwid), F32)
    a_ref[0:POOL_PAD, :] = zeros_pad
    b_ref[0:POOL_PAD, :] = zeros_pad
    a_ref[POOL_PAD:POOL_PAD + seq, :] = u_ref[0]
    a_ref[POOL_PAD + seq:rows, :] = zeros_pad
    src, dst = a_ref, b_ref
    shift = 1
    while shift < window:
        dst[POOL_PAD:rows, :] = src[POOL_PAD - shift:rows - shift, :] + src[POOL_PAD:rows, :]
        src, dst = dst, src
        shift *= 2
    off = POOL_PAD + window // 2 - 1
    win = src[off:off + seq, :]
    t = lax.broadcasted_iota(I32, (seq, 1), 0)
    lo = jnp.maximum(t - window // 2, 0)
    hi = jnp.minimum(t + window // 2 - 1, seq - 1)
    cnt = (hi - lo + 1).astype(F32)
    pooled = (win / cnt - u_ref[0]).astype(BF16)
    o_ref[0] = (_dot(pooled, gw_ref[0]) * sc_ref[0]).astype(BF16)


def _pool_kernel(u_ref, gw_ref, sc_ref, o_ref, a_ref, b_ref, *, seq):
    g = pl.program_id(1)
    for k, window in enumerate(POOL_WINDOWS):
        @pl.when(g == k)
        def _():
            _pool_group(u_ref, gw_ref, sc_ref, o_ref, a_ref, b_ref, window=window, seq=seq)


def _pool_mixer(u3d, gw_bf16, scale3d):
    b, seq, tok_width = u3d.shape
    n_groups, gwid, _ = gw_bf16.shape
    assert n_groups == len(POOL_WINDOWS) and n_groups * gwid == tok_width
    return pl.pallas_call(
        functools.partial(_pool_kernel, seq=seq),
        out_shape=jax.ShapeDtypeStruct((b, seq, tok_width), BF16),
        grid=(b, n_groups),
        in_specs=[
            pl.BlockSpec((1, seq, gwid), lambda i, g: (i, 0, g)),
            pl.BlockSpec((1, gwid, gwid), lambda i, g: (g, 0, 0)),
            pl.BlockSpec((1, 1, gwid), lambda i, g: (g, 0, 0)),
        ],
        out_specs=pl.BlockSpec((1, seq, gwid), lambda i, g: (i, 0, g)),
        scratch_shapes=[pltpu.VMEM((seq + 2 * POOL_PAD, gwid), F32),
                        pltpu.VMEM((seq + 2 * POOL_PAD, gwid), F32)],
        compiler_params=_params(2),
        name="pool_mixer",
    )(u3d, gw_bf16, scale3d)


def _xattn_kernel(q_ref, kv_ref, o_ref):
    scale = XA_HEAD_DIM ** -0.5
    for h in range(XA_HEADS):
        lo = h * XA_HEAD_DIM
        q = q_ref[:, lo:lo + XA_HEAD_DIM]
        k = kv_ref[0, :, lo:lo + XA_HEAD_DIM]
        v = kv_ref[0, :, XA_WIDTH + lo:XA_WIDTH + lo + XA_HEAD_DIM]
        s = _dot_nt(q, k) * scale
        m = jnp.max(s, axis=-1, keepdims=True)
        p = jnp.exp(s - m)
        den = jnp.sum(p, axis=-1, keepdims=True)
        o_ref[:, lo:lo + XA_HEAD_DIM] = (_dot(p.astype(BF16), v) / den).astype(BF16)


def _mem_xattn(qm2d, memkv_layer, seq):
    t = qm2d.shape[0]
    tiles_per_seq = seq // ROW_TILE
    return pl.pallas_call(
        _xattn_kernel,
        out_shape=jax.ShapeDtypeStruct((t, XA_WIDTH), BF16),
        grid=(t // ROW_TILE,),
        in_specs=[
            pl.BlockSpec((ROW_TILE, XA_WIDTH), lambda i: (i, 0)),
            pl.BlockSpec((1, MEM_LEN, 2 * XA_WIDTH), lambda i: (i // tiles_per_seq, 0, 0)),
        ],
        out_specs=pl.BlockSpec((ROW_TILE, XA_WIDTH), lambda i: (i, 0)),
        compiler_params=_params(1),
        name="mem_xattn",
    )(qm2d, memkv_layer)


def _outproj_kernel(tok_ref, mo_ref, x_ref, w_ref, g_ref, rw_ref,
                    x1_ref, h_ref, afft_ref, aff_ref, wcat_ref, x1prev_ref,
                    *, tok_width, n_experts):
    @pl.when(pl.program_id(0) == 0)
    def _():
        rw = rw_ref[...]
        w_hi = rw.astype(BF16)
        wcat_ref[:, 0:LANES] = w_hi
        wcat_ref[:, LANES:2 * LANES] = (rw - w_hi.astype(F32)).astype(BF16)
        x1prev_ref[...] = jnp.zeros_like(x1prev_ref)

    hn = _rmsnorm_rows(x1prev_ref[...], g_ref[...])
    h_prev = hn.astype(BF16)
    h_lo = (hn - h_prev.astype(F32)).astype(BF16)
    r = _dot(h_prev, wcat_ref[...]) + _dot(h_lo, wcat_ref[...])
    logits = r[:, 0:LANES] + r[:, LANES:2 * LANES]
    lt = logits.T[0:n_experts, :]
    m = jnp.max(lt, axis=0, keepdims=True)
    ex = jnp.exp(lt - m)
    afft = ex / jnp.sum(ex, axis=0, keepdims=True)
    afft_ref[0] = afft
    padded = jnp.concatenate(
        [afft, jnp.zeros((LANES - n_experts, afft.shape[1]), F32)], axis=0)
    aff = padded.T
    aff_ref[...] = aff
    h_ref[...] = _pack_row_words(h_prev, _pack_gate_lanes(aff))

    y = _dot(tok_ref[...], w_ref[0:tok_width, :]) + _dot(mo_ref[...], w_ref[tok_width:, :])
    x1 = x_ref[...] + y
    x1_ref[...] = x1
    x1prev_ref[...] = x1


def _outproj(tok2d, mo2d, x2d, w_bf16, g, rw_pad, n_experts, seq):
    t, d_model = x2d.shape
    tok_width = tok2d.shape[1]
    tiles_per_seq = seq // ROW_TILE
    n_tiles = t // ROW_TILE

    def cur(i):
        return jnp.minimum(i, n_tiles - 1)

    def prev(i):
        return jnp.maximum(i - 1, 0)

    return pl.pallas_call(
        functools.partial(_outproj_kernel, tok_width=tok_width, n_experts=n_experts),
        out_shape=(jax.ShapeDtypeStruct((t, d_model), F32),
                   jax.ShapeDtypeStruct((t, d_model // 2 + LANES), U32),
                   jax.ShapeDtypeStruct((t // seq, n_experts, seq), F32),
                   jax.ShapeDtypeStruct((t, LANES), F32)),
        grid=(n_tiles + 1,),
        in_specs=[
            pl.BlockSpec((ROW_TILE, tok_width), lambda i: (cur(i), 0)),
            pl.BlockSpec((ROW_TILE, mo2d.shape[1]), lambda i: (cur(i), 0)),
            pl.BlockSpec((ROW_TILE, d_model), lambda i: (cur(i), 0)),
            _resident((d_model, d_model), lambda i: (0, 0)),
            pl.BlockSpec((1, d_model), lambda i: (0, 0)),
            _resident((d_model, LANES), lambda i: (0, 0)),
        ],
        out_specs=(pl.BlockSpec((ROW_TILE, d_model), lambda i: (cur(i), 0)),
                   pl.BlockSpec((ROW_TILE, d_model // 2 + LANES), lambda i: (prev(i), 0)),
                   pl.BlockSpec((1, n_experts, ROW_TILE),
                                lambda i: (prev(i) // tiles_per_seq, 0, prev(i) % tiles_per_seq)),
                   pl.BlockSpec((ROW_TILE, LANES), lambda i: (prev(i), 0))),
        scratch_shapes=[pltpu.VMEM((d_model, 2 * LANES), BF16),
                        pltpu.VMEM((ROW_TILE, d_model), F32)],
        compiler_params=_params(1),
        name="outproj_router",
    )(tok2d, mo2d, x2d, w_bf16, g, rw_pad)


def _strict_triangle(n, lower):
    r = lax.broadcasted_iota(I32, (n, n), 0)
    c = lax.broadcasted_iota(I32, (n, n), 1)
    return jnp.where((c < r) if lower else (r < c), 1.0, 0.0).astype(BF16)


def _prefix_rows(mask_f32):
    s, l = mask_f32.shape
    tri = _strict_triangle(PREFIX_CHUNK, lower=True)
    carry = jnp.zeros((1, l), F32)
    out = []
    for c in range(0, s, PREFIX_CHUNK):
        m = mask_f32[c:c + PREFIX_CHUNK, :]
        out.append(_dot(tri, m.astype(BF16)) + carry)
        carry = carry + jnp.sum(m, axis=0, keepdims=True)
    return jnp.concatenate(out, axis=0)


def _prefix_lanes(mask_f32):
    e, s = mask_f32.shape
    tri = _strict_triangle(PREFIX_CHUNK, lower=False)
    carry = jnp.zeros((e, 1), F32)
    out = []
    for c in range(0, s, PREFIX_CHUNK):
        m = mask_f32[:, c:c + PREFIX_CHUNK]
        out.append(_dot(m.astype(BF16), tri) + carry)
        carry = carry + jnp.sum(m, axis=1, keepdims=True)
    return jnp.concatenate(out, axis=1)


def _select_slots(key, thr, need, prefix_fn):
    return _select_slots_and_counts(key, thr, need, prefix_fn)[0]


def _select_slots_and_counts(key, thr, need, prefix_fn):
    gt = jnp.where(key > thr, 1.0, 0.0)
    eq = jnp.where(key == thr, 1.0, 0.0)
    eq_rank = prefix_fn(eq)
    sel = gt + eq * jnp.where(eq_rank < need, 1.0, 0.0)
    pos = prefix_fn(sel)
    return jnp.where(sel > 0.5, pos, -1.0), pos


GATE_GROUP = 16
GATE_PIECES = 3


def _pack_gate_lanes(aff):
    hi = aff.astype(BF16).astype(F32)
    r1 = aff - hi
    mid = r1.astype(BF16).astype(F32)
    lo = (r1 - mid).astype(BF16).astype(F32)
    packed = hi + pltpu.roll(mid, GATE_GROUP, 1) + pltpu.roll(lo, 2 * GATE_GROUP, 1)
    return packed.astype(BF16)


def _unpack_gate(tail, e):
    lane = lax.broadcasted_iota(I32, (1, LANES), 1)
    mine = ((lane & (GATE_GROUP - 1)) == e) & (lane < GATE_PIECES * GATE_GROUP)
    return jnp.sum(jnp.where(mine, tail.astype(F32), 0.0), axis=1, keepdims=True)


def _pack_row_words(h, gate_tile):
    rows, d_model = h.shape
    half = d_model // 2
    hi = jnp.concatenate([h[:, 0:half], gate_tile], axis=1).astype(F32)
    lo = jnp.concatenate([h[:, half:], jnp.zeros((rows, LANES), BF16)], axis=1).astype(F32)
    return pltpu.bitcast(hi, U32) | (pltpu.bitcast(lo, U32) >> 16)


def _unpack_row_words(words):
    hi = pltpu.bitcast(words & jnp.uint32(0xFFFF0000), F32)
    lo = pltpu.bitcast(words << 16, F32)
    return hi, lo


def _route_kernel(afft_ref, posm_ref, thr_ref, need_ref, starts_ref, *, cap):
    n_experts, seq = afft_ref.shape[1], afft_ref.shape[2]
    key = pltpu.bitcast(afft_ref[0], I32)
    thr = jnp.zeros((n_experts, 1), I32)
    for bit in range(F32_KEY_BITS - 1, -1, -1):
        cand = thr | (1 << bit)
        cnt = jnp.sum(jnp.where(key >= cand, 1.0, 0.0), axis=1, keepdims=True)
        thr = jnp.where(cnt >= cap, cand, thr)
    n_gt = jnp.sum(jnp.where(key > thr, 1.0, 0.0), axis=1, keepdims=True)
    need = cap - n_gt
    thr_ref[0] = thr
    need_ref[0] = need
    posm, before = _select_slots_and_counts(key, thr, need, _prefix_lanes)
    posm_ref[0] = posm.astype(I32)
    starts_ref[0] = jnp.concatenate(
        [before[:, r:r + 1] for r in range(0, seq, COMBINE_ROWS)], axis=1).astype(I32)


def _route(afft, cap):
    b, n_experts, seq = afft.shape
    n_tiles = seq // COMBINE_ROWS
    return pl.pallas_call(
        functools.partial(_route_kernel, cap=cap),
        out_shape=(jax.ShapeDtypeStruct((b, n_experts, seq), I32),
                   jax.ShapeDtypeStruct((b, n_experts, 1), I32),
                   jax.ShapeDtypeStruct((b, n_experts, 1), F32),
                   jax.ShapeDtypeStruct((b, n_experts, n_tiles), I32)),
        grid=(b,),
        in_specs=[pl.BlockSpec((1, n_experts, seq), lambda i: (i, 0, 0))],
        out_specs=(pl.BlockSpec((1, n_experts, seq), lambda i: (i, 0, 0)),
                   pl.BlockSpec((1, n_experts, 1), lambda i: (i, 0, 0)),
                   pl.BlockSpec((1, n_experts, 1), lambda i: (i, 0, 0)),
                   pl.BlockSpec((1, n_experts, n_tiles), lambda i: (i, 0, 0))),
        compiler_params=_params(1),
        name="expert_route",
    )(afft)


SC_LANES = 16
SC_GATHER_ROWS = 64


def _sc_expert_gather(posm2d, h_words, n_experts, e_offset, n_e, b, seq, cap):
    width = h_words.shape[1]
    info = plsc.get_sparse_core_info()
    n_cores, n_subcores = info.num_cores, info.num_subcores
    n_workers = n_cores * n_subcores
    assert info.num_lanes == SC_LANES and (b * n_e) % n_workers == 0
    pairs_per_worker = (b * n_e) // n_workers
    mesh = plsc.VectorSubcoreMesh(core_axis_name="c", subcore_axis_name="s")

    @functools.partial(
        pl.kernel, mesh=mesh,
        out_type=jax.ShapeDtypeStruct((n_e * b * cap, width), U32),
        compiler_params=pltpu.CompilerParams(needs_layout_passes=False),
        scratch_types=[
            pltpu.VMEM((seq,), I32),
            pltpu.VMEM((cap,), I32),
            pltpu.VMEM((SC_GATHER_ROWS, width), U32),
            pltpu.SemaphoreType.DMA,
        ],
        name="sc_expert_gather",
    )
    def gather(posm_hbm, h_hbm, out_hbm, pos_v, idx_v, rows_v, sem):
        wid = lax.axis_index("s") * n_cores + lax.axis_index("c")
        for p in range(pairs_per_worker):
            pair = wid * pairs_per_worker + p
            bi = pair // n_e
            e = pair - bi * n_e
            pltpu.sync_copy(posm_hbm.at[bi * n_experts + e_offset + e], pos_v)

            @pl.loop(0, seq, step=SC_LANES)
            def _(t0):
                slots = pos_v[pl.ds(t0, SC_LANES)]
                rows = lax.iota(I32, SC_LANES) + (t0 + bi * seq)
                plsc.store_scatter(idx_v, [slots], rows, mask=slots >= 0)

            out_base = (e * b + bi) * cap
            for c in range(cap // SC_GATHER_ROWS):
                chunk = idx_v.at[pl.ds(c * SC_GATHER_ROWS, SC_GATHER_ROWS)]
                pltpu.async_copy(h_hbm.at[chunk], rows_v, sem).wait()
                pltpu.sync_copy(rows_v, out_hbm.at[pl.ds(out_base + c * SC_GATHER_ROWS, SC_GATHER_ROWS)])

    return gather(posm2d, h_words).reshape(n_e, b, cap, width)


EXPERT_ROWS = 1024
EXPERT_FTILE = 256


EXPERT_WBUFS = 2


def _expert_kernel(xg_ref, wg_hbm, wu_hbm, wd_hbm, o_ref,
                   x_ref, g_ref, hact_ref, wg_full, wu_full, wd_full, wg_buf, wu_buf, wd_buf, sem,
                   *, layer, e_offset, n_ftiles):
    e = pl.program_id(0)
    m = pl.program_id(1)
    n_e = pl.num_programs(0)
    nb, cap, width = xg_ref.shape[1], xg_ref.shape[2], xg_ref.shape[3]
    half = width - LANES
    d_model = 2 * half
    rows = nb * cap
    tf = EXPERT_FTILE
    assert n_ftiles % EXPERT_WBUFS == 0

    def tile_copies(expert, f):
        ge = e_offset + expert
        slot = f % EXPERT_WBUFS
        return (
            pltpu.make_async_copy(wg_hbm.at[layer, ge, :, pl.ds(f * tf, tf)], wg_buf.at[slot], sem.at[0, slot]),
            pltpu.make_async_copy(wu_hbm.at[layer, ge, :, pl.ds(f * tf, tf)], wu_buf.at[slot], sem.at[1, slot]),
            pltpu.make_async_copy(wd_hbm.at[layer, ge, pl.ds(f * tf, tf), :], wd_buf.at[slot], sem.at[2, slot]),
        )

    def start(expert, f):
        for cp in tile_copies(expert, f):
            cp.start()

    @pl.when((e == 0) & (m == 0))
    def _():
        for f in range(EXPERT_WBUFS):
            start(e, f)

    for i in range(nb):
        r0 = i * cap
        hi, lo = _unpack_row_words(xg_ref[0, i])
        x_ref[r0:r0 + cap, 0:half] = hi[:, 0:half].astype(BF16)
        x_ref[r0:r0 + cap, half:d_model] = lo[:, 0:half].astype(BF16)
        g_ref[r0:r0 + cap, :] = jnp.broadcast_to(
            _unpack_gate(hi[:, half:width], e_offset + e), (cap, LANES))

    def receive(f):
        slot = f % EXPERT_WBUFS
        for cp in tile_copies(e, f):
            cp.wait()
        wg_full[:, f * tf:(f + 1) * tf] = wg_buf[slot].astype(BF16)
        wu_full[:, f * tf:(f + 1) * tf] = wu_buf[slot].astype(BF16)
        wd_full[f * tf:(f + 1) * tf, :] = wd_buf[slot].astype(BF16)
        ahead = f + EXPERT_WBUFS
        if ahead < n_ftiles:
            start(e, ahead)
        else:
            @pl.when(e + 1 < n_e)
            def _():
                start(e + 1, ahead - n_ftiles)

    def body(first_group):
        x = x_ref[...]
        for f in range(n_ftiles):
            if first_group:
                receive(f)
            a = _dot(x, wg_full[:, f * tf:(f + 1) * tf])
            u = _dot(x, wu_full[:, f * tf:(f + 1) * tf])
            hact_ref[:, f * tf:(f + 1) * tf] = (a * jax.nn.sigmoid(a) * u).astype(BF16)
        for c in range(0, d_model, COL_CHUNK):
            y = _dot(hact_ref[...], wd_full[:, c:c + COL_CHUNK])
            for j in range(0, COL_CHUNK, LANES):
                o_ref[:, 0, :, c + j:c + j + LANES] = (
                    (y[:, j:j + LANES] * g_ref[...]).astype(BF16).reshape(nb, cap, LANES))

    @pl.when(m == 0)
    def _():
        body(True)

    @pl.when(m != 0)
    def _():
        body(False)


def _experts(xg, w_gate, w_up, w_down, layer, e_offset):
    n_e, b, cap, width = xg.shape
    d_model = 2 * (width - LANES)
    d_expert = w_gate.shape[3]
    nb = EXPERT_ROWS // cap
    n_ftiles = d_expert // EXPERT_FTILE
    return pl.pallas_call(
        functools.partial(_expert_kernel, layer=layer, e_offset=e_offset, n_ftiles=n_ftiles),
        out_shape=jax.ShapeDtypeStruct((b, n_e, cap, d_model), BF16),
        grid=(n_e, b // nb),
        in_specs=[
            pl.BlockSpec((1, nb, cap, width), lambda e, m: (e, m, 0, 0)),
            pl.BlockSpec(memory_space=pl.ANY),
            pl.BlockSpec(memory_space=pl.ANY),
            pl.BlockSpec(memory_space=pl.ANY),
        ],
        out_specs=pl.BlockSpec((nb, 1, cap, d_model), lambda e, m: (m, e, 0, 0)),
        scratch_shapes=[pltpu.VMEM((EXPERT_ROWS, d_model), BF16),
                        pltpu.VMEM((EXPERT_ROWS, LANES), F32),
                        pltpu.VMEM((EXPERT_ROWS, d_expert), BF16),
                        pltpu.VMEM((d_model, d_expert), BF16),
                        pltpu.VMEM((d_model, d_expert), BF16),
                        pltpu.VMEM((d_expert, d_model), BF16),
                        pltpu.VMEM((EXPERT_WBUFS, d_model, EXPERT_FTILE), F32),
                        pltpu.VMEM((EXPERT_WBUFS, d_model, EXPERT_FTILE), F32),
                        pltpu.VMEM((EXPERT_WBUFS, EXPERT_FTILE, d_model), F32),
                        pltpu.SemaphoreType.DMA((3, EXPERT_WBUFS))],
        compiler_params=_params(2),
        name="experts",
    )(xg, w_gate, w_up, w_down)


COMBINE_ROWS = 256
COMBINE_WINDOW = 64
MXU_DEPTH = 256
BF16_ROWS = 16


def _combine_kernel(starts_ref, aff_ref, thr_ref, need_ref, x1_ref, *rest, cap, n_experts, final_norm):
    n_y = len(rest) - (6 if final_norm else 5)
    y_hbm = rest[:n_y]
    g_ref = rest[n_y] if final_norm else None
    o_ref, post_ref, pfull_ref, ybuf, sem = rest[-5:]
    b = pl.program_id(0)
    t = pl.program_id(1)
    n_b = pl.num_programs(0)
    n_t = pl.num_programs(1)
    rows, win = COMBINE_ROWS, COMBINE_WINDOW
    experts_per_group = n_experts // n_y
    group_rows = ybuf.shape[0] // (2 * n_y)

    def ybuf_row(buf_slot, g):
        return pl.multiple_of((buf_slot * n_y + g) * group_rows, BF16_ROWS)

    n_chunks = sem.shape[1]
    chunks_per_group = n_chunks // n_y
    chunk_rows = group_rows // chunks_per_group
    slot = b % 2

    def chunk_copy(seq_idx, c, dst_slot):
        g, r = c // chunks_per_group, (c % chunks_per_group) * chunk_rows
        return pltpu.make_async_copy(y_hbm[g].at[seq_idx, pl.ds(r, chunk_rows), :],
                                     ybuf.at[pl.ds(ybuf_row(dst_slot, g) + r, chunk_rows), :],
                                     sem.at[dst_slot, c])

    @pl.when((b == 0) & (t == 0))
    def _():
        for c in range(n_chunks):
            chunk_copy(0, c, 0).start()

    @pl.when(t == 0)
    def _():
        for c in range(n_chunks):
            chunk_copy(b, c, slot).wait()
        key = pltpu.bitcast(aff_ref[...], I32)
        post_ref[...] = _select_slots(key, thr_ref[0], need_ref[0], _prefix_rows)

    for c in range(n_chunks):
        @pl.when((t == c) & (b + 1 < n_b))
        def _():
            chunk_copy(b + 1, c, 1 - slot).start()

    posm = post_ref[pl.ds(pl.multiple_of(t * rows, rows), rows), :]
    base = (b * (n_t + 1) + t) * n_experts
    wstart, ok = [], None
    for e in range(n_experts):
        first = starts_ref[base + e]
        end = starts_ref[base + n_experts + e]
        w0 = jnp.minimum((first // BF16_ROWS) * BF16_ROWS, cap - win)
        fits = end - w0 <= win
        wstart.append(w0)
        ok = fits if ok is None else jnp.logical_and(ok, fits)

    def finish(acc):
        if final_norm:
            acc = _rmsnorm_rows(acc, g_ref[...])
        o_ref[0] = acc

    @pl.when(ok)
    def _():
        per_dot, per_tile = MXU_DEPTH // win, LANES // win
        lane = lax.broadcasted_iota(I32, (1, LANES), 1)
        lane_f = lane.astype(F32)
        acc = x1_ref[0]
        for e0 in range(0, n_experts, per_dot):
            onehots, windows = [], []
            for e1 in range(e0, e0 + per_dot, per_tile):
                rel = None
                for k in range(per_tile - 1, -1, -1):
                    e = e1 + k
                    shifted = posm[:, e:e + 1] - (wstart[e] - k * win).astype(F32)
                    rel = shifted if rel is None else jnp.where(lane < (k + 1) * win, shifted, rel)
                onehots.append(jnp.where(rel == lane_f, 1.0, 0.0).astype(BF16))
            for e in range(e0, e0 + per_dot):
                g, el = divmod(e, experts_per_group)
                r0 = pl.multiple_of(ybuf_row(slot, g) + el * cap + wstart[e], BF16_ROWS)
                windows.append(ybuf[pl.ds(r0, win), :])
            acc = acc + _dot(jnp.concatenate(onehots, axis=1), jnp.concatenate(windows, axis=0))
        finish(acc)

    @pl.when(jnp.logical_not(ok))
    def _():
        slot_ids = lax.broadcasted_iota(I32, (1, cap), 1).astype(F32)
        for e in range(n_experts):
            pfull_ref[:, e * cap:(e + 1) * cap] = jnp.where(
                posm[:, e:e + 1] == slot_ids, 1.0, 0.0).astype(BF16)
        acc = x1_ref[0]
        for g in range(n_y):
            acc = acc + _dot(pfull_ref[:, g * group_rows:(g + 1) * group_rows],
                             ybuf[pl.ds(ybuf_row(slot, g), group_rows), :])
        finish(acc)


def _combine(starts, aff2d, thr_row, need_row, x1_3d, y_groups, cap, n_experts, final_g=None):
    b, seq, d_model = x1_3d.shape
    assert sum(y.shape[1] for y in y_groups) == n_experts * cap and cap >= COMBINE_WINDOW
    final_norm = final_g is not None
    n_tiles = seq // COMBINE_ROWS
    group_rows = y_groups[0].shape[1]
    assert all(y.shape[1] == group_rows for y in y_groups)
    assert n_tiles % len(y_groups) == 0 and group_rows % (n_tiles // len(y_groups)) == 0
    y_specs = [pl.BlockSpec(memory_space=pl.ANY) for _ in y_groups]
    g_specs = [pl.BlockSpec((1, d_model), lambda i, t, s: (0, 0))] if final_norm else []
    g_args = [final_g] if final_norm else []
    return pl.pallas_call(
        functools.partial(_combine_kernel, cap=cap, n_experts=n_experts, final_norm=final_norm),
        out_shape=jax.ShapeDtypeStruct((b, seq, d_model), F32),
        grid_spec=pltpu.PrefetchScalarGridSpec(
            num_scalar_prefetch=1,
            grid=(b, n_tiles),
            in_specs=[
                pl.BlockSpec((seq, LANES), lambda i, t, s: (i, 0)),
                pl.BlockSpec((1, 1, LANES), lambda i, t, s: (i, 0, 0)),
                pl.BlockSpec((1, 1, LANES), lambda i, t, s: (i, 0, 0)),
                pl.BlockSpec((1, COMBINE_ROWS, d_model), lambda i, t, s: (i, t, 0)),
            ] + y_specs + g_specs,
            out_specs=pl.BlockSpec((1, COMBINE_ROWS, d_model), lambda i, t, s: (i, t, 0)),
            scratch_shapes=[pltpu.VMEM((seq, LANES), F32),
                            pltpu.VMEM((COMBINE_ROWS, n_experts * cap), BF16),
                            pltpu.VMEM((2 * len(y_groups) * group_rows, d_model), BF16),
                            pltpu.SemaphoreType.DMA((2, n_tiles))],
        ),
        compiler_params=_params(2),
        name="combine",
    )(starts, aff2d, thr_row, need_row, x1_3d, *y_groups, *g_args)


EXPERT_GROUPS = 2


def _moe(afft, aff2d, h_words, x1_2d, w_gate, w_up, w_down, layer, b, seq, final_g=None):
    n_experts = afft.shape[1]
    assert n_experts <= GATE_GROUP and n_experts % EXPERT_GROUPS == 0
    d_model = x1_2d.shape[1]
    cap = CAPACITY_FACTOR * seq // n_experts
    n_e = n_experts // EXPERT_GROUPS
    posm, thr, need, tile_starts = _route(afft, cap)
    starts = jnp.concatenate([jnp.swapaxes(tile_starts, 1, 2),
                              jnp.full((b, 1, n_experts), cap, I32)], axis=1).reshape(-1)
    posm2d = posm.reshape(b * n_experts, seq)
    xgs = [_sc_expert_gather(posm2d, h_words, n_experts, g * n_e, n_e, b, seq, cap)
           for g in range(EXPERT_GROUPS)]
    ys = [_experts(xg, w_gate, w_up, w_down, layer, g * n_e).reshape(b, n_e * cap, d_model)
          for g, xg in enumerate(xgs)]
    pad = LANES - n_experts
    thr_row = jnp.pad(thr.reshape(b, 1, n_experts), ((0, 0), (0, 0), (0, pad)),
                      constant_values=np.iinfo(np.int32).max)
    need_row = jnp.pad(need.reshape(b, 1, n_experts), ((0, 0), (0, 0), (0, pad)))
    return _combine(starts, aff2d, thr_row, need_row, x1_2d.reshape(b, seq, d_model), ys, cap,
                    n_experts, final_g)


def _rotary_tile(t, cos, sin_lo, sin_hi):
    half = ROT_DIM // 2
    return t * cos + pltpu.roll(t, LANES - half, 1) * sin_lo + pltpu.roll(t, half, 1) * sin_hi


def _inproj_attn_kernel(x_ref, g_ref, w_ref, pos_ref, rot_ref,
                        q_ref, k_ref, v_ref, qm_ref, *, tok_width, kv_width):
    hn = _rmsnorm_rows(x_ref[...], g_ref[...]).astype(BF16)
    ang = pos_ref[...].astype(F32) * rot_ref[0:1, :]
    cos = jnp.cos(ang)
    sin = jnp.sin(ang)
    sin_lo = sin * rot_ref[1:2, :]
    sin_hi = sin * rot_ref[2:3, :]
    qscale = HEAD_DIM ** -0.5
    for c in range(0, tok_width, COL_CHUNK):
        pc = _dot(hn, w_ref[:, c:c + COL_CHUNK])
        for j in range(0, COL_CHUNK, LANES):
            rot = _rotary_tile(pc[:, j:j + LANES], cos, sin_lo, sin_hi)
            q_ref[:, c + j:c + j + LANES] = (rot * qscale).astype(BF16)
    kv = _dot(hn, w_ref[:, tok_width:tok_width + 2 * kv_width])
    k01 = _rotary_tile(kv[:, 0:LANES], cos, sin_lo, sin_hi)
    k2x = _rotary_tile(kv[:, LANES:2 * LANES], cos, sin_lo, sin_hi)
    k_ref[0, 0] = k01[:, 0:HEAD_DIM].astype(BF16)
    k_ref[0, 1] = k01[:, HEAD_DIM:LANES].astype(BF16)
    k_ref[0, 2] = k2x[:, 0:HEAD_DIM].astype(BF16)
    for hh in range(kv_width // HEAD_DIM):
        lo = kv_width + hh * HEAD_DIM
        v_ref[0, hh] = kv[:, lo:lo + HEAD_DIM].astype(BF16)
    qm_ref[...] = _dot(hn, w_ref[:, tok_width + 2 * kv_width:]).astype(BF16)


def _inproj_attn(x2d, g, w_bf16, pos2d, rot_rows, tok_width, kv_width, seq):
    t, d_model = x2d.shape
    n = w_bf16.shape[1]
    n_kv = kv_width // HEAD_DIM
    assert n_kv == 3 and kv_width + HEAD_DIM == 2 * LANES
    tiles_per_seq = seq // ROW_TILE
    kv_spec = pl.BlockSpec((1, n_kv, ROW_TILE, HEAD_DIM),
                           lambda i: (i // tiles_per_seq, 0, i % tiles_per_seq, 0))
    return pl.pallas_call(
        functools.partial(_inproj_attn_kernel, tok_width=tok_width, kv_width=kv_width),
        out_shape=(jax.ShapeDtypeStruct((t, tok_width), BF16),
                   jax.ShapeDtypeStruct((t // seq, n_kv, seq, HEAD_DIM), BF16),
                   jax.ShapeDtypeStruct((t // seq, n_kv, seq, HEAD_DIM), BF16),
                   jax.ShapeDtypeStruct((t, n - tok_width - 2 * kv_width), BF16)),
        grid=(t // ROW_TILE,),
        in_specs=[
            pl.BlockSpec((ROW_TILE, d_model), lambda i: (i, 0)),
            pl.BlockSpec((1, d_model), lambda i: (0, 0)),
            _resident((d_model, n), lambda i: (0, 0)),
            pl.BlockSpec((ROW_TILE, 1), lambda i: (i, 0)),
            pl.BlockSpec((8, LANES), lambda i: (0, 0)),
        ],
        out_specs=(pl.BlockSpec((ROW_TILE, tok_width), lambda i: (i, 0)),
                   kv_spec, kv_spec,
                   pl.BlockSpec((ROW_TILE, n - tok_width - 2 * kv_width), lambda i: (i, 0))),
        compiler_params=_params(1),
        name="inproj_attn",
    )(x2d, g, w_bf16, pos2d, rot_rows)


def _wattn_kernel(sink_ref, q_ref, kp_ref, kc_ref, kn_ref, vp_ref, vc_ref, vn_ref, o_ref,
                  valid_ref, kpad_ref, vpad_ref, s_ref, p_ref, inv_ref, *, seq):
    n = pl.program_id(1)
    n_kv = kc_ref.shape[1]
    pairs = GQA_RATIO // 2
    half_rows = pairs * BLOCK
    qi = lax.broadcasted_iota(I32, (BLOCK, 3 * BLOCK), 0)
    kj = lax.broadcasted_iota(I32, (BLOCK, 3 * BLOCK), 1)
    first = jnp.maximum(qi, BLOCK - n * BLOCK)
    last = jnp.minimum(qi + 2 * WINDOW, seq + BLOCK - 1 - n * BLOCK)
    valid_ref[...] = jnp.where(((kj - first) | (last - kj)) >= 0, 1.0, 0.0)
    zeros = jnp.zeros((3 * BLOCK, HEAD_DIM), BF16)
    ones_col = jnp.where(lax.broadcasted_iota(I32, (3 * BLOCK, HEAD_DIM), 1) == 0, 1.0, 0.0).astype(BF16)
    low_half = lax.broadcasted_iota(I32, (1, LANES), 1) < HEAD_DIM
    for hk in range(n_kv):
        kw = jnp.concatenate([kp_ref[0, hk], kc_ref[0, hk], kn_ref[0, hk]], axis=0)
        vw = jnp.concatenate([vp_ref[0, hk], vc_ref[0, hk], vn_ref[0, hk]], axis=0)
        kpad_ref[2 * hk] = jnp.concatenate([kw, zeros], axis=1)
        kpad_ref[2 * hk + 1] = jnp.concatenate([zeros, kw], axis=1)
        vpad_ref[2 * hk] = jnp.concatenate([vw, ones_col], axis=1)
        vpad_ref[2 * hk + 1] = jnp.concatenate([ones_col, vw], axis=1)
        tile0 = hk * pairs
        qs = jnp.concatenate(
            [q_ref[0, :, (tile0 + j) * LANES:(tile0 + j + 1) * LANES] for j in range(pairs)], axis=0)
        s_ref[hk, 0:half_rows, :] = _dot_nt(qs, kpad_ref[2 * hk])
        s_ref[hk, half_rows:2 * half_rows, :] = _dot_nt(qs, kpad_ref[2 * hk + 1])
    for hk in range(n_kv):
        for c in range(GQA_RATIO):
            j, odd = c % pairs, c // pairs
            r = c * BLOCK
            s = jnp.concatenate([
                jnp.where(valid_ref[:, 0:BLOCK] > 0.5, s_ref[hk, r:r + BLOCK, 0:BLOCK], NEG_INF),
                s_ref[hk, r:r + BLOCK, BLOCK:2 * BLOCK],
                jnp.where(valid_ref[:, 2 * BLOCK:] > 0.5, s_ref[hk, r:r + BLOCK, 2 * BLOCK:], NEG_INF),
            ], axis=1)
            sk = sink_ref[hk * GQA_RATIO + 2 * j + odd]
            m = jnp.maximum(jnp.max(s, axis=-1, keepdims=True), sk)
            p_ref[hk, r:r + BLOCK, :] = jnp.exp(s - m).astype(BF16)
            inv_ref[hk, j * BLOCK:(j + 1) * BLOCK, odd * HEAD_DIM:(odd + 1) * HEAD_DIM] = (
                jnp.broadcast_to(jnp.exp(sk - m), (BLOCK, HEAD_DIM)))
    for hk in range(n_kv):
        pv_even = _dot(p_ref[hk, 0:half_rows, :], vpad_ref[2 * hk])
        pv_odd = _dot(p_ref[hk, half_rows:2 * half_rows, :], vpad_ref[2 * hk + 1])
        den = jnp.where(low_half, pv_even[:, HEAD_DIM:HEAD_DIM + 1], pv_odd[:, 0:1]) + inv_ref[hk]
        o = jnp.where(low_half, pv_even, pv_odd) / den
        for j in range(pairs):
            lo = (hk * pairs + j) * LANES
            o_ref[0, :, lo:lo + LANES] = o[j * BLOCK:(j + 1) * BLOCK].astype(BF16)


def _window_attention(sink, q3d, k4d, v4d):
    b, seq, tok_width = q3d.shape
    n_kv = k4d.shape[1]
    nb = seq // BLOCK
    kv_block = (1, n_kv, BLOCK, HEAD_DIM)
    prev_spec = pl.BlockSpec(kv_block, lambda i, n: (i, 0, jnp.maximum(n - 1, 0), 0))
    cur_spec = pl.BlockSpec(kv_block, lambda i, n: (i, 0, n, 0))
    next_spec = pl.BlockSpec(kv_block, lambda i, n: (i, 0, jnp.minimum(n + 1, nb - 1), 0))
    return pl.pallas_call(
        functools.partial(_wattn_kernel, seq=seq),
        out_shape=jax.ShapeDtypeStruct((b, seq, tok_width), BF16),
        grid=(b, nb),
        in_specs=[
            pl.BlockSpec(memory_space=pltpu.SMEM),
            pl.BlockSpec((1, BLOCK, tok_width), lambda i, n: (i, n, 0)),
            prev_spec, cur_spec, next_spec, prev_spec, cur_spec, next_spec,
        ],
        out_specs=pl.BlockSpec((1, BLOCK, tok_width), lambda i, n: (i, n, 0)),
        scratch_shapes=[pltpu.VMEM((BLOCK, 3 * BLOCK), F32),
                        pltpu.VMEM((2 * n_kv, 3 * BLOCK, LANES), BF16),
                        pltpu.VMEM((2 * n_kv, 3 * BLOCK, LANES), BF16),
                        pltpu.VMEM((n_kv, GQA_RATIO * BLOCK, 3 * BLOCK), F32),
                        pltpu.VMEM((n_kv, GQA_RATIO * BLOCK, 3 * BLOCK), BF16),
                        pltpu.VMEM((n_kv, GQA_RATIO // 2 * BLOCK, LANES), F32)],
        compiler_params=_params(2),
        name="window_attention",
    )(sink, q3d, k4d, k4d, k4d, v4d, v4d, v4d)


def _rotary_rows(dtype=F32):
    half = ROT_DIM // 2
    inv_freq = ROPE_THETA ** (-jnp.arange(0, ROT_DIM, 2, dtype=jnp.float32) / ROT_DIM)
    lane = np.arange(LANES) % HEAD_DIM
    rotated = lane < ROT_DIM
    freq = jnp.where(jnp.asarray(rotated), inv_freq[jnp.asarray(lane % half)], 0.0)
    rows = jnp.zeros((8, LANES), dtype)
    rows = rows.at[0].set(freq)
    rows = rows.at[1].set(jnp.asarray(np.where(lane < half, -1.0, 0.0), dtype))
    rows = rows.at[2].set(jnp.asarray(np.where(rotated & (lane >= half), 1.0, 0.0), dtype))
    return rows


def kernel(x, mem, positions, norm_mix_g, norm_ffn_g, mem_norm_g, final_g, mem_w_kv,
           pool_w_in, pool_group_w, pool_scale, pool_w_out,
           attn_w_in, attn_sink, attn_w_out,
           router_w, exp_w_gate, exp_w_up, exp_w_down):
    b, seq, d_model = x.shape
    depth = norm_mix_g.shape[0]
    t = b * seq
    n_experts = router_w.shape[2]
    tok_width = pool_scale.shape[1]
    n_groups = pool_group_w.shape[1]
    kv_width = (attn_w_in.shape[2] - tok_width - XA_WIDTH) // 2
    assert seq % ROW_TILE == 0 and mem.shape[1] == MEM_LEN

    memkv = _memkv(mem.reshape(b * MEM_LEN, d_model), mem_norm_g.reshape(1, d_model),
                   mem_w_kv.astype(BF16))
    memkv = memkv.reshape(depth, b, MEM_LEN, 2 * XA_WIDTH)
    rw_pad = jnp.pad(router_w, ((0, 0), (0, 0), (0, LANES - n_experts)))
    pos2d = positions.reshape(t, 1)
    rot_rows = _rotary_rows()

    x2d = x.reshape(t, d_model)
    for layer in range(depth):
        j = layer // 2
        g_mix = norm_mix_g[layer].reshape(1, d_model)
        if layer % 2 == 0:
            u, qm = _inproj_pool(x2d, g_mix, pool_w_in[j].astype(BF16), tok_width)
            tok = _pool_mixer(u.reshape(b, seq, tok_width), pool_group_w[j].astype(BF16),
                              pool_scale[j].reshape(n_groups, 1, tok_width // n_groups))
            tok = tok.reshape(t, tok_width)
            w_out = pool_w_out[j]
        else:
            q, k, v, qm = _inproj_attn(x2d, g_mix, attn_w_in[j].astype(BF16), pos2d, rot_rows,
                                       tok_width, kv_width, seq)
            tok = _window_attention(attn_sink[j], q.reshape(b, seq, tok_width), k, v)
            tok = tok.reshape(t, tok_width)
            w_out = attn_w_out[j]
        mo = _mem_xattn(qm, memkv[layer], seq)
        x1, h, afft, aff = _outproj(tok, mo, x2d, w_out.astype(BF16),
                                    norm_ffn_g[layer].reshape(1, d_model), rw_pad[layer],
                                    n_experts, seq)
        last = layer == depth - 1
        x2 = _moe(afft, aff, h, x1, exp_w_gate, exp_w_up, exp_w_down, layer, b, seq,
                  final_g.reshape(1, d_model) if last else None)
        x2d = x2.reshape(t, d_model)
    return x2d.reshape(b, seq, d_model)
```

```python
import functools

import jax
import jax.numpy as jnp
import numpy as np
from jax import lax
from jax.experimental import pallas as pl
from jax.experimental.pallas import tpu as pltpu
from jax.experimental.pallas import tpu_sc as plsc

F32 = jnp.float32
BF16 = jnp.bfloat16
I32 = jnp.int32
U32 = jnp.uint32

EPS = 1e-6
MEM_LEN = 256
XA_HEADS = 4
XA_HEAD_DIM = 128
XA_WIDTH = XA_HEADS * XA_HEAD_DIM
POOL_WINDOWS = (2, 4, 8, 16)
HEAD_DIM = 64
GQA_RATIO = 8
WINDOW = 128
BLOCK = 128
ROPE_THETA = 500000.0
ROT_DIM = 16
NEG_INF = -1e30
CAPACITY_FACTOR = 2

LANES = 128
MIB = 1024 * 1024
VMEM_LIMIT_BYTES = 56 * MIB

ROW_TILE = 512
POOL_PAD = 16
PREFIX_CHUNK = 256
COL_CHUNK = 512
F32_KEY_BITS = 31


def _params(n_grid_dims):
    return pltpu.CompilerParams(
        dimension_semantics=("arbitrary",) * n_grid_dims,
        vmem_limit_bytes=VMEM_LIMIT_BYTES,
    )


def _resident(block_shape, index_map):
    return pl.BlockSpec(block_shape, index_map, pipeline_mode=pl.Buffered(1))


def _rmsnorm_rows(x, g):
    return x * lax.rsqrt(jnp.mean(x * x, axis=-1, keepdims=True) + EPS) * g


def _dot(a, b):
    return jnp.dot(a, b, preferred_element_type=F32)


def _dot_nt(a, b):
    return lax.dot_general(a, b, (((1,), (1,)), ((), ())), preferred_element_type=F32)


def _memkv_kernel(mem_ref, g_ref, w_ref, o_ref):
    hn = _rmsnorm_rows(mem_ref[...], g_ref[...]).astype(BF16)
    o_ref[0] = _dot(hn, w_ref[0]).astype(BF16)


def _memkv(mem2d, g, w_bf16):
    depth, d_model, n = w_bf16.shape
    rows = mem2d.shape[0]
    return pl.pallas_call(
        _memkv_kernel,
        out_shape=jax.ShapeDtypeStruct((depth, rows, n), BF16),
        grid=(depth, rows // ROW_TILE),
        in_specs=[
            pl.BlockSpec((ROW_TILE, d_model), lambda l, i: (i, 0)),
            pl.BlockSpec((1, d_model), lambda l, i: (0, 0)),
            pl.BlockSpec((1, d_model, n), lambda l, i: (l, 0, 0)),
        ],
        out_specs=pl.BlockSpec((1, ROW_TILE, n), lambda l, i: (l, i, 0)),
        compiler_params=_params(2),
        name="memkv",
    )(mem2d, g, w_bf16)


def _inproj_pool_kernel(x_ref, g_ref, w_ref, u_ref, qm_ref, *, tok_width):
    hn = _rmsnorm_rows(x_ref[...], g_ref[...]).astype(BF16)
    for c in range(0, tok_width, COL_CHUNK):
        u_ref[:, c:c + COL_CHUNK] = _dot(hn, w_ref[:, c:c + COL_CHUNK])
    qm_ref[...] = _dot(hn, w_ref[:, tok_width:]).astype(BF16)


def _inproj_pool(x2d, g, w_bf16, tok_width):
    t, d_model = x2d.shape
    n = w_bf16.shape[1]
    return pl.pallas_call(
        functools.partial(_inproj_pool_kernel, tok_width=tok_width),
        out_shape=(jax.ShapeDtypeStruct((t, tok_width), F32),
                   jax.ShapeDtypeStruct((t, n - tok_width), BF16)),
        grid=(t // ROW_TILE,),
        in_specs=[
            pl.BlockSpec((ROW_TILE, d_model), lambda i: (i, 0)),
            pl.BlockSpec((1, d_model), lambda i: (0, 0)),
            _resident((d_model, n), lambda i: (0, 0)),
        ],
        out_specs=(pl.BlockSpec((ROW_TILE, tok_width), lambda i: (i, 0)),
                   pl.BlockSpec((ROW_TILE, n - tok_width), lambda i: (i, 0))),
        compiler_params=_params(1),
        name="inproj_pool",
    )(x2d, g, w_bf16)


def _pool_group(u_ref, gw_ref, sc_ref, o_ref, a_ref, b_ref, *, window, seq):
    gwid = u_ref.shape[2]
    rows = seq + 2 * POOL_PAD
    zeros_pad = jnp.zeros((POOL_PAD, gwid), F32)
    a_ref[0:POOL_PAD, :] = zeros_pad
    b_ref[0:POOL_PAD, :] = zeros_pad
    a_ref[POOL_PAD:POOL_PAD + seq, :] = u_ref[0]
    a_ref[POOL_PAD + seq:rows, :] = zeros_pad
    src, dst = a_ref, b_ref
    shift = 1
    while shift < window:
        dst[POOL_PAD:rows, :] = src[POOL_PAD - shift:rows - shift, :] + src[POOL_PAD:rows, :]
        src, dst = dst, src
        shift *= 2
    off = POOL_PAD + window // 2 - 1
    win = src[off:off + seq, :]
    t = lax.broadcasted_iota(I32, (seq, 1), 0)
    lo = jnp.maximum(t - window // 2, 0)
    hi = jnp.minimum(t + window // 2 - 1, seq - 1)
    cnt = (hi - lo + 1).astype(F32)
    pooled = (win / cnt - u_ref[0]).astype(BF16)
    o_ref[0] = (_dot(pooled, gw_ref[0]) * sc_ref[0]).astype(BF16)


def _pool_kernel(u_ref, gw_ref, sc_ref, o_ref, a_ref, b_ref, *, seq):
    g = pl.program_id(1)
    for k, window in enumerate(POOL_WINDOWS):
        @pl.when(g == k)
        def _():
            _pool_group(u_ref, gw_ref, sc_ref, o_ref, a_ref, b_ref, window=window, seq=seq)


def _pool_mixer(u3d, gw_bf16, scale3d):
    b, seq, tok_width = u3d.shape
    n_groups, gwid, _ = gw_bf16.shape
    assert n_groups == len(POOL_WINDOWS) and n_groups * gwid == tok_width
    return pl.pallas_call(
        functools.partial(_pool_kernel, seq=seq),
        out_shape=jax.ShapeDtypeStruct((b, seq, tok_width), BF16),
        grid=(b, n_groups),
        in_specs=[
            pl.BlockSpec((1, seq, gwid), lambda i, g: (i, 0, g)),
            pl.BlockSpec((1, gwid, gwid), lambda i, g: (g, 0, 0)),
            pl.BlockSpec((1, 1, gwid), lambda i, g: (g, 0, 0)),
        ],
        out_specs=pl.BlockSpec((1, seq, gwid), lambda i, g: (i, 0, g)),
        scratch_shapes=[pltpu.VMEM((seq + 2 * POOL_PAD, gwid), F32),
                        pltpu.VMEM((seq + 2 * POOL_PAD, gwid), F32)],
        compiler_params=_params(2),
        name="pool_mixer",
    )(u3d, gw_bf16, scale3d)


def _xattn_kernel(q_ref, kv_ref, o_ref):
    scale = XA_HEAD_DIM ** -0.5
    for h in range(XA_HEADS):
        lo = h * XA_HEAD_DIM
        q = q_ref[:, lo:lo + XA_HEAD_DIM]
        k = kv_ref[0, 0, :, lo:lo + XA_HEAD_DIM]
        v = kv_ref[0, 0, :, XA_WIDTH + lo:XA_WIDTH + lo + XA_HEAD_DIM]
        s = _dot_nt(q, k) * scale
        m = jnp.max(s, axis=-1, keepdims=True)
        p = jnp.exp(s - m)
        den = jnp.sum(p, axis=-1, keepdims=True)
        o_ref[:, lo:lo + XA_HEAD_DIM] = (_dot(p.astype(BF16), v) / den).astype(BF16)


XATTN_ROWS = 1024


def _mem_xattn(qm2d, memkv, layer, seq):
    t = qm2d.shape[0]
    tiles_per_seq = seq // XATTN_ROWS
    return pl.pallas_call(
        _xattn_kernel,
        out_shape=jax.ShapeDtypeStruct((t, XA_WIDTH), BF16),
        grid=(t // XATTN_ROWS,),
        in_specs=[
            pl.BlockSpec((XATTN_ROWS, XA_WIDTH), lambda i: (i, 0)),
            pl.BlockSpec((1, 1, MEM_LEN, 2 * XA_WIDTH), lambda i: (layer, i // tiles_per_seq, 0, 0)),
        ],
        out_specs=pl.BlockSpec((XATTN_ROWS, XA_WIDTH), lambda i: (i, 0)),
        compiler_params=_params(1),
        name="mem_xattn",
    )(qm2d, memkv)


def _outproj_kernel(tok_ref, mo_ref, x_ref, w_ref, g_ref, rw_ref,
                    x1_ref, h_ref, afft_ref, aff_ref, wcat_ref, x1prev_ref,
                    *, tok_width, n_experts):
    @pl.when(pl.program_id(0) == 0)
    def _():
        rw = rw_ref[0]
        w_hi = rw.astype(BF16)
        wcat_ref[:, 0:LANES] = w_hi
        wcat_ref[:, LANES:2 * LANES] = (rw - w_hi.astype(F32)).astype(BF16)
        x1prev_ref[...] = jnp.zeros_like(x1prev_ref)

    hn = _rmsnorm_rows(x1prev_ref[...], g_ref[...])
    h_prev = hn.astype(BF16)
    h_lo = (hn - h_prev.astype(F32)).astype(BF16)
    r = _dot(h_prev, wcat_ref[...]) + _dot(h_lo, wcat_ref[...])
    logits = r[:, 0:LANES] + r[:, LANES:2 * LANES]
    lt = logits.T[0:n_experts, :]
    m = jnp.max(lt, axis=0, keepdims=True)
    ex = jnp.exp(lt - m)
    afft = ex / jnp.sum(ex, axis=0, keepdims=True)
    afft_ref[0] = afft
    padded = jnp.concatenate(
        [afft, jnp.zeros((LANES - n_experts, afft.shape[1]), F32)], axis=0)
    aff = padded.T
    aff_ref[...] = aff
    h_ref[...] = _pack_row_words(h_prev, _pack_gate_lanes(aff))

    y = _dot(tok_ref[...], w_ref[0:tok_width, :]) + _dot(mo_ref[...], w_ref[tok_width:, :])
    x1 = x_ref[...] + y
    x1_ref[...] = x1
    x1prev_ref[...] = x1


def _outproj(tok2d, mo2d, x2d, w_bf16, g, rw_pad, layer, n_experts, seq):
    t, d_model = x2d.shape
    tok_width = tok2d.shape[1]
    tiles_per_seq = seq // ROW_TILE
    n_tiles = t // ROW_TILE

    def cur(i):
        return jnp.minimum(i, n_tiles - 1)

    def prev(i):
        return jnp.maximum(i - 1, 0)

    return pl.pallas_call(
        functools.partial(_outproj_kernel, tok_width=tok_width, n_experts=n_experts),
        out_shape=(jax.ShapeDtypeStruct((t, d_model), F32),
                   jax.ShapeDtypeStruct((t, d_model // 2 + LANES), U32),
                   jax.ShapeDtypeStruct((t // seq, n_experts, seq), F32),
                   jax.ShapeDtypeStruct((t, LANES), F32)),
        grid=(n_tiles + 1,),
        in_specs=[
            pl.BlockSpec((ROW_TILE, tok_width), lambda i: (cur(i), 0)),
            pl.BlockSpec((ROW_TILE, mo2d.shape[1]), lambda i: (cur(i), 0)),
            pl.BlockSpec((ROW_TILE, d_model), lambda i: (cur(i), 0)),
            _resident((d_model, d_model), lambda i: (0, 0)),
            pl.BlockSpec((1, d_model), lambda i: (0, 0)),
            _resident((1, d_model, LANES), lambda i: (layer, 0, 0)),
        ],
        out_specs=(pl.BlockSpec((ROW_TILE, d_model), lambda i: (cur(i), 0)),
                   pl.BlockSpec((ROW_TILE, d_model // 2 + LANES), lambda i: (prev(i), 0)),
                   pl.BlockSpec((1, n_experts, ROW_TILE),
                                lambda i: (prev(i) // tiles_per_seq, 0, prev(i) % tiles_per_seq)),
                   pl.BlockSpec((ROW_TILE, LANES), lambda i: (prev(i), 0))),
        scratch_shapes=[pltpu.VMEM((d_model, 2 * LANES), BF16),
                        pltpu.VMEM((ROW_TILE, d_model), F32)],
        compiler_params=_params(1),
        name="outproj_router",
    )(tok2d, mo2d, x2d, w_bf16, g, rw_pad)


def _strict_triangle(n, lower):
    r = lax.broadcasted_iota(I32, (n, n), 0)
    c = lax.broadcasted_iota(I32, (n, n), 1)
    return jnp.where((c < r) if lower else (r < c), 1.0, 0.0).astype(BF16)


def _prefix_rows(mask_f32):
    s, l = mask_f32.shape
    tri = _strict_triangle(PREFIX_CHUNK, lower=True)
    carry = jnp.zeros((1, l), F32)
    out = []
    for c in range(0, s, PREFIX_CHUNK):
        m = mask_f32[c:c + PREFIX_CHUNK, :]
        out.append(_dot(tri, m.astype(BF16)) + carry)
        carry = carry + jnp.sum(m, axis=0, keepdims=True)
    return jnp.concatenate(out, axis=0)


def _prefix_lanes(mask_f32):
    e, s = mask_f32.shape
    tri = _strict_triangle(PREFIX_CHUNK, lower=False)
    carry = jnp.zeros((e, 1), F32)
    out = []
    for c in range(0, s, PREFIX_CHUNK):
        m = mask_f32[:, c:c + PREFIX_CHUNK]
        out.append(_dot(m.astype(BF16), tri) + carry)
        carry = carry + jnp.sum(m, axis=1, keepdims=True)
    return jnp.concatenate(out, axis=1)


def _select_slots(key, thr, need, prefix_fn):
    return _select_slots_and_counts(key, thr, need, prefix_fn)[0]


def _select_slots_and_counts(key, thr, need, prefix_fn):
    gt = jnp.where(key > thr, 1.0, 0.0)
    eq = jnp.where(key == thr, 1.0, 0.0)
    eq_rank = prefix_fn(eq)
    sel = gt + eq * jnp.where(eq_rank < need, 1.0, 0.0)
    pos = prefix_fn(sel)
    return jnp.where(sel > 0.5, pos, -1.0), pos


GATE_GROUP = 16
GATE_PIECES = 3


def _pack_gate_lanes(aff):
    hi = aff.astype(BF16).astype(F32)
    r1 = aff - hi
    mid = r1.astype(BF16).astype(F32)
    lo = (r1 - mid).astype(BF16).astype(F32)
    packed = hi + pltpu.roll(mid, GATE_GROUP, 1) + pltpu.roll(lo, 2 * GATE_GROUP, 1)
    return packed.astype(BF16)


def _unpack_gate(tail, e):
    lane = lax.broadcasted_iota(I32, (1, LANES), 1)
    mine = ((lane & (GATE_GROUP - 1)) == e) & (lane < GATE_PIECES * GATE_GROUP)
    return jnp.sum(jnp.where(mine, tail.astype(F32), 0.0), axis=1, keepdims=True)


def _pack_row_words(h, gate_tile):
    rows, d_model = h.shape
    half = d_model // 2
    hi = jnp.concatenate([h[:, 0:half], gate_tile], axis=1).astype(F32)
    lo = jnp.concatenate([h[:, half:], jnp.zeros((rows, LANES), BF16)], axis=1).astype(F32)
    return pltpu.bitcast(hi, U32) | (pltpu.bitcast(lo, U32) >> 16)


def _unpack_row_words(words):
    hi = pltpu.bitcast(words & jnp.uint32(0xFFFF0000), F32)
    lo = pltpu.bitcast(words << 16, F32)
    return hi, lo


def _route_kernel(afft_ref, posm_ref, thr_ref, need_ref, starts_ref, *, cap):
    n_experts, seq = afft_ref.shape[1], afft_ref.shape[2]
    key = pltpu.bitcast(afft_ref[0], I32)
    thr = jnp.zeros((n_experts, 1), I32)
    for bit in range(F32_KEY_BITS - 1, -1, -1):
        cand = thr | (1 << bit)
        cnt = jnp.sum(jnp.where(key >= cand, 1.0, 0.0), axis=1, keepdims=True)
        thr = jnp.where(cnt >= cap, cand, thr)
    n_gt = jnp.sum(jnp.where(key > thr, 1.0, 0.0), axis=1, keepdims=True)
    need = cap - n_gt
    thr_ref[0] = thr
    need_ref[0] = need
    posm, before = _select_slots_and_counts(key, thr, need, _prefix_lanes)
    posm_ref[0] = posm.astype(I32)
    starts_ref[0] = jnp.concatenate(
        [before[:, r:r + 1] for r in range(0, seq, COMBINE_ROWS)], axis=1).astype(I32)


def _route(afft, cap):
    b, n_experts, seq = afft.shape
    n_tiles = seq // COMBINE_ROWS
    return pl.pallas_call(
        functools.partial(_route_kernel, cap=cap),
        out_shape=(jax.ShapeDtypeStruct((b, n_experts, seq), I32),
                   jax.ShapeDtypeStruct((b, n_experts, 1), I32),
                   jax.ShapeDtypeStruct((b, n_experts, 1), F32),
                   jax.ShapeDtypeStruct((b, n_experts, n_tiles), I32)),
        grid=(b,),
        in_specs=[pl.BlockSpec((1, n_experts, seq), lambda i: (i, 0, 0))],
        out_specs=(pl.BlockSpec((1, n_experts, seq), lambda i: (i, 0, 0)),
                   pl.BlockSpec((1, n_experts, 1), lambda i: (i, 0, 0)),
                   pl.BlockSpec((1, n_experts, 1), lambda i: (i, 0, 0)),
                   pl.BlockSpec((1, n_experts, n_tiles), lambda i: (i, 0, 0))),
        compiler_params=_params(1),
        name="expert_route",
    )(afft)


SC_LANES = 16
SC_GATHER_ROWS = 64


def _sc_expert_gather(posm2d, h_words, n_experts, e_offset, n_e, b, seq, cap):
    width = h_words.shape[1]
    info = plsc.get_sparse_core_info()
    n_cores, n_subcores = info.num_cores, info.num_subcores
    n_workers = n_cores * n_subcores
    assert info.num_lanes == SC_LANES and (b * n_e) % n_workers == 0
    pairs_per_worker = (b * n_e) // n_workers
    mesh = plsc.VectorSubcoreMesh(core_axis_name="c", subcore_axis_name="s")

    @functools.partial(
        pl.kernel, mesh=mesh,
        out_type=jax.ShapeDtypeStruct((n_e * b * cap, width), U32),
        compiler_params=pltpu.CompilerParams(needs_layout_passes=False),
        scratch_types=[
            pltpu.VMEM((seq,), I32),
            pltpu.VMEM((cap,), I32),
            pltpu.VMEM((SC_GATHER_ROWS, width), U32),
            pltpu.SemaphoreType.DMA,
        ],
        name="sc_expert_gather",
    )
    def gather(posm_hbm, h_hbm, out_hbm, pos_v, idx_v, rows_v, sem):
        wid = lax.axis_index("s") * n_cores + lax.axis_index("c")
        for p in range(pairs_per_worker):
            pair = wid * pairs_per_worker + p
            bi = pair // n_e
            e = pair - bi * n_e
            pltpu.sync_copy(posm_hbm.at[bi * n_experts + e_offset + e], pos_v)

            @pl.loop(0, seq, step=SC_LANES)
            def _(t0):
                slots = pos_v[pl.ds(t0, SC_LANES)]
                rows = lax.iota(I32, SC_LANES) + (t0 + bi * seq)
                plsc.store_scatter(idx_v, [slots], rows, mask=slots >= 0)

            out_base = (e * b + bi) * cap
            for c in range(cap // SC_GATHER_ROWS):
                chunk = idx_v.at[pl.ds(c * SC_GATHER_ROWS, SC_GATHER_ROWS)]
                pltpu.async_copy(h_hbm.at[chunk], rows_v, sem).wait()
                pltpu.sync_copy(rows_v, out_hbm.at[pl.ds(out_base + c * SC_GATHER_ROWS, SC_GATHER_ROWS)])

    return gather(posm2d, h_words).reshape(n_e, b, cap, width)


EXPERT_ROWS = 1024
EXPERT_FTILE = 256


EXPERT_WBUFS = 2


def _expert_kernel(xg_ref, wg_hbm, wu_hbm, wd_hbm, o_ref,
                   x_ref, g_ref, hact_ref, wg_full, wu_full, wd_full, wg_buf, wu_buf, wd_buf, sem,
                   *, layer, e_offset, n_ftiles):
    e = pl.program_id(0)
    m = pl.program_id(1)
    n_e = pl.num_programs(0)
    nb, cap, width = xg_ref.shape[1], xg_ref.shape[2], xg_ref.shape[3]
    half = width - LANES
    d_model = 2 * half
    rows = nb * cap
    tf = EXPERT_FTILE
    assert n_ftiles % EXPERT_WBUFS == 0

    def tile_copies(expert, f):
        ge = e_offset + expert
        slot = f % EXPERT_WBUFS
        return (
            pltpu.make_async_copy(wg_hbm.at[layer, ge, :, pl.ds(f * tf, tf)], wg_buf.at[slot], sem.at[0, slot]),
            pltpu.make_async_copy(wu_hbm.at[layer, ge, :, pl.ds(f * tf, tf)], wu_buf.at[slot], sem.at[1, slot]),
            pltpu.make_async_copy(wd_hbm.at[layer, ge, pl.ds(f * tf, tf), :], wd_buf.at[slot], sem.at[2, slot]),
        )

    def start(expert, f):
        for cp in tile_copies(expert, f):
            cp.start()

    @pl.when((e == 0) & (m == 0))
    def _():
        for f in range(EXPERT_WBUFS):
            start(e, f)

    for i in range(nb):
        r0 = i * cap
        hi, lo = _unpack_row_words(xg_ref[0, i])
        x_ref[r0:r0 + cap, 0:half] = hi[:, 0:half].astype(BF16)
        x_ref[r0:r0 + cap, half:d_model] = lo[:, 0:half].astype(BF16)
        g_ref[r0:r0 + cap, :] = jnp.broadcast_to(
            _unpack_gate(hi[:, half:width], e_offset + e), (cap, LANES))

    def receive(f):
        slot = f % EXPERT_WBUFS
        for cp in tile_copies(e, f):
            cp.wait()
        wg_full[:, f * tf:(f + 1) * tf] = wg_buf[slot].astype(BF16)
        wu_full[:, f * tf:(f + 1) * tf] = wu_buf[slot].astype(BF16)
        wd_full[f * tf:(f + 1) * tf, :] = wd_buf[slot].astype(BF16)
        ahead = f + EXPERT_WBUFS
        if ahead < n_ftiles:
            start(e, ahead)
        else:
            @pl.when(e + 1 < n_e)
            def _():
                start(e + 1, ahead - n_ftiles)

    def body(first_group):
        x = x_ref[...]
        for f in range(n_ftiles):
            if first_group:
                receive(f)
            a = _dot(x, wg_full[:, f * tf:(f + 1) * tf])
            u = _dot(x, wu_full[:, f * tf:(f + 1) * tf])
            hact_ref[:, f * tf:(f + 1) * tf] = (a * jax.nn.sigmoid(a) * u).astype(BF16)
        for c in range(0, d_model, COL_CHUNK):
            y = _dot(hact_ref[...], wd_full[:, c:c + COL_CHUNK])
            for j in range(0, COL_CHUNK, LANES):
                o_ref[:, 0, :, c + j:c + j + LANES] = (
                    (y[:, j:j + LANES] * g_ref[...]).astype(BF16).reshape(nb, cap, LANES))

    @pl.when(m == 0)
    def _():
        body(True)

    @pl.when(m != 0)
    def _():
        body(False)


def _experts(xg, w_gate, w_up, w_down, layer, e_offset):
    n_e, b, cap, width = xg.shape
    d_model = 2 * (width - LANES)
    d_expert = w_gate.shape[3]
    nb = EXPERT_ROWS // cap
    n_ftiles = d_expert // EXPERT_FTILE
    return pl.pallas_call(
        functools.partial(_expert_kernel, layer=layer, e_offset=e_offset, n_ftiles=n_ftiles),
        out_shape=jax.ShapeDtypeStruct((b, n_e, cap, d_model), BF16),
        grid=(n_e, b // nb),
        in_specs=[
            pl.BlockSpec((1, nb, cap, width), lambda e, m: (e, m, 0, 0)),
            pl.BlockSpec(memory_space=pl.ANY),
            pl.BlockSpec(memory_space=pl.ANY),
            pl.BlockSpec(memory_space=pl.ANY),
        ],
        out_specs=pl.BlockSpec((nb, 1, cap, d_model), lambda e, m: (m, e, 0, 0)),
        scratch_shapes=[pltpu.VMEM((EXPERT_ROWS, d_model), BF16),
                        pltpu.VMEM((EXPERT_ROWS, LANES), F32),
                        pltpu.VMEM((EXPERT_ROWS, d_expert), BF16),
                        pltpu.VMEM((d_model, d_expert), BF16),
                        pltpu.VMEM((d_model, d_expert), BF16),
                        pltpu.VMEM((d_expert, d_model), BF16),
                        pltpu.VMEM((EXPERT_WBUFS, d_model, EXPERT_FTILE), F32),
                        pltpu.VMEM((EXPERT_WBUFS, d_model, EXPERT_FTILE), F32),
                        pltpu.VMEM((EXPERT_WBUFS, EXPERT_FTILE, d_model), F32),
                        pltpu.SemaphoreType.DMA((3, EXPERT_WBUFS))],
        compiler_params=_params(2),
        name="experts",
    )(xg, w_gate, w_up, w_down)


COMBINE_ROWS = 256
COMBINE_WINDOW = 64
MXU_DEPTH = 256
BF16_ROWS = 16


def _combine_kernel(starts_ref, aff_ref, thr_ref, need_ref, x1_ref, *rest, cap, n_experts, final_norm):
    n_y = len(rest) - (6 if final_norm else 5)
    y_hbm = rest[:n_y]
    g_ref = rest[n_y] if final_norm else None
    o_ref, post_ref, pfull_ref, ybuf, sem = rest[-5:]
    b = pl.program_id(0)
    t = pl.program_id(1)
    n_b = pl.num_programs(0)
    n_t = pl.num_programs(1)
    rows, win = COMBINE_ROWS, COMBINE_WINDOW
    experts_per_group = n_experts // n_y
    group_rows = ybuf.shape[0] // (2 * n_y)

    def ybuf_row(buf_slot, g):
        return pl.multiple_of((buf_slot * n_y + g) * group_rows, BF16_ROWS)

    n_chunks = sem.shape[1]
    chunks_per_group = n_chunks // n_y
    chunk_rows = group_rows // chunks_per_group
    slot = b % 2

    def chunk_copy(seq_idx, c, dst_slot):
        g, r = c // chunks_per_group, (c % chunks_per_group) * chunk_rows
        return pltpu.make_async_copy(y_hbm[g].at[seq_idx, pl.ds(r, chunk_rows), :],
                                     ybuf.at[pl.ds(ybuf_row(dst_slot, g) + r, chunk_rows), :],
                                     sem.at[dst_slot, c])

    @pl.when((b == 0) & (t == 0))
    def _():
        for c in range(n_chunks):
            chunk_copy(0, c, 0).start()

    @pl.when(t == 0)
    def _():
        for c in range(n_chunks):
            chunk_copy(b, c, slot).wait()
        key = pltpu.bitcast(aff_ref[...], I32)
        post_ref[...] = _select_slots(key, thr_ref[0], need_ref[0], _prefix_rows)

    for c in range(n_chunks):
        @pl.when((t == c) & (b + 1 < n_b))
        def _():
            chunk_copy(b + 1, c, 1 - slot).start()

    posm = post_ref[pl.ds(pl.multiple_of(t * rows, rows), rows), :]
    base = (b * (n_t + 1) + t) * n_experts
    wstart, ok = [], None
    for e in range(n_experts):
        first = starts_ref[base + e]
        end = starts_ref[base + n_experts + e]
        w0 = jnp.minimum((first // BF16_ROWS) * BF16_ROWS, cap - win)
        fits = end - w0 <= win
        wstart.append(w0)
        ok = fits if ok is None else jnp.logical_and(ok, fits)

    def finish(acc):
        if final_norm:
            acc = _rmsnorm_rows(acc, g_ref[...])
        o_ref[0] = acc

    @pl.when(ok)
    def _():
        per_dot, per_tile = MXU_DEPTH // win, LANES // win
        lane = lax.broadcasted_iota(I32, (1, LANES), 1)
        lane_f = lane.astype(F32)
        acc = x1_ref[0]
        for e0 in range(0, n_experts, per_dot):
            onehots, windows = [], []
            for e1 in range(e0, e0 + per_dot, per_tile):
                rel = None
                for k in range(per_tile - 1, -1, -1):
                    e = e1 + k
                    shifted = posm[:, e:e + 1] - (wstart[e] - k * win).astype(F32)
                    rel = shifted if rel is None else jnp.where(lane < (k + 1) * win, shifted, rel)
                onehots.append(jnp.where(rel == lane_f, 1.0, 0.0).astype(BF16))
            for e in range(e0, e0 + per_dot):
                g, el = divmod(e, experts_per_group)
                r0 = pl.multiple_of(ybuf_row(slot, g) + el * cap + wstart[e], BF16_ROWS)
                windows.append(ybuf[pl.ds(r0, win), :])
            acc = acc + _dot(jnp.concatenate(onehots, axis=1), jnp.concatenate(windows, axis=0))
        finish(acc)

    @pl.when(jnp.logical_not(ok))
    def _():
        slot_ids = lax.broadcasted_iota(I32, (1, cap), 1).astype(F32)
        for e in range(n_experts):
            pfull_ref[:, e * cap:(e + 1) * cap] = jnp.where(
                posm[:, e:e + 1] == slot_ids, 1.0, 0.0).astype(BF16)
        acc = x1_ref[0]
        for g in range(n_y):
            acc = acc + _dot(pfull_ref[:, g * group_rows:(g + 1) * group_rows],
                             ybuf[pl.ds(ybuf_row(slot, g), group_rows), :])
        finish(acc)


def _combine(starts, aff2d, thr_row, need_row, x1_3d, y_groups, cap, n_experts, final_g=None):
    b, seq, d_model = x1_3d.shape
    assert sum(y.shape[1] for y in y_groups) == n_experts * cap and cap >= COMBINE_WINDOW
    final_norm = final_g is not None
    n_tiles = seq // COMBINE_ROWS
    group_rows = y_groups[0].shape[1]
    assert all(y.shape[1] == group_rows for y in y_groups)
    assert n_tiles % len(y_groups) == 0 and group_rows % (n_tiles // len(y_groups)) == 0
    y_specs = [pl.BlockSpec(memory_space=pl.ANY) for _ in y_groups]
    g_specs = [pl.BlockSpec((1, d_model), lambda i, t, s: (0, 0))] if final_norm else []
    g_args = [final_g] if final_norm else []
    return pl.pallas_call(
        functools.partial(_combine_kernel, cap=cap, n_experts=n_experts, final_norm=final_norm),
        out_shape=jax.ShapeDtypeStruct((b, seq, d_model), F32),
        grid_spec=pltpu.PrefetchScalarGridSpec(
            num_scalar_prefetch=1,
            grid=(b, n_tiles),
            in_specs=[
                pl.BlockSpec((seq, LANES), lambda i, t, s: (i, 0)),
                pl.BlockSpec((1, 1, LANES), lambda i, t, s: (i, 0, 0)),
                pl.BlockSpec((1, 1, LANES), lambda i, t, s: (i, 0, 0)),
                pl.BlockSpec((1, COMBINE_ROWS, d_model), lambda i, t, s: (i, t, 0)),
            ] + y_specs + g_specs,
            out_specs=pl.BlockSpec((1, COMBINE_ROWS, d_model), lambda i, t, s: (i, t, 0)),
            scratch_shapes=[pltpu.VMEM((seq, LANES), F32),
                            pltpu.VMEM((COMBINE_ROWS, n_experts * cap), BF16),
                            pltpu.VMEM((2 * len(y_groups) * group_rows, d_model), BF16),
                            pltpu.SemaphoreType.DMA((2, n_tiles))],
        ),
        compiler_params=_params(2),
        name="combine",
    )(starts, aff2d, thr_row, need_row, x1_3d, *y_groups, *g_args)


EXPERT_GROUPS = 2


def _moe(afft, aff2d, h_words, x1_2d, w_gate, w_up, w_down, layer, b, seq, final_g=None):
    n_experts = afft.shape[1]
    assert n_experts <= GATE_GROUP and n_experts % EXPERT_GROUPS == 0
    d_model = x1_2d.shape[1]
    cap = CAPACITY_FACTOR * seq // n_experts
    n_e = n_experts // EXPERT_GROUPS
    posm, thr, need, tile_starts = _route(afft, cap)
    starts = jnp.concatenate([jnp.swapaxes(tile_starts, 1, 2),
                              jnp.full((b, 1, n_experts), cap, I32)], axis=1).reshape(-1)
    posm2d = posm.reshape(b * n_experts, seq)
    xgs = [_sc_expert_gather(posm2d, h_words, n_experts, g * n_e, n_e, b, seq, cap)
           for g in range(EXPERT_GROUPS)]
    ys = [_experts(xg, w_gate, w_up, w_down, layer, g * n_e).reshape(b, n_e * cap, d_model)
          for g, xg in enumerate(xgs)]
    pad = LANES - n_experts
    thr_row = jnp.pad(thr.reshape(b, 1, n_experts), ((0, 0), (0, 0), (0, pad)),
                      constant_values=np.iinfo(np.int32).max)
    need_row = jnp.pad(need.reshape(b, 1, n_experts), ((0, 0), (0, 0), (0, pad)))
    return _combine(starts, aff2d, thr_row, need_row, x1_2d.reshape(b, seq, d_model), ys, cap,
                    n_experts, final_g)


def _rotary_tile(t, cos, sin_lo, sin_hi):
    half = ROT_DIM // 2
    return t * cos + pltpu.roll(t, LANES - half, 1) * sin_lo + pltpu.roll(t, half, 1) * sin_hi


def _inproj_attn_kernel(x_ref, g_ref, w_ref, pos_ref, rot_ref,
                        q_ref, k_ref, v_ref, qm_ref, *, tok_width, kv_width):
    hn = _rmsnorm_rows(x_ref[...], g_ref[...]).astype(BF16)
    ang = pos_ref[...].astype(F32) * rot_ref[0:1, :]
    cos = jnp.cos(ang)
    sin = jnp.sin(ang)
    sin_lo = sin * rot_ref[1:2, :]
    sin_hi = sin * rot_ref[2:3, :]
    qscale = HEAD_DIM ** -0.5
    for c in range(0, tok_width, COL_CHUNK):
        pc = _dot(hn, w_ref[:, c:c + COL_CHUNK])
        for j in range(0, COL_CHUNK, LANES):
            rot = _rotary_tile(pc[:, j:j + LANES], cos, sin_lo, sin_hi)
            q_ref[:, c + j:c + j + LANES] = (rot * qscale).astype(BF16)
    kv = _dot(hn, w_ref[:, tok_width:tok_width + 2 * kv_width])
    k01 = _rotary_tile(kv[:, 0:LANES], cos, sin_lo, sin_hi)
    k2x = _rotary_tile(kv[:, LANES:2 * LANES], cos, sin_lo, sin_hi)
    k_ref[0, 0] = k01[:, 0:HEAD_DIM].astype(BF16)
    k_ref[0, 1] = k01[:, HEAD_DIM:LANES].astype(BF16)
    k_ref[0, 2] = k2x[:, 0:HEAD_DIM].astype(BF16)
    for hh in range(kv_width // HEAD_DIM):
        lo = kv_width + hh * HEAD_DIM
        v_ref[0, hh] = kv[:, lo:lo + HEAD_DIM].astype(BF16)
    qm_ref[...] = _dot(hn, w_ref[:, tok_width + 2 * kv_width:]).astype(BF16)


def _inproj_attn(x2d, g, w_bf16, pos2d, rot_rows, tok_width, kv_width, seq):
    t, d_model = x2d.shape
    n = w_bf16.shape[1]
    n_kv = kv_width // HEAD_DIM
    assert n_kv == 3 and kv_width + HEAD_DIM == 2 * LANES
    tiles_per_seq = seq // ROW_TILE
    kv_spec = pl.BlockSpec((1, n_kv, ROW_TILE, HEAD_DIM),
                           lambda i: (i // tiles_per_seq, 0, i % tiles_per_seq, 0))
    return pl.pallas_call(
        functools.partial(_inproj_attn_kernel, tok_width=tok_width, kv_width=kv_width),
        out_shape=(jax.ShapeDtypeStruct((t, tok_width), BF16),
                   jax.ShapeDtypeStruct((t // seq, n_kv, seq, HEAD_DIM), BF16),
                   jax.ShapeDtypeStruct((t // seq, n_kv, seq, HEAD_DIM), BF16),
                   jax.ShapeDtypeStruct((t, n - tok_width - 2 * kv_width), BF16)),
        grid=(t // ROW_TILE,),
        in_specs=[
            pl.BlockSpec((ROW_TILE, d_model), lambda i: (i, 0)),
            pl.BlockSpec((1, d_model), lambda i: (0, 0)),
            _resident((d_model, n), lambda i: (0, 0)),
            pl.BlockSpec((ROW_TILE, 1), lambda i: (i, 0)),
            pl.BlockSpec((8, LANES), lambda i: (0, 0)),
        ],
        out_specs=(pl.BlockSpec((ROW_TILE, tok_width), lambda i: (i, 0)),
                   kv_spec, kv_spec,
                   pl.BlockSpec((ROW_TILE, n - tok_width - 2 * kv_width), lambda i: (i, 0))),
        compiler_params=_params(1),
        name="inproj_attn",
    )(x2d, g, w_bf16, pos2d, rot_rows)


WATTN_QBLOCKS = 2


def _wattn_kernel(sink_ref, q_ref, kp_ref, kc_ref, kn_ref, vp_ref, vc_ref, vn_ref, o_ref,
                  valid_ref, kpad_ref, vpad_ref, s_ref, p_ref, inv_ref, *, seq):
    step = pl.program_id(1)
    n_kv = kc_ref.shape[1]
    pairs = GQA_RATIO // 2
    half_rows = pairs * BLOCK
    key_rows = (WATTN_QBLOCKS + 2) * BLOCK
    zeros = jnp.zeros((key_rows, HEAD_DIM), BF16)
    ones_col = jnp.where(lax.broadcasted_iota(I32, (key_rows, HEAD_DIM), 1) == 0, 1.0, 0.0).astype(BF16)
    low_half = lax.broadcasted_iota(I32, (1, LANES), 1) < HEAD_DIM
    for hk in range(n_kv):
        kw = jnp.concatenate([kp_ref[0, hk], kc_ref[0, hk], kn_ref[0, hk]], axis=0)
        vw = jnp.concatenate([vp_ref[0, hk], vc_ref[0, hk], vn_ref[0, hk]], axis=0)
        kpad_ref[2 * hk] = jnp.concatenate([kw, zeros], axis=1)
        kpad_ref[2 * hk + 1] = jnp.concatenate([zeros, kw], axis=1)
        vpad_ref[2 * hk] = jnp.concatenate([vw, ones_col], axis=1)
        vpad_ref[2 * hk + 1] = jnp.concatenate([ones_col, vw], axis=1)

    qi = lax.broadcasted_iota(I32, (BLOCK, 3 * BLOCK), 0)
    kj = lax.broadcasted_iota(I32, (BLOCK, 3 * BLOCK), 1)
    for qb in range(WATTN_QBLOCKS):
        n = step * WATTN_QBLOCKS + qb
        k0 = qb * BLOCK
        first = jnp.maximum(qi, BLOCK - n * BLOCK)
        last = jnp.minimum(qi + 2 * WINDOW, seq + BLOCK - 1 - n * BLOCK)
        valid_ref[qb] = jnp.where(((kj - first) | (last - kj)) >= 0, 1.0, 0.0)
        q0 = qb * BLOCK
        for hk in range(n_kv):
            tile0 = hk * pairs
            qs = jnp.concatenate(
                [q_ref[0, q0:q0 + BLOCK, (tile0 + j) * LANES:(tile0 + j + 1) * LANES]
                 for j in range(pairs)], axis=0)
            s_ref[qb, hk, 0:half_rows, :] = _dot_nt(qs, kpad_ref[2 * hk, k0:k0 + 3 * BLOCK, :])
            s_ref[qb, hk, half_rows:2 * half_rows, :] = _dot_nt(
                qs, kpad_ref[2 * hk + 1, k0:k0 + 3 * BLOCK, :])
        for hk in range(n_kv):
            for c in range(GQA_RATIO):
                j, odd = c % pairs, c // pairs
                r = c * BLOCK
                s = jnp.concatenate([
                    jnp.where(valid_ref[qb, :, 0:BLOCK] > 0.5, s_ref[qb, hk, r:r + BLOCK, 0:BLOCK], NEG_INF),
                    s_ref[qb, hk, r:r + BLOCK, BLOCK:2 * BLOCK],
                    jnp.where(valid_ref[qb, :, 2 * BLOCK:] > 0.5,
                              s_ref[qb, hk, r:r + BLOCK, 2 * BLOCK:], NEG_INF),
                ], axis=1)
                sk = sink_ref[hk * GQA_RATIO + 2 * j + odd]
                m = jnp.maximum(jnp.max(s, axis=-1, keepdims=True), sk)
                p_ref[qb, hk, r:r + BLOCK, :] = jnp.exp(s - m).astype(BF16)
                inv_ref[qb, hk, j * BLOCK:(j + 1) * BLOCK, odd * HEAD_DIM:(odd + 1) * HEAD_DIM] = (
                    jnp.broadcast_to(jnp.exp(sk - m), (BLOCK, HEAD_DIM)))
        for hk in range(n_kv):
            pv_even = _dot(p_ref[qb, hk, 0:half_rows, :], vpad_ref[2 * hk, k0:k0 + 3 * BLOCK, :])
            pv_odd = _dot(p_ref[qb, hk, half_rows:2 * half_rows, :],
                          vpad_ref[2 * hk + 1, k0:k0 + 3 * BLOCK, :])
            den = (jnp.where(low_half, pv_even[:, HEAD_DIM:HEAD_DIM + 1], pv_odd[:, 0:1])
                   + inv_ref[qb, hk])
            o = jnp.where(low_half, pv_even, pv_odd) / den
            for j in range(pairs):
                lo = (hk * pairs + j) * LANES
                o_ref[0, q0:q0 + BLOCK, lo:lo + LANES] = o[j * BLOCK:(j + 1) * BLOCK].astype(BF16)


def _window_attention(sink, q3d, k4d, v4d):
    b, seq, tok_width = q3d.shape
    n_kv = k4d.shape[1]
    nb = seq // BLOCK
    qb = WATTN_QBLOCKS
    assert nb % qb == 0
    edge_block = (1, n_kv, BLOCK, HEAD_DIM)
    prev_spec = pl.BlockSpec(edge_block, lambda i, s: (i, 0, jnp.maximum(s * qb - 1, 0), 0))
    cur_spec = pl.BlockSpec((1, n_kv, qb * BLOCK, HEAD_DIM), lambda i, s: (i, 0, s, 0))
    next_spec = pl.BlockSpec(edge_block, lambda i, s: (i, 0, jnp.minimum(s * qb + qb, nb - 1), 0))
    key_rows = (qb + 2) * BLOCK
    return pl.pallas_call(
        functools.partial(_wattn_kernel, seq=seq),
        out_shape=jax.ShapeDtypeStruct((b, seq, tok_width), BF16),
        grid=(b, nb // qb),
        in_specs=[
            pl.BlockSpec(memory_space=pltpu.SMEM),
            pl.BlockSpec((1, qb * BLOCK, tok_width), lambda i, s: (i, s, 0)),
            prev_spec, cur_spec, next_spec, prev_spec, cur_spec, next_spec,
        ],
        out_specs=pl.BlockSpec((1, qb * BLOCK, tok_width), lambda i, s: (i, s, 0)),
        scratch_shapes=[pltpu.VMEM((qb, BLOCK, 3 * BLOCK), F32),
                        pltpu.VMEM((2 * n_kv, key_rows, LANES), BF16),
                        pltpu.VMEM((2 * n_kv, key_rows, LANES), BF16),
                        pltpu.VMEM((qb, n_kv, GQA_RATIO * BLOCK, 3 * BLOCK), F32),
                        pltpu.VMEM((qb, n_kv, GQA_RATIO * BLOCK, 3 * BLOCK), BF16),
                        pltpu.VMEM((qb, n_kv, GQA_RATIO // 2 * BLOCK, LANES), F32)],
        compiler_params=_params(2),
        name="window_attention",
    )(sink, q3d, k4d, k4d, k4d, v4d, v4d, v4d)


def _rotary_rows(dtype=F32):
    half = ROT_DIM // 2
    inv_freq = ROPE_THETA ** (-jnp.arange(0, ROT_DIM, 2, dtype=jnp.float32) / ROT_DIM)
    lane = np.arange(LANES) % HEAD_DIM
    rotated = lane < ROT_DIM
    freq = jnp.where(jnp.asarray(rotated), inv_freq[jnp.asarray(lane % half)], 0.0)
    rows = jnp.zeros((8, LANES), dtype)
    rows = rows.at[0].set(freq)
    rows = rows.at[1].set(jnp.asarray(np.where(lane < half, -1.0, 0.0), dtype))
    rows = rows.at[2].set(jnp.asarray(np.where(rotated & (lane >= half), 1.0, 0.0), dtype))
    return rows


def kernel(x, mem, positions, norm_mix_g, norm_ffn_g, mem_norm_g, final_g, mem_w_kv,
           pool_w_in, pool_group_w, pool_scale, pool_w_out,
           attn_w_in, attn_sink, attn_w_out,
           router_w, exp_w_gate, exp_w_up, exp_w_down):
    b, seq, d_model = x.shape
    depth = norm_mix_g.shape[0]
    t = b * seq
    n_experts = router_w.shape[2]
    tok_width = pool_scale.shape[1]
    n_groups = pool_group_w.shape[1]
    kv_width = (attn_w_in.shape[2] - tok_width - XA_WIDTH) // 2
    assert seq % ROW_TILE == 0 and mem.shape[1] == MEM_LEN

    memkv = _memkv(mem.reshape(b * MEM_LEN, d_model), mem_norm_g.reshape(1, d_model),
                   mem_w_kv.astype(BF16))
    memkv = memkv.reshape(depth, b, MEM_LEN, 2 * XA_WIDTH)
    rw_pad = jnp.pad(router_w, ((0, 0), (0, 0), (0, LANES - n_experts)))
    pos2d = positions.reshape(t, 1)
    rot_rows = _rotary_rows()

    x2d = x.reshape(t, d_model)
    for layer in range(depth):
        j = layer // 2
        g_mix = norm_mix_g[layer].reshape(1, d_model)
        if layer % 2 == 0:
            u, qm = _inproj_pool(x2d, g_mix, pool_w_in[j].astype(BF16), tok_width)
            tok = _pool_mixer(u.reshape(b, seq, tok_width), pool_group_w[j].astype(BF16),
                              pool_scale[j].reshape(n_groups, 1, tok_width // n_groups))
            tok = tok.reshape(t, tok_width)
            w_out = pool_w_out[j]
        else:
            q, k, v, qm = _inproj_attn(x2d, g_mix, attn_w_in[j].astype(BF16), pos2d, rot_rows,
                                       tok_width, kv_width, seq)
            tok = _window_attention(attn_sink[j], q.reshape(b, seq, tok_width), k, v)
            tok = tok.reshape(t, tok_width)
            w_out = attn_w_out[j]
        mo = _mem_xattn(qm, memkv, layer, seq)
        x1, h, afft, aff = _outproj(tok, mo, x2d, w_out.astype(BF16),
                                    norm_ffn_g[layer].reshape(1, d_model), rw_pad, layer,
                                    n_experts, seq)
        last = layer == depth - 1
        x2 = _moe(afft, aff, h, x1, exp_w_gate, exp_w_up, exp_w_down, layer, b, seq,
                  final_g.reshape(1, d_model) if last else None)
        x2d = x2.reshape(t, d_model)
    return x2d.reshape(b, seq, d_model)
```

```python
import functools

import jax
import jax.numpy as jnp
import numpy as np
from jax import lax
from jax.experimental import pallas as pl
from jax.experimental.pallas import tpu as pltpu
from jax.experimental.pallas import tpu_sc as plsc

F32 = jnp.float32
BF16 = jnp.bfloat16
I32 = jnp.int32
U32 = jnp.uint32

EPS = 1e-6
MEM_LEN = 256
XA_HEADS = 4
XA_HEAD_DIM = 128
XA_WIDTH = XA_HEADS * XA_HEAD_DIM
POOL_WINDOWS = (2, 4, 8, 16)
HEAD_DIM = 64
GQA_RATIO = 8
WINDOW = 128
BLOCK = 128
ROPE_THETA = 500000.0
ROT_DIM = 16
NEG_INF = -1e30
CAPACITY_FACTOR = 2

LANES = 128
MIB = 1024 * 1024
VMEM_LIMIT_BYTES = 56 * MIB

ROW_TILE = 512
INPROJ_ROWS = 1024
INPROJ_PARTS = 4
POOL_PAD = 16
PREFIX_CHUNK = 256
COL_CHUNK = 512
F32_KEY_BITS = 31


def _params(n_grid_dims):
    return pltpu.CompilerParams(
        dimension_semantics=("arbitrary",) * n_grid_dims,
        vmem_limit_bytes=VMEM_LIMIT_BYTES,
    )


def _resident(block_shape, index_map):
    return pl.BlockSpec(block_shape, index_map, pipeline_mode=pl.Buffered(1))


def _rmsnorm_rows(x, g):
    return x * lax.rsqrt(jnp.mean(x * x, axis=-1, keepdims=True) + EPS) * g


def _dot(a, b):
    return jnp.dot(a, b, preferred_element_type=F32)


def _dot_nt(a, b):
    return lax.dot_general(a, b, (((1,), (1,)), ((), ())), preferred_element_type=F32)


def _memkv_kernel(mem_ref, g_ref, w_ref, o_ref):
    hn = _rmsnorm_rows(mem_ref[...], g_ref[...]).astype(BF16)
    o_ref[0] = _dot(hn, w_ref[0]).astype(BF16)


def _memkv(mem2d, g, w_bf16):
    depth, d_model, n = w_bf16.shape
    rows = mem2d.shape[0]
    return pl.pallas_call(
        _memkv_kernel,
        out_shape=jax.ShapeDtypeStruct((depth, rows, n), BF16),
        grid=(depth, rows // ROW_TILE),
        in_specs=[
            pl.BlockSpec((ROW_TILE, d_model), lambda l, i: (i, 0)),
            pl.BlockSpec((1, d_model), lambda l, i: (0, 0)),
            pl.BlockSpec((1, d_model, n), lambda l, i: (l, 0, 0)),
        ],
        out_specs=pl.BlockSpec((1, ROW_TILE, n), lambda l, i: (l, i, 0)),
        compiler_params=_params(2),
        name="memkv",
    )(mem2d, g, w_bf16)


def _inproj_pool_kernel(x_ref, g_ref, w_ref, u_ref, qm_ref, *, tok_width):
    hn = _rmsnorm_rows(x_ref[...], g_ref[...]).astype(BF16)
    for c in range(0, tok_width, COL_CHUNK):
        u_ref[:, c:c + COL_CHUNK] = _dot(hn, w_ref[:, c:c + COL_CHUNK])
    qm_ref[...] = _dot(hn, w_ref[:, tok_width:]).astype(BF16)


def _inproj_pool(x2d, g, w_bf16, tok_width):
    t, d_model = x2d.shape
    n = w_bf16.shape[1]
    return pl.pallas_call(
        functools.partial(_inproj_pool_kernel, tok_width=tok_width),
        out_shape=(jax.ShapeDtypeStruct((t, tok_width), F32),
                   jax.ShapeDtypeStruct((t, n - tok_width), BF16)),
        grid=(t // INPROJ_ROWS,),
        in_specs=[
            pl.BlockSpec((INPROJ_ROWS, d_model), lambda i: (i, 0)),
            pl.BlockSpec((1, d_model), lambda i: (0, 0)),
            _resident((d_model, n), lambda i: (0, 0)),
        ],
        out_specs=(pl.BlockSpec((INPROJ_ROWS, tok_width), lambda i: (i, 0)),
                   pl.BlockSpec((INPROJ_ROWS, n - tok_width), lambda i: (i, 0))),
        compiler_params=_params(1),
        name="inproj_pool",
    )(x2d, g, w_bf16)


def _pool_group(u_ref, gw_ref, sc_ref, o_ref, a_ref, b_ref, *, window, seq):
    gwid = u_ref.shape[2]
    rows = seq + 2 * POOL_PAD
    zeros_pad = jnp.zeros((POOL_PAD, gwid), F32)
    a_ref[0:POOL_PAD, :] = zeros_pad
    b_ref[0:POOL_PAD, :] = zeros_pad
    a_ref[POOL_PAD:POOL_PAD + seq, :] = u_ref[0]
    a_ref[POOL_PAD + seq:rows, :] = zeros_pad
    src, dst = a_ref, b_ref
    shift = 1
    while shift < window:
        dst[POOL_PAD:rows, :] = src[POOL_PAD - shift:rows - shift, :] + src[POOL_PAD:rows, :]
        src, dst = dst, src
        shift *= 2
    off = POOL_PAD + window // 2 - 1
    win = src[off:off + seq, :]
    t = lax.broadcasted_iota(I32, (seq, 1), 0)
    lo = jnp.maximum(t - window // 2, 0)
    hi = jnp.minimum(t + window // 2 - 1, seq - 1)
    cnt = (hi - lo + 1).astype(F32)
    pooled = (win / cnt - u_ref[0]).astype(BF16)
    o_ref[0] = (_dot(pooled, gw_ref[0]) * sc_ref[0]).astype(BF16)


def _pool_kernel(u_ref, gw_ref, sc_ref, o_ref, a_ref, b_ref, *, seq):
    g = pl.program_id(1)
    for k, window in enumerate(POOL_WINDOWS):
        @pl.when(g == k)
        def _():
            _pool_group(u_ref, gw_ref, sc_ref, o_ref, a_ref, b_ref, window=window, seq=seq)


def _pool_mixer(u3d, gw_bf16, scale3d):
    b, seq, tok_width = u3d.shape
    n_groups, gwid, _ = gw_bf16.shape
    assert n_groups == len(POOL_WINDOWS) and n_groups * gwid == tok_width
    return pl.pallas_call(
        functools.partial(_pool_kernel, seq=seq),
        out_shape=jax.ShapeDtypeStruct((b, seq, tok_width), BF16),
        grid=(b, n_groups),
        in_specs=[
            pl.BlockSpec((1, seq, gwid), lambda i, g: (i, 0, g)),
            pl.BlockSpec((1, gwid, gwid), lambda i, g: (g, 0, 0)),
            pl.BlockSpec((1, 1, gwid), lambda i, g: (g, 0, 0)),
        ],
        out_specs=pl.BlockSpec((1, seq, gwid), lambda i, g: (i, 0, g)),
        scratch_shapes=[pltpu.VMEM((seq + 2 * POOL_PAD, gwid), F32),
                        pltpu.VMEM((seq + 2 * POOL_PAD, gwid), F32)],
        compiler_params=_params(2),
        name="pool_mixer",
    )(u3d, gw_bf16, scale3d)


def _xattn_kernel(q_ref, kv_ref, o_ref):
    scale = XA_HEAD_DIM ** -0.5
    for h in range(XA_HEADS):
        lo = h * XA_HEAD_DIM
        q = q_ref[:, lo:lo + XA_HEAD_DIM]
        k = kv_ref[0, 0, :, lo:lo + XA_HEAD_DIM]
        v = kv_ref[0, 0, :, XA_WIDTH + lo:XA_WIDTH + lo + XA_HEAD_DIM]
        s = _dot_nt(q, k) * scale
        m = jnp.max(s, axis=-1, keepdims=True)
        p = jnp.exp(s - m)
        den = jnp.sum(p, axis=-1, keepdims=True)
        o_ref[:, lo:lo + XA_HEAD_DIM] = (_dot(p.astype(BF16), v) / den).astype(BF16)


XATTN_ROWS = 1024


def _mem_xattn(qm2d, memkv, layer, seq):
    t = qm2d.shape[0]
    tiles_per_seq = seq // XATTN_ROWS
    return pl.pallas_call(
        _xattn_kernel,
        out_shape=jax.ShapeDtypeStruct((t, XA_WIDTH), BF16),
        grid=(t // XATTN_ROWS,),
        in_specs=[
            pl.BlockSpec((XATTN_ROWS, XA_WIDTH), lambda i: (i, 0)),
            pl.BlockSpec((1, 1, MEM_LEN, 2 * XA_WIDTH), lambda i: (layer, i // tiles_per_seq, 0, 0)),
        ],
        out_specs=pl.BlockSpec((XATTN_ROWS, XA_WIDTH), lambda i: (i, 0)),
        compiler_params=_params(1),
        name="mem_xattn",
    )(qm2d, memkv)


def _outproj_kernel(tok_ref, mo_ref, x_ref, w_ref, g_ref, rw_ref,
                    x1_ref, h_ref, afft_ref, aff_ref, wcat_ref, x1prev_ref,
                    *, tok_width, n_experts):
    @pl.when(pl.program_id(0) == 0)
    def _():
        rw = rw_ref[0]
        w_hi = rw.astype(BF16)
        wcat_ref[:, 0:LANES] = w_hi
        wcat_ref[:, LANES:2 * LANES] = (rw - w_hi.astype(F32)).astype(BF16)
        x1prev_ref[...] = jnp.zeros_like(x1prev_ref)

    hn = _rmsnorm_rows(x1prev_ref[...], g_ref[...])
    h_prev = hn.astype(BF16)
    h_lo = (hn - h_prev.astype(F32)).astype(BF16)
    r = _dot(h_prev, wcat_ref[...]) + _dot(h_lo, wcat_ref[...])
    logits = r[:, 0:LANES] + r[:, LANES:2 * LANES]
    lt = logits.T[0:n_experts, :]
    m = jnp.max(lt, axis=0, keepdims=True)
    ex = jnp.exp(lt - m)
    afft = ex / jnp.sum(ex, axis=0, keepdims=True)
    afft_ref[0] = afft
    padded = jnp.concatenate(
        [afft, jnp.zeros((LANES - n_experts, afft.shape[1]), F32)], axis=0)
    aff = padded.T
    aff_ref[...] = aff
    h_ref[...] = _pack_row_words(h_prev, _pack_gate_lanes(aff))

    y = _dot(tok_ref[...], w_ref[0:tok_width, :]) + _dot(mo_ref[...], w_ref[tok_width:, :])
    x1 = x_ref[...] + y
    x1_ref[...] = x1
    x1prev_ref[...] = x1


def _outproj(tok2d, mo2d, x2d, w_bf16, g, rw_pad, layer, n_experts, seq):
    t, d_model = x2d.shape
    tok_width = tok2d.shape[1]
    tiles_per_seq = seq // ROW_TILE
    n_tiles = t // ROW_TILE

    def cur(i):
        return jnp.minimum(i, n_tiles - 1)

    def prev(i):
        return jnp.maximum(i - 1, 0)

    return pl.pallas_call(
        functools.partial(_outproj_kernel, tok_width=tok_width, n_experts=n_experts),
        out_shape=(jax.ShapeDtypeStruct((t, d_model), F32),
                   jax.ShapeDtypeStruct((t, d_model // 2 + LANES), U32),
                   jax.ShapeDtypeStruct((t // seq, n_experts, seq), F32),
                   jax.ShapeDtypeStruct((t, LANES), F32)),
        grid=(n_tiles + 1,),
        in_specs=[
            pl.BlockSpec((ROW_TILE, tok_width), lambda i: (cur(i), 0)),
            pl.BlockSpec((ROW_TILE, mo2d.shape[1]), lambda i: (cur(i), 0)),
            pl.BlockSpec((ROW_TILE, d_model), lambda i: (cur(i), 0)),
            _resident((d_model, d_model), lambda i: (0, 0)),
            pl.BlockSpec((1, d_model), lambda i: (0, 0)),
            _resident((1, d_model, LANES), lambda i: (layer, 0, 0)),
        ],
        out_specs=(pl.BlockSpec((ROW_TILE, d_model), lambda i: (cur(i), 0)),
                   pl.BlockSpec((ROW_TILE, d_model // 2 + LANES), lambda i: (prev(i), 0)),
                   pl.BlockSpec((1, n_experts, ROW_TILE),
                                lambda i: (prev(i) // tiles_per_seq, 0, prev(i) % tiles_per_seq)),
                   pl.BlockSpec((ROW_TILE, LANES), lambda i: (prev(i), 0))),
        scratch_shapes=[pltpu.VMEM((d_model, 2 * LANES), BF16),
                        pltpu.VMEM((ROW_TILE, d_model), F32)],
        compiler_params=_params(1),
        name="outproj_router",
    )(tok2d, mo2d, x2d, w_bf16, g, rw_pad)


def _strict_triangle(n, lower):
    r = lax.broadcasted_iota(I32, (n, n), 0)
    c = lax.broadcasted_iota(I32, (n, n), 1)
    return jnp.where((c < r) if lower else (r < c), 1.0, 0.0).astype(BF16)


def _prefix_rows(mask_f32):
    s, l = mask_f32.shape
    tri = _strict_triangle(PREFIX_CHUNK, lower=True)
    carry = jnp.zeros((1, l), F32)
    out = []
    for c in range(0, s, PREFIX_CHUNK):
        m = mask_f32[c:c + PREFIX_CHUNK, :]
        out.append(_dot(tri, m.astype(BF16)) + carry)
        carry = carry + jnp.sum(m, axis=0, keepdims=True)
    return jnp.concatenate(out, axis=0)


def _prefix_lanes(mask_f32):
    e, s = mask_f32.shape
    tri = _strict_triangle(PREFIX_CHUNK, lower=False)
    carry = jnp.zeros((e, 1), F32)
    out = []
    for c in range(0, s, PREFIX_CHUNK):
        m = mask_f32[:, c:c + PREFIX_CHUNK]
        out.append(_dot(m.astype(BF16), tri) + carry)
        carry = carry + jnp.sum(m, axis=1, keepdims=True)
    return jnp.concatenate(out, axis=1)


def _select_slots(key, thr, need, prefix_fn):
    return _select_slots_and_counts(key, thr, need, prefix_fn)[0]


def _select_slots_and_counts(key, thr, need, prefix_fn):
    gt = jnp.where(key > thr, 1.0, 0.0)
    eq = jnp.where(key == thr, 1.0, 0.0)
    eq_rank = prefix_fn(eq)
    sel = gt + eq * jnp.where(eq_rank < need, 1.0, 0.0)
    pos = prefix_fn(sel)
    return jnp.where(sel > 0.5, pos, -1.0), pos


GATE_GROUP = 16
GATE_PIECES = 3


def _pack_gate_lanes(aff):
    hi = aff.astype(BF16).astype(F32)
    r1 = aff - hi
    mid = r1.astype(BF16).astype(F32)
    lo = (r1 - mid).astype(BF16).astype(F32)
    packed = hi + pltpu.roll(mid, GATE_GROUP, 1) + pltpu.roll(lo, 2 * GATE_GROUP, 1)
    return packed.astype(BF16)


def _unpack_gate(tail, e):
    lane = lax.broadcasted_iota(I32, (1, LANES), 1)
    mine = ((lane & (GATE_GROUP - 1)) == e) & (lane < GATE_PIECES * GATE_GROUP)
    return jnp.sum(jnp.where(mine, tail.astype(F32), 0.0), axis=1, keepdims=True)


def _pack_row_words(h, gate_tile):
    rows, d_model = h.shape
    half = d_model // 2
    hi = jnp.concatenate([h[:, 0:half], gate_tile], axis=1).astype(F32)
    lo = jnp.concatenate([h[:, half:], jnp.zeros((rows, LANES), BF16)], axis=1).astype(F32)
    return pltpu.bitcast(hi, U32) | (pltpu.bitcast(lo, U32) >> 16)


def _unpack_row_words(words):
    hi = pltpu.bitcast(words & jnp.uint32(0xFFFF0000), F32)
    lo = pltpu.bitcast(words << 16, F32)
    return hi, lo


def _route_kernel(afft_ref, posm_ref, thr_ref, need_ref, starts_ref, *, cap):
    n_experts, seq = afft_ref.shape[1], afft_ref.shape[2]
    key = pltpu.bitcast(afft_ref[0], I32)
    thr = jnp.zeros((n_experts, 1), I32)
    for bit in range(F32_KEY_BITS - 1, -1, -1):
        cand = thr | (1 << bit)
        cnt = jnp.sum(jnp.where(key >= cand, 1.0, 0.0), axis=1, keepdims=True)
        thr = jnp.where(cnt >= cap, cand, thr)
    n_gt = jnp.sum(jnp.where(key > thr, 1.0, 0.0), axis=1, keepdims=True)
    need = cap - n_gt
    thr_ref[0] = thr
    need_ref[0] = need
    posm, before = _select_slots_and_counts(key, thr, need, _prefix_lanes)
    posm_ref[0] = posm.astype(I32)
    starts_ref[0] = jnp.concatenate(
        [before[:, r:r + 1] for r in range(0, seq, COMBINE_ROWS)], axis=1).astype(I32)


def _route(afft, cap):
    b, n_experts, seq = afft.shape
    n_tiles = seq // COMBINE_ROWS
    return pl.pallas_call(
        functools.partial(_route_kernel, cap=cap),
        out_shape=(jax.ShapeDtypeStruct((b, n_experts, seq), I32),
                   jax.ShapeDtypeStruct((b, n_experts, 1), I32),
                   jax.ShapeDtypeStruct((b, n_experts, 1), F32),
                   jax.ShapeDtypeStruct((b, n_experts, n_tiles), I32)),
        grid=(b,),
        in_specs=[pl.BlockSpec((1, n_experts, seq), lambda i: (i, 0, 0))],
        out_specs=(pl.BlockSpec((1, n_experts, seq), lambda i: (i, 0, 0)),
                   pl.BlockSpec((1, n_experts, 1), lambda i: (i, 0, 0)),
                   pl.BlockSpec((1, n_experts, 1), lambda i: (i, 0, 0)),
                   pl.BlockSpec((1, n_experts, n_tiles), lambda i: (i, 0, 0))),
        compiler_params=_params(1),
        name="expert_route",
    )(afft)


SC_LANES = 16
SC_GATHER_ROWS = 64


def _sc_expert_gather(posm2d, h_words, n_experts, e_offset, n_e, b, seq, cap):
    width = h_words.shape[1]
    info = plsc.get_sparse_core_info()
    n_cores, n_subcores = info.num_cores, info.num_subcores
    n_workers = n_cores * n_subcores
    assert info.num_lanes == SC_LANES and (b * n_e) % n_workers == 0
    pairs_per_worker = (b * n_e) // n_workers
    mesh = plsc.VectorSubcoreMesh(core_axis_name="c", subcore_axis_name="s")

    @functools.partial(
        pl.kernel, mesh=mesh,
        out_type=jax.ShapeDtypeStruct((n_e * b * cap, width), U32),
        compiler_params=pltpu.CompilerParams(needs_layout_passes=False),
        scratch_types=[
            pltpu.VMEM((seq,), I32),
            pltpu.VMEM((cap,), I32),
            pltpu.VMEM((SC_GATHER_ROWS, width), U32),
            pltpu.SemaphoreType.DMA,
        ],
        name="sc_expert_gather",
    )
    def gather(posm_hbm, h_hbm, out_hbm, pos_v, idx_v, rows_v, sem):
        wid = lax.axis_index("s") * n_cores + lax.axis_index("c")
        for p in range(pairs_per_worker):
            pair = wid * pairs_per_worker + p
            bi = pair // n_e
            e = pair - bi * n_e
            pltpu.sync_copy(posm_hbm.at[bi * n_experts + e_offset + e], pos_v)

            @pl.loop(0, seq, step=SC_LANES)
            def _(t0):
                slots = pos_v[pl.ds(t0, SC_LANES)]
                rows = lax.iota(I32, SC_LANES) + (t0 + bi * seq)
                plsc.store_scatter(idx_v, [slots], rows, mask=slots >= 0)

            out_base = (e * b + bi) * cap
            for c in range(cap // SC_GATHER_ROWS):
                chunk = idx_v.at[pl.ds(c * SC_GATHER_ROWS, SC_GATHER_ROWS)]
                pltpu.async_copy(h_hbm.at[chunk], rows_v, sem).wait()
                pltpu.sync_copy(rows_v, out_hbm.at[pl.ds(out_base + c * SC_GATHER_ROWS, SC_GATHER_ROWS)])

    return gather(posm2d, h_words).reshape(n_e, b, cap, width)


EXPERT_ROWS = 1024
EXPERT_FTILE = 256


EXPERT_WBUFS = 2


def _expert_kernel(xg_ref, wg_hbm, wu_hbm, wd_hbm, o_ref,
                   x_ref, g_ref, hact_ref, wg_full, wu_full, wd_full, wg_buf, wu_buf, wd_buf, sem,
                   *, layer, e_offset, n_ftiles):
    e = pl.program_id(0)
    m = pl.program_id(1)
    n_e = pl.num_programs(0)
    nb, cap, width = xg_ref.shape[1], xg_ref.shape[2], xg_ref.shape[3]
    half = width - LANES
    d_model = 2 * half
    rows = nb * cap
    tf = EXPERT_FTILE
    assert n_ftiles % EXPERT_WBUFS == 0

    def tile_copies(expert, f):
        ge = e_offset + expert
        slot = f % EXPERT_WBUFS
        return (
            pltpu.make_async_copy(wg_hbm.at[layer, ge, :, pl.ds(f * tf, tf)], wg_buf.at[slot], sem.at[0, slot]),
            pltpu.make_async_copy(wu_hbm.at[layer, ge, :, pl.ds(f * tf, tf)], wu_buf.at[slot], sem.at[1, slot]),
            pltpu.make_async_copy(wd_hbm.at[layer, ge, pl.ds(f * tf, tf), :], wd_buf.at[slot], sem.at[2, slot]),
        )

    def start(expert, f):
        for cp in tile_copies(expert, f):
            cp.start()

    @pl.when((e == 0) & (m == 0))
    def _():
        for f in range(EXPERT_WBUFS):
            start(e, f)

    for i in range(nb):
        r0 = i * cap
        hi, lo = _unpack_row_words(xg_ref[0, i])
        x_ref[r0:r0 + cap, 0:half] = hi[:, 0:half].astype(BF16)
        x_ref[r0:r0 + cap, half:d_model] = lo[:, 0:half].astype(BF16)
        g_ref[r0:r0 + cap, :] = jnp.broadcast_to(
            _unpack_gate(hi[:, half:width], e_offset + e), (cap, LANES))

    def receive(f):
        slot = f % EXPERT_WBUFS
        for cp in tile_copies(e, f):
            cp.wait()
        wg_full[:, f * tf:(f + 1) * tf] = wg_buf[slot].astype(BF16)
        wu_full[:, f * tf:(f + 1) * tf] = wu_buf[slot].astype(BF16)
        wd_full[f * tf:(f + 1) * tf, :] = wd_buf[slot].astype(BF16)
        ahead = f + EXPERT_WBUFS
        if ahead < n_ftiles:
            start(e, ahead)
        else:
            @pl.when(e + 1 < n_e)
            def _():
                start(e + 1, ahead - n_ftiles)

    def body(first_group):
        x = x_ref[...]
        for f in range(n_ftiles):
            if first_group:
                receive(f)
            a = _dot(x, wg_full[:, f * tf:(f + 1) * tf])
            u = _dot(x, wu_full[:, f * tf:(f + 1) * tf])
            hact_ref[:, f * tf:(f + 1) * tf] = (a * jax.nn.sigmoid(a) * u).astype(BF16)
        for c in range(0, d_model, COL_CHUNK):
            y = _dot(hact_ref[...], wd_full[:, c:c + COL_CHUNK])
            for j in range(0, COL_CHUNK, LANES):
                o_ref[:, 0, :, c + j:c + j + LANES] = (
                    (y[:, j:j + LANES] * g_ref[...]).astype(BF16).reshape(nb, cap, LANES))

    @pl.when(m == 0)
    def _():
        body(True)

    @pl.when(m != 0)
    def _():
        body(False)


def _experts(xg, w_gate, w_up, w_down, layer, e_offset):
    n_e, b, cap, width = xg.shape
    d_model = 2 * (width - LANES)
    d_expert = w_gate.shape[3]
    nb = EXPERT_ROWS // cap
    n_ftiles = d_expert // EXPERT_FTILE
    return pl.pallas_call(
        functools.partial(_expert_kernel, layer=layer, e_offset=e_offset, n_ftiles=n_ftiles),
        out_shape=jax.ShapeDtypeStruct((b, n_e, cap, d_model), BF16),
        grid=(n_e, b // nb),
        in_specs=[
            pl.BlockSpec((1, nb, cap, width), lambda e, m: (e, m, 0, 0)),
            pl.BlockSpec(memory_space=pl.ANY),
            pl.BlockSpec(memory_space=pl.ANY),
            pl.BlockSpec(memory_space=pl.ANY),
        ],
        out_specs=pl.BlockSpec((nb, 1, cap, d_model), lambda e, m: (m, e, 0, 0)),
        scratch_shapes=[pltpu.VMEM((EXPERT_ROWS, d_model), BF16),
                        pltpu.VMEM((EXPERT_ROWS, LANES), F32),
                        pltpu.VMEM((EXPERT_ROWS, d_expert), BF16),
                        pltpu.VMEM((d_model, d_expert), BF16),
                        pltpu.VMEM((d_model, d_expert), BF16),
                        pltpu.VMEM((d_expert, d_model), BF16),
                        pltpu.VMEM((EXPERT_WBUFS, d_model, EXPERT_FTILE), F32),
                        pltpu.VMEM((EXPERT_WBUFS, d_model, EXPERT_FTILE), F32),
                        pltpu.VMEM((EXPERT_WBUFS, EXPERT_FTILE, d_model), F32),
                        pltpu.SemaphoreType.DMA((3, EXPERT_WBUFS))],
        compiler_params=_params(2),
        name="experts",
    )(xg, w_gate, w_up, w_down)


COMBINE_ROWS = 256
COMBINE_WINDOW = 64
MXU_DEPTH = 256
BF16_ROWS = 16


def _combine_kernel(starts_ref, aff_ref, thr_ref, need_ref, x1_ref, *rest, cap, n_experts, final_norm):
    n_y = len(rest) - (6 if final_norm else 5)
    y_hbm = rest[:n_y]
    g_ref = rest[n_y] if final_norm else None
    o_ref, post_ref, pfull_ref, ybuf, sem = rest[-5:]
    b = pl.program_id(0)
    t = pl.program_id(1)
    n_b = pl.num_programs(0)
    n_t = pl.num_programs(1)
    rows, win = COMBINE_ROWS, COMBINE_WINDOW
    experts_per_group = n_experts // n_y
    group_rows = ybuf.shape[0] // (2 * n_y)

    def ybuf_row(buf_slot, g):
        return pl.multiple_of((buf_slot * n_y + g) * group_rows, BF16_ROWS)

    n_chunks = sem.shape[1]
    chunks_per_group = n_chunks // n_y
    chunk_rows = group_rows // chunks_per_group
    slot = b % 2

    def chunk_copy(seq_idx, c, dst_slot):
        g, r = c // chunks_per_group, (c % chunks_per_group) * chunk_rows
        return pltpu.make_async_copy(y_hbm[g].at[seq_idx, pl.ds(r, chunk_rows), :],
                                     ybuf.at[pl.ds(ybuf_row(dst_slot, g) + r, chunk_rows), :],
                                     sem.at[dst_slot, c])

    @pl.when((b == 0) & (t == 0))
    def _():
        for c in range(n_chunks):
            chunk_copy(0, c, 0).start()

    @pl.when(t == 0)
    def _():
        for c in range(n_chunks):
            chunk_copy(b, c, slot).wait()
        key = pltpu.bitcast(aff_ref[...], I32)
        post_ref[...] = _select_slots(key, thr_ref[0], need_ref[0], _prefix_rows)

    for c in range(n_chunks):
        @pl.when((t == c) & (b + 1 < n_b))
        def _():
            chunk_copy(b + 1, c, 1 - slot).start()

    posm = post_ref[pl.ds(pl.multiple_of(t * rows, rows), rows), :]
    base = (b * (n_t + 1) + t) * n_experts
    wstart, ok = [], None
    for e in range(n_experts):
        first = starts_ref[base + e]
        end = starts_ref[base + n_experts + e]
        w0 = jnp.minimum((first // BF16_ROWS) * BF16_ROWS, cap - win)
        fits = end - w0 <= win
        wstart.append(w0)
        ok = fits if ok is None else jnp.logical_and(ok, fits)

    def finish(acc):
        if final_norm:
            acc = _rmsnorm_rows(acc, g_ref[...])
        o_ref[0] = acc

    @pl.when(ok)
    def _():
        per_dot, per_tile = MXU_DEPTH // win, LANES // win
        lane = lax.broadcasted_iota(I32, (1, LANES), 1)
        lane_f = lane.astype(F32)
        acc = x1_ref[0]
        for e0 in range(0, n_experts, per_dot):
            onehots, windows = [], []
            for e1 in range(e0, e0 + per_dot, per_tile):
                rel = None
                for k in range(per_tile - 1, -1, -1):
                    e = e1 + k
                    shifted = posm[:, e:e + 1] - (wstart[e] - k * win).astype(F32)
                    rel = shifted if rel is None else jnp.where(lane < (k + 1) * win, shifted, rel)
                onehots.append(jnp.where(rel == lane_f, 1.0, 0.0).astype(BF16))
            for e in range(e0, e0 + per_dot):
                g, el = divmod(e, experts_per_group)
                r0 = pl.multiple_of(ybuf_row(slot, g) + el * cap + wstart[e], BF16_ROWS)
                windows.append(ybuf[pl.ds(r0, win), :])
            acc = acc + _dot(jnp.concatenate(onehots, axis=1), jnp.concatenate(windows, axis=0))
        finish(acc)

    @pl.when(jnp.logical_not(ok))
    def _():
        slot_ids = lax.broadcasted_iota(I32, (1, cap), 1).astype(F32)
        for e in range(n_experts):
            pfull_ref[:, e * cap:(e + 1) * cap] = jnp.where(
                posm[:, e:e + 1] == slot_ids, 1.0, 0.0).astype(BF16)
        acc = x1_ref[0]
        for g in range(n_y):
            acc = acc + _dot(pfull_ref[:, g * group_rows:(g + 1) * group_rows],
                             ybuf[pl.ds(ybuf_row(slot, g), group_rows), :])
        finish(acc)


def _combine(starts, aff2d, thr_row, need_row, x1_3d, y_groups, cap, n_experts, final_g=None):
    b, seq, d_model = x1_3d.shape
    assert sum(y.shape[1] for y in y_groups) == n_experts * cap and cap >= COMBINE_WINDOW
    final_norm = final_g is not None
    n_tiles = seq // COMBINE_ROWS
    group_rows = y_groups[0].shape[1]
    assert all(y.shape[1] == group_rows for y in y_groups)
    assert n_tiles % len(y_groups) == 0 and group_rows % (n_tiles // len(y_groups)) == 0
    y_specs = [pl.BlockSpec(memory_space=pl.ANY) for _ in y_groups]
    g_specs = [pl.BlockSpec((1, d_model), lambda i, t, s: (0, 0))] if final_norm else []
    g_args = [final_g] if final_norm else []
    return pl.pallas_call(
        functools.partial(_combine_kernel, cap=cap, n_experts=n_experts, final_norm=final_norm),
        out_shape=jax.ShapeDtypeStruct((b, seq, d_model), F32),
        grid_spec=pltpu.PrefetchScalarGridSpec(
            num_scalar_prefetch=1,
            grid=(b, n_tiles),
            in_specs=[
                pl.BlockSpec((seq, LANES), lambda i, t, s: (i, 0)),
                pl.BlockSpec((1, 1, LANES), lambda i, t, s: (i, 0, 0)),
                pl.BlockSpec((1, 1, LANES), lambda i, t, s: (i, 0, 0)),
                pl.BlockSpec((1, COMBINE_ROWS, d_model), lambda i, t, s: (i, t, 0)),
            ] + y_specs + g_specs,
            out_specs=pl.BlockSpec((1, COMBINE_ROWS, d_model), lambda i, t, s: (i, t, 0)),
            scratch_shapes=[pltpu.VMEM((seq, LANES), F32),
                            pltpu.VMEM((COMBINE_ROWS, n_experts * cap), BF16),
                            pltpu.VMEM((2 * len(y_groups) * group_rows, d_model), BF16),
                            pltpu.SemaphoreType.DMA((2, n_tiles))],
        ),
        compiler_params=_params(2),
        name="combine",
    )(starts, aff2d, thr_row, need_row, x1_3d, *y_groups, *g_args)


EXPERT_GROUPS = 2


def _moe(afft, aff2d, h_words, x1_2d, w_gate, w_up, w_down, layer, b, seq, final_g=None):
    n_experts = afft.shape[1]
    assert n_experts <= GATE_GROUP and n_experts % EXPERT_GROUPS == 0
    d_model = x1_2d.shape[1]
    cap = CAPACITY_FACTOR * seq // n_experts
    n_e = n_experts // EXPERT_GROUPS
    posm, thr, need, tile_starts = _route(afft, cap)
    starts = jnp.concatenate([jnp.swapaxes(tile_starts, 1, 2),
                              jnp.full((b, 1, n_experts), cap, I32)], axis=1).reshape(-1)
    posm2d = posm.reshape(b * n_experts, seq)
    xgs = [_sc_expert_gather(posm2d, h_words, n_experts, g * n_e, n_e, b, seq, cap)
           for g in range(EXPERT_GROUPS)]
    ys = [_experts(xg, w_gate, w_up, w_down, layer, g * n_e).reshape(b, n_e * cap, d_model)
          for g, xg in enumerate(xgs)]
    pad = LANES - n_experts
    thr_row = jnp.pad(thr.reshape(b, 1, n_experts), ((0, 0), (0, 0), (0, pad)),
                      constant_values=np.iinfo(np.int32).max)
    need_row = jnp.pad(need.reshape(b, 1, n_experts), ((0, 0), (0, 0), (0, pad)))
    return _combine(starts, aff2d, thr_row, need_row, x1_2d.reshape(b, seq, d_model), ys, cap,
                    n_experts, final_g)


def _rotary_tile(t, cos, sin_lo, sin_hi):
    half = ROT_DIM // 2
    return t * cos + pltpu.roll(t, LANES - half, 1) * sin_lo + pltpu.roll(t, half, 1) * sin_hi


def _inproj_attn_kernel(x_ref, g_ref, w_ref, pos_ref, rot_ref,
                        q_ref, k_ref, v_ref, qm_ref, *, tok_width, kv_width):
    qscale = HEAD_DIM ** -0.5
    rows = x_ref.shape[0] // INPROJ_PARTS

    def prepare(part):
        r = slice(part * rows, (part + 1) * rows)
        hn = _rmsnorm_rows(x_ref[r, :], g_ref[...]).astype(BF16)
        ang = pos_ref[r, :].astype(F32) * rot_ref[0:1, :]
        cos = jnp.cos(ang)
        sin = jnp.sin(ang)
        return hn, cos, sin * rot_ref[1:2, :], sin * rot_ref[2:3, :]

    def project(part, hn, cos, sin_lo, sin_hi):
        r = slice(part * rows, (part + 1) * rows)
        for c in range(0, tok_width, COL_CHUNK):
            pc = _dot(hn, w_ref[:, c:c + COL_CHUNK])
            for j in range(0, COL_CHUNK, LANES):
                rot = _rotary_tile(pc[:, j:j + LANES], cos, sin_lo, sin_hi)
                q_ref[r, c + j:c + j + LANES] = (rot * qscale).astype(BF16)
        kv = _dot(hn, w_ref[:, tok_width:tok_width + 2 * kv_width])
        k01 = _rotary_tile(kv[:, 0:LANES], cos, sin_lo, sin_hi)
        k2x = _rotary_tile(kv[:, LANES:2 * LANES], cos, sin_lo, sin_hi)
        k_ref[0, 0, r, :] = k01[:, 0:HEAD_DIM].astype(BF16)
        k_ref[0, 1, r, :] = k01[:, HEAD_DIM:LANES].astype(BF16)
        k_ref[0, 2, r, :] = k2x[:, 0:HEAD_DIM].astype(BF16)
        for hh in range(kv_width // HEAD_DIM):
            lo = kv_width + hh * HEAD_DIM
            v_ref[0, hh, r, :] = kv[:, lo:lo + HEAD_DIM].astype(BF16)
        qm_ref[r, :] = _dot(hn, w_ref[:, tok_width + 2 * kv_width:]).astype(BF16)

    prepared = [prepare(part) for part in range(INPROJ_PARTS)]
    for part in range(INPROJ_PARTS):
        project(part, *prepared[part])


def _inproj_attn(x2d, g, w_bf16, pos2d, rot_rows, tok_width, kv_width, seq):
    t, d_model = x2d.shape
    n = w_bf16.shape[1]
    n_kv = kv_width // HEAD_DIM
    assert n_kv == 3 and kv_width + HEAD_DIM == 2 * LANES
    tiles_per_seq = seq // INPROJ_ROWS
    kv_spec = pl.BlockSpec((1, n_kv, INPROJ_ROWS, HEAD_DIM),
                           lambda i: (i // tiles_per_seq, 0, i % tiles_per_seq, 0))
    return pl.pallas_call(
        functools.partial(_inproj_attn_kernel, tok_width=tok_width, kv_width=kv_width),
        out_shape=(jax.ShapeDtypeStruct((t, tok_width), BF16),
                   jax.ShapeDtypeStruct((t // seq, n_kv, seq, HEAD_DIM), BF16),
                   jax.ShapeDtypeStruct((t // seq, n_kv, seq, HEAD_DIM), BF16),
                   jax.ShapeDtypeStruct((t, n - tok_width - 2 * kv_width), BF16)),
        grid=(t // INPROJ_ROWS,),
        in_specs=[
            pl.BlockSpec((INPROJ_ROWS, d_model), lambda i: (i, 0)),
            pl.BlockSpec((1, d_model), lambda i: (0, 0)),
            _resident((d_model, n), lambda i: (0, 0)),
            pl.BlockSpec((INPROJ_ROWS, 1), lambda i: (i, 0)),
            pl.BlockSpec((8, LANES), lambda i: (0, 0)),
        ],
        out_specs=(pl.BlockSpec((INPROJ_ROWS, tok_width), lambda i: (i, 0)),
                   kv_spec, kv_spec,
                   pl.BlockSpec((INPROJ_ROWS, n - tok_width - 2 * kv_width), lambda i: (i, 0))),
        compiler_params=_params(1),
        name="inproj_attn",
    )(x2d, g, w_bf16, pos2d, rot_rows)


WATTN_QBLOCKS = 4


def _wattn_kernel(sink_ref, q_ref, kp_ref, kc_ref, kn_ref, vp_ref, vc_ref, vn_ref, o_ref,
                  valid_ref, kpad_ref, vpad_ref, s_ref, p_ref, inv_ref, *, seq):
    step = pl.program_id(1)
    n_kv = kc_ref.shape[1]
    pairs = GQA_RATIO // 2
    half_rows = pairs * BLOCK
    key_rows = (WATTN_QBLOCKS + 2) * BLOCK
    zeros = jnp.zeros((key_rows, HEAD_DIM), BF16)
    ones_col = jnp.where(lax.broadcasted_iota(I32, (key_rows, HEAD_DIM), 1) == 0, 1.0, 0.0).astype(BF16)
    low_half = lax.broadcasted_iota(I32, (1, LANES), 1) < HEAD_DIM
    for hk in range(n_kv):
        kw = jnp.concatenate([kp_ref[0, hk], kc_ref[0, hk], kn_ref[0, hk]], axis=0)
        vw = jnp.concatenate([vp_ref[0, hk], vc_ref[0, hk], vn_ref[0, hk]], axis=0)
        kpad_ref[2 * hk] = jnp.concatenate([kw, zeros], axis=1)
        kpad_ref[2 * hk + 1] = jnp.concatenate([zeros, kw], axis=1)
        vpad_ref[2 * hk] = jnp.concatenate([vw, ones_col], axis=1)
        vpad_ref[2 * hk + 1] = jnp.concatenate([ones_col, vw], axis=1)

    qi = lax.broadcasted_iota(I32, (BLOCK, 3 * BLOCK), 0)
    kj = lax.broadcasted_iota(I32, (BLOCK, 3 * BLOCK), 1)
    for qb in range(WATTN_QBLOCKS):
        n = step * WATTN_QBLOCKS + qb
        k0 = qb * BLOCK
        first = jnp.maximum(qi, BLOCK - n * BLOCK)
        last = jnp.minimum(qi + 2 * WINDOW, seq + BLOCK - 1 - n * BLOCK)
        valid_ref[qb] = jnp.where(((kj - first) | (last - kj)) >= 0, 1.0, 0.0)
        q0 = qb * BLOCK
        for hk in range(n_kv):
            tile0 = hk * pairs
            qs = jnp.concatenate(
                [q_ref[0, q0:q0 + BLOCK, (tile0 + j) * LANES:(tile0 + j + 1) * LANES]
                 for j in range(pairs)], axis=0)
            s_ref[qb, hk, 0:half_rows, :] = _dot_nt(qs, kpad_ref[2 * hk, k0:k0 + 3 * BLOCK, :])
            s_ref[qb, hk, half_rows:2 * half_rows, :] = _dot_nt(
                qs, kpad_ref[2 * hk + 1, k0:k0 + 3 * BLOCK, :])
        for hk in range(n_kv):
            for c in range(GQA_RATIO):
                j, odd = c % pairs, c // pairs
                r = c * BLOCK
                s = jnp.concatenate([
                    jnp.where(valid_ref[qb, :, 0:BLOCK] > 0.5, s_ref[qb, hk, r:r + BLOCK, 0:BLOCK], NEG_INF),
                    s_ref[qb, hk, r:r + BLOCK, BLOCK:2 * BLOCK],
                    jnp.where(valid_ref[qb, :, 2 * BLOCK:] > 0.5,
                              s_ref[qb, hk, r:r + BLOCK, 2 * BLOCK:], NEG_INF),
                ], axis=1)
                sk = sink_ref[hk * GQA_RATIO + 2 * j + odd]
                m = jnp.maximum(jnp.max(s, axis=-1, keepdims=True), sk)
                p_ref[qb, hk, r:r + BLOCK, :] = jnp.exp(s - m).astype(BF16)
                inv_ref[qb, hk, j * BLOCK:(j + 1) * BLOCK, odd * HEAD_DIM:(odd + 1) * HEAD_DIM] = (
                    jnp.broadcast_to(jnp.exp(sk - m), (BLOCK, HEAD_DIM)))
        for hk in range(n_kv):
            pv_even = _dot(p_ref[qb, hk, 0:half_rows, :], vpad_ref[2 * hk, k0:k0 + 3 * BLOCK, :])
            pv_odd = _dot(p_ref[qb, hk, half_rows:2 * half_rows, :],
                          vpad_ref[2 * hk + 1, k0:k0 + 3 * BLOCK, :])
            den = (jnp.where(low_half, pv_even[:, HEAD_DIM:HEAD_DIM + 1], pv_odd[:, 0:1])
                   + inv_ref[qb, hk])
            o = jnp.where(low_half, pv_even, pv_odd) / den
            for j in range(pairs):
                lo = (hk * pairs + j) * LANES
                o_ref[0, q0:q0 + BLOCK, lo:lo + LANES] = o[j * BLOCK:(j + 1) * BLOCK].astype(BF16)


def _window_attention(sink, q3d, k4d, v4d):
    b, seq, tok_width = q3d.shape
    n_kv = k4d.shape[1]
    nb = seq // BLOCK
    qb = WATTN_QBLOCKS
    assert nb % qb == 0
    edge_block = (1, n_kv, BLOCK, HEAD_DIM)
    prev_spec = pl.BlockSpec(edge_block, lambda i, s: (i, 0, jnp.maximum(s * qb - 1, 0), 0))
    cur_spec = pl.BlockSpec((1, n_kv, qb * BLOCK, HEAD_DIM), lambda i, s: (i, 0, s, 0))
    next_spec = pl.BlockSpec(edge_block, lambda i, s: (i, 0, jnp.minimum(s * qb + qb, nb - 1), 0))
    key_rows = (qb + 2) * BLOCK
    return pl.pallas_call(
        functools.partial(_wattn_kernel, seq=seq),
        out_shape=jax.ShapeDtypeStruct((b, seq, tok_width), BF16),
        grid=(b, nb // qb),
        in_specs=[
            pl.BlockSpec(memory_space=pltpu.SMEM),
            pl.BlockSpec((1, qb * BLOCK, tok_width), lambda i, s: (i, s, 0)),
            prev_spec, cur_spec, next_spec, prev_spec, cur_spec, next_spec,
        ],
        out_specs=pl.BlockSpec((1, qb * BLOCK, tok_width), lambda i, s: (i, s, 0)),
        scratch_shapes=[pltpu.VMEM((qb, BLOCK, 3 * BLOCK), F32),
                        pltpu.VMEM((2 * n_kv, key_rows, LANES), BF16),
                        pltpu.VMEM((2 * n_kv, key_rows, LANES), BF16),
                        pltpu.VMEM((qb, n_kv, GQA_RATIO * BLOCK, 3 * BLOCK), F32),
                        pltpu.VMEM((qb, n_kv, GQA_RATIO * BLOCK, 3 * BLOCK), BF16),
                        pltpu.VMEM((qb, n_kv, GQA_RATIO // 2 * BLOCK, LANES), F32)],
        compiler_params=_params(2),
        name="window_attention",
    )(sink, q3d, k4d, k4d, k4d, v4d, v4d, v4d)


def _rotary_rows(dtype=F32):
    half = ROT_DIM // 2
    inv_freq = ROPE_THETA ** (-jnp.arange(0, ROT_DIM, 2, dtype=jnp.float32) / ROT_DIM)
    lane = np.arange(LANES) % HEAD_DIM
    rotated = lane < ROT_DIM
    freq = jnp.where(jnp.asarray(rotated), inv_freq[jnp.asarray(lane % half)], 0.0)
    rows = jnp.zeros((8, LANES), dtype)
    rows = rows.at[0].set(freq)
    rows = rows.at[1].set(jnp.asarray(np.where(lane < half, -1.0, 0.0), dtype))
    rows = rows.at[2].set(jnp.asarray(np.where(rotated & (lane >= half), 1.0, 0.0), dtype))
    return rows


def kernel(x, mem, positions, norm_mix_g, norm_ffn_g, mem_norm_g, final_g, mem_w_kv,
           pool_w_in, pool_group_w, pool_scale, pool_w_out,
           attn_w_in, attn_sink, attn_w_out,
           router_w, exp_w_gate, exp_w_up, exp_w_down):
    b, seq, d_model = x.shape
    depth = norm_mix_g.shape[0]
    t = b * seq
    n_experts = router_w.shape[2]
    tok_width = pool_scale.shape[1]
    n_groups = pool_group_w.shape[1]
    kv_width = (attn_w_in.shape[2] - tok_width - XA_WIDTH) // 2
    assert seq % ROW_TILE == 0 and seq % INPROJ_ROWS == 0 and mem.shape[1] == MEM_LEN

    memkv = _memkv(mem.reshape(b * MEM_LEN, d_model), mem_norm_g.reshape(1, d_model),
                   mem_w_kv.astype(BF16))
    memkv = memkv.reshape(depth, b, MEM_LEN, 2 * XA_WIDTH)
    rw_pad = jnp.pad(router_w, ((0, 0), (0, 0), (0, LANES - n_experts)))
    pos2d = positions.reshape(t, 1)
    rot_rows = _rotary_rows()

    x2d = x.reshape(t, d_model)
    for layer in range(depth):
        j = layer // 2
        g_mix = norm_mix_g[layer].reshape(1, d_model)
        if layer % 2 == 0:
            u, qm = _inproj_pool(x2d, g_mix, pool_w_in[j].astype(BF16), tok_width)
            tok = _pool_mixer(u.reshape(b, seq, tok_width), pool_group_w[j].astype(BF16),
                              pool_scale[j].reshape(n_groups, 1, tok_width // n_groups))
            tok = tok.reshape(t, tok_width)
            w_out = pool_w_out[j]
        else:
            q, k, v, qm = _inproj_attn(x2d, g_mix, attn_w_in[j].astype(BF16), pos2d, rot_rows,
                                       tok_width, kv_width, seq)
            tok = _window_attention(attn_sink[j], q.reshape(b, seq, tok_width), k, v)
            tok = tok.reshape(t, tok_width)
            w_out = attn_w_out[j]
        mo = _mem_xattn(qm, memkv, layer, seq)
        x1, h, afft, aff = _outproj(tok, mo, x2d, w_out.astype(BF16),
                                    norm_ffn_g[layer].reshape(1, d_model), rw_pad, layer,
                                    n_experts, seq)
        last = layer == depth - 1
        x2 = _moe(afft, aff, h, x1, exp_w_gate, exp_w_up, exp_w_down, layer, b, seq,
                  final_g.reshape(1, d_model) if last else None)
        x2d = x2.reshape(t, d_model)
    return x2d.reshape(b, seq, d_model)
```

```python
import functools

import jax
import jax.numpy as jnp
import numpy as np
from jax import lax
from jax.experimental import pallas as pl
from jax.experimental.pallas import tpu as pltpu
from jax.experimental.pallas import tpu_sc as plsc

F32 = jnp.float32
BF16 = jnp.bfloat16
I32 = jnp.int32
U32 = jnp.uint32

EPS = 1e-6
MEM_LEN = 256
XA_HEADS = 4
XA_HEAD_DIM = 128
XA_WIDTH = XA_HEADS * XA_HEAD_DIM
POOL_WINDOWS = (2, 4, 8, 16)
HEAD_DIM = 64
GQA_RATIO = 8
WINDOW = 128
BLOCK = 128
ROPE_THETA = 500000.0
ROT_DIM = 16
NEG_INF = -1e30
CAPACITY_FACTOR = 2

LANES = 128
SUBLANES = 8
MIB = 1024 * 1024
VMEM_LIMIT_BYTES = 56 * MIB

ROW_TILE = 512
INPROJ_ROWS = 1024
INPROJ_PARTS = 4
POOL_PAD = 16
PREFIX_CHUNK = 256
COL_CHUNK = 512
F32_KEY_BITS = 31


def _params(n_grid_dims):
    return pltpu.CompilerParams(
        dimension_semantics=("arbitrary",) * n_grid_dims,
        vmem_limit_bytes=VMEM_LIMIT_BYTES,
    )


def _resident(block_shape, index_map):
    return pl.BlockSpec(block_shape, index_map, pipeline_mode=pl.Buffered(1))


def _rmsnorm_rows(x, g):
    return x * lax.rsqrt(jnp.mean(x * x, axis=-1, keepdims=True) + EPS) * g


def _dot(a, b):
    return jnp.dot(a, b, preferred_element_type=F32)


def _dot_nt(a, b):
    return lax.dot_general(a, b, (((1,), (1,)), ((), ())), preferred_element_type=F32)


def _memkv_kernel(mem_ref, g_ref, w_ref, o_ref):
    hn = _rmsnorm_rows(mem_ref[...], g_ref[...]).astype(BF16)
    o_ref[0] = _dot(hn, w_ref[0]).astype(BF16)


def _memkv(mem2d, g, w_bf16):
    depth, d_model, n = w_bf16.shape
    rows = mem2d.shape[0]
    return pl.pallas_call(
        _memkv_kernel,
        out_shape=jax.ShapeDtypeStruct((depth, rows, n), BF16),
        grid=(depth, rows // ROW_TILE),
        in_specs=[
            pl.BlockSpec((ROW_TILE, d_model), lambda l, i: (i, 0)),
            pl.BlockSpec((1, d_model), lambda l, i: (0, 0)),
            pl.BlockSpec((1, d_model, n), lambda l, i: (l, 0, 0)),
        ],
        out_specs=pl.BlockSpec((1, ROW_TILE, n), lambda l, i: (l, i, 0)),
        compiler_params=_params(2),
        name="memkv",
    )(mem2d, g, w_bf16)


def _inproj_pool_kernel(x_ref, g_ref, w_ref, u_ref, qm_ref, *, tok_width):
    hn = _rmsnorm_rows(x_ref[...], g_ref[...]).astype(BF16)
    for c in range(0, tok_width, COL_CHUNK):
        u_ref[:, c:c + COL_CHUNK] = _dot(hn, w_ref[:, c:c + COL_CHUNK])
    qm_ref[...] = _dot(hn, w_ref[:, tok_width:]).astype(BF16)


def _inproj_pool(x2d, g, w_bf16, tok_width):
    t, d_model = x2d.shape
    n = w_bf16.shape[1]
    return pl.pallas_call(
        functools.partial(_inproj_pool_kernel, tok_width=tok_width),
        out_shape=(jax.ShapeDtypeStruct((t, tok_width), F32),
                   jax.ShapeDtypeStruct((t, n - tok_width), BF16)),
        grid=(t // INPROJ_ROWS,),
        in_specs=[
            pl.BlockSpec((INPROJ_ROWS, d_model), lambda i: (i, 0)),
            pl.BlockSpec((1, d_model), lambda i: (0, 0)),
            _resident((d_model, n), lambda i: (0, 0)),
        ],
        out_specs=(pl.BlockSpec((INPROJ_ROWS, tok_width), lambda i: (i, 0)),
                   pl.BlockSpec((INPROJ_ROWS, n - tok_width), lambda i: (i, 0))),
        compiler_params=_params(1),
        name="inproj_pool",
    )(x2d, g, w_bf16)


def _pool_group(u_ref, gw_ref, sc_ref, o_ref, *, window, seq):
    gwid = u_ref.shape[2]
    rows = seq + 2 * POOL_PAD
    half = window // 2
    assert 2 * half <= POOL_PAD
    zeros_pad = jnp.zeros((POOL_PAD, gwid), F32)
    u = u_ref[0]
    p = jnp.concatenate([zeros_pad, u, zeros_pad], axis=0)
    k = 1
    while k < half:
        p = p + pltpu.roll(p, rows - k, 0)
        k *= 2
    before = p if half % SUBLANES == 0 else pltpu.roll(p, half, 0)
    shift = half if half % SUBLANES == 0 else 0
    win = before[POOL_PAD - shift:POOL_PAD - shift + seq, :] + p[POOL_PAD:POOL_PAD + seq, :]
    t = lax.broadcasted_iota(I32, (seq, 1), 0)
    lo = jnp.maximum(t - half, 0)
    hi = jnp.minimum(t + half - 1, seq - 1)
    cnt = (hi - lo + 1).astype(F32)
    pooled = (win / cnt - u).astype(BF16)
    o_ref[0] = (_dot(pooled, gw_ref[0]) * sc_ref[0]).astype(BF16)


def _pool_kernel(u_ref, gw_ref, sc_ref, o_ref, *, seq):
    g = pl.program_id(1)
    for k, window in enumerate(POOL_WINDOWS):
        @pl.when(g == k)
        def _():
            _pool_group(u_ref, gw_ref, sc_ref, o_ref, window=window, seq=seq)


def _pool_mixer(u3d, gw_bf16, scale3d):
    b, seq, tok_width = u3d.shape
    n_groups, gwid, _ = gw_bf16.shape
    assert n_groups == len(POOL_WINDOWS) and n_groups * gwid == tok_width
    return pl.pallas_call(
        functools.partial(_pool_kernel, seq=seq),
        out_shape=jax.ShapeDtypeStruct((b, seq, tok_width), BF16),
        grid=(b, n_groups),
        in_specs=[
            pl.BlockSpec((1, seq, gwid), lambda i, g: (i, 0, g)),
            pl.BlockSpec((1, gwid, gwid), lambda i, g: (g, 0, 0)),
            pl.BlockSpec((1, 1, gwid), lambda i, g: (g, 0, 0)),
        ],
        out_specs=pl.BlockSpec((1, seq, gwid), lambda i, g: (i, 0, g)),
        compiler_params=_params(2),
        name="pool_mixer",
    )(u3d, gw_bf16, scale3d)


def _xattn_kernel(q_ref, kv_ref, o_ref):
    scale = XA_HEAD_DIM ** -0.5
    for h in range(XA_HEADS):
        lo = h * XA_HEAD_DIM
        q = q_ref[:, lo:lo + XA_HEAD_DIM]
        k = kv_ref[0, 0, :, lo:lo + XA_HEAD_DIM]
        v = kv_ref[0, 0, :, XA_WIDTH + lo:XA_WIDTH + lo + XA_HEAD_DIM]
        s = _dot_nt(q, k) * scale
        m = jnp.max(s, axis=-1, keepdims=True)
        p = jnp.exp(s - m)
        den = jnp.sum(p, axis=-1, keepdims=True)
        o_ref[:, lo:lo + XA_HEAD_DIM] = (_dot(p.astype(BF16), v) / den).astype(BF16)


XATTN_ROWS = 1024


def _mem_xattn(qm2d, memkv, layer, seq):
    t = qm2d.shape[0]
    tiles_per_seq = seq // XATTN_ROWS
    return pl.pallas_call(
        _xattn_kernel,
        out_shape=jax.ShapeDtypeStruct((t, XA_WIDTH), BF16),
        grid=(t // XATTN_ROWS,),
        in_specs=[
            pl.BlockSpec((XATTN_ROWS, XA_WIDTH), lambda i: (i, 0)),
            pl.BlockSpec((1, 1, MEM_LEN, 2 * XA_WIDTH), lambda i: (layer, i // tiles_per_seq, 0, 0)),
        ],
        out_specs=pl.BlockSpec((XATTN_ROWS, XA_WIDTH), lambda i: (i, 0)),
        compiler_params=_params(1),
        name="mem_xattn",
    )(qm2d, memkv)


def _outproj_kernel(tok_ref, mo_ref, x_ref, w_ref, g_ref, rw_ref,
                    x1_ref, h_ref, afft_ref, aff_ref, wcat_ref, x1prev_ref,
                    *, tok_width, n_experts):
    @pl.when(pl.program_id(0) == 0)
    def _():
        rw = rw_ref[0]
        w_hi = rw.astype(BF16)
        wcat_ref[:, 0:LANES] = w_hi
        wcat_ref[:, LANES:2 * LANES] = (rw - w_hi.astype(F32)).astype(BF16)
        x1prev_ref[...] = jnp.zeros_like(x1prev_ref)

    hn = _rmsnorm_rows(x1prev_ref[...], g_ref[...])
    h_prev = hn.astype(BF16)
    h_lo = (hn - h_prev.astype(F32)).astype(BF16)
    r = _dot(h_prev, wcat_ref[...]) + _dot(h_lo, wcat_ref[...])
    logits = r[:, 0:LANES] + r[:, LANES:2 * LANES]
    lt = logits.T[0:n_experts, :]
    m = jnp.max(lt, axis=0, keepdims=True)
    ex = jnp.exp(lt - m)
    afft = ex / jnp.sum(ex, axis=0, keepdims=True)
    afft_ref[0] = afft
    padded = jnp.concatenate(
        [afft, jnp.zeros((LANES - n_experts, afft.shape[1]), F32)], axis=0)
    aff = padded.T
    aff_ref[...] = aff
    h_ref[...] = _pack_row_words(h_prev, _pack_gate_lanes(aff))

    y = _dot(tok_ref[...], w_ref[0:tok_width, :]) + _dot(mo_ref[...], w_ref[tok_width:, :])
    x1 = x_ref[...] + y
    x1_ref[...] = x1
    x1prev_ref[...] = x1


def _outproj(tok2d, mo2d, x2d, w_bf16, g, rw_pad, layer, n_experts, seq):
    t, d_model = x2d.shape
    tok_width = tok2d.shape[1]
    tiles_per_seq = seq // ROW_TILE
    n_tiles = t // ROW_TILE

    def cur(i):
        return jnp.minimum(i, n_tiles - 1)

    def prev(i):
        return jnp.maximum(i - 1, 0)

    return pl.pallas_call(
        functools.partial(_outproj_kernel, tok_width=tok_width, n_experts=n_experts),
        out_shape=(jax.ShapeDtypeStruct((t, d_model), F32),
                   jax.ShapeDtypeStruct((t, d_model // 2 + LANES), U32),
                   jax.ShapeDtypeStruct((t // seq, n_experts, seq), F32),
                   jax.ShapeDtypeStruct((t, LANES), F32)),
        grid=(n_tiles + 1,),
        in_specs=[
            pl.BlockSpec((ROW_TILE, tok_width), lambda i: (cur(i), 0)),
            pl.BlockSpec((ROW_TILE, mo2d.shape[1]), lambda i: (cur(i), 0)),
            pl.BlockSpec((ROW_TILE, d_model), lambda i: (cur(i), 0)),
            _resident((d_model, d_model), lambda i: (0, 0)),
            pl.BlockSpec((1, d_model), lambda i: (0, 0)),
            _resident((1, d_model, LANES), lambda i: (layer, 0, 0)),
        ],
        out_specs=(pl.BlockSpec((ROW_TILE, d_model), lambda i: (cur(i), 0)),
                   pl.BlockSpec((ROW_TILE, d_model // 2 + LANES), lambda i: (prev(i), 0)),
                   pl.BlockSpec((1, n_experts, ROW_TILE),
                                lambda i: (prev(i) // tiles_per_seq, 0, prev(i) % tiles_per_seq)),
                   pl.BlockSpec((ROW_TILE, LANES), lambda i: (prev(i), 0))),
        scratch_shapes=[pltpu.VMEM((d_model, 2 * LANES), BF16),
                        pltpu.VMEM((ROW_TILE, d_model), F32)],
        compiler_params=_params(1),
        name="outproj_router",
    )(tok2d, mo2d, x2d, w_bf16, g, rw_pad)


def _strict_triangle(n, lower):
    r = lax.broadcasted_iota(I32, (n, n), 0)
    c = lax.broadcasted_iota(I32, (n, n), 1)
    return jnp.where((c < r) if lower else (r < c), 1.0, 0.0).astype(BF16)


def _prefix_rows(mask_f32):
    s, l = mask_f32.shape
    tri = _strict_triangle(PREFIX_CHUNK, lower=True)
    carry = jnp.zeros((1, l), F32)
    out = []
    for c in range(0, s, PREFIX_CHUNK):
        m = mask_f32[c:c + PREFIX_CHUNK, :]
        out.append(_dot(tri, m.astype(BF16)) + carry)
        carry = carry + jnp.sum(m, axis=0, keepdims=True)
    return jnp.concatenate(out, axis=0)


def _prefix_lanes(mask_f32):
    e, s = mask_f32.shape
    tri = _strict_triangle(PREFIX_CHUNK, lower=False)
    carry = jnp.zeros((e, 1), F32)
    out = []
    for c in range(0, s, PREFIX_CHUNK):
        m = mask_f32[:, c:c + PREFIX_CHUNK]
        out.append(_dot(m.astype(BF16), tri) + carry)
        carry = carry + jnp.sum(m, axis=1, keepdims=True)
    return jnp.concatenate(out, axis=1)


def _select_slots(key, thr, need, prefix_fn):
    return _select_slots_and_counts(key, thr, need, prefix_fn)[0]


def _select_slots_and_counts(key, thr, need, prefix_fn):
    gt = jnp.where(key > thr, 1.0, 0.0)
    eq = jnp.where(key == thr, 1.0, 0.0)
    eq_rank = prefix_fn(eq)
    sel = gt + eq * jnp.where(eq_rank < need, 1.0, 0.0)
    pos = prefix_fn(sel)
    return jnp.where(sel > 0.5, pos, -1.0), pos


GATE_GROUP = 16
GATE_PIECES = 3


def _pack_gate_lanes(aff):
    hi = aff.astype(BF16).astype(F32)
    r1 = aff - hi
    mid = r1.astype(BF16).astype(F32)
    lo = (r1 - mid).astype(BF16).astype(F32)
    packed = hi + pltpu.roll(mid, GATE_GROUP, 1) + pltpu.roll(lo, 2 * GATE_GROUP, 1)
    return packed.astype(BF16)


def _unpack_gate(tail, e):
    lane = lax.broadcasted_iota(I32, (1, LANES), 1)
    mine = ((lane & (GATE_GROUP - 1)) == e) & (lane < GATE_PIECES * GATE_GROUP)
    return jnp.sum(jnp.where(mine, tail.astype(F32), 0.0), axis=1, keepdims=True)


def _pack_row_words(h, gate_tile):
    rows, d_model = h.shape
    half = d_model // 2
    hi = jnp.concatenate([h[:, 0:half], gate_tile], axis=1).astype(F32)
    lo = jnp.concatenate([h[:, half:], jnp.zeros((rows, LANES), BF16)], axis=1).astype(F32)
    return pltpu.bitcast(hi, U32) | (pltpu.bitcast(lo, U32) >> 16)


def _unpack_row_words(words):
    hi = pltpu.bitcast(words & jnp.uint32(0xFFFF0000), F32)
    lo = pltpu.bitcast(words << 16, F32)
    return hi, lo


def _route_kernel(afft_ref, posm_ref, thr_ref, need_ref, starts_ref, *, cap):
    n_experts, seq = afft_ref.shape[1], afft_ref.shape[2]
    key = pltpu.bitcast(afft_ref[0], I32)
    thr = jnp.zeros((n_experts, 1), I32)
    for bit in range(F32_KEY_BITS - 1, -1, -1):
        cand = thr | (1 << bit)
        cnt = jnp.sum(jnp.where(key >= cand, 1.0, 0.0), axis=1, keepdims=True)
        thr = jnp.where(cnt >= cap, cand, thr)
    n_gt = jnp.sum(jnp.where(key > thr, 1.0, 0.0), axis=1, keepdims=True)
    need = cap - n_gt
    thr_ref[0] = thr
    need_ref[0] = need
    posm, before = _select_slots_and_counts(key, thr, need, _prefix_lanes)
    posm_ref[0] = posm.astype(I32)
    starts_ref[0] = jnp.concatenate(
        [before[:, r:r + 1] for r in range(0, seq, COMBINE_ROWS)], axis=1).astype(I32)


def _route(afft, cap):
    b, n_experts, seq = afft.shape
    n_tiles = seq // COMBINE_ROWS
    rows = b * n_experts
    outs = pl.pallas_call(
        functools.partial(_route_kernel, cap=cap),
        out_shape=(jax.ShapeDtypeStruct((1, rows, seq), I32),
                   jax.ShapeDtypeStruct((1, rows, 1), I32),
                   jax.ShapeDtypeStruct((1, rows, 1), F32),
                   jax.ShapeDtypeStruct((1, rows, n_tiles), I32)),
        grid=(1,),
        in_specs=[pl.BlockSpec((1, rows, seq), lambda i: (0, 0, 0))],
        out_specs=(pl.BlockSpec((1, rows, seq), lambda i: (0, 0, 0)),
                   pl.BlockSpec((1, rows, 1), lambda i: (0, 0, 0)),
                   pl.BlockSpec((1, rows, 1), lambda i: (0, 0, 0)),
                   pl.BlockSpec((1, rows, n_tiles), lambda i: (0, 0, 0))),
        compiler_params=_params(1),
        name="expert_route",
    )(afft.reshape(1, rows, seq))
    return tuple(o.reshape(b, n_experts, o.shape[2]) for o in outs)


SC_LANES = 16
SC_GATHER_ROWS = 64


def _sc_expert_gather(posm2d, h_words, n_experts, e_offset, n_e, b, seq, cap):
    width = h_words.shape[1]
    info = plsc.get_sparse_core_info()
    n_cores, n_subcores = info.num_cores, info.num_subcores
    n_workers = n_cores * n_subcores
    assert info.num_lanes == SC_LANES and (b * n_e) % n_workers == 0
    pairs_per_worker = (b * n_e) // n_workers
    mesh = plsc.VectorSubcoreMesh(core_axis_name="c", subcore_axis_name="s")

    @functools.partial(
        pl.kernel, mesh=mesh,
        out_type=jax.ShapeDtypeStruct((n_e * b * cap, width), U32),
        compiler_params=pltpu.CompilerParams(needs_layout_passes=False),
        scratch_types=[
            pltpu.VMEM((seq,), I32),
            pltpu.VMEM((cap,), I32),
            pltpu.VMEM((SC_GATHER_ROWS, width), U32),
            pltpu.SemaphoreType.DMA,
        ],
        name="sc_expert_gather",
    )
    def gather(posm_hbm, h_hbm, out_hbm, pos_v, idx_v, rows_v, sem):
        wid = lax.axis_index("s") * n_cores + lax.axis_index("c")
        for p in range(pairs_per_worker):
            pair = wid * pairs_per_worker + p
            bi = pair // n_e
            e = pair - bi * n_e
            pltpu.sync_copy(posm_hbm.at[bi * n_experts + e_offset + e], pos_v)

            @pl.loop(0, seq, step=SC_LANES)
            def _(t0):
                slots = pos_v[pl.ds(t0, SC_LANES)]
                rows = lax.iota(I32, SC_LANES) + (t0 + bi * seq)
                plsc.store_scatter(idx_v, [slots], rows, mask=slots >= 0)

            out_base = (e * b + bi) * cap
            for c in range(cap // SC_GATHER_ROWS):
                chunk = idx_v.at[pl.ds(c * SC_GATHER_ROWS, SC_GATHER_ROWS)]
                pltpu.async_copy(h_hbm.at[chunk], rows_v, sem).wait()
                pltpu.sync_copy(rows_v, out_hbm.at[pl.ds(out_base + c * SC_GATHER_ROWS, SC_GATHER_ROWS)])

    return gather(posm2d, h_words).reshape(n_e, b, cap, width)


EXPERT_ROWS = 1024
EXPERT_FTILE = 256


EXPERT_WBUFS = 2


def _expert_kernel(xg_ref, wg_hbm, wu_hbm, wd_hbm, o_ref,
                   x_ref, g_ref, hact_ref, wg_full, wu_full, wd_full, wg_buf, wu_buf, wd_buf, sem,
                   *, layer, e_offset, n_ftiles):
    e = pl.program_id(0)
    m = pl.program_id(1)
    n_e = pl.num_programs(0)
    nb, cap, width = xg_ref.shape[1], xg_ref.shape[2], xg_ref.shape[3]
    half = width - LANES
    d_model = 2 * half
    rows = nb * cap
    tf = EXPERT_FTILE
    assert n_ftiles % EXPERT_WBUFS == 0

    def tile_copies(expert, f):
        ge = e_offset + expert
        slot = f % EXPERT_WBUFS
        return (
            pltpu.make_async_copy(wg_hbm.at[layer, ge, :, pl.ds(f * tf, tf)], wg_buf.at[slot], sem.at[0, slot]),
            pltpu.make_async_copy(wu_hbm.at[layer, ge, :, pl.ds(f * tf, tf)], wu_buf.at[slot], sem.at[1, slot]),
            pltpu.make_async_copy(wd_hbm.at[layer, ge, pl.ds(f * tf, tf), :], wd_buf.at[slot], sem.at[2, slot]),
        )

    def start(expert, f):
        for cp in tile_copies(expert, f):
            cp.start()

    @pl.when((e == 0) & (m == 0))
    def _():
        for f in range(EXPERT_WBUFS):
            start(e, f)

    for i in range(nb):
        r0 = i * cap
        hi, lo = _unpack_row_words(xg_ref[0, i])
        x_ref[r0:r0 + cap, 0:half] = hi[:, 0:half].astype(BF16)
        x_ref[r0:r0 + cap, half:d_model] = lo[:, 0:half].astype(BF16)
        g_ref[r0:r0 + cap, :] = jnp.broadcast_to(
            _unpack_gate(hi[:, half:width], e_offset + e), (cap, LANES))

    def receive(f):
        slot = f % EXPERT_WBUFS
        for cp in tile_copies(e, f):
            cp.wait()
        wg_full[:, f * tf:(f + 1) * tf] = wg_buf[slot].astype(BF16)
        wu_full[:, f * tf:(f + 1) * tf] = wu_buf[slot].astype(BF16)
        wd_full[f * tf:(f + 1) * tf, :] = wd_buf[slot].astype(BF16)
        ahead = f + EXPERT_WBUFS
        if ahead < n_ftiles:
            start(e, ahead)
        else:
            @pl.when(e + 1 < n_e)
            def _():
                start(e + 1, ahead - n_ftiles)

    def body(first_group):
        x = x_ref[...]
        for f in range(n_ftiles):
            if first_group:
                receive(f)
            a = _dot(x, wg_full[:, f * tf:(f + 1) * tf])
            u = _dot(x, wu_full[:, f * tf:(f + 1) * tf])
            hact_ref[:, f * tf:(f + 1) * tf] = (a * jax.nn.sigmoid(a) * u).astype(BF16)
        for c in range(0, d_model, COL_CHUNK):
            y = _dot(hact_ref[...], wd_full[:, c:c + COL_CHUNK])
            for j in range(0, COL_CHUNK, LANES):
                o_ref[:, 0, :, c + j:c + j + LANES] = (
                    (y[:, j:j + LANES] * g_ref[...]).astype(BF16).reshape(nb, cap, LANES))

    @pl.when(m == 0)
    def _():
        body(True)

    @pl.when(m != 0)
    def _():
        body(False)


def _experts(xg, w_gate, w_up, w_down, layer, e_offset):
    n_e, b, cap, width = xg.shape
    d_model = 2 * (width - LANES)
    d_expert = w_gate.shape[3]
    nb = EXPERT_ROWS // cap
    n_ftiles = d_expert // EXPERT_FTILE
    return pl.pallas_call(
        functools.partial(_expert_kernel, layer=layer, e_offset=e_offset, n_ftiles=n_ftiles),
        out_shape=jax.ShapeDtypeStruct((b, n_e, cap, d_model), BF16),
        grid=(n_e, b // nb),
        in_specs=[
            pl.BlockSpec((1, nb, cap, width), lambda e, m: (e, m, 0, 0)),
            pl.BlockSpec(memory_space=pl.ANY),
            pl.BlockSpec(memory_space=pl.ANY),
            pl.BlockSpec(memory_space=pl.ANY),
        ],
        out_specs=pl.BlockSpec((nb, 1, cap, d_model), lambda e, m: (m, e, 0, 0)),
        scratch_shapes=[pltpu.VMEM((EXPERT_ROWS, d_model), BF16),
                        pltpu.VMEM((EXPERT_ROWS, LANES), F32),
                        pltpu.VMEM((EXPERT_ROWS, d_expert), BF16),
                        pltpu.VMEM((d_model, d_expert), BF16),
                        pltpu.VMEM((d_model, d_expert), BF16),
                        pltpu.VMEM((d_expert, d_model), BF16),
                        pltpu.VMEM((EXPERT_WBUFS, d_model, EXPERT_FTILE), F32),
                        pltpu.VMEM((EXPERT_WBUFS, d_model, EXPERT_FTILE), F32),
                        pltpu.VMEM((EXPERT_WBUFS, EXPERT_FTILE, d_model), F32),
                        pltpu.SemaphoreType.DMA((3, EXPERT_WBUFS))],
        compiler_params=_params(2),
        name="experts",
    )(xg, w_gate, w_up, w_down)


COMBINE_ROWS = 256
COMBINE_WINDOW = 64
MXU_DEPTH = 256
BF16_ROWS = 16


def _combine_kernel(starts_ref, aff_ref, thr_ref, need_ref, x1_ref, *rest, cap, n_experts, final_norm):
    n_y = len(rest) - (6 if final_norm else 5)
    y_hbm = rest[:n_y]
    g_ref = rest[n_y] if final_norm else None
    o_ref, post_ref, pfull_ref, ybuf, sem = rest[-5:]
    b = pl.program_id(0)
    t = pl.program_id(1)
    n_b = pl.num_programs(0)
    n_t = pl.num_programs(1)
    rows, win = COMBINE_ROWS, COMBINE_WINDOW
    experts_per_group = n_experts // n_y
    group_rows = ybuf.shape[0] // (2 * n_y)

    def ybuf_row(buf_slot, g):
        return pl.multiple_of((buf_slot * n_y + g) * group_rows, BF16_ROWS)

    n_chunks = sem.shape[1]
    chunks_per_group = n_chunks // n_y
    chunk_rows = group_rows // chunks_per_group
    slot = b % 2

    def chunk_copy(seq_idx, c, dst_slot):
        g, r = c // chunks_per_group, (c % chunks_per_group) * chunk_rows
        return pltpu.make_async_copy(y_hbm[g].at[seq_idx, pl.ds(r, chunk_rows), :],
                                     ybuf.at[pl.ds(ybuf_row(dst_slot, g) + r, chunk_rows), :],
                                     sem.at[dst_slot, c])

    @pl.when((b == 0) & (t == 0))
    def _():
        for c in range(n_chunks):
            chunk_copy(0, c, 0).start()

    @pl.when(t == 0)
    def _():
        for c in range(n_chunks):
            chunk_copy(b, c, slot).wait()
        key = pltpu.bitcast(aff_ref[...], I32)
        post_ref[...] = _select_slots(key, thr_ref[0], need_ref[0], _prefix_rows)

    for c in range(n_chunks):
        @pl.when((t == c) & (b + 1 < n_b))
        def _():
            chunk_copy(b + 1, c, 1 - slot).start()

    posm = post_ref[pl.ds(pl.multiple_of(t * rows, rows), rows), :]
    base = (b * (n_t + 1) + t) * n_experts
    wstart, ok = [], None
    for e in range(n_experts):
        first = starts_ref[base + e]
        end = starts_ref[base + n_experts + e]
        w0 = jnp.minimum((first // BF16_ROWS) * BF16_ROWS, cap - win)
        fits = end - w0 <= win
        wstart.append(w0)
        ok = fits if ok is None else jnp.logical_and(ok, fits)

    def finish(acc):
        if final_norm:
            acc = _rmsnorm_rows(acc, g_ref[...])
        o_ref[0] = acc

    @pl.when(ok)
    def _():
        per_dot, per_tile = MXU_DEPTH // win, LANES // win
        lane = lax.broadcasted_iota(I32, (1, LANES), 1)
        lane_f = lane.astype(F32)
        acc = x1_ref[0]
        for e0 in range(0, n_experts, per_dot):
            onehots, windows = [], []
            for e1 in range(e0, e0 + per_dot, per_tile):
                rel = None
                for k in range(per_tile - 1, -1, -1):
                    e = e1 + k
                    shifted = posm[:, e:e + 1] - (wstart[e] - k * win).astype(F32)
                    rel = shifted if rel is None else jnp.where(lane < (k + 1) * win, shifted, rel)
                onehots.append(jnp.where(rel == lane_f, 1.0, 0.0).astype(BF16))
            for e in range(e0, e0 + per_dot):
                g, el = divmod(e, experts_per_group)
                r0 = pl.multiple_of(ybuf_row(slot, g) + el * cap + wstart[e], BF16_ROWS)
                windows.append(ybuf[pl.ds(r0, win), :])
            acc = acc + _dot(jnp.concatenate(onehots, axis=1), jnp.concatenate(windows, axis=0))
        finish(acc)

    @pl.when(jnp.logical_not(ok))
    def _():
        slot_ids = lax.broadcasted_iota(I32, (1, cap), 1).astype(F32)
        for e in range(n_experts):
            pfull_ref[:, e * cap:(e + 1) * cap] = jnp.where(
                posm[:, e:e + 1] == slot_ids, 1.0, 0.0).astype(BF16)
        acc = x1_ref[0]
        for g in range(n_y):
            acc = acc + _dot(pfull_ref[:, g * group_rows:(g + 1) * group_rows],
                             ybuf[pl.ds(ybuf_row(slot, g), group_rows), :])
        finish(acc)


def _combine(starts, aff2d, thr_row, need_row, x1_3d, y_groups, cap, n_experts, final_g=None):
    b, seq, d_model = x1_3d.shape
    assert sum(y.shape[1] for y in y_groups) == n_experts * cap and cap >= COMBINE_WINDOW
    final_norm = final_g is not None
    n_tiles = seq // COMBINE_ROWS
    group_rows = y_groups[0].shape[1]
    assert all(y.shape[1] == group_rows for y in y_groups)
    assert n_tiles % len(y_groups) == 0 and group_rows % (n_tiles // len(y_groups)) == 0
    y_specs = [pl.BlockSpec(memory_space=pl.ANY) for _ in y_groups]
    g_specs = [pl.BlockSpec((1, d_model), lambda i, t, s: (0, 0))] if final_norm else []
    g_args = [final_g] if final_norm else []
    return pl.pallas_call(
        functools.partial(_combine_kernel, cap=cap, n_experts=n_experts, final_norm=final_norm),
        out_shape=jax.ShapeDtypeStruct((b, seq, d_model), F32),
        grid_spec=pltpu.PrefetchScalarGridSpec(
            num_scalar_prefetch=1,
            grid=(b, n_tiles),
            in_specs=[
                pl.BlockSpec((seq, LANES), lambda i, t, s: (i, 0)),
                pl.BlockSpec((1, 1, LANES), lambda i, t, s: (i, 0, 0)),
                pl.BlockSpec((1, 1, LANES), lambda i, t, s: (i, 0, 0)),
                pl.BlockSpec((1, COMBINE_ROWS, d_model), lambda i, t, s: (i, t, 0)),
            ] + y_specs + g_specs,
            out_specs=pl.BlockSpec((1, COMBINE_ROWS, d_model), lambda i, t, s: (i, t, 0)),
            scratch_shapes=[pltpu.VMEM((seq, LANES), F32),
                            pltpu.VMEM((COMBINE_ROWS, n_experts * cap), BF16),
                            pltpu.VMEM((2 * len(y_groups) * group_rows, d_model), BF16),
                            pltpu.SemaphoreType.DMA((2, n_tiles))],
        ),
        compiler_params=_params(2),
        name="combine",
    )(starts, aff2d, thr_row, need_row, x1_3d, *y_groups, *g_args)


EXPERT_GROUPS = 2


def _moe(afft, aff2d, h_words, x1_2d, w_gate, w_up, w_down, layer, b, seq, final_g=None):
    n_experts = afft.shape[1]
    assert n_experts <= GATE_GROUP and n_experts % EXPERT_GROUPS == 0
    d_model = x1_2d.shape[1]
    cap = CAPACITY_FACTOR * seq // n_experts
    n_e = n_experts // EXPERT_GROUPS
    posm, thr, need, tile_starts = _route(afft, cap)
    starts = jnp.concatenate([jnp.swapaxes(tile_starts, 1, 2),
                              jnp.full((b, 1, n_experts), cap, I32)], axis=1).reshape(-1)
    posm2d = posm.reshape(b * n_experts, seq)
    xgs = [_sc_expert_gather(posm2d, h_words, n_experts, g * n_e, n_e, b, seq, cap)
           for g in range(EXPERT_GROUPS)]
    ys = [_experts(xg, w_gate, w_up, w_down, layer, g * n_e).reshape(b, n_e * cap, d_model)
          for g, xg in enumerate(xgs)]
    pad = LANES - n_experts
    thr_row = jnp.pad(thr.reshape(b, 1, n_experts), ((0, 0), (0, 0), (0, pad)),
                      constant_values=np.iinfo(np.int32).max)
    need_row = jnp.pad(need.reshape(b, 1, n_experts), ((0, 0), (0, 0), (0, pad)))
    return _combine(starts, aff2d, thr_row, need_row, x1_2d.reshape(b, seq, d_model), ys, cap,
                    n_experts, final_g)


def _rotary_tile(t, cos, sin_lo, sin_hi):
    half = ROT_DIM // 2
    return t * cos + pltpu.roll(t, LANES - half, 1) * sin_lo + pltpu.roll(t, half, 1) * sin_hi


def _inproj_attn_kernel(x_ref, g_ref, w_ref, pos_ref, rot_ref,
                        q_ref, k_ref, v_ref, qm_ref, *, tok_width, kv_width):
    qscale = HEAD_DIM ** -0.5
    rows = x_ref.shape[0] // INPROJ_PARTS

    def prepare(part):
        r = slice(part * rows, (part + 1) * rows)
        hn = _rmsnorm_rows(x_ref[r, :], g_ref[...]).astype(BF16)
        ang = pos_ref[r, :].astype(F32) * rot_ref[0:1, :]
        cos = jnp.cos(ang)
        sin = jnp.sin(ang)
        return hn, cos, sin * rot_ref[1:2, :], sin * rot_ref[2:3, :]

    def project(part, hn, cos, sin_lo, sin_hi):
        r = slice(part * rows, (part + 1) * rows)
        for c in range(0, tok_width, COL_CHUNK):
            pc = _dot(hn, w_ref[:, c:c + COL_CHUNK])
            for j in range(0, COL_CHUNK, LANES):
                rot = _rotary_tile(pc[:, j:j + LANES], cos, sin_lo, sin_hi)
                q_ref[r, c + j:c + j + LANES] = (rot * qscale).astype(BF16)
        kv = _dot(hn, w_ref[:, tok_width:tok_width + 2 * kv_width])
        k01 = _rotary_tile(kv[:, 0:LANES], cos, sin_lo, sin_hi)
        k2x = _rotary_tile(kv[:, LANES:2 * LANES], cos, sin_lo, sin_hi)
        k_ref[0, 0, r, :] = k01[:, 0:HEAD_DIM].astype(BF16)
        k_ref[0, 1, r, :] = k01[:, HEAD_DIM:LANES].astype(BF16)
        k_ref[0, 2, r, :] = k2x[:, 0:HEAD_DIM].astype(BF16)
        for hh in range(kv_width // HEAD_DIM):
            lo = kv_width + hh * HEAD_DIM
            v_ref[0, hh, r, :] = kv[:, lo:lo + HEAD_DIM].astype(BF16)
        qm_ref[r, :] = _dot(hn, w_ref[:, tok_width + 2 * kv_width:]).astype(BF16)

    prepared = [prepare(part) for part in range(INPROJ_PARTS)]
    for part in range(INPROJ_PARTS):
        project(part, *prepared[part])


def _inproj_attn(x2d, g, w_bf16, pos2d, rot_rows, tok_width, kv_width, seq):
    t, d_model = x2d.shape
    n = w_bf16.shape[1]
    n_kv = kv_width // HEAD_DIM
    assert n_kv == 3 and kv_width + HEAD_DIM == 2 * LANES
    tiles_per_seq = seq // INPROJ_ROWS
    kv_spec = pl.BlockSpec((1, n_kv, INPROJ_ROWS, HEAD_DIM),
                           lambda i: (i // tiles_per_seq, 0, i % tiles_per_seq, 0))
    return pl.pallas_call(
        functools.partial(_inproj_attn_kernel, tok_width=tok_width, kv_width=kv_width),
        out_shape=(jax.ShapeDtypeStruct((t, tok_width), BF16),
                   jax.ShapeDtypeStruct((t // seq, n_kv, seq, HEAD_DIM), BF16),
                   jax.ShapeDtypeStruct((t // seq, n_kv, seq, HEAD_DIM), BF16),
                   jax.ShapeDtypeStruct((t, n - tok_width - 2 * kv_width), BF16)),
        grid=(t // INPROJ_ROWS,),
        in_specs=[
            pl.BlockSpec((INPROJ_ROWS, d_model), lambda i: (i, 0)),
            pl.BlockSpec((1, d_model), lambda i: (0, 0)),
            _resident((d_model, n), lambda i: (0, 0)),
            pl.BlockSpec((INPROJ_ROWS, 1), lambda i: (i, 0)),
            pl.BlockSpec((8, LANES), lambda i: (0, 0)),
        ],
        out_specs=(pl.BlockSpec((INPROJ_ROWS, tok_width), lambda i: (i, 0)),
                   kv_spec, kv_spec,
                   pl.BlockSpec((INPROJ_ROWS, n - tok_width - 2 * kv_width), lambda i: (i, 0))),
        compiler_params=_params(1),
        name="inproj_attn",
    )(x2d, g, w_bf16, pos2d, rot_rows)


WATTN_QBLOCKS = 4


def _wattn_kernel(sink_ref, q_ref, kp_ref, kc_ref, kn_ref, vp_ref, vc_ref, vn_ref, o_ref,
                  valid_ref, kpad_ref, vpad_ref, s_ref, p_ref, inv_ref, *, seq):
    step = pl.program_id(1)
    n_kv = kc_ref.shape[1]
    pairs = GQA_RATIO // 2
    half_rows = pairs * BLOCK
    key_rows = (WATTN_QBLOCKS + 2) * BLOCK
    zeros = jnp.zeros((key_rows, HEAD_DIM), BF16)
    ones_col = jnp.where(lax.broadcasted_iota(I32, (key_rows, HEAD_DIM), 1) == 0, 1.0, 0.0).astype(BF16)
    low_half = lax.broadcasted_iota(I32, (1, LANES), 1) < HEAD_DIM
    for hk in range(n_kv):
        kw = jnp.concatenate([kp_ref[0, hk], kc_ref[0, hk], kn_ref[0, hk]], axis=0)
        vw = jnp.concatenate([vp_ref[0, hk], vc_ref[0, hk], vn_ref[0, hk]], axis=0)
        kpad_ref[2 * hk] = jnp.concatenate([kw, zeros], axis=1)
        kpad_ref[2 * hk + 1] = jnp.concatenate([zeros, kw], axis=1)
        vpad_ref[2 * hk] = jnp.concatenate([vw, ones_col], axis=1)
        vpad_ref[2 * hk + 1] = jnp.concatenate([ones_col, vw], axis=1)

    qi = lax.broadcasted_iota(I32, (BLOCK, 3 * BLOCK), 0)
    kj = lax.broadcasted_iota(I32, (BLOCK, 3 * BLOCK), 1)
    for qb in range(WATTN_QBLOCKS):
        n = step * WATTN_QBLOCKS + qb
        k0 = qb * BLOCK
        first = jnp.maximum(qi, BLOCK - n * BLOCK)
        last = jnp.minimum(qi + 2 * WINDOW, seq + BLOCK - 1 - n * BLOCK)
        valid_ref[qb] = jnp.where(((kj - first) | (last - kj)) >= 0, 1.0, 0.0)
        q0 = qb * BLOCK
        for hk in range(n_kv):
            tile0 = hk * pairs
            qs = jnp.concatenate(
                [q_ref[0, q0:q0 + BLOCK, (tile0 + j) * LANES:(tile0 + j + 1) * LANES]
                 for j in range(pairs)], axis=0)
            s_ref[qb, hk, 0:half_rows, :] = _dot_nt(qs, kpad_ref[2 * hk, k0:k0 + 3 * BLOCK, :])
            s_ref[qb, hk, half_rows:2 * half_rows, :] = _dot_nt(
                qs, kpad_ref[2 * hk + 1, k0:k0 + 3 * BLOCK, :])
        for hk in range(n_kv):
            for c in range(GQA_RATIO):
                j, odd = c % pairs, c // pairs
                r = c * BLOCK
                s = jnp.concatenate([
                    jnp.where(valid_ref[qb, :, 0:BLOCK] > 0.5, s_ref[qb, hk, r:r + BLOCK, 0:BLOCK], NEG_INF),
                    s_ref[qb, hk, r:r + BLOCK, BLOCK:2 * BLOCK],
                    jnp.where(valid_ref[qb, :, 2 * BLOCK:] > 0.5,
                              s_ref[qb, hk, r:r + BLOCK, 2 * BLOCK:], NEG_INF),
                ], axis=1)
                sk = sink_ref[hk * GQA_RATIO + 2 * j + odd]
                m = jnp.maximum(jnp.max(s, axis=-1, keepdims=True), sk)
                p_ref[qb, hk, r:r + BLOCK, :] = jnp.exp(s - m).astype(BF16)
                inv_ref[qb, hk, j * BLOCK:(j + 1) * BLOCK, odd * HEAD_DIM:(odd + 1) * HEAD_DIM] = (
                    jnp.broadcast_to(jnp.exp(sk - m), (BLOCK, HEAD_DIM)))
        for hk in range(n_kv):
            pv_even = _dot(p_ref[qb, hk, 0:half_rows, :], vpad_ref[2 * hk, k0:k0 + 3 * BLOCK, :])
            pv_odd = _dot(p_ref[qb, hk, half_rows:2 * half_rows, :],
                          vpad_ref[2 * hk + 1, k0:k0 + 3 * BLOCK, :])
            den = (jnp.where(low_half, pv_even[:, HEAD_DIM:HEAD_DIM + 1], pv_odd[:, 0:1])
                   + inv_ref[qb, hk])
            o = jnp.where(low_half, pv_even, pv_odd) / den
            for j in range(pairs):
                lo = (hk * pairs + j) * LANES
                o_ref[0, q0:q0 + BLOCK, lo:lo + LANES] = o[j * BLOCK:(j + 1) * BLOCK].astype(BF16)


def _window_attention(sink, q3d, k4d, v4d):
    b, seq, tok_width = q3d.shape
    n_kv = k4d.shape[1]
    nb = seq // BLOCK
    qb = WATTN_QBLOCKS
    assert nb % qb == 0
    edge_block = (1, n_kv, BLOCK, HEAD_DIM)
    prev_spec = pl.BlockSpec(edge_block, lambda i, s: (i, 0, jnp.maximum(s * qb - 1, 0), 0))
    cur_spec = pl.BlockSpec((1, n_kv, qb * BLOCK, HEAD_DIM), lambda i, s: (i, 0, s, 0))
    next_spec = pl.BlockSpec(edge_block, lambda i, s: (i, 0, jnp.minimum(s * qb + qb, nb - 1), 0))
    key_rows = (qb + 2) * BLOCK
    return pl.pallas_call(
        functools.partial(_wattn_kernel, seq=seq),
        out_shape=jax.ShapeDtypeStruct((b, seq, tok_width), BF16),
        grid=(b, nb // qb),
        in_specs=[
            pl.BlockSpec(memory_space=pltpu.SMEM),
            pl.BlockSpec((1, qb * BLOCK, tok_width), lambda i, s: (i, s, 0)),
            prev_spec, cur_spec, next_spec, prev_spec, cur_spec, next_spec,
        ],
        out_specs=pl.BlockSpec((1, qb * BLOCK, tok_width), lambda i, s: (i, s, 0)),
        scratch_shapes=[pltpu.VMEM((qb, BLOCK, 3 * BLOCK), F32),
                        pltpu.VMEM((2 * n_kv, key_rows, LANES), BF16),
                        pltpu.VMEM((2 * n_kv, key_rows, LANES), BF16),
                        pltpu.VMEM((qb, n_kv, GQA_RATIO * BLOCK, 3 * BLOCK), F32),
                        pltpu.VMEM((qb, n_kv, GQA_RATIO * BLOCK, 3 * BLOCK), BF16),
                        pltpu.VMEM((qb, n_kv, GQA_RATIO // 2 * BLOCK, LANES), F32)],
        compiler_params=_params(2),
        name="window_attention",
    )(sink, q3d, k4d, k4d, k4d, v4d, v4d, v4d)


def _rotary_rows(dtype=F32):
    half = ROT_DIM // 2
    inv_freq = ROPE_THETA ** (-jnp.arange(0, ROT_DIM, 2, dtype=jnp.float32) / ROT_DIM)
    lane = np.arange(LANES) % HEAD_DIM
    rotated = lane < ROT_DIM
    freq = jnp.where(jnp.asarray(rotated), inv_freq[jnp.asarray(lane % half)], 0.0)
    rows = jnp.zeros((8, LANES), dtype)
    rows = rows.at[0].set(freq)
    rows = rows.at[1].set(jnp.asarray(np.where(lane < half, -1.0, 0.0), dtype))
    rows = rows.at[2].set(jnp.asarray(np.where(rotated & (lane >= half), 1.0, 0.0), dtype))
    return rows


def kernel(x, mem, positions, norm_mix_g, norm_ffn_g, mem_norm_g, final_g, mem_w_kv,
           pool_w_in, pool_group_w, pool_scale, pool_w_out,
           attn_w_in, attn_sink, attn_w_out,
           router_w, exp_w_gate, exp_w_up, exp_w_down):
    b, seq, d_model = x.shape
    depth = norm_mix_g.shape[0]
    t = b * seq
    n_experts = router_w.shape[2]
    tok_width = pool_scale.shape[1]
    n_groups = pool_group_w.shape[1]
    kv_width = (attn_w_in.shape[2] - tok_width - XA_WIDTH) // 2
    assert seq % ROW_TILE == 0 and seq % INPROJ_ROWS == 0 and mem.shape[1] == MEM_LEN

    memkv = _memkv(mem.reshape(b * MEM_LEN, d_model), mem_norm_g.reshape(1, d_model),
                   mem_w_kv.astype(BF16))
    memkv = memkv.reshape(depth, b, MEM_LEN, 2 * XA_WIDTH)
    rw_pad = jnp.pad(router_w, ((0, 0), (0, 0), (0, LANES - n_experts)))
    pos2d = positions.reshape(t, 1)
    rot_rows = _rotary_rows()

    x2d = x.reshape(t, d_model)
    for layer in range(depth):
        j = layer // 2
        g_mix = norm_mix_g[layer].reshape(1, d_model)
        if layer % 2 == 0:
            u, qm = _inproj_pool(x2d, g_mix, pool_w_in[j].astype(BF16), tok_width)
            tok = _pool_mixer(u.reshape(b, seq, tok_width), pool_group_w[j].astype(BF16),
                              pool_scale[j].reshape(n_groups, 1, tok_width // n_groups))
            tok = tok.reshape(t, tok_width)
            w_out = pool_w_out[j]
        else:
            q, k, v, qm = _inproj_attn(x2d, g_mix, attn_w_in[j].astype(BF16), pos2d, rot_rows,
                                       tok_width, kv_width, seq)
            tok = _window_attention(attn_sink[j], q.reshape(b, seq, tok_width), k, v)
            tok = tok.reshape(t, tok_width)
            w_out = attn_w_out[j]
        mo = _mem_xattn(qm, memkv, layer, seq)
        x1, h, afft, aff = _outproj(tok, mo, x2d, w_out.astype(BF16),
                                    norm_ffn_g[layer].reshape(1, d_model), rw_pad, layer,
                                    n_experts, seq)
        last = layer == depth - 1
        x2 = _moe(afft, aff, h, x1, exp_w_gate, exp_w_up, exp_w_down, layer, b, seq,
                  final_g.reshape(1, d_model) if last else None)
        x2d = x2.reshape(t, d_model)
    return x2d.reshape(b, seq, d_model)
```

```python
import functools

import jax
import jax.numpy as jnp
import numpy as np
from jax import lax
from jax.experimental import pallas as pl
from jax.experimental.pallas import tpu as pltpu
from jax.experimental.pallas import tpu_sc as plsc

F32 = jnp.float32
BF16 = jnp.bfloat16
I32 = jnp.int32
U32 = jnp.uint32

EPS = 1e-6
MEM_LEN = 256
XA_HEADS = 4
XA_HEAD_DIM = 128
XA_WIDTH = XA_HEADS * XA_HEAD_DIM
POOL_WINDOWS = (2, 4, 8, 16)
HEAD_DIM = 64
GQA_RATIO = 8
WINDOW = 128
BLOCK = 128
ROPE_THETA = 500000.0
ROT_DIM = 16
NEG_INF = -1e30
CAPACITY_FACTOR = 2

LANES = 128
SUBLANES = 8
MIB = 1024 * 1024
VMEM_LIMIT_BYTES = 56 * MIB

ROW_TILE = 512
INPROJ_ROWS = 1024
INPROJ_PARTS = 4
POOL_PAD = 16
PREFIX_CHUNK = 256
COL_CHUNK = 512
F32_KEY_BITS = 31


def _params(n_grid_dims):
    return pltpu.CompilerParams(
        dimension_semantics=("arbitrary",) * n_grid_dims,
        vmem_limit_bytes=VMEM_LIMIT_BYTES,
    )


def _resident(block_shape, index_map):
    return pl.BlockSpec(block_shape, index_map, pipeline_mode=pl.Buffered(1))


def _rmsnorm_rows(x, g):
    return x * lax.rsqrt(jnp.mean(x * x, axis=-1, keepdims=True) + EPS) * g


def _dot(a, b):
    return jnp.dot(a, b, preferred_element_type=F32)


def _dot_nt(a, b):
    return lax.dot_general(a, b, (((1,), (1,)), ((), ())), preferred_element_type=F32)


def _memkv_kernel(mem_ref, g_ref, w_ref, o_ref):
    hn = _rmsnorm_rows(mem_ref[...], g_ref[...]).astype(BF16)
    o_ref[0] = _dot(hn, w_ref[0]).astype(BF16)


def _memkv(mem2d, g, w_bf16):
    depth, d_model, n = w_bf16.shape
    rows = mem2d.shape[0]
    return pl.pallas_call(
        _memkv_kernel,
        out_shape=jax.ShapeDtypeStruct((depth, rows, n), BF16),
        grid=(depth, rows // ROW_TILE),
        in_specs=[
            pl.BlockSpec((ROW_TILE, d_model), lambda l, i: (i, 0)),
            pl.BlockSpec((1, d_model), lambda l, i: (0, 0)),
            pl.BlockSpec((1, d_model, n), lambda l, i: (l, 0, 0)),
        ],
        out_specs=pl.BlockSpec((1, ROW_TILE, n), lambda l, i: (l, i, 0)),
        compiler_params=_params(2),
        name="memkv",
    )(mem2d, g, w_bf16)


def _inproj_pool_kernel(x_ref, g_ref, w_ref, u_ref, qm_ref, *, tok_width):
    hn = _rmsnorm_rows(x_ref[...], g_ref[...]).astype(BF16)
    for c in range(0, tok_width, COL_CHUNK):
        u_ref[:, c:c + COL_CHUNK] = _dot(hn, w_ref[:, c:c + COL_CHUNK])
    qm_ref[...] = _dot(hn, w_ref[:, tok_width:]).astype(BF16)


def _inproj_pool(x2d, g, w_bf16, tok_width):
    t, d_model = x2d.shape
    n = w_bf16.shape[1]
    return pl.pallas_call(
        functools.partial(_inproj_pool_kernel, tok_width=tok_width),
        out_shape=(jax.ShapeDtypeStruct((t, tok_width), F32),
                   jax.ShapeDtypeStruct((t, n - tok_width), BF16)),
        grid=(t // INPROJ_ROWS,),
        in_specs=[
            pl.BlockSpec((INPROJ_ROWS, d_model), lambda i: (i, 0)),
            pl.BlockSpec((1, d_model), lambda i: (0, 0)),
            _resident((d_model, n), lambda i: (0, 0)),
        ],
        out_specs=(pl.BlockSpec((INPROJ_ROWS, tok_width), lambda i: (i, 0)),
                   pl.BlockSpec((INPROJ_ROWS, n - tok_width), lambda i: (i, 0))),
        compiler_params=_params(1),
        name="inproj_pool",
    )(x2d, g, w_bf16)


def _pool_group(u_ref, gw_ref, sc_ref, o_ref, *, window, seq):
    gwid = u_ref.shape[2]
    rows = seq + 2 * POOL_PAD
    half = window // 2
    assert 2 * half <= POOL_PAD
    zeros_pad = jnp.zeros((POOL_PAD, gwid), F32)
    u = u_ref[0]
    p = jnp.concatenate([zeros_pad, u, zeros_pad], axis=0)
    k = 1
    while k < half:
        p = p + pltpu.roll(p, rows - k, 0)
        k *= 2
    before = p if half % SUBLANES == 0 else pltpu.roll(p, half, 0)
    shift = half if half % SUBLANES == 0 else 0
    win = before[POOL_PAD - shift:POOL_PAD - shift + seq, :] + p[POOL_PAD:POOL_PAD + seq, :]
    t = lax.broadcasted_iota(I32, (seq, 1), 0)
    lo = jnp.maximum(t - half, 0)
    hi = jnp.minimum(t + half - 1, seq - 1)
    cnt = (hi - lo + 1).astype(F32)
    pooled = (win / cnt - u).astype(BF16)
    o_ref[0] = (_dot(pooled, gw_ref[0]) * sc_ref[0]).astype(BF16)


def _pool_kernel(u_ref, gw_ref, sc_ref, o_ref, *, seq):
    g = pl.program_id(1)
    for k, window in enumerate(POOL_WINDOWS):
        @pl.when(g == k)
        def _():
            _pool_group(u_ref, gw_ref, sc_ref, o_ref, window=window, seq=seq)


def _pool_mixer(u3d, gw_bf16, scale3d):
    b, seq, tok_width = u3d.shape
    n_groups, gwid, _ = gw_bf16.shape
    assert n_groups == len(POOL_WINDOWS) and n_groups * gwid == tok_width
    return pl.pallas_call(
        functools.partial(_pool_kernel, seq=seq),
        out_shape=jax.ShapeDtypeStruct((b, seq, tok_width), BF16),
        grid=(b, n_groups),
        in_specs=[
            pl.BlockSpec((1, seq, gwid), lambda i, g: (i, 0, g)),
            pl.BlockSpec((1, gwid, gwid), lambda i, g: (g, 0, 0)),
            pl.BlockSpec((1, 1, gwid), lambda i, g: (g, 0, 0)),
        ],
        out_specs=pl.BlockSpec((1, seq, gwid), lambda i, g: (i, 0, g)),
        compiler_params=_params(2),
        name="pool_mixer",
    )(u3d, gw_bf16, scale3d)


def _xattn_kernel(q_ref, kv_ref, o_ref):
    scale = XA_HEAD_DIM ** -0.5
    for h in range(XA_HEADS):
        lo = h * XA_HEAD_DIM
        q = q_ref[:, lo:lo + XA_HEAD_DIM]
        k = kv_ref[0, 0, :, lo:lo + XA_HEAD_DIM]
        v = kv_ref[0, 0, :, XA_WIDTH + lo:XA_WIDTH + lo + XA_HEAD_DIM]
        s = _dot_nt(q, k) * scale
        m = jnp.max(s, axis=-1, keepdims=True)
        p = jnp.exp(s - m)
        den = jnp.sum(p, axis=-1, keepdims=True)
        o_ref[:, lo:lo + XA_HEAD_DIM] = (_dot(p.astype(BF16), v) / den).astype(BF16)


XATTN_ROWS = 1024


def _mem_xattn(qm2d, memkv, layer, seq):
    t = qm2d.shape[0]
    tiles_per_seq = seq // XATTN_ROWS
    return pl.pallas_call(
        _xattn_kernel,
        out_shape=jax.ShapeDtypeStruct((t, XA_WIDTH), BF16),
        grid=(t // XATTN_ROWS,),
        in_specs=[
            pl.BlockSpec((XATTN_ROWS, XA_WIDTH), lambda i: (i, 0)),
            pl.BlockSpec((1, 1, MEM_LEN, 2 * XA_WIDTH), lambda i: (layer, i // tiles_per_seq, 0, 0)),
        ],
        out_specs=pl.BlockSpec((XATTN_ROWS, XA_WIDTH), lambda i: (i, 0)),
        compiler_params=_params(1),
        name="mem_xattn",
    )(qm2d, memkv)


def _outproj_kernel(tok_ref, mo_ref, x_ref, w_ref, g_ref, rw_ref,
                    x1_ref, h_ref, afft_ref, aff_ref, wcat_ref, x1prev_ref,
                    *, tok_width, n_experts):
    @pl.when(pl.program_id(0) == 0)
    def _():
        rw = rw_ref[0]
        w_hi = rw.astype(BF16)
        wcat_ref[:, 0:LANES] = w_hi
        wcat_ref[:, LANES:2 * LANES] = (rw - w_hi.astype(F32)).astype(BF16)
        x1prev_ref[...] = jnp.zeros_like(x1prev_ref)

    hn = _rmsnorm_rows(x1prev_ref[...], g_ref[...])
    h_prev = hn.astype(BF16)
    h_lo = (hn - h_prev.astype(F32)).astype(BF16)
    r = _dot(h_prev, wcat_ref[...]) + _dot(h_lo, wcat_ref[...])
    logits = r[:, 0:LANES] + r[:, LANES:2 * LANES]
    lt = logits.T[0:n_experts, :]
    m = jnp.max(lt, axis=0, keepdims=True)
    ex = jnp.exp(lt - m)
    afft = ex / jnp.sum(ex, axis=0, keepdims=True)
    afft_ref[0] = afft
    padded = jnp.concatenate(
        [afft, jnp.zeros((LANES - n_experts, afft.shape[1]), F32)], axis=0)
    aff = padded.T
    aff_ref[...] = aff
    h_ref[...] = _pack_row_words(h_prev, _pack_gate_lanes(aff))

    y = _dot(tok_ref[...], w_ref[0:tok_width, :]) + _dot(mo_ref[...], w_ref[tok_width:, :])
    x1 = x_ref[...] + y
    x1_ref[...] = x1
    x1prev_ref[...] = x1


def _outproj(tok2d, mo2d, x2d, w_bf16, g, rw_pad, layer, n_experts, seq):
    t, d_model = x2d.shape
    tok_width = tok2d.shape[1]
    tiles_per_seq = seq // ROW_TILE
    n_tiles = t // ROW_TILE

    def cur(i):
        return jnp.minimum(i, n_tiles - 1)

    def prev(i):
        return jnp.maximum(i - 1, 0)

    return pl.pallas_call(
        functools.partial(_outproj_kernel, tok_width=tok_width, n_experts=n_experts),
        out_shape=(jax.ShapeDtypeStruct((t, d_model), F32),
                   jax.ShapeDtypeStruct((t, d_model // 2 + LANES), U32),
                   jax.ShapeDtypeStruct((t // seq, n_experts, seq), F32),
                   jax.ShapeDtypeStruct((t, LANES), F32)),
        grid=(n_tiles + 1,),
        in_specs=[
            pl.BlockSpec((ROW_TILE, tok_width), lambda i: (cur(i), 0)),
            pl.BlockSpec((ROW_TILE, mo2d.shape[1]), lambda i: (cur(i), 0)),
            pl.BlockSpec((ROW_TILE, d_model), lambda i: (cur(i), 0)),
            _resident((d_model, d_model), lambda i: (0, 0)),
            pl.BlockSpec((1, d_model), lambda i: (0, 0)),
            _resident((1, d_model, LANES), lambda i: (layer, 0, 0)),
        ],
        out_specs=(pl.BlockSpec((ROW_TILE, d_model), lambda i: (cur(i), 0)),
                   pl.BlockSpec((ROW_TILE, d_model // 2 + LANES), lambda i: (prev(i), 0)),
                   pl.BlockSpec((1, n_experts, ROW_TILE),
                                lambda i: (prev(i) // tiles_per_seq, 0, prev(i) % tiles_per_seq)),
                   pl.BlockSpec((ROW_TILE, LANES), lambda i: (prev(i), 0))),
        scratch_shapes=[pltpu.VMEM((d_model, 2 * LANES), BF16),
                        pltpu.VMEM((ROW_TILE, d_model), F32)],
        compiler_params=_params(1),
        name="outproj_router",
    )(tok2d, mo2d, x2d, w_bf16, g, rw_pad)


def _strict_triangle(n, lower):
    r = lax.broadcasted_iota(I32, (n, n), 0)
    c = lax.broadcasted_iota(I32, (n, n), 1)
    return jnp.where((c < r) if lower else (r < c), 1.0, 0.0).astype(BF16)


def _prefix_rows(mask_f32):
    s, l = mask_f32.shape
    tri = _strict_triangle(PREFIX_CHUNK, lower=True)
    carry = jnp.zeros((1, l), F32)
    out = []
    for c in range(0, s, PREFIX_CHUNK):
        m = mask_f32[c:c + PREFIX_CHUNK, :]
        out.append(_dot(tri, m.astype(BF16)) + carry)
        carry = carry + jnp.sum(m, axis=0, keepdims=True)
    return jnp.concatenate(out, axis=0)


def _prefix_lanes(mask_f32):
    e, s = mask_f32.shape
    tri = _strict_triangle(PREFIX_CHUNK, lower=False)
    carry = jnp.zeros((e, 1), F32)
    out = []
    for c in range(0, s, PREFIX_CHUNK):
        m = mask_f32[:, c:c + PREFIX_CHUNK]
        out.append(_dot(m.astype(BF16), tri) + carry)
        carry = carry + jnp.sum(m, axis=1, keepdims=True)
    return jnp.concatenate(out, axis=1)


def _select_slots(key, thr, need, prefix_fn):
    return _select_slots_and_counts(key, thr, need, prefix_fn)[0]


def _select_slots_and_counts(key, thr, need, prefix_fn):
    gt = jnp.where(key > thr, 1.0, 0.0)
    eq = jnp.where(key == thr, 1.0, 0.0)
    eq_rank = prefix_fn(eq)
    sel = gt + eq * jnp.where(eq_rank < need, 1.0, 0.0)
    pos = prefix_fn(sel)
    return jnp.where(sel > 0.5, pos, -1.0), pos


GATE_GROUP = 16
GATE_PIECES = 3


def _pack_gate_lanes(aff):
    hi = aff.astype(BF16).astype(F32)
    r1 = aff - hi
    mid = r1.astype(BF16).astype(F32)
    lo = (r1 - mid).astype(BF16).astype(F32)
    packed = hi + pltpu.roll(mid, GATE_GROUP, 1) + pltpu.roll(lo, 2 * GATE_GROUP, 1)
    return packed.astype(BF16)


def _unpack_gate(tail, e):
    lane = lax.broadcasted_iota(I32, (1, LANES), 1)
    mine = ((lane & (GATE_GROUP - 1)) == e) & (lane < GATE_PIECES * GATE_GROUP)
    return jnp.sum(jnp.where(mine, tail.astype(F32), 0.0), axis=1, keepdims=True)


def _pack_row_words(h, gate_tile):
    rows, d_model = h.shape
    half = d_model // 2
    hi = jnp.concatenate([h[:, 0:half], gate_tile], axis=1).astype(F32)
    lo = jnp.concatenate([h[:, half:], jnp.zeros((rows, LANES), BF16)], axis=1).astype(F32)
    return pltpu.bitcast(hi, U32) | (pltpu.bitcast(lo, U32) >> 16)


def _unpack_row_words(words):
    hi = pltpu.bitcast(words & jnp.uint32(0xFFFF0000), F32)
    lo = pltpu.bitcast(words << 16, F32)
    return hi, lo


def _route_kernel(afft_ref, posm_ref, thr_ref, need_ref, starts_ref, *, cap):
    n_experts, seq = afft_ref.shape[1], afft_ref.shape[2]
    key = pltpu.bitcast(afft_ref[0], I32)
    thr = jnp.zeros((n_experts, 1), I32)
    for bit in range(F32_KEY_BITS - 1, -1, -1):
        cand = thr | (1 << bit)
        cnt = jnp.sum(jnp.where(key >= cand, 1.0, 0.0), axis=1, keepdims=True)
        thr = jnp.where(cnt >= cap, cand, thr)
    n_gt = jnp.sum(jnp.where(key > thr, 1.0, 0.0), axis=1, keepdims=True)
    need = cap - n_gt
    thr_ref[0] = thr
    need_ref[0] = need
    posm, before = _select_slots_and_counts(key, thr, need, _prefix_lanes)
    posm_ref[0] = posm.astype(I32)
    starts_ref[0] = jnp.concatenate(
        [before[:, r:r + 1] for r in range(0, seq, COMBINE_ROWS)], axis=1).astype(I32)


def _route(afft, cap):
    b, n_experts, seq = afft.shape
    n_tiles = seq // COMBINE_ROWS
    rows = b * n_experts
    outs = pl.pallas_call(
        functools.partial(_route_kernel, cap=cap),
        out_shape=(jax.ShapeDtypeStruct((1, rows, seq), I32),
                   jax.ShapeDtypeStruct((1, rows, 1), I32),
                   jax.ShapeDtypeStruct((1, rows, 1), F32),
                   jax.ShapeDtypeStruct((1, rows, n_tiles), I32)),
        grid=(1,),
        in_specs=[pl.BlockSpec((1, rows, seq), lambda i: (0, 0, 0))],
        out_specs=(pl.BlockSpec((1, rows, seq), lambda i: (0, 0, 0)),
                   pl.BlockSpec((1, rows, 1), lambda i: (0, 0, 0)),
                   pl.BlockSpec((1, rows, 1), lambda i: (0, 0, 0)),
                   pl.BlockSpec((1, rows, n_tiles), lambda i: (0, 0, 0))),
        compiler_params=_params(1),
        name="expert_route",
    )(afft.reshape(1, rows, seq))
    return tuple(o.reshape(b, n_experts, o.shape[2]) for o in outs)


SC_LANES = 16
SC_GATHER_ROWS = 64
SC_BACKGROUND_SHARE = 2


def _sc_expert_gather(posm2d, h_words, n_experts, e_offset, n_e, b, seq, cap, worker_share=1):
    width = h_words.shape[1]
    info = plsc.get_sparse_core_info()
    n_cores, n_subcores = info.num_cores, info.num_subcores
    n_workers = n_cores * n_subcores // worker_share
    assert info.num_lanes == SC_LANES and (b * n_e) % n_workers == 0
    pairs_per_worker = (b * n_e) // n_workers
    mesh = plsc.VectorSubcoreMesh(core_axis_name="c", subcore_axis_name="s")

    @functools.partial(
        pl.kernel, mesh=mesh,
        out_type=jax.ShapeDtypeStruct((n_e * b * cap, width), U32),
        compiler_params=pltpu.CompilerParams(needs_layout_passes=False),
        scratch_types=[
            pltpu.VMEM((seq,), I32),
            pltpu.VMEM((cap,), I32),
            pltpu.VMEM((SC_GATHER_ROWS, width), U32),
            pltpu.SemaphoreType.DMA,
        ],
        name="sc_expert_gather",
    )
    def gather(posm_hbm, h_hbm, out_hbm, pos_v, idx_v, rows_v, sem):
        wid = lax.axis_index("s") * n_cores + lax.axis_index("c")

        def move_pair(pair):
            bi = pair // n_e
            e = pair - bi * n_e
            pltpu.sync_copy(posm_hbm.at[bi * n_experts + e_offset + e], pos_v)

            @pl.loop(0, seq, step=SC_LANES)
            def _(t0):
                slots = pos_v[pl.ds(t0, SC_LANES)]
                rows = lax.iota(I32, SC_LANES) + (t0 + bi * seq)
                plsc.store_scatter(idx_v, [slots], rows, mask=slots >= 0)

            out_base = (e * b + bi) * cap
            for c in range(cap // SC_GATHER_ROWS):
                chunk = idx_v.at[pl.ds(c * SC_GATHER_ROWS, SC_GATHER_ROWS)]
                pltpu.async_copy(h_hbm.at[chunk], rows_v, sem).wait()
                pltpu.sync_copy(rows_v, out_hbm.at[pl.ds(out_base + c * SC_GATHER_ROWS, SC_GATHER_ROWS)])

        @pl.when(wid < n_workers)
        def _():
            for p in range(pairs_per_worker):
                move_pair(wid * pairs_per_worker + p)

    return gather(posm2d, h_words).reshape(n_e, b, cap, width)


EXPERT_ROWS = 1024
EXPERT_FTILE = 256


EXPERT_WBUFS = 2


def _expert_kernel(xg_ref, wg_hbm, wu_hbm, wd_hbm, o_ref,
                   x_ref, g_ref, hact_ref, wg_full, wu_full, wd_full, wg_buf, wu_buf, wd_buf, sem,
                   *, layer, e_offset, n_ftiles):
    e = pl.program_id(0)
    m = pl.program_id(1)
    n_e = pl.num_programs(0)
    nb, cap, width = xg_ref.shape[1], xg_ref.shape[2], xg_ref.shape[3]
    half = width - LANES
    d_model = 2 * half
    rows = nb * cap
    tf = EXPERT_FTILE
    assert n_ftiles % EXPERT_WBUFS == 0

    def tile_copies(expert, f):
        ge = e_offset + expert
        slot = f % EXPERT_WBUFS
        return (
            pltpu.make_async_copy(wg_hbm.at[layer, ge, :, pl.ds(f * tf, tf)], wg_buf.at[slot], sem.at[0, slot]),
            pltpu.make_async_copy(wu_hbm.at[layer, ge, :, pl.ds(f * tf, tf)], wu_buf.at[slot], sem.at[1, slot]),
            pltpu.make_async_copy(wd_hbm.at[layer, ge, pl.ds(f * tf, tf), :], wd_buf.at[slot], sem.at[2, slot]),
        )

    def start(expert, f):
        for cp in tile_copies(expert, f):
            cp.start()

    @pl.when((e == 0) & (m == 0))
    def _():
        for f in range(EXPERT_WBUFS):
            start(e, f)

    for i in range(nb):
        r0 = i * cap
        hi, lo = _unpack_row_words(xg_ref[0, i])
        x_ref[r0:r0 + cap, 0:half] = hi[:, 0:half].astype(BF16)
        x_ref[r0:r0 + cap, half:d_model] = lo[:, 0:half].astype(BF16)
        g_ref[r0:r0 + cap, :] = jnp.broadcast_to(
            _unpack_gate(hi[:, half:width], e_offset + e), (cap, LANES))

    def receive(f):
        slot = f % EXPERT_WBUFS
        for cp in tile_copies(e, f):
            cp.wait()
        wg_full[:, f * tf:(f + 1) * tf] = wg_buf[slot].astype(BF16)
        wu_full[:, f * tf:(f + 1) * tf] = wu_buf[slot].astype(BF16)
        wd_full[f * tf:(f + 1) * tf, :] = wd_buf[slot].astype(BF16)
        ahead = f + EXPERT_WBUFS
        if ahead < n_ftiles:
            start(e, ahead)
        else:
            @pl.when(e + 1 < n_e)
            def _():
                start(e + 1, ahead - n_ftiles)

    def body(first_group):
        x = x_ref[...]
        for f in range(n_ftiles):
            if first_group:
                receive(f)
            a = _dot(x, wg_full[:, f * tf:(f + 1) * tf])
            u = _dot(x, wu_full[:, f * tf:(f + 1) * tf])
            hact_ref[:, f * tf:(f + 1) * tf] = (a * jax.nn.sigmoid(a) * u).astype(BF16)
        for c in range(0, d_model, COL_CHUNK):
            y = _dot(hact_ref[...], wd_full[:, c:c + COL_CHUNK])
            for j in range(0, COL_CHUNK, LANES):
                o_ref[:, 0, :, c + j:c + j + LANES] = (
                    (y[:, j:j + LANES] * g_ref[...]).astype(BF16).reshape(nb, cap, LANES))

    @pl.when(m == 0)
    def _():
        body(True)

    @pl.when(m != 0)
    def _():
        body(False)


def _experts(xg, w_gate, w_up, w_down, layer, e_offset):
    n_e, b, cap, width = xg.shape
    d_model = 2 * (width - LANES)
    d_expert = w_gate.shape[3]
    nb = EXPERT_ROWS // cap
    n_ftiles = d_expert // EXPERT_FTILE
    return pl.pallas_call(
        functools.partial(_expert_kernel, layer=layer, e_offset=e_offset, n_ftiles=n_ftiles),
        out_shape=jax.ShapeDtypeStruct((b, n_e, cap, d_model), BF16),
        grid=(n_e, b // nb),
        in_specs=[
            pl.BlockSpec((1, nb, cap, width), lambda e, m: (e, m, 0, 0)),
            pl.BlockSpec(memory_space=pl.ANY),
            pl.BlockSpec(memory_space=pl.ANY),
            pl.BlockSpec(memory_space=pl.ANY),
        ],
        out_specs=pl.BlockSpec((nb, 1, cap, d_model), lambda e, m: (m, e, 0, 0)),
        scratch_shapes=[pltpu.VMEM((EXPERT_ROWS, d_model), BF16),
                        pltpu.VMEM((EXPERT_ROWS, LANES), F32),
                        pltpu.VMEM((EXPERT_ROWS, d_expert), BF16),
                        pltpu.VMEM((d_model, d_expert), BF16),
                        pltpu.VMEM((d_model, d_expert), BF16),
                        pltpu.VMEM((d_expert, d_model), BF16),
                        pltpu.VMEM((EXPERT_WBUFS, d_model, EXPERT_FTILE), F32),
                        pltpu.VMEM((EXPERT_WBUFS, d_model, EXPERT_FTILE), F32),
                        pltpu.VMEM((EXPERT_WBUFS, EXPERT_FTILE, d_model), F32),
                        pltpu.SemaphoreType.DMA((3, EXPERT_WBUFS))],
        compiler_params=_params(2),
        name="experts",
    )(xg, w_gate, w_up, w_down)


COMBINE_ROWS = 256
COMBINE_WINDOW = 64
MXU_DEPTH = 256
BF16_ROWS = 16


def _combine_kernel(starts_ref, aff_ref, thr_ref, need_ref, x1_ref, *rest, cap, n_experts, final_norm):
    n_y = len(rest) - (6 if final_norm else 5)
    y_hbm = rest[:n_y]
    g_ref = rest[n_y] if final_norm else None
    o_ref, post_ref, pfull_ref, ybuf, sem = rest[-5:]
    b = pl.program_id(0)
    t = pl.program_id(1)
    n_b = pl.num_programs(0)
    n_t = pl.num_programs(1)
    rows, win = COMBINE_ROWS, COMBINE_WINDOW
    experts_per_group = n_experts // n_y
    group_rows = ybuf.shape[0] // (2 * n_y)

    def ybuf_row(buf_slot, g):
        return pl.multiple_of((buf_slot * n_y + g) * group_rows, BF16_ROWS)

    n_chunks = sem.shape[1]
    chunks_per_group = n_chunks // n_y
    chunk_rows = group_rows // chunks_per_group
    slot = b % 2

    def chunk_copy(seq_idx, c, dst_slot):
        g, r = c // chunks_per_group, (c % chunks_per_group) * chunk_rows
        return pltpu.make_async_copy(y_hbm[g].at[seq_idx, pl.ds(r, chunk_rows), :],
                                     ybuf.at[pl.ds(ybuf_row(dst_slot, g) + r, chunk_rows), :],
                                     sem.at[dst_slot, c])

    @pl.when((b == 0) & (t == 0))
    def _():
        for c in range(n_chunks):
            chunk_copy(0, c, 0).start()

    @pl.when(t == 0)
    def _():
        for c in range(n_chunks):
            chunk_copy(b, c, slot).wait()
        key = pltpu.bitcast(aff_ref[...], I32)
        post_ref[...] = _select_slots(key, thr_ref[0], need_ref[0], _prefix_rows)

    for c in range(n_chunks):
        @pl.when((t == c) & (b + 1 < n_b))
        def _():
            chunk_copy(b + 1, c, 1 - slot).start()

    posm = post_ref[pl.ds(pl.multiple_of(t * rows, rows), rows), :]
    base = (b * (n_t + 1) + t) * n_experts
    wstart, ok = [], None
    for e in range(n_experts):
        first = starts_ref[base + e]
        end = starts_ref[base + n_experts + e]
        w0 = jnp.minimum((first // BF16_ROWS) * BF16_ROWS, cap - win)
        fits = end - w0 <= win
        wstart.append(w0)
        ok = fits if ok is None else jnp.logical_and(ok, fits)

    def finish(acc):
        if final_norm:
            acc = _rmsnorm_rows(acc, g_ref[...])
        o_ref[0] = acc

    @pl.when(ok)
    def _():
        per_dot, per_tile = MXU_DEPTH // win, LANES // win
        lane = lax.broadcasted_iota(I32, (1, LANES), 1)
        lane_f = lane.astype(F32)
        acc = x1_ref[0]
        for e0 in range(0, n_experts, per_dot):
            onehots, windows = [], []
            for e1 in range(e0, e0 + per_dot, per_tile):
                rel = None
                for k in range(per_tile - 1, -1, -1):
                    e = e1 + k
                    shifted = posm[:, e:e + 1] - (wstart[e] - k * win).astype(F32)
                    rel = shifted if rel is None else jnp.where(lane < (k + 1) * win, shifted, rel)
                onehots.append(jnp.where(rel == lane_f, 1.0, 0.0).astype(BF16))
            for e in range(e0, e0 + per_dot):
                g, el = divmod(e, experts_per_group)
                r0 = pl.multiple_of(ybuf_row(slot, g) + el * cap + wstart[e], BF16_ROWS)
                windows.append(ybuf[pl.ds(r0, win), :])
            acc = acc + _dot(jnp.concatenate(onehots, axis=1), jnp.concatenate(windows, axis=0))
        finish(acc)

    @pl.when(jnp.logical_not(ok))
    def _():
        slot_ids = lax.broadcasted_iota(I32, (1, cap), 1).astype(F32)
        for e in range(n_experts):
            pfull_ref[:, e * cap:(e + 1) * cap] = jnp.where(
                posm[:, e:e + 1] == slot_ids, 1.0, 0.0).astype(BF16)
        acc = x1_ref[0]
        for g in range(n_y):
            acc = acc + _dot(pfull_ref[:, g * group_rows:(g + 1) * group_rows],
                             ybuf[pl.ds(ybuf_row(slot, g), group_rows), :])
        finish(acc)


def _combine(starts, aff2d, thr_row, need_row, x1_3d, y_groups, cap, n_experts, final_g=None):
    b, seq, d_model = x1_3d.shape
    assert sum(y.shape[1] for y in y_groups) == n_experts * cap and cap >= COMBINE_WINDOW
    final_norm = final_g is not None
    n_tiles = seq // COMBINE_ROWS
    group_rows = y_groups[0].shape[1]
    assert all(y.shape[1] == group_rows for y in y_groups)
    assert n_tiles % len(y_groups) == 0 and group_rows % (n_tiles // len(y_groups)) == 0
    y_specs = [pl.BlockSpec(memory_space=pl.ANY) for _ in y_groups]
    g_specs = [pl.BlockSpec((1, d_model), lambda i, t, s: (0, 0))] if final_norm else []
    g_args = [final_g] if final_norm else []
    return pl.pallas_call(
        functools.partial(_combine_kernel, cap=cap, n_experts=n_experts, final_norm=final_norm),
        out_shape=jax.ShapeDtypeStruct((b, seq, d_model), F32),
        grid_spec=pltpu.PrefetchScalarGridSpec(
            num_scalar_prefetch=1,
            grid=(b, n_tiles),
            in_specs=[
                pl.BlockSpec((seq, LANES), lambda i, t, s: (i, 0)),
                pl.BlockSpec((1, 1, LANES), lambda i, t, s: (i, 0, 0)),
                pl.BlockSpec((1, 1, LANES), lambda i, t, s: (i, 0, 0)),
                pl.BlockSpec((1, COMBINE_ROWS, d_model), lambda i, t, s: (i, t, 0)),
            ] + y_specs + g_specs,
            out_specs=pl.BlockSpec((1, COMBINE_ROWS, d_model), lambda i, t, s: (i, t, 0)),
            scratch_shapes=[pltpu.VMEM((seq, LANES), F32),
                            pltpu.VMEM((COMBINE_ROWS, n_experts * cap), BF16),
                            pltpu.VMEM((2 * len(y_groups) * group_rows, d_model), BF16),
                            pltpu.SemaphoreType.DMA((2, n_tiles))],
        ),
        compiler_params=_params(2),
        name="combine",
    )(starts, aff2d, thr_row, need_row, x1_3d, *y_groups, *g_args)


EXPERT_GROUPS = 2


def _moe(afft, aff2d, h_words, x1_2d, w_gate, w_up, w_down, layer, b, seq, final_g=None):
    n_experts = afft.shape[1]
    assert n_experts <= GATE_GROUP and n_experts % EXPERT_GROUPS == 0
    d_model = x1_2d.shape[1]
    cap = CAPACITY_FACTOR * seq // n_experts
    n_e = n_experts // EXPERT_GROUPS
    posm, thr, need, tile_starts = _route(afft, cap)
    starts = jnp.concatenate([jnp.swapaxes(tile_starts, 1, 2),
                              jnp.full((b, 1, n_experts), cap, I32)], axis=1).reshape(-1)
    posm2d = posm.reshape(b * n_experts, seq)
    xgs = [_sc_expert_gather(posm2d, h_words, n_experts, g * n_e, n_e, b, seq, cap,
                             worker_share=1 if g == 0 else SC_BACKGROUND_SHARE)
           for g in range(EXPERT_GROUPS)]
    ys = [_experts(xg, w_gate, w_up, w_down, layer, g * n_e).reshape(b, n_e * cap, d_model)
          for g, xg in enumerate(xgs)]
    pad = LANES - n_experts
    thr_row = jnp.pad(thr.reshape(b, 1, n_experts), ((0, 0), (0, 0), (0, pad)),
                      constant_values=np.iinfo(np.int32).max)
    need_row = jnp.pad(need.reshape(b, 1, n_experts), ((0, 0), (0, 0), (0, pad)))
    return _combine(starts, aff2d, thr_row, need_row, x1_2d.reshape(b, seq, d_model), ys, cap,
                    n_experts, final_g)


def _rotary_tile(t, cos, sin_lo, sin_hi):
    half = ROT_DIM // 2
    return t * cos + pltpu.roll(t, LANES - half, 1) * sin_lo + pltpu.roll(t, half, 1) * sin_hi


def _inproj_attn_kernel(x_ref, g_ref, w_ref, pos_ref, rot_ref,
                        q_ref, k_ref, v_ref, qm_ref, *, tok_width, kv_width):
    qscale = HEAD_DIM ** -0.5
    rows = x_ref.shape[0] // INPROJ_PARTS

    def prepare(part):
        r = slice(part * rows, (part + 1) * rows)
        hn = _rmsnorm_rows(x_ref[r, :], g_ref[...]).astype(BF16)
        ang = pos_ref[r, :].astype(F32) * rot_ref[0:1, :]
        cos = jnp.cos(ang)
        sin = jnp.sin(ang)
        return hn, cos, sin * rot_ref[1:2, :], sin * rot_ref[2:3, :]

    def project(part, hn, cos, sin_lo, sin_hi):
        r = slice(part * rows, (part + 1) * rows)
        for c in range(0, tok_width, COL_CHUNK):
            pc = _dot(hn, w_ref[:, c:c + COL_CHUNK])
            for j in range(0, COL_CHUNK, LANES):
                rot = _rotary_tile(pc[:, j:j + LANES], cos, sin_lo, sin_hi)
                q_ref[r, c + j:c + j + LANES] = (rot * qscale).astype(BF16)
        kv = _dot(hn, w_ref[:, tok_width:tok_width + 2 * kv_width])
        k01 = _rotary_tile(kv[:, 0:LANES], cos, sin_lo, sin_hi)
        k2x = _rotary_tile(kv[:, LANES:2 * LANES], cos, sin_lo, sin_hi)
        k_ref[0, 0, r, :] = k01[:, 0:HEAD_DIM].astype(BF16)
        k_ref[0, 1, r, :] = k01[:, HEAD_DIM:LANES].astype(BF16)
        k_ref[0, 2, r, :] = k2x[:, 0:HEAD_DIM].astype(BF16)
        for hh in range(kv_width // HEAD_DIM):
            lo = kv_width + hh * HEAD_DIM
            v_ref[0, hh, r, :] = kv[:, lo:lo + HEAD_DIM].astype(BF16)
        qm_ref[r, :] = _dot(hn, w_ref[:, tok_width + 2 * kv_width:]).astype(BF16)

    prepared = [prepare(part) for part in range(INPROJ_PARTS)]
    for part in range(INPROJ_PARTS):
        project(part, *prepared[part])


def _inproj_attn(x2d, g, w_bf16, pos2d, rot_rows, tok_width, kv_width, seq):
    t, d_model = x2d.shape
    n = w_bf16.shape[1]
    n_kv = kv_width // HEAD_DIM
    assert n_kv == 3 and kv_width + HEAD_DIM == 2 * LANES
    tiles_per_seq = seq // INPROJ_ROWS
    kv_spec = pl.BlockSpec((1, n_kv, INPROJ_ROWS, HEAD_DIM),
                           lambda i: (i // tiles_per_seq, 0, i % tiles_per_seq, 0))
    return pl.pallas_call(
        functools.partial(_inproj_attn_kernel, tok_width=tok_width, kv_width=kv_width),
        out_shape=(jax.ShapeDtypeStruct((t, tok_width), BF16),
                   jax.ShapeDtypeStruct((t // seq, n_kv, seq, HEAD_DIM), BF16),
                   jax.ShapeDtypeStruct((t // seq, n_kv, seq, HEAD_DIM), BF16),
                   jax.ShapeDtypeStruct((t, n - tok_width - 2 * kv_width), BF16)),
        grid=(t // INPROJ_ROWS,),
        in_specs=[
            pl.BlockSpec((INPROJ_ROWS, d_model), lambda i: (i, 0)),
            pl.BlockSpec((1, d_model), lambda i: (0, 0)),
            _resident((d_model, n), lambda i: (0, 0)),
            pl.BlockSpec((INPROJ_ROWS, 1), lambda i: (i, 0)),
            pl.BlockSpec((8, LANES), lambda i: (0, 0)),
        ],
        out_specs=(pl.BlockSpec((INPROJ_ROWS, tok_width), lambda i: (i, 0)),
                   kv_spec, kv_spec,
                   pl.BlockSpec((INPROJ_ROWS, n - tok_width - 2 * kv_width), lambda i: (i, 0))),
        compiler_params=_params(1),
        name="inproj_attn",
    )(x2d, g, w_bf16, pos2d, rot_rows)


WATTN_QBLOCKS = 4


def _wattn_kernel(sink_ref, q_ref, kp_ref, kc_ref, kn_ref, vp_ref, vc_ref, vn_ref, o_ref,
                  valid_ref, kpad_ref, vpad_ref, s_ref, p_ref, inv_ref, *, seq):
    step = pl.program_id(1)
    n_kv = kc_ref.shape[1]
    pairs = GQA_RATIO // 2
    half_rows = pairs * BLOCK
    key_rows = (WATTN_QBLOCKS + 2) * BLOCK
    zeros = jnp.zeros((key_rows, HEAD_DIM), BF16)
    ones_col = jnp.where(lax.broadcasted_iota(I32, (key_rows, HEAD_DIM), 1) == 0, 1.0, 0.0).astype(BF16)
    low_half = lax.broadcasted_iota(I32, (1, LANES), 1) < HEAD_DIM
    for hk in range(n_kv):
        kw = jnp.concatenate([kp_ref[0, hk], kc_ref[0, hk], kn_ref[0, hk]], axis=0)
        vw = jnp.concatenate([vp_ref[0, hk], vc_ref[0, hk], vn_ref[0, hk]], axis=0)
        kpad_ref[2 * hk] = jnp.concatenate([kw, zeros], axis=1)
        kpad_ref[2 * hk + 1] = jnp.concatenate([zeros, kw], axis=1)
        vpad_ref[2 * hk] = jnp.concatenate([vw, ones_col], axis=1)
        vpad_ref[2 * hk + 1] = jnp.concatenate([ones_col, vw], axis=1)

    qi = lax.broadcasted_iota(I32, (BLOCK, 3 * BLOCK), 0)
    kj = lax.broadcasted_iota(I32, (BLOCK, 3 * BLOCK), 1)
    for qb in range(WATTN_QBLOCKS):
        n = step * WATTN_QBLOCKS + qb
        k0 = qb * BLOCK
        first = jnp.maximum(qi, BLOCK - n * BLOCK)
        last = jnp.minimum(qi + 2 * WINDOW, seq + BLOCK - 1 - n * BLOCK)
        valid_ref[qb] = jnp.where(((kj - first) | (last - kj)) >= 0, 1.0, 0.0)
        q0 = qb * BLOCK
        for hk in range(n_kv):
            tile0 = hk * pairs
            qs = jnp.concatenate(
                [q_ref[0, q0:q0 + BLOCK, (tile0 + j) * LANES:(tile0 + j + 1) * LANES]
                 for j in range(pairs)], axis=0)
            s_ref[qb, hk, 0:half_rows, :] = _dot_nt(qs, kpad_ref[2 * hk, k0:k0 + 3 * BLOCK, :])
            s_ref[qb, hk, half_rows:2 * half_rows, :] = _dot_nt(
                qs, kpad_ref[2 * hk + 1, k0:k0 + 3 * BLOCK, :])
        for hk in range(n_kv):
            for c in range(GQA_RATIO):
                j, odd = c % pairs, c // pairs
                r = c * BLOCK
                s = jnp.concatenate([
                    jnp.where(valid_ref[qb, :, 0:BLOCK] > 0.5, s_ref[qb, hk, r:r + BLOCK, 0:BLOCK], NEG_INF),
                    s_ref[qb, hk, r:r + BLOCK, BLOCK:2 * BLOCK],
                    jnp.where(valid_ref[qb, :, 2 * BLOCK:] > 0.5,
                              s_ref[qb, hk, r:r + BLOCK, 2 * BLOCK:], NEG_INF),
                ], axis=1)
                sk = sink_ref[hk * GQA_RATIO + 2 * j + odd]
                m = jnp.maximum(jnp.max(s, axis=-1, keepdims=True), sk)
                p_ref[qb, hk, r:r + BLOCK, :] = jnp.exp(s - m).astype(BF16)
                inv_ref[qb, hk, j * BLOCK:(j + 1) * BLOCK, odd * HEAD_DIM:(odd + 1) * HEAD_DIM] = (
                    jnp.broadcast_to(jnp.exp(sk - m), (BLOCK, HEAD_DIM)))
        for hk in range(n_kv):
            pv_even = _dot(p_ref[qb, hk, 0:half_rows, :], vpad_ref[2 * hk, k0:k0 + 3 * BLOCK, :])
            pv_odd = _dot(p_ref[qb, hk, half_rows:2 * half_rows, :],
                          vpad_ref[2 * hk + 1, k0:k0 + 3 * BLOCK, :])
            den = (jnp.where(low_half, pv_even[:, HEAD_DIM:HEAD_DIM + 1], pv_odd[:, 0:1])
                   + inv_ref[qb, hk])
            o = jnp.where(low_half, pv_even, pv_odd) / den
            for j in range(pairs):
                lo = (hk * pairs + j) * LANES
                o_ref[0, q0:q0 + BLOCK, lo:lo + LANES] = o[j * BLOCK:(j + 1) * BLOCK].astype(BF16)


def _window_attention(sink, q3d, k4d, v4d):
    b, seq, tok_width = q3d.shape
    n_kv = k4d.shape[1]
    nb = seq // BLOCK
    qb = WATTN_QBLOCKS
    assert nb % qb == 0
    edge_block = (1, n_kv, BLOCK, HEAD_DIM)
    prev_spec = pl.BlockSpec(edge_block, lambda i, s: (i, 0, jnp.maximum(s * qb - 1, 0), 0))
    cur_spec = pl.BlockSpec((1, n_kv, qb * BLOCK, HEAD_DIM), lambda i, s: (i, 0, s, 0))
    next_spec = pl.BlockSpec(edge_block, lambda i, s: (i, 0, jnp.minimum(s * qb + qb, nb - 1), 0))
    key_rows = (qb + 2) * BLOCK
    return pl.pallas_call(
        functools.partial(_wattn_kernel, seq=seq),
        out_shape=jax.ShapeDtypeStruct((b, seq, tok_width), BF16),
        grid=(b, nb // qb),
        in_specs=[
            pl.BlockSpec(memory_space=pltpu.SMEM),
            pl.BlockSpec((1, qb * BLOCK, tok_width), lambda i, s: (i, s, 0)),
            prev_spec, cur_spec, next_spec, prev_spec, cur_spec, next_spec,
        ],
        out_specs=pl.BlockSpec((1, qb * BLOCK, tok_width), lambda i, s: (i, s, 0)),
        scratch_shapes=[pltpu.VMEM((qb, BLOCK, 3 * BLOCK), F32),
                        pltpu.VMEM((2 * n_kv, key_rows, LANES), BF16),
                        pltpu.VMEM((2 * n_kv, key_rows, LANES), BF16),
                        pltpu.VMEM((qb, n_kv, GQA_RATIO * BLOCK, 3 * BLOCK), F32),
                        pltpu.VMEM((qb, n_kv, GQA_RATIO * BLOCK, 3 * BLOCK), BF16),
                        pltpu.VMEM((qb, n_kv, GQA_RATIO // 2 * BLOCK, LANES), F32)],
        compiler_params=_params(2),
        name="window_attention",
    )(sink, q3d, k4d, k4d, k4d, v4d, v4d, v4d)


def _rotary_rows(dtype=F32):
    half = ROT_DIM // 2
    inv_freq = ROPE_THETA ** (-jnp.arange(0, ROT_DIM, 2, dtype=jnp.float32) / ROT_DIM)
    lane = np.arange(LANES) % HEAD_DIM
    rotated = lane < ROT_DIM
    freq = jnp.where(jnp.asarray(rotated), inv_freq[jnp.asarray(lane % half)], 0.0)
    rows = jnp.zeros((8, LANES), dtype)
    rows = rows.at[0].set(freq)
    rows = rows.at[1].set(jnp.asarray(np.where(lane < half, -1.0, 0.0), dtype))
    rows = rows.at[2].set(jnp.asarray(np.where(rotated & (lane >= half), 1.0, 0.0), dtype))
    return rows


def kernel(x, mem, positions, norm_mix_g, norm_ffn_g, mem_norm_g, final_g, mem_w_kv,
           pool_w_in, pool_group_w, pool_scale, pool_w_out,
           attn_w_in, attn_sink, attn_w_out,
           router_w, exp_w_gate, exp_w_up, exp_w_down):
    b, seq, d_model = x.shape
    depth = norm_mix_g.shape[0]
    t = b * seq
    n_experts = router_w.shape[2]
    tok_width = pool_scale.shape[1]
    n_groups = pool_group_w.shape[1]
    kv_width = (attn_w_in.shape[2] - tok_width - XA_WIDTH) // 2
    assert seq % ROW_TILE == 0 and seq % INPROJ_ROWS == 0 and mem.shape[1] == MEM_LEN

    memkv = _memkv(mem.reshape(b * MEM_LEN, d_model), mem_norm_g.reshape(1, d_model),
                   mem_w_kv.astype(BF16))
    memkv = memkv.reshape(depth, b, MEM_LEN, 2 * XA_WIDTH)
    rw_pad = jnp.pad(router_w, ((0, 0), (0, 0), (0, LANES - n_experts)))
    pos2d = positions.reshape(t, 1)
    rot_rows = _rotary_rows()

    x2d = x.reshape(t, d_model)
    for layer in range(depth):
        j = layer // 2
        g_mix = norm_mix_g[layer].reshape(1, d_model)
        if layer % 2 == 0:
            u, qm = _inproj_pool(x2d, g_mix, pool_w_in[j].astype(BF16), tok_width)
            tok = _pool_mixer(u.reshape(b, seq, tok_width), pool_group_w[j].astype(BF16),
                              pool_scale[j].reshape(n_groups, 1, tok_width // n_groups))
            tok = tok.reshape(t, tok_width)
            w_out = pool_w_out[j]
        else:
            q, k, v, qm = _inproj_attn(x2d, g_mix, attn_w_in[j].astype(BF16), pos2d, rot_rows,
                                       tok_width, kv_width, seq)
            tok = _window_attention(attn_sink[j], q.reshape(b, seq, tok_width), k, v)
            tok = tok.reshape(t, tok_width)
            w_out = attn_w_out[j]
        mo = _mem_xattn(qm, memkv, layer, seq)
        x1, h, afft, aff = _outproj(tok, mo, x2d, w_out.astype(BF16),
                                    norm_ffn_g[layer].reshape(1, d_model), rw_pad, layer,
                                    n_experts, seq)
        last = layer == depth - 1
        x2 = _moe(afft, aff, h, x1, exp_w_gate, exp_w_up, exp_w_down, layer, b, seq,
                  final_g.reshape(1, d_model) if last else None)
        x2d = x2.reshape(t, d_model)
    return x2d.reshape(b, seq, d_model)
```

```python
import functools

import jax
import jax.numpy as jnp
import numpy as np
from jax import lax
from jax.experimental import pallas as pl
from jax.experimental.pallas import tpu as pltpu
from jax.experimental.pallas import tpu_sc as plsc

F32 = jnp.float32
BF16 = jnp.bfloat16
I32 = jnp.int32
U32 = jnp.uint32

EPS = 1e-6
MEM_LEN = 256
XA_HEADS = 4
XA_HEAD_DIM = 128
XA_WIDTH = XA_HEADS * XA_HEAD_DIM
POOL_WINDOWS = (2, 4, 8, 16)
HEAD_DIM = 64
GQA_RATIO = 8
WINDOW = 128
BLOCK = 128
ROPE_THETA = 500000.0
ROT_DIM = 16
NEG_INF = -1e30
CAPACITY_FACTOR = 2

LANES = 128
SUBLANES = 8
MIB = 1024 * 1024
VMEM_LIMIT_BYTES = 56 * MIB

ROW_TILE = 512
INPROJ_ROWS = 1024
INPROJ_PARTS = 4
POOL_PAD = 16
PREFIX_CHUNK = 256
COL_CHUNK = 512
F32_KEY_BITS = 31


def _params(n_grid_dims):
    return pltpu.CompilerParams(
        dimension_semantics=("arbitrary",) * n_grid_dims,
        vmem_limit_bytes=VMEM_LIMIT_BYTES,
    )


def _resident(block_shape, index_map):
    return pl.BlockSpec(block_shape, index_map, pipeline_mode=pl.Buffered(1))


def _rmsnorm_rows(x, g):
    return x * lax.rsqrt(jnp.mean(x * x, axis=-1, keepdims=True) + EPS) * g


def _dot(a, b):
    return jnp.dot(a, b, preferred_element_type=F32)


def _dot_nt(a, b):
    return lax.dot_general(a, b, (((1,), (1,)), ((), ())), preferred_element_type=F32)


def _memkv_kernel(mem_ref, g_ref, w_ref, o_ref):
    hn = _rmsnorm_rows(mem_ref[...], g_ref[...]).astype(BF16)
    o_ref[0] = _dot(hn, w_ref[0]).astype(BF16)


def _memkv(mem2d, g, w_bf16):
    depth, d_model, n = w_bf16.shape
    rows = mem2d.shape[0]
    return pl.pallas_call(
        _memkv_kernel,
        out_shape=jax.ShapeDtypeStruct((depth, rows, n), BF16),
        grid=(depth, rows // ROW_TILE),
        in_specs=[
            pl.BlockSpec((ROW_TILE, d_model), lambda l, i: (i, 0)),
            pl.BlockSpec((1, d_model), lambda l, i: (0, 0)),
            pl.BlockSpec((1, d_model, n), lambda l, i: (l, 0, 0)),
        ],
        out_specs=pl.BlockSpec((1, ROW_TILE, n), lambda l, i: (l, i, 0)),
        compiler_params=_params(2),
        name="memkv",
    )(mem2d, g, w_bf16)


def _inproj_pool_kernel(x_ref, g_ref, w_ref, u_ref, qm_ref, *, tok_width):
    hn = _rmsnorm_rows(x_ref[...], g_ref[...]).astype(BF16)
    for c in range(0, tok_width, COL_CHUNK):
        u_ref[:, c:c + COL_CHUNK] = _dot(hn, w_ref[:, c:c + COL_CHUNK])
    qm_ref[...] = _dot(hn, w_ref[:, tok_width:]).astype(BF16)


def _inproj_pool(x2d, g, w_bf16, tok_width):
    t, d_model = x2d.shape
    n = w_bf16.shape[1]
    return pl.pallas_call(
        functools.partial(_inproj_pool_kernel, tok_width=tok_width),
        out_shape=(jax.ShapeDtypeStruct((t, tok_width), F32),
                   jax.ShapeDtypeStruct((t, n - tok_width), BF16)),
        grid=(t // INPROJ_ROWS,),
        in_specs=[
            pl.BlockSpec((INPROJ_ROWS, d_model), lambda i: (i, 0)),
            pl.BlockSpec((1, d_model), lambda i: (0, 0)),
            _resident((d_model, n), lambda i: (0, 0)),
        ],
        out_specs=(pl.BlockSpec((INPROJ_ROWS, tok_width), lambda i: (i, 0)),
                   pl.BlockSpec((INPROJ_ROWS, n - tok_width), lambda i: (i, 0))),
        compiler_params=_params(1),
        name="inproj_pool",
    )(x2d, g, w_bf16)


def _pool_group(u_ref, gw_ref, sc_ref, o_ref, *, window, seq):
    gwid = u_ref.shape[2]
    rows = seq + 2 * POOL_PAD
    half = window // 2
    assert 2 * half <= POOL_PAD
    zeros_pad = jnp.zeros((POOL_PAD, gwid), F32)
    u = u_ref[0]
    p = jnp.concatenate([zeros_pad, u, zeros_pad], axis=0)
    k = 1
    while k < half:
        p = p + pltpu.roll(p, rows - k, 0)
        k *= 2
    before = p if half % SUBLANES == 0 else pltpu.roll(p, half, 0)
    shift = half if half % SUBLANES == 0 else 0
    win = before[POOL_PAD - shift:POOL_PAD - shift + seq, :] + p[POOL_PAD:POOL_PAD + seq, :]
    t = lax.broadcasted_iota(I32, (seq, 1), 0)
    lo = jnp.maximum(t - half, 0)
    hi = jnp.minimum(t + half - 1, seq - 1)
    cnt = (hi - lo + 1).astype(F32)
    pooled = (win / cnt - u).astype(BF16)
    o_ref[0] = (_dot(pooled, gw_ref[0]) * sc_ref[0]).astype(BF16)


def _pool_kernel(u_ref, gw_ref, sc_ref, o_ref, *, seq):
    g = pl.program_id(1)
    for k, window in enumerate(POOL_WINDOWS):
        @pl.when(g == k)
        def _():
            _pool_group(u_ref, gw_ref, sc_ref, o_ref, window=window, seq=seq)


def _pool_mixer(u3d, gw_bf16, scale3d):
    b, seq, tok_width = u3d.shape
    n_groups, gwid, _ = gw_bf16.shape
    assert n_groups == len(POOL_WINDOWS) and n_groups * gwid == tok_width
    return pl.pallas_call(
        functools.partial(_pool_kernel, seq=seq),
        out_shape=jax.ShapeDtypeStruct((b, seq, tok_width), BF16),
        grid=(b, n_groups),
        in_specs=[
            pl.BlockSpec((1, seq, gwid), lambda i, g: (i, 0, g)),
            pl.BlockSpec((1, gwid, gwid), lambda i, g: (g, 0, 0)),
            pl.BlockSpec((1, 1, gwid), lambda i, g: (g, 0, 0)),
        ],
        out_specs=pl.BlockSpec((1, seq, gwid), lambda i, g: (i, 0, g)),
        compiler_params=_params(2),
        name="pool_mixer",
    )(u3d, gw_bf16, scale3d)


def _xattn_kernel(q_ref, kv_ref, o_ref):
    scale = XA_HEAD_DIM ** -0.5
    for h in range(XA_HEADS):
        lo = h * XA_HEAD_DIM
        q = q_ref[:, lo:lo + XA_HEAD_DIM]
        k = kv_ref[0, 0, :, lo:lo + XA_HEAD_DIM]
        v = kv_ref[0, 0, :, XA_WIDTH + lo:XA_WIDTH + lo + XA_HEAD_DIM]
        s = _dot_nt(q, k) * scale
        m = jnp.max(s, axis=-1, keepdims=True)
        p = jnp.exp(s - m)
        den = jnp.sum(p, axis=-1, keepdims=True)
        o_ref[:, lo:lo + XA_HEAD_DIM] = (_dot(p.astype(BF16), v) / den).astype(BF16)


XATTN_ROWS = 1024


def _mem_xattn(qm2d, memkv, layer, seq):
    t = qm2d.shape[0]
    tiles_per_seq = seq // XATTN_ROWS
    return pl.pallas_call(
        _xattn_kernel,
        out_shape=jax.ShapeDtypeStruct((t, XA_WIDTH), BF16),
        grid=(t // XATTN_ROWS,),
        in_specs=[
            pl.BlockSpec((XATTN_ROWS, XA_WIDTH), lambda i: (i, 0)),
            pl.BlockSpec((1, 1, MEM_LEN, 2 * XA_WIDTH), lambda i: (layer, i // tiles_per_seq, 0, 0)),
        ],
        out_specs=pl.BlockSpec((XATTN_ROWS, XA_WIDTH), lambda i: (i, 0)),
        compiler_params=_params(1),
        name="mem_xattn",
    )(qm2d, memkv)


def _outproj_kernel(tok_ref, mo_ref, x_ref, w_ref, g_ref, rw_ref,
                    x1_ref, h_ref, afft_ref, aff_ref, wcat_ref, x1prev_ref,
                    *, tok_width, n_experts):
    @pl.when(pl.program_id(0) == 0)
    def _():
        rw = rw_ref[0]
        w_hi = rw.astype(BF16)
        wcat_ref[:, 0:LANES] = w_hi
        wcat_ref[:, LANES:2 * LANES] = (rw - w_hi.astype(F32)).astype(BF16)
        x1prev_ref[...] = jnp.zeros_like(x1prev_ref)

    hn = _rmsnorm_rows(x1prev_ref[...], g_ref[...])
    h_prev = hn.astype(BF16)
    h_lo = (hn - h_prev.astype(F32)).astype(BF16)
    r = _dot(h_prev, wcat_ref[...]) + _dot(h_lo, wcat_ref[...])
    logits = r[:, 0:LANES] + r[:, LANES:2 * LANES]
    lt = logits.T[0:n_experts, :]
    m = jnp.max(lt, axis=0, keepdims=True)
    ex = jnp.exp(lt - m)
    afft = ex / jnp.sum(ex, axis=0, keepdims=True)
    afft_ref[0] = afft
    padded = jnp.concatenate(
        [afft, jnp.zeros((LANES - n_experts, afft.shape[1]), F32)], axis=0)
    aff = padded.T
    aff_ref[...] = aff
    h_ref[...] = _pack_row_words(h_prev, _pack_gate_lanes(aff))

    y = _dot(tok_ref[...], w_ref[0:tok_width, :]) + _dot(mo_ref[...], w_ref[tok_width:, :])
    x1 = x_ref[...] + y
    x1_ref[...] = x1
    x1prev_ref[...] = x1


def _outproj(tok2d, mo2d, x2d, w_bf16, g, rw_pad, layer, n_experts, seq):
    t, d_model = x2d.shape
    tok_width = tok2d.shape[1]
    tiles_per_seq = seq // ROW_TILE
    n_tiles = t // ROW_TILE

    def cur(i):
        return jnp.minimum(i, n_tiles - 1)

    def prev(i):
        return jnp.maximum(i - 1, 0)

    return pl.pallas_call(
        functools.partial(_outproj_kernel, tok_width=tok_width, n_experts=n_experts),
        out_shape=(jax.ShapeDtypeStruct((t, d_model), F32),
                   jax.ShapeDtypeStruct((t, d_model // 2 + LANES), U32),
                   jax.ShapeDtypeStruct((t // seq, n_experts, seq), F32),
                   jax.ShapeDtypeStruct((t, LANES), F32)),
        grid=(n_tiles + 1,),
        in_specs=[
            pl.BlockSpec((ROW_TILE, tok_width), lambda i: (cur(i), 0)),
            pl.BlockSpec((ROW_TILE, mo2d.shape[1]), lambda i: (cur(i), 0)),
            pl.BlockSpec((ROW_TILE, d_model), lambda i: (cur(i), 0)),
            _resident((d_model, d_model), lambda i: (0, 0)),
            pl.BlockSpec((1, d_model), lambda i: (0, 0)),
            _resident((1, d_model, LANES), lambda i: (layer, 0, 0)),
        ],
        out_specs=(pl.BlockSpec((ROW_TILE, d_model), lambda i: (cur(i), 0)),
                   pl.BlockSpec((ROW_TILE, d_model // 2 + LANES), lambda i: (prev(i), 0)),
                   pl.BlockSpec((1, n_experts, ROW_TILE),
                                lambda i: (prev(i) // tiles_per_seq, 0, prev(i) % tiles_per_seq)),
                   pl.BlockSpec((ROW_TILE, LANES), lambda i: (prev(i), 0))),
        scratch_shapes=[pltpu.VMEM((d_model, 2 * LANES), BF16),
                        pltpu.VMEM((ROW_TILE, d_model), F32)],
        compiler_params=_params(1),
        name="outproj_router",
    )(tok2d, mo2d, x2d, w_bf16, g, rw_pad)


def _strict_triangle(n, lower):
    r = lax.broadcasted_iota(I32, (n, n), 0)
    c = lax.broadcasted_iota(I32, (n, n), 1)
    return jnp.where((c < r) if lower else (r < c), 1.0, 0.0).astype(BF16)


def _prefix_rows(mask_f32):
    s, l = mask_f32.shape
    tri = _strict_triangle(PREFIX_CHUNK, lower=True)
    carry = jnp.zeros((1, l), F32)
    out = []
    for c in range(0, s, PREFIX_CHUNK):
        m = mask_f32[c:c + PREFIX_CHUNK, :]
        out.append(_dot(tri, m.astype(BF16)) + carry)
        carry = carry + jnp.sum(m, axis=0, keepdims=True)
    return jnp.concatenate(out, axis=0)


def _prefix_lanes(mask_f32):
    e, s = mask_f32.shape
    tri = _strict_triangle(PREFIX_CHUNK, lower=False)
    carry = jnp.zeros((e, 1), F32)
    out = []
    for c in range(0, s, PREFIX_CHUNK):
        m = mask_f32[:, c:c + PREFIX_CHUNK]
        out.append(_dot(m.astype(BF16), tri) + carry)
        carry = carry + jnp.sum(m, axis=1, keepdims=True)
    return jnp.concatenate(out, axis=1)


def _select_slots(key, thr, need, prefix_fn):
    return _select_slots_and_counts(key, thr, need, prefix_fn)[0]


def _select_slots_and_counts(key, thr, need, prefix_fn):
    gt = jnp.where(key > thr, 1.0, 0.0)
    eq = jnp.where(key == thr, 1.0, 0.0)
    eq_rank = prefix_fn(eq)
    sel = gt + eq * jnp.where(eq_rank < need, 1.0, 0.0)
    pos = prefix_fn(sel)
    return jnp.where(sel > 0.5, pos, -1.0), pos


GATE_GROUP = 16
GATE_PIECES = 3


def _pack_gate_lanes(aff):
    hi = aff.astype(BF16).astype(F32)
    r1 = aff - hi
    mid = r1.astype(BF16).astype(F32)
    lo = (r1 - mid).astype(BF16).astype(F32)
    packed = hi + pltpu.roll(mid, GATE_GROUP, 1) + pltpu.roll(lo, 2 * GATE_GROUP, 1)
    return packed.astype(BF16)


def _unpack_gate(tail, e):
    lane = lax.broadcasted_iota(I32, (1, LANES), 1)
    mine = ((lane & (GATE_GROUP - 1)) == e) & (lane < GATE_PIECES * GATE_GROUP)
    return jnp.sum(jnp.where(mine, tail.astype(F32), 0.0), axis=1, keepdims=True)


def _pack_row_words(h, gate_tile):
    rows, d_model = h.shape
    half = d_model // 2
    hi = jnp.concatenate([h[:, 0:half], gate_tile], axis=1).astype(F32)
    lo = jnp.concatenate([h[:, half:], jnp.zeros((rows, LANES), BF16)], axis=1).astype(F32)
    return pltpu.bitcast(hi, U32) | (pltpu.bitcast(lo, U32) >> 16)


def _unpack_row_words(words):
    hi = pltpu.bitcast(words & jnp.uint32(0xFFFF0000), F32)
    lo = pltpu.bitcast(words << 16, F32)
    return hi, lo


def _route_kernel(afft_ref, posm_ref, thr_ref, need_ref, starts_ref, *, cap):
    n_experts, seq = afft_ref.shape[1], afft_ref.shape[2]
    key = pltpu.bitcast(afft_ref[0], I32)
    thr = jnp.zeros((n_experts, 1), I32)
    for bit in range(F32_KEY_BITS - 1, -1, -1):
        cand = thr | (1 << bit)
        cnt = jnp.sum(jnp.where(key >= cand, 1.0, 0.0), axis=1, keepdims=True)
        thr = jnp.where(cnt >= cap, cand, thr)
    n_gt = jnp.sum(jnp.where(key > thr, 1.0, 0.0), axis=1, keepdims=True)
    need = cap - n_gt
    thr_ref[0] = thr
    need_ref[0] = need
    posm, before = _select_slots_and_counts(key, thr, need, _prefix_lanes)
    posm_ref[0] = posm.astype(I32)
    starts_ref[0] = jnp.concatenate(
        [before[:, r:r + 1] for r in range(0, seq, COMBINE_ROWS)], axis=1).astype(I32)


def _route(afft, cap):
    b, n_experts, seq = afft.shape
    n_tiles = seq // COMBINE_ROWS
    rows = b * n_experts
    outs = pl.pallas_call(
        functools.partial(_route_kernel, cap=cap),
        out_shape=(jax.ShapeDtypeStruct((1, rows, seq), I32),
                   jax.ShapeDtypeStruct((1, rows, 1), I32),
                   jax.ShapeDtypeStruct((1, rows, 1), F32),
                   jax.ShapeDtypeStruct((1, rows, n_tiles), I32)),
        grid=(1,),
        in_specs=[pl.BlockSpec((1, rows, seq), lambda i: (0, 0, 0))],
        out_specs=(pl.BlockSpec((1, rows, seq), lambda i: (0, 0, 0)),
                   pl.BlockSpec((1, rows, 1), lambda i: (0, 0, 0)),
                   pl.BlockSpec((1, rows, 1), lambda i: (0, 0, 0)),
                   pl.BlockSpec((1, rows, n_tiles), lambda i: (0, 0, 0))),
        compiler_params=_params(1),
        name="expert_route",
    )(afft.reshape(1, rows, seq))
    return tuple(o.reshape(b, n_experts, o.shape[2]) for o in outs)


SC_LANES = 16
SC_GATHER_ROWS = 64
SC_BACKGROUND_SHARE = 2


def _sc_expert_gather(posm2d, h_words, n_experts, e_offset, n_e, b, seq, cap, worker_share=1):
    width = h_words.shape[1]
    info = plsc.get_sparse_core_info()
    n_cores, n_subcores = info.num_cores, info.num_subcores
    n_workers = n_cores * n_subcores // worker_share
    assert info.num_lanes == SC_LANES and (b * n_e) % n_workers == 0
    pairs_per_worker = (b * n_e) // n_workers
    mesh = plsc.VectorSubcoreMesh(core_axis_name="c", subcore_axis_name="s")

    @functools.partial(
        pl.kernel, mesh=mesh,
        out_type=jax.ShapeDtypeStruct((n_e * b * cap, width), U32),
        compiler_params=pltpu.CompilerParams(needs_layout_passes=False),
        scratch_types=[
            pltpu.VMEM((seq,), I32),
            pltpu.VMEM((cap,), I32),
            pltpu.VMEM((SC_GATHER_ROWS, width), U32),
            pltpu.SemaphoreType.DMA,
        ],
        name="sc_expert_gather",
    )
    def gather(posm_hbm, h_hbm, out_hbm, pos_v, idx_v, rows_v, sem):
        wid = lax.axis_index("s") * n_cores + lax.axis_index("c")

        def move_pair(pair):
            bi = pair // n_e
            e = pair - bi * n_e
            pltpu.sync_copy(posm_hbm.at[bi * n_experts + e_offset + e], pos_v)

            @pl.loop(0, seq, step=SC_LANES)
            def _(t0):
                slots = pos_v[pl.ds(t0, SC_LANES)]
                rows = lax.iota(I32, SC_LANES) + (t0 + bi * seq)
                plsc.store_scatter(idx_v, [slots], rows, mask=slots >= 0)

            out_base = (e * b + bi) * cap
            for c in range(cap // SC_GATHER_ROWS):
                chunk = idx_v.at[pl.ds(c * SC_GATHER_ROWS, SC_GATHER_ROWS)]
                pltpu.async_copy(h_hbm.at[chunk], rows_v, sem).wait()
                pltpu.sync_copy(rows_v, out_hbm.at[pl.ds(out_base + c * SC_GATHER_ROWS, SC_GATHER_ROWS)])

        @pl.when(wid < n_workers)
        def _():
            for p in range(pairs_per_worker):
                move_pair(wid * pairs_per_worker + p)

    return gather(posm2d, h_words).reshape(n_e, b, cap, width)


EXPERT_ROWS = 1024
EXPERT_FTILE = 256


EXPERT_WBUFS = 2


def _expert_kernel(xg_ref, wg_hbm, wu_hbm, wd_hbm, o_ref,
                   x_ref, g_ref, hact_ref, wg_full, wu_full, wd_full, wg_buf, wu_buf, wd_buf, sem,
                   *, layer, e_offset, n_ftiles):
    e = pl.program_id(0)
    m = pl.program_id(1)
    n_e = pl.num_programs(0)
    nb, cap, width = xg_ref.shape[1], xg_ref.shape[2], xg_ref.shape[3]
    half = width - LANES
    d_model = 2 * half
    rows = nb * cap
    tf = EXPERT_FTILE
    assert n_ftiles % EXPERT_WBUFS == 0

    def tile_copies(expert, f):
        ge = e_offset + expert
        slot = f % EXPERT_WBUFS
        return (
            pltpu.make_async_copy(wg_hbm.at[layer, ge, :, pl.ds(f * tf, tf)], wg_buf.at[slot], sem.at[0, slot]),
            pltpu.make_async_copy(wu_hbm.at[layer, ge, :, pl.ds(f * tf, tf)], wu_buf.at[slot], sem.at[1, slot]),
            pltpu.make_async_copy(wd_hbm.at[layer, ge, pl.ds(f * tf, tf), :], wd_buf.at[slot], sem.at[2, slot]),
        )

    def start(expert, f):
        for cp in tile_copies(expert, f):
            cp.start()

    @pl.when((e == 0) & (m == 0))
    def _():
        for f in range(EXPERT_WBUFS):
            start(e, f)

    for i in range(nb):
        r0 = i * cap
        hi, lo = _unpack_row_words(xg_ref[0, i])
        x_ref[r0:r0 + cap, 0:half] = hi[:, 0:half].astype(BF16)
        x_ref[r0:r0 + cap, half:d_model] = lo[:, 0:half].astype(BF16)
        g_ref[r0:r0 + cap, :] = jnp.broadcast_to(
            _unpack_gate(hi[:, half:width], e_offset + e), (cap, LANES))

    def receive(f):
        slot = f % EXPERT_WBUFS
        for cp in tile_copies(e, f):
            cp.wait()
        wg_full[:, f * tf:(f + 1) * tf] = wg_buf[slot].astype(BF16)
        wu_full[:, f * tf:(f + 1) * tf] = wu_buf[slot].astype(BF16)
        wd_full[f * tf:(f + 1) * tf, :] = wd_buf[slot].astype(BF16)
        ahead = f + EXPERT_WBUFS
        if ahead < n_ftiles:
            start(e, ahead)
        else:
            @pl.when(e + 1 < n_e)
            def _():
                start(e + 1, ahead - n_ftiles)

    def body(first_group):
        x = x_ref[...]
        for f in range(n_ftiles):
            if first_group:
                receive(f)
            a = _dot(x, wg_full[:, f * tf:(f + 1) * tf])
            u = _dot(x, wu_full[:, f * tf:(f + 1) * tf])
            hact_ref[:, f * tf:(f + 1) * tf] = (a * jax.nn.sigmoid(a) * u).astype(BF16)
        for c in range(0, d_model, COL_CHUNK):
            y = _dot(hact_ref[...], wd_full[:, c:c + COL_CHUNK])
            for j in range(0, COL_CHUNK, LANES):
                o_ref[:, 0, :, c + j:c + j + LANES] = (
                    (y[:, j:j + LANES] * g_ref[...]).astype(BF16).reshape(nb, cap, LANES))

    @pl.when(m == 0)
    def _():
        body(True)

    @pl.when(m != 0)
    def _():
        body(False)


def _experts(xg, w_gate, w_up, w_down, layer, e_offset):
    n_e, b, cap, width = xg.shape
    d_model = 2 * (width - LANES)
    d_expert = w_gate.shape[3]
    nb = EXPERT_ROWS // cap
    n_ftiles = d_expert // EXPERT_FTILE
    return pl.pallas_call(
        functools.partial(_expert_kernel, layer=layer, e_offset=e_offset, n_ftiles=n_ftiles),
        out_shape=jax.ShapeDtypeStruct((b, n_e, cap, d_model), BF16),
        grid=(n_e, b // nb),
        in_specs=[
            pl.BlockSpec((1, nb, cap, width), lambda e, m: (e, m, 0, 0)),
            pl.BlockSpec(memory_space=pl.ANY),
            pl.BlockSpec(memory_space=pl.ANY),
            pl.BlockSpec(memory_space=pl.ANY),
        ],
        out_specs=pl.BlockSpec((nb, 1, cap, d_model), lambda e, m: (m, e, 0, 0)),
        scratch_shapes=[pltpu.VMEM((EXPERT_ROWS, d_model), BF16),
                        pltpu.VMEM((EXPERT_ROWS, LANES), F32),
                        pltpu.VMEM((EXPERT_ROWS, d_expert), BF16),
                        pltpu.VMEM((d_model, d_expert), BF16),
                        pltpu.VMEM((d_model, d_expert), BF16),
                        pltpu.VMEM((d_expert, d_model), BF16),
                        pltpu.VMEM((EXPERT_WBUFS, d_model, EXPERT_FTILE), F32),
                        pltpu.VMEM((EXPERT_WBUFS, d_model, EXPERT_FTILE), F32),
                        pltpu.VMEM((EXPERT_WBUFS, EXPERT_FTILE, d_model), F32),
                        pltpu.SemaphoreType.DMA((3, EXPERT_WBUFS))],
        compiler_params=_params(2),
        name="experts",
    )(xg, w_gate, w_up, w_down)


COMBINE_ROWS = 256
COMBINE_WINDOW = 64
MXU_DEPTH = 256
BF16_ROWS = 16


def _combine_kernel(starts_ref, aff_ref, thr_ref, need_ref, x1_ref, *rest, cap, n_experts, final_norm):
    n_y = len(rest) - (6 if final_norm else 5)
    y_hbm = rest[:n_y]
    g_ref = rest[n_y] if final_norm else None
    o_ref, post_ref, pfull_ref, ybuf, sem = rest[-5:]
    b = pl.program_id(0)
    t = pl.program_id(1)
    n_b = pl.num_programs(0)
    n_t = pl.num_programs(1)
    rows, win = COMBINE_ROWS, COMBINE_WINDOW
    experts_per_group = n_experts // n_y
    group_rows = ybuf.shape[0] // (2 * n_y)

    def ybuf_row(buf_slot, g):
        return pl.multiple_of((buf_slot * n_y + g) * group_rows, BF16_ROWS)

    n_chunks = sem.shape[1]
    chunks_per_group = n_chunks // n_y
    chunk_rows = group_rows // chunks_per_group
    slot = b % 2

    def chunk_copy(seq_idx, c, dst_slot):
        g, r = c // chunks_per_group, (c % chunks_per_group) * chunk_rows
        return pltpu.make_async_copy(y_hbm[g].at[seq_idx, pl.ds(r, chunk_rows), :],
                                     ybuf.at[pl.ds(ybuf_row(dst_slot, g) + r, chunk_rows), :],
                                     sem.at[dst_slot, c])

    @pl.when((b == 0) & (t == 0))
    def _():
        for c in range(n_chunks):
            chunk_copy(0, c, 0).start()

    @pl.when(t == 0)
    def _():
        for c in range(n_chunks):
            chunk_copy(b, c, slot).wait()
        key = pltpu.bitcast(aff_ref[...], I32)
        post_ref[...] = _select_slots(key, thr_ref[0], need_ref[0], _prefix_rows)

    for c in range(n_chunks):
        @pl.when((t == c) & (b + 1 < n_b))
        def _():
            chunk_copy(b + 1, c, 1 - slot).start()

    posm = post_ref[pl.ds(pl.multiple_of(t * rows, rows), rows), :]
    base = (b * (n_t + 1) + t) * n_experts
    wstart, ok = [], None
    for e in range(n_experts):
        first = starts_ref[base + e]
        end = starts_ref[base + n_experts + e]
        w0 = jnp.minimum((first // BF16_ROWS) * BF16_ROWS, cap - win)
        fits = end - w0 <= win
        wstart.append(w0)
        ok = fits if ok is None else jnp.logical_and(ok, fits)

    def finish(acc):
        if final_norm:
            acc = _rmsnorm_rows(acc, g_ref[...])
        o_ref[0] = acc

    @pl.when(ok)
    def _():
        per_dot, per_tile = MXU_DEPTH // win, LANES // win
        lane = lax.broadcasted_iota(I32, (1, LANES), 1)
        lane_f = lane.astype(F32)
        acc = x1_ref[0]
        for e0 in range(0, n_experts, per_dot):
            onehots, windows = [], []
            for e1 in range(e0, e0 + per_dot, per_tile):
                rel = None
                for k in range(per_tile - 1, -1, -1):
                    e = e1 + k
                    shifted = posm[:, e:e + 1] - (wstart[e] - k * win).astype(F32)
                    rel = shifted if rel is None else jnp.where(lane < (k + 1) * win, shifted, rel)
                onehots.append(jnp.where(rel == lane_f, 1.0, 0.0).astype(BF16))
            for e in range(e0, e0 + per_dot):
                g, el = divmod(e, experts_per_group)
                r0 = pl.multiple_of(ybuf_row(slot, g) + el * cap + wstart[e], BF16_ROWS)
                windows.append(ybuf[pl.ds(r0, win), :])
            acc = acc + _dot(jnp.concatenate(onehots, axis=1), jnp.concatenate(windows, axis=0))
        finish(acc)

    @pl.when(jnp.logical_not(ok))
    def _():
        slot_ids = lax.broadcasted_iota(I32, (1, cap), 1).astype(F32)
        for e in range(n_experts):
            pfull_ref[:, e * cap:(e + 1) * cap] = jnp.where(
                posm[:, e:e + 1] == slot_ids, 1.0, 0.0).astype(BF16)
        acc = x1_ref[0]
        for g in range(n_y):
            acc = acc + _dot(pfull_ref[:, g * group_rows:(g + 1) * group_rows],
                             ybuf[pl.ds(ybuf_row(slot, g), group_rows), :])
        finish(acc)


def _combine(starts, aff2d, thr_row, need_row, x1_3d, y_groups, cap, n_experts, final_g=None):
    b, seq, d_model = x1_3d.shape
    assert sum(y.shape[1] for y in y_groups) == n_experts * cap and cap >= COMBINE_WINDOW
    final_norm = final_g is not None
    n_tiles = seq // COMBINE_ROWS
    group_rows = y_groups[0].shape[1]
    assert all(y.shape[1] == group_rows for y in y_groups)
    assert n_tiles % len(y_groups) == 0 and group_rows % (n_tiles // len(y_groups)) == 0
    y_specs = [pl.BlockSpec(memory_space=pl.ANY) for _ in y_groups]
    g_specs = [pl.BlockSpec((1, d_model), lambda i, t, s: (0, 0))] if final_norm else []
    g_args = [final_g] if final_norm else []
    return pl.pallas_call(
        functools.partial(_combine_kernel, cap=cap, n_experts=n_experts, final_norm=final_norm),
        out_shape=jax.ShapeDtypeStruct((b, seq, d_model), F32),
        grid_spec=pltpu.PrefetchScalarGridSpec(
            num_scalar_prefetch=1,
            grid=(b, n_tiles),
            in_specs=[
                pl.BlockSpec((seq, LANES), lambda i, t, s: (i, 0)),
                pl.BlockSpec((1, 1, LANES), lambda i, t, s: (i, 0, 0)),
                pl.BlockSpec((1, 1, LANES), lambda i, t, s: (i, 0, 0)),
                pl.BlockSpec((1, COMBINE_ROWS, d_model), lambda i, t, s: (i, t, 0)),
            ] + y_specs + g_specs,
            out_specs=pl.BlockSpec((1, COMBINE_ROWS, d_model), lambda i, t, s: (i, t, 0)),
            scratch_shapes=[pltpu.VMEM((seq, LANES), F32),
                            pltpu.VMEM((COMBINE_ROWS, n_experts * cap), BF16),
                            pltpu.VMEM((2 * len(y_groups) * group_rows, d_model), BF16),
                            pltpu.SemaphoreType.DMA((2, n_tiles))],
        ),
        compiler_params=_params(2),
        name="combine",
    )(starts, aff2d, thr_row, need_row, x1_3d, *y_groups, *g_args)


EXPERT_GROUPS = 2


def _moe(afft, aff2d, h_words, x1_2d, w_gate, w_up, w_down, layer, b, seq, final_g=None):
    n_experts = afft.shape[1]
    assert n_experts <= GATE_GROUP and n_experts % EXPERT_GROUPS == 0
    d_model = x1_2d.shape[1]
    cap = CAPACITY_FACTOR * seq // n_experts
    n_e = n_experts // EXPERT_GROUPS
    posm, thr, need, tile_starts = _route(afft, cap)
    starts = jnp.concatenate([jnp.swapaxes(tile_starts, 1, 2),
                              jnp.full((b, 1, n_experts), cap, I32)], axis=1).reshape(-1)
    posm2d = posm.reshape(b * n_experts, seq)
    xgs = [_sc_expert_gather(posm2d, h_words, n_experts, g * n_e, n_e, b, seq, cap,
                             worker_share=1 if g == EXPERT_GROUPS - 1 else SC_BACKGROUND_SHARE)
           for g in range(EXPERT_GROUPS)]
    ys = [_experts(xg, w_gate, w_up, w_down, layer, g * n_e).reshape(b, n_e * cap, d_model)
          for g, xg in enumerate(xgs)]
    pad = LANES - n_experts
    thr_row = jnp.pad(thr.reshape(b, 1, n_experts), ((0, 0), (0, 0), (0, pad)),
                      constant_values=np.iinfo(np.int32).max)
    need_row = jnp.pad(need.reshape(b, 1, n_experts), ((0, 0), (0, 0), (0, pad)))
    return _combine(starts, aff2d, thr_row, need_row, x1_2d.reshape(b, seq, d_model), ys, cap,
                    n_experts, final_g)


def _rotary_tile(t, cos, sin_lo, sin_hi):
    half = ROT_DIM // 2
    return t * cos + pltpu.roll(t, LANES - half, 1) * sin_lo + pltpu.roll(t, half, 1) * sin_hi


def _inproj_attn_kernel(x_ref, g_ref, w_ref, pos_ref, rot_ref,
                        q_ref, k_ref, v_ref, qm_ref, *, tok_width, kv_width):
    qscale = HEAD_DIM ** -0.5
    rows = x_ref.shape[0] // INPROJ_PARTS

    def prepare(part):
        r = slice(part * rows, (part + 1) * rows)
        hn = _rmsnorm_rows(x_ref[r, :], g_ref[...]).astype(BF16)
        ang = pos_ref[r, :].astype(F32) * rot_ref[0:1, :]
        cos = jnp.cos(ang)
        sin = jnp.sin(ang)
        return hn, cos, sin * rot_ref[1:2, :], sin * rot_ref[2:3, :]

    def project(part, hn, cos, sin_lo, sin_hi):
        r = slice(part * rows, (part + 1) * rows)
        for c in range(0, tok_width, COL_CHUNK):
            pc = _dot(hn, w_ref[:, c:c + COL_CHUNK])
            for j in range(0, COL_CHUNK, LANES):
                rot = _rotary_tile(pc[:, j:j + LANES], cos, sin_lo, sin_hi)
                q_ref[r, c + j:c + j + LANES] = (rot * qscale).astype(BF16)
        kv = _dot(hn, w_ref[:, tok_width:tok_width + 2 * kv_width])
        k01 = _rotary_tile(kv[:, 0:LANES], cos, sin_lo, sin_hi)
        k2x = _rotary_tile(kv[:, LANES:2 * LANES], cos, sin_lo, sin_hi)
        k_ref[0, 0, r, :] = k01[:, 0:HEAD_DIM].astype(BF16)
        k_ref[0, 1, r, :] = k01[:, HEAD_DIM:LANES].astype(BF16)
        k_ref[0, 2, r, :] = k2x[:, 0:HEAD_DIM].astype(BF16)
        for hh in range(kv_width // HEAD_DIM):
            lo = kv_width + hh * HEAD_DIM
            v_ref[0, hh, r, :] = kv[:, lo:lo + HEAD_DIM].astype(BF16)
        qm_ref[r, :] = _dot(hn, w_ref[:, tok_width + 2 * kv_width:]).astype(BF16)

    prepared = [prepare(part) for part in range(INPROJ_PARTS)]
    for part in range(INPROJ_PARTS):
        project(part, *prepared[part])


def _inproj_attn(x2d, g, w_bf16, pos2d, rot_rows, tok_width, kv_width, seq):
    t, d_model = x2d.shape
    n = w_bf16.shape[1]
    n_kv = kv_width // HEAD_DIM
    assert n_kv == 3 and kv_width + HEAD_DIM == 2 * LANES
    tiles_per_seq = seq // INPROJ_ROWS
    kv_spec = pl.BlockSpec((1, n_kv, INPROJ_ROWS, HEAD_DIM),
                           lambda i: (i // tiles_per_seq, 0, i % tiles_per_seq, 0))
    return pl.pallas_call(
        functools.partial(_inproj_attn_kernel, tok_width=tok_width, kv_width=kv_width),
        out_shape=(jax.ShapeDtypeStruct((t, tok_width), BF16),
                   jax.ShapeDtypeStruct((t // seq, n_kv, seq, HEAD_DIM), BF16),
                   jax.ShapeDtypeStruct((t // seq, n_kv, seq, HEAD_DIM), BF16),
                   jax.ShapeDtypeStruct((t, n - tok_width - 2 * kv_width), BF16)),
        grid=(t // INPROJ_ROWS,),
        in_specs=[
            pl.BlockSpec((INPROJ_ROWS, d_model), lambda i: (i, 0)),
            pl.BlockSpec((1, d_model), lambda i: (0, 0)),
            _resident((d_model, n), lambda i: (0, 0)),
            pl.BlockSpec((INPROJ_ROWS, 1), lambda i: (i, 0)),
            pl.BlockSpec((8, LANES), lambda i: (0, 0)),
        ],
        out_specs=(pl.BlockSpec((INPROJ_ROWS, tok_width), lambda i: (i, 0)),
                   kv_spec, kv_spec,
                   pl.BlockSpec((INPROJ_ROWS, n - tok_width - 2 * kv_width), lambda i: (i, 0))),
        compiler_params=_params(1),
        name="inproj_attn",
    )(x2d, g, w_bf16, pos2d, rot_rows)


WATTN_QBLOCKS = 4


def _wattn_kernel(sink_ref, q_ref, kp_ref, kc_ref, kn_ref, vp_ref, vc_ref, vn_ref, o_ref,
                  valid_ref, kpad_ref, vpad_ref, s_ref, p_ref, inv_ref, *, seq):
    step = pl.program_id(1)
    n_kv = kc_ref.shape[1]
    pairs = GQA_RATIO // 2
    half_rows = pairs * BLOCK
    key_rows = (WATTN_QBLOCKS + 2) * BLOCK
    zeros = jnp.zeros((key_rows, HEAD_DIM), BF16)
    ones_col = jnp.where(lax.broadcasted_iota(I32, (key_rows, HEAD_DIM), 1) == 0, 1.0, 0.0).astype(BF16)
    low_half = lax.broadcasted_iota(I32, (1, LANES), 1) < HEAD_DIM
    for hk in range(n_kv):
        kw = jnp.concatenate([kp_ref[0, hk], kc_ref[0, hk], kn_ref[0, hk]], axis=0)
        vw = jnp.concatenate([vp_ref[0, hk], vc_ref[0, hk], vn_ref[0, hk]], axis=0)
        kpad_ref[2 * hk] = jnp.concatenate([kw, zeros], axis=1)
        kpad_ref[2 * hk + 1] = jnp.concatenate([zeros, kw], axis=1)
        vpad_ref[2 * hk] = jnp.concatenate([vw, ones_col], axis=1)
        vpad_ref[2 * hk + 1] = jnp.concatenate([ones_col, vw], axis=1)

    qi = lax.broadcasted_iota(I32, (BLOCK, 3 * BLOCK), 0)
    kj = lax.broadcasted_iota(I32, (BLOCK, 3 * BLOCK), 1)
    for qb in range(WATTN_QBLOCKS):
        n = step * WATTN_QBLOCKS + qb
        k0 = qb * BLOCK
        first = jnp.maximum(qi, BLOCK - n * BLOCK)
        last = jnp.minimum(qi + 2 * WINDOW, seq + BLOCK - 1 - n * BLOCK)
        valid_ref[qb] = jnp.where(((kj - first) | (last - kj)) >= 0, 1.0, 0.0)
        q0 = qb * BLOCK
        for hk in range(n_kv):
            tile0 = hk * pairs
            qs = jnp.concatenate(
                [q_ref[0, q0:q0 + BLOCK, (tile0 + j) * LANES:(tile0 + j + 1) * LANES]
                 for j in range(pairs)], axis=0)
            s_ref[qb, hk, 0:half_rows, :] = _dot_nt(qs, kpad_ref[2 * hk, k0:k0 + 3 * BLOCK, :])
            s_ref[qb, hk, half_rows:2 * half_rows, :] = _dot_nt(
                qs, kpad_ref[2 * hk + 1, k0:k0 + 3 * BLOCK, :])
        for hk in range(n_kv):
            for c in range(GQA_RATIO):
                j, odd = c % pairs, c // pairs
                r = c * BLOCK
                s = jnp.concatenate([
                    jnp.where(valid_ref[qb, :, 0:BLOCK] > 0.5, s_ref[qb, hk, r:r + BLOCK, 0:BLOCK], NEG_INF),
                    s_ref[qb, hk, r:r + BLOCK, BLOCK:2 * BLOCK],
                    jnp.where(valid_ref[qb, :, 2 * BLOCK:] > 0.5,
                              s_ref[qb, hk, r:r + BLOCK, 2 * BLOCK:], NEG_INF),
                ], axis=1)
                sk = sink_ref[hk * GQA_RATIO + 2 * j + odd]
                m = jnp.maximum(jnp.max(s, axis=-1, keepdims=True), sk)
                p_ref[qb, hk, r:r + BLOCK, :] = jnp.exp(s - m).astype(BF16)
                inv_ref[qb, hk, j * BLOCK:(j + 1) * BLOCK, odd * HEAD_DIM:(odd + 1) * HEAD_DIM] = (
                    jnp.broadcast_to(jnp.exp(sk - m), (BLOCK, HEAD_DIM)))
        for hk in range(n_kv):
            pv_even = _dot(p_ref[qb, hk, 0:half_rows, :], vpad_ref[2 * hk, k0:k0 + 3 * BLOCK, :])
            pv_odd = _dot(p_ref[qb, hk, half_rows:2 * half_rows, :],
                          vpad_ref[2 * hk + 1, k0:k0 + 3 * BLOCK, :])
            den = (jnp.where(low_half, pv_even[:, HEAD_DIM:HEAD_DIM + 1], pv_odd[:, 0:1])
                   + inv_ref[qb, hk])
            o = jnp.where(low_half, pv_even, pv_odd) / den
            for j in range(pairs):
                lo = (hk * pairs + j) * LANES
                o_ref[0, q0:q0 + BLOCK, lo:lo + LANES] = o[j * BLOCK:(j + 1) * BLOCK].astype(BF16)


def _window_attention(sink, q3d, k4d, v4d):
    b, seq, tok_width = q3d.shape
    n_kv = k4d.shape[1]
    nb = seq // BLOCK
    qb = WATTN_QBLOCKS
    assert nb % qb == 0
    edge_block = (1, n_kv, BLOCK, HEAD_DIM)
    prev_spec = pl.BlockSpec(edge_block, lambda i, s: (i, 0, jnp.maximum(s * qb - 1, 0), 0))
    cur_spec = pl.BlockSpec((1, n_kv, qb * BLOCK, HEAD_DIM), lambda i, s: (i, 0, s, 0))
    next_spec = pl.BlockSpec(edge_block, lambda i, s: (i, 0, jnp.minimum(s * qb + qb, nb - 1), 0))
    key_rows = (qb + 2) * BLOCK
    return pl.pallas_call(
        functools.partial(_wattn_kernel, seq=seq),
        out_shape=jax.ShapeDtypeStruct((b, seq, tok_width), BF16),
        grid=(b, nb // qb),
        in_specs=[
            pl.BlockSpec(memory_space=pltpu.SMEM),
            pl.BlockSpec((1, qb * BLOCK, tok_width), lambda i, s: (i, s, 0)),
            prev_spec, cur_spec, next_spec, prev_spec, cur_spec, next_spec,
        ],
        out_specs=pl.BlockSpec((1, qb * BLOCK, tok_width), lambda i, s: (i, s, 0)),
        scratch_shapes=[pltpu.VMEM((qb, BLOCK, 3 * BLOCK), F32),
                        pltpu.VMEM((2 * n_kv, key_rows, LANES), BF16),
                        pltpu.VMEM((2 * n_kv, key_rows, LANES), BF16),
                        pltpu.VMEM((qb, n_kv, GQA_RATIO * BLOCK, 3 * BLOCK), F32),
                        pltpu.VMEM((qb, n_kv, GQA_RATIO * BLOCK, 3 * BLOCK), BF16),
                        pltpu.VMEM((qb, n_kv, GQA_RATIO // 2 * BLOCK, LANES), F32)],
        compiler_params=_params(2),
        name="window_attention",
    )(sink, q3d, k4d, k4d, k4d, v4d, v4d, v4d)


def _rotary_rows(dtype=F32):
    half = ROT_DIM // 2
    inv_freq = ROPE_THETA ** (-jnp.arange(0, ROT_DIM, 2, dtype=jnp.float32) / ROT_DIM)
    lane = np.arange(LANES) % HEAD_DIM
    rotated = lane < ROT_DIM
    freq = jnp.where(jnp.asarray(rotated), inv_freq[jnp.asarray(lane % half)], 0.0)
    rows = jnp.zeros((8, LANES), dtype)
    rows = rows.at[0].set(freq)
    rows = rows.at[1].set(jnp.asarray(np.where(lane < half, -1.0, 0.0), dtype))
    rows = rows.at[2].set(jnp.asarray(np.where(rotated & (lane >= half), 1.0, 0.0), dtype))
    return rows


def kernel(x, mem, positions, norm_mix_g, norm_ffn_g, mem_norm_g, final_g, mem_w_kv,
           pool_w_in, pool_group_w, pool_scale, pool_w_out,
           attn_w_in, attn_sink, attn_w_out,
           router_w, exp_w_gate, exp_w_up, exp_w_down):
    b, seq, d_model = x.shape
    depth = norm_mix_g.shape[0]
    t = b * seq
    n_experts = router_w.shape[2]
    tok_width = pool_scale.shape[1]
    n_groups = pool_group_w.shape[1]
    kv_width = (attn_w_in.shape[2] - tok_width - XA_WIDTH) // 2
    assert seq % ROW_TILE == 0 and seq % INPROJ_ROWS == 0 and mem.shape[1] == MEM_LEN

    memkv = _memkv(mem.reshape(b * MEM_LEN, d_model), mem_norm_g.reshape(1, d_model),
                   mem_w_kv.astype(BF16))
    memkv = memkv.reshape(depth, b, MEM_LEN, 2 * XA_WIDTH)
    rw_pad = jnp.pad(router_w, ((0, 0), (0, 0), (0, LANES - n_experts)))
    pos2d = positions.reshape(t, 1)
    rot_rows = _rotary_rows()

    x2d = x.reshape(t, d_model)
    for layer in range(depth):
        j = layer // 2
        g_mix = norm_mix_g[layer].reshape(1, d_model)
        if layer % 2 == 0:
            u, qm = _inproj_pool(x2d, g_mix, pool_w_in[j].astype(BF16), tok_width)
            tok = _pool_mixer(u.reshape(b, seq, tok_width), pool_group_w[j].astype(BF16),
                              pool_scale[j].reshape(n_groups, 1, tok_width // n_groups))
            tok = tok.reshape(t, tok_width)
            w_out = pool_w_out[j]
        else:
            q, k, v, qm = _inproj_attn(x2d, g_mix, attn_w_in[j].astype(BF16), pos2d, rot_rows,
                                       tok_width, kv_width, seq)
            tok = _window_attention(attn_sink[j], q.reshape(b, seq, tok_width), k, v)
            tok = tok.reshape(t, tok_width)
            w_out = attn_w_out[j]
        mo = _mem_xattn(qm, memkv, layer, seq)
        x1, h, afft, aff = _outproj(tok, mo, x2d, w_out.astype(BF16),
                                    norm_ffn_g[layer].reshape(1, d_model), rw_pad, layer,
                                    n_experts, seq)
        last = layer == depth - 1
        x2 = _moe(afft, aff, h, x1, exp_w_gate, exp_w_up, exp_w_down, layer, b, seq,
                  final_g.reshape(1, d_model) if last else None)
        x2d = x2.reshape(t, d_model)
    return x2d.reshape(b, seq, d_model)
```

```python
import functools

import jax
import jax.numpy as jnp
import numpy as np
from jax import lax
from jax.experimental import pallas as pl
from jax.experimental.pallas import tpu as pltpu
from jax.experimental.pallas import tpu_sc as plsc

F32 = jnp.float32
BF16 = jnp.bfloat16
I32 = jnp.int32
U32 = jnp.uint32

EPS = 1e-6
MEM_LEN = 256
XA_HEADS = 4
XA_HEAD_DIM = 128
XA_WIDTH = XA_HEADS * XA_HEAD_DIM
POOL_WINDOWS = (2, 4, 8, 16)
HEAD_DIM = 64
GQA_RATIO = 8
WINDOW = 128
BLOCK = 128
ROPE_THETA = 500000.0
ROT_DIM = 16
NEG_INF = -1e30
CAPACITY_FACTOR = 2

LANES = 128
SUBLANES = 8
MIB = 1024 * 1024
VMEM_LIMIT_BYTES = 56 * MIB

ROW_TILE = 512
INPROJ_ROWS = 1024
INPROJ_PARTS = 4
POOL_PAD = 16
PREFIX_CHUNK = 256
COL_CHUNK = 512
F32_KEY_BITS = 31


def _params(n_grid_dims):
    return pltpu.CompilerParams(
        dimension_semantics=("arbitrary",) * n_grid_dims,
        vmem_limit_bytes=VMEM_LIMIT_BYTES,
    )


def _resident(block_shape, index_map):
    return pl.BlockSpec(block_shape, index_map, pipeline_mode=pl.Buffered(1))


def _rmsnorm_rows(x, g):
    return x * lax.rsqrt(jnp.mean(x * x, axis=-1, keepdims=True) + EPS) * g


def _dot(a, b):
    return jnp.dot(a, b, preferred_element_type=F32)


def _dot_nt(a, b):
    return lax.dot_general(a, b, (((1,), (1,)), ((), ())), preferred_element_type=F32)


def _memkv_kernel(mem_ref, g_ref, w_ref, o_ref):
    hn = _rmsnorm_rows(mem_ref[...], g_ref[...]).astype(BF16)
    o_ref[0] = _dot(hn, w_ref[0]).astype(BF16)


def _memkv(mem2d, g, w_bf16):
    depth, d_model, n = w_bf16.shape
    rows = mem2d.shape[0]
    return pl.pallas_call(
        _memkv_kernel,
        out_shape=jax.ShapeDtypeStruct((depth, rows, n), BF16),
        grid=(depth, rows // ROW_TILE),
        in_specs=[
            pl.BlockSpec((ROW_TILE, d_model), lambda l, i: (i, 0)),
            pl.BlockSpec((1, d_model), lambda l, i: (0, 0)),
            pl.BlockSpec((1, d_model, n), lambda l, i: (l, 0, 0)),
        ],
        out_specs=pl.BlockSpec((1, ROW_TILE, n), lambda l, i: (l, i, 0)),
        compiler_params=_params(2),
        name="memkv",
    )(mem2d, g, w_bf16)


def _inproj_pool_kernel(x_ref, g_ref, w_ref, u_ref, qm_ref, *, tok_width):
    hn = _rmsnorm_rows(x_ref[...], g_ref[...]).astype(BF16)
    for c in range(0, tok_width, COL_CHUNK):
        u_ref[:, c:c + COL_CHUNK] = _dot(hn, w_ref[:, c:c + COL_CHUNK])
    qm_ref[...] = _dot(hn, w_ref[:, tok_width:]).astype(BF16)


def _inproj_pool(x2d, g, w_bf16, tok_width):
    t, d_model = x2d.shape
    n = w_bf16.shape[1]
    return pl.pallas_call(
        functools.partial(_inproj_pool_kernel, tok_width=tok_width),
        out_shape=(jax.ShapeDtypeStruct((t, tok_width), F32),
                   jax.ShapeDtypeStruct((t, n - tok_width), BF16)),
        grid=(t // INPROJ_ROWS,),
        in_specs=[
            pl.BlockSpec((INPROJ_ROWS, d_model), lambda i: (i, 0)),
            pl.BlockSpec((1, d_model), lambda i: (0, 0)),
            _resident((d_model, n), lambda i: (0, 0)),
        ],
        out_specs=(pl.BlockSpec((INPROJ_ROWS, tok_width), lambda i: (i, 0)),
                   pl.BlockSpec((INPROJ_ROWS, n - tok_width), lambda i: (i, 0))),
        compiler_params=_params(1),
        name="inproj_pool",
    )(x2d, g, w_bf16)


def _pool_group(u_ref, gw_ref, sc_ref, o_ref, *, window, seq):
    gwid = u_ref.shape[2]
    rows = seq + 2 * POOL_PAD
    half = window // 2
    assert 2 * half <= POOL_PAD
    zeros_pad = jnp.zeros((POOL_PAD, gwid), F32)
    u = u_ref[0]
    p = jnp.concatenate([zeros_pad, u, zeros_pad], axis=0)
    k = 1
    while k < half:
        p = p + pltpu.roll(p, rows - k, 0)
        k *= 2
    before = p if half % SUBLANES == 0 else pltpu.roll(p, half, 0)
    shift = half if half % SUBLANES == 0 else 0
    win = before[POOL_PAD - shift:POOL_PAD - shift + seq, :] + p[POOL_PAD:POOL_PAD + seq, :]
    t = lax.broadcasted_iota(I32, (seq, 1), 0)
    lo = jnp.maximum(t - half, 0)
    hi = jnp.minimum(t + half - 1, seq - 1)
    cnt = (hi - lo + 1).astype(F32)
    pooled = (win / cnt - u).astype(BF16)
    o_ref[0] = (_dot(pooled, gw_ref[0]) * sc_ref[0]).astype(BF16)


def _pool_kernel(u_ref, gw_ref, sc_ref, o_ref, *, seq):
    g = pl.program_id(1)
    for k, window in enumerate(POOL_WINDOWS):
        @pl.when(g == k)
        def _():
            _pool_group(u_ref, gw_ref, sc_ref, o_ref, window=window, seq=seq)


def _pool_mixer(u3d, gw_bf16, scale3d):
    b, seq, tok_width = u3d.shape
    n_groups, gwid, _ = gw_bf16.shape
    assert n_groups == len(POOL_WINDOWS) and n_groups * gwid == tok_width
    return pl.pallas_call(
        functools.partial(_pool_kernel, seq=seq),
        out_shape=jax.ShapeDtypeStruct((b, seq, tok_width), BF16),
        grid=(b, n_groups),
        in_specs=[
            pl.BlockSpec((1, seq, gwid), lambda i, g: (i, 0, g)),
            pl.BlockSpec((1, gwid, gwid), lambda i, g: (g, 0, 0)),
            pl.BlockSpec((1, 1, gwid), lambda i, g: (g, 0, 0)),
        ],
        out_specs=pl.BlockSpec((1, seq, gwid), lambda i, g: (i, 0, g)),
        compiler_params=_params(2),
        name="pool_mixer",
    )(u3d, gw_bf16, scale3d)


def _xattn_kernel(q_ref, kv_ref, o_ref):
    scale = XA_HEAD_DIM ** -0.5
    for h in range(XA_HEADS):
        lo = h * XA_HEAD_DIM
        q = q_ref[:, lo:lo + XA_HEAD_DIM]
        k = kv_ref[0, 0, :, lo:lo + XA_HEAD_DIM]
        v = kv_ref[0, 0, :, XA_WIDTH + lo:XA_WIDTH + lo + XA_HEAD_DIM]
        s = _dot_nt(q, k) * scale
        m = jnp.max(s, axis=-1, keepdims=True)
        p = jnp.exp(s - m)
        den = jnp.sum(p, axis=-1, keepdims=True)
        o_ref[:, lo:lo + XA_HEAD_DIM] = (_dot(p.astype(BF16), v) / den).astype(BF16)


XATTN_ROWS = 1024


def _mem_xattn(qm2d, memkv, layer, seq):
    t = qm2d.shape[0]
    tiles_per_seq = seq // XATTN_ROWS
    return pl.pallas_call(
        _xattn_kernel,
        out_shape=jax.ShapeDtypeStruct((t, XA_WIDTH), BF16),
        grid=(t // XATTN_ROWS,),
        in_specs=[
            pl.BlockSpec((XATTN_ROWS, XA_WIDTH), lambda i: (i, 0)),
            pl.BlockSpec((1, 1, MEM_LEN, 2 * XA_WIDTH), lambda i: (layer, i // tiles_per_seq, 0, 0)),
        ],
        out_specs=pl.BlockSpec((XATTN_ROWS, XA_WIDTH), lambda i: (i, 0)),
        compiler_params=_params(1),
        name="mem_xattn",
    )(qm2d, memkv)


def _outproj_kernel(tok_ref, mo_ref, x_ref, w_ref, g_ref, rw_ref,
                    x1_ref, h_ref, afft_ref, aff_ref, wcat_ref, x1prev_ref,
                    *, tok_width, n_experts):
    @pl.when(pl.program_id(0) == 0)
    def _():
        rw = rw_ref[0]
        w_hi = rw.astype(BF16)
        wcat_ref[:, 0:LANES] = w_hi
        wcat_ref[:, LANES:2 * LANES] = (rw - w_hi.astype(F32)).astype(BF16)
        x1prev_ref[...] = jnp.zeros_like(x1prev_ref)

    hn = _rmsnorm_rows(x1prev_ref[...], g_ref[...])
    h_prev = hn.astype(BF16)
    h_lo = (hn - h_prev.astype(F32)).astype(BF16)
    r = _dot(h_prev, wcat_ref[...]) + _dot(h_lo, wcat_ref[...])
    logits = r[:, 0:LANES] + r[:, LANES:2 * LANES]
    lt = logits.T[0:n_experts, :]
    m = jnp.max(lt, axis=0, keepdims=True)
    ex = jnp.exp(lt - m)
    afft = ex / jnp.sum(ex, axis=0, keepdims=True)
    afft_ref[0] = afft
    padded = jnp.concatenate(
        [afft, jnp.zeros((LANES - n_experts, afft.shape[1]), F32)], axis=0)
    aff = padded.T
    aff_ref[...] = aff
    h_ref[...] = _pack_row_words(h_prev, _pack_gate_lanes(aff))

    y = _dot(tok_ref[...], w_ref[0:tok_width, :]) + _dot(mo_ref[...], w_ref[tok_width:, :])
    x1 = x_ref[...] + y
    x1_ref[...] = x1
    x1prev_ref[...] = x1


def _outproj(tok2d, mo2d, x2d, w_bf16, g, rw_pad, layer, n_experts, seq):
    t, d_model = x2d.shape
    tok_width = tok2d.shape[1]
    tiles_per_seq = seq // ROW_TILE
    n_tiles = t // ROW_TILE

    def cur(i):
        return jnp.minimum(i, n_tiles - 1)

    def prev(i):
        return jnp.maximum(i - 1, 0)

    return pl.pallas_call(
        functools.partial(_outproj_kernel, tok_width=tok_width, n_experts=n_experts),
        out_shape=(jax.ShapeDtypeStruct((t, d_model), F32),
                   jax.ShapeDtypeStruct((t, d_model // 2 + LANES), U32),
                   jax.ShapeDtypeStruct((t // seq, n_experts, seq), F32),
                   jax.ShapeDtypeStruct((t, LANES), F32)),
        grid=(n_tiles + 1,),
        in_specs=[
            pl.BlockSpec((ROW_TILE, tok_width), lambda i: (cur(i), 0)),
            pl.BlockSpec((ROW_TILE, mo2d.shape[1]), lambda i: (cur(i), 0)),
            pl.BlockSpec((ROW_TILE, d_model), lambda i: (cur(i), 0)),
            _resident((d_model, d_model), lambda i: (0, 0)),
            pl.BlockSpec((1, d_model), lambda i: (0, 0)),
            _resident((1, d_model, LANES), lambda i: (layer, 0, 0)),
        ],
        out_specs=(pl.BlockSpec((ROW_TILE, d_model), lambda i: (cur(i), 0)),
                   pl.BlockSpec((ROW_TILE, d_model // 2 + LANES), lambda i: (prev(i), 0)),
                   pl.BlockSpec((1, n_experts, ROW_TILE),
                                lambda i: (prev(i) // tiles_per_seq, 0, prev(i) % tiles_per_seq)),
                   pl.BlockSpec((ROW_TILE, LANES), lambda i: (prev(i), 0))),
        scratch_shapes=[pltpu.VMEM((d_model, 2 * LANES), BF16),
                        pltpu.VMEM((ROW_TILE, d_model), F32)],
        compiler_params=_params(1),
        name="outproj_router",
    )(tok2d, mo2d, x2d, w_bf16, g, rw_pad)


def _strict_triangle(n, lower):
    r = lax.broadcasted_iota(I32, (n, n), 0)
    c = lax.broadcasted_iota(I32, (n, n), 1)
    return jnp.where((c < r) if lower else (r < c), 1.0, 0.0).astype(BF16)


def _prefix_rows(mask_f32):
    s, l = mask_f32.shape
    tri = _strict_triangle(PREFIX_CHUNK, lower=True)
    carry = jnp.zeros((1, l), F32)
    out = []
    for c in range(0, s, PREFIX_CHUNK):
        m = mask_f32[c:c + PREFIX_CHUNK, :]
        out.append(_dot(tri, m.astype(BF16)) + carry)
        carry = carry + jnp.sum(m, axis=0, keepdims=True)
    return jnp.concatenate(out, axis=0)


def _prefix_lanes(mask_f32):
    e, s = mask_f32.shape
    tri = _strict_triangle(PREFIX_CHUNK, lower=False)
    carry = jnp.zeros((e, 1), F32)
    out = []
    for c in range(0, s, PREFIX_CHUNK):
        m = mask_f32[:, c:c + PREFIX_CHUNK]
        out.append(_dot(m.astype(BF16), tri) + carry)
        carry = carry + jnp.sum(m, axis=1, keepdims=True)
    return jnp.concatenate(out, axis=1)


def _select_slots(key, thr, need, prefix_fn):
    return _select_slots_and_counts(key, thr, need, prefix_fn)[0]


def _select_slots_and_counts(key, thr, need, prefix_fn):
    gt = jnp.where(key > thr, 1.0, 0.0)
    eq = jnp.where(key == thr, 1.0, 0.0)
    eq_rank = prefix_fn(eq)
    sel = gt + eq * jnp.where(eq_rank < need, 1.0, 0.0)
    pos = prefix_fn(sel)
    return jnp.where(sel > 0.5, pos, -1.0), pos


GATE_GROUP = 16
GATE_PIECES = 3


def _pack_gate_lanes(aff):
    hi = aff.astype(BF16).astype(F32)
    r1 = aff - hi
    mid = r1.astype(BF16).astype(F32)
    lo = (r1 - mid).astype(BF16).astype(F32)
    packed = hi + pltpu.roll(mid, GATE_GROUP, 1) + pltpu.roll(lo, 2 * GATE_GROUP, 1)
    return packed.astype(BF16)


def _unpack_gate(tail, e):
    lane = lax.broadcasted_iota(I32, (1, LANES), 1)
    mine = ((lane & (GATE_GROUP - 1)) == e) & (lane < GATE_PIECES * GATE_GROUP)
    return jnp.sum(jnp.where(mine, tail.astype(F32), 0.0), axis=1, keepdims=True)


def _pack_row_words(h, gate_tile):
    rows, d_model = h.shape
    half = d_model // 2
    hi = jnp.concatenate([h[:, 0:half], gate_tile], axis=1).astype(F32)
    lo = jnp.concatenate([h[:, half:], jnp.zeros((rows, LANES), BF16)], axis=1).astype(F32)
    return pltpu.bitcast(hi, U32) | (pltpu.bitcast(lo, U32) >> 16)


def _unpack_row_words(words):
    hi = pltpu.bitcast(words & jnp.uint32(0xFFFF0000), F32)
    lo = pltpu.bitcast(words << 16, F32)
    return hi, lo


def _route_kernel(afft_ref, posm_ref, thr_ref, need_ref, starts_ref, *, cap):
    n_experts, seq = afft_ref.shape[1], afft_ref.shape[2]
    key = pltpu.bitcast(afft_ref[0], I32)
    thr = jnp.zeros((n_experts, 1), I32)
    for bit in range(F32_KEY_BITS - 1, -1, -1):
        cand = thr | (1 << bit)
        cnt = jnp.sum(jnp.where(key >= cand, 1.0, 0.0), axis=1, keepdims=True)
        thr = jnp.where(cnt >= cap, cand, thr)
    n_gt = jnp.sum(jnp.where(key > thr, 1.0, 0.0), axis=1, keepdims=True)
    need = cap - n_gt
    thr_ref[0] = thr
    need_ref[0] = need
    posm, before = _select_slots_and_counts(key, thr, need, _prefix_lanes)
    posm_ref[0] = posm.astype(I32)
    starts_ref[0] = jnp.concatenate(
        [before[:, r:r + 1] for r in range(0, seq, COMBINE_ROWS)], axis=1).astype(I32)


def _route(afft, cap):
    b, n_experts, seq = afft.shape
    n_tiles = seq // COMBINE_ROWS
    rows = b * n_experts
    outs = pl.pallas_call(
        functools.partial(_route_kernel, cap=cap),
        out_shape=(jax.ShapeDtypeStruct((1, rows, seq), I32),
                   jax.ShapeDtypeStruct((1, rows, 1), I32),
                   jax.ShapeDtypeStruct((1, rows, 1), F32),
                   jax.ShapeDtypeStruct((1, rows, n_tiles), I32)),
        grid=(1,),
        in_specs=[pl.BlockSpec((1, rows, seq), lambda i: (0, 0, 0))],
        out_specs=(pl.BlockSpec((1, rows, seq), lambda i: (0, 0, 0)),
                   pl.BlockSpec((1, rows, 1), lambda i: (0, 0, 0)),
                   pl.BlockSpec((1, rows, 1), lambda i: (0, 0, 0)),
                   pl.BlockSpec((1, rows, n_tiles), lambda i: (0, 0, 0))),
        compiler_params=_params(1),
        name="expert_route",
    )(afft.reshape(1, rows, seq))
    return tuple(o.reshape(b, n_experts, o.shape[2]) for o in outs)


SC_LANES = 16
SC_GATHER_ROWS = 64
SC_BACKGROUND_SHARE = 4


def _sc_expert_gather(posm2d, h_words, n_experts, e_offset, n_e, b, seq, cap, worker_share=1):
    width = h_words.shape[1]
    info = plsc.get_sparse_core_info()
    n_cores, n_subcores = info.num_cores, info.num_subcores
    n_workers = n_cores * n_subcores // worker_share
    assert info.num_lanes == SC_LANES and (b * n_e) % n_workers == 0
    pairs_per_worker = (b * n_e) // n_workers
    mesh = plsc.VectorSubcoreMesh(core_axis_name="c", subcore_axis_name="s")

    @functools.partial(
        pl.kernel, mesh=mesh,
        out_type=jax.ShapeDtypeStruct((n_e * b * cap, width), U32),
        compiler_params=pltpu.CompilerParams(needs_layout_passes=False),
        scratch_types=[
            pltpu.VMEM((seq,), I32),
            pltpu.VMEM((cap,), I32),
            pltpu.VMEM((SC_GATHER_ROWS, width), U32),
            pltpu.SemaphoreType.DMA,
        ],
        name="sc_expert_gather",
    )
    def gather(posm_hbm, h_hbm, out_hbm, pos_v, idx_v, rows_v, sem):
        wid = lax.axis_index("s") * n_cores + lax.axis_index("c")

        def move_pair(pair):
            bi = pair // n_e
            e = pair - bi * n_e
            pltpu.sync_copy(posm_hbm.at[bi * n_experts + e_offset + e], pos_v)

            @pl.loop(0, seq, step=SC_LANES)
            def _(t0):
                slots = pos_v[pl.ds(t0, SC_LANES)]
                rows = lax.iota(I32, SC_LANES) + (t0 + bi * seq)
                plsc.store_scatter(idx_v, [slots], rows, mask=slots >= 0)

            out_base = (e * b + bi) * cap
            for c in range(cap // SC_GATHER_ROWS):
                chunk = idx_v.at[pl.ds(c * SC_GATHER_ROWS, SC_GATHER_ROWS)]
                pltpu.async_copy(h_hbm.at[chunk], rows_v, sem).wait()
                pltpu.sync_copy(rows_v, out_hbm.at[pl.ds(out_base + c * SC_GATHER_ROWS, SC_GATHER_ROWS)])

        @pl.when(wid < n_workers)
        def _():
            for p in range(pairs_per_worker):
                move_pair(wid * pairs_per_worker + p)

    return gather(posm2d, h_words).reshape(n_e, b, cap, width)


EXPERT_ROWS = 1024
EXPERT_FTILE = 256


EXPERT_WBUFS = 2


def _expert_kernel(xg_ref, wg_hbm, wu_hbm, wd_hbm, o_ref,
                   x_ref, g_ref, hact_ref, wg_full, wu_full, wd_full, wg_buf, wu_buf, wd_buf, sem,
                   *, layer, e_offset, n_ftiles):
    e = pl.program_id(0)
    m = pl.program_id(1)
    n_e = pl.num_programs(0)
    nb, cap, width = xg_ref.shape[1], xg_ref.shape[2], xg_ref.shape[3]
    half = width - LANES
    d_model = 2 * half
    rows = nb * cap
    tf = EXPERT_FTILE
    assert n_ftiles % EXPERT_WBUFS == 0

    def tile_copies(expert, f):
        ge = e_offset + expert
        slot = f % EXPERT_WBUFS
        return (
            pltpu.make_async_copy(wg_hbm.at[layer, ge, :, pl.ds(f * tf, tf)], wg_buf.at[slot], sem.at[0, slot]),
            pltpu.make_async_copy(wu_hbm.at[layer, ge, :, pl.ds(f * tf, tf)], wu_buf.at[slot], sem.at[1, slot]),
            pltpu.make_async_copy(wd_hbm.at[layer, ge, pl.ds(f * tf, tf), :], wd_buf.at[slot], sem.at[2, slot]),
        )

    def start(expert, f):
        for cp in tile_copies(expert, f):
            cp.start()

    @pl.when((e == 0) & (m == 0))
    def _():
        for f in range(EXPERT_WBUFS):
            start(e, f)

    for i in range(nb):
        r0 = i * cap
        hi, lo = _unpack_row_words(xg_ref[0, i])
        x_ref[r0:r0 + cap, 0:half] = hi[:, 0:half].astype(BF16)
        x_ref[r0:r0 + cap, half:d_model] = lo[:, 0:half].astype(BF16)
        g_ref[r0:r0 + cap, :] = jnp.broadcast_to(
            _unpack_gate(hi[:, half:width], e_offset + e), (cap, LANES))

    def receive(f):
        slot = f % EXPERT_WBUFS
        for cp in tile_copies(e, f):
            cp.wait()
        wg_full[:, f * tf:(f + 1) * tf] = wg_buf[slot].astype(BF16)
        wu_full[:, f * tf:(f + 1) * tf] = wu_buf[slot].astype(BF16)
        wd_full[f * tf:(f + 1) * tf, :] = wd_buf[slot].astype(BF16)
        ahead = f + EXPERT_WBUFS
        if ahead < n_ftiles:
            start(e, ahead)
        else:
            @pl.when(e + 1 < n_e)
            def _():
                start(e + 1, ahead - n_ftiles)

    def body(first_group):
        x = x_ref[...]
        for f in range(n_ftiles):
            if first_group:
                receive(f)
            a = _dot(x, wg_full[:, f * tf:(f + 1) * tf])
            u = _dot(x, wu_full[:, f * tf:(f + 1) * tf])
            hact_ref[:, f * tf:(f + 1) * tf] = (a * jax.nn.sigmoid(a) * u).astype(BF16)
        for c in range(0, d_model, COL_CHUNK):
            y = _dot(hact_ref[...], wd_full[:, c:c + COL_CHUNK])
            for j in range(0, COL_CHUNK, LANES):
                o_ref[:, 0, :, c + j:c + j + LANES] = (
                    (y[:, j:j + LANES] * g_ref[...]).astype(BF16).reshape(nb, cap, LANES))

    @pl.when(m == 0)
    def _():
        body(True)

    @pl.when(m != 0)
    def _():
        body(False)


def _experts(xg, w_gate, w_up, w_down, layer, e_offset):
    n_e, b, cap, width = xg.shape
    d_model = 2 * (width - LANES)
    d_expert = w_gate.shape[3]
    nb = EXPERT_ROWS // cap
    n_ftiles = d_expert // EXPERT_FTILE
    return pl.pallas_call(
        functools.partial(_expert_kernel, layer=layer, e_offset=e_offset, n_ftiles=n_ftiles),
        out_shape=jax.ShapeDtypeStruct((b, n_e, cap, d_model), BF16),
        grid=(n_e, b // nb),
        in_specs=[
            pl.BlockSpec((1, nb, cap, width), lambda e, m: (e, m, 0, 0)),
            pl.BlockSpec(memory_space=pl.ANY),
            pl.BlockSpec(memory_space=pl.ANY),
            pl.BlockSpec(memory_space=pl.ANY),
        ],
        out_specs=pl.BlockSpec((nb, 1, cap, d_model), lambda e, m: (m, e, 0, 0)),
        scratch_shapes=[pltpu.VMEM((EXPERT_ROWS, d_model), BF16),
                        pltpu.VMEM((EXPERT_ROWS, LANES), F32),
                        pltpu.VMEM((EXPERT_ROWS, d_expert), BF16),
                        pltpu.VMEM((d_model, d_expert), BF16),
                        pltpu.VMEM((d_model, d_expert), BF16),
                        pltpu.VMEM((d_expert, d_model), BF16),
                        pltpu.VMEM((EXPERT_WBUFS, d_model, EXPERT_FTILE), F32),
                        pltpu.VMEM((EXPERT_WBUFS, d_model, EXPERT_FTILE), F32),
                        pltpu.VMEM((EXPERT_WBUFS, EXPERT_FTILE, d_model), F32),
                        pltpu.SemaphoreType.DMA((3, EXPERT_WBUFS))],
        compiler_params=_params(2),
        name="experts",
    )(xg, w_gate, w_up, w_down)


COMBINE_ROWS = 256
COMBINE_WINDOW = 64
MXU_DEPTH = 256
BF16_ROWS = 16


def _combine_kernel(starts_ref, aff_ref, thr_ref, need_ref, x1_ref, *rest, cap, n_experts, final_norm):
    n_y = len(rest) - (6 if final_norm else 5)
    y_hbm = rest[:n_y]
    g_ref = rest[n_y] if final_norm else None
    o_ref, post_ref, pfull_ref, ybuf, sem = rest[-5:]
    b = pl.program_id(0)
    t = pl.program_id(1)
    n_b = pl.num_programs(0)
    n_t = pl.num_programs(1)
    rows, win = COMBINE_ROWS, COMBINE_WINDOW
    experts_per_group = n_experts // n_y
    group_rows = ybuf.shape[0] // (2 * n_y)

    def ybuf_row(buf_slot, g):
        return pl.multiple_of((buf_slot * n_y + g) * group_rows, BF16_ROWS)

    n_chunks = sem.shape[1]
    chunks_per_group = n_chunks // n_y
    chunk_rows = group_rows // chunks_per_group
    slot = b % 2

    def chunk_copy(seq_idx, c, dst_slot):
        g, r = c // chunks_per_group, (c % chunks_per_group) * chunk_rows
        return pltpu.make_async_copy(y_hbm[g].at[seq_idx, pl.ds(r, chunk_rows), :],
                                     ybuf.at[pl.ds(ybuf_row(dst_slot, g) + r, chunk_rows), :],
                                     sem.at[dst_slot, c])

    @pl.when((b == 0) & (t == 0))
    def _():
        for c in range(n_chunks):
            chunk_copy(0, c, 0).start()

    @pl.when(t == 0)
    def _():
        for c in range(n_chunks):
            chunk_copy(b, c, slot).wait()
        key = pltpu.bitcast(aff_ref[...], I32)
        post_ref[...] = _select_slots(key, thr_ref[0], need_ref[0], _prefix_rows)

    for c in range(n_chunks):
        @pl.when((t == c) & (b + 1 < n_b))
        def _():
            chunk_copy(b + 1, c, 1 - slot).start()

    posm = post_ref[pl.ds(pl.multiple_of(t * rows, rows), rows), :]
    base = (b * (n_t + 1) + t) * n_experts
    wstart, ok = [], None
    for e in range(n_experts):
        first = starts_ref[base + e]
        end = starts_ref[base + n_experts + e]
        w0 = jnp.minimum((first // BF16_ROWS) * BF16_ROWS, cap - win)
        fits = end - w0 <= win
        wstart.append(w0)
        ok = fits if ok is None else jnp.logical_and(ok, fits)

    def finish(acc):
        if final_norm:
            acc = _rmsnorm_rows(acc, g_ref[...])
        o_ref[0] = acc

    @pl.when(ok)
    def _():
        per_dot, per_tile = MXU_DEPTH // win, LANES // win
        lane = lax.broadcasted_iota(I32, (1, LANES), 1)
        lane_f = lane.astype(F32)
        acc = x1_ref[0]
        for e0 in range(0, n_experts, per_dot):
            onehots, windows = [], []
            for e1 in range(e0, e0 + per_dot, per_tile):
                rel = None
                for k in range(per_tile - 1, -1, -1):
                    e = e1 + k
                    shifted = posm[:, e:e + 1] - (wstart[e] - k * win).astype(F32)
                    rel = shifted if rel is None else jnp.where(lane < (k + 1) * win, shifted, rel)
                onehots.append(jnp.where(rel == lane_f, 1.0, 0.0).astype(BF16))
            for e in range(e0, e0 + per_dot):
                g, el = divmod(e, experts_per_group)
                r0 = pl.multiple_of(ybuf_row(slot, g) + el * cap + wstart[e], BF16_ROWS)
                windows.append(ybuf[pl.ds(r0, win), :])
            acc = acc + _dot(jnp.concatenate(onehots, axis=1), jnp.concatenate(windows, axis=0))
        finish(acc)

    @pl.when(jnp.logical_not(ok))
    def _():
        slot_ids = lax.broadcasted_iota(I32, (1, cap), 1).astype(F32)
        for e in range(n_experts):
            pfull_ref[:, e * cap:(e + 1) * cap] = jnp.where(
                posm[:, e:e + 1] == slot_ids, 1.0, 0.0).astype(BF16)
        acc = x1_ref[0]
        for g in range(n_y):
            acc = acc + _dot(pfull_ref[:, g * group_rows:(g + 1) * group_rows],
                             ybuf[pl.ds(ybuf_row(slot, g), group_rows), :])
        finish(acc)


def _combine(starts, aff2d, thr_row, need_row, x1_3d, y_groups, cap, n_experts, final_g=None):
    b, seq, d_model = x1_3d.shape
    assert sum(y.shape[1] for y in y_groups) == n_experts * cap and cap >= COMBINE_WINDOW
    final_norm = final_g is not None
    n_tiles = seq // COMBINE_ROWS
    group_rows = y_groups[0].shape[1]
    assert all(y.shape[1] == group_rows for y in y_groups)
    assert n_tiles % len(y_groups) == 0 and group_rows % (n_tiles // len(y_groups)) == 0
    y_specs = [pl.BlockSpec(memory_space=pl.ANY) for _ in y_groups]
    g_specs = [pl.BlockSpec((1, d_model), lambda i, t, s: (0, 0))] if final_norm else []
    g_args = [final_g] if final_norm else []
    return pl.pallas_call(
        functools.partial(_combine_kernel, cap=cap, n_experts=n_experts, final_norm=final_norm),
        out_shape=jax.ShapeDtypeStruct((b, seq, d_model), F32),
        grid_spec=pltpu.PrefetchScalarGridSpec(
            num_scalar_prefetch=1,
            grid=(b, n_tiles),
            in_specs=[
                pl.BlockSpec((seq, LANES), lambda i, t, s: (i, 0)),
                pl.BlockSpec((1, 1, LANES), lambda i, t, s: (i, 0, 0)),
                pl.BlockSpec((1, 1, LANES), lambda i, t, s: (i, 0, 0)),
                pl.BlockSpec((1, COMBINE_ROWS, d_model), lambda i, t, s: (i, t, 0)),
            ] + y_specs + g_specs,
            out_specs=pl.BlockSpec((1, COMBINE_ROWS, d_model), lambda i, t, s: (i, t, 0)),
            scratch_shapes=[pltpu.VMEM((seq, LANES), F32),
                            pltpu.VMEM((COMBINE_ROWS, n_experts * cap), BF16),
                            pltpu.VMEM((2 * len(y_groups) * group_rows, d_model), BF16),
                            pltpu.SemaphoreType.DMA((2, n_tiles))],
        ),
        compiler_params=_params(2),
        name="combine",
    )(starts, aff2d, thr_row, need_row, x1_3d, *y_groups, *g_args)


EXPERT_GROUPS = 2


def _moe(afft, aff2d, h_words, x1_2d, w_gate, w_up, w_down, layer, b, seq, final_g=None):
    n_experts = afft.shape[1]
    assert n_experts <= GATE_GROUP and n_experts % EXPERT_GROUPS == 0
    d_model = x1_2d.shape[1]
    cap = CAPACITY_FACTOR * seq // n_experts
    n_e = n_experts // EXPERT_GROUPS
    posm, thr, need, tile_starts = _route(afft, cap)
    starts = jnp.concatenate([jnp.swapaxes(tile_starts, 1, 2),
                              jnp.full((b, 1, n_experts), cap, I32)], axis=1).reshape(-1)
    posm2d = posm.reshape(b * n_experts, seq)
    xgs = [_sc_expert_gather(posm2d, h_words, n_experts, g * n_e, n_e, b, seq, cap,
                             worker_share=1 if g == EXPERT_GROUPS - 1 else SC_BACKGROUND_SHARE)
           for g in range(EXPERT_GROUPS)]
    ys = [_experts(xg, w_gate, w_up, w_down, layer, g * n_e).reshape(b, n_e * cap, d_model)
          for g, xg in enumerate(xgs)]
    pad = LANES - n_experts
    thr_row = jnp.pad(thr.reshape(b, 1, n_experts), ((0, 0), (0, 0), (0, pad)),
                      constant_values=np.iinfo(np.int32).max)
    need_row = jnp.pad(need.reshape(b, 1, n_experts), ((0, 0), (0, 0), (0, pad)))
    return _combine(starts, aff2d, thr_row, need_row, x1_2d.reshape(b, seq, d_model), ys, cap,
                    n_experts, final_g)


def _rotary_tile(t, cos, sin_lo, sin_hi):
    half = ROT_DIM // 2
    return t * cos + pltpu.roll(t, LANES - half, 1) * sin_lo + pltpu.roll(t, half, 1) * sin_hi


def _inproj_attn_kernel(x_ref, g_ref, w_ref, pos_ref, rot_ref,
                        q_ref, k_ref, v_ref, qm_ref, *, tok_width, kv_width):
    qscale = HEAD_DIM ** -0.5
    rows = x_ref.shape[0] // INPROJ_PARTS

    def prepare(part):
        r = slice(part * rows, (part + 1) * rows)
        hn = _rmsnorm_rows(x_ref[r, :], g_ref[...]).astype(BF16)
        ang = pos_ref[r, :].astype(F32) * rot_ref[0:1, :]
        cos = jnp.cos(ang)
        sin = jnp.sin(ang)
        return hn, cos, sin * rot_ref[1:2, :], sin * rot_ref[2:3, :]

    def project(part, hn, cos, sin_lo, sin_hi):
        r = slice(part * rows, (part + 1) * rows)
        for c in range(0, tok_width, COL_CHUNK):
            pc = _dot(hn, w_ref[:, c:c + COL_CHUNK])
            for j in range(0, COL_CHUNK, LANES):
                rot = _rotary_tile(pc[:, j:j + LANES], cos, sin_lo, sin_hi)
                q_ref[r, c + j:c + j + LANES] = (rot * qscale).astype(BF16)
        kv = _dot(hn, w_ref[:, tok_width:tok_width + 2 * kv_width])
        k01 = _rotary_tile(kv[:, 0:LANES], cos, sin_lo, sin_hi)
        k2x = _rotary_tile(kv[:, LANES:2 * LANES], cos, sin_lo, sin_hi)
        k_ref[0, 0, r, :] = k01[:, 0:HEAD_DIM].astype(BF16)
        k_ref[0, 1, r, :] = k01[:, HEAD_DIM:LANES].astype(BF16)
        k_ref[0, 2, r, :] = k2x[:, 0:HEAD_DIM].astype(BF16)
        for hh in range(kv_width // HEAD_DIM):
            lo = kv_width + hh * HEAD_DIM
            v_ref[0, hh, r, :] = kv[:, lo:lo + HEAD_DIM].astype(BF16)
        qm_ref[r, :] = _dot(hn, w_ref[:, tok_width + 2 * kv_width:]).astype(BF16)

    prepared = [prepare(part) for part in range(INPROJ_PARTS)]
    for part in range(INPROJ_PARTS):
        project(part, *prepared[part])


def _inproj_attn(x2d, g, w_bf16, pos2d, rot_rows, tok_width, kv_width, seq):
    t, d_model = x2d.shape
    n = w_bf16.shape[1]
    n_kv = kv_width // HEAD_DIM
    assert n_kv == 3 and kv_width + HEAD_DIM == 2 * LANES
    tiles_per_seq = seq // INPROJ_ROWS
    kv_spec = pl.BlockSpec((1, n_kv, INPROJ_ROWS, HEAD_DIM),
                           lambda i: (i // tiles_per_seq, 0, i % tiles_per_seq, 0))
    return pl.pallas_call(
        functools.partial(_inproj_attn_kernel, tok_width=tok_width, kv_width=kv_width),
        out_shape=(jax.ShapeDtypeStruct((t, tok_width), BF16),
                   jax.ShapeDtypeStruct((t // seq, n_kv, seq, HEAD_DIM), BF16),
                   jax.ShapeDtypeStruct((t // seq, n_kv, seq, HEAD_DIM), BF16),
                   jax.ShapeDtypeStruct((t, n - tok_width - 2 * kv_width), BF16)),
        grid=(t // INPROJ_ROWS,),
        in_specs=[
            pl.BlockSpec((INPROJ_ROWS, d_model), lambda i: (i, 0)),
            pl.BlockSpec((1, d_model), lambda i: (0, 0)),
            _resident((d_model, n), lambda i: (0, 0)),
            pl.BlockSpec((INPROJ_ROWS, 1), lambda i: (i, 0)),
            pl.BlockSpec((8, LANES), lambda i: (0, 0)),
        ],
        out_specs=(pl.BlockSpec((INPROJ_ROWS, tok_width), lambda i: (i, 0)),
                   kv_spec, kv_spec,
                   pl.BlockSpec((INPROJ_ROWS, n - tok_width - 2 * kv_width), lambda i: (i, 0))),
        compiler_params=_params(1),
        name="inproj_attn",
    )(x2d, g, w_bf16, pos2d, rot_rows)


WATTN_QBLOCKS = 4


def _wattn_kernel(sink_ref, q_ref, kp_ref, kc_ref, kn_ref, vp_ref, vc_ref, vn_ref, o_ref,
                  valid_ref, kpad_ref, vpad_ref, s_ref, p_ref, inv_ref, *, seq):
    step = pl.program_id(1)
    n_kv = kc_ref.shape[1]
    pairs = GQA_RATIO // 2
    half_rows = pairs * BLOCK
    key_rows = (WATTN_QBLOCKS + 2) * BLOCK
    zeros = jnp.zeros((key_rows, HEAD_DIM), BF16)
    ones_col = jnp.where(lax.broadcasted_iota(I32, (key_rows, HEAD_DIM), 1) == 0, 1.0, 0.0).astype(BF16)
    low_half = lax.broadcasted_iota(I32, (1, LANES), 1) < HEAD_DIM
    for hk in range(n_kv):
        kw = jnp.concatenate([kp_ref[0, hk], kc_ref[0, hk], kn_ref[0, hk]], axis=0)
        vw = jnp.concatenate([vp_ref[0, hk], vc_ref[0, hk], vn_ref[0, hk]], axis=0)
        kpad_ref[2 * hk] = jnp.concatenate([kw, zeros], axis=1)
        kpad_ref[2 * hk + 1] = jnp.concatenate([zeros, kw], axis=1)
        vpad_ref[2 * hk] = jnp.concatenate([vw, ones_col], axis=1)
        vpad_ref[2 * hk + 1] = jnp.concatenate([ones_col, vw], axis=1)

    qi = lax.broadcasted_iota(I32, (BLOCK, 3 * BLOCK), 0)
    kj = lax.broadcasted_iota(I32, (BLOCK, 3 * BLOCK), 1)
    for qb in range(WATTN_QBLOCKS):
        n = step * WATTN_QBLOCKS + qb
        k0 = qb * BLOCK
        first = jnp.maximum(qi, BLOCK - n * BLOCK)
        last = jnp.minimum(qi + 2 * WINDOW, seq + BLOCK - 1 - n * BLOCK)
        valid_ref[qb] = jnp.where(((kj - first) | (last - kj)) >= 0, 1.0, 0.0)
        q0 = qb * BLOCK
        for hk in range(n_kv):
            tile0 = hk * pairs
            qs = jnp.concatenate(
                [q_ref[0, q0:q0 + BLOCK, (tile0 + j) * LANES:(tile0 + j + 1) * LANES]
                 for j in range(pairs)], axis=0)
            s_ref[qb, hk, 0:half_rows, :] = _dot_nt(qs, kpad_ref[2 * hk, k0:k0 + 3 * BLOCK, :])
            s_ref[qb, hk, half_rows:2 * half_rows, :] = _dot_nt(
                qs, kpad_ref[2 * hk + 1, k0:k0 + 3 * BLOCK, :])
        for hk in range(n_kv):
            for c in range(GQA_RATIO):
                j, odd = c % pairs, c // pairs
                r = c * BLOCK
                s = jnp.concatenate([
                    jnp.where(valid_ref[qb, :, 0:BLOCK] > 0.5, s_ref[qb, hk, r:r + BLOCK, 0:BLOCK], NEG_INF),
                    s_ref[qb, hk, r:r + BLOCK, BLOCK:2 * BLOCK],
                    jnp.where(valid_ref[qb, :, 2 * BLOCK:] > 0.5,
                              s_ref[qb, hk, r:r + BLOCK, 2 * BLOCK:], NEG_INF),
                ], axis=1)
                sk = sink_ref[hk * GQA_RATIO + 2 * j + odd]
                m = jnp.maximum(jnp.max(s, axis=-1, keepdims=True), sk)
                p_ref[qb, hk, r:r + BLOCK, :] = jnp.exp(s - m).astype(BF16)
                inv_ref[qb, hk, j * BLOCK:(j + 1) * BLOCK, odd * HEAD_DIM:(odd + 1) * HEAD_DIM] = (
                    jnp.broadcast_to(jnp.exp(sk - m), (BLOCK, HEAD_DIM)))
        for hk in range(n_kv):
            pv_even = _dot(p_ref[qb, hk, 0:half_rows, :], vpad_ref[2 * hk, k0:k0 + 3 * BLOCK, :])
            pv_odd = _dot(p_ref[qb, hk, half_rows:2 * half_rows, :],
                          vpad_ref[2 * hk + 1, k0:k0 + 3 * BLOCK, :])
            den = (jnp.where(low_half, pv_even[:, HEAD_DIM:HEAD_DIM + 1], pv_odd[:, 0:1])
                   + inv_ref[qb, hk])
            o = jnp.where(low_half, pv_even, pv_odd) / den
            for j in range(pairs):
                lo = (hk * pairs + j) * LANES
                o_ref[0, q0:q0 + BLOCK, lo:lo + LANES] = o[j * BLOCK:(j + 1) * BLOCK].astype(BF16)


def _window_attention(sink, q3d, k4d, v4d):
    b, seq, tok_width = q3d.shape
    n_kv = k4d.shape[1]
    nb = seq // BLOCK
    qb = WATTN_QBLOCKS
    assert nb % qb == 0
    edge_block = (1, n_kv, BLOCK, HEAD_DIM)
    prev_spec = pl.BlockSpec(edge_block, lambda i, s: (i, 0, jnp.maximum(s * qb - 1, 0), 0))
    cur_spec = pl.BlockSpec((1, n_kv, qb * BLOCK, HEAD_DIM), lambda i, s: (i, 0, s, 0))
    next_spec = pl.BlockSpec(edge_block, lambda i, s: (i, 0, jnp.minimum(s * qb + qb, nb - 1), 0))
    key_rows = (qb + 2) * BLOCK
    return pl.pallas_call(
        functools.partial(_wattn_kernel, seq=seq),
        out_shape=jax.ShapeDtypeStruct((b, seq, tok_width), BF16),
        grid=(b, nb // qb),
        in_specs=[
            pl.BlockSpec(memory_space=pltpu.SMEM),
            pl.BlockSpec((1, qb * BLOCK, tok_width), lambda i, s: (i, s, 0)),
            prev_spec, cur_spec, next_spec, prev_spec, cur_spec, next_spec,
        ],
        out_specs=pl.BlockSpec((1, qb * BLOCK, tok_width), lambda i, s: (i, s, 0)),
        scratch_shapes=[pltpu.VMEM((qb, BLOCK, 3 * BLOCK), F32),
                        pltpu.VMEM((2 * n_kv, key_rows, LANES), BF16),
                        pltpu.VMEM((2 * n_kv, key_rows, LANES), BF16),
                        pltpu.VMEM((qb, n_kv, GQA_RATIO * BLOCK, 3 * BLOCK), F32),
                        pltpu.VMEM((qb, n_kv, GQA_RATIO * BLOCK, 3 * BLOCK), BF16),
                        pltpu.VMEM((qb, n_kv, GQA_RATIO // 2 * BLOCK, LANES), F32)],
        compiler_params=_params(2),
        name="window_attention",
    )(sink, q3d, k4d, k4d, k4d, v4d, v4d, v4d)


def _rotary_rows(dtype=F32):
    half = ROT_DIM // 2
    inv_freq = ROPE_THETA ** (-jnp.arange(0, ROT_DIM, 2, dtype=jnp.float32) / ROT_DIM)
    lane = np.arange(LANES) % HEAD_DIM
    rotated = lane < ROT_DIM
    freq = jnp.where(jnp.asarray(rotated), inv_freq[jnp.asarray(lane % half)], 0.0)
    rows = jnp.zeros((8, LANES), dtype)
    rows = rows.at[0].set(freq)
    rows = rows.at[1].set(jnp.asarray(np.where(lane < half, -1.0, 0.0), dtype))
    rows = rows.at[2].set(jnp.asarray(np.where(rotated & (lane >= half), 1.0, 0.0), dtype))
    return rows


def kernel(x, mem, positions, norm_mix_g, norm_ffn_g, mem_norm_g, final_g, mem_w_kv,
           pool_w_in, pool_group_w, pool_scale, pool_w_out,
           attn_w_in, attn_sink, attn_w_out,
           router_w, exp_w_gate, exp_w_up, exp_w_down):
    b, seq, d_model = x.shape
    depth = norm_mix_g.shape[0]
    t = b * seq
    n_experts = router_w.shape[2]
    tok_width = pool_scale.shape[1]
    n_groups = pool_group_w.shape[1]
    kv_width = (attn_w_in.shape[2] - tok_width - XA_WIDTH) // 2
    assert seq % ROW_TILE == 0 and seq % INPROJ_ROWS == 0 and mem.shape[1] == MEM_LEN

    memkv = _memkv(mem.reshape(b * MEM_LEN, d_model), mem_norm_g.reshape(1, d_model),
                   mem_w_kv.astype(BF16))
    memkv = memkv.reshape(depth, b, MEM_LEN, 2 * XA_WIDTH)
    rw_pad = jnp.pad(router_w, ((0, 0), (0, 0), (0, LANES - n_experts)))
    pos2d = positions.reshape(t, 1)
    rot_rows = _rotary_rows()

    x2d = x.reshape(t, d_model)
    for layer in range(depth):
        j = layer // 2
        g_mix = norm_mix_g[layer].reshape(1, d_model)
        if layer % 2 == 0:
            u, qm = _inproj_pool(x2d, g_mix, pool_w_in[j].astype(BF16), tok_width)
            tok = _pool_mixer(u.reshape(b, seq, tok_width), pool_group_w[j].astype(BF16),
                              pool_scale[j].reshape(n_groups, 1, tok_width // n_groups))
            tok = tok.reshape(t, tok_width)
            w_out = pool_w_out[j]
        else:
            q, k, v, qm = _inproj_attn(x2d, g_mix, attn_w_in[j].astype(BF16), pos2d, rot_rows,
                                       tok_width, kv_width, seq)
            tok = _window_attention(attn_sink[j], q.reshape(b, seq, tok_width), k, v)
            tok = tok.reshape(t, tok_width)
            w_out = attn_w_out[j]
        mo = _mem_xattn(qm, memkv, layer, seq)
        x1, h, afft, aff = _outproj(tok, mo, x2d, w_out.astype(BF16),
                                    norm_ffn_g[layer].reshape(1, d_model), rw_pad, layer,
                                    n_experts, seq)
        last = layer == depth - 1
        x2 = _moe(afft, aff, h, x1, exp_w_gate, exp_w_up, exp_w_down, layer, b, seq,
                  final_g.reshape(1, d_model) if last else None)
        x2d = x2.reshape(t, d_model)
    return x2d.reshape(b, seq, d_model)
```

```python
import functools

import jax
import jax.numpy as jnp
import numpy as np
from jax import lax
from jax.experimental import pallas as pl
from jax.experimental.pallas import tpu as pltpu
from jax.experimental.pallas import tpu_sc as plsc

F32 = jnp.float32
BF16 = jnp.bfloat16
I32 = jnp.int32
U32 = jnp.uint32

EPS = 1e-6
MEM_LEN = 256
XA_HEADS = 4
XA_HEAD_DIM = 128
XA_WIDTH = XA_HEADS * XA_HEAD_DIM
POOL_WINDOWS = (2, 4, 8, 16)
HEAD_DIM = 64
GQA_RATIO = 8
WINDOW = 128
BLOCK = 128
ROPE_THETA = 500000.0
ROT_DIM = 16
NEG_INF = -1e30
CAPACITY_FACTOR = 2

LANES = 128
SUBLANES = 8
MIB = 1024 * 1024
VMEM_LIMIT_BYTES = 56 * MIB

ROW_TILE = 512
INPROJ_ROWS = 1024
INPROJ_PARTS = 4
POOL_PAD = 16
PREFIX_CHUNK = 256
COL_CHUNK = 512
F32_KEY_BITS = 31


def _params(n_grid_dims):
    return pltpu.CompilerParams(
        dimension_semantics=("arbitrary",) * n_grid_dims,
        vmem_limit_bytes=VMEM_LIMIT_BYTES,
    )


def _resident(block_shape, index_map):
    return pl.BlockSpec(block_shape, index_map, pipeline_mode=pl.Buffered(1))


def _rmsnorm_rows(x, g):
    return x * lax.rsqrt(jnp.mean(x * x, axis=-1, keepdims=True) + EPS) * g


def _dot(a, b):
    return jnp.dot(a, b, preferred_element_type=F32)


def _dot_nt(a, b):
    return lax.dot_general(a, b, (((1,), (1,)), ((), ())), preferred_element_type=F32)


def _memkv_kernel(mem_ref, g_ref, w_ref, o_ref):
    hn = _rmsnorm_rows(mem_ref[...], g_ref[...]).astype(BF16)
    o_ref[0] = _dot(hn, w_ref[0]).astype(BF16)


def _memkv(mem2d, g, w_bf16):
    depth, d_model, n = w_bf16.shape
    rows = mem2d.shape[0]
    return pl.pallas_call(
        _memkv_kernel,
        out_shape=jax.ShapeDtypeStruct((depth, rows, n), BF16),
        grid=(depth, rows // ROW_TILE),
        in_specs=[
            pl.BlockSpec((ROW_TILE, d_model), lambda l, i: (i, 0)),
            pl.BlockSpec((1, d_model), lambda l, i: (0, 0)),
            pl.BlockSpec((1, d_model, n), lambda l, i: (l, 0, 0)),
        ],
        out_specs=pl.BlockSpec((1, ROW_TILE, n), lambda l, i: (l, i, 0)),
        compiler_params=_params(2),
        name="memkv",
    )(mem2d, g, w_bf16)


def _inproj_pool_kernel(x_ref, g_ref, w_ref, u_ref, qm_ref, *, tok_width):
    hn = _rmsnorm_rows(x_ref[...], g_ref[...]).astype(BF16)
    for c in range(0, tok_width, COL_CHUNK):
        u_ref[:, c:c + COL_CHUNK] = _dot(hn, w_ref[:, c:c + COL_CHUNK])
    qm_ref[...] = _dot(hn, w_ref[:, tok_width:]).astype(BF16)


def _inproj_pool(x2d, g, w_bf16, tok_width):
    t, d_model = x2d.shape
    n = w_bf16.shape[1]
    return pl.pallas_call(
        functools.partial(_inproj_pool_kernel, tok_width=tok_width),
        out_shape=(jax.ShapeDtypeStruct((t, tok_width), F32),
                   jax.ShapeDtypeStruct((t, n - tok_width), BF16)),
        grid=(t // INPROJ_ROWS,),
        in_specs=[
            pl.BlockSpec((INPROJ_ROWS, d_model), lambda i: (i, 0)),
            pl.BlockSpec((1, d_model), lambda i: (0, 0)),
            _resident((d_model, n), lambda i: (0, 0)),
        ],
        out_specs=(pl.BlockSpec((INPROJ_ROWS, tok_width), lambda i: (i, 0)),
                   pl.BlockSpec((INPROJ_ROWS, n - tok_width), lambda i: (i, 0))),
        compiler_params=_params(1),
        name="inproj_pool",
    )(x2d, g, w_bf16)


def _pool_group(u_ref, gw_ref, sc_ref, o_ref, *, window, seq):
    gwid = u_ref.shape[2]
    rows = seq + 2 * POOL_PAD
    half = window // 2
    assert 2 * half <= POOL_PAD
    zeros_pad = jnp.zeros((POOL_PAD, gwid), F32)
    u = u_ref[0]
    p = jnp.concatenate([zeros_pad, u, zeros_pad], axis=0)
    k = 1
    while k < half:
        p = p + pltpu.roll(p, rows - k, 0)
        k *= 2
    before = p if half % SUBLANES == 0 else pltpu.roll(p, half, 0)
    shift = half if half % SUBLANES == 0 else 0
    win = before[POOL_PAD - shift:POOL_PAD - shift + seq, :] + p[POOL_PAD:POOL_PAD + seq, :]
    t = lax.broadcasted_iota(I32, (seq, 1), 0)
    lo = jnp.maximum(t - half, 0)
    hi = jnp.minimum(t + half - 1, seq - 1)
    cnt = (hi - lo + 1).astype(F32)
    pooled = (win / cnt - u).astype(BF16)
    o_ref[0] = (_dot(pooled, gw_ref[0]) * sc_ref[0]).astype(BF16)


def _pool_kernel(u_ref, gw_ref, sc_ref, o_ref, *, seq):
    g = pl.program_id(1)
    for k, window in enumerate(POOL_WINDOWS):
        @pl.when(g == k)
        def _():
            _pool_group(u_ref, gw_ref, sc_ref, o_ref, window=window, seq=seq)


def _pool_mixer(u3d, gw_bf16, scale3d):
    b, seq, tok_width = u3d.shape
    n_groups, gwid, _ = gw_bf16.shape
    assert n_groups == len(POOL_WINDOWS) and n_groups * gwid == tok_width
    return pl.pallas_call(
        functools.partial(_pool_kernel, seq=seq),
        out_shape=jax.ShapeDtypeStruct((b, seq, tok_width), BF16),
        grid=(b, n_groups),
        in_specs=[
            pl.BlockSpec((1, seq, gwid), lambda i, g: (i, 0, g)),
            pl.BlockSpec((1, gwid, gwid), lambda i, g: (g, 0, 0)),
            pl.BlockSpec((1, 1, gwid), lambda i, g: (g, 0, 0)),
        ],
        out_specs=pl.BlockSpec((1, seq, gwid), lambda i, g: (i, 0, g)),
        compiler_params=_params(2),
        name="pool_mixer",
    )(u3d, gw_bf16, scale3d)


def _xattn_kernel(q_ref, kv_ref, o_ref):
    scale = XA_HEAD_DIM ** -0.5
    for h in range(XA_HEADS):
        lo = h * XA_HEAD_DIM
        q = q_ref[:, lo:lo + XA_HEAD_DIM]
        k = kv_ref[0, 0, :, lo:lo + XA_HEAD_DIM]
        v = kv_ref[0, 0, :, XA_WIDTH + lo:XA_WIDTH + lo + XA_HEAD_DIM]
        s = _dot_nt(q, k) * scale
        m = jnp.max(s, axis=-1, keepdims=True)
        p = jnp.exp(s - m)
        den = jnp.sum(p, axis=-1, keepdims=True)
        o_ref[:, lo:lo + XA_HEAD_DIM] = (_dot(p.astype(BF16), v) / den).astype(BF16)


XATTN_ROWS = 1024


def _mem_xattn(qm2d, memkv, layer, seq):
    t = qm2d.shape[0]
    tiles_per_seq = seq // XATTN_ROWS
    return pl.pallas_call(
        _xattn_kernel,
        out_shape=jax.ShapeDtypeStruct((t, XA_WIDTH), BF16),
        grid=(t // XATTN_ROWS,),
        in_specs=[
            pl.BlockSpec((XATTN_ROWS, XA_WIDTH), lambda i: (i, 0)),
            pl.BlockSpec((1, 1, MEM_LEN, 2 * XA_WIDTH), lambda i: (layer, i // tiles_per_seq, 0, 0)),
        ],
        out_specs=pl.BlockSpec((XATTN_ROWS, XA_WIDTH), lambda i: (i, 0)),
        compiler_params=_params(1),
        name="mem_xattn",
    )(qm2d, memkv)


def _outproj_kernel(tok_ref, mo_ref, x_ref, w_ref, g_ref, rw_ref,
                    x1_ref, h_ref, afft_ref, aff_ref, wcat_ref, x1prev_ref,
                    *, tok_width, n_experts):
    @pl.when(pl.program_id(0) == 0)
    def _():
        rw = rw_ref[0]
        w_hi = rw.astype(BF16)
        wcat_ref[:, 0:LANES] = w_hi
        wcat_ref[:, LANES:2 * LANES] = (rw - w_hi.astype(F32)).astype(BF16)
        x1prev_ref[...] = jnp.zeros_like(x1prev_ref)

    hn = _rmsnorm_rows(x1prev_ref[...], g_ref[...])
    h_prev = hn.astype(BF16)
    h_lo = (hn - h_prev.astype(F32)).astype(BF16)
    r = _dot(h_prev, wcat_ref[...]) + _dot(h_lo, wcat_ref[...])
    logits = r[:, 0:LANES] + r[:, LANES:2 * LANES]
    lt = logits.T[0:n_experts, :]
    m = jnp.max(lt, axis=0, keepdims=True)
    ex = jnp.exp(lt - m)
    afft = ex / jnp.sum(ex, axis=0, keepdims=True)
    afft_ref[0] = afft
    padded = jnp.concatenate(
        [afft, jnp.zeros((LANES - n_experts, afft.shape[1]), F32)], axis=0)
    aff = padded.T
    aff_ref[...] = aff
    h_ref[...] = _pack_row_words(h_prev, _pack_gate_lanes(aff))

    y = _dot(tok_ref[...], w_ref[0:tok_width, :]) + _dot(mo_ref[...], w_ref[tok_width:, :])
    x1 = x_ref[...] + y
    x1_ref[...] = x1
    x1prev_ref[...] = x1


def _outproj(tok2d, mo2d, x2d, w_bf16, g, rw_pad, layer, n_experts, seq):
    t, d_model = x2d.shape
    tok_width = tok2d.shape[1]
    tiles_per_seq = seq // ROW_TILE
    n_tiles = t // ROW_TILE

    def cur(i):
        return jnp.minimum(i, n_tiles - 1)

    def prev(i):
        return jnp.maximum(i - 1, 0)

    return pl.pallas_call(
        functools.partial(_outproj_kernel, tok_width=tok_width, n_experts=n_experts),
        out_shape=(jax.ShapeDtypeStruct((t, d_model), F32),
                   jax.ShapeDtypeStruct((t, d_model // 2 + LANES), U32),
                   jax.ShapeDtypeStruct((t // seq, n_experts, seq), F32),
                   jax.ShapeDtypeStruct((t, LANES), F32)),
        grid=(n_tiles + 1,),
        in_specs=[
            pl.BlockSpec((ROW_TILE, tok_width), lambda i: (cur(i), 0)),
            pl.BlockSpec((ROW_TILE, mo2d.shape[1]), lambda i: (cur(i), 0)),
            pl.BlockSpec((ROW_TILE, d_model), lambda i: (cur(i), 0)),
            _resident((d_model, d_model), lambda i: (0, 0)),
            pl.BlockSpec((1, d_model), lambda i: (0, 0)),
            _resident((1, d_model, LANES), lambda i: (layer, 0, 0)),
        ],
        out_specs=(pl.BlockSpec((ROW_TILE, d_model), lambda i: (cur(i), 0)),
                   pl.BlockSpec((ROW_TILE, d_model // 2 + LANES), lambda i: (prev(i), 0)),
                   pl.BlockSpec((1, n_experts, ROW_TILE),
                                lambda i: (prev(i) // tiles_per_seq, 0, prev(i) % tiles_per_seq)),
                   pl.BlockSpec((ROW_TILE, LANES), lambda i: (prev(i), 0))),
        scratch_shapes=[pltpu.VMEM((d_model, 2 * LANES), BF16),
                        pltpu.VMEM((ROW_TILE, d_model), F32)],
        compiler_params=_params(1),
        name="outproj_router",
    )(tok2d, mo2d, x2d, w_bf16, g, rw_pad)


def _strict_triangle(n, lower):
    r = lax.broadcasted_iota(I32, (n, n), 0)
    c = lax.broadcasted_iota(I32, (n, n), 1)
    return jnp.where((c < r) if lower else (r < c), 1.0, 0.0).astype(BF16)


def _prefix_rows(mask_f32):
    s, l = mask_f32.shape
    tri = _strict_triangle(PREFIX_CHUNK, lower=True)
    carry = jnp.zeros((1, l), F32)
    out = []
    for c in range(0, s, PREFIX_CHUNK):
        m = mask_f32[c:c + PREFIX_CHUNK, :]
        out.append(_dot(tri, m.astype(BF16)) + carry)
        carry = carry + jnp.sum(m, axis=0, keepdims=True)
    return jnp.concatenate(out, axis=0)


def _prefix_lanes(mask_f32):
    e, s = mask_f32.shape
    tri = _strict_triangle(PREFIX_CHUNK, lower=False)
    carry = jnp.zeros((e, 1), F32)
    out = []
    for c in range(0, s, PREFIX_CHUNK):
        m = mask_f32[:, c:c + PREFIX_CHUNK]
        out.append(_dot(m.astype(BF16), tri) + carry)
        carry = carry + jnp.sum(m, axis=1, keepdims=True)
    return jnp.concatenate(out, axis=1)


def _select_slots(key, thr, need, prefix_fn):
    return _select_slots_and_counts(key, thr, need, prefix_fn)[0]


def _select_slots_and_counts(key, thr, need, prefix_fn):
    gt = jnp.where(key > thr, 1.0, 0.0)
    eq = jnp.where(key == thr, 1.0, 0.0)
    eq_rank = prefix_fn(eq)
    sel = gt + eq * jnp.where(eq_rank < need, 1.0, 0.0)
    pos = prefix_fn(sel)
    return jnp.where(sel > 0.5, pos, -1.0), pos


GATE_GROUP = 16
GATE_PIECES = 3


def _pack_gate_lanes(aff):
    hi = aff.astype(BF16).astype(F32)
    r1 = aff - hi
    mid = r1.astype(BF16).astype(F32)
    lo = (r1 - mid).astype(BF16).astype(F32)
    packed = hi + pltpu.roll(mid, GATE_GROUP, 1) + pltpu.roll(lo, 2 * GATE_GROUP, 1)
    return packed.astype(BF16)


def _unpack_gate(tail, e):
    lane = lax.broadcasted_iota(I32, (1, LANES), 1)
    mine = ((lane & (GATE_GROUP - 1)) == e) & (lane < GATE_PIECES * GATE_GROUP)
    return jnp.sum(jnp.where(mine, tail.astype(F32), 0.0), axis=1, keepdims=True)


def _pack_row_words(h, gate_tile):
    rows, d_model = h.shape
    half = d_model // 2
    hi = jnp.concatenate([h[:, 0:half], gate_tile], axis=1).astype(F32)
    lo = jnp.concatenate([h[:, half:], jnp.zeros((rows, LANES), BF16)], axis=1).astype(F32)
    return pltpu.bitcast(hi, U32) | (pltpu.bitcast(lo, U32) >> 16)


def _unpack_row_words(words):
    hi = pltpu.bitcast(words & jnp.uint32(0xFFFF0000), F32)
    lo = pltpu.bitcast(words << 16, F32)
    return hi, lo


def _route_kernel(afft_ref, posm_ref, thr_ref, need_ref, starts_ref, *, cap):
    n_experts, seq = afft_ref.shape[1], afft_ref.shape[2]
    key = pltpu.bitcast(afft_ref[0], I32)
    thr = jnp.zeros((n_experts, 1), I32)
    for bit in range(F32_KEY_BITS - 1, -1, -1):
        cand = thr | (1 << bit)
        cnt = jnp.sum(jnp.where(key >= cand, 1.0, 0.0), axis=1, keepdims=True)
        thr = jnp.where(cnt >= cap, cand, thr)
    n_gt = jnp.sum(jnp.where(key > thr, 1.0, 0.0), axis=1, keepdims=True)
    need = cap - n_gt
    thr_ref[0] = thr
    need_ref[0] = need
    posm, before = _select_slots_and_counts(key, thr, need, _prefix_lanes)
    posm_ref[0] = posm.astype(I32)
    starts_ref[0] = jnp.concatenate(
        [before[:, r:r + 1] for r in range(0, seq, COMBINE_ROWS)], axis=1).astype(I32)


def _route(afft, cap):
    b, n_experts, seq = afft.shape
    n_tiles = seq // COMBINE_ROWS
    rows = b * n_experts
    outs = pl.pallas_call(
        functools.partial(_route_kernel, cap=cap),
        out_shape=(jax.ShapeDtypeStruct((1, rows, seq), I32),
                   jax.ShapeDtypeStruct((1, rows, 1), I32),
                   jax.ShapeDtypeStruct((1, rows, 1), F32),
                   jax.ShapeDtypeStruct((1, rows, n_tiles), I32)),
        grid=(1,),
        in_specs=[pl.BlockSpec((1, rows, seq), lambda i: (0, 0, 0))],
        out_specs=(pl.BlockSpec((1, rows, seq), lambda i: (0, 0, 0)),
                   pl.BlockSpec((1, rows, 1), lambda i: (0, 0, 0)),
                   pl.BlockSpec((1, rows, 1), lambda i: (0, 0, 0)),
                   pl.BlockSpec((1, rows, n_tiles), lambda i: (0, 0, 0))),
        compiler_params=_params(1),
        name="expert_route",
    )(afft.reshape(1, rows, seq))
    return tuple(o.reshape(b, n_experts, o.shape[2]) for o in outs)


SC_LANES = 16
SC_GATHER_ROWS = 64
SC_BACKGROUND_SHARE = 2


def _sc_expert_gather(posm2d, h_words, n_experts, e_offset, n_e, b, seq, cap, worker_share=1):
    width = h_words.shape[1]
    info = plsc.get_sparse_core_info()
    n_cores, n_subcores = info.num_cores, info.num_subcores
    n_workers = n_cores * n_subcores // worker_share
    assert info.num_lanes == SC_LANES and (b * n_e) % n_workers == 0
    pairs_per_worker = (b * n_e) // n_workers
    mesh = plsc.VectorSubcoreMesh(core_axis_name="c", subcore_axis_name="s")

    @functools.partial(
        pl.kernel, mesh=mesh,
        out_type=jax.ShapeDtypeStruct((n_e * b * cap, width), U32),
        compiler_params=pltpu.CompilerParams(needs_layout_passes=False),
        scratch_types=[
            pltpu.VMEM((seq,), I32),
            pltpu.VMEM((cap,), I32),
            pltpu.VMEM((SC_GATHER_ROWS, width), U32),
            pltpu.SemaphoreType.DMA,
        ],
        name="sc_expert_gather",
    )
    def gather(posm_hbm, h_hbm, out_hbm, pos_v, idx_v, rows_v, sem):
        wid = lax.axis_index("s") * n_cores + lax.axis_index("c")

        def move_pair(pair):
            bi = pair // n_e
            e = pair - bi * n_e
            pltpu.sync_copy(posm_hbm.at[bi * n_experts + e_offset + e], pos_v)

            @pl.loop(0, seq, step=SC_LANES)
            def _(t0):
                slots = pos_v[pl.ds(t0, SC_LANES)]
                rows = lax.iota(I32, SC_LANES) + (t0 + bi * seq)
                plsc.store_scatter(idx_v, [slots], rows, mask=slots >= 0)

            out_base = (e * b + bi) * cap
            for c in range(cap // SC_GATHER_ROWS):
                chunk = idx_v.at[pl.ds(c * SC_GATHER_ROWS, SC_GATHER_ROWS)]
                pltpu.async_copy(h_hbm.at[chunk], rows_v, sem).wait()
                pltpu.sync_copy(rows_v, out_hbm.at[pl.ds(out_base + c * SC_GATHER_ROWS, SC_GATHER_ROWS)])

        @pl.when(wid < n_workers)
        def _():
            for p in range(pairs_per_worker):
                move_pair(wid * pairs_per_worker + p)

    return gather(posm2d, h_words).reshape(n_e, b, cap, width)


EXPERT_ROWS = 1024
EXPERT_FTILE = 256


EXPERT_WBUFS = 2


def _expert_kernel(xg_ref, wg_hbm, wu_hbm, wd_hbm, o_ref,
                   x_ref, g_ref, hact_ref, wg_full, wu_full, wd_full, wg_buf, wu_buf, wd_buf, sem,
                   *, layer, e_offset, n_ftiles):
    e = pl.program_id(0)
    m = pl.program_id(1)
    n_e = pl.num_programs(0)
    nb, cap, width = xg_ref.shape[1], xg_ref.shape[2], xg_ref.shape[3]
    half = width - LANES
    d_model = 2 * half
    rows = nb * cap
    tf = EXPERT_FTILE
    assert n_ftiles == 2 * EXPERT_WBUFS
    nxt = jnp.minimum(e + 1, n_e - 1)

    def gate_up_copies(expert, f):
        ge = e_offset + expert
        slot = f % EXPERT_WBUFS
        return (
            pltpu.make_async_copy(wg_hbm.at[layer, ge, :, pl.ds(f * tf, tf)], wg_buf.at[slot], sem.at[0, slot]),
            pltpu.make_async_copy(wu_hbm.at[layer, ge, :, pl.ds(f * tf, tf)], wu_buf.at[slot], sem.at[1, slot]),
        )

    def down_copy(expert, f):
        slot = f % EXPERT_WBUFS
        return pltpu.make_async_copy(wd_hbm.at[layer, e_offset + expert, pl.ds(f * tf, tf), :],
                                     wd_buf.at[slot], sem.at[2, slot])

    def start_gate_up(expert, f):
        for cp in gate_up_copies(expert, f):
            cp.start()

    def wait_gate_up(expert, f):
        for cp in gate_up_copies(expert, f):
            cp.wait()

    def keep_gate_up(expert, f):
        slot = f % EXPERT_WBUFS
        wg_full[:, f * tf:(f + 1) * tf] = wg_buf[slot].astype(BF16)
        wu_full[:, f * tf:(f + 1) * tf] = wu_buf[slot].astype(BF16)
        if f + EXPERT_WBUFS < n_ftiles:
            start_gate_up(expert, f + EXPERT_WBUFS)

    def keep_down(f):
        slot = f % EXPERT_WBUFS
        wd_full[f * tf:(f + 1) * tf, :] = wd_buf[slot].astype(BF16)
        if f + EXPERT_WBUFS < n_ftiles:
            down_copy(e, f + EXPERT_WBUFS).start()

    for i in range(nb):
        r0 = i * cap
        hi, lo = _unpack_row_words(xg_ref[0, i])
        x_ref[r0:r0 + cap, 0:half] = hi[:, 0:half].astype(BF16)
        x_ref[r0:r0 + cap, half:d_model] = lo[:, 0:half].astype(BF16)
        g_ref[r0:r0 + cap, :] = jnp.broadcast_to(
            _unpack_gate(hi[:, half:width], e_offset + e), (cap, LANES))

    def hidden_tile(x, f):
        a = _dot(x, wg_full[:, f * tf:(f + 1) * tf])
        u = _dot(x, wu_full[:, f * tf:(f + 1) * tf])
        hact_ref[:, f * tf:(f + 1) * tf] = (a * jax.nn.sigmoid(a) * u).astype(BF16)

    def down_projection(midway=None):
        n_chunks = d_model // COL_CHUNK
        for ci in range(n_chunks):
            c = ci * COL_CHUNK
            y = _dot(hact_ref[...], wd_full[:, c:c + COL_CHUNK])
            for j in range(0, COL_CHUNK, LANES):
                o_ref[:, 0, :, c + j:c + j + LANES] = (
                    (y[:, j:j + LANES] * g_ref[...]).astype(BF16).reshape(nb, cap, LANES))
            if midway is not None and ci == n_chunks // 2 - 1:
                midway()

    def first_row_group(call_start):
        for f in range(EXPERT_WBUFS):
            if call_start:
                start_gate_up(e, f)
                down_copy(e, f).start()
            else:
                start_gate_up(nxt, f)
        for f in range(EXPERT_WBUFS):
            down_copy(e, f).wait()
        for f in range(EXPERT_WBUFS):
            keep_down(f)
        x = x_ref[...]
        for f in range(n_ftiles):
            if call_start:
                wait_gate_up(e, f)
                keep_gate_up(e, f)
                if f + EXPERT_WBUFS >= n_ftiles:
                    start_gate_up(nxt, f + EXPERT_WBUFS - n_ftiles)
            hidden_tile(x, f)
            if EXPERT_WBUFS <= f + 1 < n_ftiles:
                down_copy(e, f + 1).wait()
                keep_down(f + 1)
        down_projection()

    @pl.when((m == 0) & (e == 0))
    def _():
        first_row_group(True)

    @pl.when((m == 0) & (e > 0))
    def _():
        first_row_group(False)

    @pl.when(m != 0)
    def _():
        @pl.when(e + 1 < n_e)
        def _():
            for f in range(EXPERT_WBUFS):
                down_copy(e + 1, f).start()

        for f in range(EXPERT_WBUFS):
            wait_gate_up(nxt, f)
        x = x_ref[...]
        for f in range(n_ftiles):
            hidden_tile(x, f)
            if 0 <= f - 1 < EXPERT_WBUFS:
                keep_gate_up(nxt, f - 1)
        def refill_rest():
            for f in range(EXPERT_WBUFS, n_ftiles):
                wait_gate_up(nxt, f)
            for f in range(EXPERT_WBUFS, n_ftiles):
                keep_gate_up(nxt, f)

        down_projection(midway=refill_rest)


def _experts(xg, w_gate, w_up, w_down, layer, e_offset):
    n_e, b, cap, width = xg.shape
    d_model = 2 * (width - LANES)
    d_expert = w_gate.shape[3]
    nb = EXPERT_ROWS // cap
    assert b == 2 * nb
    n_ftiles = d_expert // EXPERT_FTILE
    return pl.pallas_call(
        functools.partial(_expert_kernel, layer=layer, e_offset=e_offset, n_ftiles=n_ftiles),
        out_shape=jax.ShapeDtypeStruct((b, n_e, cap, d_model), BF16),
        grid=(n_e, b // nb),
        in_specs=[
            pl.BlockSpec((1, nb, cap, width), lambda e, m: (e, m, 0, 0)),
            pl.BlockSpec(memory_space=pl.ANY),
            pl.BlockSpec(memory_space=pl.ANY),
            pl.BlockSpec(memory_space=pl.ANY),
        ],
        out_specs=pl.BlockSpec((nb, 1, cap, d_model), lambda e, m: (m, e, 0, 0)),
        scratch_shapes=[pltpu.VMEM((EXPERT_ROWS, d_model), BF16),
                        pltpu.VMEM((EXPERT_ROWS, LANES), F32),
                        pltpu.VMEM((EXPERT_ROWS, d_expert), BF16),
                        pltpu.VMEM((d_model, d_expert), BF16),
                        pltpu.VMEM((d_model, d_expert), BF16),
                        pltpu.VMEM((d_expert, d_model), BF16),
                        pltpu.VMEM((EXPERT_WBUFS, d_model, EXPERT_FTILE), F32),
                        pltpu.VMEM((EXPERT_WBUFS, d_model, EXPERT_FTILE), F32),
                        pltpu.VMEM((EXPERT_WBUFS, EXPERT_FTILE, d_model), F32),
                        pltpu.SemaphoreType.DMA((3, EXPERT_WBUFS))],
        compiler_params=_params(2),
        name="experts",
    )(xg, w_gate, w_up, w_down)


COMBINE_ROWS = 256
COMBINE_WINDOW = 64
MXU_DEPTH = 256
BF16_ROWS = 16


def _combine_kernel(starts_ref, aff_ref, thr_ref, need_ref, x1_ref, *rest, cap, n_experts, final_norm):
    n_y = len(rest) - (6 if final_norm else 5)
    y_hbm = rest[:n_y]
    g_ref = rest[n_y] if final_norm else None
    o_ref, post_ref, pfull_ref, ybuf, sem = rest[-5:]
    b = pl.program_id(0)
    t = pl.program_id(1)
    n_b = pl.num_programs(0)
    n_t = pl.num_programs(1)
    rows, win = COMBINE_ROWS, COMBINE_WINDOW
    experts_per_group = n_experts // n_y
    group_rows = ybuf.shape[0] // (2 * n_y)

    def ybuf_row(buf_slot, g):
        return pl.multiple_of((buf_slot * n_y + g) * group_rows, BF16_ROWS)

    n_chunks = sem.shape[1]
    chunks_per_group = n_chunks // n_y
    chunk_rows = group_rows // chunks_per_group
    slot = b % 2

    def chunk_copy(seq_idx, c, dst_slot):
        g, r = c // chunks_per_group, (c % chunks_per_group) * chunk_rows
        return pltpu.make_async_copy(y_hbm[g].at[seq_idx, pl.ds(r, chunk_rows), :],
                                     ybuf.at[pl.ds(ybuf_row(dst_slot, g) + r, chunk_rows), :],
                                     sem.at[dst_slot, c])

    @pl.when((b == 0) & (t == 0))
    def _():
        for c in range(n_chunks):
            chunk_copy(0, c, 0).start()

    @pl.when(t == 0)
    def _():
        for c in range(n_chunks):
            chunk_copy(b, c, slot).wait()
        key = pltpu.bitcast(aff_ref[...], I32)
        post_ref[...] = _select_slots(key, thr_ref[0], need_ref[0], _prefix_rows)

    for c in range(n_chunks):
        @pl.when((t == c) & (b + 1 < n_b))
        def _():
            chunk_copy(b + 1, c, 1 - slot).start()

    posm = post_ref[pl.ds(pl.multiple_of(t * rows, rows), rows), :]
    base = (b * (n_t + 1) + t) * n_experts
    wstart, ok = [], None
    for e in range(n_experts):
        first = starts_ref[base + e]
        end = starts_ref[base + n_experts + e]
        w0 = jnp.minimum((first // BF16_ROWS) * BF16_ROWS, cap - win)
        fits = end - w0 <= win
        wstart.append(w0)
        ok = fits if ok is None else jnp.logical_and(ok, fits)

    def finish(acc):
        if final_norm:
            acc = _rmsnorm_rows(acc, g_ref[...])
        o_ref[0] = acc

    @pl.when(ok)
    def _():
        per_dot, per_tile = MXU_DEPTH // win, LANES // win
        lane = lax.broadcasted_iota(I32, (1, LANES), 1)
        lane_f = lane.astype(F32)
        acc = x1_ref[0]
        for e0 in range(0, n_experts, per_dot):
            onehots, windows = [], []
            for e1 in range(e0, e0 + per_dot, per_tile):
                rel = None
                for k in range(per_tile - 1, -1, -1):
                    e = e1 + k
                    shifted = posm[:, e:e + 1] - (wstart[e] - k * win).astype(F32)
                    rel = shifted if rel is None else jnp.where(lane < (k + 1) * win, shifted, rel)
                onehots.append(jnp.where(rel == lane_f, 1.0, 0.0).astype(BF16))
            for e in range(e0, e0 + per_dot):
                g, el = divmod(e, experts_per_group)
                r0 = pl.multiple_of(ybuf_row(slot, g) + el * cap + wstart[e], BF16_ROWS)
                windows.append(ybuf[pl.ds(r0, win), :])
            acc = acc + _dot(jnp.concatenate(onehots, axis=1), jnp.concatenate(windows, axis=0))
        finish(acc)

    @pl.when(jnp.logical_not(ok))
    def _():
        slot_ids = lax.broadcasted_iota(I32, (1, cap), 1).astype(F32)
        for e in range(n_experts):
            pfull_ref[:, e * cap:(e + 1) * cap] = jnp.where(
                posm[:, e:e + 1] == slot_ids, 1.0, 0.0).astype(BF16)
        acc = x1_ref[0]
        for g in range(n_y):
            acc = acc + _dot(pfull_ref[:, g * group_rows:(g + 1) * group_rows],
                             ybuf[pl.ds(ybuf_row(slot, g), group_rows), :])
        finish(acc)


def _combine(starts, aff2d, thr_row, need_row, x1_3d, y_groups, cap, n_experts, final_g=None):
    b, seq, d_model = x1_3d.shape
    assert sum(y.shape[1] for y in y_groups) == n_experts * cap and cap >= COMBINE_WINDOW
    final_norm = final_g is not None
    n_tiles = seq // COMBINE_ROWS
    group_rows = y_groups[0].shape[1]
    assert all(y.shape[1] == group_rows for y in y_groups)
    assert n_tiles % len(y_groups) == 0 and group_rows % (n_tiles // len(y_groups)) == 0
    y_specs = [pl.BlockSpec(memory_space=pl.ANY) for _ in y_groups]
    g_specs = [pl.BlockSpec((1, d_model), lambda i, t, s: (0, 0))] if final_norm else []
    g_args = [final_g] if final_norm else []
    return pl.pallas_call(
        functools.partial(_combine_kernel, cap=cap, n_experts=n_experts, final_norm=final_norm),
        out_shape=jax.ShapeDtypeStruct((b, seq, d_model), F32),
        grid_spec=pltpu.PrefetchScalarGridSpec(
            num_scalar_prefetch=1,
            grid=(b, n_tiles),
            in_specs=[
                pl.BlockSpec((seq, LANES), lambda i, t, s: (i, 0)),
                pl.BlockSpec((1, 1, LANES), lambda i, t, s: (i, 0, 0)),
                pl.BlockSpec((1, 1, LANES), lambda i, t, s: (i, 0, 0)),
                pl.BlockSpec((1, COMBINE_ROWS, d_model), lambda i, t, s: (i, t, 0)),
            ] + y_specs + g_specs,
            out_specs=pl.BlockSpec((1, COMBINE_ROWS, d_model), lambda i, t, s: (i, t, 0)),
            scratch_shapes=[pltpu.VMEM((seq, LANES), F32),
                            pltpu.VMEM((COMBINE_ROWS, n_experts * cap), BF16),
                            pltpu.VMEM((2 * len(y_groups) * group_rows, d_model), BF16),
                            pltpu.SemaphoreType.DMA((2, n_tiles))],
        ),
        compiler_params=_params(2),
        name="combine",
    )(starts, aff2d, thr_row, need_row, x1_3d, *y_groups, *g_args)


EXPERT_GROUPS = 2


def _moe(afft, aff2d, h_words, x1_2d, w_gate, w_up, w_down, layer, b, seq, final_g=None):
    n_experts = afft.shape[1]
    assert n_experts <= GATE_GROUP and n_experts % EXPERT_GROUPS == 0
    d_model = x1_2d.shape[1]
    cap = CAPACITY_FACTOR * seq // n_experts
    n_e = n_experts // EXPERT_GROUPS
    posm, thr, need, tile_starts = _route(afft, cap)
    starts = jnp.concatenate([jnp.swapaxes(tile_starts, 1, 2),
                              jnp.full((b, 1, n_experts), cap, I32)], axis=1).reshape(-1)
    posm2d = posm.reshape(b * n_experts, seq)
    xgs = [_sc_expert_gather(posm2d, h_words, n_experts, g * n_e, n_e, b, seq, cap,
                             worker_share=1 if g == EXPERT_GROUPS - 1 else SC_BACKGROUND_SHARE)
           for g in range(EXPERT_GROUPS)]
    ys = [_experts(xg, w_gate, w_up, w_down, layer, g * n_e).reshape(b, n_e * cap, d_model)
          for g, xg in enumerate(xgs)]
    pad = LANES - n_experts
    thr_row = jnp.pad(thr.reshape(b, 1, n_experts), ((0, 0), (0, 0), (0, pad)),
                      constant_values=np.iinfo(np.int32).max)
    need_row = jnp.pad(need.reshape(b, 1, n_experts), ((0, 0), (0, 0), (0, pad)))
    return _combine(starts, aff2d, thr_row, need_row, x1_2d.reshape(b, seq, d_model), ys, cap,
                    n_experts, final_g)


def _rotary_tile(t, cos, sin_lo, sin_hi):
    half = ROT_DIM // 2
    return t * cos + pltpu.roll(t, LANES - half, 1) * sin_lo + pltpu.roll(t, half, 1) * sin_hi


def _inproj_attn_kernel(x_ref, g_ref, w_ref, pos_ref, rot_ref,
                        q_ref, k_ref, v_ref, qm_ref, *, tok_width, kv_width):
    qscale = HEAD_DIM ** -0.5
    rows = x_ref.shape[0] // INPROJ_PARTS

    def prepare(part):
        r = slice(part * rows, (part + 1) * rows)
        hn = _rmsnorm_rows(x_ref[r, :], g_ref[...]).astype(BF16)
        ang = pos_ref[r, :].astype(F32) * rot_ref[0:1, :]
        cos = jnp.cos(ang)
        sin = jnp.sin(ang)
        return hn, cos, sin * rot_ref[1:2, :], sin * rot_ref[2:3, :]

    def project(part, hn, cos, sin_lo, sin_hi):
        r = slice(part * rows, (part + 1) * rows)
        for c in range(0, tok_width, COL_CHUNK):
            pc = _dot(hn, w_ref[:, c:c + COL_CHUNK])
            for j in range(0, COL_CHUNK, LANES):
                rot = _rotary_tile(pc[:, j:j + LANES], cos, sin_lo, sin_hi)
                q_ref[r, c + j:c + j + LANES] = (rot * qscale).astype(BF16)
        kv = _dot(hn, w_ref[:, tok_width:tok_width + 2 * kv_width])
        k01 = _rotary_tile(kv[:, 0:LANES], cos, sin_lo, sin_hi)
        k2x = _rotary_tile(kv[:, LANES:2 * LANES], cos, sin_lo, sin_hi)
        k_ref[0, 0, r, :] = k01[:, 0:HEAD_DIM].astype(BF16)
        k_ref[0, 1, r, :] = k01[:, HEAD_DIM:LANES].astype(BF16)
        k_ref[0, 2, r, :] = k2x[:, 0:HEAD_DIM].astype(BF16)
        for hh in range(kv_width // HEAD_DIM):
            lo = kv_width + hh * HEAD_DIM
            v_ref[0, hh, r, :] = kv[:, lo:lo + HEAD_DIM].astype(BF16)
        qm_ref[r, :] = _dot(hn, w_ref[:, tok_width + 2 * kv_width:]).astype(BF16)

    prepared = [prepare(part) for part in range(INPROJ_PARTS)]
    for part in range(INPROJ_PARTS):
        project(part, *prepared[part])


def _inproj_attn(x2d, g, w_bf16, pos2d, rot_rows, tok_width, kv_width, seq):
    t, d_model = x2d.shape
    n = w_bf16.shape[1]
    n_kv = kv_width // HEAD_DIM
    assert n_kv == 3 and kv_width + HEAD_DIM == 2 * LANES
    tiles_per_seq = seq // INPROJ_ROWS
    kv_spec = pl.BlockSpec((1, n_kv, INPROJ_ROWS, HEAD_DIM),
                           lambda i: (i // tiles_per_seq, 0, i % tiles_per_seq, 0))
    return pl.pallas_call(
        functools.partial(_inproj_attn_kernel, tok_width=tok_width, kv_width=kv_width),
        out_shape=(jax.ShapeDtypeStruct((t, tok_width), BF16),
                   jax.ShapeDtypeStruct((t // seq, n_kv, seq, HEAD_DIM), BF16),
                   jax.ShapeDtypeStruct((t // seq, n_kv, seq, HEAD_DIM), BF16),
                   jax.ShapeDtypeStruct((t, n - tok_width - 2 * kv_width), BF16)),
        grid=(t // INPROJ_ROWS,),
        in_specs=[
            pl.BlockSpec((INPROJ_ROWS, d_model), lambda i: (i, 0)),
            pl.BlockSpec((1, d_model), lambda i: (0, 0)),
            _resident((d_model, n), lambda i: (0, 0)),
            pl.BlockSpec((INPROJ_ROWS, 1), lambda i: (i, 0)),
            pl.BlockSpec((8, LANES), lambda i: (0, 0)),
        ],
        out_specs=(pl.BlockSpec((INPROJ_ROWS, tok_width), lambda i: (i, 0)),
                   kv_spec, kv_spec,
                   pl.BlockSpec((INPROJ_ROWS, n - tok_width - 2 * kv_width), lambda i: (i, 0))),
        compiler_params=_params(1),
        name="inproj_attn",
    )(x2d, g, w_bf16, pos2d, rot_rows)


WATTN_QBLOCKS = 4


def _wattn_kernel(sink_ref, q_ref, kp_ref, kc_ref, kn_ref, vp_ref, vc_ref, vn_ref, o_ref,
                  valid_ref, kpad_ref, vpad_ref, s_ref, p_ref, inv_ref, *, seq):
    step = pl.program_id(1)
    n_kv = kc_ref.shape[1]
    pairs = GQA_RATIO // 2
    half_rows = pairs * BLOCK
    key_rows = (WATTN_QBLOCKS + 2) * BLOCK
    zeros = jnp.zeros((key_rows, HEAD_DIM), BF16)
    ones_col = jnp.where(lax.broadcasted_iota(I32, (key_rows, HEAD_DIM), 1) == 0, 1.0, 0.0).astype(BF16)
    low_half = lax.broadcasted_iota(I32, (1, LANES), 1) < HEAD_DIM
    for hk in range(n_kv):
        kw = jnp.concatenate([kp_ref[0, hk], kc_ref[0, hk], kn_ref[0, hk]], axis=0)
        vw = jnp.concatenate([vp_ref[0, hk], vc_ref[0, hk], vn_ref[0, hk]], axis=0)
        kpad_ref[2 * hk] = jnp.concatenate([kw, zeros], axis=1)
        kpad_ref[2 * hk + 1] = jnp.concatenate([zeros, kw], axis=1)
        vpad_ref[2 * hk] = jnp.concatenate([vw, ones_col], axis=1)
        vpad_ref[2 * hk + 1] = jnp.concatenate([ones_col, vw], axis=1)

    qi = lax.broadcasted_iota(I32, (BLOCK, 3 * BLOCK), 0)
    kj = lax.broadcasted_iota(I32, (BLOCK, 3 * BLOCK), 1)
    for qb in range(WATTN_QBLOCKS):
        n = step * WATTN_QBLOCKS + qb
        k0 = qb * BLOCK
        first = jnp.maximum(qi, BLOCK - n * BLOCK)
        last = jnp.minimum(qi + 2 * WINDOW, seq + BLOCK - 1 - n * BLOCK)
        valid_ref[qb] = jnp.where(((kj - first) | (last - kj)) >= 0, 1.0, 0.0)
        q0 = qb * BLOCK
        for hk in range(n_kv):
            tile0 = hk * pairs
            qs = jnp.concatenate(
                [q_ref[0, q0:q0 + BLOCK, (tile0 + j) * LANES:(tile0 + j + 1) * LANES]
                 for j in range(pairs)], axis=0)
            s_ref[qb, hk, 0:half_rows, :] = _dot_nt(qs, kpad_ref[2 * hk, k0:k0 + 3 * BLOCK, :])
            s_ref[qb, hk, half_rows:2 * half_rows, :] = _dot_nt(
                qs, kpad_ref[2 * hk + 1, k0:k0 + 3 * BLOCK, :])
        for hk in range(n_kv):
            for c in range(GQA_RATIO):
                j, odd = c % pairs, c // pairs
                r = c * BLOCK
                s = jnp.concatenate([
                    jnp.where(valid_ref[qb, :, 0:BLOCK] > 0.5, s_ref[qb, hk, r:r + BLOCK, 0:BLOCK], NEG_INF),
                    s_ref[qb, hk, r:r + BLOCK, BLOCK:2 * BLOCK],
                    jnp.where(valid_ref[qb, :, 2 * BLOCK:] > 0.5,
                              s_ref[qb, hk, r:r + BLOCK, 2 * BLOCK:], NEG_INF),
                ], axis=1)
                sk = sink_ref[hk * GQA_RATIO + 2 * j + odd]
                m = jnp.maximum(jnp.max(s, axis=-1, keepdims=True), sk)
                p_ref[qb, hk, r:r + BLOCK, :] = jnp.exp(s - m).astype(BF16)
                inv_ref[qb, hk, j * BLOCK:(j + 1) * BLOCK, odd * HEAD_DIM:(odd + 1) * HEAD_DIM] = (
                    jnp.broadcast_to(jnp.exp(sk - m), (BLOCK, HEAD_DIM)))
        for hk in range(n_kv):
            pv_even = _dot(p_ref[qb, hk, 0:half_rows, :], vpad_ref[2 * hk, k0:k0 + 3 * BLOCK, :])
            pv_odd = _dot(p_ref[qb, hk, half_rows:2 * half_rows, :],
                          vpad_ref[2 * hk + 1, k0:k0 + 3 * BLOCK, :])
            den = (jnp.where(low_half, pv_even[:, HEAD_DIM:HEAD_DIM + 1], pv_odd[:, 0:1])
                   + inv_ref[qb, hk])
            o = jnp.where(low_half, pv_even, pv_odd) / den
            for j in range(pairs):
                lo = (hk * pairs + j) * LANES
                o_ref[0, q0:q0 + BLOCK, lo:lo + LANES] = o[j * BLOCK:(j + 1) * BLOCK].astype(BF16)


def _window_attention(sink, q3d, k4d, v4d):
    b, seq, tok_width = q3d.shape
    n_kv = k4d.shape[1]
    nb = seq // BLOCK
    qb = WATTN_QBLOCKS
    assert nb % qb == 0
    edge_block = (1, n_kv, BLOCK, HEAD_DIM)
    prev_spec = pl.BlockSpec(edge_block, lambda i, s: (i, 0, jnp.maximum(s * qb - 1, 0), 0))
    cur_spec = pl.BlockSpec((1, n_kv, qb * BLOCK, HEAD_DIM), lambda i, s: (i, 0, s, 0))
    next_spec = pl.BlockSpec(edge_block, lambda i, s: (i, 0, jnp.minimum(s * qb + qb, nb - 1), 0))
    key_rows = (qb + 2) * BLOCK
    return pl.pallas_call(
        functools.partial(_wattn_kernel, seq=seq),
        out_shape=jax.ShapeDtypeStruct((b, seq, tok_width), BF16),
        grid=(b, nb // qb),
        in_specs=[
            pl.BlockSpec(memory_space=pltpu.SMEM),
            pl.BlockSpec((1, qb * BLOCK, tok_width), lambda i, s: (i, s, 0)),
            prev_spec, cur_spec, next_spec, prev_spec, cur_spec, next_spec,
        ],
        out_specs=pl.BlockSpec((1, qb * BLOCK, tok_width), lambda i, s: (i, s, 0)),
        scratch_shapes=[pltpu.VMEM((qb, BLOCK, 3 * BLOCK), F32),
                        pltpu.VMEM((2 * n_kv, key_rows, LANES), BF16),
                        pltpu.VMEM((2 * n_kv, key_rows, LANES), BF16),
                        pltpu.VMEM((qb, n_kv, GQA_RATIO * BLOCK, 3 * BLOCK), F32),
                        pltpu.VMEM((qb, n_kv, GQA_RATIO * BLOCK, 3 * BLOCK), BF16),
                        pltpu.VMEM((qb, n_kv, GQA_RATIO // 2 * BLOCK, LANES), F32)],
        compiler_params=_params(2),
        name="window_attention",
    )(sink, q3d, k4d, k4d, k4d, v4d, v4d, v4d)


def _rotary_rows(dtype=F32):
    half = ROT_DIM // 2
    inv_freq = ROPE_THETA ** (-jnp.arange(0, ROT_DIM, 2, dtype=jnp.float32) / ROT_DIM)
    lane = np.arange(LANES) % HEAD_DIM
    rotated = lane < ROT_DIM
    freq = jnp.where(jnp.asarray(rotated), inv_freq[jnp.asarray(lane % half)], 0.0)
    rows = jnp.zeros((8, LANES), dtype)
    rows = rows.at[0].set(freq)
    rows = rows.at[1].set(jnp.asarray(np.where(lane < half, -1.0, 0.0), dtype))
    rows = rows.at[2].set(jnp.asarray(np.where(rotated & (lane >= half), 1.0, 0.0), dtype))
    return rows


def kernel(x, mem, positions, norm_mix_g, norm_ffn_g, mem_norm_g, final_g, mem_w_kv,
           pool_w_in, pool_group_w, pool_scale, pool_w_out,
           attn_w_in, attn_sink, attn_w_out,
           router_w, exp_w_gate, exp_w_up, exp_w_down):
    b, seq, d_model = x.shape
    depth = norm_mix_g.shape[0]
    t = b * seq
    n_experts = router_w.shape[2]
    tok_width = pool_scale.shape[1]
    n_groups = pool_group_w.shape[1]
    kv_width = (attn_w_in.shape[2] - tok_width - XA_WIDTH) // 2
    assert seq % ROW_TILE == 0 and seq % INPROJ_ROWS == 0 and mem.shape[1] == MEM_LEN

    memkv = _memkv(mem.reshape(b * MEM_LEN, d_model), mem_norm_g.reshape(1, d_model),
                   mem_w_kv.astype(BF16))
    memkv = memkv.reshape(depth, b, MEM_LEN, 2 * XA_WIDTH)
    rw_pad = jnp.pad(router_w, ((0, 0), (0, 0), (0, LANES - n_experts)))
    pos2d = positions.reshape(t, 1)
    rot_rows = _rotary_rows()

    x2d = x.reshape(t, d_model)
    for layer in range(depth):
        j = layer // 2
        g_mix = norm_mix_g[layer].reshape(1, d_model)
        if layer % 2 == 0:
            u, qm = _inproj_pool(x2d, g_mix, pool_w_in[j].astype(BF16), tok_width)
            tok = _pool_mixer(u.reshape(b, seq, tok_width), pool_group_w[j].astype(BF16),
                              pool_scale[j].reshape(n_groups, 1, tok_width // n_groups))
            tok = tok.reshape(t, tok_width)
            w_out = pool_w_out[j]
        else:
            q, k, v, qm = _inproj_attn(x2d, g_mix, attn_w_in[j].astype(BF16), pos2d, rot_rows,
                                       tok_width, kv_width, seq)
            tok = _window_attention(attn_sink[j], q.reshape(b, seq, tok_width), k, v)
            tok = tok.reshape(t, tok_width)
            w_out = attn_w_out[j]
        mo = _mem_xattn(qm, memkv, layer, seq)
        x1, h, afft, aff = _outproj(tok, mo, x2d, w_out.astype(BF16),
                                    norm_ffn_g[layer].reshape(1, d_model), rw_pad, layer,
                                    n_experts, seq)
        last = layer == depth - 1
        x2 = _moe(afft, aff, h, x1, exp_w_gate, exp_w_up, exp_w_down, layer, b, seq,
                  final_g.reshape(1, d_model) if last else None)
        x2d = x2.reshape(t, d_model)
    return x2d.reshape(b, seq, d_model)
```

```python
import functools

import jax
import jax.numpy as jnp
import numpy as np
from jax import lax
from jax.experimental import pallas as pl
from jax.experimental.pallas import tpu as pltpu
from jax.experimental.pallas import tpu_sc as plsc

F32 = jnp.float32
BF16 = jnp.bfloat16
I32 = jnp.int32
U32 = jnp.uint32

EPS = 1e-6
MEM_LEN = 256
XA_HEADS = 4
XA_HEAD_DIM = 128
XA_WIDTH = XA_HEADS * XA_HEAD_DIM
POOL_WINDOWS = (2, 4, 8, 16)
HEAD_DIM = 64
GQA_RATIO = 8
WINDOW = 128
BLOCK = 128
ROPE_THETA = 500000.0
ROT_DIM = 16
NEG_INF = -1e30
CAPACITY_FACTOR = 2

LANES = 128
SUBLANES = 8
MIB = 1024 * 1024
VMEM_LIMIT_BYTES = 56 * MIB

ROW_TILE = 512
INPROJ_ROWS = 1024
INPROJ_PARTS = 4
POOL_PAD = 16
PREFIX_CHUNK = 256
COL_CHUNK = 512
F32_KEY_BITS = 31


def _params(n_grid_dims):
    return pltpu.CompilerParams(
        dimension_semantics=("arbitrary",) * n_grid_dims,
        vmem_limit_bytes=VMEM_LIMIT_BYTES,
    )


def _resident(block_shape, index_map):
    return pl.BlockSpec(block_shape, index_map, pipeline_mode=pl.Buffered(1))


def _rmsnorm_rows(x, g):
    return x * lax.rsqrt(jnp.mean(x * x, axis=-1, keepdims=True) + EPS) * g


def _dot(a, b):
    return jnp.dot(a, b, preferred_element_type=F32)


def _dot_nt(a, b):
    return lax.dot_general(a, b, (((1,), (1,)), ((), ())), preferred_element_type=F32)


def _memkv_kernel(mem_ref, g_ref, w_ref, o_ref):
    hn = _rmsnorm_rows(mem_ref[...], g_ref[...]).astype(BF16)
    o_ref[0] = _dot(hn, w_ref[0]).astype(BF16)


def _memkv(mem2d, g, w_bf16):
    depth, d_model, n = w_bf16.shape
    rows = mem2d.shape[0]
    return pl.pallas_call(
        _memkv_kernel,
        out_shape=jax.ShapeDtypeStruct((depth, rows, n), BF16),
        grid=(depth, rows // ROW_TILE),
        in_specs=[
            pl.BlockSpec((ROW_TILE, d_model), lambda l, i: (i, 0)),
            pl.BlockSpec((1, d_model), lambda l, i: (0, 0)),
            pl.BlockSpec((1, d_model, n), lambda l, i: (l, 0, 0)),
        ],
        out_specs=pl.BlockSpec((1, ROW_TILE, n), lambda l, i: (l, i, 0)),
        compiler_params=_params(2),
        name="memkv",
    )(mem2d, g, w_bf16)


def _inproj_pool_kernel(x_ref, g_ref, w_ref, u_ref, qm_ref, *, tok_width):
    hn = _rmsnorm_rows(x_ref[...], g_ref[...]).astype(BF16)
    for c in range(0, tok_width, COL_CHUNK):
        u_ref[:, c:c + COL_CHUNK] = _dot(hn, w_ref[:, c:c + COL_CHUNK])
    qm_ref[...] = _dot(hn, w_ref[:, tok_width:]).astype(BF16)


def _inproj_pool(x2d, g, w_bf16, tok_width):
    t, d_model = x2d.shape
    n = w_bf16.shape[1]
    return pl.pallas_call(
        functools.partial(_inproj_pool_kernel, tok_width=tok_width),
        out_shape=(jax.ShapeDtypeStruct((t, tok_width), F32),
                   jax.ShapeDtypeStruct((t, n - tok_width), BF16)),
        grid=(t // INPROJ_ROWS,),
        in_specs=[
            pl.BlockSpec((INPROJ_ROWS, d_model), lambda i: (i, 0)),
            pl.BlockSpec((1, d_model), lambda i: (0, 0)),
            _resident((d_model, n), lambda i: (0, 0)),
        ],
        out_specs=(pl.BlockSpec((INPROJ_ROWS, tok_width), lambda i: (i, 0)),
                   pl.BlockSpec((INPROJ_ROWS, n - tok_width), lambda i: (i, 0))),
        compiler_params=_params(1),
        name="inproj_pool",
    )(x2d, g, w_bf16)


def _pool_group(u_ref, gw_ref, sc_ref, o_ref, *, window, seq):
    gwid = u_ref.shape[2]
    rows = seq + 2 * POOL_PAD
    half = window // 2
    assert 2 * half <= POOL_PAD
    zeros_pad = jnp.zeros((POOL_PAD, gwid), F32)
    u = u_ref[0]
    p = jnp.concatenate([zeros_pad, u, zeros_pad], axis=0)
    k = 1
    while k < half:
        p = p + pltpu.roll(p, rows - k, 0)
        k *= 2
    before = p if half % SUBLANES == 0 else pltpu.roll(p, half, 0)
    shift = half if half % SUBLANES == 0 else 0
    win = before[POOL_PAD - shift:POOL_PAD - shift + seq, :] + p[POOL_PAD:POOL_PAD + seq, :]
    t = lax.broadcasted_iota(I32, (seq, 1), 0)
    lo = jnp.maximum(t - half, 0)
    hi = jnp.minimum(t + half - 1, seq - 1)
    cnt = (hi - lo + 1).astype(F32)
    pooled = (win / cnt - u).astype(BF16)
    o_ref[0] = (_dot(pooled, gw_ref[0]) * sc_ref[0]).astype(BF16)


def _pool_kernel(u_ref, gw_ref, sc_ref, o_ref, *, seq):
    g = pl.program_id(1)
    for k, window in enumerate(POOL_WINDOWS):
        @pl.when(g == k)
        def _():
            _pool_group(u_ref, gw_ref, sc_ref, o_ref, window=window, seq=seq)


def _pool_mixer(u3d, gw_bf16, scale3d):
    b, seq, tok_width = u3d.shape
    n_groups, gwid, _ = gw_bf16.shape
    assert n_groups == len(POOL_WINDOWS) and n_groups * gwid == tok_width
    return pl.pallas_call(
        functools.partial(_pool_kernel, seq=seq),
        out_shape=jax.ShapeDtypeStruct((b, seq, tok_width), BF16),
        grid=(b, n_groups),
        in_specs=[
            pl.BlockSpec((1, seq, gwid), lambda i, g: (i, 0, g)),
            pl.BlockSpec((1, gwid, gwid), lambda i, g: (g, 0, 0)),
            pl.BlockSpec((1, 1, gwid), lambda i, g: (g, 0, 0)),
        ],
        out_specs=pl.BlockSpec((1, seq, gwid), lambda i, g: (i, 0, g)),
        compiler_params=_params(2),
        name="pool_mixer",
    )(u3d, gw_bf16, scale3d)


def _xattn_kernel(q_ref, kv_ref, o_ref):
    scale = XA_HEAD_DIM ** -0.5
    for h in range(XA_HEADS):
        lo = h * XA_HEAD_DIM
        q = q_ref[:, lo:lo + XA_HEAD_DIM]
        k = kv_ref[0, 0, :, lo:lo + XA_HEAD_DIM]
        v = kv_ref[0, 0, :, XA_WIDTH + lo:XA_WIDTH + lo + XA_HEAD_DIM]
        s = _dot_nt(q, k) * scale
        m = jnp.max(s, axis=-1, keepdims=True)
        p = jnp.exp(s - m)
        den = jnp.sum(p, axis=-1, keepdims=True)
        o_ref[:, lo:lo + XA_HEAD_DIM] = (_dot(p.astype(BF16), v) / den).astype(BF16)


XATTN_ROWS = 2048


def _mem_xattn(qm2d, memkv, layer, seq):
    t = qm2d.shape[0]
    tiles_per_seq = seq // XATTN_ROWS
    return pl.pallas_call(
        _xattn_kernel,
        out_shape=jax.ShapeDtypeStruct((t, XA_WIDTH), BF16),
        grid=(t // XATTN_ROWS,),
        in_specs=[
            pl.BlockSpec((XATTN_ROWS, XA_WIDTH), lambda i: (i, 0)),
            pl.BlockSpec((1, 1, MEM_LEN, 2 * XA_WIDTH), lambda i: (layer, i // tiles_per_seq, 0, 0)),
        ],
        out_specs=pl.BlockSpec((XATTN_ROWS, XA_WIDTH), lambda i: (i, 0)),
        compiler_params=_params(1),
        name="mem_xattn",
    )(qm2d, memkv)


def _outproj_kernel(tok_ref, mo_ref, x_ref, w_ref, g_ref, rw_ref,
                    x1_ref, h_ref, afft_ref, aff_ref, wcat_ref, x1prev_ref,
                    *, tok_width, n_experts):
    @pl.when(pl.program_id(0) == 0)
    def _():
        rw = rw_ref[0]
        w_hi = rw.astype(BF16)
        wcat_ref[:, 0:LANES] = w_hi
        wcat_ref[:, LANES:2 * LANES] = (rw - w_hi.astype(F32)).astype(BF16)
        x1prev_ref[...] = jnp.zeros_like(x1prev_ref)

    hn = _rmsnorm_rows(x1prev_ref[...], g_ref[...])
    h_prev = hn.astype(BF16)
    h_lo = (hn - h_prev.astype(F32)).astype(BF16)
    r = _dot(h_prev, wcat_ref[...]) + _dot(h_lo, wcat_ref[...])
    logits = r[:, 0:LANES] + r[:, LANES:2 * LANES]
    lt = logits.T[0:n_experts, :]
    m = jnp.max(lt, axis=0, keepdims=True)
    ex = jnp.exp(lt - m)
    afft = ex / jnp.sum(ex, axis=0, keepdims=True)
    afft_ref[0] = afft
    padded = jnp.concatenate(
        [afft, jnp.zeros((LANES - n_experts, afft.shape[1]), F32)], axis=0)
    aff = padded.T
    aff_ref[...] = aff
    h_ref[...] = _pack_row_words(h_prev, _pack_gate_lanes(aff))

    y = _dot(tok_ref[...], w_ref[0:tok_width, :]) + _dot(mo_ref[...], w_ref[tok_width:, :])
    x1 = x_ref[...] + y
    x1_ref[...] = x1
    x1prev_ref[...] = x1


def _outproj(tok2d, mo2d, x2d, w_bf16, g, rw_pad, layer, n_experts, seq):
    t, d_model = x2d.shape
    tok_width = tok2d.shape[1]
    tiles_per_seq = seq // ROW_TILE
    n_tiles = t // ROW_TILE

    def cur(i):
        return jnp.minimum(i, n_tiles - 1)

    def prev(i):
        return jnp.maximum(i - 1, 0)

    return pl.pallas_call(
        functools.partial(_outproj_kernel, tok_width=tok_width, n_experts=n_experts),
        out_shape=(jax.ShapeDtypeStruct((t, d_model), F32),
                   jax.ShapeDtypeStruct((t, d_model // 2 + LANES), U32),
                   jax.ShapeDtypeStruct((t // seq, n_experts, seq), F32),
                   jax.ShapeDtypeStruct((t, LANES), F32)),
        grid=(n_tiles + 1,),
        in_specs=[
            pl.BlockSpec((ROW_TILE, tok_width), lambda i: (cur(i), 0)),
            pl.BlockSpec((ROW_TILE, mo2d.shape[1]), lambda i: (cur(i), 0)),
            pl.BlockSpec((ROW_TILE, d_model), lambda i: (cur(i), 0)),
            _resident((d_model, d_model), lambda i: (0, 0)),
            pl.BlockSpec((1, d_model), lambda i: (0, 0)),
            _resident((1, d_model, LANES), lambda i: (layer, 0, 0)),
        ],
        out_specs=(pl.BlockSpec((ROW_TILE, d_model), lambda i: (cur(i), 0)),
                   pl.BlockSpec((ROW_TILE, d_model // 2 + LANES), lambda i: (prev(i), 0)),
                   pl.BlockSpec((1, n_experts, ROW_TILE),
                                lambda i: (prev(i) // tiles_per_seq, 0, prev(i) % tiles_per_seq)),
                   pl.BlockSpec((ROW_TILE, LANES), lambda i: (prev(i), 0))),
        scratch_shapes=[pltpu.VMEM((d_model, 2 * LANES), BF16),
                        pltpu.VMEM((ROW_TILE, d_model), F32)],
        compiler_params=_params(1),
        name="outproj_router",
    )(tok2d, mo2d, x2d, w_bf16, g, rw_pad)


def _strict_triangle(n, lower):
    r = lax.broadcasted_iota(I32, (n, n), 0)
    c = lax.broadcasted_iota(I32, (n, n), 1)
    return jnp.where((c < r) if lower else (r < c), 1.0, 0.0).astype(BF16)


def _prefix_rows(mask_f32):
    s, l = mask_f32.shape
    tri = _strict_triangle(PREFIX_CHUNK, lower=True)
    carry = jnp.zeros((1, l), F32)
    out = []
    for c in range(0, s, PREFIX_CHUNK):
        m = mask_f32[c:c + PREFIX_CHUNK, :]
        out.append(_dot(tri, m.astype(BF16)) + carry)
        carry = carry + jnp.sum(m, axis=0, keepdims=True)
    return jnp.concatenate(out, axis=0)


def _prefix_lanes(mask_f32):
    e, s = mask_f32.shape
    tri = _strict_triangle(PREFIX_CHUNK, lower=False)
    carry = jnp.zeros((e, 1), F32)
    out = []
    for c in range(0, s, PREFIX_CHUNK):
        m = mask_f32[:, c:c + PREFIX_CHUNK]
        out.append(_dot(m.astype(BF16), tri) + carry)
        carry = carry + jnp.sum(m, axis=1, keepdims=True)
    return jnp.concatenate(out, axis=1)


def _select_slots(key, thr, need, prefix_fn):
    return _select_slots_and_counts(key, thr, need, prefix_fn)[0]


def _select_slots_and_counts(key, thr, need, prefix_fn):
    gt = jnp.where(key > thr, 1.0, 0.0)
    eq = jnp.where(key == thr, 1.0, 0.0)
    eq_rank = prefix_fn(eq)
    sel = gt + eq * jnp.where(eq_rank < need, 1.0, 0.0)
    pos = prefix_fn(sel)
    return jnp.where(sel > 0.5, pos, -1.0), pos


GATE_GROUP = 16
GATE_PIECES = 3


def _pack_gate_lanes(aff):
    hi = aff.astype(BF16).astype(F32)
    r1 = aff - hi
    mid = r1.astype(BF16).astype(F32)
    lo = (r1 - mid).astype(BF16).astype(F32)
    packed = hi + pltpu.roll(mid, GATE_GROUP, 1) + pltpu.roll(lo, 2 * GATE_GROUP, 1)
    return packed.astype(BF16)


def _unpack_gate(tail, e):
    lane = lax.broadcasted_iota(I32, (1, LANES), 1)
    mine = ((lane & (GATE_GROUP - 1)) == e) & (lane < GATE_PIECES * GATE_GROUP)
    return jnp.sum(jnp.where(mine, tail.astype(F32), 0.0), axis=1, keepdims=True)


def _pack_row_words(h, gate_tile):
    rows, d_model = h.shape
    half = d_model // 2
    hi = jnp.concatenate([h[:, 0:half], gate_tile], axis=1).astype(F32)
    lo = jnp.concatenate([h[:, half:], jnp.zeros((rows, LANES), BF16)], axis=1).astype(F32)
    return pltpu.bitcast(hi, U32) | (pltpu.bitcast(lo, U32) >> 16)


def _unpack_row_words(words):
    hi = pltpu.bitcast(words & jnp.uint32(0xFFFF0000), F32)
    lo = pltpu.bitcast(words << 16, F32)
    return hi, lo


def _route_kernel(afft_ref, posm_ref, thr_ref, need_ref, starts_ref, *, cap):
    n_experts, seq = afft_ref.shape[1], afft_ref.shape[2]
    key = pltpu.bitcast(afft_ref[0], I32)
    thr = jnp.zeros((n_experts, 1), I32)
    for bit in range(F32_KEY_BITS - 1, -1, -1):
        cand = thr | (1 << bit)
        cnt = jnp.sum(jnp.where(key >= cand, 1.0, 0.0), axis=1, keepdims=True)
        thr = jnp.where(cnt >= cap, cand, thr)
    n_gt = jnp.sum(jnp.where(key > thr, 1.0, 0.0), axis=1, keepdims=True)
    need = cap - n_gt
    thr_ref[0] = thr
    need_ref[0] = need
    posm, before = _select_slots_and_counts(key, thr, need, _prefix_lanes)
    posm_ref[0] = posm.astype(I32)
    starts_ref[0] = jnp.concatenate(
        [before[:, r:r + 1] for r in range(0, seq, COMBINE_ROWS)], axis=1).astype(I32)


def _route(afft, cap):
    b, n_experts, seq = afft.shape
    n_tiles = seq // COMBINE_ROWS
    rows = b * n_experts
    outs = pl.pallas_call(
        functools.partial(_route_kernel, cap=cap),
        out_shape=(jax.ShapeDtypeStruct((1, rows, seq), I32),
                   jax.ShapeDtypeStruct((1, rows, 1), I32),
                   jax.ShapeDtypeStruct((1, rows, 1), F32),
                   jax.ShapeDtypeStruct((1, rows, n_tiles), I32)),
        grid=(1,),
        in_specs=[pl.BlockSpec((1, rows, seq), lambda i: (0, 0, 0))],
        out_specs=(pl.BlockSpec((1, rows, seq), lambda i: (0, 0, 0)),
                   pl.BlockSpec((1, rows, 1), lambda i: (0, 0, 0)),
                   pl.BlockSpec((1, rows, 1), lambda i: (0, 0, 0)),
                   pl.BlockSpec((1, rows, n_tiles), lambda i: (0, 0, 0))),
        compiler_params=_params(1),
        name="expert_route",
    )(afft.reshape(1, rows, seq))
    return tuple(o.reshape(b, n_experts, o.shape[2]) for o in outs)


SC_LANES = 16
SC_GATHER_ROWS = 64
SC_BACKGROUND_SHARE = 2


def _sc_expert_gather(posm2d, h_words, n_experts, e_offset, n_e, b, seq, cap, worker_share=1):
    width = h_words.shape[1]
    info = plsc.get_sparse_core_info()
    n_cores, n_subcores = info.num_cores, info.num_subcores
    n_workers = n_cores * n_subcores // worker_share
    assert info.num_lanes == SC_LANES and (b * n_e) % n_workers == 0
    pairs_per_worker = (b * n_e) // n_workers
    mesh = plsc.VectorSubcoreMesh(core_axis_name="c", subcore_axis_name="s")

    @functools.partial(
        pl.kernel, mesh=mesh,
        out_type=jax.ShapeDtypeStruct((n_e * b * cap, width), U32),
        compiler_params=pltpu.CompilerParams(needs_layout_passes=False),
        scratch_types=[
            pltpu.VMEM((seq,), I32),
            pltpu.VMEM((cap,), I32),
            pltpu.VMEM((SC_GATHER_ROWS, width), U32),
            pltpu.SemaphoreType.DMA,
        ],
        name="sc_expert_gather",
    )
    def gather(posm_hbm, h_hbm, out_hbm, pos_v, idx_v, rows_v, sem):
        wid = lax.axis_index("s") * n_cores + lax.axis_index("c")

        def move_pair(pair):
            bi = pair // n_e
            e = pair - bi * n_e
            pltpu.sync_copy(posm_hbm.at[bi * n_experts + e_offset + e], pos_v)

            @pl.loop(0, seq, step=SC_LANES)
            def _(t0):
                slots = pos_v[pl.ds(t0, SC_LANES)]
                rows = lax.iota(I32, SC_LANES) + (t0 + bi * seq)
                plsc.store_scatter(idx_v, [slots], rows, mask=slots >= 0)

            out_base = (e * b + bi) * cap
            for c in range(cap // SC_GATHER_ROWS):
                chunk = idx_v.at[pl.ds(c * SC_GATHER_ROWS, SC_GATHER_ROWS)]
                pltpu.async_copy(h_hbm.at[chunk], rows_v, sem).wait()
                pltpu.sync_copy(rows_v, out_hbm.at[pl.ds(out_base + c * SC_GATHER_ROWS, SC_GATHER_ROWS)])

        @pl.when(wid < n_workers)
        def _():
            for p in range(pairs_per_worker):
                move_pair(wid * pairs_per_worker + p)

    return gather(posm2d, h_words).reshape(n_e, b, cap, width)


EXPERT_ROWS = 1024
EXPERT_FTILE = 256


EXPERT_WBUFS = 2


def _expert_kernel(xg_ref, wg_hbm, wu_hbm, wd_hbm, o_ref,
                   x_ref, g_ref, hact_ref, wg_full, wu_full, wd_full, wg_buf, wu_buf, wd_buf, sem,
                   *, layer, e_offset, n_ftiles):
    e = pl.program_id(0)
    m = pl.program_id(1)
    n_e = pl.num_programs(0)
    nb, cap, width = xg_ref.shape[1], xg_ref.shape[2], xg_ref.shape[3]
    half = width - LANES
    d_model = 2 * half
    rows = nb * cap
    tf = EXPERT_FTILE
    assert n_ftiles % EXPERT_WBUFS == 0

    def tile_copies(expert, f):
        ge = e_offset + expert
        slot = f % EXPERT_WBUFS
        return (
            pltpu.make_async_copy(wg_hbm.at[layer, ge, :, pl.ds(f * tf, tf)], wg_buf.at[slot], sem.at[0, slot]),
            pltpu.make_async_copy(wu_hbm.at[layer, ge, :, pl.ds(f * tf, tf)], wu_buf.at[slot], sem.at[1, slot]),
            pltpu.make_async_copy(wd_hbm.at[layer, ge, pl.ds(f * tf, tf), :], wd_buf.at[slot], sem.at[2, slot]),
        )

    def start(expert, f):
        for cp in tile_copies(expert, f):
            cp.start()

    @pl.when((e == 0) & (m == 0))
    def _():
        for f in range(EXPERT_WBUFS):
            start(e, f)

    for i in range(nb):
        r0 = i * cap
        hi, lo = _unpack_row_words(xg_ref[0, i])
        x_ref[r0:r0 + cap, 0:half] = hi[:, 0:half].astype(BF16)
        x_ref[r0:r0 + cap, half:d_model] = lo[:, 0:half].astype(BF16)
        g_ref[r0:r0 + cap, :] = jnp.broadcast_to(
            _unpack_gate(hi[:, half:width], e_offset + e), (cap, LANES))

    def receive(f):
        slot = f % EXPERT_WBUFS
        for cp in tile_copies(e, f):
            cp.wait()
        wg_full[:, f * tf:(f + 1) * tf] = wg_buf[slot].astype(BF16)
        wu_full[:, f * tf:(f + 1) * tf] = wu_buf[slot].astype(BF16)
        wd_full[f * tf:(f + 1) * tf, :] = wd_buf[slot].astype(BF16)
        ahead = f + EXPERT_WBUFS
        if ahead < n_ftiles:
            start(e, ahead)
        else:
            @pl.when(e + 1 < n_e)
            def _():
                start(e + 1, ahead - n_ftiles)

    def body(first_group):
        x = x_ref[...]
        for f in range(n_ftiles):
            if first_group:
                receive(f)
            a = _dot(x, wg_full[:, f * tf:(f + 1) * tf])
            u = _dot(x, wu_full[:, f * tf:(f + 1) * tf])
            hact_ref[:, f * tf:(f + 1) * tf] = (a * jax.nn.sigmoid(a) * u).astype(BF16)
        for c in range(0, d_model, COL_CHUNK):
            y = _dot(hact_ref[...], wd_full[:, c:c + COL_CHUNK])
            for j in range(0, COL_CHUNK, LANES):
                o_ref[:, 0, :, c + j:c + j + LANES] = (
                    (y[:, j:j + LANES] * g_ref[...]).astype(BF16).reshape(nb, cap, LANES))

    @pl.when(m == 0)
    def _():
        body(True)

    @pl.when(m != 0)
    def _():
        body(False)


def _experts(xg, w_gate, w_up, w_down, layer, e_offset):
    n_e, b, cap, width = xg.shape
    d_model = 2 * (width - LANES)
    d_expert = w_gate.shape[3]
    nb = EXPERT_ROWS // cap
    n_ftiles = d_expert // EXPERT_FTILE
    return pl.pallas_call(
        functools.partial(_expert_kernel, layer=layer, e_offset=e_offset, n_ftiles=n_ftiles),
        out_shape=jax.ShapeDtypeStruct((b, n_e, cap, d_model), BF16),
        grid=(n_e, b // nb),
        in_specs=[
            pl.BlockSpec((1, nb, cap, width), lambda e, m: (e, m, 0, 0)),
            pl.BlockSpec(memory_space=pl.ANY),
            pl.BlockSpec(memory_space=pl.ANY),
            pl.BlockSpec(memory_space=pl.ANY),
        ],
        out_specs=pl.BlockSpec((nb, 1, cap, d_model), lambda e, m: (m, e, 0, 0)),
        scratch_shapes=[pltpu.VMEM((EXPERT_ROWS, d_model), BF16),
                        pltpu.VMEM((EXPERT_ROWS, LANES), F32),
                        pltpu.VMEM((EXPERT_ROWS, d_expert), BF16),
                        pltpu.VMEM((d_model, d_expert), BF16),
                        pltpu.VMEM((d_model, d_expert), BF16),
                        pltpu.VMEM((d_expert, d_model), BF16),
                        pltpu.VMEM((EXPERT_WBUFS, d_model, EXPERT_FTILE), F32),
                        pltpu.VMEM((EXPERT_WBUFS, d_model, EXPERT_FTILE), F32),
                        pltpu.VMEM((EXPERT_WBUFS, EXPERT_FTILE, d_model), F32),
                        pltpu.SemaphoreType.DMA((3, EXPERT_WBUFS))],
        compiler_params=_params(2),
        name="experts",
    )(xg, w_gate, w_up, w_down)


COMBINE_ROWS = 256
COMBINE_WINDOW = 64
MXU_DEPTH = 256
BF16_ROWS = 16


def _combine_kernel(starts_ref, aff_ref, thr_ref, need_ref, x1_ref, *rest, cap, n_experts, final_norm):
    n_y = len(rest) - (6 if final_norm else 5)
    y_hbm = rest[:n_y]
    g_ref = rest[n_y] if final_norm else None
    o_ref, post_ref, pfull_ref, ybuf, sem = rest[-5:]
    b = pl.program_id(0)
    t = pl.program_id(1)
    n_b = pl.num_programs(0)
    n_t = pl.num_programs(1)
    rows, win = COMBINE_ROWS, COMBINE_WINDOW
    experts_per_group = n_experts // n_y
    group_rows = ybuf.shape[0] // (2 * n_y)

    def ybuf_row(buf_slot, g):
        return pl.multiple_of((buf_slot * n_y + g) * group_rows, BF16_ROWS)

    n_chunks = sem.shape[1]
    chunks_per_group = n_chunks // n_y
    chunk_rows = group_rows // chunks_per_group
    slot = b % 2

    def chunk_copy(seq_idx, c, dst_slot):
        g, r = c // chunks_per_group, (c % chunks_per_group) * chunk_rows
        return pltpu.make_async_copy(y_hbm[g].at[seq_idx, pl.ds(r, chunk_rows), :],
                                     ybuf.at[pl.ds(ybuf_row(dst_slot, g) + r, chunk_rows), :],
                                     sem.at[dst_slot, c])

    @pl.when((b == 0) & (t == 0))
    def _():
        for c in range(n_chunks):
            chunk_copy(0, c, 0).start()

    @pl.when(t == 0)
    def _():
        for c in range(n_chunks):
            chunk_copy(b, c, slot).wait()
        key = pltpu.bitcast(aff_ref[...], I32)
        post_ref[...] = _select_slots(key, thr_ref[0], need_ref[0], _prefix_rows)

    for c in range(n_chunks):
        @pl.when((t == c) & (b + 1 < n_b))
        def _():
            chunk_copy(b + 1, c, 1 - slot).start()

    posm = post_ref[pl.ds(pl.multiple_of(t * rows, rows), rows), :]
    base = (b * (n_t + 1) + t) * n_experts
    wstart, ok = [], None
    for e in range(n_experts):
        first = starts_ref[base + e]
        end = starts_ref[base + n_experts + e]
        w0 = jnp.minimum((first // BF16_ROWS) * BF16_ROWS, cap - win)
        fits = end - w0 <= win
        wstart.append(w0)
        ok = fits if ok is None else jnp.logical_and(ok, fits)

    def finish(acc):
        if final_norm:
            acc = _rmsnorm_rows(acc, g_ref[...])
        o_ref[0] = acc

    @pl.when(ok)
    def _():
        per_dot, per_tile = MXU_DEPTH // win, LANES // win
        lane = lax.broadcasted_iota(I32, (1, LANES), 1)
        lane_f = lane.astype(F32)
        acc = x1_ref[0]
        for e0 in range(0, n_experts, per_dot):
            onehots, windows = [], []
            for e1 in range(e0, e0 + per_dot, per_tile):
                rel = None
                for k in range(per_tile - 1, -1, -1):
                    e = e1 + k
                    shifted = posm[:, e:e + 1] - (wstart[e] - k * win).astype(F32)
                    rel = shifted if rel is None else jnp.where(lane < (k + 1) * win, shifted, rel)
                onehots.append(jnp.where(rel == lane_f, 1.0, 0.0).astype(BF16))
            for e in range(e0, e0 + per_dot):
                g, el = divmod(e, experts_per_group)
                r0 = pl.multiple_of(ybuf_row(slot, g) + el * cap + wstart[e], BF16_ROWS)
                windows.append(ybuf[pl.ds(r0, win), :])
            acc = acc + _dot(jnp.concatenate(onehots, axis=1), jnp.concatenate(windows, axis=0))
        finish(acc)

    @pl.when(jnp.logical_not(ok))
    def _():
        slot_ids = lax.broadcasted_iota(I32, (1, cap), 1).astype(F32)
        for e in range(n_experts):
            pfull_ref[:, e * cap:(e + 1) * cap] = jnp.where(
                posm[:, e:e + 1] == slot_ids, 1.0, 0.0).astype(BF16)
        acc = x1_ref[0]
        for g in range(n_y):
            acc = acc + _dot(pfull_ref[:, g * group_rows:(g + 1) * group_rows],
                             ybuf[pl.ds(ybuf_row(slot, g), group_rows), :])
        finish(acc)


def _combine(starts, aff2d, thr_row, need_row, x1_3d, y_groups, cap, n_experts, final_g=None):
    b, seq, d_model = x1_3d.shape
    assert sum(y.shape[1] for y in y_groups) == n_experts * cap and cap >= COMBINE_WINDOW
    final_norm = final_g is not None
    n_tiles = seq // COMBINE_ROWS
    group_rows = y_groups[0].shape[1]
    assert all(y.shape[1] == group_rows for y in y_groups)
    assert n_tiles % len(y_groups) == 0 and group_rows % (n_tiles // len(y_groups)) == 0
    y_specs = [pl.BlockSpec(memory_space=pl.ANY) for _ in y_groups]
    g_specs = [pl.BlockSpec((1, d_model), lambda i, t, s: (0, 0))] if final_norm else []
    g_args = [final_g] if final_norm else []
    return pl.pallas_call(
        functools.partial(_combine_kernel, cap=cap, n_experts=n_experts, final_norm=final_norm),
        out_shape=jax.ShapeDtypeStruct((b, seq, d_model), F32),
        grid_spec=pltpu.PrefetchScalarGridSpec(
            num_scalar_prefetch=1,
            grid=(b, n_tiles),
            in_specs=[
                pl.BlockSpec((seq, LANES), lambda i, t, s: (i, 0)),
                pl.BlockSpec((1, 1, LANES), lambda i, t, s: (i, 0, 0)),
                pl.BlockSpec((1, 1, LANES), lambda i, t, s: (i, 0, 0)),
                pl.BlockSpec((1, COMBINE_ROWS, d_model), lambda i, t, s: (i, t, 0)),
            ] + y_specs + g_specs,
            out_specs=pl.BlockSpec((1, COMBINE_ROWS, d_model), lambda i, t, s: (i, t, 0)),
            scratch_shapes=[pltpu.VMEM((seq, LANES), F32),
                            pltpu.VMEM((COMBINE_ROWS, n_experts * cap), BF16),
                            pltpu.VMEM((2 * len(y_groups) * group_rows, d_model), BF16),
                            pltpu.SemaphoreType.DMA((2, n_tiles))],
        ),
        compiler_params=_params(2),
        name="combine",
    )(starts, aff2d, thr_row, need_row, x1_3d, *y_groups, *g_args)


EXPERT_GROUPS = 2


def _moe(afft, aff2d, h_words, x1_2d, w_gate, w_up, w_down, layer, b, seq, final_g=None):
    n_experts = afft.shape[1]
    assert n_experts <= GATE_GROUP and n_experts % EXPERT_GROUPS == 0
    d_model = x1_2d.shape[1]
    cap = CAPACITY_FACTOR * seq // n_experts
    n_e = n_experts // EXPERT_GROUPS
    posm, thr, need, tile_starts = _route(afft, cap)
    starts = jnp.concatenate([jnp.swapaxes(tile_starts, 1, 2),
                              jnp.full((b, 1, n_experts), cap, I32)], axis=1).reshape(-1)
    posm2d = posm.reshape(b * n_experts, seq)
    xgs = [_sc_expert_gather(posm2d, h_words, n_experts, g * n_e, n_e, b, seq, cap,
                             worker_share=1 if g == EXPERT_GROUPS - 1 else SC_BACKGROUND_SHARE)
           for g in range(EXPERT_GROUPS)]
    ys = [_experts(xg, w_gate, w_up, w_down, layer, g * n_e).reshape(b, n_e * cap, d_model)
          for g, xg in enumerate(xgs)]
    pad = LANES - n_experts
    thr_row = jnp.pad(thr.reshape(b, 1, n_experts), ((0, 0), (0, 0), (0, pad)),
                      constant_values=np.iinfo(np.int32).max)
    need_row = jnp.pad(need.reshape(b, 1, n_experts), ((0, 0), (0, 0), (0, pad)))
    return _combine(starts, aff2d, thr_row, need_row, x1_2d.reshape(b, seq, d_model), ys, cap,
                    n_experts, final_g)


def _rotary_tile(t, cos, sin_lo, sin_hi):
    half = ROT_DIM // 2
    return t * cos + pltpu.roll(t, LANES - half, 1) * sin_lo + pltpu.roll(t, half, 1) * sin_hi


def _inproj_attn_kernel(x_ref, g_ref, w_ref, pos_ref, rot_ref,
                        q_ref, k_ref, v_ref, qm_ref, *, tok_width, kv_width):
    qscale = HEAD_DIM ** -0.5
    rows = x_ref.shape[0] // INPROJ_PARTS

    def prepare(part):
        r = slice(part * rows, (part + 1) * rows)
        hn = _rmsnorm_rows(x_ref[r, :], g_ref[...]).astype(BF16)
        ang = pos_ref[r, :].astype(F32) * rot_ref[0:1, :]
        cos = jnp.cos(ang)
        sin = jnp.sin(ang)
        return hn, cos, sin * rot_ref[1:2, :], sin * rot_ref[2:3, :]

    def project(part, hn, cos, sin_lo, sin_hi):
        r = slice(part * rows, (part + 1) * rows)
        for c in range(0, tok_width, COL_CHUNK):
            pc = _dot(hn, w_ref[:, c:c + COL_CHUNK])
            for j in range(0, COL_CHUNK, LANES):
                rot = _rotary_tile(pc[:, j:j + LANES], cos, sin_lo, sin_hi)
                q_ref[r, c + j:c + j + LANES] = (rot * qscale).astype(BF16)
        kv = _dot(hn, w_ref[:, tok_width:tok_width + 2 * kv_width])
        k01 = _rotary_tile(kv[:, 0:LANES], cos, sin_lo, sin_hi)
        k2x = _rotary_tile(kv[:, LANES:2 * LANES], cos, sin_lo, sin_hi)
        k_ref[0, 0, r, :] = k01[:, 0:HEAD_DIM].astype(BF16)
        k_ref[0, 1, r, :] = k01[:, HEAD_DIM:LANES].astype(BF16)
        k_ref[0, 2, r, :] = k2x[:, 0:HEAD_DIM].astype(BF16)
        for hh in range(kv_width // HEAD_DIM):
            lo = kv_width + hh * HEAD_DIM
            v_ref[0, hh, r, :] = kv[:, lo:lo + HEAD_DIM].astype(BF16)
        qm_ref[r, :] = _dot(hn, w_ref[:, tok_width + 2 * kv_width:]).astype(BF16)

    prepared = [prepare(part) for part in range(INPROJ_PARTS)]
    for part in range(INPROJ_PARTS):
        project(part, *prepared[part])


def _inproj_attn(x2d, g, w_bf16, pos2d, rot_rows, tok_width, kv_width, seq):
    t, d_model = x2d.shape
    n = w_bf16.shape[1]
    n_kv = kv_width // HEAD_DIM
    assert n_kv == 3 and kv_width + HEAD_DIM == 2 * LANES
    tiles_per_seq = seq // INPROJ_ROWS
    kv_spec = pl.BlockSpec((1, n_kv, INPROJ_ROWS, HEAD_DIM),
                           lambda i: (i // tiles_per_seq, 0, i % tiles_per_seq, 0))
    return pl.pallas_call(
        functools.partial(_inproj_attn_kernel, tok_width=tok_width, kv_width=kv_width),
        out_shape=(jax.ShapeDtypeStruct((t, tok_width), BF16),
                   jax.ShapeDtypeStruct((t // seq, n_kv, seq, HEAD_DIM), BF16),
                   jax.ShapeDtypeStruct((t // seq, n_kv, seq, HEAD_DIM), BF16),
                   jax.ShapeDtypeStruct((t, n - tok_width - 2 * kv_width), BF16)),
        grid=(t // INPROJ_ROWS,),
        in_specs=[
            pl.BlockSpec((INPROJ_ROWS, d_model), lambda i: (i, 0)),
            pl.BlockSpec((1, d_model), lambda i: (0, 0)),
            _resident((d_model, n), lambda i: (0, 0)),
            pl.BlockSpec((INPROJ_ROWS, 1), lambda i: (i, 0)),
            pl.BlockSpec((8, LANES), lambda i: (0, 0)),
        ],
        out_specs=(pl.BlockSpec((INPROJ_ROWS, tok_width), lambda i: (i, 0)),
                   kv_spec, kv_spec,
                   pl.BlockSpec((INPROJ_ROWS, n - tok_width - 2 * kv_width), lambda i: (i, 0))),
        compiler_params=_params(1),
        name="inproj_attn",
    )(x2d, g, w_bf16, pos2d, rot_rows)


WATTN_QBLOCKS = 4


def _wattn_kernel(sink_ref, q_ref, kp_ref, kc_ref, kn_ref, vp_ref, vc_ref, vn_ref, o_ref,
                  valid_ref, kpad_ref, vpad_ref, s_ref, p_ref, inv_ref, *, seq):
    step = pl.program_id(1)
    n_kv = kc_ref.shape[1]
    pairs = GQA_RATIO // 2
    half_rows = pairs * BLOCK
    key_rows = (WATTN_QBLOCKS + 2) * BLOCK
    zeros = jnp.zeros((key_rows, HEAD_DIM), BF16)
    ones_col = jnp.where(lax.broadcasted_iota(I32, (key_rows, HEAD_DIM), 1) == 0, 1.0, 0.0).astype(BF16)
    low_half = lax.broadcasted_iota(I32, (1, LANES), 1) < HEAD_DIM
    for hk in range(n_kv):
        kw = jnp.concatenate([kp_ref[0, hk], kc_ref[0, hk], kn_ref[0, hk]], axis=0)
        vw = jnp.concatenate([vp_ref[0, hk], vc_ref[0, hk], vn_ref[0, hk]], axis=0)
        kpad_ref[2 * hk] = jnp.concatenate([kw, zeros], axis=1)
        kpad_ref[2 * hk + 1] = jnp.concatenate([zeros, kw], axis=1)
        vpad_ref[2 * hk] = jnp.concatenate([vw, ones_col], axis=1)
        vpad_ref[2 * hk + 1] = jnp.concatenate([ones_col, vw], axis=1)

    qi = lax.broadcasted_iota(I32, (BLOCK, 3 * BLOCK), 0)
    kj = lax.broadcasted_iota(I32, (BLOCK, 3 * BLOCK), 1)
    for qb in range(WATTN_QBLOCKS):
        n = step * WATTN_QBLOCKS + qb
        k0 = qb * BLOCK
        first = jnp.maximum(qi, BLOCK - n * BLOCK)
        last = jnp.minimum(qi + 2 * WINDOW, seq + BLOCK - 1 - n * BLOCK)
        valid_ref[qb] = jnp.where(((kj - first) | (last - kj)) >= 0, 1.0, 0.0)
        q0 = qb * BLOCK
        for hk in range(n_kv):
            tile0 = hk * pairs
            qs = jnp.concatenate(
                [q_ref[0, q0:q0 + BLOCK, (tile0 + j) * LANES:(tile0 + j + 1) * LANES]
                 for j in range(pairs)], axis=0)
            s_ref[qb, hk, 0:half_rows, :] = _dot_nt(qs, kpad_ref[2 * hk, k0:k0 + 3 * BLOCK, :])
            s_ref[qb, hk, half_rows:2 * half_rows, :] = _dot_nt(
                qs, kpad_ref[2 * hk + 1, k0:k0 + 3 * BLOCK, :])
        for hk in range(n_kv):
            for c in range(GQA_RATIO):
                j, odd = c % pairs, c // pairs
                r = c * BLOCK
                s = jnp.concatenate([
                    jnp.where(valid_ref[qb, :, 0:BLOCK] > 0.5, s_ref[qb, hk, r:r + BLOCK, 0:BLOCK], NEG_INF),
                    s_ref[qb, hk, r:r + BLOCK, BLOCK:2 * BLOCK],
                    jnp.where(valid_ref[qb, :, 2 * BLOCK:] > 0.5,
                              s_ref[qb, hk, r:r + BLOCK, 2 * BLOCK:], NEG_INF),
                ], axis=1)
                sk = sink_ref[hk * GQA_RATIO + 2 * j + odd]
                m = jnp.maximum(jnp.max(s, axis=-1, keepdims=True), sk)
                p_ref[qb, hk, r:r + BLOCK, :] = jnp.exp(s - m).astype(BF16)
                inv_ref[qb, hk, j * BLOCK:(j + 1) * BLOCK, odd * HEAD_DIM:(odd + 1) * HEAD_DIM] = (
                    jnp.broadcast_to(jnp.exp(sk - m), (BLOCK, HEAD_DIM)))
        for hk in range(n_kv):
            pv_even = _dot(p_ref[qb, hk, 0:half_rows, :], vpad_ref[2 * hk, k0:k0 + 3 * BLOCK, :])
            pv_odd = _dot(p_ref[qb, hk, half_rows:2 * half_rows, :],
                          vpad_ref[2 * hk + 1, k0:k0 + 3 * BLOCK, :])
            den = (jnp.where(low_half, pv_even[:, HEAD_DIM:HEAD_DIM + 1], pv_odd[:, 0:1])
                   + inv_ref[qb, hk])
            o = jnp.where(low_half, pv_even, pv_odd) / den
            for j in range(pairs):
                lo = (hk * pairs + j) * LANES
                o_ref[0, q0:q0 + BLOCK, lo:lo + LANES] = o[j * BLOCK:(j + 1) * BLOCK].astype(BF16)


def _window_attention(sink, q3d, k4d, v4d):
    b, seq, tok_width = q3d.shape
    n_kv = k4d.shape[1]
    nb = seq // BLOCK
    qb = WATTN_QBLOCKS
    assert nb % qb == 0
    edge_block = (1, n_kv, BLOCK, HEAD_DIM)
    prev_spec = pl.BlockSpec(edge_block, lambda i, s: (i, 0, jnp.maximum(s * qb - 1, 0), 0))
    cur_spec = pl.BlockSpec((1, n_kv, qb * BLOCK, HEAD_DIM), lambda i, s: (i, 0, s, 0))
    next_spec = pl.BlockSpec(edge_block, lambda i, s: (i, 0, jnp.minimum(s * qb + qb, nb - 1), 0))
    key_rows = (qb + 2) * BLOCK
    return pl.pallas_call(
        functools.partial(_wattn_kernel, seq=seq),
        out_shape=jax.ShapeDtypeStruct((b, seq, tok_width), BF16),
        grid=(b, nb // qb),
        in_specs=[
            pl.BlockSpec(memory_space=pltpu.SMEM),
            pl.BlockSpec((1, qb * BLOCK, tok_width), lambda i, s: (i, s, 0)),
            prev_spec, cur_spec, next_spec, prev_spec, cur_spec, next_spec,
        ],
        out_specs=pl.BlockSpec((1, qb * BLOCK, tok_width), lambda i, s: (i, s, 0)),
        scratch_shapes=[pltpu.VMEM((qb, BLOCK, 3 * BLOCK), F32),
                        pltpu.VMEM((2 * n_kv, key_rows, LANES), BF16),
                        pltpu.VMEM((2 * n_kv, key_rows, LANES), BF16),
                        pltpu.VMEM((qb, n_kv, GQA_RATIO * BLOCK, 3 * BLOCK), F32),
                        pltpu.VMEM((qb, n_kv, GQA_RATIO * BLOCK, 3 * BLOCK), BF16),
                        pltpu.VMEM((qb, n_kv, GQA_RATIO // 2 * BLOCK, LANES), F32)],
        compiler_params=_params(2),
        name="window_attention",
    )(sink, q3d, k4d, k4d, k4d, v4d, v4d, v4d)


def _rotary_rows(dtype=F32):
    half = ROT_DIM // 2
    inv_freq = ROPE_THETA ** (-jnp.arange(0, ROT_DIM, 2, dtype=jnp.float32) / ROT_DIM)
    lane = np.arange(LANES) % HEAD_DIM
    rotated = lane < ROT_DIM
    freq = jnp.where(jnp.asarray(rotated), inv_freq[jnp.asarray(lane % half)], 0.0)
    rows = jnp.zeros((8, LANES), dtype)
    rows = rows.at[0].set(freq)
    rows = rows.at[1].set(jnp.asarray(np.where(lane < half, -1.0, 0.0), dtype))
    rows = rows.at[2].set(jnp.asarray(np.where(rotated & (lane >= half), 1.0, 0.0), dtype))
    return rows


def kernel(x, mem, positions, norm_mix_g, norm_ffn_g, mem_norm_g, final_g, mem_w_kv,
           pool_w_in, pool_group_w, pool_scale, pool_w_out,
           attn_w_in, attn_sink, attn_w_out,
           router_w, exp_w_gate, exp_w_up, exp_w_down):
    b, seq, d_model = x.shape
    depth = norm_mix_g.shape[0]
    t = b * seq
    n_experts = router_w.shape[2]
    tok_width = pool_scale.shape[1]
    n_groups = pool_group_w.shape[1]
    kv_width = (attn_w_in.shape[2] - tok_width - XA_WIDTH) // 2
    assert seq % ROW_TILE == 0 and seq % INPROJ_ROWS == 0 and mem.shape[1] == MEM_LEN

    memkv = _memkv(mem.reshape(b * MEM_LEN, d_model), mem_norm_g.reshape(1, d_model),
                   mem_w_kv.astype(BF16))
    memkv = memkv.reshape(depth, b, MEM_LEN, 2 * XA_WIDTH)
    rw_pad = jnp.pad(router_w, ((0, 0), (0, 0), (0, LANES - n_experts)))
    pos2d = positions.reshape(t, 1)
    rot_rows = _rotary_rows()

    x2d = x.reshape(t, d_model)
    for layer in range(depth):
        j = layer // 2
        g_mix = norm_mix_g[layer].reshape(1, d_model)
        if layer % 2 == 0:
            u, qm = _inproj_pool(x2d, g_mix, pool_w_in[j].astype(BF16), tok_width)
            tok = _pool_mixer(u.reshape(b, seq, tok_width), pool_group_w[j].astype(BF16),
                              pool_scale[j].reshape(n_groups, 1, tok_width // n_groups))
            tok = tok.reshape(t, tok_width)
            w_out = pool_w_out[j]
        else:
            q, k, v, qm = _inproj_attn(x2d, g_mix, attn_w_in[j].astype(BF16), pos2d, rot_rows,
                                       tok_width, kv_width, seq)
            tok = _window_attention(attn_sink[j], q.reshape(b, seq, tok_width), k, v)
            tok = tok.reshape(t, tok_width)
            w_out = attn_w_out[j]
        mo = _mem_xattn(qm, memkv, layer, seq)
        x1, h, afft, aff = _outproj(tok, mo, x2d, w_out.astype(BF16),
                                    norm_ffn_g[layer].reshape(1, d_model), rw_pad, layer,
                                    n_experts, seq)
        last = layer == depth - 1
        x2 = _moe(afft, aff, h, x1, exp_w_gate, exp_w_up, exp_w_down, layer, b, seq,
                  final_g.reshape(1, d_model) if last else None)
        x2d = x2.reshape(t, d_model)
    return x2d.reshape(b, seq, d_model)
```

```python
import functools

import jax
import jax.numpy as jnp
import numpy as np
from jax import lax
from jax.experimental import pallas as pl
from jax.experimental.pallas import tpu as pltpu
from jax.experimental.pallas import tpu_sc as plsc

F32 = jnp.float32
BF16 = jnp.bfloat16
I32 = jnp.int32
U32 = jnp.uint32

EPS = 1e-6
MEM_LEN = 256
XA_HEADS = 4
XA_HEAD_DIM = 128
XA_WIDTH = XA_HEADS * XA_HEAD_DIM
POOL_WINDOWS = (2, 4, 8, 16)
HEAD_DIM = 64
GQA_RATIO = 8
WINDOW = 128
BLOCK = 128
ROPE_THETA = 500000.0
ROT_DIM = 16
NEG_INF = -1e30
CAPACITY_FACTOR = 2

LANES = 128
SUBLANES = 8
MIB = 1024 * 1024
VMEM_LIMIT_BYTES = 56 * MIB

ROW_TILE = 512
INPROJ_ROWS = 1024
INPROJ_PARTS = 4
POOL_PAD = 16
PREFIX_CHUNK = 256
COL_CHUNK = 512
F32_KEY_BITS = 31


def _params(n_grid_dims):
    return pltpu.CompilerParams(
        dimension_semantics=("arbitrary",) * n_grid_dims,
        vmem_limit_bytes=VMEM_LIMIT_BYTES,
    )


def _resident(block_shape, index_map):
    return pl.BlockSpec(block_shape, index_map, pipeline_mode=pl.Buffered(1))


def _rmsnorm_rows(x, g):
    return x * lax.rsqrt(jnp.mean(x * x, axis=-1, keepdims=True) + EPS) * g


def _dot(a, b):
    return jnp.dot(a, b, preferred_element_type=F32)


def _dot_nt(a, b):
    return lax.dot_general(a, b, (((1,), (1,)), ((), ())), preferred_element_type=F32)


def _memkv_kernel(mem_ref, g_ref, w_ref, o_ref):
    hn = _rmsnorm_rows(mem_ref[...], g_ref[...]).astype(BF16)
    o_ref[0] = _dot(hn, w_ref[0]).astype(BF16)


def _memkv(mem2d, g, w_bf16):
    depth, d_model, n = w_bf16.shape
    rows = mem2d.shape[0]
    return pl.pallas_call(
        _memkv_kernel,
        out_shape=jax.ShapeDtypeStruct((depth, rows, n), BF16),
        grid=(depth, rows // ROW_TILE),
        in_specs=[
            pl.BlockSpec((ROW_TILE, d_model), lambda l, i: (i, 0)),
            pl.BlockSpec((1, d_model), lambda l, i: (0, 0)),
            pl.BlockSpec((1, d_model, n), lambda l, i: (l, 0, 0)),
        ],
        out_specs=pl.BlockSpec((1, ROW_TILE, n), lambda l, i: (l, i, 0)),
        compiler_params=_params(2),
        name="memkv",
    )(mem2d, g, w_bf16)


def _inproj_pool_kernel(x_ref, g_ref, w_ref, u_ref, qm_ref, *, tok_width):
    hn = _rmsnorm_rows(x_ref[...], g_ref[...]).astype(BF16)
    for c in range(0, tok_width, COL_CHUNK):
        u_ref[:, c:c + COL_CHUNK] = _dot(hn, w_ref[:, c:c + COL_CHUNK])
    qm_ref[...] = _dot(hn, w_ref[:, tok_width:]).astype(BF16)


def _inproj_pool(x2d, g, w_bf16, tok_width):
    t, d_model = x2d.shape
    n = w_bf16.shape[1]
    return pl.pallas_call(
        functools.partial(_inproj_pool_kernel, tok_width=tok_width),
        out_shape=(jax.ShapeDtypeStruct((t, tok_width), F32),
                   jax.ShapeDtypeStruct((t, n - tok_width), BF16)),
        grid=(t // INPROJ_ROWS,),
        in_specs=[
            pl.BlockSpec((INPROJ_ROWS, d_model), lambda i: (i, 0)),
            pl.BlockSpec((1, d_model), lambda i: (0, 0)),
            _resident((d_model, n), lambda i: (0, 0)),
        ],
        out_specs=(pl.BlockSpec((INPROJ_ROWS, tok_width), lambda i: (i, 0)),
                   pl.BlockSpec((INPROJ_ROWS, n - tok_width), lambda i: (i, 0))),
        compiler_params=_params(1),
        name="inproj_pool",
    )(x2d, g, w_bf16)


def _pool_group(u_ref, gw_ref, sc_ref, o_ref, *, window, seq):
    gwid = u_ref.shape[2]
    rows = seq + 2 * POOL_PAD
    half = window // 2
    assert 2 * half <= POOL_PAD
    zeros_pad = jnp.zeros((POOL_PAD, gwid), F32)
    u = u_ref[0]
    p = jnp.concatenate([zeros_pad, u, zeros_pad], axis=0)
    k = 1
    while k < half:
        p = p + pltpu.roll(p, rows - k, 0)
        k *= 2
    before = p if half % SUBLANES == 0 else pltpu.roll(p, half, 0)
    shift = half if half % SUBLANES == 0 else 0
    win = before[POOL_PAD - shift:POOL_PAD - shift + seq, :] + p[POOL_PAD:POOL_PAD + seq, :]
    t = lax.broadcasted_iota(I32, (seq, 1), 0)
    lo = jnp.maximum(t - half, 0)
    hi = jnp.minimum(t + half - 1, seq - 1)
    cnt = (hi - lo + 1).astype(F32)
    pooled = (win / cnt - u).astype(BF16)
    o_ref[0] = (_dot(pooled, gw_ref[0]) * sc_ref[0]).astype(BF16)


def _pool_kernel(u_ref, gw_ref, sc_ref, o_ref, *, seq):
    g = pl.program_id(1)
    for k, window in enumerate(POOL_WINDOWS):
        @pl.when(g == k)
        def _():
            _pool_group(u_ref, gw_ref, sc_ref, o_ref, window=window, seq=seq)


def _pool_mixer(u3d, gw_bf16, scale3d):
    b, seq, tok_width = u3d.shape
    n_groups, gwid, _ = gw_bf16.shape
    assert n_groups == len(POOL_WINDOWS) and n_groups * gwid == tok_width
    return pl.pallas_call(
        functools.partial(_pool_kernel, seq=seq),
        out_shape=jax.ShapeDtypeStruct((b, seq, tok_width), BF16),
        grid=(b, n_groups),
        in_specs=[
            pl.BlockSpec((1, seq, gwid), lambda i, g: (i, 0, g)),
            pl.BlockSpec((1, gwid, gwid), lambda i, g: (g, 0, 0)),
            pl.BlockSpec((1, 1, gwid), lambda i, g: (g, 0, 0)),
        ],
        out_specs=pl.BlockSpec((1, seq, gwid), lambda i, g: (i, 0, g)),
        compiler_params=_params(2),
        name="pool_mixer",
    )(u3d, gw_bf16, scale3d)


def _xattn_kernel(q_ref, kv_ref, o_ref):
    scale = XA_HEAD_DIM ** -0.5
    for h in range(XA_HEADS):
        lo = h * XA_HEAD_DIM
        q = q_ref[:, lo:lo + XA_HEAD_DIM]
        k = kv_ref[0, 0, :, lo:lo + XA_HEAD_DIM]
        v = kv_ref[0, 0, :, XA_WIDTH + lo:XA_WIDTH + lo + XA_HEAD_DIM]
        s = _dot_nt(q, k) * scale
        m = jnp.max(s, axis=-1, keepdims=True)
        p = jnp.exp(s - m)
        den = jnp.sum(p, axis=-1, keepdims=True)
        o_ref[:, lo:lo + XA_HEAD_DIM] = (_dot(p.astype(BF16), v) / den).astype(BF16)


XATTN_ROWS = 2048


def _mem_xattn(qm2d, memkv, layer, seq):
    t = qm2d.shape[0]
    tiles_per_seq = seq // XATTN_ROWS
    return pl.pallas_call(
        _xattn_kernel,
        out_shape=jax.ShapeDtypeStruct((t, XA_WIDTH), BF16),
        grid=(t // XATTN_ROWS,),
        in_specs=[
            pl.BlockSpec((XATTN_ROWS, XA_WIDTH), lambda i: (i, 0)),
            pl.BlockSpec((1, 1, MEM_LEN, 2 * XA_WIDTH), lambda i: (layer, i // tiles_per_seq, 0, 0)),
        ],
        out_specs=pl.BlockSpec((XATTN_ROWS, XA_WIDTH), lambda i: (i, 0)),
        compiler_params=_params(1),
        name="mem_xattn",
    )(qm2d, memkv)


def _outproj_kernel(tok_ref, mo_ref, x_ref, w_ref, g_ref, rw_ref,
                    x1_ref, h_ref, afft_ref, aff_ref, wcat_ref, x1prev_ref,
                    *, tok_width, n_experts):
    @pl.when(pl.program_id(0) == 0)
    def _():
        rw = rw_ref[0]
        w_hi = rw.astype(BF16)
        wcat_ref[:, 0:LANES] = w_hi
        wcat_ref[:, LANES:2 * LANES] = (rw - w_hi.astype(F32)).astype(BF16)
        x1prev_ref[...] = jnp.zeros_like(x1prev_ref)

    hn = _rmsnorm_rows(x1prev_ref[...], g_ref[...])
    h_prev = hn.astype(BF16)
    h_lo = (hn - h_prev.astype(F32)).astype(BF16)
    r = _dot(h_prev, wcat_ref[...]) + _dot(h_lo, wcat_ref[...])
    logits = r[:, 0:LANES] + r[:, LANES:2 * LANES]
    lt = logits.T[0:n_experts, :]
    m = jnp.max(lt, axis=0, keepdims=True)
    ex = jnp.exp(lt - m)
    afft = ex / jnp.sum(ex, axis=0, keepdims=True)
    afft_ref[0] = afft
    padded = jnp.concatenate(
        [afft, jnp.zeros((LANES - n_experts, afft.shape[1]), F32)], axis=0)
    aff = padded.T
    aff_ref[...] = aff
    h_ref[...] = _pack_row_words(h_prev, _pack_gate_lanes(aff))

    y = _dot(tok_ref[...], w_ref[0:tok_width, :]) + _dot(mo_ref[...], w_ref[tok_width:, :])
    x1 = x_ref[...] + y
    x1_ref[...] = x1
    x1prev_ref[...] = x1


def _outproj(tok2d, mo2d, x2d, w_bf16, g, rw_pad, layer, n_experts, seq):
    t, d_model = x2d.shape
    tok_width = tok2d.shape[1]
    tiles_per_seq = seq // ROW_TILE
    n_tiles = t // ROW_TILE

    def cur(i):
        return jnp.minimum(i, n_tiles - 1)

    def prev(i):
        return jnp.maximum(i - 1, 0)

    return pl.pallas_call(
        functools.partial(_outproj_kernel, tok_width=tok_width, n_experts=n_experts),
        out_shape=(jax.ShapeDtypeStruct((t, d_model), F32),
                   jax.ShapeDtypeStruct((t, d_model // 2 + LANES), U32),
                   jax.ShapeDtypeStruct((t // seq, n_experts, seq), F32),
                   jax.ShapeDtypeStruct((t, LANES), F32)),
        grid=(n_tiles + 1,),
        in_specs=[
            pl.BlockSpec((ROW_TILE, tok_width), lambda i: (cur(i), 0)),
            pl.BlockSpec((ROW_TILE, mo2d.shape[1]), lambda i: (cur(i), 0)),
            pl.BlockSpec((ROW_TILE, d_model), lambda i: (cur(i), 0)),
            _resident((d_model, d_model), lambda i: (0, 0)),
            pl.BlockSpec((1, d_model), lambda i: (0, 0)),
            _resident((1, d_model, LANES), lambda i: (layer, 0, 0)),
        ],
        out_specs=(pl.BlockSpec((ROW_TILE, d_model), lambda i: (cur(i), 0)),
                   pl.BlockSpec((ROW_TILE, d_model // 2 + LANES), lambda i: (prev(i), 0)),
                   pl.BlockSpec((1, n_experts, ROW_TILE),
                                lambda i: (prev(i) // tiles_per_seq, 0, prev(i) % tiles_per_seq)),
                   pl.BlockSpec((ROW_TILE, LANES), lambda i: (prev(i), 0))),
        scratch_shapes=[pltpu.VMEM((d_model, 2 * LANES), BF16),
                        pltpu.VMEM((ROW_TILE, d_model), F32)],
        compiler_params=_params(1),
        name="outproj_router",
    )(tok2d, mo2d, x2d, w_bf16, g, rw_pad)


def _strict_triangle(n, lower):
    r = lax.broadcasted_iota(I32, (n, n), 0)
    c = lax.broadcasted_iota(I32, (n, n), 1)
    return jnp.where((c < r) if lower else (r < c), 1.0, 0.0).astype(BF16)


def _prefix_rows(mask_f32):
    s, l = mask_f32.shape
    tri = _strict_triangle(PREFIX_CHUNK, lower=True)
    carry = jnp.zeros((1, l), F32)
    out = []
    for c in range(0, s, PREFIX_CHUNK):
        m = mask_f32[c:c + PREFIX_CHUNK, :]
        out.append(_dot(tri, m.astype(BF16)) + carry)
        carry = carry + jnp.sum(m, axis=0, keepdims=True)
    return jnp.concatenate(out, axis=0)


def _prefix_lanes(mask_f32):
    e, s = mask_f32.shape
    tri = _strict_triangle(PREFIX_CHUNK, lower=False)
    carry = jnp.zeros((e, 1), F32)
    out = []
    for c in range(0, s, PREFIX_CHUNK):
        m = mask_f32[:, c:c + PREFIX_CHUNK]
        out.append(_dot(m.astype(BF16), tri) + carry)
        carry = carry + jnp.sum(m, axis=1, keepdims=True)
    return jnp.concatenate(out, axis=1)


def _select_slots(key, thr, need, prefix_fn):
    return _select_slots_and_counts(key, thr, need, prefix_fn)[0]


def _select_slots_and_counts(key, thr, need, prefix_fn):
    gt = jnp.where(key > thr, 1.0, 0.0)
    eq = jnp.where(key == thr, 1.0, 0.0)
    eq_rank = prefix_fn(eq)
    sel = gt + eq * jnp.where(eq_rank < need, 1.0, 0.0)
    pos = prefix_fn(sel)
    return jnp.where(sel > 0.5, pos, -1.0), pos


GATE_GROUP = 16
GATE_PIECES = 3


def _pack_gate_lanes(aff):
    hi = aff.astype(BF16).astype(F32)
    r1 = aff - hi
    mid = r1.astype(BF16).astype(F32)
    lo = (r1 - mid).astype(BF16).astype(F32)
    packed = hi + pltpu.roll(mid, GATE_GROUP, 1) + pltpu.roll(lo, 2 * GATE_GROUP, 1)
    return packed.astype(BF16)


def _unpack_gate(tail, e):
    lane = lax.broadcasted_iota(I32, (1, LANES), 1)
    mine = ((lane & (GATE_GROUP - 1)) == e) & (lane < GATE_PIECES * GATE_GROUP)
    return jnp.sum(jnp.where(mine, tail.astype(F32), 0.0), axis=1, keepdims=True)


def _pack_row_words(h, gate_tile):
    rows, d_model = h.shape
    half = d_model // 2
    hi = jnp.concatenate([h[:, 0:half], gate_tile], axis=1).astype(F32)
    lo = jnp.concatenate([h[:, half:], jnp.zeros((rows, LANES), BF16)], axis=1).astype(F32)
    return pltpu.bitcast(hi, U32) | (pltpu.bitcast(lo, U32) >> 16)


def _unpack_row_words(words):
    hi = pltpu.bitcast(words & jnp.uint32(0xFFFF0000), F32)
    lo = pltpu.bitcast(words << 16, F32)
    return hi, lo


def _route_kernel(afft_ref, posm_ref, thr_ref, need_ref, starts_ref, *, cap):
    n_experts, seq = afft_ref.shape[1], afft_ref.shape[2]
    key = pltpu.bitcast(afft_ref[0], I32)
    thr = jnp.zeros((n_experts, 1), I32)
    for bit in range(F32_KEY_BITS - 1, -1, -1):
        cand = thr | (1 << bit)
        cnt = jnp.sum(jnp.where(key >= cand, 1.0, 0.0), axis=1, keepdims=True)
        thr = jnp.where(cnt >= cap, cand, thr)
    n_gt = jnp.sum(jnp.where(key > thr, 1.0, 0.0), axis=1, keepdims=True)
    need = cap - n_gt
    thr_ref[0] = thr
    need_ref[0] = need
    posm, before = _select_slots_and_counts(key, thr, need, _prefix_lanes)
    posm_ref[0] = posm.astype(I32)
    starts_ref[0] = jnp.concatenate(
        [before[:, r:r + 1] for r in range(0, seq, COMBINE_ROWS)], axis=1).astype(I32)


def _route(afft, cap):
    b, n_experts, seq = afft.shape
    n_tiles = seq // COMBINE_ROWS
    rows = b * n_experts
    outs = pl.pallas_call(
        functools.partial(_route_kernel, cap=cap),
        out_shape=(jax.ShapeDtypeStruct((1, rows, seq), I32),
                   jax.ShapeDtypeStruct((1, rows, 1), I32),
                   jax.ShapeDtypeStruct((1, rows, 1), F32),
                   jax.ShapeDtypeStruct((1, rows, n_tiles), I32)),
        grid=(1,),
        in_specs=[pl.BlockSpec((1, rows, seq), lambda i: (0, 0, 0))],
        out_specs=(pl.BlockSpec((1, rows, seq), lambda i: (0, 0, 0)),
                   pl.BlockSpec((1, rows, 1), lambda i: (0, 0, 0)),
                   pl.BlockSpec((1, rows, 1), lambda i: (0, 0, 0)),
                   pl.BlockSpec((1, rows, n_tiles), lambda i: (0, 0, 0))),
        compiler_params=_params(1),
        name="expert_route",
    )(afft.reshape(1, rows, seq))
    return tuple(o.reshape(b, n_experts, o.shape[2]) for o in outs)


SC_LANES = 16
SC_GATHER_ROWS = 64
SC_BACKGROUND_SHARE = 2


def _sc_expert_gather(posm2d, h_words, n_experts, e_offset, n_e, b, seq, cap, worker_share=1):
    width = h_words.shape[1]
    info = plsc.get_sparse_core_info()
    n_cores, n_subcores = info.num_cores, info.num_subcores
    n_workers = n_cores * n_subcores // worker_share
    assert info.num_lanes == SC_LANES and (b * n_e) % n_workers == 0
    pairs_per_worker = (b * n_e) // n_workers
    mesh = plsc.VectorSubcoreMesh(core_axis_name="c", subcore_axis_name="s")

    @functools.partial(
        pl.kernel, mesh=mesh,
        out_type=jax.ShapeDtypeStruct((n_e * b * cap, width), U32),
        compiler_params=pltpu.CompilerParams(needs_layout_passes=False),
        scratch_types=[
            pltpu.VMEM((seq,), I32),
            pltpu.VMEM((cap,), I32),
            pltpu.VMEM((SC_GATHER_ROWS, width), U32),
            pltpu.SemaphoreType.DMA,
        ],
        name="sc_expert_gather",
    )
    def gather(posm_hbm, h_hbm, out_hbm, pos_v, idx_v, rows_v, sem):
        wid = lax.axis_index("s") * n_cores + lax.axis_index("c")

        def move_pair(pair):
            bi = pair // n_e
            e = pair - bi * n_e
            pltpu.sync_copy(posm_hbm.at[bi * n_experts + e_offset + e], pos_v)

            @pl.loop(0, seq, step=SC_LANES)
            def _(t0):
                slots = pos_v[pl.ds(t0, SC_LANES)]
                rows = lax.iota(I32, SC_LANES) + (t0 + bi * seq)
                plsc.store_scatter(idx_v, [slots], rows, mask=slots >= 0)

            out_base = (e * b + bi) * cap
            for c in range(cap // SC_GATHER_ROWS):
                chunk = idx_v.at[pl.ds(c * SC_GATHER_ROWS, SC_GATHER_ROWS)]
                pltpu.async_copy(h_hbm.at[chunk], rows_v, sem).wait()
                pltpu.sync_copy(rows_v, out_hbm.at[pl.ds(out_base + c * SC_GATHER_ROWS, SC_GATHER_ROWS)])

        @pl.when(wid < n_workers)
        def _():
            for p in range(pairs_per_worker):
                move_pair(wid * pairs_per_worker + p)

    return gather(posm2d, h_words).reshape(n_e, b, cap, width)


EXPERT_ROWS = 1024
EXPERT_FTILE = 256
EXPERT_WBUFS = 2


def _expert_kernel(xg_ref, wg_hbm, wu_hbm, wd_hbm, o_ref,
                   x_ref, g_ref, hact_ref, wg_full, wu_full, wd_full, wg_buf, wu_buf, wd_buf, sem,
                   *, layer, e_offset, n_ftiles):
    e = pl.program_id(0)
    m = pl.program_id(1)
    n_e = pl.num_programs(0)
    nb, cap, width = xg_ref.shape[1], xg_ref.shape[2], xg_ref.shape[3]
    half = width - LANES
    d_model = 2 * half
    rows = nb * cap
    tf = EXPERT_FTILE
    assert n_ftiles % EXPERT_WBUFS == 0

    def tile_copies(expert, f):
        ge = e_offset + expert
        slot = f % EXPERT_WBUFS
        return (
            pltpu.make_async_copy(wg_hbm.at[layer, ge, :, pl.ds(f * tf, tf)], wg_buf.at[slot], sem.at[0, slot]),
            pltpu.make_async_copy(wu_hbm.at[layer, ge, :, pl.ds(f * tf, tf)], wu_buf.at[slot], sem.at[1, slot]),
            pltpu.make_async_copy(wd_hbm.at[layer, ge, pl.ds(f * tf, tf), :], wd_buf.at[slot], sem.at[2, slot]),
        )

    def start(expert, f):
        for cp in tile_copies(expert, f):
            cp.start()

    @pl.when((e == 0) & (m == 0))
    def _():
        for f in range(EXPERT_WBUFS):
            start(e, f)

    for i in range(nb):
        r0 = i * cap
        hi, lo = _unpack_row_words(xg_ref[0, i])
        x_ref[r0:r0 + cap, 0:half] = hi[:, 0:half].astype(BF16)
        x_ref[r0:r0 + cap, half:d_model] = lo[:, 0:half].astype(BF16)
        g_ref[r0:r0 + cap, :] = jnp.broadcast_to(
            _unpack_gate(hi[:, half:width], e_offset + e), (cap, LANES))

    def receive(f):
        slot = f % EXPERT_WBUFS
        for cp in tile_copies(e, f):
            cp.wait()
        wg_full[:, f * tf:(f + 1) * tf] = wg_buf[slot].astype(BF16)
        wu_full[:, f * tf:(f + 1) * tf] = wu_buf[slot].astype(BF16)
        wd_full[f * tf:(f + 1) * tf, :] = wd_buf[slot].astype(BF16)
        ahead = f + EXPERT_WBUFS
        if ahead < n_ftiles:
            start(e, ahead)
        else:
            @pl.when(e + 1 < n_e)
            def _():
                start(e + 1, ahead - n_ftiles)

    def body(first_group):
        x = x_ref[...]
        for f in range(n_ftiles):
            if first_group:
                receive(f)
            a = _dot(x, wg_full[:, f * tf:(f + 1) * tf])
            u = _dot(x, wu_full[:, f * tf:(f + 1) * tf])
            hact_ref[:, f * tf:(f + 1) * tf] = (a * jax.nn.sigmoid(a) * u).astype(BF16)
        for c in range(0, d_model, COL_CHUNK):
            y = _dot(hact_ref[...], wd_full[:, c:c + COL_CHUNK])
            for j in range(0, COL_CHUNK, LANES):
                o_ref[:, 0, :, c + j:c + j + LANES] = (
                    (y[:, j:j + LANES] * g_ref[...]).astype(BF16).reshape(nb, cap, LANES))

    @pl.when(m == 0)
    def _():
        body(True)

    @pl.when(m != 0)
    def _():
        body(False)


def _experts(xg, w_gate, w_up, w_down, layer, e_offset):
    n_e, b, cap, width = xg.shape
    d_model = 2 * (width - LANES)
    d_expert = w_gate.shape[3]
    nb = EXPERT_ROWS // cap
    n_ftiles = d_expert // EXPERT_FTILE
    return pl.pallas_call(
        functools.partial(_expert_kernel, layer=layer, e_offset=e_offset, n_ftiles=n_ftiles),
        out_shape=jax.ShapeDtypeStruct((b, n_e, cap, d_model), BF16),
        grid=(n_e, b // nb),
        in_specs=[
            pl.BlockSpec((1, nb, cap, width), lambda e, m: (e, m, 0, 0)),
            pl.BlockSpec(memory_space=pl.ANY),
            pl.BlockSpec(memory_space=pl.ANY),
            pl.BlockSpec(memory_space=pl.ANY),
        ],
        out_specs=pl.BlockSpec((nb, 1, cap, d_model), lambda e, m: (m, e, 0, 0)),
        scratch_shapes=[pltpu.VMEM((EXPERT_ROWS, d_model), BF16),
                        pltpu.VMEM((EXPERT_ROWS, LANES), F32),
                        pltpu.VMEM((EXPERT_ROWS, d_expert), BF16),
                        pltpu.VMEM((d_model, d_expert), BF16),
                        pltpu.VMEM((d_model, d_expert), BF16),
                        pltpu.VMEM((d_expert, d_model), BF16),
                        pltpu.VMEM((EXPERT_WBUFS, d_model, EXPERT_FTILE), F32),
                        pltpu.VMEM((EXPERT_WBUFS, d_model, EXPERT_FTILE), F32),
                        pltpu.VMEM((EXPERT_WBUFS, EXPERT_FTILE, d_model), F32),
                        pltpu.SemaphoreType.DMA((3, EXPERT_WBUFS))],
        compiler_params=_params(2),
        name="experts",
    )(xg, w_gate, w_up, w_down)


COMBINE_ROWS = 256
COMBINE_WINDOW = 64
MXU_DEPTH = 256
BF16_ROWS = 16


def _combine_kernel(starts_ref, aff_ref, thr_ref, need_ref, x1_ref, *rest, cap, n_experts, final_norm):
    n_y = len(rest) - (6 if final_norm else 5)
    y_hbm = rest[:n_y]
    g_ref = rest[n_y] if final_norm else None
    o_ref, post_ref, pfull_ref, ybuf, sem = rest[-5:]
    b = pl.program_id(0)
    t = pl.program_id(1)
    n_b = pl.num_programs(0)
    n_t = pl.num_programs(1)
    rows, win = COMBINE_ROWS, COMBINE_WINDOW
    experts_per_group = n_experts // n_y
    group_rows = ybuf.shape[0] // (2 * n_y)

    def ybuf_row(buf_slot, g):
        return pl.multiple_of((buf_slot * n_y + g) * group_rows, BF16_ROWS)

    n_chunks = sem.shape[1]
    chunks_per_group = n_chunks // n_y
    chunk_rows = group_rows // chunks_per_group
    slot = b % 2

    def chunk_copy(seq_idx, c, dst_slot):
        g, r = c // chunks_per_group, (c % chunks_per_group) * chunk_rows
        return pltpu.make_async_copy(y_hbm[g].at[seq_idx, pl.ds(r, chunk_rows), :],
                                     ybuf.at[pl.ds(ybuf_row(dst_slot, g) + r, chunk_rows), :],
                                     sem.at[dst_slot, c])

    @pl.when((b == 0) & (t == 0))
    def _():
        for c in range(n_chunks):
            chunk_copy(0, c, 0).start()

    @pl.when(t == 0)
    def _():
        for c in range(n_chunks):
            chunk_copy(b, c, slot).wait()
        key = pltpu.bitcast(aff_ref[...], I32)
        post_ref[...] = _select_slots(key, thr_ref[0], need_ref[0], _prefix_rows)

    for c in range(n_chunks):
        @pl.when((t == c) & (b + 1 < n_b))
        def _():
            chunk_copy(b + 1, c, 1 - slot).start()

    posm = post_ref[pl.ds(pl.multiple_of(t * rows, rows), rows), :]
    base = (b * (n_t + 1) + t) * n_experts
    wstart, ok = [], None
    for e in range(n_experts):
        first = starts_ref[base + e]
        end = starts_ref[base + n_experts + e]
        w0 = jnp.minimum((first // BF16_ROWS) * BF16_ROWS, cap - win)
        fits = end - w0 <= win
        wstart.append(w0)
        ok = fits if ok is None else jnp.logical_and(ok, fits)

    def finish(acc):
        if final_norm:
            acc = _rmsnorm_rows(acc, g_ref[...])
        o_ref[0] = acc

    @pl.when(ok)
    def _():
        per_dot, per_tile = MXU_DEPTH // win, LANES // win
        lane = lax.broadcasted_iota(I32, (1, LANES), 1)
        lane_f = lane.astype(F32)
        acc = x1_ref[0]
        for e0 in range(0, n_experts, per_dot):
            onehots, windows = [], []
            for e1 in range(e0, e0 + per_dot, per_tile):
                rel = None
                for k in range(per_tile - 1, -1, -1):
                    e = e1 + k
                    shifted = posm[:, e:e + 1] - (wstart[e] - k * win).astype(F32)
                    rel = shifted if rel is None else jnp.where(lane < (k + 1) * win, shifted, rel)
                onehots.append(jnp.where(rel == lane_f, 1.0, 0.0).astype(BF16))
            for e in range(e0, e0 + per_dot):
                g, el = divmod(e, experts_per_group)
                r0 = pl.multiple_of(ybuf_row(slot, g) + el * cap + wstart[e], BF16_ROWS)
                windows.append(ybuf[pl.ds(r0, win), :])
            acc = acc + _dot(jnp.concatenate(onehots, axis=1), jnp.concatenate(windows, axis=0))
        finish(acc)

    @pl.when(jnp.logical_not(ok))
    def _():
        slot_ids = lax.broadcasted_iota(I32, (1, cap), 1).astype(F32)
        for e in range(n_experts):
            pfull_ref[:, e * cap:(e + 1) * cap] = jnp.where(
                posm[:, e:e + 1] == slot_ids, 1.0, 0.0).astype(BF16)
        acc = x1_ref[0]
        for g in range(n_y):
            acc = acc + _dot(pfull_ref[:, g * group_rows:(g + 1) * group_rows],
                             ybuf[pl.ds(ybuf_row(slot, g), group_rows), :])
        finish(acc)


def _combine(starts, aff2d, thr_row, need_row, x1_3d, y_groups, cap, n_experts, final_g=None):
    b, seq, d_model = x1_3d.shape
    assert sum(y.shape[1] for y in y_groups) == n_experts * cap and cap >= COMBINE_WINDOW
    final_norm = final_g is not None
    n_tiles = seq // COMBINE_ROWS
    group_rows = y_groups[0].shape[1]
    assert all(y.shape[1] == group_rows for y in y_groups)
    assert n_tiles % len(y_groups) == 0 and group_rows % (n_tiles // len(y_groups)) == 0
    y_specs = [pl.BlockSpec(memory_space=pl.ANY) for _ in y_groups]
    g_specs = [pl.BlockSpec((1, d_model), lambda i, t, s: (0, 0))] if final_norm else []
    g_args = [final_g] if final_norm else []
    return pl.pallas_call(
        functools.partial(_combine_kernel, cap=cap, n_experts=n_experts, final_norm=final_norm),
        out_shape=jax.ShapeDtypeStruct((b, seq, d_model), F32),
        grid_spec=pltpu.PrefetchScalarGridSpec(
            num_scalar_prefetch=1,
            grid=(b, n_tiles),
            in_specs=[
                pl.BlockSpec((seq, LANES), lambda i, t, s: (i, 0)),
                pl.BlockSpec((1, 1, LANES), lambda i, t, s: (i, 0, 0)),
                pl.BlockSpec((1, 1, LANES), lambda i, t, s: (i, 0, 0)),
                pl.BlockSpec((1, COMBINE_ROWS, d_model), lambda i, t, s: (i, t, 0)),
            ] + y_specs + g_specs,
            out_specs=pl.BlockSpec((1, COMBINE_ROWS, d_model), lambda i, t, s: (i, t, 0)),
            scratch_shapes=[pltpu.VMEM((seq, LANES), F32),
                            pltpu.VMEM((COMBINE_ROWS, n_experts * cap), BF16),
                            pltpu.VMEM((2 * len(y_groups) * group_rows, d_model), BF16),
                            pltpu.SemaphoreType.DMA((2, n_tiles))],
        ),
        compiler_params=_params(2),
        name="combine",
    )(starts, aff2d, thr_row, need_row, x1_3d, *y_groups, *g_args)


EXPERT_GROUPS = 2


def _moe(afft, aff2d, h_words, x1_2d, w_gate, w_up, w_down, layer, b, seq, final_g=None):
    n_experts = afft.shape[1]
    assert n_experts <= GATE_GROUP and n_experts % EXPERT_GROUPS == 0
    d_model = x1_2d.shape[1]
    cap = CAPACITY_FACTOR * seq // n_experts
    n_e = n_experts // EXPERT_GROUPS
    posm, thr, need, tile_starts = _route(afft, cap)
    starts = jnp.concatenate([jnp.swapaxes(tile_starts, 1, 2),
                              jnp.full((b, 1, n_experts), cap, I32)], axis=1).reshape(-1)
    posm2d = posm.reshape(b * n_experts, seq)
    xgs = [_sc_expert_gather(posm2d, h_words, n_experts, g * n_e, n_e, b, seq, cap,
                             worker_share=1 if g == EXPERT_GROUPS - 1 else SC_BACKGROUND_SHARE)
           for g in range(EXPERT_GROUPS)]
    ys = [_experts(xg, w_gate, w_up, w_down, layer, g * n_e).reshape(b, n_e * cap, d_model)
          for g, xg in enumerate(xgs)]
    pad = LANES - n_experts
    thr_row = jnp.pad(thr.reshape(b, 1, n_experts), ((0, 0), (0, 0), (0, pad)),
                      constant_values=np.iinfo(np.int32).max)
    need_row = jnp.pad(need.reshape(b, 1, n_experts), ((0, 0), (0, 0), (0, pad)))
    return _combine(starts, aff2d, thr_row, need_row, x1_2d.reshape(b, seq, d_model), ys, cap,
                    n_experts, final_g)


def _rotary_tile(t, cos, sin_lo, sin_hi):
    half = ROT_DIM // 2
    return t * cos + pltpu.roll(t, LANES - half, 1) * sin_lo + pltpu.roll(t, half, 1) * sin_hi


def _inproj_attn_kernel(x_ref, g_ref, w_ref, pos_ref, rot_ref,
                        q_ref, k_ref, v_ref, qm_ref, *, tok_width, kv_width):
    qscale = HEAD_DIM ** -0.5
    rows = x_ref.shape[0] // INPROJ_PARTS

    def prepare(part):
        r = slice(part * rows, (part + 1) * rows)
        hn = _rmsnorm_rows(x_ref[r, :], g_ref[...]).astype(BF16)
        ang = pos_ref[r, :].astype(F32) * rot_ref[0:1, :]
        cos = jnp.cos(ang)
        sin = jnp.sin(ang)
        return hn, cos, sin * rot_ref[1:2, :], sin * rot_ref[2:3, :]

    def project(part, hn, cos, sin_lo, sin_hi):
        r = slice(part * rows, (part + 1) * rows)
        for c in range(0, tok_width, COL_CHUNK):
            pc = _dot(hn, w_ref[:, c:c + COL_CHUNK])
            for j in range(0, COL_CHUNK, LANES):
                rot = _rotary_tile(pc[:, j:j + LANES], cos, sin_lo, sin_hi)
                q_ref[r, c + j:c + j + LANES] = (rot * qscale).astype(BF16)
        kv = _dot(hn, w_ref[:, tok_width:tok_width + 2 * kv_width])
        k01 = _rotary_tile(kv[:, 0:LANES], cos, sin_lo, sin_hi)
        k2x = _rotary_tile(kv[:, LANES:2 * LANES], cos, sin_lo, sin_hi)
        k_ref[0, 0, r, :] = k01[:, 0:HEAD_DIM].astype(BF16)
        k_ref[0, 1, r, :] = k01[:, HEAD_DIM:LANES].astype(BF16)
        k_ref[0, 2, r, :] = k2x[:, 0:HEAD_DIM].astype(BF16)
        for hh in range(kv_width // HEAD_DIM):
            lo = kv_width + hh * HEAD_DIM
            v_ref[0, hh, r, :] = kv[:, lo:lo + HEAD_DIM].astype(BF16)
        qm_ref[r, :] = _dot(hn, w_ref[:, tok_width + 2 * kv_width:]).astype(BF16)

    prepared = [prepare(part) for part in range(INPROJ_PARTS)]
    for part in range(INPROJ_PARTS):
        project(part, *prepared[part])


def _inproj_attn(x2d, g, w_bf16, pos2d, rot_rows, tok_width, kv_width, seq):
    t, d_model = x2d.shape
    n = w_bf16.shape[1]
    n_kv = kv_width // HEAD_DIM
    assert n_kv == 3 and kv_width + HEAD_DIM == 2 * LANES
    tiles_per_seq = seq // INPROJ_ROWS
    kv_spec = pl.BlockSpec((1, n_kv, INPROJ_ROWS, HEAD_DIM),
                           lambda i: (i // tiles_per_seq, 0, i % tiles_per_seq, 0))
    return pl.pallas_call(
        functools.partial(_inproj_attn_kernel, tok_width=tok_width, kv_width=kv_width),
        out_shape=(jax.ShapeDtypeStruct((t, tok_width), BF16),
                   jax.ShapeDtypeStruct((t // seq, n_kv, seq, HEAD_DIM), BF16),
                   jax.ShapeDtypeStruct((t // seq, n_kv, seq, HEAD_DIM), BF16),
                   jax.ShapeDtypeStruct((t, n - tok_width - 2 * kv_width), BF16)),
        grid=(t // INPROJ_ROWS,),
        in_specs=[
            pl.BlockSpec((INPROJ_ROWS, d_model), lambda i: (i, 0)),
            pl.BlockSpec((1, d_model), lambda i: (0, 0)),
            _resident((d_model, n), lambda i: (0, 0)),
            pl.BlockSpec((INPROJ_ROWS, 1), lambda i: (i, 0)),
            pl.BlockSpec((8, LANES), lambda i: (0, 0)),
        ],
        out_specs=(pl.BlockSpec((INPROJ_ROWS, tok_width), lambda i: (i, 0)),
                   kv_spec, kv_spec,
                   pl.BlockSpec((INPROJ_ROWS, n - tok_width - 2 * kv_width), lambda i: (i, 0))),
        compiler_params=_params(1),
        name="inproj_attn",
    )(x2d, g, w_bf16, pos2d, rot_rows)


WATTN_QBLOCKS = 4


def _wattn_kernel(sink_ref, q_ref, kp_ref, kc_ref, kn_ref, vp_ref, vc_ref, vn_ref, o_ref,
                  valid_ref, kpad_ref, vpad_ref, s_ref, p_ref, inv_ref, *, seq):
    step = pl.program_id(1)
    n_kv = kc_ref.shape[1]
    pairs = GQA_RATIO // 2
    half_rows = pairs * BLOCK
    key_rows = (WATTN_QBLOCKS + 2) * BLOCK
    zeros = jnp.zeros((key_rows, HEAD_DIM), BF16)
    ones_col = jnp.where(lax.broadcasted_iota(I32, (key_rows, HEAD_DIM), 1) == 0, 1.0, 0.0).astype(BF16)
    low_half = lax.broadcasted_iota(I32, (1, LANES), 1) < HEAD_DIM
    for hk in range(n_kv):
        kw = jnp.concatenate([kp_ref[0, hk], kc_ref[0, hk], kn_ref[0, hk]], axis=0)
        vw = jnp.concatenate([vp_ref[0, hk], vc_ref[0, hk], vn_ref[0, hk]], axis=0)
        kpad_ref[2 * hk] = jnp.concatenate([kw, zeros], axis=1)
        kpad_ref[2 * hk + 1] = jnp.concatenate([zeros, kw], axis=1)
        vpad_ref[2 * hk] = jnp.concatenate([vw, ones_col], axis=1)
        vpad_ref[2 * hk + 1] = jnp.concatenate([ones_col, vw], axis=1)

    qi = lax.broadcasted_iota(I32, (BLOCK, 3 * BLOCK), 0)
    kj = lax.broadcasted_iota(I32, (BLOCK, 3 * BLOCK), 1)
    for qb in range(WATTN_QBLOCKS):
        n = step * WATTN_QBLOCKS + qb
        k0 = qb * BLOCK
        first = jnp.maximum(qi, BLOCK - n * BLOCK)
        last = jnp.minimum(qi + 2 * WINDOW, seq + BLOCK - 1 - n * BLOCK)
        valid_ref[qb] = jnp.where(((kj - first) | (last - kj)) >= 0, 1.0, 0.0)
        q0 = qb * BLOCK
        for hk in range(n_kv):
            tile0 = hk * pairs
            qs = jnp.concatenate(
                [q_ref[0, q0:q0 + BLOCK, (tile0 + j) * LANES:(tile0 + j + 1) * LANES]
                 for j in range(pairs)], axis=0)
            s_ref[qb, hk, 0:half_rows, :] = _dot_nt(qs, kpad_ref[2 * hk, k0:k0 + 3 * BLOCK, :])
            s_ref[qb, hk, half_rows:2 * half_rows, :] = _dot_nt(
                qs, kpad_ref[2 * hk + 1, k0:k0 + 3 * BLOCK, :])
        for hk in range(n_kv):
            for c in range(GQA_RATIO):
                j, odd = c % pairs, c // pairs
                r = c * BLOCK
                s = jnp.concatenate([
                    jnp.where(valid_ref[qb, :, 0:BLOCK] > 0.5, s_ref[qb, hk, r:r + BLOCK, 0:BLOCK], NEG_INF),
                    s_ref[qb, hk, r:r + BLOCK, BLOCK:2 * BLOCK],
                    jnp.where(valid_ref[qb, :, 2 * BLOCK:] > 0.5,
                              s_ref[qb, hk, r:r + BLOCK, 2 * BLOCK:], NEG_INF),
                ], axis=1)
                sk = sink_ref[hk * GQA_RATIO + 2 * j + odd]
                m = jnp.maximum(jnp.max(s, axis=-1, keepdims=True), sk)
                p_ref[qb, hk, r:r + BLOCK, :] = jnp.exp(s - m).astype(BF16)
                inv_ref[qb, hk, j * BLOCK:(j + 1) * BLOCK, odd * HEAD_DIM:(odd + 1) * HEAD_DIM] = (
                    jnp.broadcast_to(jnp.exp(sk - m), (BLOCK, HEAD_DIM)))
        for hk in range(n_kv):
            pv_even = _dot(p_ref[qb, hk, 0:half_rows, :], vpad_ref[2 * hk, k0:k0 + 3 * BLOCK, :])
            pv_odd = _dot(p_ref[qb, hk, half_rows:2 * half_rows, :],
                          vpad_ref[2 * hk + 1, k0:k0 + 3 * BLOCK, :])
            den = (jnp.where(low_half, pv_even[:, HEAD_DIM:HEAD_DIM + 1], pv_odd[:, 0:1])
                   + inv_ref[qb, hk])
            o = jnp.where(low_half, pv_even, pv_odd) / den
            for j in range(pairs):
                lo = (hk * pairs + j) * LANES
                o_ref[0, q0:q0 + BLOCK, lo:lo + LANES] = o[j * BLOCK:(j + 1) * BLOCK].astype(BF16)


def _window_attention(sink, q3d, k4d, v4d):
    b, seq, tok_width = q3d.shape
    n_kv = k4d.shape[1]
    nb = seq // BLOCK
    qb = WATTN_QBLOCKS
    assert nb % qb == 0
    edge_block = (1, n_kv, BLOCK, HEAD_DIM)
    prev_spec = pl.BlockSpec(edge_block, lambda i, s: (i, 0, jnp.maximum(s * qb - 1, 0), 0))
    cur_spec = pl.BlockSpec((1, n_kv, qb * BLOCK, HEAD_DIM), lambda i, s: (i, 0, s, 0))
    next_spec = pl.BlockSpec(edge_block, lambda i, s: (i, 0, jnp.minimum(s * qb + qb, nb - 1), 0))
    key_rows = (qb + 2) * BLOCK
    return pl.pallas_call(
        functools.partial(_wattn_kernel, seq=seq),
        out_shape=jax.ShapeDtypeStruct((b, seq, tok_width), BF16),
        grid=(b, nb // qb),
        in_specs=[
            pl.BlockSpec(memory_space=pltpu.SMEM),
            pl.BlockSpec((1, qb * BLOCK, tok_width), lambda i, s: (i, s, 0)),
            prev_spec, cur_spec, next_spec, prev_spec, cur_spec, next_spec,
        ],
        out_specs=pl.BlockSpec((1, qb * BLOCK, tok_width), lambda i, s: (i, s, 0)),
        scratch_shapes=[pltpu.VMEM((qb, BLOCK, 3 * BLOCK), F32),
                        pltpu.VMEM((2 * n_kv, key_rows, LANES), BF16),
                        pltpu.VMEM((2 * n_kv, key_rows, LANES), BF16),
                        pltpu.VMEM((qb, n_kv, GQA_RATIO * BLOCK, 3 * BLOCK), F32),
                        pltpu.VMEM((qb, n_kv, GQA_RATIO * BLOCK, 3 * BLOCK), BF16),
                        pltpu.VMEM((qb, n_kv, GQA_RATIO // 2 * BLOCK, LANES), F32)],
        compiler_params=_params(2),
        name="window_attention",
    )(sink, q3d, k4d, k4d, k4d, v4d, v4d, v4d)


def _rotary_rows(dtype=F32):
    half = ROT_DIM // 2
    inv_freq = ROPE_THETA ** (-jnp.arange(0, ROT_DIM, 2, dtype=jnp.float32) / ROT_DIM)
    lane = np.arange(LANES) % HEAD_DIM
    rotated = lane < ROT_DIM
    freq = jnp.where(jnp.asarray(rotated), inv_freq[jnp.asarray(lane % half)], 0.0)
    rows = jnp.zeros((8, LANES), dtype)
    rows = rows.at[0].set(freq)
    rows = rows.at[1].set(jnp.asarray(np.where(lane < half, -1.0, 0.0), dtype))
    rows = rows.at[2].set(jnp.asarray(np.where(rotated & (lane >= half), 1.0, 0.0), dtype))
    return rows


def kernel(x, mem, positions, norm_mix_g, norm_ffn_g, mem_norm_g, final_g, mem_w_kv,
           pool_w_in, pool_group_w, pool_scale, pool_w_out,
           attn_w_in, attn_sink, attn_w_out,
           router_w, exp_w_gate, exp_w_up, exp_w_down):
    b, seq, d_model = x.shape
    depth = norm_mix_g.shape[0]
    t = b * seq
    n_experts = router_w.shape[2]
    tok_width = pool_scale.shape[1]
    n_groups = pool_group_w.shape[1]
    kv_width = (attn_w_in.shape[2] - tok_width - XA_WIDTH) // 2
    assert seq % ROW_TILE == 0 and seq % INPROJ_ROWS == 0 and mem.shape[1] == MEM_LEN

    memkv = _memkv(mem.reshape(b * MEM_LEN, d_model), mem_norm_g.reshape(1, d_model),
                   mem_w_kv.astype(BF16))
    memkv = memkv.reshape(depth, b, MEM_LEN, 2 * XA_WIDTH)
    rw_pad = jnp.pad(router_w, ((0, 0), (0, 0), (0, LANES - n_experts)))
    pos2d = positions.reshape(t, 1)
    rot_rows = _rotary_rows()

    x2d = x.reshape(t, d_model)
    for layer in range(depth):
        j = layer // 2
        g_mix = norm_mix_g[layer].reshape(1, d_model)
        if layer % 2 == 0:
            u, qm = _inproj_pool(x2d, g_mix, pool_w_in[j].astype(BF16), tok_width)
            tok = _pool_mixer(u.reshape(b, seq, tok_width), pool_group_w[j].astype(BF16),
                              pool_scale[j].reshape(n_groups, 1, tok_width // n_groups))
            tok = tok.reshape(t, tok_width)
            w_out = pool_w_out[j]
        else:
            q, k, v, qm = _inproj_attn(x2d, g_mix, attn_w_in[j].astype(BF16), pos2d, rot_rows,
                                       tok_width, kv_width, seq)
            tok = _window_attention(attn_sink[j], q.reshape(b, seq, tok_width), k, v)
            tok = tok.reshape(t, tok_width)
            w_out = attn_w_out[j]
        mo = _mem_xattn(qm, memkv, layer, seq)
        x1, h, afft, aff = _outproj(tok, mo, x2d, w_out.astype(BF16),
                                    norm_ffn_g[layer].reshape(1, d_model), rw_pad, layer,
                                    n_experts, seq)
        last = layer == depth - 1
        x2 = _moe(afft, aff, h, x1, exp_w_gate, exp_w_up, exp_w_down, layer, b, seq,
                  final_g.reshape(1, d_model) if last else None)
        x2d = x2.reshape(t, d_model)
    return x2d.reshape(b, seq, d_model)
```

```python
import functools

import jax
import jax.numpy as jnp
import numpy as np
from jax import lax
from jax.experimental import pallas as pl
from jax.experimental.pallas import tpu as pltpu
from jax.experimental.pallas import tpu_sc as plsc

F32 = jnp.float32
BF16 = jnp.bfloat16
I32 = jnp.int32
U32 = jnp.uint32

EPS = 1e-6
MEM_LEN = 256
XA_HEADS = 4
XA_HEAD_DIM = 128
XA_WIDTH = XA_HEADS * XA_HEAD_DIM
POOL_WINDOWS = (2, 4, 8, 16)
HEAD_DIM = 64
GQA_RATIO = 8
WINDOW = 128
BLOCK = 128
ROPE_THETA = 500000.0
ROT_DIM = 16
NEG_INF = -1e30
CAPACITY_FACTOR = 2

LANES = 128
SUBLANES = 8
MIB = 1024 * 1024
VMEM_LIMIT_BYTES = 56 * MIB

ROW_TILE = 512
INPROJ_ROWS = 1024
INPROJ_PARTS = 4
POOL_PAD = 16
PREFIX_CHUNK = 256
COL_CHUNK = 512
F32_KEY_BITS = 31


def _params(n_grid_dims):
    return pltpu.CompilerParams(
        dimension_semantics=("arbitrary",) * n_grid_dims,
        vmem_limit_bytes=VMEM_LIMIT_BYTES,
    )


def _resident(block_shape, index_map):
    return pl.BlockSpec(block_shape, index_map, pipeline_mode=pl.Buffered(1))


def _rmsnorm_rows(x, g):
    return x * lax.rsqrt(jnp.mean(x * x, axis=-1, keepdims=True) + EPS) * g


def _dot(a, b):
    return jnp.dot(a, b, preferred_element_type=F32)


def _dot_nt(a, b):
    return lax.dot_general(a, b, (((1,), (1,)), ((), ())), preferred_element_type=F32)


def _memkv_kernel(mem_ref, g_ref, w_ref, o_ref):
    hn = _rmsnorm_rows(mem_ref[...], g_ref[...]).astype(BF16)
    o_ref[0] = _dot(hn, w_ref[0]).astype(BF16)


def _memkv(mem2d, g, w_bf16):
    depth, d_model, n = w_bf16.shape
    rows = mem2d.shape[0]
    return pl.pallas_call(
        _memkv_kernel,
        out_shape=jax.ShapeDtypeStruct((depth, rows, n), BF16),
        grid=(depth, rows // ROW_TILE),
        in_specs=[
            pl.BlockSpec((ROW_TILE, d_model), lambda l, i: (i, 0)),
            pl.BlockSpec((1, d_model), lambda l, i: (0, 0)),
            pl.BlockSpec((1, d_model, n), lambda l, i: (l, 0, 0)),
        ],
        out_specs=pl.BlockSpec((1, ROW_TILE, n), lambda l, i: (l, i, 0)),
        compiler_params=_params(2),
        name="memkv",
    )(mem2d, g, w_bf16)


def _inproj_pool_kernel(x_ref, g_ref, w_ref, u_ref, qm_ref, *, tok_width):
    hn = _rmsnorm_rows(x_ref[...], g_ref[...]).astype(BF16)
    for c in range(0, tok_width, COL_CHUNK):
        u_ref[:, c:c + COL_CHUNK] = _dot(hn, w_ref[:, c:c + COL_CHUNK])
    qm_ref[...] = _dot(hn, w_ref[:, tok_width:]).astype(BF16)


def _inproj_pool(x2d, g, w_bf16, tok_width):
    t, d_model = x2d.shape
    n = w_bf16.shape[1]
    return pl.pallas_call(
        functools.partial(_inproj_pool_kernel, tok_width=tok_width),
        out_shape=(jax.ShapeDtypeStruct((t, tok_width), F32),
                   jax.ShapeDtypeStruct((t, n - tok_width), BF16)),
        grid=(t // INPROJ_ROWS,),
        in_specs=[
            pl.BlockSpec((INPROJ_ROWS, d_model), lambda i: (i, 0)),
            pl.BlockSpec((1, d_model), lambda i: (0, 0)),
            _resident((d_model, n), lambda i: (0, 0)),
        ],
        out_specs=(pl.BlockSpec((INPROJ_ROWS, tok_width), lambda i: (i, 0)),
                   pl.BlockSpec((INPROJ_ROWS, n - tok_width), lambda i: (i, 0))),
        compiler_params=_params(1),
        name="inproj_pool",
    )(x2d, g, w_bf16)


def _pool_group(u_ref, gw_ref, sc_ref, o_ref, *, window, seq):
    gwid = u_ref.shape[2]
    rows = seq + 2 * POOL_PAD
    half = window // 2
    assert 2 * half <= POOL_PAD
    zeros_pad = jnp.zeros((POOL_PAD, gwid), F32)
    u = u_ref[0]
    p = jnp.concatenate([zeros_pad, u, zeros_pad], axis=0)
    k = 1
    while k < half:
        p = p + pltpu.roll(p, rows - k, 0)
        k *= 2
    before = p if half % SUBLANES == 0 else pltpu.roll(p, half, 0)
    shift = half if half % SUBLANES == 0 else 0
    win = before[POOL_PAD - shift:POOL_PAD - shift + seq, :] + p[POOL_PAD:POOL_PAD + seq, :]
    t = lax.broadcasted_iota(I32, (seq, 1), 0)
    lo = jnp.maximum(t - half, 0)
    hi = jnp.minimum(t + half - 1, seq - 1)
    cnt = (hi - lo + 1).astype(F32)
    pooled = (win / cnt - u).astype(BF16)
    o_ref[0] = (_dot(pooled, gw_ref[0]) * sc_ref[0]).astype(BF16)


def _pool_kernel(u_ref, gw_ref, sc_ref, o_ref, *, seq):
    g = pl.program_id(1)
    for k, window in enumerate(POOL_WINDOWS):
        @pl.when(g == k)
        def _():
            _pool_group(u_ref, gw_ref, sc_ref, o_ref, window=window, seq=seq)


def _pool_mixer(u3d, gw_bf16, scale3d):
    b, seq, tok_width = u3d.shape
    n_groups, gwid, _ = gw_bf16.shape
    assert n_groups == len(POOL_WINDOWS) and n_groups * gwid == tok_width
    return pl.pallas_call(
        functools.partial(_pool_kernel, seq=seq),
        out_shape=jax.ShapeDtypeStruct((b, seq, tok_width), BF16),
        grid=(b, n_groups),
        in_specs=[
            pl.BlockSpec((1, seq, gwid), lambda i, g: (i, 0, g)),
            pl.BlockSpec((1, gwid, gwid), lambda i, g: (g, 0, 0)),
            pl.BlockSpec((1, 1, gwid), lambda i, g: (g, 0, 0)),
        ],
        out_specs=pl.BlockSpec((1, seq, gwid), lambda i, g: (i, 0, g)),
        compiler_params=_params(2),
        name="pool_mixer",
    )(u3d, gw_bf16, scale3d)


def _xattn_kernel(q_ref, kv_ref, o_ref):
    scale = XA_HEAD_DIM ** -0.5
    for h in range(XA_HEADS):
        lo = h * XA_HEAD_DIM
        q = q_ref[:, lo:lo + XA_HEAD_DIM]
        k = kv_ref[0, 0, :, lo:lo + XA_HEAD_DIM]
        v = kv_ref[0, 0, :, XA_WIDTH + lo:XA_WIDTH + lo + XA_HEAD_DIM]
        s = _dot_nt(q, k) * scale
        m = jnp.max(s, axis=-1, keepdims=True)
        p = jnp.exp(s - m)
        den = jnp.sum(p, axis=-1, keepdims=True)
        o_ref[:, lo:lo + XA_HEAD_DIM] = (_dot(p.astype(BF16), v) / den).astype(BF16)


XATTN_ROWS = 2048


def _mem_xattn(qm2d, memkv, layer, seq):
    t = qm2d.shape[0]
    tiles_per_seq = seq // XATTN_ROWS
    return pl.pallas_call(
        _xattn_kernel,
        out_shape=jax.ShapeDtypeStruct((t, XA_WIDTH), BF16),
        grid=(t // XATTN_ROWS,),
        in_specs=[
            pl.BlockSpec((XATTN_ROWS, XA_WIDTH), lambda i: (i, 0)),
            pl.BlockSpec((1, 1, MEM_LEN, 2 * XA_WIDTH), lambda i: (layer, i // tiles_per_seq, 0, 0)),
        ],
        out_specs=pl.BlockSpec((XATTN_ROWS, XA_WIDTH), lambda i: (i, 0)),
        compiler_params=_params(1),
        name="mem_xattn",
    )(qm2d, memkv)


def _outproj_kernel(tok_ref, mo_ref, x_ref, w_ref, g_ref, rw_ref,
                    x1_ref, h_ref, afft_ref, aff_ref, wcat_ref, x1prev_ref,
                    *, tok_width, n_experts):
    i = pl.program_id(0)
    last = pl.num_programs(0) - 1

    @pl.when(i == 0)
    def _():
        rw = rw_ref[0]
        w_hi = rw.astype(BF16)
        wcat_ref[:, 0:LANES] = w_hi
        wcat_ref[:, LANES:2 * LANES] = (rw - w_hi.astype(F32)).astype(BF16)

    def route_previous():
        hn = _rmsnorm_rows(x1prev_ref[...], g_ref[...])
        h_prev = hn.astype(BF16)
        h_lo = (hn - h_prev.astype(F32)).astype(BF16)
        r = _dot(h_prev, wcat_ref[...]) + _dot(h_lo, wcat_ref[...])
        logits = r[:, 0:LANES] + r[:, LANES:2 * LANES]
        lt = logits.T[0:n_experts, :]
        m = jnp.max(lt, axis=0, keepdims=True)
        ex = jnp.exp(lt - m)
        afft = ex / jnp.sum(ex, axis=0, keepdims=True)
        afft_ref[0] = afft
        padded = jnp.concatenate(
            [afft, jnp.zeros((LANES - n_experts, afft.shape[1]), F32)], axis=0)
        aff = padded.T
        aff_ref[...] = aff
        h_ref[...] = _pack_row_words(h_prev, _pack_gate_lanes(aff))

    def project_current():
        y = _dot(tok_ref[...], w_ref[0:tok_width, :]) + _dot(mo_ref[...], w_ref[tok_width:, :])
        x1 = x_ref[...] + y
        x1_ref[...] = x1
        x1prev_ref[...] = x1

    @pl.when(i == 0)
    def _():
        project_current()

    @pl.when((i > 0) & (i < last))
    def _():
        route_previous()
        project_current()

    @pl.when(i == last)
    def _():
        route_previous()


def _outproj(tok2d, mo2d, x2d, w_bf16, g, rw_pad, layer, n_experts, seq):
    t, d_model = x2d.shape
    tok_width = tok2d.shape[1]
    tiles_per_seq = seq // ROW_TILE
    n_tiles = t // ROW_TILE

    def cur(i):
        return jnp.minimum(i, n_tiles - 1)

    def prev(i):
        return jnp.maximum(i - 1, 0)

    return pl.pallas_call(
        functools.partial(_outproj_kernel, tok_width=tok_width, n_experts=n_experts),
        out_shape=(jax.ShapeDtypeStruct((t, d_model), F32),
                   jax.ShapeDtypeStruct((t, d_model // 2 + LANES), U32),
                   jax.ShapeDtypeStruct((t // seq, n_experts, seq), F32),
                   jax.ShapeDtypeStruct((t, LANES), F32)),
        grid=(n_tiles + 1,),
        in_specs=[
            pl.BlockSpec((ROW_TILE, tok_width), lambda i: (cur(i), 0)),
            pl.BlockSpec((ROW_TILE, mo2d.shape[1]), lambda i: (cur(i), 0)),
            pl.BlockSpec((ROW_TILE, d_model), lambda i: (cur(i), 0)),
            _resident((d_model, d_model), lambda i: (0, 0)),
            pl.BlockSpec((1, d_model), lambda i: (0, 0)),
            _resident((1, d_model, LANES), lambda i: (layer, 0, 0)),
        ],
        out_specs=(pl.BlockSpec((ROW_TILE, d_model), lambda i: (cur(i), 0)),
                   pl.BlockSpec((ROW_TILE, d_model // 2 + LANES), lambda i: (prev(i), 0)),
                   pl.BlockSpec((1, n_experts, ROW_TILE),
                                lambda i: (prev(i) // tiles_per_seq, 0, prev(i) % tiles_per_seq)),
                   pl.BlockSpec((ROW_TILE, LANES), lambda i: (prev(i), 0))),
        scratch_shapes=[pltpu.VMEM((d_model, 2 * LANES), BF16),
                        pltpu.VMEM((ROW_TILE, d_model), F32)],
        compiler_params=_params(1),
        name="outproj_router",
    )(tok2d, mo2d, x2d, w_bf16, g, rw_pad)


def _strict_triangle(n, lower):
    r = lax.broadcasted_iota(I32, (n, n), 0)
    c = lax.broadcasted_iota(I32, (n, n), 1)
    return jnp.where((c < r) if lower else (r < c), 1.0, 0.0).astype(BF16)


def _prefix_rows(mask_f32):
    s, l = mask_f32.shape
    tri = _strict_triangle(PREFIX_CHUNK, lower=True)
    carry = jnp.zeros((1, l), F32)
    out = []
    for c in range(0, s, PREFIX_CHUNK):
        m = mask_f32[c:c + PREFIX_CHUNK, :]
        out.append(_dot(tri, m.astype(BF16)) + carry)
        carry = carry + jnp.sum(m, axis=0, keepdims=True)
    return jnp.concatenate(out, axis=0)


def _prefix_lanes(mask_f32):
    e, s = mask_f32.shape
    tri = _strict_triangle(PREFIX_CHUNK, lower=False)
    carry = jnp.zeros((e, 1), F32)
    out = []
    for c in range(0, s, PREFIX_CHUNK):
        m = mask_f32[:, c:c + PREFIX_CHUNK]
        out.append(_dot(m.astype(BF16), tri) + carry)
        carry = carry + jnp.sum(m, axis=1, keepdims=True)
    return jnp.concatenate(out, axis=1)


def _select_slots(key, thr, need, prefix_fn):
    return _select_slots_and_counts(key, thr, need, prefix_fn)[0]


def _select_slots_and_counts(key, thr, need, prefix_fn):
    gt = jnp.where(key > thr, 1.0, 0.0)
    eq = jnp.where(key == thr, 1.0, 0.0)
    eq_rank = prefix_fn(eq)
    sel = gt + eq * jnp.where(eq_rank < need, 1.0, 0.0)
    pos = prefix_fn(sel)
    return jnp.where(sel > 0.5, pos, -1.0), pos


GATE_GROUP = 16
GATE_PIECES = 3


def _pack_gate_lanes(aff):
    hi = aff.astype(BF16).astype(F32)
    r1 = aff - hi
    mid = r1.astype(BF16).astype(F32)
    lo = (r1 - mid).astype(BF16).astype(F32)
    packed = hi + pltpu.roll(mid, GATE_GROUP, 1) + pltpu.roll(lo, 2 * GATE_GROUP, 1)
    return packed.astype(BF16)


def _unpack_gate(tail, e):
    lane = lax.broadcasted_iota(I32, (1, LANES), 1)
    mine = ((lane & (GATE_GROUP - 1)) == e) & (lane < GATE_PIECES * GATE_GROUP)
    return jnp.sum(jnp.where(mine, tail.astype(F32), 0.0), axis=1, keepdims=True)


def _pack_row_words(h, gate_tile):
    rows, d_model = h.shape
    half = d_model // 2
    hi = jnp.concatenate([h[:, 0:half], gate_tile], axis=1).astype(F32)
    lo = jnp.concatenate([h[:, half:], jnp.zeros((rows, LANES), BF16)], axis=1).astype(F32)
    return pltpu.bitcast(hi, U32) | (pltpu.bitcast(lo, U32) >> 16)


def _unpack_row_words(words):
    hi = pltpu.bitcast(words & jnp.uint32(0xFFFF0000), F32)
    lo = pltpu.bitcast(words << 16, F32)
    return hi, lo


def _route_kernel(afft_ref, posm_ref, thr_ref, need_ref, starts_ref, *, cap):
    n_experts, seq = afft_ref.shape[1], afft_ref.shape[2]
    key = pltpu.bitcast(afft_ref[0], I32)
    thr = jnp.zeros((n_experts, 1), I32)
    for bit in range(F32_KEY_BITS - 1, -1, -1):
        cand = thr | (1 << bit)
        cnt = jnp.sum(jnp.where(key >= cand, 1.0, 0.0), axis=1, keepdims=True)
        thr = jnp.where(cnt >= cap, cand, thr)
    n_gt = jnp.sum(jnp.where(key > thr, 1.0, 0.0), axis=1, keepdims=True)
    need = cap - n_gt
    thr_ref[0] = thr
    need_ref[0] = need
    posm, before = _select_slots_and_counts(key, thr, need, _prefix_lanes)
    posm_ref[0] = posm.astype(I32)
    starts_ref[0] = jnp.concatenate(
        [before[:, r:r + 1] for r in range(0, seq, COMBINE_ROWS)], axis=1).astype(I32)


def _route(afft, cap):
    b, n_experts, seq = afft.shape
    n_tiles = seq // COMBINE_ROWS
    rows = b * n_experts
    outs = pl.pallas_call(
        functools.partial(_route_kernel, cap=cap),
        out_shape=(jax.ShapeDtypeStruct((1, rows, seq), I32),
                   jax.ShapeDtypeStruct((1, rows, 1), I32),
                   jax.ShapeDtypeStruct((1, rows, 1), F32),
                   jax.ShapeDtypeStruct((1, rows, n_tiles), I32)),
        grid=(1,),
        in_specs=[pl.BlockSpec((1, rows, seq), lambda i: (0, 0, 0))],
        out_specs=(pl.BlockSpec((1, rows, seq), lambda i: (0, 0, 0)),
                   pl.BlockSpec((1, rows, 1), lambda i: (0, 0, 0)),
                   pl.BlockSpec((1, rows, 1), lambda i: (0, 0, 0)),
                   pl.BlockSpec((1, rows, n_tiles), lambda i: (0, 0, 0))),
        compiler_params=_params(1),
        name="expert_route",
    )(afft.reshape(1, rows, seq))
    return tuple(o.reshape(b, n_experts, o.shape[2]) for o in outs)


SC_LANES = 16
SC_GATHER_ROWS = 64
SC_BACKGROUND_SHARE = 2


def _sc_expert_gather(posm2d, h_words, n_experts, e_offset, n_e, b, seq, cap, worker_share=1):
    width = h_words.shape[1]
    info = plsc.get_sparse_core_info()
    n_cores, n_subcores = info.num_cores, info.num_subcores
    n_workers = n_cores * n_subcores // worker_share
    assert info.num_lanes == SC_LANES and (b * n_e) % n_workers == 0
    pairs_per_worker = (b * n_e) // n_workers
    mesh = plsc.VectorSubcoreMesh(core_axis_name="c", subcore_axis_name="s")

    @functools.partial(
        pl.kernel, mesh=mesh,
        out_type=jax.ShapeDtypeStruct((n_e * b * cap, width), U32),
        compiler_params=pltpu.CompilerParams(needs_layout_passes=False),
        scratch_types=[
            pltpu.VMEM((seq,), I32),
            pltpu.VMEM((cap,), I32),
            pltpu.VMEM((SC_GATHER_ROWS, width), U32),
            pltpu.SemaphoreType.DMA,
        ],
        name="sc_expert_gather",
    )
    def gather(posm_hbm, h_hbm, out_hbm, pos_v, idx_v, rows_v, sem):
        wid = lax.axis_index("s") * n_cores + lax.axis_index("c")

        def move_pair(pair):
            bi = pair // n_e
            e = pair - bi * n_e
            pltpu.sync_copy(posm_hbm.at[bi * n_experts + e_offset + e], pos_v)

            @pl.loop(0, seq, step=SC_LANES)
            def _(t0):
                slots = pos_v[pl.ds(t0, SC_LANES)]
                rows = lax.iota(I32, SC_LANES) + (t0 + bi * seq)
                plsc.store_scatter(idx_v, [slots], rows, mask=slots >= 0)

            out_base = (e * b + bi) * cap
            for c in range(cap // SC_GATHER_ROWS):
                chunk = idx_v.at[pl.ds(c * SC_GATHER_ROWS, SC_GATHER_ROWS)]
                pltpu.async_copy(h_hbm.at[chunk], rows_v, sem).wait()
                pltpu.sync_copy(rows_v, out_hbm.at[pl.ds(out_base + c * SC_GATHER_ROWS, SC_GATHER_ROWS)])

        @pl.when(wid < n_workers)
        def _():
            for p in range(pairs_per_worker):
                move_pair(wid * pairs_per_worker + p)

    return gather(posm2d, h_words).reshape(n_e, b, cap, width)


EXPERT_ROWS = 1024
EXPERT_FTILE = 256
EXPERT_WBUFS = 2


def _expert_kernel(xg_ref, wg_hbm, wu_hbm, wd_hbm, o_ref,
                   x_ref, g_ref, hact_ref, wg_full, wu_full, wd_full, wg_buf, wu_buf, wd_buf, sem,
                   *, layer, e_offset, n_ftiles):
    e = pl.program_id(0)
    m = pl.program_id(1)
    n_e = pl.num_programs(0)
    nb, cap, width = xg_ref.shape[1], xg_ref.shape[2], xg_ref.shape[3]
    half = width - LANES
    d_model = 2 * half
    rows = nb * cap
    tf = EXPERT_FTILE
    assert n_ftiles % EXPERT_WBUFS == 0

    def tile_copies(expert, f):
        ge = e_offset + expert
        slot = f % EXPERT_WBUFS
        return (
            pltpu.make_async_copy(wg_hbm.at[layer, ge, :, pl.ds(f * tf, tf)], wg_buf.at[slot], sem.at[0, slot]),
            pltpu.make_async_copy(wu_hbm.at[layer, ge, :, pl.ds(f * tf, tf)], wu_buf.at[slot], sem.at[1, slot]),
            pltpu.make_async_copy(wd_hbm.at[layer, ge, pl.ds(f * tf, tf), :], wd_buf.at[slot], sem.at[2, slot]),
        )

    def start(expert, f):
        for cp in tile_copies(expert, f):
            cp.start()

    @pl.when((e == 0) & (m == 0))
    def _():
        for f in range(EXPERT_WBUFS):
            start(e, f)

    for i in range(nb):
        r0 = i * cap
        hi, lo = _unpack_row_words(xg_ref[0, i])
        x_ref[r0:r0 + cap, 0:half] = hi[:, 0:half].astype(BF16)
        x_ref[r0:r0 + cap, half:d_model] = lo[:, 0:half].astype(BF16)
        g_ref[r0:r0 + cap, :] = jnp.broadcast_to(
            _unpack_gate(hi[:, half:width], e_offset + e), (cap, LANES))

    def receive(f):
        slot = f % EXPERT_WBUFS
        for cp in tile_copies(e, f):
            cp.wait()
        wg_full[:, f * tf:(f + 1) * tf] = wg_buf[slot].astype(BF16)
        wu_full[:, f * tf:(f + 1) * tf] = wu_buf[slot].astype(BF16)
        wd_full[f * tf:(f + 1) * tf, :] = wd_buf[slot].astype(BF16)
        ahead = f + EXPERT_WBUFS
        if ahead < n_ftiles:
            start(e, ahead)
        else:
            @pl.when(e + 1 < n_e)
            def _():
                start(e + 1, ahead - n_ftiles)

    def body(first_group):
        x = x_ref[...]
        for f in range(n_ftiles):
            if first_group:
                receive(f)
            a = _dot(x, wg_full[:, f * tf:(f + 1) * tf])
            u = _dot(x, wu_full[:, f * tf:(f + 1) * tf])
            hact_ref[:, f * tf:(f + 1) * tf] = (a * jax.nn.sigmoid(a) * u).astype(BF16)
        for c in range(0, d_model, COL_CHUNK):
            y = _dot(hact_ref[...], wd_full[:, c:c + COL_CHUNK])
            for j in range(0, COL_CHUNK, LANES):
                o_ref[:, 0, :, c + j:c + j + LANES] = (
                    (y[:, j:j + LANES] * g_ref[...]).astype(BF16).reshape(nb, cap, LANES))

    @pl.when(m == 0)
    def _():
        body(True)

    @pl.when(m != 0)
    def _():
        body(False)


def _experts(xg, w_gate, w_up, w_down, layer, e_offset):
    n_e, b, cap, width = xg.shape
    d_model = 2 * (width - LANES)
    d_expert = w_gate.shape[3]
    nb = EXPERT_ROWS // cap
    n_ftiles = d_expert // EXPERT_FTILE
    return pl.pallas_call(
        functools.partial(_expert_kernel, layer=layer, e_offset=e_offset, n_ftiles=n_ftiles),
        out_shape=jax.ShapeDtypeStruct((b, n_e, cap, d_model), BF16),
        grid=(n_e, b // nb),
        in_specs=[
            pl.BlockSpec((1, nb, cap, width), lambda e, m: (e, m, 0, 0)),
            pl.BlockSpec(memory_space=pl.ANY),
            pl.BlockSpec(memory_space=pl.ANY),
            pl.BlockSpec(memory_space=pl.ANY),
        ],
        out_specs=pl.BlockSpec((nb, 1, cap, d_model), lambda e, m: (m, e, 0, 0)),
        scratch_shapes=[pltpu.VMEM((EXPERT_ROWS, d_model), BF16),
                        pltpu.VMEM((EXPERT_ROWS, LANES), F32),
                        pltpu.VMEM((EXPERT_ROWS, d_expert), BF16),
                        pltpu.VMEM((d_model, d_expert), BF16),
                        pltpu.VMEM((d_model, d_expert), BF16),
                        pltpu.VMEM((d_expert, d_model), BF16),
                        pltpu.VMEM((EXPERT_WBUFS, d_model, EXPERT_FTILE), F32),
                        pltpu.VMEM((EXPERT_WBUFS, d_model, EXPERT_FTILE), F32),
                        pltpu.VMEM((EXPERT_WBUFS, EXPERT_FTILE, d_model), F32),
                        pltpu.SemaphoreType.DMA((3, EXPERT_WBUFS))],
        compiler_params=_params(2),
        name="experts",
    )(xg, w_gate, w_up, w_down)


COMBINE_ROWS = 256
COMBINE_WINDOW = 64
MXU_DEPTH = 256
BF16_ROWS = 16


def _combine_kernel(starts_ref, aff_ref, thr_ref, need_ref, x1_ref, *rest, cap, n_experts, final_norm):
    n_y = len(rest) - (6 if final_norm else 5)
    y_hbm = rest[:n_y]
    g_ref = rest[n_y] if final_norm else None
    o_ref, post_ref, pfull_ref, ybuf, sem = rest[-5:]
    b = pl.program_id(0)
    t = pl.program_id(1)
    n_b = pl.num_programs(0)
    n_t = pl.num_programs(1)
    rows, win = COMBINE_ROWS, COMBINE_WINDOW
    experts_per_group = n_experts // n_y
    group_rows = ybuf.shape[0] // (2 * n_y)

    def ybuf_row(buf_slot, g):
        return pl.multiple_of((buf_slot * n_y + g) * group_rows, BF16_ROWS)

    n_chunks = sem.shape[1]
    chunks_per_group = n_chunks // n_y
    chunk_rows = group_rows // chunks_per_group
    slot = b % 2

    def chunk_copy(seq_idx, c, dst_slot):
        g, r = c // chunks_per_group, (c % chunks_per_group) * chunk_rows
        return pltpu.make_async_copy(y_hbm[g].at[seq_idx, pl.ds(r, chunk_rows), :],
                                     ybuf.at[pl.ds(ybuf_row(dst_slot, g) + r, chunk_rows), :],
                                     sem.at[dst_slot, c])

    @pl.when((b == 0) & (t == 0))
    def _():
        for c in range(n_chunks):
            chunk_copy(0, c, 0).start()

    @pl.when(t == 0)
    def _():
        for c in range(n_chunks):
            chunk_copy(b, c, slot).wait()
        key = pltpu.bitcast(aff_ref[...], I32)
        post_ref[...] = _select_slots(key, thr_ref[0], need_ref[0], _prefix_rows)

    for c in range(n_chunks):
        @pl.when((t == c) & (b + 1 < n_b))
        def _():
            chunk_copy(b + 1, c, 1 - slot).start()

    posm = post_ref[pl.ds(pl.multiple_of(t * rows, rows), rows), :]
    base = (b * (n_t + 1) + t) * n_experts
    wstart, ok = [], None
    for e in range(n_experts):
        first = starts_ref[base + e]
        end = starts_ref[base + n_experts + e]
        w0 = jnp.minimum((first // BF16_ROWS) * BF16_ROWS, cap - win)
        fits = end - w0 <= win
        wstart.append(w0)
        ok = fits if ok is None else jnp.logical_and(ok, fits)

    def finish(acc):
        if final_norm:
            acc = _rmsnorm_rows(acc, g_ref[...])
        o_ref[0] = acc

    @pl.when(ok)
    def _():
        per_dot, per_tile = MXU_DEPTH // win, LANES // win
        lane = lax.broadcasted_iota(I32, (1, LANES), 1)
        lane_f = lane.astype(F32)
        acc = x1_ref[0]
        for e0 in range(0, n_experts, per_dot):
            onehots, windows = [], []
            for e1 in range(e0, e0 + per_dot, per_tile):
                rel = None
                for k in range(per_tile - 1, -1, -1):
                    e = e1 + k
                    shifted = posm[:, e:e + 1] - (wstart[e] - k * win).astype(F32)
                    rel = shifted if rel is None else jnp.where(lane < (k + 1) * win, shifted, rel)
                onehots.append(jnp.where(rel == lane_f, 1.0, 0.0).astype(BF16))
            for e in range(e0, e0 + per_dot):
                g, el = divmod(e, experts_per_group)
                r0 = pl.multiple_of(ybuf_row(slot, g) + el * cap + wstart[e], BF16_ROWS)
                windows.append(ybuf[pl.ds(r0, win), :])
            acc = acc + _dot(jnp.concatenate(onehots, axis=1), jnp.concatenate(windows, axis=0))
        finish(acc)

    @pl.when(jnp.logical_not(ok))
    def _():
        slot_ids = lax.broadcasted_iota(I32, (1, cap), 1).astype(F32)
        for e in range(n_experts):
            pfull_ref[:, e * cap:(e + 1) * cap] = jnp.where(
                posm[:, e:e + 1] == slot_ids, 1.0, 0.0).astype(BF16)
        acc = x1_ref[0]
        for g in range(n_y):
            acc = acc + _dot(pfull_ref[:, g * group_rows:(g + 1) * group_rows],
                             ybuf[pl.ds(ybuf_row(slot, g), group_rows), :])
        finish(acc)


def _combine(starts, aff2d, thr_row, need_row, x1_3d, y_groups, cap, n_experts, final_g=None):
    b, seq, d_model = x1_3d.shape
    assert sum(y.shape[1] for y in y_groups) == n_experts * cap and cap >= COMBINE_WINDOW
    final_norm = final_g is not None
    n_tiles = seq // COMBINE_ROWS
    group_rows = y_groups[0].shape[1]
    assert all(y.shape[1] == group_rows for y in y_groups)
    assert n_tiles % len(y_groups) == 0 and group_rows % (n_tiles // len(y_groups)) == 0
    y_specs = [pl.BlockSpec(memory_space=pl.ANY) for _ in y_groups]
    g_specs = [pl.BlockSpec((1, d_model), lambda i, t, s: (0, 0))] if final_norm else []
    g_args = [final_g] if final_norm else []
    return pl.pallas_call(
        functools.partial(_combine_kernel, cap=cap, n_experts=n_experts, final_norm=final_norm),
        out_shape=jax.ShapeDtypeStruct((b, seq, d_model), F32),
        grid_spec=pltpu.PrefetchScalarGridSpec(
            num_scalar_prefetch=1,
            grid=(b, n_tiles),
            in_specs=[
                pl.BlockSpec((seq, LANES), lambda i, t, s: (i, 0)),
                pl.BlockSpec((1, 1, LANES), lambda i, t, s: (i, 0, 0)),
                pl.BlockSpec((1, 1, LANES), lambda i, t, s: (i, 0, 0)),
                pl.BlockSpec((1, COMBINE_ROWS, d_model), lambda i, t, s: (i, t, 0)),
            ] + y_specs + g_specs,
            out_specs=pl.BlockSpec((1, COMBINE_ROWS, d_model), lambda i, t, s: (i, t, 0)),
            scratch_shapes=[pltpu.VMEM((seq, LANES), F32),
                            pltpu.VMEM((COMBINE_ROWS, n_experts * cap), BF16),
                            pltpu.VMEM((2 * len(y_groups) * group_rows, d_model), BF16),
                            pltpu.SemaphoreType.DMA((2, n_tiles))],
        ),
        compiler_params=_params(2),
        name="combine",
    )(starts, aff2d, thr_row, need_row, x1_3d, *y_groups, *g_args)


EXPERT_GROUPS = 2


def _moe(afft, aff2d, h_words, x1_2d, w_gate, w_up, w_down, layer, b, seq, final_g=None):
    n_experts = afft.shape[1]
    assert n_experts <= GATE_GROUP and n_experts % EXPERT_GROUPS == 0
    d_model = x1_2d.shape[1]
    cap = CAPACITY_FACTOR * seq // n_experts
    n_e = n_experts // EXPERT_GROUPS
    posm, thr, need, tile_starts = _route(afft, cap)
    starts = jnp.concatenate([jnp.swapaxes(tile_starts, 1, 2),
                              jnp.full((b, 1, n_experts), cap, I32)], axis=1).reshape(-1)
    posm2d = posm.reshape(b * n_experts, seq)
    xgs = [_sc_expert_gather(posm2d, h_words, n_experts, g * n_e, n_e, b, seq, cap,
                             worker_share=1 if g == EXPERT_GROUPS - 1 else SC_BACKGROUND_SHARE)
           for g in range(EXPERT_GROUPS)]
    ys = [_experts(xg, w_gate, w_up, w_down, layer, g * n_e).reshape(b, n_e * cap, d_model)
          for g, xg in enumerate(xgs)]
    pad = LANES - n_experts
    thr_row = jnp.pad(thr.reshape(b, 1, n_experts), ((0, 0), (0, 0), (0, pad)),
                      constant_values=np.iinfo(np.int32).max)
    need_row = jnp.pad(need.reshape(b, 1, n_experts), ((0, 0), (0, 0), (0, pad)))
    return _combine(starts, aff2d, thr_row, need_row, x1_2d.reshape(b, seq, d_model), ys, cap,
                    n_experts, final_g)


def _rotary_tile(t, cos, sin_lo, sin_hi):
    half = ROT_DIM // 2
    return t * cos + pltpu.roll(t, LANES - half, 1) * sin_lo + pltpu.roll(t, half, 1) * sin_hi


def _inproj_attn_kernel(x_ref, g_ref, w_ref, pos_ref, rot_ref,
                        q_ref, k_ref, v_ref, qm_ref, *, tok_width, kv_width):
    qscale = HEAD_DIM ** -0.5
    rows = x_ref.shape[0] // INPROJ_PARTS

    def prepare(part):
        r = slice(part * rows, (part + 1) * rows)
        hn = _rmsnorm_rows(x_ref[r, :], g_ref[...]).astype(BF16)
        ang = pos_ref[r, :].astype(F32) * rot_ref[0:1, :]
        cos = jnp.cos(ang)
        sin = jnp.sin(ang)
        return hn, cos, sin * rot_ref[1:2, :], sin * rot_ref[2:3, :]

    def project(part, hn, cos, sin_lo, sin_hi):
        r = slice(part * rows, (part + 1) * rows)
        for c in range(0, tok_width, COL_CHUNK):
            pc = _dot(hn, w_ref[:, c:c + COL_CHUNK])
            for j in range(0, COL_CHUNK, LANES):
                rot = _rotary_tile(pc[:, j:j + LANES], cos, sin_lo, sin_hi)
                q_ref[r, c + j:c + j + LANES] = (rot * qscale).astype(BF16)
        kv = _dot(hn, w_ref[:, tok_width:tok_width + 2 * kv_width])
        k01 = _rotary_tile(kv[:, 0:LANES], cos, sin_lo, sin_hi)
        k2x = _rotary_tile(kv[:, LANES:2 * LANES], cos, sin_lo, sin_hi)
        k_ref[0, 0, r, :] = k01[:, 0:HEAD_DIM].astype(BF16)
        k_ref[0, 1, r, :] = k01[:, HEAD_DIM:LANES].astype(BF16)
        k_ref[0, 2, r, :] = k2x[:, 0:HEAD_DIM].astype(BF16)
        for hh in range(kv_width // HEAD_DIM):
            lo = kv_width + hh * HEAD_DIM
            v_ref[0, hh, r, :] = kv[:, lo:lo + HEAD_DIM].astype(BF16)
        qm_ref[r, :] = _dot(hn, w_ref[:, tok_width + 2 * kv_width:]).astype(BF16)

    prepared = [prepare(part) for part in range(INPROJ_PARTS)]
    for part in range(INPROJ_PARTS):
        project(part, *prepared[part])


def _inproj_attn(x2d, g, w_bf16, pos2d, rot_rows, tok_width, kv_width, seq):
    t, d_model = x2d.shape
    n = w_bf16.shape[1]
    n_kv = kv_width // HEAD_DIM
    assert n_kv == 3 and kv_width + HEAD_DIM == 2 * LANES
    tiles_per_seq = seq // INPROJ_ROWS
    kv_spec = pl.BlockSpec((1, n_kv, INPROJ_ROWS, HEAD_DIM),
                           lambda i: (i // tiles_per_seq, 0, i % tiles_per_seq, 0))
    return pl.pallas_call(
        functools.partial(_inproj_attn_kernel, tok_width=tok_width, kv_width=kv_width),
        out_shape=(jax.ShapeDtypeStruct((t, tok_width), BF16),
                   jax.ShapeDtypeStruct((t // seq, n_kv, seq, HEAD_DIM), BF16),
                   jax.ShapeDtypeStruct((t // seq, n_kv, seq, HEAD_DIM), BF16),
                   jax.ShapeDtypeStruct((t, n - tok_width - 2 * kv_width), BF16)),
        grid=(t // INPROJ_ROWS,),
        in_specs=[
            pl.BlockSpec((INPROJ_ROWS, d_model), lambda i: (i, 0)),
            pl.BlockSpec((1, d_model), lambda i: (0, 0)),
            _resident((d_model, n), lambda i: (0, 0)),
            pl.BlockSpec((INPROJ_ROWS, 1), lambda i: (i, 0)),
            pl.BlockSpec((8, LANES), lambda i: (0, 0)),
        ],
        out_specs=(pl.BlockSpec((INPROJ_ROWS, tok_width), lambda i: (i, 0)),
                   kv_spec, kv_spec,
                   pl.BlockSpec((INPROJ_ROWS, n - tok_width - 2 * kv_width), lambda i: (i, 0))),
        compiler_params=_params(1),
        name="inproj_attn",
    )(x2d, g, w_bf16, pos2d, rot_rows)


WATTN_QBLOCKS = 4


def _wattn_kernel(sink_ref, q_ref, kp_ref, kc_ref, kn_ref, vp_ref, vc_ref, vn_ref, o_ref,
                  maskadd_ref, kpad_ref, vpad_ref, s_ref, p_ref, inv_ref, *, seq):
    step = pl.program_id(1)
    n_kv = kc_ref.shape[1]
    pairs = GQA_RATIO // 2
    half_rows = pairs * BLOCK
    key_rows = (WATTN_QBLOCKS + 2) * BLOCK
    zeros = jnp.zeros((key_rows, HEAD_DIM), BF16)
    ones_col = jnp.where(lax.broadcasted_iota(I32, (key_rows, HEAD_DIM), 1) == 0, 1.0, 0.0).astype(BF16)
    low_half = lax.broadcasted_iota(I32, (1, LANES), 1) < HEAD_DIM
    for hk in range(n_kv):
        kw = jnp.concatenate([kp_ref[0, hk], kc_ref[0, hk], kn_ref[0, hk]], axis=0)
        vw = jnp.concatenate([vp_ref[0, hk], vc_ref[0, hk], vn_ref[0, hk]], axis=0)
        kpad_ref[2 * hk] = jnp.concatenate([kw, zeros], axis=1)
        kpad_ref[2 * hk + 1] = jnp.concatenate([zeros, kw], axis=1)
        vpad_ref[2 * hk] = jnp.concatenate([vw, ones_col], axis=1)
        vpad_ref[2 * hk + 1] = jnp.concatenate([ones_col, vw], axis=1)

    qi = lax.broadcasted_iota(I32, (BLOCK, 3 * BLOCK), 0)
    kj = lax.broadcasted_iota(I32, (BLOCK, 3 * BLOCK), 1)
    for qb in range(WATTN_QBLOCKS):
        n = step * WATTN_QBLOCKS + qb
        k0 = qb * BLOCK
        first = jnp.maximum(qi, BLOCK - n * BLOCK)
        last = jnp.minimum(qi + 2 * WINDOW, seq + BLOCK - 1 - n * BLOCK)
        maskadd_ref[qb] = jnp.where(((kj - first) | (last - kj)) >= 0, 0.0, NEG_INF)
        q0 = qb * BLOCK
        for hk in range(n_kv):
            tile0 = hk * pairs
            qs = jnp.concatenate(
                [q_ref[0, q0:q0 + BLOCK, (tile0 + j) * LANES:(tile0 + j + 1) * LANES]
                 for j in range(pairs)], axis=0)
            s_ref[qb, hk, 0:half_rows, :] = _dot_nt(qs, kpad_ref[2 * hk, k0:k0 + 3 * BLOCK, :])
            s_ref[qb, hk, half_rows:2 * half_rows, :] = _dot_nt(
                qs, kpad_ref[2 * hk + 1, k0:k0 + 3 * BLOCK, :])
        for hk in range(n_kv):
            for c in range(GQA_RATIO):
                j, odd = c % pairs, c // pairs
                r = c * BLOCK
                s = jnp.concatenate([
                    s_ref[qb, hk, r:r + BLOCK, 0:BLOCK] + maskadd_ref[qb, :, 0:BLOCK],
                    s_ref[qb, hk, r:r + BLOCK, BLOCK:2 * BLOCK],
                    s_ref[qb, hk, r:r + BLOCK, 2 * BLOCK:] + maskadd_ref[qb, :, 2 * BLOCK:],
                ], axis=1)
                sk = sink_ref[hk * GQA_RATIO + 2 * j + odd]
                m = jnp.maximum(jnp.max(s, axis=-1, keepdims=True), sk)
                p_ref[qb, hk, r:r + BLOCK, :] = jnp.exp(s - m).astype(BF16)
                inv_ref[qb, hk, j * BLOCK:(j + 1) * BLOCK, odd * HEAD_DIM:(odd + 1) * HEAD_DIM] = (
                    jnp.broadcast_to(jnp.exp(sk - m), (BLOCK, HEAD_DIM)))
        for hk in range(n_kv):
            pv_even = _dot(p_ref[qb, hk, 0:half_rows, :], vpad_ref[2 * hk, k0:k0 + 3 * BLOCK, :])
            pv_odd = _dot(p_ref[qb, hk, half_rows:2 * half_rows, :],
                          vpad_ref[2 * hk + 1, k0:k0 + 3 * BLOCK, :])
            den = (jnp.where(low_half, pv_even[:, HEAD_DIM:HEAD_DIM + 1], pv_odd[:, 0:1])
                   + inv_ref[qb, hk])
            o = jnp.where(low_half, pv_even, pv_odd) / den
            for j in range(pairs):
                lo = (hk * pairs + j) * LANES
                o_ref[0, q0:q0 + BLOCK, lo:lo + LANES] = o[j * BLOCK:(j + 1) * BLOCK].astype(BF16)


def _window_attention(sink, q3d, k4d, v4d):
    b, seq, tok_width = q3d.shape
    n_kv = k4d.shape[1]
    nb = seq // BLOCK
    qb = WATTN_QBLOCKS
    assert nb % qb == 0
    edge_block = (1, n_kv, BLOCK, HEAD_DIM)
    prev_spec = pl.BlockSpec(edge_block, lambda i, s: (i, 0, jnp.maximum(s * qb - 1, 0), 0))
    cur_spec = pl.BlockSpec((1, n_kv, qb * BLOCK, HEAD_DIM), lambda i, s: (i, 0, s, 0))
    next_spec = pl.BlockSpec(edge_block, lambda i, s: (i, 0, jnp.minimum(s * qb + qb, nb - 1), 0))
    key_rows = (qb + 2) * BLOCK
    return pl.pallas_call(
        functools.partial(_wattn_kernel, seq=seq),
        out_shape=jax.ShapeDtypeStruct((b, seq, tok_width), BF16),
        grid=(b, nb // qb),
        in_specs=[
            pl.BlockSpec(memory_space=pltpu.SMEM),
            pl.BlockSpec((1, qb * BLOCK, tok_width), lambda i, s: (i, s, 0)),
            prev_spec, cur_spec, next_spec, prev_spec, cur_spec, next_spec,
        ],
        out_specs=pl.BlockSpec((1, qb * BLOCK, tok_width), lambda i, s: (i, s, 0)),
        scratch_shapes=[pltpu.VMEM((qb, BLOCK, 3 * BLOCK), F32),
                        pltpu.VMEM((2 * n_kv, key_rows, LANES), BF16),
                        pltpu.VMEM((2 * n_kv, key_rows, LANES), BF16),
                        pltpu.VMEM((qb, n_kv, GQA_RATIO * BLOCK, 3 * BLOCK), F32),
                        pltpu.VMEM((qb, n_kv, GQA_RATIO * BLOCK, 3 * BLOCK), BF16),
                        pltpu.VMEM((qb, n_kv, GQA_RATIO // 2 * BLOCK, LANES), F32)],
        compiler_params=_params(2),
        name="window_attention",
    )(sink, q3d, k4d, k4d, k4d, v4d, v4d, v4d)


def _rotary_rows(dtype=F32):
    half = ROT_DIM // 2
    inv_freq = ROPE_THETA ** (-jnp.arange(0, ROT_DIM, 2, dtype=jnp.float32) / ROT_DIM)
    lane = np.arange(LANES) % HEAD_DIM
    rotated = lane < ROT_DIM
    freq = jnp.where(jnp.asarray(rotated), inv_freq[jnp.asarray(lane % half)], 0.0)
    rows = jnp.zeros((8, LANES), dtype)
    rows = rows.at[0].set(freq)
    rows = rows.at[1].set(jnp.asarray(np.where(lane < half, -1.0, 0.0), dtype))
    rows = rows.at[2].set(jnp.asarray(np.where(rotated & (lane >= half), 1.0, 0.0), dtype))
    return rows


def kernel(x, mem, positions, norm_mix_g, norm_ffn_g, mem_norm_g, final_g, mem_w_kv,
           pool_w_in, pool_group_w, pool_scale, pool_w_out,
           attn_w_in, attn_sink, attn_w_out,
           router_w, exp_w_gate, exp_w_up, exp_w_down):
    b, seq, d_model = x.shape
    depth = norm_mix_g.shape[0]
    t = b * seq
    n_experts = router_w.shape[2]
    tok_width = pool_scale.shape[1]
    n_groups = pool_group_w.shape[1]
    kv_width = (attn_w_in.shape[2] - tok_width - XA_WIDTH) // 2
    assert seq % ROW_TILE == 0 and seq % INPROJ_ROWS == 0 and mem.shape[1] == MEM_LEN

    memkv = _memkv(mem.reshape(b * MEM_LEN, d_model), mem_norm_g.reshape(1, d_model),
                   mem_w_kv.astype(BF16))
    memkv = memkv.reshape(depth, b, MEM_LEN, 2 * XA_WIDTH)
    rw_pad = jnp.pad(router_w, ((0, 0), (0, 0), (0, LANES - n_experts)))
    pos2d = positions.reshape(t, 1)
    rot_rows = _rotary_rows()

    x2d = x.reshape(t, d_model)
    for layer in range(depth):
        j = layer // 2
        g_mix = norm_mix_g[layer].reshape(1, d_model)
        if layer % 2 == 0:
            u, qm = _inproj_pool(x2d, g_mix, pool_w_in[j].astype(BF16), tok_width)
            tok = _pool_mixer(u.reshape(b, seq, tok_width), pool_group_w[j].astype(BF16),
                              pool_scale[j].reshape(n_groups, 1, tok_width // n_groups))
            tok = tok.reshape(t, tok_width)
            w_out = pool_w_out[j]
        else:
            q, k, v, qm = _inproj_attn(x2d, g_mix, attn_w_in[j].astype(BF16), pos2d, rot_rows,
                                       tok_width, kv_width, seq)
            tok = _window_attention(attn_sink[j], q.reshape(b, seq, tok_width), k, v)
            tok = tok.reshape(t, tok_width)
            w_out = attn_w_out[j]
        mo = _mem_xattn(qm, memkv, layer, seq)
        x1, h, afft, aff = _outproj(tok, mo, x2d, w_out.astype(BF16),
                                    norm_ffn_g[layer].reshape(1, d_model), rw_pad, layer,
                                    n_experts, seq)
        last = layer == depth - 1
        x2 = _moe(afft, aff, h, x1, exp_w_gate, exp_w_up, exp_w_down, layer, b, seq,
                  final_g.reshape(1, d_model) if last else None)
        x2d = x2.reshape(t, d_model)
    return x2d.reshape(b, seq, d_model)
```

```python
import functools

import jax
import jax.numpy as jnp
import numpy as np
from jax import lax
from jax.experimental import pallas as pl
from jax.experimental.pallas import tpu as pltpu
from jax.experimental.pallas import tpu_sc as plsc

F32 = jnp.float32
BF16 = jnp.bfloat16
I32 = jnp.int32
U32 = jnp.uint32

EPS = 1e-6
MEM_LEN = 256
XA_HEADS = 4
XA_HEAD_DIM = 128
XA_WIDTH = XA_HEADS * XA_HEAD_DIM
POOL_WINDOWS = (2, 4, 8, 16)
HEAD_DIM = 64
GQA_RATIO = 8
WINDOW = 128
BLOCK = 128
ROPE_THETA = 500000.0
ROT_DIM = 16
NEG_INF = -1e30
CAPACITY_FACTOR = 2

LANES = 128
SUBLANES = 8
MIB = 1024 * 1024
VMEM_LIMIT_BYTES = 56 * MIB

ROW_TILE = 512
INPROJ_ROWS = 1024
INPROJ_PARTS = 4
POOL_PAD = 16
PREFIX_CHUNK = 256
COL_CHUNK = 512
F32_KEY_BITS = 31


def _params(n_grid_dims):
    return pltpu.CompilerParams(
        dimension_semantics=("arbitrary",) * n_grid_dims,
        vmem_limit_bytes=VMEM_LIMIT_BYTES,
    )


def _resident(block_shape, index_map):
    return pl.BlockSpec(block_shape, index_map, pipeline_mode=pl.Buffered(1))


def _rmsnorm_rows(x, g):
    return x * lax.rsqrt(jnp.mean(x * x, axis=-1, keepdims=True) + EPS) * g


def _dot(a, b):
    return jnp.dot(a, b, preferred_element_type=F32)


def _dot_nt(a, b):
    return lax.dot_general(a, b, (((1,), (1,)), ((), ())), preferred_element_type=F32)


def _memkv_kernel(mem_ref, g_ref, w_ref, o_ref):
    hn = _rmsnorm_rows(mem_ref[...], g_ref[...]).astype(BF16)
    o_ref[0] = _dot(hn, w_ref[0]).astype(BF16)


def _memkv(mem2d, g, w_bf16):
    depth, d_model, n = w_bf16.shape
    rows = mem2d.shape[0]
    return pl.pallas_call(
        _memkv_kernel,
        out_shape=jax.ShapeDtypeStruct((depth, rows, n), BF16),
        grid=(depth, rows // ROW_TILE),
        in_specs=[
            pl.BlockSpec((ROW_TILE, d_model), lambda l, i: (i, 0)),
            pl.BlockSpec((1, d_model), lambda l, i: (0, 0)),
            pl.BlockSpec((1, d_model, n), lambda l, i: (l, 0, 0)),
        ],
        out_specs=pl.BlockSpec((1, ROW_TILE, n), lambda l, i: (l, i, 0)),
        compiler_params=_params(2),
        name="memkv",
    )(mem2d, g, w_bf16)


def _inproj_pool_kernel(x_ref, g_ref, w_ref, u_ref, qm_ref, *, tok_width):
    hn = _rmsnorm_rows(x_ref[...], g_ref[...]).astype(BF16)
    for c in range(0, tok_width, COL_CHUNK):
        u_ref[:, c:c + COL_CHUNK] = _dot(hn, w_ref[:, c:c + COL_CHUNK])
    qm_ref[...] = _dot(hn, w_ref[:, tok_width:]).astype(BF16)


def _inproj_pool(x2d, g, w_bf16, tok_width):
    t, d_model = x2d.shape
    n = w_bf16.shape[1]
    return pl.pallas_call(
        functools.partial(_inproj_pool_kernel, tok_width=tok_width),
        out_shape=(jax.ShapeDtypeStruct((t, tok_width), F32),
                   jax.ShapeDtypeStruct((t, n - tok_width), BF16)),
        grid=(t // INPROJ_ROWS,),
        in_specs=[
            pl.BlockSpec((INPROJ_ROWS, d_model), lambda i: (i, 0)),
            pl.BlockSpec((1, d_model), lambda i: (0, 0)),
            _resident((d_model, n), lambda i: (0, 0)),
        ],
        out_specs=(pl.BlockSpec((INPROJ_ROWS, tok_width), lambda i: (i, 0)),
                   pl.BlockSpec((INPROJ_ROWS, n - tok_width), lambda i: (i, 0))),
        compiler_params=_params(1),
        name="inproj_pool",
    )(x2d, g, w_bf16)


def _pool_group(u_ref, gw_ref, sc_ref, o_ref, *, window, seq):
    gwid = u_ref.shape[2]
    rows = seq + 2 * POOL_PAD
    half = window // 2
    assert 2 * half <= POOL_PAD
    zeros_pad = jnp.zeros((POOL_PAD, gwid), F32)
    u = u_ref[0]
    p = jnp.concatenate([zeros_pad, u, zeros_pad], axis=0)
    k = 1
    while k < half:
        p = p + pltpu.roll(p, rows - k, 0)
        k *= 2
    before = p if half % SUBLANES == 0 else pltpu.roll(p, half, 0)
    shift = half if half % SUBLANES == 0 else 0
    win = before[POOL_PAD - shift:POOL_PAD - shift + seq, :] + p[POOL_PAD:POOL_PAD + seq, :]
    t = lax.broadcasted_iota(I32, (seq, 1), 0)
    lo = jnp.maximum(t - half, 0)
    hi = jnp.minimum(t + half - 1, seq - 1)
    cnt = (hi - lo + 1).astype(F32)
    pooled = (win / cnt - u).astype(BF16)
    o_ref[0] = (_dot(pooled, gw_ref[0]) * sc_ref[0]).astype(BF16)


def _pool_kernel(u_ref, gw_ref, sc_ref, o_ref, *, seq):
    g = pl.program_id(1)
    for k, window in enumerate(POOL_WINDOWS):
        @pl.when(g == k)
        def _():
            _pool_group(u_ref, gw_ref, sc_ref, o_ref, window=window, seq=seq)


def _pool_mixer(u3d, gw_bf16, scale3d):
    b, seq, tok_width = u3d.shape
    n_groups, gwid, _ = gw_bf16.shape
    assert n_groups == len(POOL_WINDOWS) and n_groups * gwid == tok_width
    return pl.pallas_call(
        functools.partial(_pool_kernel, seq=seq),
        out_shape=jax.ShapeDtypeStruct((b, seq, tok_width), BF16),
        grid=(b, n_groups),
        in_specs=[
            pl.BlockSpec((1, seq, gwid), lambda i, g: (i, 0, g)),
            pl.BlockSpec((1, gwid, gwid), lambda i, g: (g, 0, 0)),
            pl.BlockSpec((1, 1, gwid), lambda i, g: (g, 0, 0)),
        ],
        out_specs=pl.BlockSpec((1, seq, gwid), lambda i, g: (i, 0, g)),
        compiler_params=_params(2),
        name="pool_mixer",
    )(u3d, gw_bf16, scale3d)


def _xattn_kernel(q_ref, kv_ref, o_ref):
    scale = XA_HEAD_DIM ** -0.5
    for h in range(XA_HEADS):
        lo = h * XA_HEAD_DIM
        q = q_ref[:, lo:lo + XA_HEAD_DIM]
        k = kv_ref[0, 0, :, lo:lo + XA_HEAD_DIM]
        v = kv_ref[0, 0, :, XA_WIDTH + lo:XA_WIDTH + lo + XA_HEAD_DIM]
        s = _dot_nt(q, k) * scale
        m = jnp.max(s, axis=-1, keepdims=True)
        p = jnp.exp(s - m)
        den = jnp.sum(p, axis=-1, keepdims=True)
        o_ref[:, lo:lo + XA_HEAD_DIM] = (_dot(p.astype(BF16), v) / den).astype(BF16)


XATTN_ROWS = 2048


def _mem_xattn(qm2d, memkv, layer, seq):
    t = qm2d.shape[0]
    tiles_per_seq = seq // XATTN_ROWS
    return pl.pallas_call(
        _xattn_kernel,
        out_shape=jax.ShapeDtypeStruct((t, XA_WIDTH), BF16),
        grid=(t // XATTN_ROWS,),
        in_specs=[
            pl.BlockSpec((XATTN_ROWS, XA_WIDTH), lambda i: (i, 0)),
            pl.BlockSpec((1, 1, MEM_LEN, 2 * XA_WIDTH), lambda i: (layer, i // tiles_per_seq, 0, 0)),
        ],
        out_specs=pl.BlockSpec((XATTN_ROWS, XA_WIDTH), lambda i: (i, 0)),
        compiler_params=_params(1),
        name="mem_xattn",
    )(qm2d, memkv)


def _outproj_kernel(tok_ref, mo_ref, x_ref, w_ref, g_ref, rw_ref,
                    x1_ref, h_ref, afft_ref, aff_ref, wcat_ref, x1prev_ref,
                    *, tok_width, n_experts):
    i = pl.program_id(0)
    last = pl.num_programs(0) - 1

    @pl.when(i == 0)
    def _():
        rw = rw_ref[0]
        w_hi = rw.astype(BF16)
        wcat_ref[:, 0:LANES] = w_hi
        wcat_ref[:, LANES:2 * LANES] = (rw - w_hi.astype(F32)).astype(BF16)

    def route_previous():
        hn = _rmsnorm_rows(x1prev_ref[...], g_ref[...])
        h_prev = hn.astype(BF16)
        h_lo = (hn - h_prev.astype(F32)).astype(BF16)
        r = _dot(h_prev, wcat_ref[...]) + _dot(h_lo, wcat_ref[...])
        logits = r[:, 0:LANES] + r[:, LANES:2 * LANES]
        lt = logits.T[0:n_experts, :]
        m = jnp.max(lt, axis=0, keepdims=True)
        ex = jnp.exp(lt - m)
        afft = ex / jnp.sum(ex, axis=0, keepdims=True)
        afft_ref[0] = afft
        padded = jnp.concatenate(
            [afft, jnp.zeros((LANES - n_experts, afft.shape[1]), F32)], axis=0)
        aff = padded.T
        aff_ref[...] = aff
        h_ref[...] = _pack_row_words(h_prev, _pack_gate_lanes(aff))

    def project_current():
        y = _dot(tok_ref[...], w_ref[0:tok_width, :]) + _dot(mo_ref[...], w_ref[tok_width:, :])
        x1 = x_ref[...] + y
        x1_ref[...] = x1
        x1prev_ref[...] = x1

    @pl.when(i == 0)
    def _():
        project_current()

    @pl.when((i > 0) & (i < last))
    def _():
        route_previous()
        project_current()

    @pl.when(i == last)
    def _():
        route_previous()


def _outproj(tok2d, mo2d, x2d, w_bf16, g, rw_pad, layer, n_experts, seq):
    t, d_model = x2d.shape
    tok_width = tok2d.shape[1]
    tiles_per_seq = seq // ROW_TILE
    n_tiles = t // ROW_TILE

    def cur(i):
        return jnp.minimum(i, n_tiles - 1)

    def prev(i):
        return jnp.maximum(i - 1, 0)

    return pl.pallas_call(
        functools.partial(_outproj_kernel, tok_width=tok_width, n_experts=n_experts),
        out_shape=(jax.ShapeDtypeStruct((t, d_model), F32),
                   jax.ShapeDtypeStruct((t, d_model // 2 + LANES), U32),
                   jax.ShapeDtypeStruct((t // seq, n_experts, seq), F32),
                   jax.ShapeDtypeStruct((t, LANES), F32)),
        grid=(n_tiles + 1,),
        in_specs=[
            pl.BlockSpec((ROW_TILE, tok_width), lambda i: (cur(i), 0)),
            pl.BlockSpec((ROW_TILE, mo2d.shape[1]), lambda i: (cur(i), 0)),
            pl.BlockSpec((ROW_TILE, d_model), lambda i: (cur(i), 0)),
            _resident((d_model, d_model), lambda i: (0, 0)),
            pl.BlockSpec((1, d_model), lambda i: (0, 0)),
            _resident((1, d_model, LANES), lambda i: (layer, 0, 0)),
        ],
        out_specs=(pl.BlockSpec((ROW_TILE, d_model), lambda i: (cur(i), 0)),
                   pl.BlockSpec((ROW_TILE, d_model // 2 + LANES), lambda i: (prev(i), 0)),
                   pl.BlockSpec((1, n_experts, ROW_TILE),
                                lambda i: (prev(i) // tiles_per_seq, 0, prev(i) % tiles_per_seq)),
                   pl.BlockSpec((ROW_TILE, LANES), lambda i: (prev(i), 0))),
        scratch_shapes=[pltpu.VMEM((d_model, 2 * LANES), BF16),
                        pltpu.VMEM((ROW_TILE, d_model), F32)],
        compiler_params=_params(1),
        name="outproj_router",
    )(tok2d, mo2d, x2d, w_bf16, g, rw_pad)


def _strict_triangle(n, lower):
    r = lax.broadcasted_iota(I32, (n, n), 0)
    c = lax.broadcasted_iota(I32, (n, n), 1)
    return jnp.where((c < r) if lower else (r < c), 1.0, 0.0).astype(BF16)


def _prefix_rows(mask_f32):
    s, l = mask_f32.shape
    tri = _strict_triangle(PREFIX_CHUNK, lower=True)
    carry = jnp.zeros((1, l), F32)
    out = []
    for c in range(0, s, PREFIX_CHUNK):
        m = mask_f32[c:c + PREFIX_CHUNK, :]
        out.append(_dot(tri, m.astype(BF16)) + carry)
        carry = carry + jnp.sum(m, axis=0, keepdims=True)
    return jnp.concatenate(out, axis=0)


def _prefix_lanes(mask_f32):
    e, s = mask_f32.shape
    tri = _strict_triangle(PREFIX_CHUNK, lower=False)
    carry = jnp.zeros((e, 1), F32)
    out = []
    for c in range(0, s, PREFIX_CHUNK):
        m = mask_f32[:, c:c + PREFIX_CHUNK]
        out.append(_dot(m.astype(BF16), tri) + carry)
        carry = carry + jnp.sum(m, axis=1, keepdims=True)
    return jnp.concatenate(out, axis=1)


def _select_slots(key, thr, need, prefix_fn):
    return _select_slots_and_counts(key, thr, need, prefix_fn)[0]


def _select_slots_and_counts(key, thr, need, prefix_fn):
    gt = jnp.where(key > thr, 1.0, 0.0)
    eq = jnp.where(key == thr, 1.0, 0.0)
    eq_rank = prefix_fn(eq)
    sel = gt + eq * jnp.where(eq_rank < need, 1.0, 0.0)
    pos = prefix_fn(sel)
    return jnp.where(sel > 0.5, pos, -1.0), pos


GATE_GROUP = 16
GATE_PIECES = 3


def _pack_gate_lanes(aff):
    hi = aff.astype(BF16).astype(F32)
    r1 = aff - hi
    mid = r1.astype(BF16).astype(F32)
    lo = (r1 - mid).astype(BF16).astype(F32)
    packed = hi + pltpu.roll(mid, GATE_GROUP, 1) + pltpu.roll(lo, 2 * GATE_GROUP, 1)
    return packed.astype(BF16)


def _unpack_gate(tail, e):
    lane = lax.broadcasted_iota(I32, (1, LANES), 1)
    mine = ((lane & (GATE_GROUP - 1)) == e) & (lane < GATE_PIECES * GATE_GROUP)
    return jnp.sum(jnp.where(mine, tail.astype(F32), 0.0), axis=1, keepdims=True)


def _pack_row_words(h, gate_tile):
    rows, d_model = h.shape
    half = d_model // 2
    hi = jnp.concatenate([h[:, 0:half], gate_tile], axis=1).astype(F32)
    lo = jnp.concatenate([h[:, half:], jnp.zeros((rows, LANES), BF16)], axis=1).astype(F32)
    return pltpu.bitcast(hi, U32) | (pltpu.bitcast(lo, U32) >> 16)


def _unpack_row_words(words):
    hi = pltpu.bitcast(words & jnp.uint32(0xFFFF0000), F32)
    lo = pltpu.bitcast(words << 16, F32)
    return hi, lo


def _route_kernel(afft_ref, posm_ref, thr_ref, need_ref, starts_ref, *, cap):
    n_experts, seq = afft_ref.shape[1], afft_ref.shape[2]
    key = pltpu.bitcast(afft_ref[0], I32)
    thr = jnp.zeros((n_experts, 1), I32)
    for bit in range(F32_KEY_BITS - 1, -1, -1):
        cand = thr | (1 << bit)
        cnt = jnp.sum(jnp.where(key >= cand, 1.0, 0.0), axis=1, keepdims=True)
        thr = jnp.where(cnt >= cap, cand, thr)
    n_gt = jnp.sum(jnp.where(key > thr, 1.0, 0.0), axis=1, keepdims=True)
    need = cap - n_gt
    thr_ref[0] = thr
    need_ref[0] = need
    posm, before = _select_slots_and_counts(key, thr, need, _prefix_lanes)
    posm_ref[0] = posm.astype(I32)
    starts_ref[0] = jnp.concatenate(
        [before[:, r:r + 1] for r in range(0, seq, COMBINE_ROWS)], axis=1).astype(I32)


def _route(afft, cap):
    b, n_experts, seq = afft.shape
    n_tiles = seq // COMBINE_ROWS
    rows = b * n_experts
    outs = pl.pallas_call(
        functools.partial(_route_kernel, cap=cap),
        out_shape=(jax.ShapeDtypeStruct((1, rows, seq), I32),
                   jax.ShapeDtypeStruct((1, rows, 1), I32),
                   jax.ShapeDtypeStruct((1, rows, 1), F32),
                   jax.ShapeDtypeStruct((1, rows, n_tiles), I32)),
        grid=(1,),
        in_specs=[pl.BlockSpec((1, rows, seq), lambda i: (0, 0, 0))],
        out_specs=(pl.BlockSpec((1, rows, seq), lambda i: (0, 0, 0)),
                   pl.BlockSpec((1, rows, 1), lambda i: (0, 0, 0)),
                   pl.BlockSpec((1, rows, 1), lambda i: (0, 0, 0)),
                   pl.BlockSpec((1, rows, n_tiles), lambda i: (0, 0, 0))),
        compiler_params=_params(1),
        name="expert_route",
    )(afft.reshape(1, rows, seq))
    return tuple(o.reshape(b, n_experts, o.shape[2]) for o in outs)


SC_LANES = 16
SC_GATHER_ROWS = 64
SC_BACKGROUND_SHARE = 2


def _sc_expert_gather(posm2d, h_words, n_experts, e_offset, n_e, b, seq, cap, worker_share=1):
    width = h_words.shape[1]
    info = plsc.get_sparse_core_info()
    n_cores, n_subcores = info.num_cores, info.num_subcores
    n_workers = n_cores * n_subcores // worker_share
    assert info.num_lanes == SC_LANES and (b * n_e) % n_workers == 0
    pairs_per_worker = (b * n_e) // n_workers
    mesh = plsc.VectorSubcoreMesh(core_axis_name="c", subcore_axis_name="s")

    @functools.partial(
        pl.kernel, mesh=mesh,
        out_type=jax.ShapeDtypeStruct((n_e * b * cap, width), U32),
        compiler_params=pltpu.CompilerParams(needs_layout_passes=False),
        scratch_types=[
            pltpu.VMEM((seq,), I32),
            pltpu.VMEM((cap,), I32),
            pltpu.VMEM((SC_GATHER_ROWS, width), U32),
            pltpu.SemaphoreType.DMA,
        ],
        name="sc_expert_gather",
    )
    def gather(posm_hbm, h_hbm, out_hbm, pos_v, idx_v, rows_v, sem):
        wid = lax.axis_index("s") * n_cores + lax.axis_index("c")

        def move_pair(pair):
            bi = pair // n_e
            e = pair - bi * n_e
            pltpu.sync_copy(posm_hbm.at[bi * n_experts + e_offset + e], pos_v)

            @pl.loop(0, seq, step=SC_LANES)
            def _(t0):
                slots = pos_v[pl.ds(t0, SC_LANES)]
                rows = lax.iota(I32, SC_LANES) + (t0 + bi * seq)
                plsc.store_scatter(idx_v, [slots], rows, mask=slots >= 0)

            out_base = (e * b + bi) * cap
            for c in range(cap // SC_GATHER_ROWS):
                chunk = idx_v.at[pl.ds(c * SC_GATHER_ROWS, SC_GATHER_ROWS)]
                pltpu.async_copy(h_hbm.at[chunk], rows_v, sem).wait()
                pltpu.sync_copy(rows_v, out_hbm.at[pl.ds(out_base + c * SC_GATHER_ROWS, SC_GATHER_ROWS)])

        @pl.when(wid < n_workers)
        def _():
            for p in range(pairs_per_worker):
                move_pair(wid * pairs_per_worker + p)

    return gather(posm2d, h_words).reshape(n_e, b, cap, width)


EXPERT_ROWS = 1024
EXPERT_FTILE = 256
EXPERT_WBUFS = 2


def _expert_kernel(xg_ref, wg_hbm, wu_hbm, wd_hbm, o_ref,
                   x_ref, g_ref, hact_ref, wg_full, wu_full, wd_full, wg_buf, wu_buf, wd_buf, sem,
                   *, layer, e_offset, n_ftiles):
    e = pl.program_id(0)
    m = pl.program_id(1)
    n_e = pl.num_programs(0)
    nb, cap, width = xg_ref.shape[1], xg_ref.shape[2], xg_ref.shape[3]
    half = width - LANES
    d_model = 2 * half
    rows = nb * cap
    tf = EXPERT_FTILE
    assert n_ftiles % EXPERT_WBUFS == 0

    def tile_copies(expert, f):
        ge = e_offset + expert
        slot = f % EXPERT_WBUFS
        return (
            pltpu.make_async_copy(wg_hbm.at[layer, ge, :, pl.ds(f * tf, tf)], wg_buf.at[slot], sem.at[0, slot]),
            pltpu.make_async_copy(wu_hbm.at[layer, ge, :, pl.ds(f * tf, tf)], wu_buf.at[slot], sem.at[1, slot]),
            pltpu.make_async_copy(wd_hbm.at[layer, ge, pl.ds(f * tf, tf), :], wd_buf.at[slot], sem.at[2, slot]),
        )

    def start(expert, f):
        for cp in tile_copies(expert, f):
            cp.start()

    @pl.when((e == 0) & (m == 0))
    def _():
        for f in range(EXPERT_WBUFS):
            start(e, f)

    for i in range(nb):
        r0 = i * cap
        for c in range(0, half, LANES):
            hi, lo = _unpack_row_words(xg_ref[0, i, :, c:c + LANES])
            x_ref[r0:r0 + cap, c:c + LANES] = hi.astype(BF16)
            x_ref[r0:r0 + cap, half + c:half + c + LANES] = lo.astype(BF16)
        gate_tile, _ = _unpack_row_words(xg_ref[0, i, :, half:width])
        g_ref[r0:r0 + cap, :] = jnp.broadcast_to(_unpack_gate(gate_tile, e_offset + e), (cap, LANES))

    def receive(f):
        slot = f % EXPERT_WBUFS
        for cp in tile_copies(e, f):
            cp.wait()
        wg_full[:, f * tf:(f + 1) * tf] = wg_buf[slot].astype(BF16)
        wu_full[:, f * tf:(f + 1) * tf] = wu_buf[slot].astype(BF16)
        wd_full[f * tf:(f + 1) * tf, :] = wd_buf[slot].astype(BF16)
        ahead = f + EXPERT_WBUFS
        if ahead < n_ftiles:
            start(e, ahead)
        else:
            @pl.when(e + 1 < n_e)
            def _():
                start(e + 1, ahead - n_ftiles)

    def body(first_group):
        x = x_ref[...]
        for f in range(n_ftiles):
            if first_group:
                receive(f)
            a = _dot(x, wg_full[:, f * tf:(f + 1) * tf])
            u = _dot(x, wu_full[:, f * tf:(f + 1) * tf])
            hact_ref[:, f * tf:(f + 1) * tf] = (a * jax.nn.sigmoid(a) * u).astype(BF16)
        for c in range(0, d_model, COL_CHUNK):
            y = _dot(hact_ref[...], wd_full[:, c:c + COL_CHUNK])
            for j in range(0, COL_CHUNK, LANES):
                o_ref[:, 0, :, c + j:c + j + LANES] = (
                    (y[:, j:j + LANES] * g_ref[...]).astype(BF16).reshape(nb, cap, LANES))

    @pl.when(m == 0)
    def _():
        body(True)

    @pl.when(m != 0)
    def _():
        body(False)


def _experts(xg, w_gate, w_up, w_down, layer, e_offset):
    n_e, b, cap, width = xg.shape
    d_model = 2 * (width - LANES)
    d_expert = w_gate.shape[3]
    nb = EXPERT_ROWS // cap
    n_ftiles = d_expert // EXPERT_FTILE
    return pl.pallas_call(
        functools.partial(_expert_kernel, layer=layer, e_offset=e_offset, n_ftiles=n_ftiles),
        out_shape=jax.ShapeDtypeStruct((b, n_e, cap, d_model), BF16),
        grid=(n_e, b // nb),
        in_specs=[
            pl.BlockSpec((1, nb, cap, width), lambda e, m: (e, m, 0, 0)),
            pl.BlockSpec(memory_space=pl.ANY),
            pl.BlockSpec(memory_space=pl.ANY),
            pl.BlockSpec(memory_space=pl.ANY),
        ],
        out_specs=pl.BlockSpec((nb, 1, cap, d_model), lambda e, m: (m, e, 0, 0)),
        scratch_shapes=[pltpu.VMEM((EXPERT_ROWS, d_model), BF16),
                        pltpu.VMEM((EXPERT_ROWS, LANES), F32),
                        pltpu.VMEM((EXPERT_ROWS, d_expert), BF16),
                        pltpu.VMEM((d_model, d_expert), BF16),
                        pltpu.VMEM((d_model, d_expert), BF16),
                        pltpu.VMEM((d_expert, d_model), BF16),
                        pltpu.VMEM((EXPERT_WBUFS, d_model, EXPERT_FTILE), F32),
                        pltpu.VMEM((EXPERT_WBUFS, d_model, EXPERT_FTILE), F32),
                        pltpu.VMEM((EXPERT_WBUFS, EXPERT_FTILE, d_model), F32),
                        pltpu.SemaphoreType.DMA((3, EXPERT_WBUFS))],
        compiler_params=_params(2),
        name="experts",
    )(xg, w_gate, w_up, w_down)


COMBINE_ROWS = 256
COMBINE_WINDOW = 64
MXU_DEPTH = 256
BF16_ROWS = 16


def _combine_kernel(starts_ref, aff_ref, thr_ref, need_ref, x1_ref, *rest, cap, n_experts, final_norm):
    n_y = len(rest) - (6 if final_norm else 5)
    y_hbm = rest[:n_y]
    g_ref = rest[n_y] if final_norm else None
    o_ref, post_ref, pfull_ref, ybuf, sem = rest[-5:]
    b = pl.program_id(0)
    t = pl.program_id(1)
    n_b = pl.num_programs(0)
    n_t = pl.num_programs(1)
    rows, win = COMBINE_ROWS, COMBINE_WINDOW
    experts_per_group = n_experts // n_y
    group_rows = ybuf.shape[0] // (2 * n_y)

    def ybuf_row(buf_slot, g):
        return pl.multiple_of((buf_slot * n_y + g) * group_rows, BF16_ROWS)

    n_chunks = sem.shape[1]
    chunks_per_group = n_chunks // n_y
    chunk_rows = group_rows // chunks_per_group
    slot = b % 2

    def chunk_copy(seq_idx, c, dst_slot):
        g, r = c // chunks_per_group, (c % chunks_per_group) * chunk_rows
        return pltpu.make_async_copy(y_hbm[g].at[seq_idx, pl.ds(r, chunk_rows), :],
                                     ybuf.at[pl.ds(ybuf_row(dst_slot, g) + r, chunk_rows), :],
                                     sem.at[dst_slot, c])

    @pl.when((b == 0) & (t == 0))
    def _():
        for c in range(n_chunks):
            chunk_copy(0, c, 0).start()

    @pl.when(t == 0)
    def _():
        for c in range(n_chunks):
            chunk_copy(b, c, slot).wait()
        key = pltpu.bitcast(aff_ref[...], I32)
        post_ref[...] = _select_slots(key, thr_ref[0], need_ref[0], _prefix_rows)

    for c in range(n_chunks):
        @pl.when((t == c) & (b + 1 < n_b))
        def _():
            chunk_copy(b + 1, c, 1 - slot).start()

    posm = post_ref[pl.ds(pl.multiple_of(t * rows, rows), rows), :]
    base = (b * (n_t + 1) + t) * n_experts
    wstart, ok = [], None
    for e in range(n_experts):
        first = starts_ref[base + e]
        end = starts_ref[base + n_experts + e]
        w0 = jnp.minimum((first // BF16_ROWS) * BF16_ROWS, cap - win)
        fits = end - w0 <= win
        wstart.append(w0)
        ok = fits if ok is None else jnp.logical_and(ok, fits)

    def finish(acc):
        if final_norm:
            acc = _rmsnorm_rows(acc, g_ref[...])
        o_ref[0] = acc

    @pl.when(ok)
    def _():
        per_dot, per_tile = MXU_DEPTH // win, LANES // win
        lane = lax.broadcasted_iota(I32, (1, LANES), 1)
        lane_f = lane.astype(F32)
        acc = x1_ref[0]
        for e0 in range(0, n_experts, per_dot):
            onehots, windows = [], []
            for e1 in range(e0, e0 + per_dot, per_tile):
                rel = None
                for k in range(per_tile - 1, -1, -1):
                    e = e1 + k
                    shifted = posm[:, e:e + 1] - (wstart[e] - k * win).astype(F32)
                    rel = shifted if rel is None else jnp.where(lane < (k + 1) * win, shifted, rel)
                onehots.append(jnp.where(rel == lane_f, 1.0, 0.0).astype(BF16))
            for e in range(e0, e0 + per_dot):
                g, el = divmod(e, experts_per_group)
                r0 = pl.multiple_of(ybuf_row(slot, g) + el * cap + wstart[e], BF16_ROWS)
                windows.append(ybuf[pl.ds(r0, win), :])
            acc = acc + _dot(jnp.concatenate(onehots, axis=1), jnp.concatenate(windows, axis=0))
        finish(acc)

    @pl.when(jnp.logical_not(ok))
    def _():
        slot_ids = lax.broadcasted_iota(I32, (1, cap), 1).astype(F32)
        for e in range(n_experts):
            pfull_ref[:, e * cap:(e + 1) * cap] = jnp.where(
                posm[:, e:e + 1] == slot_ids, 1.0, 0.0).astype(BF16)
        acc = x1_ref[0]
        for g in range(n_y):
            acc = acc + _dot(pfull_ref[:, g * group_rows:(g + 1) * group_rows],
                             ybuf[pl.ds(ybuf_row(slot, g), group_rows), :])
        finish(acc)


def _combine(starts, aff2d, thr_row, need_row, x1_3d, y_groups, cap, n_experts, final_g=None):
    b, seq, d_model = x1_3d.shape
    assert sum(y.shape[1] for y in y_groups) == n_experts * cap and cap >= COMBINE_WINDOW
    final_norm = final_g is not None
    n_tiles = seq // COMBINE_ROWS
    group_rows = y_groups[0].shape[1]
    assert all(y.shape[1] == group_rows for y in y_groups)
    assert n_tiles % len(y_groups) == 0 and group_rows % (n_tiles // len(y_groups)) == 0
    y_specs = [pl.BlockSpec(memory_space=pl.ANY) for _ in y_groups]
    g_specs = [pl.BlockSpec((1, d_model), lambda i, t, s: (0, 0))] if final_norm else []
    g_args = [final_g] if final_norm else []
    return pl.pallas_call(
        functools.partial(_combine_kernel, cap=cap, n_experts=n_experts, final_norm=final_norm),
        out_shape=jax.ShapeDtypeStruct((b, seq, d_model), F32),
        grid_spec=pltpu.PrefetchScalarGridSpec(
            num_scalar_prefetch=1,
            grid=(b, n_tiles),
            in_specs=[
                pl.BlockSpec((seq, LANES), lambda i, t, s: (i, 0)),
                pl.BlockSpec((1, 1, LANES), lambda i, t, s: (i, 0, 0)),
                pl.BlockSpec((1, 1, LANES), lambda i, t, s: (i, 0, 0)),
                pl.BlockSpec((1, COMBINE_ROWS, d_model), lambda i, t, s: (i, t, 0)),
            ] + y_specs + g_specs,
            out_specs=pl.BlockSpec((1, COMBINE_ROWS, d_model), lambda i, t, s: (i, t, 0)),
            scratch_shapes=[pltpu.VMEM((seq, LANES), F32),
                            pltpu.VMEM((COMBINE_ROWS, n_experts * cap), BF16),
                            pltpu.VMEM((2 * len(y_groups) * group_rows, d_model), BF16),
                            pltpu.SemaphoreType.DMA((2, n_tiles))],
        ),
        compiler_params=_params(2),
        name="combine",
    )(starts, aff2d, thr_row, need_row, x1_3d, *y_groups, *g_args)


EXPERT_GROUPS = 2


def _moe(afft, aff2d, h_words, x1_2d, w_gate, w_up, w_down, layer, b, seq, final_g=None):
    n_experts = afft.shape[1]
    assert n_experts <= GATE_GROUP and n_experts % EXPERT_GROUPS == 0
    d_model = x1_2d.shape[1]
    cap = CAPACITY_FACTOR * seq // n_experts
    n_e = n_experts // EXPERT_GROUPS
    posm, thr, need, tile_starts = _route(afft, cap)
    starts = jnp.concatenate([jnp.swapaxes(tile_starts, 1, 2),
                              jnp.full((b, 1, n_experts), cap, I32)], axis=1).reshape(-1)
    posm2d = posm.reshape(b * n_experts, seq)
    xgs = [_sc_expert_gather(posm2d, h_words, n_experts, g * n_e, n_e, b, seq, cap,
                             worker_share=1 if g == EXPERT_GROUPS - 1 else SC_BACKGROUND_SHARE)
           for g in range(EXPERT_GROUPS)]
    ys = [_experts(xg, w_gate, w_up, w_down, layer, g * n_e).reshape(b, n_e * cap, d_model)
          for g, xg in enumerate(xgs)]
    pad = LANES - n_experts
    thr_row = jnp.pad(thr.reshape(b, 1, n_experts), ((0, 0), (0, 0), (0, pad)),
                      constant_values=np.iinfo(np.int32).max)
    need_row = jnp.pad(need.reshape(b, 1, n_experts), ((0, 0), (0, 0), (0, pad)))
    return _combine(starts, aff2d, thr_row, need_row, x1_2d.reshape(b, seq, d_model), ys, cap,
                    n_experts, final_g)


def _rotary_tile(t, cos, sin_lo, sin_hi):
    half = ROT_DIM // 2
    return t * cos + pltpu.roll(t, LANES - half, 1) * sin_lo + pltpu.roll(t, half, 1) * sin_hi


def _inproj_attn_kernel(x_ref, g_ref, w_ref, pos_ref, rot_ref,
                        q_ref, k_ref, v_ref, qm_ref, *, tok_width, kv_width):
    qscale = HEAD_DIM ** -0.5
    rows = x_ref.shape[0] // INPROJ_PARTS

    def prepare(part):
        r = slice(part * rows, (part + 1) * rows)
        hn = _rmsnorm_rows(x_ref[r, :], g_ref[...]).astype(BF16)
        ang = pos_ref[r, :].astype(F32) * rot_ref[0:1, :]
        cos = jnp.cos(ang)
        sin = jnp.sin(ang)
        return hn, cos, sin * rot_ref[1:2, :], sin * rot_ref[2:3, :]

    def project(part, hn, cos, sin_lo, sin_hi):
        r = slice(part * rows, (part + 1) * rows)
        for c in range(0, tok_width, COL_CHUNK):
            pc = _dot(hn, w_ref[:, c:c + COL_CHUNK])
            for j in range(0, COL_CHUNK, LANES):
                rot = _rotary_tile(pc[:, j:j + LANES], cos, sin_lo, sin_hi)
                q_ref[r, c + j:c + j + LANES] = (rot * qscale).astype(BF16)
        kv = _dot(hn, w_ref[:, tok_width:tok_width + 2 * kv_width])
        k01 = _rotary_tile(kv[:, 0:LANES], cos, sin_lo, sin_hi)
        k2x = _rotary_tile(kv[:, LANES:2 * LANES], cos, sin_lo, sin_hi)
        k_ref[0, 0, r, :] = k01[:, 0:HEAD_DIM].astype(BF16)
        k_ref[0, 1, r, :] = k01[:, HEAD_DIM:LANES].astype(BF16)
        k_ref[0, 2, r, :] = k2x[:, 0:HEAD_DIM].astype(BF16)
        for hh in range(kv_width // HEAD_DIM):
            lo = kv_width + hh * HEAD_DIM
            v_ref[0, hh, r, :] = kv[:, lo:lo + HEAD_DIM].astype(BF16)
        qm_ref[r, :] = _dot(hn, w_ref[:, tok_width + 2 * kv_width:]).astype(BF16)

    prepared = [prepare(part) for part in range(INPROJ_PARTS)]
    for part in range(INPROJ_PARTS):
        project(part, *prepared[part])


def _inproj_attn(x2d, g, w_bf16, pos2d, rot_rows, tok_width, kv_width, seq):
    t, d_model = x2d.shape
    n = w_bf16.shape[1]
    n_kv = kv_width // HEAD_DIM
    assert n_kv == 3 and kv_width + HEAD_DIM == 2 * LANES
    tiles_per_seq = seq // INPROJ_ROWS
    kv_spec = pl.BlockSpec((1, n_kv, INPROJ_ROWS, HEAD_DIM),
                           lambda i: (i // tiles_per_seq, 0, i % tiles_per_seq, 0))
    return pl.pallas_call(
        functools.partial(_inproj_attn_kernel, tok_width=tok_width, kv_width=kv_width),
        out_shape=(jax.ShapeDtypeStruct((t, tok_width), BF16),
                   jax.ShapeDtypeStruct((t // seq, n_kv, seq, HEAD_DIM), BF16),
                   jax.ShapeDtypeStruct((t // seq, n_kv, seq, HEAD_DIM), BF16),
                   jax.ShapeDtypeStruct((t, n - tok_width - 2 * kv_width), BF16)),
        grid=(t // INPROJ_ROWS,),
        in_specs=[
            pl.BlockSpec((INPROJ_ROWS, d_model), lambda i: (i, 0)),
            pl.BlockSpec((1, d_model), lambda i: (0, 0)),
            _resident((d_model, n), lambda i: (0, 0)),
            pl.BlockSpec((INPROJ_ROWS, 1), lambda i: (i, 0)),
            pl.BlockSpec((8, LANES), lambda i: (0, 0)),
        ],
        out_specs=(pl.BlockSpec((INPROJ_ROWS, tok_width), lambda i: (i, 0)),
                   kv_spec, kv_spec,
                   pl.BlockSpec((INPROJ_ROWS, n - tok_width - 2 * kv_width), lambda i: (i, 0))),
        compiler_params=_params(1),
        name="inproj_attn",
    )(x2d, g, w_bf16, pos2d, rot_rows)


WATTN_QBLOCKS = 4


def _wattn_kernel(sink_ref, q_ref, kp_ref, kc_ref, kn_ref, vp_ref, vc_ref, vn_ref, o_ref,
                  maskadd_ref, kpad_ref, vpad_ref, s_ref, p_ref, inv_ref, *, seq):
    step = pl.program_id(1)
    n_kv = kc_ref.shape[1]
    pairs = GQA_RATIO // 2
    half_rows = pairs * BLOCK
    key_rows = (WATTN_QBLOCKS + 2) * BLOCK
    zeros = jnp.zeros((key_rows, HEAD_DIM), BF16)
    ones_col = jnp.where(lax.broadcasted_iota(I32, (key_rows, HEAD_DIM), 1) == 0, 1.0, 0.0).astype(BF16)
    low_half = lax.broadcasted_iota(I32, (1, LANES), 1) < HEAD_DIM
    for hk in range(n_kv):
        kw = jnp.concatenate([kp_ref[0, hk], kc_ref[0, hk], kn_ref[0, hk]], axis=0)
        vw = jnp.concatenate([vp_ref[0, hk], vc_ref[0, hk], vn_ref[0, hk]], axis=0)
        kpad_ref[2 * hk] = jnp.concatenate([kw, zeros], axis=1)
        kpad_ref[2 * hk + 1] = jnp.concatenate([zeros, kw], axis=1)
        vpad_ref[2 * hk] = jnp.concatenate([vw, ones_col], axis=1)
        vpad_ref[2 * hk + 1] = jnp.concatenate([ones_col, vw], axis=1)

    qi = lax.broadcasted_iota(I32, (BLOCK, 3 * BLOCK), 0)
    kj = lax.broadcasted_iota(I32, (BLOCK, 3 * BLOCK), 1)
    for qb in range(WATTN_QBLOCKS):
        n = step * WATTN_QBLOCKS + qb
        k0 = qb * BLOCK
        first = jnp.maximum(qi, BLOCK - n * BLOCK)
        last = jnp.minimum(qi + 2 * WINDOW, seq + BLOCK - 1 - n * BLOCK)
        maskadd_ref[qb] = jnp.where(((kj - first) | (last - kj)) >= 0, 0.0, NEG_INF)
        q0 = qb * BLOCK
        for hk in range(n_kv):
            tile0 = hk * pairs
            qs = jnp.concatenate(
                [q_ref[0, q0:q0 + BLOCK, (tile0 + j) * LANES:(tile0 + j + 1) * LANES]
                 for j in range(pairs)], axis=0)
            s_ref[qb, hk, 0:half_rows, :] = _dot_nt(qs, kpad_ref[2 * hk, k0:k0 + 3 * BLOCK, :])
            s_ref[qb, hk, half_rows:2 * half_rows, :] = _dot_nt(
                qs, kpad_ref[2 * hk + 1, k0:k0 + 3 * BLOCK, :])
        for hk in range(n_kv):
            for c in range(GQA_RATIO):
                j, odd = c % pairs, c // pairs
                r = c * BLOCK
                s = jnp.concatenate([
                    s_ref[qb, hk, r:r + BLOCK, 0:BLOCK] + maskadd_ref[qb, :, 0:BLOCK],
                    s_ref[qb, hk, r:r + BLOCK, BLOCK:2 * BLOCK],
                    s_ref[qb, hk, r:r + BLOCK, 2 * BLOCK:] + maskadd_ref[qb, :, 2 * BLOCK:],
                ], axis=1)
                sk = sink_ref[hk * GQA_RATIO + 2 * j + odd]
                m = jnp.maximum(jnp.max(s, axis=-1, keepdims=True), sk)
                p_ref[qb, hk, r:r + BLOCK, :] = jnp.exp(s - m).astype(BF16)
                inv_ref[qb, hk, j * BLOCK:(j + 1) * BLOCK, odd * HEAD_DIM:(odd + 1) * HEAD_DIM] = (
                    jnp.broadcast_to(jnp.exp(sk - m), (BLOCK, HEAD_DIM)))
        for hk in range(n_kv):
            pv_even = _dot(p_ref[qb, hk, 0:half_rows, :], vpad_ref[2 * hk, k0:k0 + 3 * BLOCK, :])
            pv_odd = _dot(p_ref[qb, hk, half_rows:2 * half_rows, :],
                          vpad_ref[2 * hk + 1, k0:k0 + 3 * BLOCK, :])
            den = (jnp.where(low_half, pv_even[:, HEAD_DIM:HEAD_DIM + 1], pv_odd[:, 0:1])
                   + inv_ref[qb, hk])
            o = jnp.where(low_half, pv_even, pv_odd) / den
            for j in range(pairs):
                lo = (hk * pairs + j) * LANES
                o_ref[0, q0:q0 + BLOCK, lo:lo + LANES] = o[j * BLOCK:(j + 1) * BLOCK].astype(BF16)


def _window_attention(sink, q3d, k4d, v4d):
    b, seq, tok_width = q3d.shape
    n_kv = k4d.shape[1]
    nb = seq // BLOCK
    qb = WATTN_QBLOCKS
    assert nb % qb == 0
    edge_block = (1, n_kv, BLOCK, HEAD_DIM)
    prev_spec = pl.BlockSpec(edge_block, lambda i, s: (i, 0, jnp.maximum(s * qb - 1, 0), 0))
    cur_spec = pl.BlockSpec((1, n_kv, qb * BLOCK, HEAD_DIM), lambda i, s: (i, 0, s, 0))
    next_spec = pl.BlockSpec(edge_block, lambda i, s: (i, 0, jnp.minimum(s * qb + qb, nb - 1), 0))
    key_rows = (qb + 2) * BLOCK
    return pl.pallas_call(
        functools.partial(_wattn_kernel, seq=seq),
        out_shape=jax.ShapeDtypeStruct((b, seq, tok_width), BF16),
        grid=(b, nb // qb),
        in_specs=[
            pl.BlockSpec(memory_space=pltpu.SMEM),
            pl.BlockSpec((1, qb * BLOCK, tok_width), lambda i, s: (i, s, 0)),
            prev_spec, cur_spec, next_spec, prev_spec, cur_spec, next_spec,
        ],
        out_specs=pl.BlockSpec((1, qb * BLOCK, tok_width), lambda i, s: (i, s, 0)),
        scratch_shapes=[pltpu.VMEM((qb, BLOCK, 3 * BLOCK), F32),
                        pltpu.VMEM((2 * n_kv, key_rows, LANES), BF16),
                        pltpu.VMEM((2 * n_kv, key_rows, LANES), BF16),
                        pltpu.VMEM((qb, n_kv, GQA_RATIO * BLOCK, 3 * BLOCK), F32),
                        pltpu.VMEM((qb, n_kv, GQA_RATIO * BLOCK, 3 * BLOCK), BF16),
                        pltpu.VMEM((qb, n_kv, GQA_RATIO // 2 * BLOCK, LANES), F32)],
        compiler_params=_params(2),
        name="window_attention",
    )(sink, q3d, k4d, k4d, k4d, v4d, v4d, v4d)


def _rotary_rows(dtype=F32):
    half = ROT_DIM // 2
    inv_freq = ROPE_THETA ** (-jnp.arange(0, ROT_DIM, 2, dtype=jnp.float32) / ROT_DIM)
    lane = np.arange(LANES) % HEAD_DIM
    rotated = lane < ROT_DIM
    freq = jnp.where(jnp.asarray(rotated), inv_freq[jnp.asarray(lane % half)], 0.0)
    rows = jnp.zeros((8, LANES), dtype)
    rows = rows.at[0].set(freq)
    rows = rows.at[1].set(jnp.asarray(np.where(lane < half, -1.0, 0.0), dtype))
    rows = rows.at[2].set(jnp.asarray(np.where(rotated & (lane >= half), 1.0, 0.0), dtype))
    return rows


def kernel(x, mem, positions, norm_mix_g, norm_ffn_g, mem_norm_g, final_g, mem_w_kv,
           pool_w_in, pool_group_w, pool_scale, pool_w_out,
           attn_w_in, attn_sink, attn_w_out,
           router_w, exp_w_gate, exp_w_up, exp_w_down):
    b, seq, d_model = x.shape
    depth = norm_mix_g.shape[0]
    t = b * seq
    n_experts = router_w.shape[2]
    tok_width = pool_scale.shape[1]
    n_groups = pool_group_w.shape[1]
    kv_width = (attn_w_in.shape[2] - tok_width - XA_WIDTH) // 2
    assert seq % ROW_TILE == 0 and seq % INPROJ_ROWS == 0 and mem.shape[1] == MEM_LEN

    memkv = _memkv(mem.reshape(b * MEM_LEN, d_model), mem_norm_g.reshape(1, d_model),
                   mem_w_kv.astype(BF16))
    memkv = memkv.reshape(depth, b, MEM_LEN, 2 * XA_WIDTH)
    rw_pad = jnp.pad(router_w, ((0, 0), (0, 0), (0, LANES - n_experts)))
    pos2d = positions.reshape(t, 1)
    rot_rows = _rotary_rows()

    x2d = x.reshape(t, d_model)
    for layer in range(depth):
        j = layer // 2
        g_mix = norm_mix_g[layer].reshape(1, d_model)
        if layer % 2 == 0:
            u, qm = _inproj_pool(x2d, g_mix, pool_w_in[j].astype(BF16), tok_width)
            tok = _pool_mixer(u.reshape(b, seq, tok_width), pool_group_w[j].astype(BF16),
                              pool_scale[j].reshape(n_groups, 1, tok_width // n_groups))
            tok = tok.reshape(t, tok_width)
            w_out = pool_w_out[j]
        else:
            q, k, v, qm = _inproj_attn(x2d, g_mix, attn_w_in[j].astype(BF16), pos2d, rot_rows,
                                       tok_width, kv_width, seq)
            tok = _window_attention(attn_sink[j], q.reshape(b, seq, tok_width), k, v)
            tok = tok.reshape(t, tok_width)
            w_out = attn_w_out[j]
        mo = _mem_xattn(qm, memkv, layer, seq)
        x1, h, afft, aff = _outproj(tok, mo, x2d, w_out.astype(BF16),
                                    norm_ffn_g[layer].reshape(1, d_model), rw_pad, layer,
                                    n_experts, seq)
        last = layer == depth - 1
        x2 = _moe(afft, aff, h, x1, exp_w_gate, exp_w_up, exp_w_down, layer, b, seq,
                  final_g.reshape(1, d_model) if last else None)
        x2d = x2.reshape(t, d_model)
    return x2d.reshape(b, seq, d_model)
```
